```python
import math
import jax, jax.numpy as jnp
from jax import lax
import numpy as np

D_MODEL = 1024
BATCH = 8
SEQ = 2048
DEPTH = 1
DEC_BATCH = 128
DEC_SEQ = 8
PAST_LEN = 16384
PAGE_SIZE = 128

MIX_WIDTH = D_MODEL
A_WIDTH = MIX_WIDTH // 2
A_HEADS = 4
A_HEAD_DIM = A_WIDTH // A_HEADS
CHUNK = 128
B_WIDTH = MIX_WIDTH - A_WIDTH
SSD_HEAD_DIM = 64
SSD_HEADS = B_WIDTH // SSD_HEAD_DIM
SSD_GROUPS = 2
HEADS_PER_GROUP = SSD_HEADS // SSD_GROUPS
SSD_STATE = 128
CONV_K = 4
CONV_DIM = B_WIDTH + 2 * SSD_GROUPS * SSD_STATE
IN_DIM = 2 * A_WIDTH + B_WIDTH + CONV_DIM + SSD_HEADS
N_EXPERTS = 32
TOP_K = 4
D_FF = D_MODEL
SWIGLU_LIMIT = 7.0
SWIGLU_ALPHA = 1.702
EXPERT_BLOCK = 128
EPS = 1e-6

kernel_name = 'hymba_gmlp_ssd_moe_adaln_step'


def rmsnorm(x, g):
    xf = x.astype(jnp.float32)
    r = xf * lax.rsqrt(jnp.mean(xf * xf, axis=-1, keepdims=True) + EPS)
    return (r * g.astype(jnp.float32)).astype(x.dtype)


def seg_decay(cs):
    n = cs.shape[-1]
    diff = cs[..., :, None] - cs[..., None, :]
    mask = jnp.tril(jnp.ones((n, n), dtype=bool))
    return jnp.exp(jnp.where(mask, diff, -jnp.inf))


def spatial_mix(v, w_s, b_s):
    b, L, nh, hd = v.shape
    cl = CHUNK if L % CHUNK == 0 else L
    mask = jnp.tril(jnp.ones((cl, cl), dtype=bool))
    w = jnp.where(mask, w_s[:, :cl, :cl], 0.0).astype(v.dtype)
    vc = v.reshape(b, L // cl, cl, nh, hd)
    bias = b_s[:, :cl].T.astype(v.dtype)[None, None, :, :, None]
    s = jnp.einsum('hts,bcshd->bcthd', w, vc) + bias
    return s.reshape(b, L, nh, hd)


def causal_conv(xbc, prev, w, bias):
    L = xbc.shape[1]
    xp = jnp.concatenate([prev.astype(xbc.dtype), xbc], axis=1)
    acc = bias
    for k in range(CONV_K):
        acc = acc + xp[:, k:k + L] * w[k]
    return jax.nn.silu(acc), xp[:, L:]


def ssd(x, dt, a, bm, cm, init_state):
    b, L, h, p = x.shape
    cl = CHUNK if L % CHUNK == 0 else L
    nc = L // cl
    f32 = jnp.float32
    xdt = (x.astype(f32) * dt[..., None]).reshape(b, nc, cl, h, p)
    bh = jnp.repeat(bm.astype(f32), HEADS_PER_GROUP, axis=2).reshape(b, nc, cl, h, -1)
    ch = jnp.repeat(cm.astype(f32), HEADS_PER_GROUP, axis=2).reshape(b, nc, cl, h, -1)
    a_cs = jnp.cumsum((dt * a).reshape(b, nc, cl, h).transpose(0, 3, 1, 2), axis=-1)
    scores = jnp.einsum('bclhn,bcshn->bhcls', ch, bh) * seg_decay(a_cs)
    y_diag = jnp.einsum('bhcls,bcshp->bclhp', scores, xdt)
    decay_to_end = jnp.exp(a_cs[..., -1:] - a_cs)
    chunk_states = jnp.einsum('bclhn,bhcl,bclhp->bchpn', bh, decay_to_end, xdt)
    states = jnp.concatenate([init_state.astype(f32)[:, None], chunk_states], axis=1)
    chunk_cs = jnp.cumsum(jnp.pad(a_cs[..., -1], ((0, 0), (0, 0), (1, 0))), axis=-1)
    states = jnp.einsum('bhzc,bchpn->bzhpn', seg_decay(chunk_cs), states)
    y_off = jnp.einsum('bclhn,bchpn,bhcl->bclhp', ch, states[:, :-1], jnp.exp(a_cs))
    y = (y_diag + y_off).reshape(b, L, h, p)
    return y, states[:, -1]


def parallel_mixers(h, ssm0, conv0, lp):
    b, L, _ = h.shape
    proj = h @ lp['w_in']
    cuts = np.cumsum([A_WIDTH, A_WIDTH, B_WIDTH, CONV_DIM]).tolist()
    u_a, v_a, z, xbc, dt_raw = jnp.split(proj, cuts, axis=-1)
    u_a = jax.nn.gelu(u_a, approximate=False)
    v_a = rmsnorm(jax.nn.gelu(v_a, approximate=False).reshape(b, L, A_HEADS, A_HEAD_DIM),
                  lp['g_v_a'].reshape(A_HEADS, A_HEAD_DIM))
    s_a = spatial_mix(v_a, lp['w_spatial'], lp['b_spatial'])
    out_a = rmsnorm(u_a * s_a.reshape(b, L, A_WIDTH), lp['g_out_a'])
    xbc, conv_new = causal_conv(xbc, conv0, lp['conv_w'], lp['conv_b'])
    xs, bm, cm = jnp.split(xbc, [B_WIDTH, B_WIDTH + SSD_GROUPS * SSD_STATE], axis=-1)
    dt = jax.nn.softplus(dt_raw.astype(jnp.float32) + lp['dt_bias'].astype(jnp.float32))
    a = -jnp.exp(lp['a_log'].astype(jnp.float32))
    xs = xs.reshape(b, L, SSD_HEADS, SSD_HEAD_DIM)
    y, ssm_new = ssd(xs, dt, a, bm.reshape(b, L, SSD_GROUPS, SSD_STATE),
                     cm.reshape(b, L, SSD_GROUPS, SSD_STATE), ssm0)
    y = y + xs.astype(jnp.float32) * lp['d_skip'].astype(jnp.float32)[:, None]
    gated = (y.reshape(b, L, B_WIDTH) * jax.nn.silu(z.astype(jnp.float32))).astype(h.dtype)
    out_b = rmsnorm(gated.reshape(b, L, SSD_GROUPS, B_WIDTH // SSD_GROUPS),
                    lp['g_out_b'].reshape(SSD_GROUPS, B_WIDTH // SSD_GROUPS)).reshape(b, L, B_WIDTH)
    out = jnp.concatenate([out_a, out_b], axis=-1) @ lp['w_out']
    return out, ssm_new.astype(ssm0.dtype), conv_new, v_a.reshape(b, L, A_WIDTH)


def moe(h, lp):
    shape = h.shape
    x = h.reshape(-1, D_MODEL)
    T = x.shape[0]
    logits = (x @ lp['w_router'] + lp['b_router']).astype(jnp.float32)
    top_v, top_i = lax.top_k(logits, TOP_K)
    top_w = jax.nn.softmax(top_v, axis=-1).astype(h.dtype)
    n_assign = T * TOP_K
    flat_e = top_i.reshape(n_assign)
    flat_tok = jnp.arange(n_assign, dtype=jnp.int32) // TOP_K
    flat_w = top_w.reshape(n_assign)
    order = jnp.argsort(flat_e)
    sorted_e = flat_e[order]
    counts = jnp.zeros((N_EXPERTS,), jnp.int32).at[flat_e].add(1)
    starts = jnp.cumsum(counts) - counts
    padded = (counts + EXPERT_BLOCK - 1) // EXPERT_BLOCK * EXPERT_BLOCK
    pad_ends = jnp.cumsum(padded)
    pad_starts = pad_ends - padded
    dest = pad_starts[sorted_e] + jnp.arange(n_assign, dtype=jnp.int32) - starts[sorted_e]
    n_blocks = -(-(n_assign + N_EXPERTS * (EXPERT_BLOCK - 1)) // EXPERT_BLOCK)
    n_rows = n_blocks * EXPERT_BLOCK
    row_tok = jnp.full((n_rows,), T, jnp.int32).at[dest].set(flat_tok[order])
    row_w = jnp.zeros((n_rows,), h.dtype).at[dest].set(flat_w[order])
    block_start = jnp.arange(n_blocks, dtype=jnp.int32) * EXPERT_BLOCK
    block_e = jnp.minimum(jnp.searchsorted(pad_ends, block_start, side='right'), N_EXPERTS - 1)
    x_rows = jnp.concatenate([x, jnp.zeros((1, D_MODEL), x.dtype)], axis=0)[row_tok]
    x_rows = x_rows.reshape(n_blocks, EXPERT_BLOCK, D_MODEL)

    def expert_block(args):
        xb, e = args
        g = jnp.minimum(xb @ lp['w_gate'][e] + lp['b_gate'][e], SWIGLU_LIMIT)
        u = jnp.clip(xb @ lp['w_up'][e] + lp['b_up'][e], -SWIGLU_LIMIT, SWIGLU_LIMIT)
        act = g * jax.nn.sigmoid(SWIGLU_ALPHA * g) * (u + 1.0)
        return act @ lp['w_down'][e] + lp['b_down'][e]

    y_rows = lax.map(expert_block, (x_rows, block_e)).reshape(n_rows, D_MODEL) * row_w[:, None]
    out = jax.ops.segment_sum(y_rows, row_tok, num_segments=T + 1)[:T]
    return out.reshape(shape)


def layer(x, c, ssm0, conv0, lp):
    ada = (jax.nn.silu(c) @ lp['w_ada'] + lp['b_ada'])[:, None, :]
    sh_m, sc_m, gt_m, sh_f, sc_f, gt_f = jnp.split(ada, 6, axis=-1)
    h = rmsnorm(x, lp['g_mix']) * (1.0 + sc_m) + sh_m
    mix, ssm_new, conv_new, v_rows = parallel_mixers(h, ssm0, conv0, lp)
    x = x + gt_m * mix
    h = rmsnorm(x, lp['g_ffn']) * (1.0 + sc_f) + sh_f
    x = x + gt_f * moe(h, lp)
    return x, ssm_new, conv_new, v_rows


def setup_inputs(seed: int = 0) -> dict:
    key = jax.random.key(seed)
    ks = jax.random.split(key, 32)
    f32 = jnp.float32

    def nrm(k, shape, scale):
        return jax.random.normal(k, shape, f32) * scale

    L = DEPTH
    dt0 = jnp.exp(jax.random.uniform(ks[14], (L, SSD_HEADS), f32, minval=math.log(1e-3), maxval=math.log(1e-1)))
    return {
        'x_prompt': nrm(ks[0], (BATCH, SEQ, D_MODEL), 1.0),
        'x_sample': nrm(ks[1], (DEC_BATCH, DEC_SEQ, D_MODEL), 1.0),
        'c_prompt': nrm(ks[2], (BATCH, D_MODEL), 1.0),
        'c_sample': nrm(ks[3], (DEC_BATCH, D_MODEL), 1.0),
        'state_ssm': nrm(ks[4], (L, DEC_BATCH, SSD_HEADS, SSD_HEAD_DIM, SSD_STATE), 0.5),
        'state_conv': nrm(ks[5], (L, DEC_BATCH, CONV_K - 1, CONV_DIM), 1.0),
        'w_ada': nrm(ks[6], (L, D_MODEL, 6 * D_MODEL), 0.5 * D_MODEL ** -0.5),
        'b_ada': nrm(ks[7], (L, 6 * D_MODEL), 0.02),
        'g_mix': 1.0 + nrm(ks[8], (L, D_MODEL), 0.1),
        'w_in': nrm(ks[9], (L, D_MODEL, IN_DIM), D_MODEL ** -0.5),
        'g_v_a': 1.0 + nrm(ks[10], (L, A_WIDTH), 0.1),
        'w_spatial': nrm(ks[11], (L, A_HEADS, CHUNK, CHUNK), CHUNK ** -0.5),
        'b_spatial': 1.0 + nrm(ks[12], (L, A_HEADS, CHUNK), 0.1),
        'g_out_a': 1.0 + nrm(ks[13], (L, A_WIDTH), 0.1),
        'conv_w': nrm(ks[15], (L, CONV_K, CONV_DIM), CONV_K ** -0.5),
        'conv_b': nrm(ks[16], (L, CONV_DIM), 0.02),
        'dt_bias': dt0 + jnp.log(-jnp.expm1(-dt0)),
        'a_log': jnp.log(jax.random.uniform(ks[17], (L, SSD_HEADS), f32, minval=1.0, maxval=16.0)),
        'd_skip': 1.0 + nrm(ks[18], (L, SSD_HEADS), 0.1),
        'g_out_b': 1.0 + nrm(ks[19], (L, B_WIDTH), 0.1),
        'w_out': nrm(ks[20], (L, MIX_WIDTH, D_MODEL), MIX_WIDTH ** -0.5),
        'g_ffn': 1.0 + nrm(ks[21], (L, D_MODEL), 0.1),
        'w_router': nrm(ks[22], (L, D_MODEL, N_EXPERTS), D_MODEL ** -0.5),
        'b_router': nrm(ks[23], (L, N_EXPERTS), 0.01),
        'w_gate': nrm(ks[24], (L, N_EXPERTS, D_MODEL, D_FF), D_MODEL ** -0.5),
        'b_gate': nrm(ks[25], (L, N_EXPERTS, D_FF), 0.02),
        'w_up': nrm(ks[26], (L, N_EXPERTS, D_MODEL, D_FF), D_MODEL ** -0.5),
        'b_up': nrm(ks[27], (L, N_EXPERTS, D_FF), 0.02),
        'w_down': nrm(ks[28], (L, N_EXPERTS, D_FF, D_MODEL), D_FF ** -0.5),
        'b_down': nrm(ks[29], (L, N_EXPERTS, D_MODEL), 0.02),
        'g_final': 1.0 + nrm(ks[30], (D_MODEL,), 0.1),
    }


def reference(x_prompt, x_sample, c_prompt, c_sample, state_ssm, state_conv, w_ada, b_ada, g_mix, w_in,
              g_v_a, w_spatial, b_spatial, g_out_a, conv_w, conv_b, dt_bias, a_log, d_skip, g_out_b, w_out,
              g_ffn, w_router, b_router, w_gate, b_gate, w_up, b_up, w_down, b_down, g_final):
    yp, ys = x_prompt, x_sample
    bp = x_prompt.shape[0]
    ssm_p, conv_p, ssm_s, conv_s, v_s = [], [], [], [], []
    for l in range(DEPTH):
        lp = {'w_ada': w_ada[l], 'b_ada': b_ada[l], 'g_mix': g_mix[l], 'w_in': w_in[l], 'g_v_a': g_v_a[l],
              'w_spatial': w_spatial[l], 'b_spatial': b_spatial[l], 'g_out_a': g_out_a[l],
              'conv_w': conv_w[l], 'conv_b': conv_b[l], 'dt_bias': dt_bias[l], 'a_log': a_log[l],
              'd_skip': d_skip[l], 'g_out_b': g_out_b[l], 'w_out': w_out[l], 'g_ffn': g_ffn[l],
              'w_router': w_router[l], 'b_router': b_router[l], 'w_gate': w_gate[l], 'b_gate': b_gate[l],
              'w_up': w_up[l], 'b_up': b_up[l], 'w_down': w_down[l], 'b_down': b_down[l]}
        ssm0 = jnp.zeros((bp, SSD_HEADS, SSD_HEAD_DIM, SSD_STATE), x_prompt.dtype)
        conv0 = jnp.zeros((bp, CONV_K - 1, CONV_DIM), x_prompt.dtype)
        yp, sp_new, cp_new, _ = layer(yp, c_prompt, ssm0, conv0, lp)
        ys, ss_new, cs_new, vs_new = layer(ys, c_sample, state_ssm[l], state_conv[l], lp)
        ssm_p.append(sp_new)
        conv_p.append(cp_new)
        ssm_s.append(ss_new)
        conv_s.append(cs_new)
        v_s.append(vs_new)
    y_prompt = rmsnorm(yp, g_final)
    y_sample = rmsnorm(ys, g_final)
    return (y_prompt, y_sample, jnp.stack(ssm_p), jnp.stack(conv_p), jnp.stack(ssm_s), jnp.stack(conv_s), jnp.stack(v_s))
```

```python
import functools
import math

import numpy as np
import jax
import jax.numpy as jnp
from jax import lax
from jax.experimental import pallas as pl
from jax.experimental.pallas import tpu as pltpu

F32 = jnp.float32
BF16 = jnp.bfloat16
I32 = jnp.int32

D_MODEL = 1024
A_WIDTH = 512
A_HEADS = 4
A_HEAD_DIM = 128
CHUNK = 128
B_WIDTH = 512
SSD_HEAD_DIM = 64
SSD_HEADS = 8
SSD_GROUPS = 2
SSD_STATE = 128
GROUP_W = B_WIDTH // SSD_GROUPS
CONV_K = 4
CONV_DIM = 1024
CONV_PAD = 8
N_EXPERTS = 32
TOP_K = 4
SWIGLU_LIMIT = 7.0
SWIGLU_ALPHA = 1.702
EPS = 1e-6
LANES = 128

TOK_TILE = 512
SAMPLE_SEQ_TILE = 16
ROW_TILE = 256
VMEM_LIMIT = 56 * 1024 * 1024


def _dot(a, b):
    return jnp.dot(a, b, preferred_element_type=F32)


def _dot_nt(a, b):
    return lax.dot_general(a, b, (((1,), (1,)), ((), ())), preferred_element_type=F32)


def _split(x):
    hi = x.astype(BF16)
    lo = (x - hi.astype(F32)).astype(BF16)
    return hi, lo


def _dot_exact_l(t, x):
    hi, lo = _split(x)
    return _dot(t, hi) + _dot(t, lo)


def _dot_exact_r(x, t):
    hi, lo = _split(x)
    return _dot(hi, t) + _dot(lo, t)


def _silu(x):
    return x * jax.nn.sigmoid(x)


def _gelu(x):
    return 0.5 * x * (1.0 + lax.erf(x * (1.0 / math.sqrt(2.0))))


def _softplus(x):
    return jnp.maximum(x, 0.0) + jnp.log1p(jnp.exp(-jnp.abs(x)))


def _rms(x, g):
    return x * lax.rsqrt(jnp.mean(x * x, axis=-1, keepdims=True) + EPS) * g


def _ada_kernel(c_ref, w_ref, b_ref, o_ref):
    s_hi, s_lo = _split(_silu(c_ref[...]))
    w_hi, w_lo = _split(w_ref[...])
    o_ref[...] = _dot(s_hi, w_hi) + _dot(s_lo, w_hi) + _dot(s_hi, w_lo) + b_ref[...]


def _ada(c_all, w_ada, b_ada):
    m = c_all.shape[0]
    n = w_ada.shape[1]
    bn = 512
    return pl.pallas_call(
        _ada_kernel,
        out_shape=jax.ShapeDtypeStruct((m, n), F32),
        grid=(n // bn,),
        in_specs=[pl.BlockSpec((m, D_MODEL), lambda j: (0, 0)),
                  pl.BlockSpec((D_MODEL, bn), lambda j: (0, j)),
                  pl.BlockSpec((1, bn), lambda j: (0, j))],
        out_specs=pl.BlockSpec((m, bn), lambda j: (0, j)),
        compiler_params=pltpu.CompilerParams(dimension_semantics=("arbitrary",), vmem_limit_bytes=VMEM_LIMIT),
        name="ada",
    )(c_all, w_ada, b_ada)


def _mixer_front(x3, sh, sc, prev, refs, xp_ref):
    (g_mix, w_uvz, w_xbc, w_dt, w_dtt, wbd, bias_sp, g_v, g_oa, conv_w, conv_b, dt_bias, dt_bias_t, a_row, a_col) = refs
    sb, l, _ = x3.shape
    tm = sb * l
    xn = x3 * lax.rsqrt(jnp.mean(x3 * x3, axis=-1, keepdims=True) + EPS) * g_mix[...]
    h = (xn * (1.0 + sc) + sh).reshape(tm, D_MODEL)
    hb = h.astype(BF16)
    uvz = _dot(hb, w_uvz[...])
    xbc = _dot(hb, w_xbc[...])
    dt_raw = _dot(hb, w_dt[...])
    dtt_raw = _dot_nt(w_dtt[...], hb)

    u = _gelu(uvz[:, :A_WIDTH])
    vg = _gelu(uvz[:, A_WIDTH:2 * A_WIDTH])
    z = uvz[:, 2 * A_WIDTH:]
    v_parts, s_parts = [], []
    for hd in range(A_HEADS):
        sl = slice(hd * A_HEAD_DIM, (hd + 1) * A_HEAD_DIM)
        vh = _rms(vg[:, sl], g_v[:, sl])
        v_parts.append(vh)
        s_parts.append(_dot(wbd[hd], vh.astype(BF16)))
    v = jnp.concatenate(v_parts, axis=1)
    s_a = jnp.concatenate(s_parts, axis=1) + bias_sp[...]
    out_a = _rms(u * s_a, g_oa[...])

    xp_ref[:, 0:CONV_PAD, :] = prev
    xp_ref[:, CONV_PAD:, :] = xbc.reshape(sb, l, CONV_DIM)
    acc = conv_b[...]
    for k in range(CONV_K):
        off = CONV_PAD - (CONV_K - 1) + k
        acc = acc + xp_ref[:, off:off + l, :] * conv_w[k:k + 1, :]
    xc = _silu(acc).reshape(tm, CONV_DIM)
    dt = _softplus(dt_raw + dt_bias[...])
    dtt = _softplus(dtt_raw + dt_bias_t[...])
    d_a = dt * a_row[...]
    d_at = dtt * a_col[...]
    return out_a, v, z, xc, dt, d_a, d_at


def _ssd_chunk(xs, bm, cm, dt, d_a, d_at, cref):
    tril, triu, ones, expand, mask = cref
    cs = _dot_exact_l(tril[...], d_a)
    cs_t = _dot_exact_r(d_at, triu[...])
    cs_tot = _dot_exact_l(ones[...], d_a)
    vals = jnp.concatenate([dt, jnp.exp(cs_tot - cs), jnp.exp(cs)], axis=0)
    vals_e = _dot_exact_r(vals, expand[...])
    n = xs.shape[0]
    dt_e, dte_e, e_e = vals_e[:n], vals_e[n:2 * n], vals_e[2 * n:]
    xdt = xs * dt_e
    xdtd = xdt * dte_e
    msk = mask[...] > 0.5
    row_lt_half = lax.broadcasted_iota(I32, (2 * n, LANES), 0) < n
    lane_lt_half = lax.broadcasted_iota(I32, (2 * n, LANES), 1) < SSD_HEAD_DIM
    y_parts = []
    for g in range(SSD_GROUPS):
        cb = _dot_nt(cm[:, g * SSD_STATE:(g + 1) * SSD_STATE].astype(BF16),
                     bm[:, g * SSD_STATE:(g + 1) * SSD_STATE].astype(BF16))
        for hp in range(SSD_HEADS // SSD_GROUPS // 2):
            h0 = g * (SSD_HEADS // SSD_GROUPS) + 2 * hp
            ms = []
            for hh in (h0, h0 + 1):
                diff = cs[:, hh:hh + 1] - cs_t[hh:hh + 1, :]
                ms.append((cb * jnp.where(msk, jnp.exp(jnp.where(msk, diff, 0.0)), 0.0)).astype(BF16))
            pair = xdt[:, h0 * SSD_HEAD_DIM:(h0 + 2) * SSD_HEAD_DIM]
            rhs = jnp.where(row_lt_half == lane_lt_half, jnp.concatenate([pair, pair], axis=0), 0.0).astype(BF16)
            y_parts.append(_dot(jnp.concatenate(ms, axis=1), rhs))
    y_diag = jnp.concatenate(y_parts, axis=1)
    return y_diag, e_e, xdtd, cs_tot


def _mixer_back(y, xs, z, out_a, dskip_e, g_ob):
    y = y + xs * dskip_e
    gated = y * _silu(z)
    parts = [_rms(gated[:, g * GROUP_W:(g + 1) * GROUP_W], g_ob[:, g * GROUP_W:(g + 1) * GROUP_W])
             for g in range(SSD_GROUPS)]
    return jnp.concatenate([out_a] + parts, axis=1).astype(BF16)


N_FRONT = 15
N_SSD = 5


def _prompt_mixer_kernel(tiles_per_seq, x_ref, sh_ref, sc_ref, *rest):
    front = rest[:N_FRONT]
    cref = rest[N_FRONT:N_FRONT + N_SSD]
    dskip_e, g_ob = rest[N_FRONT + N_SSD:N_FRONT + N_SSD + 2]
    mixed_ref, conv_out_ref, ssm_out_ref = rest[N_FRONT + N_SSD + 2:N_FRONT + N_SSD + 5]
    xp_ref, carry_ref, st_ref = rest[N_FRONT + N_SSD + 5:]
    i = pl.program_id(0)
    first = (i % tiles_per_seq) == 0

    @pl.when(first)
    def _():
        carry_ref[...] = jnp.zeros_like(carry_ref)
        st_ref[...] = jnp.zeros_like(st_ref)

    x3 = x_ref[...]
    l = x3.shape[1]
    out_a, _, z, xc, dt, d_a, d_at = _mixer_front(x3, sh_ref[...], sc_ref[...], carry_ref[...], front, xp_ref)
    carry_ref[...] = xp_ref[:, l:l + CONV_PAD, :]
    xs = xc[:, :B_WIDTH]
    y_rows = []
    for c in range(l // CHUNK):
        r = slice(c * CHUNK, (c + 1) * CHUNK)
        bm = xc[r, B_WIDTH:B_WIDTH + SSD_GROUPS * SSD_STATE]
        cm = xc[r, B_WIDTH + SSD_GROUPS * SSD_STATE:]
        y_diag, e_e, xdtd, _ = _ssd_chunk(xs[r], bm, cm, dt[r], d_a[r], d_at[:, r], cref)
        st = st_ref[...]
        y_off, upd = [], []
        for g in range(SSD_GROUPS):
            gs = slice(g * GROUP_W, (g + 1) * GROUP_W)
            ns = slice(g * SSD_STATE, (g + 1) * SSD_STATE)
            y_off.append(_dot(cm[:, ns].astype(BF16), st[:, gs].astype(BF16)))
            upd.append(_dot(bm[:, ns].T.astype(BF16), xdtd[:, gs].astype(BF16)))
        y_rows.append(y_diag + jnp.concatenate(y_off, axis=1) * e_e)
        st_ref[...] = st * e_e[CHUNK - 1:CHUNK, :] + jnp.concatenate(upd, axis=1)
    y = jnp.concatenate(y_rows, axis=0)
    mixed_ref[...] = _mixer_back(y, xs, z, out_a, dskip_e[...], g_ob[...])

    @pl.when((i % tiles_per_seq) == tiles_per_seq - 1)
    def _():
        conv_out_ref[...] = xp_ref[:, l + CONV_PAD - (CONV_K - 1):l + CONV_PAD, :]
        ssm_out_ref[0] = st_ref[...].T


def _sample_mixer_kernel(x_ref, sh_ref, sc_ref, prev_ref, ssm0_ref, *rest):
    front = rest[:N_FRONT]
    cref = rest[N_FRONT:N_FRONT + N_SSD]
    dskip_e, g_ob, selseq = rest[N_FRONT + N_SSD:N_FRONT + N_SSD + 3]
    mixed_ref, v_ref, conv_out_ref, ssm_out_ref = rest[N_FRONT + N_SSD + 3:N_FRONT + N_SSD + 7]
    xp_ref, yoff_ref, cbf_ref, bbf_ref, t1_ref, dtab_ref = rest[N_FRONT + N_SSD + 7:]
    x3 = x_ref[...]
    sb, l, _ = x3.shape
    tm = sb * l
    out_a, v, z, xc, dt, d_a, d_at = _mixer_front(x3, sh_ref[...], sc_ref[...], prev_ref[...], front, xp_ref)
    v_ref[...] = v
    conv_out_ref[...] = xp_ref[:, l + CONV_PAD - (CONV_K - 1):l + CONV_PAD, :]
    xs = xc[:, :B_WIDTH]
    bm = xc[:, B_WIDTH:B_WIDTH + SSD_GROUPS * SSD_STATE]
    cm = xc[:, B_WIDTH + SSD_GROUPS * SSD_STATE:]
    y_diag, e_e, xdtd, _ = _ssd_chunk(xs, bm, cm, dt, d_a, d_at, cref)

    e_tot = jnp.exp(_dot_exact_l(selseq[...], d_a))
    for hh in range(SSD_HEADS):
        dtab_ref[hh] = jnp.broadcast_to(e_tot[:, hh:hh + 1], (sb, LANES))
    cbf_ref[...] = cm
    bbf_ref[...] = bm
    for g in range(SSD_GROUPS):
        t1_ref[g] = xdtd[:, g * GROUP_W:(g + 1) * GROUP_W].T.astype(BF16)
    seq_of_row = lax.broadcasted_iota(I32, (tm, SSD_STATE), 0) // l
    heads_per_group = SSD_HEADS // SSD_GROUPS

    def body(j, carry):
        r0 = pl.multiple_of(j * l, l)
        s0 = ssm0_ref[j]
        for g in range(SSD_GROUPS):
            ns = slice(g * SSD_STATE, (g + 1) * SSD_STATE)
            s0g = s0[g * heads_per_group:(g + 1) * heads_per_group].reshape(GROUP_W, SSD_STATE)
            cj = cbf_ref[pl.ds(r0, l), ns].astype(BF16)
            yoff_ref[pl.ds(r0, l), g * GROUP_W:(g + 1) * GROUP_W] = _dot_nt(cj, s0g.astype(BF16))
            bmask = jnp.where(seq_of_row == j, bbf_ref[:, ns], 0.0).astype(BF16)
            upd = _dot(t1_ref[g], bmask)
            for hq in range(heads_per_group):
                hh = g * heads_per_group + hq
                dec = dtab_ref[hh, pl.ds(j, 1), :]
                ssm_out_ref[j, hh] = s0[hh] * dec + upd[hq * SSD_HEAD_DIM:(hq + 1) * SSD_HEAD_DIM]
        return carry

    lax.fori_loop(0, sb, body, 0)
    y = y_diag + yoff_ref[...] * e_e
    mixed_ref[...] = _mixer_back(y, xs, z, out_a, dskip_e[...], g_ob[...])


def _const_spec(a):
    nd = a.ndim
    return pl.BlockSpec(a.shape, lambda i, _nd=nd: (0,) * _nd)


def _spatial_consts(w_spatial, b_spatial, cl, tm):
    w = jnp.where(jnp.tril(jnp.ones((cl, cl), bool)), w_spatial[:, :cl, :cl], 0.0)
    eye = jnp.eye(tm // cl, dtype=F32)
    wbd = jnp.einsum("ab,hts->hatbs", eye, w).reshape(A_HEADS, tm, tm).astype(BF16)
    bias = jnp.tile(jnp.repeat(b_spatial[:, :cl].T, A_HEAD_DIM, axis=1), (tm // cl, 1))
    return wbd, bias


def _ssd_consts(cl):
    r = np.arange(CHUNK)
    same = (r[:, None] // cl) == (r[None, :] // cl)
    tril = same & (r[:, None] >= r[None, :])
    expand = np.zeros((LANES, B_WIDTH), np.float32)
    for hh in range(SSD_HEADS):
        expand[hh, hh * SSD_HEAD_DIM:(hh + 1) * SSD_HEAD_DIM] = 1.0
    return (jnp.asarray(tril, BF16), jnp.asarray(tril.T, BF16), jnp.asarray(same, BF16),
            jnp.asarray(expand, BF16), jnp.asarray(tril, F32))


def _front_weights(p):
    w_in = p["w_in"]
    c0, c1 = 3 * A_WIDTH, 3 * A_WIDTH + CONV_DIM
    w_dt = w_in[:, c1:]
    pad8 = lambda v: jnp.pad(v, (0, LANES - SSD_HEADS))
    a = -jnp.exp(p["a_log"])
    return dict(
        g_mix=p["g_mix"][None, :],
        w_uvz=w_in[:, :c0].astype(BF16),
        w_xbc=w_in[:, c0:c1].astype(BF16),
        w_dt=jnp.pad(w_dt, ((0, 0), (0, LANES - SSD_HEADS))).astype(BF16),
        w_dtt=w_dt.T.astype(BF16),
        g_v=p["g_v_a"][None, :], g_oa=p["g_out_a"][None, :],
        conv_w=p["conv_w"], conv_b=p["conv_b"][None, :],
        dt_bias=pad8(p["dt_bias"])[None, :], dt_bias_t=p["dt_bias"][:, None],
        a_row=pad8(a)[None, :], a_col=a[:, None],
        dskip_e=jnp.repeat(p["d_skip"], SSD_HEAD_DIM)[None, :],
        g_ob=p["g_out_b"][None, :],
    )


def _front_list(fw, wbd, bias_sp):
    return [fw["g_mix"], fw["w_uvz"], fw["w_xbc"], fw["w_dt"], fw["w_dtt"], wbd, bias_sp, fw["g_v"], fw["g_oa"],
            fw["conv_w"], fw["conv_b"], fw["dt_bias"], fw["dt_bias_t"], fw["a_row"], fw["a_col"]]


def _prompt_mixer(x, sh, sc, fw, p):
    nseq, lseq, _ = x.shape
    tps = lseq // TOK_TILE
    nt = nseq * tps
    x4 = x.reshape(nt, TOK_TILE, D_MODEL)
    wbd, bias_sp = _spatial_consts(p["w_spatial"], p["b_spatial"], CHUNK, TOK_TILE)
    consts = _front_list(fw, wbd, bias_sp) + list(_ssd_consts(CHUNK)) + [fw["dskip_e"], fw["g_ob"]]
    seq_spec = pl.BlockSpec((1, 1, D_MODEL), lambda i: (i // tps, 0, 0))
    mixed, conv_new, ssm_new = pl.pallas_call(
        functools.partial(_prompt_mixer_kernel, tps),
        out_shape=(jax.ShapeDtypeStruct((nt * TOK_TILE, D_MODEL), BF16),
                   jax.ShapeDtypeStruct((nseq, CONV_K - 1, CONV_DIM), F32),
                   jax.ShapeDtypeStruct((nseq, B_WIDTH, SSD_STATE), F32)),
        grid=(nt,),
        in_specs=[pl.BlockSpec((1, TOK_TILE, D_MODEL), lambda i: (i, 0, 0)), seq_spec, seq_spec]
                 + [_const_spec(a) for a in consts],
        out_specs=(pl.BlockSpec((TOK_TILE, D_MODEL), lambda i: (i, 0)),
                   pl.BlockSpec((1, CONV_K - 1, CONV_DIM), lambda i: (i // tps, 0, 0)),
                   pl.BlockSpec((1, B_WIDTH, SSD_STATE), lambda i: (i // tps, 0, 0))),
        scratch_shapes=[pltpu.VMEM((1, TOK_TILE + CONV_PAD, CONV_DIM), F32),
                        pltpu.VMEM((1, CONV_PAD, CONV_DIM), F32),
                        pltpu.VMEM((SSD_STATE, B_WIDTH), F32)],
        compiler_params=pltpu.CompilerParams(dimension_semantics=("arbitrary",), vmem_limit_bytes=VMEM_LIMIT),
        name="prompt_mixer",
    )(x4, sh, sc, *consts)
    return mixed, conv_new, ssm_new.reshape(nseq, SSD_HEADS, SSD_HEAD_DIM, SSD_STATE)


def _sample_mixer(x, sh, sc, state_ssm, state_conv, fw, p):
    nseq, l, _ = x.shape
    sb = SAMPLE_SEQ_TILE
    tm = sb * l
    assert tm == CHUNK
    wbd, bias_sp = _spatial_consts(p["w_spatial"], p["b_spatial"], l, tm)
    selseq = jnp.asarray((np.arange(tm)[None, :] // l) == np.arange(sb)[:, None], BF16)
    consts = _front_list(fw, wbd, bias_sp) + list(_ssd_consts(l)) + [fw["dskip_e"], fw["g_ob"], selseq]
    prev = jnp.pad(state_conv, ((0, 0), (CONV_PAD - (CONV_K - 1), 0), (0, 0)))
    seq_spec = pl.BlockSpec((sb, 1, D_MODEL), lambda i: (i, 0, 0))
    ssm_spec = pl.BlockSpec((sb, SSD_HEADS, SSD_HEAD_DIM, SSD_STATE), lambda i: (i, 0, 0, 0))
    return pl.pallas_call(
        _sample_mixer_kernel,
        out_shape=(jax.ShapeDtypeStruct((nseq * l, D_MODEL), BF16),
                   jax.ShapeDtypeStruct((nseq * l, A_WIDTH), F32),
                   jax.ShapeDtypeStruct((nseq, CONV_K - 1, CONV_DIM), F32),
                   jax.ShapeDtypeStruct(state_ssm.shape, F32)),
        grid=(nseq // sb,),
        in_specs=[pl.BlockSpec((sb, l, D_MODEL), lambda i: (i, 0, 0)), seq_spec, seq_spec,
                  pl.BlockSpec((sb, CONV_PAD, CONV_DIM), lambda i: (i, 0, 0)), ssm_spec]
                 + [_const_spec(a) for a in consts],
        out_specs=(pl.BlockSpec((tm, D_MODEL), lambda i: (i, 0)),
                   pl.BlockSpec((tm, A_WIDTH), lambda i: (i, 0)),
                   pl.BlockSpec((sb, CONV_K - 1, CONV_DIM), lambda i: (i, 0, 0)),
                   ssm_spec),
        scratch_shapes=[pltpu.VMEM((sb, l + CONV_PAD, CONV_DIM), F32),
                        pltpu.VMEM((tm, B_WIDTH), F32),
                        pltpu.VMEM((tm, SSD_GROUPS * SSD_STATE), F32),
                        pltpu.VMEM((tm, SSD_GROUPS * SSD_STATE), F32),
                        pltpu.VMEM((SSD_GROUPS, GROUP_W, tm), BF16),
                        pltpu.VMEM((SSD_HEADS, sb, LANES), F32)],
        compiler_params=pltpu.CompilerParams(dimension_semantics=("arbitrary",), vmem_limit_bytes=VMEM_LIMIT),
        name="sample_mixer",
    )(x, sh, sc, prev, state_ssm, *consts)


def _post_kernel(mixed_ref, x_ref, gt_ref, sc_ref, sh_ref, w_out_ref, g_ffn_ref, wr_hi_ref, wr_lo_ref, br_ref,
                 x1_ref, h2p_ref, ids_ref, wts_ref):
    x3 = x_ref[...]
    sb, l, _ = x3.shape
    tm = sb * l
    mix = _dot(mixed_ref[...], w_out_ref[...]).reshape(sb, l, D_MODEL)
    x1 = x3 + gt_ref[...] * mix
    x1_ref[...] = x1
    xn = x1 * lax.rsqrt(jnp.mean(x1 * x1, axis=-1, keepdims=True) + EPS) * g_ffn_ref[...]
    h2 = (xn * (1.0 + sc_ref[...]) + sh_ref[...]).reshape(tm, D_MODEL)
    h2p_ref[...] = h2
    h_hi, h_lo = _split(h2)
    logits = (_dot_nt(wr_hi_ref[...], h_hi) + _dot_nt(wr_hi_ref[...], h_lo) + _dot_nt(wr_lo_ref[...], h_hi)
              + br_ref[...])
    e_iota = lax.broadcasted_iota(I32, logits.shape, 0)
    vals, idxs = [], []
    for _ in range(TOP_K):
        m = jnp.max(logits, axis=0, keepdims=True)
        idx = jnp.min(jnp.where(logits == m, e_iota, N_EXPERTS), axis=0, keepdims=True)
        vals.append(m)
        idxs.append(idx)
        logits = jnp.where(e_iota == idx, -jnp.inf, logits)
    ex = [jnp.exp(v - vals[0]) for v in vals]
    tot = ex[0] + ex[1] + ex[2] + ex[3]
    ids_ref[...] = jnp.concatenate(idxs, axis=0)
    wts_ref[...] = jnp.concatenate([e / tot for e in ex], axis=0)


def _post(mixed, x, gt, sc, sh, w_out_b, g_ffn, wr_hi, wr_lo, br, sb, l, seq_div):
    n3, _, _ = x.shape
    nblk = n3 // sb
    tm = sb * l
    t = n3 * l
    ada_spec = pl.BlockSpec((sb, 1, D_MODEL), lambda i: (i // seq_div, 0, 0))
    consts = [w_out_b, g_ffn, wr_hi, wr_lo, br]
    return pl.pallas_call(
        _post_kernel,
        out_shape=(jax.ShapeDtypeStruct(x.shape, F32),
                   jax.ShapeDtypeStruct((t, D_MODEL), F32),
                   jax.ShapeDtypeStruct((TOP_K, t), I32),
                   jax.ShapeDtypeStruct((TOP_K, t), F32)),
        grid=(nblk,),
        in_specs=[pl.BlockSpec((tm, D_MODEL), lambda i: (i, 0)),
                  pl.BlockSpec((sb, l, D_MODEL), lambda i: (i, 0, 0)), ada_spec, ada_spec, ada_spec]
                 + [_const_spec(a) for a in consts],
        out_specs=(pl.BlockSpec((sb, l, D_MODEL), lambda i: (i, 0, 0)),
                   pl.BlockSpec((tm, D_MODEL), lambda i: (i, 0)),
                   pl.BlockSpec((TOP_K, tm), lambda i: (0, i)),
                   pl.BlockSpec((TOP_K, tm), lambda i: (0, i))),
        compiler_params=pltpu.CompilerParams(dimension_semantics=("arbitrary",), vmem_limit_bytes=VMEM_LIMIT),
        name="post",
    )(mixed, x, gt, sc, sh, *consts)


def _plan_kernel(n_tiles_pad, ids_ref, pos_ref, tile_e_ref, cnt_ref, carry_ref, start_ref):
    ph = pl.program_id(0)
    b = pl.program_id(1)
    ids = ids_ref[...]
    tm = ids.shape[1]
    e_iota = lax.broadcasted_iota(I32, (N_EXPERTS, tm), 0)
    onehot = [ids[k:k + 1, :] == e_iota for k in range(TOP_K)]
    sel = (onehot[0] | onehot[1]) | (onehot[2] | onehot[3])
    m = jnp.where(sel, 1.0, 0.0)
    rowsum = jnp.sum(m, axis=1, keepdims=True)

    @pl.when((ph == 0) & (b == 0))
    def _():
        cnt_ref[...] = jnp.zeros_like(cnt_ref)

    @pl.when(ph == 0)
    def _():
        cnt_ref[...] += rowsum

    @pl.when((ph == 1) & (b == 0))
    def _():
        cnt = cnt_ref[...]
        padded = jnp.ceil(cnt * (1.0 / ROW_TILE)) * ROW_TILE
        r = lax.broadcasted_iota(I32, (N_EXPERTS, N_EXPERTS), 0)
        c = lax.broadcasted_iota(I32, (N_EXPERTS, N_EXPERTS), 1)
        as_row = jnp.sum(jnp.where(r == c, padded, 0.0), axis=0, keepdims=True)
        start = jnp.sum(jnp.where(c < r, as_row, 0.0), axis=1, keepdims=True)
        start_ref[...] = start
        carry_ref[...] = jnp.zeros_like(carry_ref)
        ends = start + padded
        tile_row = (lax.broadcasted_iota(I32, (N_EXPERTS, n_tiles_pad), 1) * ROW_TILE).astype(F32)
        n_le = jnp.sum(jnp.where(ends <= tile_row, 1.0, 0.0), axis=0, keepdims=True)
        total = jnp.sum(padded, axis=0, keepdims=True)
        lane = lax.broadcasted_iota(I32, (1, n_tiles_pad), 1)
        n_used = total * (1.0 / ROW_TILE)
        tile_e = jnp.minimum(n_le, N_EXPERTS - 1.0)
        tile_e_ref[...] = jnp.where(lane == n_tiles_pad - 1, n_used, tile_e).astype(I32)

    @pl.when(ph == 1)
    def _():
        t_r = lax.broadcasted_iota(I32, (tm, tm), 0)
        t_c = lax.broadcasted_iota(I32, (tm, tm), 1)
        upper = jnp.where(t_r < t_c, 1.0, 0.0).astype(BF16)
        before = _dot(m.astype(BF16), upper) + carry_ref[...] + start_ref[...]
        pos_ref[...] = jnp.concatenate(
            [jnp.sum(jnp.where(onehot[k], before, 0.0), axis=0, keepdims=True) for k in range(TOP_K)],
            axis=0).astype(I32)
        carry_ref[...] += rowsum


def _plan(ids, n_tiles_pad):
    t = ids.shape[1]
    tm = TOK_TILE
    nb = t // tm
    return pl.pallas_call(
        functools.partial(_plan_kernel, n_tiles_pad),
        out_shape=(jax.ShapeDtypeStruct((TOP_K, t), I32), jax.ShapeDtypeStruct((1, n_tiles_pad), I32)),
        grid=(2, nb),
        in_specs=[pl.BlockSpec((TOP_K, tm), lambda ph, b: (0, b))],
        out_specs=(pl.BlockSpec((TOP_K, tm), lambda ph, b: (0, b * ph)),
                   pl.BlockSpec((1, n_tiles_pad), lambda ph, b: (0, 0))),
        scratch_shapes=[pltpu.VMEM((N_EXPERTS, 1), F32), pltpu.VMEM((N_EXPERTS, 1), F32),
                        pltpu.VMEM((N_EXPERTS, 1), F32)],
        compiler_params=pltpu.CompilerParams(dimension_semantics=("arbitrary", "arbitrary")),
        name="plan",
    )(ids)


def _row_copy_wait(rows_ref, sem, n):
    pltpu.make_async_copy(rows_ref.at[pl.ds(0, n)], rows_ref.at[pl.ds(0, n)], sem).wait()


def _dispatch_kernel(pos_ref, h_ref, xs_in_ref, xs_ref, sem):
    del xs_in_ref
    tm = h_ref.shape[0]

    def body(t, carry):
        for k in range(TOP_K):
            pltpu.make_async_copy(h_ref.at[t], xs_ref.at[pos_ref[k, t]], sem).start()
        return carry

    lax.fori_loop(0, tm, body, 0)
    for k in range(TOP_K):
        _row_copy_wait(h_ref, sem, tm)


def _dispatch(pos, h2p, xs):
    t = h2p.shape[0]
    tm = TOK_TILE
    return pl.pallas_call(
        _dispatch_kernel,
        out_shape=jax.ShapeDtypeStruct(xs.shape, xs.dtype),
        grid=(t // tm,),
        in_specs=[pl.BlockSpec((TOP_K, tm), lambda i: (0, i), memory_space=pltpu.SMEM),
                  pl.BlockSpec((tm, D_MODEL), lambda i: (i, 0)),
                  pl.BlockSpec(memory_space=pl.ANY)],
        out_specs=pl.BlockSpec(memory_space=pl.ANY),
        scratch_shapes=[pltpu.SemaphoreType.DMA],
        input_output_aliases={2: 0},
        compiler_params=pltpu.CompilerParams(dimension_semantics=("arbitrary",)),
        name="dispatch",
    )(pos, h2p, xs)


def _expert_kernel(te_ref, xs_ref, wg_ref, wu_ref, wd_ref, bg_ref, bu_ref, bd_ref, y_ref, wgb, wub, wdb):
    i = pl.program_id(0)
    n_used = te_ref[te_ref.shape[0] - 1]
    prev_e = te_ref[jnp.maximum(i - 1, 0)]
    new_e = (i == 0) | (te_ref[i] != prev_e)

    @pl.when((i < n_used) & new_e)
    def _():
        wgb[...] = wg_ref[0].astype(BF16)
        wub[...] = wu_ref[0].astype(BF16)
        wdb[...] = wd_ref[0].astype(BF16)

    @pl.when(i < n_used)
    def _():
        x = xs_ref[...].astype(BF16)
        g = jnp.minimum(_dot(x, wgb[...]) + bg_ref[0], SWIGLU_LIMIT)
        u = jnp.clip(_dot(x, wub[...]) + bu_ref[0], -SWIGLU_LIMIT, SWIGLU_LIMIT)
        act = g * jax.nn.sigmoid(SWIGLU_ALPHA * g) * (u + 1.0)
        y_ref[...] = _dot(act.astype(BF16), wdb[...]) + bd_ref[0]

    @pl.when(i >= n_used)
    def _():
        y_ref[...] = jnp.zeros_like(y_ref)


def _experts(tile_e, xs, w_gate, b_gate, w_up, b_up, w_down, b_down):
    n_rows = xs.shape[0]
    n_tiles = n_rows // ROW_TILE
    last = tile_e.shape[0] - 1

    def row_map(i, te):
        return (jnp.minimum(i, te[last] - 1), 0)

    def w_map(i, te):
        return (te[jnp.minimum(i, te[last] - 1)], 0, 0)

    w_spec = pl.BlockSpec((1, D_MODEL, D_MODEL), w_map)
    b_spec = pl.BlockSpec((1, 1, D_MODEL), w_map)
    return pl.pallas_call(
        _expert_kernel,
        out_shape=jax.ShapeDtypeStruct((n_rows, D_MODEL), F32),
        grid_spec=pltpu.PrefetchScalarGridSpec(
            num_scalar_prefetch=1,
            grid=(n_tiles,),
            in_specs=[pl.BlockSpec((ROW_TILE, D_MODEL), row_map), w_spec, w_spec, w_spec, b_spec, b_spec, b_spec],
            out_specs=pl.BlockSpec((ROW_TILE, D_MODEL), lambda i, te: (i, 0)),
            scratch_shapes=[pltpu.VMEM((D_MODEL, D_MODEL), BF16)] * 3,
        ),
        compiler_params=pltpu.CompilerParams(dimension_semantics=("arbitrary",), vmem_limit_bytes=VMEM_LIMIT),
        name="experts",
    )(tile_e, xs, w_gate, w_up, w_down, b_gate[:, None, :], b_up[:, None, :], b_down[:, None, :])


def _combine_kernel(pos_ref, y_ref, wts_ref, x1_ref, gt_ref, gf_ref, out_ref, buf, sem):
    x1 = x1_ref[...]
    sb, l, _ = x1.shape
    tm = sb * l

    def body(t, carry):
        for k in range(TOP_K):
            pltpu.make_async_copy(y_ref.at[pos_ref[k, t]], buf.at[k, t], sem).start()
        return carry

    lax.fori_loop(0, tm, body, 0)
    for k in range(TOP_K):
        _row_copy_wait(buf.at[k], sem, tm)
    wts = wts_ref[...]
    moe = wts[:, 0:1] * buf[0]
    for k in range(1, TOP_K):
        moe = moe + wts[:, k:k + 1] * buf[k]
    x2 = x1 + gt_ref[...] * moe.reshape(sb, l, D_MODEL)
    out_ref[...] = x2 * lax.rsqrt(jnp.mean(x2 * x2, axis=-1, keepdims=True) + EPS) * gf_ref[...]


def _combine(pos, y, wts, x1, gt, g_final, sb, l, seq_div):
    n3 = x1.shape[0]
    nblk = n3 // sb
    tm = sb * l
    return pl.pallas_call(
        _combine_kernel,
        out_shape=jax.ShapeDtypeStruct(x1.shape, F32),
        grid=(nblk,),
        in_specs=[pl.BlockSpec((TOP_K, tm), lambda i: (0, i), memory_space=pltpu.SMEM),
                  pl.BlockSpec(memory_space=pl.ANY),
                  pl.BlockSpec((tm, TOP_K), lambda i: (i, 0)),
                  pl.BlockSpec((sb, l, D_MODEL), lambda i: (i, 0, 0)),
                  pl.BlockSpec((sb, 1, D_MODEL), lambda i: (i // seq_div, 0, 0)),
                  pl.BlockSpec((1, D_MODEL), lambda i: (0, 0))],
        out_specs=pl.BlockSpec((sb, l, D_MODEL), lambda i: (i, 0, 0)),
        scratch_shapes=[pltpu.VMEM((TOP_K, tm, D_MODEL), F32), pltpu.SemaphoreType.DMA],
        compiler_params=pltpu.CompilerParams(dimension_semantics=("arbitrary",), vmem_limit_bytes=VMEM_LIMIT),
        name="combine",
    )(pos, y, wts, x1, gt, g_final)


def kernel(x_prompt, x_sample, c_prompt, c_sample, state_ssm, state_conv, w_ada, b_ada, g_mix, w_in, g_v_a, w_spatial, b_spatial, g_out_a, conv_w, conv_b, dt_bias, a_log, d_skip, g_out_b, w_out, g_ffn, w_router, b_router, w_gate, b_gate, w_up, b_up, w_down, b_down, g_final):
    assert w_ada.shape[0] == 1, "single-layer step"
    p = dict(w_in=w_in[0], g_mix=g_mix[0], g_v_a=g_v_a[0], w_spatial=w_spatial[0], b_spatial=b_spatial[0],
             g_out_a=g_out_a[0], conv_w=conv_w[0], conv_b=conv_b[0], dt_bias=dt_bias[0], a_log=a_log[0],
             d_skip=d_skip[0], g_out_b=g_out_b[0])
    bp, lp, _ = x_prompt.shape
    bs, ls, _ = x_sample.shape
    tp, ts = bp * lp, bs * ls

    ada = _ada(jnp.concatenate([c_prompt, c_sample], axis=0), w_ada[0], b_ada[0][None, :])
    ada = ada.reshape(bp + bs, 6, 1, D_MODEL)
    ada_p = [ada[:bp, j] for j in range(6)]
    ada_s = [ada[bp:, j] for j in range(6)]

    fw = _front_weights(p)
    mixed_p, conv_p, ssm_p = _prompt_mixer(x_prompt, ada_p[0], ada_p[1], fw, p)
    mixed_s, v_s, conv_s, ssm_s = _sample_mixer(x_sample, ada_s[0], ada_s[1], state_ssm[0], state_conv[0], fw, p)

    w_out_b = w_out[0].astype(BF16)
    g_ffn2 = g_ffn[0][None, :]
    wr_t = w_router[0].T
    wr_hi = wr_t.astype(BF16)
    wr_lo = (wr_t - wr_hi.astype(F32)).astype(BF16)
    br = b_router[0][:, None]
    tps = lp // TOK_TILE
    sbs = TOK_TILE // ls
    xp3 = x_prompt.reshape(bp * tps, TOK_TILE, D_MODEL)
    x1_p, h2p_p, ids_p, wts_p = _post(mixed_p, xp3, ada_p[2], ada_p[4], ada_p[3], w_out_b, g_ffn2, wr_hi, wr_lo, br,
                                      1, TOK_TILE, tps)
    x1_s, h2p_s, ids_s, wts_s = _post(mixed_s, x_sample, ada_s[2], ada_s[4], ada_s[3], w_out_b, g_ffn2, wr_hi, wr_lo,
                                      br, sbs, ls, 1)

    n_assign = (tp + ts) * TOP_K
    n_tiles = -(-(n_assign + N_EXPERTS * (ROW_TILE - 1)) // ROW_TILE)
    n_tiles_pad = -(-(n_tiles + 1) // LANES) * LANES
    pos, tile_e = _plan(jnp.concatenate([ids_p, ids_s], axis=1), n_tiles_pad)
    pos_p, pos_s = pos[:, :tp], pos[:, tp:]

    xs = jnp.zeros((n_tiles * ROW_TILE, D_MODEL), F32)
    xs = _dispatch(pos_p, h2p_p, xs)
    xs = _dispatch(pos_s, h2p_s, xs)
    y = _experts(tile_e[0], xs, w_gate[0], b_gate[0], w_up[0], b_up[0], w_down[0], b_down[0])

    gf = g_final[None, :]
    y_p = _combine(pos_p, y, wts_p.T, x1_p, ada_p[5], gf, 1, TOK_TILE, tps).reshape(bp, lp, D_MODEL)
    y_s = _combine(pos_s, y, wts_s.T, x1_s, ada_s[5], gf, sbs, ls, 1)

    return (y_p, y_s, ssm_p[None], conv_p[None], ssm_s[None], conv_s[None], v_s.reshape(1, bs, ls, A_WIDTH))
```

```python
import functools
import math

import numpy as np
import jax
import jax.numpy as jnp
from jax import lax
from jax.experimental import pallas as pl
from jax.experimental.pallas import tpu as pltpu

F32 = jnp.float32
BF16 = jnp.bfloat16
I32 = jnp.int32

D_MODEL = 1024
A_WIDTH = 512
A_HEADS = 4
A_HEAD_DIM = 128
CHUNK = 128
B_WIDTH = 512
SSD_HEAD_DIM = 64
SSD_HEADS = 8
SSD_GROUPS = 2
SSD_STATE = 128
GROUP_W = B_WIDTH // SSD_GROUPS
CONV_K = 4
CONV_DIM = 1024
CONV_PAD = 8
N_EXPERTS = 32
TOP_K = 4
SWIGLU_LIMIT = 7.0
SWIGLU_ALPHA = 1.702
EPS = 1e-6
LANES = 128

TOK_TILE = 512
SAMPLE_SEQ_TILE = 16
ROW_TILE = 256
ROW_GROUP = 16
TILE_GROUPS = ROW_TILE // ROW_GROUP
LOCAL_GROUPS = TOK_TILE * TOP_K // ROW_GROUP + N_EXPERTS
LOCAL_ROWS = LOCAL_GROUPS * ROW_GROUP
GDST_LANES = 256
VMEM_LIMIT = 56 * 1024 * 1024


def _dot(a, b):
    return jnp.dot(a, b, preferred_element_type=F32)


def _dot_nt(a, b):
    return lax.dot_general(a, b, (((1,), (1,)), ((), ())), preferred_element_type=F32)


def _split(x):
    hi = x.astype(BF16)
    lo = (x - hi.astype(F32)).astype(BF16)
    return hi, lo


def _dot_exact_l(t, x):
    hi, lo = _split(x)
    return _dot(t, hi) + _dot(t, lo)


def _dot_exact_r(x, t):
    hi, lo = _split(x)
    return _dot(hi, t) + _dot(lo, t)


def _silu(x):
    return x * jax.nn.sigmoid(x)


def _gelu(x):
    return 0.5 * x * (1.0 + lax.erf(x * (1.0 / math.sqrt(2.0))))


def _softplus(x):
    return jnp.maximum(x, 0.0) + jnp.log1p(jnp.exp(-jnp.abs(x)))


def _rms(x, g):
    return x * lax.rsqrt(jnp.mean(x * x, axis=-1, keepdims=True) + EPS) * g


def _ada_kernel(c_ref, w_ref, b_ref, o_ref):
    s_hi, s_lo = _split(_silu(c_ref[...]))
    w_hi, w_lo = _split(w_ref[...])
    o_ref[...] = _dot(s_hi, w_hi) + _dot(s_lo, w_hi) + _dot(s_hi, w_lo) + b_ref[...]


def _ada(c_all, w_ada, b_ada):
    m = c_all.shape[0]
    n = w_ada.shape[1]
    bn = 512
    return pl.pallas_call(
        _ada_kernel,
        out_shape=jax.ShapeDtypeStruct((m, n), F32),
        grid=(n // bn,),
        in_specs=[pl.BlockSpec((m, D_MODEL), lambda j: (0, 0)),
                  pl.BlockSpec((D_MODEL, bn), lambda j: (0, j)),
                  pl.BlockSpec((1, bn), lambda j: (0, j))],
        out_specs=pl.BlockSpec((m, bn), lambda j: (0, j)),
        compiler_params=pltpu.CompilerParams(dimension_semantics=("arbitrary",), vmem_limit_bytes=VMEM_LIMIT),
        name="ada",
    )(c_all, w_ada, b_ada)


def _mixer_front(x3, sh, sc, prev, refs, xp_ref):
    (g_mix, w_uvz, w_xbc, w_dt, w_dtt, wbd, bias_sp, g_v, g_oa, conv_w, conv_b, dt_bias, dt_bias_t, a_row, a_col) = refs
    sb, l, _ = x3.shape
    tm = sb * l
    xn = x3 * lax.rsqrt(jnp.mean(x3 * x3, axis=-1, keepdims=True) + EPS) * g_mix[...]
    h = (xn * (1.0 + sc) + sh).reshape(tm, D_MODEL)
    hb = h.astype(BF16)
    uvz = _dot(hb, w_uvz[...])
    xbc = _dot(hb, w_xbc[...])
    dt_raw = _dot(hb, w_dt[...])
    dtt_raw = _dot_nt(w_dtt[...], hb)

    u = _gelu(uvz[:, :A_WIDTH])
    vg = _gelu(uvz[:, A_WIDTH:2 * A_WIDTH])
    z = uvz[:, 2 * A_WIDTH:]
    v_parts, s_parts = [], []
    for hd in range(A_HEADS):
        sl = slice(hd * A_HEAD_DIM, (hd + 1) * A_HEAD_DIM)
        vh = _rms(vg[:, sl], g_v[:, sl])
        v_parts.append(vh)
        s_parts.append(_dot(wbd[hd], vh.astype(BF16)))
    v = jnp.concatenate(v_parts, axis=1)
    s_a = jnp.concatenate(s_parts, axis=1) + bias_sp[...]
    out_a = _rms(u * s_a, g_oa[...])

    xp_ref[:, 0:CONV_PAD, :] = prev
    xp_ref[:, CONV_PAD:, :] = xbc.reshape(sb, l, CONV_DIM)
    acc = conv_b[...]
    for k in range(CONV_K):
        off = CONV_PAD - (CONV_K - 1) + k
        acc = acc + xp_ref[:, off:off + l, :] * conv_w[k:k + 1, :]
    xc = _silu(acc).reshape(tm, CONV_DIM)
    dt = _softplus(dt_raw + dt_bias[...])
    dtt = _softplus(dtt_raw + dt_bias_t[...])
    d_a = dt * a_row[...]
    d_at = dtt * a_col[...]
    return out_a, v, z, xc, dt, d_a, d_at


def _ssd_chunk(xs, bm, cm, dt, d_a, d_at, cref):
    tril, triu, ones, expand, mask = cref
    cs = _dot_exact_l(tril[...], d_a)
    cs_t = _dot_exact_r(d_at, triu[...])
    cs_tot = _dot_exact_l(ones[...], d_a)
    vals = jnp.concatenate([dt, jnp.exp(cs_tot - cs), jnp.exp(cs)], axis=0)
    vals_e = _dot_exact_r(vals, expand[...])
    n = xs.shape[0]
    dt_e, dte_e, e_e = vals_e[:n], vals_e[n:2 * n], vals_e[2 * n:]
    xdt = xs * dt_e
    xdtd = xdt * dte_e
    msk = mask[...] > 0.5
    row_lt_half = lax.broadcasted_iota(I32, (2 * n, LANES), 0) < n
    lane_lt_half = lax.broadcasted_iota(I32, (2 * n, LANES), 1) < SSD_HEAD_DIM
    y_parts = []
    for g in range(SSD_GROUPS):
        cb = _dot_nt(cm[:, g * SSD_STATE:(g + 1) * SSD_STATE].astype(BF16),
                     bm[:, g * SSD_STATE:(g + 1) * SSD_STATE].astype(BF16))
        for hp in range(SSD_HEADS // SSD_GROUPS // 2):
            h0 = g * (SSD_HEADS // SSD_GROUPS) + 2 * hp
            ms = []
            for hh in (h0, h0 + 1):
                diff = cs[:, hh:hh + 1] - cs_t[hh:hh + 1, :]
                ms.append((cb * jnp.where(msk, jnp.exp(jnp.where(msk, diff, 0.0)), 0.0)).astype(BF16))
            pair = xdt[:, h0 * SSD_HEAD_DIM:(h0 + 2) * SSD_HEAD_DIM]
            rhs = jnp.where(row_lt_half == lane_lt_half, jnp.concatenate([pair, pair], axis=0), 0.0).astype(BF16)
            y_parts.append(_dot(jnp.concatenate(ms, axis=1), rhs))
    y_diag = jnp.concatenate(y_parts, axis=1)
    return y_diag, e_e, xdtd, cs_tot


def _mixer_back(y, xs, z, out_a, dskip_e, g_ob):
    y = y + xs * dskip_e
    gated = y * _silu(z)
    parts = [_rms(gated[:, g * GROUP_W:(g + 1) * GROUP_W], g_ob[:, g * GROUP_W:(g + 1) * GROUP_W])
             for g in range(SSD_GROUPS)]
    return jnp.concatenate([out_a] + parts, axis=1).astype(BF16)


N_FRONT = 15
N_SSD = 5


def _prompt_mixer_kernel(tiles_per_seq, x_ref, sh_ref, sc_ref, *rest):
    front = rest[:N_FRONT]
    cref = rest[N_FRONT:N_FRONT + N_SSD]
    dskip_e, g_ob = rest[N_FRONT + N_SSD:N_FRONT + N_SSD + 2]
    mixed_ref, conv_out_ref, ssm_out_ref = rest[N_FRONT + N_SSD + 2:N_FRONT + N_SSD + 5]
    xp_ref, carry_ref, st_ref = rest[N_FRONT + N_SSD + 5:]
    i = pl.program_id(0)
    first = (i % tiles_per_seq) == 0

    @pl.when(first)
    def _():
        carry_ref[...] = jnp.zeros_like(carry_ref)
        st_ref[...] = jnp.zeros_like(st_ref)

    x3 = x_ref[...]
    l = x3.shape[1]
    out_a, _, z, xc, dt, d_a, d_at = _mixer_front(x3, sh_ref[...], sc_ref[...], carry_ref[...], front, xp_ref)
    carry_ref[...] = xp_ref[:, l:l + CONV_PAD, :]
    xs = xc[:, :B_WIDTH]
    y_rows = []
    for c in range(l // CHUNK):
        r = slice(c * CHUNK, (c + 1) * CHUNK)
        bm = xc[r, B_WIDTH:B_WIDTH + SSD_GROUPS * SSD_STATE]
        cm = xc[r, B_WIDTH + SSD_GROUPS * SSD_STATE:]
        y_diag, e_e, xdtd, _ = _ssd_chunk(xs[r], bm, cm, dt[r], d_a[r], d_at[:, r], cref)
        st = st_ref[...]
        y_off, upd = [], []
        for g in range(SSD_GROUPS):
            gs = slice(g * GROUP_W, (g + 1) * GROUP_W)
            ns = slice(g * SSD_STATE, (g + 1) * SSD_STATE)
            y_off.append(_dot(cm[:, ns].astype(BF16), st[:, gs].astype(BF16)))
            upd.append(_dot(bm[:, ns].T.astype(BF16), xdtd[:, gs].astype(BF16)))
        y_rows.append(y_diag + jnp.concatenate(y_off, axis=1) * e_e)
        st_ref[...] = st * e_e[CHUNK - 1:CHUNK, :] + jnp.concatenate(upd, axis=1)
    y = jnp.concatenate(y_rows, axis=0)
    mixed_ref[...] = _mixer_back(y, xs, z, out_a, dskip_e[...], g_ob[...])

    @pl.when((i % tiles_per_seq) == tiles_per_seq - 1)
    def _():
        conv_out_ref[...] = xp_ref[:, l + CONV_PAD - (CONV_K - 1):l + CONV_PAD, :]
        ssm_out_ref[0] = st_ref[...].T


def _sample_mixer_kernel(x_ref, sh_ref, sc_ref, prev_ref, ssm0_ref, *rest):
    front = rest[:N_FRONT]
    cref = rest[N_FRONT:N_FRONT + N_SSD]
    dskip_e, g_ob, selseq = rest[N_FRONT + N_SSD:N_FRONT + N_SSD + 3]
    mixed_ref, v_ref, conv_out_ref, ssm_out_ref = rest[N_FRONT + N_SSD + 3:N_FRONT + N_SSD + 7]
    xp_ref, yoff_ref, cbf_ref, bbf_ref, t1_ref, dtab_ref = rest[N_FRONT + N_SSD + 7:]
    x3 = x_ref[...]
    sb, l, _ = x3.shape
    tm = sb * l
    out_a, v, z, xc, dt, d_a, d_at = _mixer_front(x3, sh_ref[...], sc_ref[...], prev_ref[...], front, xp_ref)
    v_ref[...] = v
    conv_out_ref[...] = xp_ref[:, l + CONV_PAD - (CONV_K - 1):l + CONV_PAD, :]
    xs = xc[:, :B_WIDTH]
    bm = xc[:, B_WIDTH:B_WIDTH + SSD_GROUPS * SSD_STATE]
    cm = xc[:, B_WIDTH + SSD_GROUPS * SSD_STATE:]
    y_diag, e_e, xdtd, _ = _ssd_chunk(xs, bm, cm, dt, d_a, d_at, cref)

    e_tot = jnp.exp(_dot_exact_l(selseq[...], d_a))
    for hh in range(SSD_HEADS):
        dtab_ref[hh] = jnp.broadcast_to(e_tot[:, hh:hh + 1], (sb, LANES))
    cbf_ref[...] = cm
    bbf_ref[...] = bm
    for g in range(SSD_GROUPS):
        t1_ref[g] = xdtd[:, g * GROUP_W:(g + 1) * GROUP_W].T.astype(BF16)
    seq_of_row = lax.broadcasted_iota(I32, (tm, SSD_STATE), 0) // l
    heads_per_group = SSD_HEADS // SSD_GROUPS

    def body(j, carry):
        r0 = pl.multiple_of(j * l, l)
        s0 = ssm0_ref[j]
        for g in range(SSD_GROUPS):
            ns = slice(g * SSD_STATE, (g + 1) * SSD_STATE)
            s0g = s0[g * heads_per_group:(g + 1) * heads_per_group].reshape(GROUP_W, SSD_STATE)
            cj = cbf_ref[pl.ds(r0, l), ns].astype(BF16)
            yoff_ref[pl.ds(r0, l), g * GROUP_W:(g + 1) * GROUP_W] = _dot_nt(cj, s0g.astype(BF16))
            bmask = jnp.where(seq_of_row == j, bbf_ref[:, ns], 0.0).astype(BF16)
            upd = _dot(t1_ref[g], bmask)
            for hq in range(heads_per_group):
                hh = g * heads_per_group + hq
                dec = dtab_ref[hh, pl.ds(j, 1), :]
                ssm_out_ref[j, hh] = s0[hh] * dec + upd[hq * SSD_HEAD_DIM:(hq + 1) * SSD_HEAD_DIM]
        return carry

    lax.fori_loop(0, sb, body, 0)
    y = y_diag + yoff_ref[...] * e_e
    mixed_ref[...] = _mixer_back(y, xs, z, out_a, dskip_e[...], g_ob[...])


def _const_spec(a):
    nd = a.ndim
    return pl.BlockSpec(a.shape, lambda i, _nd=nd: (0,) * _nd)


def _spatial_consts(w_spatial, b_spatial, cl, tm):
    w = jnp.where(jnp.tril(jnp.ones((cl, cl), bool)), w_spatial[:, :cl, :cl], 0.0)
    eye = jnp.eye(tm // cl, dtype=F32)
    wbd = jnp.einsum("ab,hts->hatbs", eye, w).reshape(A_HEADS, tm, tm).astype(BF16)
    bias = jnp.tile(jnp.repeat(b_spatial[:, :cl].T, A_HEAD_DIM, axis=1), (tm // cl, 1))
    return wbd, bias


def _ssd_consts(cl):
    r = np.arange(CHUNK)
    same = (r[:, None] // cl) == (r[None, :] // cl)
    tril = same & (r[:, None] >= r[None, :])
    expand = np.zeros((LANES, B_WIDTH), np.float32)
    for hh in range(SSD_HEADS):
        expand[hh, hh * SSD_HEAD_DIM:(hh + 1) * SSD_HEAD_DIM] = 1.0
    return (jnp.asarray(tril, BF16), jnp.asarray(tril.T, BF16), jnp.asarray(same, BF16),
            jnp.asarray(expand, BF16), jnp.asarray(tril, F32))


def _front_weights(p):
    w_in = p["w_in"]
    c0, c1 = 3 * A_WIDTH, 3 * A_WIDTH + CONV_DIM
    w_dt = w_in[:, c1:]
    pad8 = lambda v: jnp.pad(v, (0, LANES - SSD_HEADS))
    a = -jnp.exp(p["a_log"])
    return dict(
        g_mix=p["g_mix"][None, :],
        w_uvz=w_in[:, :c0].astype(BF16),
        w_xbc=w_in[:, c0:c1].astype(BF16),
        w_dt=jnp.pad(w_dt, ((0, 0), (0, LANES - SSD_HEADS))).astype(BF16),
        w_dtt=w_dt.T.astype(BF16),
        g_v=p["g_v_a"][None, :], g_oa=p["g_out_a"][None, :],
        conv_w=p["conv_w"], conv_b=p["conv_b"][None, :],
        dt_bias=pad8(p["dt_bias"])[None, :], dt_bias_t=p["dt_bias"][:, None],
        a_row=pad8(a)[None, :], a_col=a[:, None],
        dskip_e=jnp.repeat(p["d_skip"], SSD_HEAD_DIM)[None, :],
        g_ob=p["g_out_b"][None, :],
    )


def _front_list(fw, wbd, bias_sp):
    return [fw["g_mix"], fw["w_uvz"], fw["w_xbc"], fw["w_dt"], fw["w_dtt"], wbd, bias_sp, fw["g_v"], fw["g_oa"],
            fw["conv_w"], fw["conv_b"], fw["dt_bias"], fw["dt_bias_t"], fw["a_row"], fw["a_col"]]


def _prompt_mixer(x, sh, sc, fw, p):
    nseq, lseq, _ = x.shape
    tps = lseq // TOK_TILE
    nt = nseq * tps
    x4 = x.reshape(nt, TOK_TILE, D_MODEL)
    wbd, bias_sp = _spatial_consts(p["w_spatial"], p["b_spatial"], CHUNK, TOK_TILE)
    consts = _front_list(fw, wbd, bias_sp) + list(_ssd_consts(CHUNK)) + [fw["dskip_e"], fw["g_ob"]]
    seq_spec = pl.BlockSpec((1, 1, D_MODEL), lambda i: (i // tps, 0, 0))
    mixed, conv_new, ssm_new = pl.pallas_call(
        functools.partial(_prompt_mixer_kernel, tps),
        out_shape=(jax.ShapeDtypeStruct((nt * TOK_TILE, D_MODEL), BF16),
                   jax.ShapeDtypeStruct((nseq, CONV_K - 1, CONV_DIM), F32),
                   jax.ShapeDtypeStruct((nseq, B_WIDTH, SSD_STATE), F32)),
        grid=(nt,),
        in_specs=[pl.BlockSpec((1, TOK_TILE, D_MODEL), lambda i: (i, 0, 0)), seq_spec, seq_spec]
                 + [_const_spec(a) for a in consts],
        out_specs=(pl.BlockSpec((TOK_TILE, D_MODEL), lambda i: (i, 0)),
                   pl.BlockSpec((1, CONV_K - 1, CONV_DIM), lambda i: (i // tps, 0, 0)),
                   pl.BlockSpec((1, B_WIDTH, SSD_STATE), lambda i: (i // tps, 0, 0))),
        scratch_shapes=[pltpu.VMEM((1, TOK_TILE + CONV_PAD, CONV_DIM), F32),
                        pltpu.VMEM((1, CONV_PAD, CONV_DIM), F32),
                        pltpu.VMEM((SSD_STATE, B_WIDTH), F32)],
        compiler_params=pltpu.CompilerParams(dimension_semantics=("arbitrary",), vmem_limit_bytes=VMEM_LIMIT),
        name="prompt_mixer",
    )(x4, sh, sc, *consts)
    return mixed, conv_new, ssm_new.reshape(nseq, SSD_HEADS, SSD_HEAD_DIM, SSD_STATE)


def _sample_mixer(x, sh, sc, state_ssm, state_conv, fw, p):
    nseq, l, _ = x.shape
    sb = SAMPLE_SEQ_TILE
    tm = sb * l
    assert tm == CHUNK
    wbd, bias_sp = _spatial_consts(p["w_spatial"], p["b_spatial"], l, tm)
    selseq = jnp.asarray((np.arange(tm)[None, :] // l) == np.arange(sb)[:, None], BF16)
    consts = _front_list(fw, wbd, bias_sp) + list(_ssd_consts(l)) + [fw["dskip_e"], fw["g_ob"], selseq]
    prev = jnp.pad(state_conv, ((0, 0), (CONV_PAD - (CONV_K - 1), 0), (0, 0)))
    seq_spec = pl.BlockSpec((sb, 1, D_MODEL), lambda i: (i, 0, 0))
    ssm_spec = pl.BlockSpec((sb, SSD_HEADS, SSD_HEAD_DIM, SSD_STATE), lambda i: (i, 0, 0, 0))
    return pl.pallas_call(
        _sample_mixer_kernel,
        out_shape=(jax.ShapeDtypeStruct((nseq * l, D_MODEL), BF16),
                   jax.ShapeDtypeStruct((nseq * l, A_WIDTH), F32),
                   jax.ShapeDtypeStruct((nseq, CONV_K - 1, CONV_DIM), F32),
                   jax.ShapeDtypeStruct(state_ssm.shape, F32)),
        grid=(nseq // sb,),
        in_specs=[pl.BlockSpec((sb, l, D_MODEL), lambda i: (i, 0, 0)), seq_spec, seq_spec,
                  pl.BlockSpec((sb, CONV_PAD, CONV_DIM), lambda i: (i, 0, 0)), ssm_spec]
                 + [_const_spec(a) for a in consts],
        out_specs=(pl.BlockSpec((tm, D_MODEL), lambda i: (i, 0)),
                   pl.BlockSpec((tm, A_WIDTH), lambda i: (i, 0)),
                   pl.BlockSpec((sb, CONV_K - 1, CONV_DIM), lambda i: (i, 0, 0)),
                   ssm_spec),
        scratch_shapes=[pltpu.VMEM((sb, l + CONV_PAD, CONV_DIM), F32),
                        pltpu.VMEM((tm, B_WIDTH), F32),
                        pltpu.VMEM((tm, SSD_GROUPS * SSD_STATE), F32),
                        pltpu.VMEM((tm, SSD_GROUPS * SSD_STATE), F32),
                        pltpu.VMEM((SSD_GROUPS, GROUP_W, tm), BF16),
                        pltpu.VMEM((SSD_HEADS, sb, LANES), F32)],
        compiler_params=pltpu.CompilerParams(dimension_semantics=("arbitrary",), vmem_limit_bytes=VMEM_LIMIT),
        name="sample_mixer",
    )(x, sh, sc, prev, state_ssm, *consts)


def _post_kernel(mixed_ref, x_ref, gt_ref, sc_ref, sh_ref, w_out_ref, g_ffn_ref, wr_hi_ref, wr_lo_ref, br_ref,
                 x1_ref, h2p_ref, ids_ref, wts_ref):
    x3 = x_ref[...]
    sb, l, _ = x3.shape
    tm = sb * l
    mix = _dot(mixed_ref[...], w_out_ref[...]).reshape(sb, l, D_MODEL)
    x1 = x3 + gt_ref[...] * mix
    x1_ref[...] = x1
    xn = x1 * lax.rsqrt(jnp.mean(x1 * x1, axis=-1, keepdims=True) + EPS) * g_ffn_ref[...]
    h2 = (xn * (1.0 + sc_ref[...]) + sh_ref[...]).reshape(tm, D_MODEL)
    h_hi, h_lo = _split(h2)
    h2p_ref[...] = h_hi
    logits = (_dot_nt(wr_hi_ref[...], h_hi) + _dot_nt(wr_hi_ref[...], h_lo) + _dot_nt(wr_lo_ref[...], h_hi)
              + br_ref[...])
    e_iota = lax.broadcasted_iota(I32, logits.shape, 0)
    vals, idxs = [], []
    for _ in range(TOP_K):
        m = jnp.max(logits, axis=0, keepdims=True)
        idx = jnp.min(jnp.where(logits == m, e_iota, N_EXPERTS), axis=0, keepdims=True)
        vals.append(m)
        idxs.append(idx)
        logits = jnp.where(e_iota == idx, -jnp.inf, logits)
    ex = [jnp.exp(v - vals[0]) for v in vals]
    tot = ex[0] + ex[1] + ex[2] + ex[3]
    ids_ref[...] = jnp.concatenate(idxs, axis=0)
    wts_ref[...] = jnp.concatenate([e / tot for e in ex], axis=0)


def _post(mixed, x, gt, sc, sh, w_out_b, g_ffn, wr_hi, wr_lo, br, sb, l, seq_div):
    n3, _, _ = x.shape
    nblk = n3 // sb
    tm = sb * l
    t = n3 * l
    ada_spec = pl.BlockSpec((sb, 1, D_MODEL), lambda i: (i // seq_div, 0, 0))
    consts = [w_out_b, g_ffn, wr_hi, wr_lo, br]
    return pl.pallas_call(
        _post_kernel,
        out_shape=(jax.ShapeDtypeStruct(x.shape, F32),
                   jax.ShapeDtypeStruct((t, D_MODEL), BF16),
                   jax.ShapeDtypeStruct((TOP_K, t), I32),
                   jax.ShapeDtypeStruct((TOP_K, t), F32)),
        grid=(nblk,),
        in_specs=[pl.BlockSpec((tm, D_MODEL), lambda i: (i, 0)),
                  pl.BlockSpec((sb, l, D_MODEL), lambda i: (i, 0, 0)), ada_spec, ada_spec, ada_spec]
                 + [_const_spec(a) for a in consts],
        out_specs=(pl.BlockSpec((sb, l, D_MODEL), lambda i: (i, 0, 0)),
                   pl.BlockSpec((tm, D_MODEL), lambda i: (i, 0)),
                   pl.BlockSpec((TOP_K, tm), lambda i: (0, i)),
                   pl.BlockSpec((TOP_K, tm), lambda i: (0, i))),
        compiler_params=pltpu.CompilerParams(dimension_semantics=("arbitrary",), vmem_limit_bytes=VMEM_LIMIT),
        name="post",
    )(mixed, x, gt, sc, sh, *consts)


def _strict_upper(n):
    r = lax.broadcasted_iota(I32, (n, n), 0)
    c = lax.broadcasted_iota(I32, (n, n), 1)
    return jnp.where(r < c, 1.0, 0.0).astype(BF16)


def _expert_prefix(col):
    r = lax.broadcasted_iota(I32, (N_EXPERTS, N_EXPERTS), 0)
    c = lax.broadcasted_iota(I32, (N_EXPERTS, N_EXPERTS), 1)
    as_row = jnp.sum(jnp.where(r == c, col, 0.0), axis=0, keepdims=True)
    return jnp.sum(jnp.where(c < r, as_row, 0.0), axis=1, keepdims=True)


def _plan_kernel(n_tiles_pad, dump_group, ids_ref, lr_ref, gdst_ref, tile_e_ref, seg_ref, gb_ref):
    ph = pl.program_id(0)
    b = pl.program_id(1)
    ids = ids_ref[...]
    tm = ids.shape[1]
    e_iota = lax.broadcasted_iota(I32, (N_EXPERTS, tm), 0)
    onehot = [ids[k:k + 1, :] == e_iota for k in range(TOP_K)]
    sel = (onehot[0] | onehot[1]) | (onehot[2] | onehot[3])
    m = jnp.where(sel, 1.0, 0.0)
    rowsum = jnp.sum(m, axis=1, keepdims=True)
    blk_lane = lax.broadcasted_iota(I32, (N_EXPERTS, LANES), 1)

    @pl.when((ph == 0) & (b == 0))
    def _():
        seg_ref[...] = jnp.zeros_like(seg_ref)

    @pl.when(ph == 0)
    def _():
        seg = jnp.ceil(rowsum * (1.0 / ROW_GROUP))
        seg_ref[...] = jnp.where(blk_lane == b, seg, seg_ref[...])

    @pl.when((ph == 1) & (b == 0))
    def _():
        seg = seg_ref[...]
        tot = jnp.sum(seg, axis=1, keepdims=True)
        padded = jnp.ceil(tot * (1.0 / TILE_GROUPS)) * TILE_GROUPS
        gstart = _expert_prefix(padded)
        gb_ref[...] = gstart + _dot(seg.astype(BF16), _strict_upper(LANES))
        ends = gstart + padded
        tile_g = (lax.broadcasted_iota(I32, (N_EXPERTS, n_tiles_pad), 1) * TILE_GROUPS).astype(F32)
        n_le = jnp.sum(jnp.where(ends <= tile_g, 1.0, 0.0), axis=0, keepdims=True)
        n_used = jnp.sum(padded, axis=0, keepdims=True) * (1.0 / TILE_GROUPS)
        lane = lax.broadcasted_iota(I32, (1, n_tiles_pad), 1)
        tile_e = jnp.minimum(n_le, N_EXPERTS - 1.0)
        tile_e_ref[...] = jnp.where(lane == n_tiles_pad - 1, n_used, tile_e).astype(I32)

    @pl.when(ph == 1)
    def _():
        seg_b = jnp.sum(jnp.where(blk_lane == b, seg_ref[...], 0.0), axis=1, keepdims=True)
        gb_b = jnp.sum(jnp.where(blk_lane == b, gb_ref[...], 0.0), axis=1, keepdims=True)
        loc_b = _expert_prefix(seg_b)
        before = _dot(m.astype(BF16), _strict_upper(tm)) + loc_b * ROW_GROUP
        lr_ref[...] = jnp.concatenate(
            [jnp.sum(jnp.where(onehot[k], before, 0.0), axis=0, keepdims=True) for k in range(TOP_K)],
            axis=0).astype(I32)
        g = lax.broadcasted_iota(I32, (N_EXPERTS, GDST_LANES), 1).astype(F32)
        inside = (loc_b <= g) & (g < loc_b + seg_b)
        dst = jnp.sum(jnp.where(inside, gb_b + g - loc_b, 0.0), axis=0, keepdims=True)
        used = jnp.sum(jnp.where(inside, 1.0, 0.0), axis=0, keepdims=True) > 0.5
        gdst_ref[0] = jnp.where(used, dst, dump_group + g[0:1, :]).astype(I32)


def _plan(ids, n_tiles_pad, dump_group):
    t = ids.shape[1]
    tm = TOK_TILE
    nb = t // tm
    assert nb <= LANES
    return pl.pallas_call(
        functools.partial(_plan_kernel, n_tiles_pad, float(dump_group)),
        out_shape=(jax.ShapeDtypeStruct((TOP_K, t), I32),
                   jax.ShapeDtypeStruct((nb, 1, GDST_LANES), I32),
                   jax.ShapeDtypeStruct((1, n_tiles_pad), I32)),
        grid=(2, nb),
        in_specs=[pl.BlockSpec((TOP_K, tm), lambda ph, b: (0, b))],
        out_specs=(pl.BlockSpec((TOP_K, tm), lambda ph, b: (0, b * ph)),
                   pl.BlockSpec((1, 1, GDST_LANES), lambda ph, b: (b * ph, 0, 0)),
                   pl.BlockSpec((1, n_tiles_pad), lambda ph, b: (0, 0))),
        scratch_shapes=[pltpu.VMEM((N_EXPERTS, LANES), F32), pltpu.VMEM((N_EXPERTS, LANES), F32)],
        compiler_params=pltpu.CompilerParams(dimension_semantics=("arbitrary", "arbitrary")),
        name="plan",
    )(ids)


def _local_hits(lr):
    r_iota = lax.broadcasted_iota(I32, (LOCAL_ROWS, lr.shape[1]), 0)
    return [r_iota == lr[k:k + 1, :] for k in range(TOP_K)]


def _group_copy(loc_ref, far_ref, gdst_ref, g, sem, to_far):
    dst = pl.multiple_of(gdst_ref[0, 0, g] * ROW_GROUP, ROW_GROUP)
    near = loc_ref.at[pl.ds(g * ROW_GROUP, ROW_GROUP)]
    far = far_ref.at[pl.ds(dst, ROW_GROUP)]
    return pltpu.make_async_copy(near, far, sem) if to_far else pltpu.make_async_copy(far, near, sem)


def _dispatch_kernel(n_first, gdst_ref, lr_ref, ha_ref, hb_ref, xs_in_ref, xs_ref, loc_ref, sem):
    del xs_in_ref
    i = pl.program_id(0)
    hits = _local_hits(lr_ref[...])
    p = jnp.where((hits[0] | hits[1]) | (hits[2] | hits[3]), 1.0, 0.0).astype(BF16)

    @pl.when(i < n_first)
    def _():
        loc_ref[...] = _dot(p, ha_ref[...]).astype(BF16)

    @pl.when(i >= n_first)
    def _():
        loc_ref[...] = _dot(p, hb_ref[...]).astype(BF16)

    copies = [_group_copy(loc_ref, xs_ref, gdst_ref, g, sem, True) for g in range(LOCAL_GROUPS)]
    for c in copies:
        c.start()
    for c in copies:
        c.wait()


def _dispatch(gdst, lr, h_a, h_b, xs):
    tm = TOK_TILE
    na, nb2 = h_a.shape[0] // tm, h_b.shape[0] // tm
    return pl.pallas_call(
        functools.partial(_dispatch_kernel, na),
        out_shape=jax.ShapeDtypeStruct(xs.shape, xs.dtype),
        grid=(na + nb2,),
        in_specs=[pl.BlockSpec((1, 1, GDST_LANES), lambda i: (i, 0, 0), memory_space=pltpu.SMEM),
                  pl.BlockSpec((TOP_K, tm), lambda i: (0, i)),
                  pl.BlockSpec((tm, D_MODEL), lambda i: (jnp.minimum(i, na - 1), 0)),
                  pl.BlockSpec((tm, D_MODEL), lambda i: (jnp.maximum(i - na, 0), 0)),
                  pl.BlockSpec(memory_space=pl.ANY)],
        out_specs=pl.BlockSpec(memory_space=pl.ANY),
        scratch_shapes=[pltpu.VMEM((LOCAL_ROWS, D_MODEL), BF16), pltpu.SemaphoreType.DMA],
        input_output_aliases={4: 0},
        compiler_params=pltpu.CompilerParams(dimension_semantics=("arbitrary",), vmem_limit_bytes=VMEM_LIMIT),
        name="dispatch",
    )(gdst, lr, h_a, h_b, xs)


def _expert_kernel(te_ref, xs_ref, wg_ref, wu_ref, wd_ref, bg_ref, bu_ref, bd_ref, y_ref, wgb, wub, wdb):
    i = pl.program_id(0)
    n_used = te_ref[te_ref.shape[0] - 1]
    prev_e = te_ref[jnp.maximum(i - 1, 0)]
    new_e = (i == 0) | (te_ref[i] != prev_e)

    @pl.when((i < n_used) & new_e)
    def _():
        wgb[...] = wg_ref[0].astype(BF16)
        wub[...] = wu_ref[0].astype(BF16)
        wdb[...] = wd_ref[0].astype(BF16)

    @pl.when(i < n_used)
    def _():
        x = xs_ref[...]
        g = jnp.minimum(_dot(x, wgb[...]) + bg_ref[0], SWIGLU_LIMIT)
        u = jnp.clip(_dot(x, wub[...]) + bu_ref[0], -SWIGLU_LIMIT, SWIGLU_LIMIT)
        act = g * jax.nn.sigmoid(SWIGLU_ALPHA * g) * (u + 1.0)
        y_ref[...] = (_dot(act.astype(BF16), wdb[...]) + bd_ref[0]).astype(BF16)

    @pl.when(i >= n_used)
    def _():
        y_ref[...] = jnp.zeros_like(y_ref)


def _experts(tile_e, xs, w_gate, b_gate, w_up, b_up, w_down, b_down):
    n_rows = xs.shape[0]
    n_tiles = n_rows // ROW_TILE
    last = tile_e.shape[0] - 1

    def row_map(i, te):
        return (jnp.minimum(i, te[last] - 1), 0)

    def w_map(i, te):
        return (te[jnp.minimum(i, te[last] - 1)], 0, 0)

    w_spec = pl.BlockSpec((1, D_MODEL, D_MODEL), w_map)
    b_spec = pl.BlockSpec((1, 1, D_MODEL), w_map)
    return pl.pallas_call(
        _expert_kernel,
        out_shape=jax.ShapeDtypeStruct((n_rows, D_MODEL), BF16),
        grid_spec=pltpu.PrefetchScalarGridSpec(
            num_scalar_prefetch=1,
            grid=(n_tiles,),
            in_specs=[pl.BlockSpec((ROW_TILE, D_MODEL), row_map), w_spec, w_spec, w_spec, b_spec, b_spec, b_spec],
            out_specs=pl.BlockSpec((ROW_TILE, D_MODEL), lambda i, te: (i, 0)),
            scratch_shapes=[pltpu.VMEM((D_MODEL, D_MODEL), BF16)] * 3,
        ),
        compiler_params=pltpu.CompilerParams(dimension_semantics=("arbitrary",), vmem_limit_bytes=VMEM_LIMIT),
        name="experts",
    )(tile_e, xs, w_gate, w_up, w_down, b_gate[:, None, :], b_up[:, None, :], b_down[:, None, :])


def _combine_kernel(gdst_ref, lr_ref, wts_ref, y_ref, x1_ref, gt_ref, gf_ref, out_ref, loc_ref, sem):
    x1 = x1_ref[...]
    sb, l, _ = x1.shape
    copies = [_group_copy(loc_ref, y_ref, gdst_ref, g, sem, False) for g in range(LOCAL_GROUPS)]
    for c in copies:
        c.start()
    hits = _local_hits(lr_ref[...])
    wts = wts_ref[...]
    pw = jnp.zeros(hits[0].shape, F32)
    for k in range(TOP_K):
        pw = jnp.where(hits[k], wts[k:k + 1, :], pw)
    w_row = jnp.sum(pw, axis=1, keepdims=True)
    p = jnp.where((hits[0] | hits[1]) | (hits[2] | hits[3]), 1.0, 0.0).astype(BF16)
    for c in copies:
        c.wait()
    yw = (loc_ref[...].astype(F32) * w_row).astype(BF16)
    moe = lax.dot_general(p, yw, (((0,), (0,)), ((), ())), preferred_element_type=F32)
    x2 = x1 + gt_ref[...] * moe.reshape(sb, l, D_MODEL)
    out_ref[...] = x2 * lax.rsqrt(jnp.mean(x2 * x2, axis=-1, keepdims=True) + EPS) * gf_ref[...]


def _combine(gdst, lr, wts, y, x1, gt, g_final, sb, l, seq_div, blk_off):
    n3 = x1.shape[0]
    nblk = n3 // sb
    tm = sb * l
    assert tm == TOK_TILE
    return pl.pallas_call(
        _combine_kernel,
        out_shape=jax.ShapeDtypeStruct(x1.shape, F32),
        grid=(nblk,),
        in_specs=[pl.BlockSpec((1, 1, GDST_LANES), lambda i: (i + blk_off, 0, 0), memory_space=pltpu.SMEM),
                  pl.BlockSpec((TOP_K, tm), lambda i: (0, i + blk_off)),
                  pl.BlockSpec((TOP_K, tm), lambda i: (0, i + blk_off)),
                  pl.BlockSpec(memory_space=pl.ANY),
                  pl.BlockSpec((sb, l, D_MODEL), lambda i: (i, 0, 0)),
                  pl.BlockSpec((sb, 1, D_MODEL), lambda i: (i // seq_div, 0, 0)),
                  pl.BlockSpec((1, D_MODEL), lambda i: (0, 0))],
        out_specs=pl.BlockSpec((sb, l, D_MODEL), lambda i: (i, 0, 0)),
        scratch_shapes=[pltpu.VMEM((LOCAL_ROWS, D_MODEL), BF16), pltpu.SemaphoreType.DMA],
        compiler_params=pltpu.CompilerParams(dimension_semantics=("arbitrary",), vmem_limit_bytes=VMEM_LIMIT),
        name="combine",
    )(gdst, lr, wts, y, x1, gt, g_final)


def kernel(x_prompt, x_sample, c_prompt, c_sample, state_ssm, state_conv, w_ada, b_ada, g_mix, w_in, g_v_a, w_spatial, b_spatial, g_out_a, conv_w, conv_b, dt_bias, a_log, d_skip, g_out_b, w_out, g_ffn, w_router, b_router, w_gate, b_gate, w_up, b_up, w_down, b_down, g_final):
    assert w_ada.shape[0] == 1, "single-layer step"
    p = dict(w_in=w_in[0], g_mix=g_mix[0], g_v_a=g_v_a[0], w_spatial=w_spatial[0], b_spatial=b_spatial[0],
             g_out_a=g_out_a[0], conv_w=conv_w[0], conv_b=conv_b[0], dt_bias=dt_bias[0], a_log=a_log[0],
             d_skip=d_skip[0], g_out_b=g_out_b[0])
    bp, lp, _ = x_prompt.shape
    bs, ls, _ = x_sample.shape
    tp, ts = bp * lp, bs * ls

    ada = _ada(jnp.concatenate([c_prompt, c_sample], axis=0), w_ada[0], b_ada[0][None, :])
    ada = ada.reshape(bp + bs, 6, 1, D_MODEL)
    ada_p = [ada[:bp, j] for j in range(6)]
    ada_s = [ada[bp:, j] for j in range(6)]

    fw = _front_weights(p)
    mixed_p, conv_p, ssm_p = _prompt_mixer(x_prompt, ada_p[0], ada_p[1], fw, p)
    mixed_s, v_s, conv_s, ssm_s = _sample_mixer(x_sample, ada_s[0], ada_s[1], state_ssm[0], state_conv[0], fw, p)

    w_out_b = w_out[0].astype(BF16)
    g_ffn2 = g_ffn[0][None, :]
    wr_t = w_router[0].T
    wr_hi = wr_t.astype(BF16)
    wr_lo = (wr_t - wr_hi.astype(F32)).astype(BF16)
    br = b_router[0][:, None]
    tps = lp // TOK_TILE
    sbs = TOK_TILE // ls
    xp3 = x_prompt.reshape(bp * tps, TOK_TILE, D_MODEL)
    x1_p, h2p_p, ids_p, wts_p = _post(mixed_p, xp3, ada_p[2], ada_p[4], ada_p[3], w_out_b, g_ffn2, wr_hi, wr_lo, br,
                                      1, TOK_TILE, tps)
    x1_s, h2p_s, ids_s, wts_s = _post(mixed_s, x_sample, ada_s[2], ada_s[4], ada_s[3], w_out_b, g_ffn2, wr_hi, wr_lo,
                                      br, sbs, ls, 1)

    n_blocks = (tp + ts) // TOK_TILE
    max_groups = (tp + ts) * TOP_K // ROW_GROUP + n_blocks * N_EXPERTS + N_EXPERTS * (TILE_GROUPS - 1)
    n_tiles = -(-max_groups // TILE_GROUPS)
    n_tiles_all = n_tiles + LOCAL_GROUPS // TILE_GROUPS
    n_tiles_pad = -(-(n_tiles_all + 1) // LANES) * LANES
    lr, gdst, tile_e = _plan(jnp.concatenate([ids_p, ids_s], axis=1), n_tiles_pad, n_tiles * TILE_GROUPS)
    wts = jnp.concatenate([wts_p, wts_s], axis=1)

    xs = jnp.zeros((n_tiles_all * ROW_TILE, D_MODEL), BF16)
    xs = _dispatch(gdst, lr, h2p_p, h2p_s, xs)
    y = _experts(tile_e[0], xs, w_gate[0], b_gate[0], w_up[0], b_up[0], w_down[0], b_down[0])

    gf = g_final[None, :]
    y_p = _combine(gdst, lr, wts, y, x1_p, ada_p[5], gf, 1, TOK_TILE, tps, 0).reshape(bp, lp, D_MODEL)
    y_s = _combine(gdst, lr, wts, y, x1_s, ada_s[5], gf, sbs, ls, 1, tp // TOK_TILE)

    return (y_p, y_s, ssm_p[None], conv_p[None], ssm_s[None], conv_s[None], v_s.reshape(1, bs, ls, A_WIDTH))
```

```python
import functools
import math

import numpy as np
import jax
import jax.numpy as jnp
from jax import lax
from jax.experimental import pallas as pl
from jax.experimental.pallas import tpu as pltpu

F32 = jnp.float32
BF16 = jnp.bfloat16
I32 = jnp.int32

D_MODEL = 1024
A_WIDTH = 512
A_HEADS = 4
A_HEAD_DIM = 128
CHUNK = 128
B_WIDTH = 512
SSD_HEAD_DIM = 64
SSD_HEADS = 8
SSD_GROUPS = 2
SSD_STATE = 128
GROUP_W = B_WIDTH // SSD_GROUPS
CONV_K = 4
CONV_DIM = 1024
CONV_PAD = 8
N_EXPERTS = 32
TOP_K = 4
SWIGLU_LIMIT = 7.0
SWIGLU_ALPHA = 1.702
EPS = 1e-6
LANES = 128

TOK_TILE = 512
SAMPLE_SEQ_TILE = 16
ROW_TILE = 256
ROW_GROUP = 16
TILE_GROUPS = ROW_TILE // ROW_GROUP
LOCAL_GROUPS = TOK_TILE * TOP_K // ROW_GROUP + N_EXPERTS
LOCAL_ROWS = LOCAL_GROUPS * ROW_GROUP
GDST_LANES = 256
ROW_W = D_MODEL + LANES
INFO_ROWS = 16
MASK_ROWS = 256
VMEM_LIMIT = 56 * 1024 * 1024


def _dot(a, b):
    return jnp.dot(a, b, preferred_element_type=F32)


def _dot_nt(a, b):
    return lax.dot_general(a, b, (((1,), (1,)), ((), ())), preferred_element_type=F32)


def _split(x):
    hi = x.astype(BF16)
    lo = (x - hi.astype(F32)).astype(BF16)
    return hi, lo


def _dot_exact_l(t, x):
    hi, lo = _split(x)
    return _dot(t, hi) + _dot(t, lo)


def _dot_exact_r(x, t):
    hi, lo = _split(x)
    return _dot(hi, t) + _dot(lo, t)


def _silu(x):
    return x * jax.nn.sigmoid(x)


def _gelu(x):
    return 0.5 * x * (1.0 + lax.erf(x * (1.0 / math.sqrt(2.0))))


def _softplus(x):
    return jnp.maximum(x, 0.0) + jnp.log1p(jnp.exp(-jnp.abs(x)))


def _rms(x, g):
    return x * lax.rsqrt(jnp.mean(x * x, axis=-1, keepdims=True) + EPS) * g


def _ada_kernel(c_ref, w_ref, b_ref, o_ref):
    s_hi, s_lo = _split(_silu(c_ref[...]))
    w_hi, w_lo = _split(w_ref[...])
    o_ref[...] = _dot(s_hi, w_hi) + _dot(s_lo, w_hi) + _dot(s_hi, w_lo) + b_ref[...]


def _ada(c_all, w_ada, b_ada):
    m = c_all.shape[0]
    n = w_ada.shape[1]
    bn = 512
    return pl.pallas_call(
        _ada_kernel,
        out_shape=jax.ShapeDtypeStruct((m, n), F32),
        grid=(n // bn,),
        in_specs=[pl.BlockSpec((m, D_MODEL), lambda j: (0, 0)),
                  pl.BlockSpec((D_MODEL, bn), lambda j: (0, j)),
                  pl.BlockSpec((1, bn), lambda j: (0, j))],
        out_specs=pl.BlockSpec((m, bn), lambda j: (0, j)),
        compiler_params=pltpu.CompilerParams(dimension_semantics=("arbitrary",), vmem_limit_bytes=VMEM_LIMIT),
        name="ada",
    )(c_all, w_ada, b_ada)


def _mixer_front(x3, sh, sc, prev, refs, xp_ref):
    (g_mix, w_uvz, w_xbc, w_dt, w_dtt, wbd, bias_sp, g_v, g_oa, conv_w, conv_b, dt_bias, dt_bias_t, a_row, a_col) = refs
    sb, l, _ = x3.shape
    tm = sb * l
    xn = x3 * lax.rsqrt(jnp.mean(x3 * x3, axis=-1, keepdims=True) + EPS) * g_mix[...]
    h = (xn * (1.0 + sc) + sh).reshape(tm, D_MODEL)
    hb = h.astype(BF16)
    uvz = _dot(hb, w_uvz[...])
    xbc = _dot(hb, w_xbc[...])
    dt_raw = _dot(hb, w_dt[...])
    dtt_raw = _dot_nt(w_dtt[...], hb)

    u = _gelu(uvz[:, :A_WIDTH])
    vg = _gelu(uvz[:, A_WIDTH:2 * A_WIDTH])
    z = uvz[:, 2 * A_WIDTH:]
    v_parts, s_parts = [], []
    for hd in range(A_HEADS):
        sl = slice(hd * A_HEAD_DIM, (hd + 1) * A_HEAD_DIM)
        vh = _rms(vg[:, sl], g_v[:, sl])
        v_parts.append(vh)
        s_parts.append(_dot(wbd[hd], vh.astype(BF16)))
    v = jnp.concatenate(v_parts, axis=1)
    s_a = jnp.concatenate(s_parts, axis=1) + bias_sp[...]
    out_a = _rms(u * s_a, g_oa[...])

    xp_ref[:, 0:CONV_PAD, :] = prev
    xp_ref[:, CONV_PAD:, :] = xbc.reshape(sb, l, CONV_DIM)
    acc = conv_b[...]
    for k in range(CONV_K):
        off = CONV_PAD - (CONV_K - 1) + k
        acc = acc + xp_ref[:, off:off + l, :] * conv_w[k:k + 1, :]
    xc = _silu(acc).reshape(tm, CONV_DIM)
    dt = _softplus(dt_raw + dt_bias[...])
    dtt = _softplus(dtt_raw + dt_bias_t[...])
    d_a = dt * a_row[...]
    d_at = dtt * a_col[...]
    return out_a, v, z, xc, dt, d_a, d_at


def _ssd_chunk(xs, bm, cm, dt, d_a, d_at, cref):
    tril, triu, ones, expand, mask = cref
    cs = _dot_exact_l(tril[...], d_a)
    cs_t = _dot_exact_r(d_at, triu[...])
    cs_tot = _dot_exact_l(ones[...], d_a)
    vals = jnp.concatenate([dt, jnp.exp(cs_tot - cs), jnp.exp(cs)], axis=0)
    vals_e = _dot_exact_r(vals, expand[...])
    n = xs.shape[0]
    dt_e, dte_e, e_e = vals_e[:n], vals_e[n:2 * n], vals_e[2 * n:]
    xdt = xs * dt_e
    xdtd = xdt * dte_e
    msk = mask[...] > 0.5
    row_lt_half = lax.broadcasted_iota(I32, (2 * n, LANES), 0) < n
    lane_lt_half = lax.broadcasted_iota(I32, (2 * n, LANES), 1) < SSD_HEAD_DIM
    y_parts = []
    for g in range(SSD_GROUPS):
        cb = _dot_nt(cm[:, g * SSD_STATE:(g + 1) * SSD_STATE].astype(BF16),
                     bm[:, g * SSD_STATE:(g + 1) * SSD_STATE].astype(BF16))
        for hp in range(SSD_HEADS // SSD_GROUPS // 2):
            h0 = g * (SSD_HEADS // SSD_GROUPS) + 2 * hp
            ms = []
            for hh in (h0, h0 + 1):
                diff = cs[:, hh:hh + 1] - cs_t[hh:hh + 1, :]
                ms.append((cb * jnp.where(msk, jnp.exp(jnp.where(msk, diff, 0.0)), 0.0)).astype(BF16))
            pair = xdt[:, h0 * SSD_HEAD_DIM:(h0 + 2) * SSD_HEAD_DIM]
            rhs = jnp.where(row_lt_half == lane_lt_half, jnp.concatenate([pair, pair], axis=0), 0.0).astype(BF16)
            y_parts.append(_dot(jnp.concatenate(ms, axis=1), rhs))
    y_diag = jnp.concatenate(y_parts, axis=1)
    return y_diag, e_e, xdtd, cs_tot


def _mixer_back(y, xs, z, out_a, dskip_e, g_ob):
    y = y + xs * dskip_e
    gated = y * _silu(z)
    parts = [_rms(gated[:, g * GROUP_W:(g + 1) * GROUP_W], g_ob[:, g * GROUP_W:(g + 1) * GROUP_W])
             for g in range(SSD_GROUPS)]
    return jnp.concatenate([out_a] + parts, axis=1).astype(BF16)


N_FRONT = 15
N_SSD = 5


def _prompt_mixer_kernel(tiles_per_seq, x_ref, sh_ref, sc_ref, *rest):
    front = rest[:N_FRONT]
    cref = rest[N_FRONT:N_FRONT + N_SSD]
    dskip_e, g_ob = rest[N_FRONT + N_SSD:N_FRONT + N_SSD + 2]
    mixed_ref, conv_out_ref, ssm_out_ref = rest[N_FRONT + N_SSD + 2:N_FRONT + N_SSD + 5]
    xp_ref, carry_ref, st_ref = rest[N_FRONT + N_SSD + 5:]
    i = pl.program_id(0)
    first = (i % tiles_per_seq) == 0

    @pl.when(first)
    def _():
        carry_ref[...] = jnp.zeros_like(carry_ref)
        st_ref[...] = jnp.zeros_like(st_ref)

    x3 = x_ref[...]
    l = x3.shape[1]
    out_a, _, z, xc, dt, d_a, d_at = _mixer_front(x3, sh_ref[...], sc_ref[...], carry_ref[...], front, xp_ref)
    carry_ref[...] = xp_ref[:, l:l + CONV_PAD, :]
    xs = xc[:, :B_WIDTH]
    y_rows = []
    for c in range(l // CHUNK):
        r = slice(c * CHUNK, (c + 1) * CHUNK)
        bm = xc[r, B_WIDTH:B_WIDTH + SSD_GROUPS * SSD_STATE]
        cm = xc[r, B_WIDTH + SSD_GROUPS * SSD_STATE:]
        y_diag, e_e, xdtd, _ = _ssd_chunk(xs[r], bm, cm, dt[r], d_a[r], d_at[:, r], cref)
        st = st_ref[...]
        y_off, upd = [], []
        for g in range(SSD_GROUPS):
            gs = slice(g * GROUP_W, (g + 1) * GROUP_W)
            ns = slice(g * SSD_STATE, (g + 1) * SSD_STATE)
            y_off.append(_dot(cm[:, ns].astype(BF16), st[:, gs].astype(BF16)))
            upd.append(_dot(bm[:, ns].T.astype(BF16), xdtd[:, gs].astype(BF16)))
        y_rows.append(y_diag + jnp.concatenate(y_off, axis=1) * e_e)
        st_ref[...] = st * e_e[CHUNK - 1:CHUNK, :] + jnp.concatenate(upd, axis=1)
    y = jnp.concatenate(y_rows, axis=0)
    mixed_ref[...] = _mixer_back(y, xs, z, out_a, dskip_e[...], g_ob[...])

    @pl.when((i % tiles_per_seq) == tiles_per_seq - 1)
    def _():
        conv_out_ref[...] = xp_ref[:, l + CONV_PAD - (CONV_K - 1):l + CONV_PAD, :]
        ssm_out_ref[0] = st_ref[...].T


def _sample_mixer_kernel(x_ref, sh_ref, sc_ref, prev_ref, ssm0_ref, *rest):
    front = rest[:N_FRONT]
    cref = rest[N_FRONT:N_FRONT + N_SSD]
    dskip_e, g_ob, selseq = rest[N_FRONT + N_SSD:N_FRONT + N_SSD + 3]
    mixed_ref, v_ref, conv_out_ref, ssm_out_ref = rest[N_FRONT + N_SSD + 3:N_FRONT + N_SSD + 7]
    xp_ref, yoff_ref, cbf_ref, bbf_ref, t1_ref, dtab_ref = rest[N_FRONT + N_SSD + 7:]
    x3 = x_ref[...]
    sb, l, _ = x3.shape
    tm = sb * l
    out_a, v, z, xc, dt, d_a, d_at = _mixer_front(x3, sh_ref[...], sc_ref[...], prev_ref[...], front, xp_ref)
    v_ref[...] = v
    conv_out_ref[...] = xp_ref[:, l + CONV_PAD - (CONV_K - 1):l + CONV_PAD, :]
    xs = xc[:, :B_WIDTH]
    bm = xc[:, B_WIDTH:B_WIDTH + SSD_GROUPS * SSD_STATE]
    cm = xc[:, B_WIDTH + SSD_GROUPS * SSD_STATE:]
    y_diag, e_e, xdtd, _ = _ssd_chunk(xs, bm, cm, dt, d_a, d_at, cref)

    e_tot = jnp.exp(_dot_exact_l(selseq[...], d_a))
    for hh in range(SSD_HEADS):
        dtab_ref[hh] = jnp.broadcast_to(e_tot[:, hh:hh + 1], (sb, LANES))
    cbf_ref[...] = cm
    bbf_ref[...] = bm
    for g in range(SSD_GROUPS):
        t1_ref[g] = xdtd[:, g * GROUP_W:(g + 1) * GROUP_W].T.astype(BF16)
    seq_of_row = lax.broadcasted_iota(I32, (tm, SSD_STATE), 0) // l
    heads_per_group = SSD_HEADS // SSD_GROUPS

    def body(j, carry):
        r0 = pl.multiple_of(j * l, l)
        s0 = ssm0_ref[j]
        for g in range(SSD_GROUPS):
            ns = slice(g * SSD_STATE, (g + 1) * SSD_STATE)
            s0g = s0[g * heads_per_group:(g + 1) * heads_per_group].reshape(GROUP_W, SSD_STATE)
            cj = cbf_ref[pl.ds(r0, l), ns].astype(BF16)
            yoff_ref[pl.ds(r0, l), g * GROUP_W:(g + 1) * GROUP_W] = _dot_nt(cj, s0g.astype(BF16))
            bmask = jnp.where(seq_of_row == j, bbf_ref[:, ns], 0.0).astype(BF16)
            upd = _dot(t1_ref[g], bmask)
            for hq in range(heads_per_group):
                hh = g * heads_per_group + hq
                dec = dtab_ref[hh, pl.ds(j, 1), :]
                ssm_out_ref[j, hh] = s0[hh] * dec + upd[hq * SSD_HEAD_DIM:(hq + 1) * SSD_HEAD_DIM]
        return carry

    lax.fori_loop(0, sb, body, 0)
    y = y_diag + yoff_ref[...] * e_e
    mixed_ref[...] = _mixer_back(y, xs, z, out_a, dskip_e[...], g_ob[...])


def _const_spec(a):
    nd = a.ndim
    return pl.BlockSpec(a.shape, lambda i, _nd=nd: (0,) * _nd)


def _spatial_consts(w_spatial, b_spatial, cl, tm):
    w = jnp.where(jnp.tril(jnp.ones((cl, cl), bool)), w_spatial[:, :cl, :cl], 0.0)
    eye = jnp.eye(tm // cl, dtype=F32)
    wbd = jnp.einsum("ab,hts->hatbs", eye, w).reshape(A_HEADS, tm, tm).astype(BF16)
    bias = jnp.tile(jnp.repeat(b_spatial[:, :cl].T, A_HEAD_DIM, axis=1), (tm // cl, 1))
    return wbd, bias


def _ssd_consts(cl):
    r = np.arange(CHUNK)
    same = (r[:, None] // cl) == (r[None, :] // cl)
    tril = same & (r[:, None] >= r[None, :])
    expand = np.zeros((LANES, B_WIDTH), np.float32)
    for hh in range(SSD_HEADS):
        expand[hh, hh * SSD_HEAD_DIM:(hh + 1) * SSD_HEAD_DIM] = 1.0
    return (jnp.asarray(tril, BF16), jnp.asarray(tril.T, BF16), jnp.asarray(same, BF16),
            jnp.asarray(expand, BF16), jnp.asarray(tril, F32))


def _front_weights(p):
    w_in = p["w_in"]
    c0, c1 = 3 * A_WIDTH, 3 * A_WIDTH + CONV_DIM
    w_dt = w_in[:, c1:]
    pad8 = lambda v: jnp.pad(v, (0, LANES - SSD_HEADS))
    a = -jnp.exp(p["a_log"])
    return dict(
        g_mix=p["g_mix"][None, :],
        w_uvz=w_in[:, :c0].astype(BF16),
        w_xbc=w_in[:, c0:c1].astype(BF16),
        w_dt=jnp.pad(w_dt, ((0, 0), (0, LANES - SSD_HEADS))).astype(BF16),
        w_dtt=w_dt.T.astype(BF16),
        g_v=p["g_v_a"][None, :], g_oa=p["g_out_a"][None, :],
        conv_w=p["conv_w"], conv_b=p["conv_b"][None, :],
        dt_bias=pad8(p["dt_bias"])[None, :], dt_bias_t=p["dt_bias"][:, None],
        a_row=pad8(a)[None, :], a_col=a[:, None],
        dskip_e=jnp.repeat(p["d_skip"], SSD_HEAD_DIM)[None, :],
        g_ob=p["g_out_b"][None, :],
    )


def _front_list(fw, wbd, bias_sp):
    return [fw["g_mix"], fw["w_uvz"], fw["w_xbc"], fw["w_dt"], fw["w_dtt"], wbd, bias_sp, fw["g_v"], fw["g_oa"],
            fw["conv_w"], fw["conv_b"], fw["dt_bias"], fw["dt_bias_t"], fw["a_row"], fw["a_col"]]


def _prompt_mixer(x, sh, sc, fw, p):
    nseq, lseq, _ = x.shape
    tps = lseq // TOK_TILE
    nt = nseq * tps
    x4 = x.reshape(nt, TOK_TILE, D_MODEL)
    wbd, bias_sp = _spatial_consts(p["w_spatial"], p["b_spatial"], CHUNK, TOK_TILE)
    consts = _front_list(fw, wbd, bias_sp) + list(_ssd_consts(CHUNK)) + [fw["dskip_e"], fw["g_ob"]]
    seq_spec = pl.BlockSpec((1, 1, D_MODEL), lambda i: (i // tps, 0, 0))
    mixed, conv_new, ssm_new = pl.pallas_call(
        functools.partial(_prompt_mixer_kernel, tps),
        out_shape=(jax.ShapeDtypeStruct((nt * TOK_TILE, D_MODEL), BF16),
                   jax.ShapeDtypeStruct((nseq, CONV_K - 1, CONV_DIM), F32),
                   jax.ShapeDtypeStruct((nseq, B_WIDTH, SSD_STATE), F32)),
        grid=(nt,),
        in_specs=[pl.BlockSpec((1, TOK_TILE, D_MODEL), lambda i: (i, 0, 0)), seq_spec, seq_spec]
                 + [_const_spec(a) for a in consts],
        out_specs=(pl.BlockSpec((TOK_TILE, D_MODEL), lambda i: (i, 0)),
                   pl.BlockSpec((1, CONV_K - 1, CONV_DIM), lambda i: (i // tps, 0, 0)),
                   pl.BlockSpec((1, B_WIDTH, SSD_STATE), lambda i: (i // tps, 0, 0))),
        scratch_shapes=[pltpu.VMEM((1, TOK_TILE + CONV_PAD, CONV_DIM), F32),
                        pltpu.VMEM((1, CONV_PAD, CONV_DIM), F32),
                        pltpu.VMEM((SSD_STATE, B_WIDTH), F32)],
        compiler_params=pltpu.CompilerParams(dimension_semantics=("arbitrary",), vmem_limit_bytes=VMEM_LIMIT),
        name="prompt_mixer",
    )(x4, sh, sc, *consts)
    return mixed, conv_new, ssm_new.reshape(nseq, SSD_HEADS, SSD_HEAD_DIM, SSD_STATE)


def _sample_mixer(x, sh, sc, state_ssm, state_conv, fw, p):
    nseq, l, _ = x.shape
    sb = SAMPLE_SEQ_TILE
    tm = sb * l
    assert tm == CHUNK
    wbd, bias_sp = _spatial_consts(p["w_spatial"], p["b_spatial"], l, tm)
    selseq = jnp.asarray((np.arange(tm)[None, :] // l) == np.arange(sb)[:, None], BF16)
    consts = _front_list(fw, wbd, bias_sp) + list(_ssd_consts(l)) + [fw["dskip_e"], fw["g_ob"], selseq]
    prev = jnp.pad(state_conv, ((0, 0), (CONV_PAD - (CONV_K - 1), 0), (0, 0)))
    seq_spec = pl.BlockSpec((sb, 1, D_MODEL), lambda i: (i, 0, 0))
    ssm_spec = pl.BlockSpec((sb, SSD_HEADS, SSD_HEAD_DIM, SSD_STATE), lambda i: (i, 0, 0, 0))
    return pl.pallas_call(
        _sample_mixer_kernel,
        out_shape=(jax.ShapeDtypeStruct((nseq * l, D_MODEL), BF16),
                   jax.ShapeDtypeStruct((nseq * l, A_WIDTH), F32),
                   jax.ShapeDtypeStruct((nseq, CONV_K - 1, CONV_DIM), F32),
                   jax.ShapeDtypeStruct(state_ssm.shape, F32)),
        grid=(nseq // sb,),
        in_specs=[pl.BlockSpec((sb, l, D_MODEL), lambda i: (i, 0, 0)), seq_spec, seq_spec,
                  pl.BlockSpec((sb, CONV_PAD, CONV_DIM), lambda i: (i, 0, 0)), ssm_spec]
                 + [_const_spec(a) for a in consts],
        out_specs=(pl.BlockSpec((tm, D_MODEL), lambda i: (i, 0)),
                   pl.BlockSpec((tm, A_WIDTH), lambda i: (i, 0)),
                   pl.BlockSpec((sb, CONV_K - 1, CONV_DIM), lambda i: (i, 0, 0)),
                   ssm_spec),
        scratch_shapes=[pltpu.VMEM((sb, l + CONV_PAD, CONV_DIM), F32),
                        pltpu.VMEM((tm, B_WIDTH), F32),
                        pltpu.VMEM((tm, SSD_GROUPS * SSD_STATE), F32),
                        pltpu.VMEM((tm, SSD_GROUPS * SSD_STATE), F32),
                        pltpu.VMEM((SSD_GROUPS, GROUP_W, tm), BF16),
                        pltpu.VMEM((SSD_HEADS, sb, LANES), F32)],
        compiler_params=pltpu.CompilerParams(dimension_semantics=("arbitrary",), vmem_limit_bytes=VMEM_LIMIT),
        name="sample_mixer",
    )(x, sh, sc, prev, state_ssm, *consts)


def _post_kernel(mixed_ref, x_ref, gt_ref, sc_ref, sh_ref, w_out_ref, g_ffn_ref, wr_hi_ref, wr_lo_ref, br_ref,
                 x1_ref, h2p_ref, ids_ref):
    x3 = x_ref[...]
    sb, l, _ = x3.shape
    tm = sb * l
    mix = _dot(mixed_ref[...], w_out_ref[...]).reshape(sb, l, D_MODEL)
    x1 = x3 + gt_ref[...] * mix
    x1_ref[...] = x1
    xn = x1 * lax.rsqrt(jnp.mean(x1 * x1, axis=-1, keepdims=True) + EPS) * g_ffn_ref[...]
    h2 = (xn * (1.0 + sc_ref[...]) + sh_ref[...]).reshape(tm, D_MODEL)
    h_hi, h_lo = _split(h2)
    h2p_ref[:, :D_MODEL] = h_hi
    logits = (_dot_nt(wr_hi_ref[...], h_hi) + _dot_nt(wr_hi_ref[...], h_lo) + _dot_nt(wr_lo_ref[...], h_hi)
              + br_ref[...])
    e_iota = lax.broadcasted_iota(I32, logits.shape, 0)
    vals, idxs = [], []
    for _ in range(TOP_K):
        m = jnp.max(logits, axis=0, keepdims=True)
        idx = jnp.min(jnp.where(logits == m, e_iota, N_EXPERTS), axis=0, keepdims=True)
        vals.append(m)
        idxs.append(idx)
        logits = jnp.where(e_iota == idx, -jnp.inf, logits)
    ex = [jnp.exp(v - vals[0]) for v in vals]
    tot = ex[0] + ex[1] + ex[2] + ex[3]
    ids = jnp.concatenate(idxs, axis=0)
    ids_ref[...] = ids
    wts = jnp.concatenate([e / tot for e in ex], axis=0)
    w_hi = wts.astype(BF16).astype(F32)
    info = jnp.concatenate([ids.astype(F32), w_hi, wts - w_hi, jnp.zeros((TOP_K, tm), F32)], axis=0).astype(BF16)
    r = lax.broadcasted_iota(I32, (INFO_ROWS, LANES), 0)
    c = lax.broadcasted_iota(I32, (INFO_ROWS, LANES), 1)
    place = jnp.where(r == c, 1.0, 0.0).astype(BF16)
    h2p_ref[:, D_MODEL:] = lax.dot_general(info, place, (((0,), (0,)), ((), ())),
                                           preferred_element_type=F32).astype(BF16)


def _post(mixed, x, gt, sc, sh, w_out_b, g_ffn, wr_hi, wr_lo, br, sb, l, seq_div):
    n3, _, _ = x.shape
    nblk = n3 // sb
    tm = sb * l
    t = n3 * l
    ada_spec = pl.BlockSpec((sb, 1, D_MODEL), lambda i: (i // seq_div, 0, 0))
    consts = [w_out_b, g_ffn, wr_hi, wr_lo, br]
    return pl.pallas_call(
        _post_kernel,
        out_shape=(jax.ShapeDtypeStruct(x.shape, F32),
                   jax.ShapeDtypeStruct((t, ROW_W), BF16),
                   jax.ShapeDtypeStruct((TOP_K, t), I32)),
        grid=(nblk,),
        in_specs=[pl.BlockSpec((tm, D_MODEL), lambda i: (i, 0)),
                  pl.BlockSpec((sb, l, D_MODEL), lambda i: (i, 0, 0)), ada_spec, ada_spec, ada_spec]
                 + [_const_spec(a) for a in consts],
        out_specs=(pl.BlockSpec((sb, l, D_MODEL), lambda i: (i, 0, 0)),
                   pl.BlockSpec((tm, ROW_W), lambda i: (i, 0)),
                   pl.BlockSpec((TOP_K, tm), lambda i: (0, i))),
        compiler_params=pltpu.CompilerParams(dimension_semantics=("arbitrary",), vmem_limit_bytes=VMEM_LIMIT),
        name="post",
    )(mixed, x, gt, sc, sh, *consts)


def _strict_upper(n):
    r = lax.broadcasted_iota(I32, (n, n), 0)
    c = lax.broadcasted_iota(I32, (n, n), 1)
    return jnp.where(r < c, 1.0, 0.0).astype(BF16)


def _expert_prefix(col):
    r = lax.broadcasted_iota(I32, (N_EXPERTS, N_EXPERTS), 0)
    c = lax.broadcasted_iota(I32, (N_EXPERTS, N_EXPERTS), 1)
    as_row = jnp.sum(jnp.where(r == c, col, 0.0), axis=0, keepdims=True)
    return jnp.sum(jnp.where(c < r, as_row, 0.0), axis=1, keepdims=True)


def _plan_kernel(dump_group, ids_ref, lr_ref, gdst_ref, tile_e_ref, seg_ref, gb_ref):
    ph = pl.program_id(0)
    b = pl.program_id(1)
    ids = ids_ref[...]
    tm = ids.shape[1]
    e_iota = lax.broadcasted_iota(I32, (N_EXPERTS, tm), 0)
    onehot = [ids[k:k + 1, :] == e_iota for k in range(TOP_K)]
    sel = (onehot[0] | onehot[1]) | (onehot[2] | onehot[3])
    m = jnp.where(sel, 1.0, 0.0)
    rowsum = jnp.sum(m, axis=1, keepdims=True)
    blk_lane = lax.broadcasted_iota(I32, (N_EXPERTS, LANES), 1)

    @pl.when((ph == 0) & (b == 0))
    def _():
        seg_ref[...] = jnp.zeros_like(seg_ref)

    @pl.when(ph == 0)
    def _():
        seg = jnp.ceil(rowsum * (1.0 / ROW_GROUP))
        seg_ref[...] = jnp.where(blk_lane == b, seg, seg_ref[...])

    @pl.when((ph == 1) & (b == 0))
    def _():
        seg = seg_ref[...]
        tot = jnp.sum(seg, axis=1, keepdims=True)
        padded = jnp.ceil(tot * (1.0 / TILE_GROUPS)) * TILE_GROUPS
        gstart = _expert_prefix(padded)
        gb_ref[...] = gstart + _dot(seg.astype(BF16), _strict_upper(LANES))
        r = lax.broadcasted_iota(I32, (N_EXPERTS, LANES), 0)
        c = lax.broadcasted_iota(I32, (N_EXPERTS, LANES), 1)
        starts = jnp.sum(jnp.where(r == c, gstart, 0.0), axis=0, keepdims=True)
        n_used = jnp.sum(padded, axis=0, keepdims=True)
        lane = lax.broadcasted_iota(I32, (1, LANES), 1)
        tile_e_ref[...] = (jnp.where(lane == N_EXPERTS, n_used, starts) * (1.0 / TILE_GROUPS)).astype(I32)

    @pl.when(ph == 1)
    def _():
        seg_b = jnp.sum(jnp.where(blk_lane == b, seg_ref[...], 0.0), axis=1, keepdims=True)
        gb_b = jnp.sum(jnp.where(blk_lane == b, gb_ref[...], 0.0), axis=1, keepdims=True)
        loc_b = _expert_prefix(seg_b)
        before = _dot(m.astype(BF16), _strict_upper(tm)) + loc_b * ROW_GROUP
        lr_ref[...] = jnp.concatenate(
            [jnp.sum(jnp.where(onehot[k], before, 0.0), axis=0, keepdims=True) for k in range(TOP_K)],
            axis=0).astype(I32)
        g = lax.broadcasted_iota(I32, (N_EXPERTS, GDST_LANES), 1).astype(F32)
        inside = (loc_b <= g) & (g < loc_b + seg_b)
        dst = jnp.sum(jnp.where(inside, gb_b + g - loc_b, 0.0), axis=0, keepdims=True)
        used = jnp.sum(jnp.where(inside, 1.0, 0.0), axis=0, keepdims=True) > 0.5
        gdst_ref[0] = jnp.where(used, dst, dump_group + g[0:1, :]).astype(I32)


def _plan(ids, dump_group):
    t = ids.shape[1]
    tm = TOK_TILE
    nb = t // tm
    assert nb <= LANES
    return pl.pallas_call(
        functools.partial(_plan_kernel, float(dump_group)),
        out_shape=(jax.ShapeDtypeStruct((TOP_K, t), I32),
                   jax.ShapeDtypeStruct((nb, 1, GDST_LANES), I32),
                   jax.ShapeDtypeStruct((1, LANES), I32)),
        grid=(2, nb),
        in_specs=[pl.BlockSpec((TOP_K, tm), lambda ph, b: (0, b))],
        out_specs=(pl.BlockSpec((TOP_K, tm), lambda ph, b: (0, b * ph)),
                   pl.BlockSpec((1, 1, GDST_LANES), lambda ph, b: (b * ph, 0, 0)),
                   pl.BlockSpec((1, LANES), lambda ph, b: (0, 0))),
        scratch_shapes=[pltpu.VMEM((N_EXPERTS, LANES), F32), pltpu.VMEM((N_EXPERTS, LANES), F32)],
        compiler_params=pltpu.CompilerParams(dimension_semantics=("arbitrary", "arbitrary")),
        name="plan",
    )(ids)


def _sort_matrix(lr, c):
    r_iota = lax.broadcasted_iota(I32, (MASK_ROWS, lr.shape[1]), 0) + c * MASK_ROWS
    p = jnp.where(r_iota == lr[TOP_K - 1:TOP_K, :], 1.0, 0.0)
    for k in range(TOP_K - 1):
        p = jnp.where(r_iota == lr[k:k + 1, :], 1.0, p)
    return p.astype(BF16)


def _group_copy(loc_ref, far_ref, gdst_ref, g, sem, to_far):
    dst = pl.multiple_of(gdst_ref[0, 0, g] * ROW_GROUP, ROW_GROUP)
    near = loc_ref.at[pl.ds(g * ROW_GROUP, ROW_GROUP)]
    far = far_ref.at[pl.ds(dst, ROW_GROUP)]
    return pltpu.make_async_copy(near, far, sem) if to_far else pltpu.make_async_copy(far, near, sem)


def _dispatch_kernel(n_first, gdst_ref, lr_ref, ha_ref, hb_ref, xs_in_ref, xs_ref, h_ref, loc_ref, sem):
    del xs_in_ref
    i = pl.program_id(0)

    @pl.when(i < n_first)
    def _():
        h_ref[...] = ha_ref[...]

    @pl.when(i >= n_first)
    def _():
        h_ref[...] = hb_ref[...]

    lr = lr_ref[...]
    for c in range(LOCAL_ROWS // MASK_ROWS):
        loc_ref[c * MASK_ROWS:(c + 1) * MASK_ROWS, :] = _dot(_sort_matrix(lr, c), h_ref[...]).astype(BF16)
    copies = [_group_copy(loc_ref, xs_ref, gdst_ref, g, sem, True) for g in range(LOCAL_GROUPS)]
    for c in copies:
        c.start()
    for c in copies:
        c.wait()


def _dispatch(gdst, lr, h_a, h_b, xs):
    tm = TOK_TILE
    na, nb2 = h_a.shape[0] // tm, h_b.shape[0] // tm
    return pl.pallas_call(
        functools.partial(_dispatch_kernel, na),
        out_shape=jax.ShapeDtypeStruct(xs.shape, xs.dtype),
        grid=(na + nb2,),
        in_specs=[pl.BlockSpec((1, 1, GDST_LANES), lambda i: (i, 0, 0), memory_space=pltpu.SMEM),
                  pl.BlockSpec((TOP_K, tm), lambda i: (0, i)),
                  pl.BlockSpec((tm, ROW_W), lambda i: (jnp.minimum(i, na - 1), 0)),
                  pl.BlockSpec((tm, ROW_W), lambda i: (jnp.maximum(i - na, 0), 0)),
                  pl.BlockSpec(memory_space=pl.ANY)],
        out_specs=pl.BlockSpec(memory_space=pl.ANY),
        scratch_shapes=[pltpu.VMEM((tm, ROW_W), BF16), pltpu.VMEM((LOCAL_ROWS, ROW_W), BF16),
                        pltpu.SemaphoreType.DMA],
        input_output_aliases={4: 0},
        compiler_params=pltpu.CompilerParams(dimension_semantics=("arbitrary",), vmem_limit_bytes=VMEM_LIMIT),
        name="dispatch",
    )(gdst, lr, h_a, h_b, xs)


def _expert_kernel(n_tiles_all, ts_ref, xs_ref, wg_ref, wu_ref, wd_ref, bg_ref, bu_ref, bd_ref, y_ref,
                   wgb, wub, wdb, xbuf, ybuf, sem_in, sem_out):
    e = pl.program_id(0)
    t0 = ts_ref[e]
    nt = ts_ref[e + 1] - t0

    def in_copy(t, slot):
        rows = pl.ds(pl.multiple_of((t0 + t) * ROW_TILE, ROW_TILE), ROW_TILE)
        return pltpu.make_async_copy(xs_ref.at[rows], xbuf.at[slot], sem_in.at[slot])

    def out_copy(tile, slot):
        rows = pl.ds(pl.multiple_of(tile * ROW_TILE, ROW_TILE), ROW_TILE)
        return pltpu.make_async_copy(ybuf.at[slot], y_ref.at[rows], sem_out.at[slot])

    @pl.when(nt > 0)
    def _():
        in_copy(0, 0).start()
        wgb[...] = wg_ref[0].astype(BF16)
        wub[...] = wu_ref[0].astype(BF16)
        wdb[...] = wd_ref[0].astype(BF16)
        e_f = e.astype(F32)

        def body(t, carry):
            slot = t % 2
            in_copy(t, slot).wait()

            @pl.when(t + 1 < nt)
            def _():
                in_copy(t + 1, 1 - slot).start()

            @pl.when(t >= 2)
            def _():
                out_copy(t0 + t - 2, slot).wait()

            xw = xbuf[slot]
            x = xw[:, :D_MODEL]
            info = xw[:, D_MODEL:].astype(F32)
            w_row = jnp.zeros((ROW_TILE, 1), F32)
            for k in range(TOP_K):
                wk = info[:, TOP_K + k:TOP_K + k + 1] + info[:, 2 * TOP_K + k:2 * TOP_K + k + 1]
                w_row = w_row + jnp.where(info[:, k:k + 1] == e_f, wk, 0.0)
            g = jnp.minimum(_dot(x, wgb[...]) + bg_ref[0], SWIGLU_LIMIT)
            u = jnp.clip(_dot(x, wub[...]) + bu_ref[0], -SWIGLU_LIMIT, SWIGLU_LIMIT)
            act = g * jax.nn.sigmoid(SWIGLU_ALPHA * g) * (u + 1.0)
            ybuf[slot] = ((_dot(act.astype(BF16), wdb[...]) + bd_ref[0]) * w_row).astype(BF16)
            out_copy(t0 + t, slot).start()
            return carry

        lax.fori_loop(0, nt, body, 0)

        @pl.when(nt >= 2)
        def _():
            out_copy(t0 + nt - 2, nt % 2).wait()

        out_copy(t0 + nt - 1, (nt - 1) % 2).wait()

    @pl.when(e == pl.num_programs(0) - 1)
    def _():
        n_used = ts_ref[N_EXPERTS]
        ybuf[0] = jnp.zeros((ROW_TILE, D_MODEL), BF16)

        def zbody(tile, carry):
            c = out_copy(tile, 0)
            c.start()
            c.wait()
            return carry

        lax.fori_loop(n_used, n_tiles_all, zbody, 0)


def _experts(tile_start, xs, w_gate, b_gate, w_up, b_up, w_down, b_down):
    n_rows = xs.shape[0]
    w_spec = pl.BlockSpec((1, D_MODEL, D_MODEL), lambda e, ts: (e, 0, 0))
    b_spec = pl.BlockSpec((1, 1, D_MODEL), lambda e, ts: (e, 0, 0))
    any_spec = pl.BlockSpec(memory_space=pl.ANY)
    return pl.pallas_call(
        functools.partial(_expert_kernel, n_rows // ROW_TILE),
        out_shape=jax.ShapeDtypeStruct((n_rows, D_MODEL), BF16),
        grid_spec=pltpu.PrefetchScalarGridSpec(
            num_scalar_prefetch=1,
            grid=(N_EXPERTS,),
            in_specs=[any_spec, w_spec, w_spec, w_spec, b_spec, b_spec, b_spec],
            out_specs=any_spec,
            scratch_shapes=[pltpu.VMEM((D_MODEL, D_MODEL), BF16)] * 3
                           + [pltpu.VMEM((2, ROW_TILE, ROW_W), BF16), pltpu.VMEM((2, ROW_TILE, D_MODEL), BF16),
                              pltpu.SemaphoreType.DMA((2,)), pltpu.SemaphoreType.DMA((2,))],
        ),
        compiler_params=pltpu.CompilerParams(dimension_semantics=("arbitrary",), vmem_limit_bytes=VMEM_LIMIT),
        name="experts",
    )(tile_start, xs, w_gate, w_up, w_down, b_gate[:, None, :], b_up[:, None, :], b_down[:, None, :])


def _combine_kernel(gdst_ref, lr_ref, y_ref, x1_ref, gt_ref, gf_ref, out_ref, loc_ref, sem):
    x1 = x1_ref[...]
    sb, l, _ = x1.shape
    copies = [_group_copy(loc_ref, y_ref, gdst_ref, g, sem, False) for g in range(LOCAL_GROUPS)]
    for c in copies:
        c.start()
    for c in copies:
        c.wait()
    lr = lr_ref[...]
    moe = jnp.zeros((sb * l, D_MODEL), F32)
    for c in range(LOCAL_ROWS // MASK_ROWS):
        rows = loc_ref[c * MASK_ROWS:(c + 1) * MASK_ROWS, :]
        moe = moe + lax.dot_general(_sort_matrix(lr, c), rows, (((0,), (0,)), ((), ())),
                                    preferred_element_type=F32)
    x2 = x1 + gt_ref[...] * moe.reshape(sb, l, D_MODEL)
    out_ref[...] = x2 * lax.rsqrt(jnp.mean(x2 * x2, axis=-1, keepdims=True) + EPS) * gf_ref[...]


def _combine(gdst, lr, y, x1, gt, g_final, sb, l, seq_div, blk_off):
    n3 = x1.shape[0]
    nblk = n3 // sb
    tm = sb * l
    assert tm == TOK_TILE
    return pl.pallas_call(
        _combine_kernel,
        out_shape=jax.ShapeDtypeStruct(x1.shape, F32),
        grid=(nblk,),
        in_specs=[pl.BlockSpec((1, 1, GDST_LANES), lambda i: (i + blk_off, 0, 0), memory_space=pltpu.SMEM),
                  pl.BlockSpec((TOP_K, tm), lambda i: (0, i + blk_off)),
                  pl.BlockSpec(memory_space=pl.ANY),
                  pl.BlockSpec((sb, l, D_MODEL), lambda i: (i, 0, 0)),
                  pl.BlockSpec((sb, 1, D_MODEL), lambda i: (i // seq_div, 0, 0)),
                  pl.BlockSpec((1, D_MODEL), lambda i: (0, 0))],
        out_specs=pl.BlockSpec((sb, l, D_MODEL), lambda i: (i, 0, 0)),
        scratch_shapes=[pltpu.VMEM((LOCAL_ROWS, D_MODEL), BF16), pltpu.SemaphoreType.DMA],
        compiler_params=pltpu.CompilerParams(dimension_semantics=("arbitrary",), vmem_limit_bytes=VMEM_LIMIT),
        name="combine",
    )(gdst, lr, y, x1, gt, g_final)


def kernel(x_prompt, x_sample, c_prompt, c_sample, state_ssm, state_conv, w_ada, b_ada, g_mix, w_in, g_v_a, w_spatial, b_spatial, g_out_a, conv_w, conv_b, dt_bias, a_log, d_skip, g_out_b, w_out, g_ffn, w_router, b_router, w_gate, b_gate, w_up, b_up, w_down, b_down, g_final):
    assert w_ada.shape[0] == 1, "single-layer step"
    p = dict(w_in=w_in[0], g_mix=g_mix[0], g_v_a=g_v_a[0], w_spatial=w_spatial[0], b_spatial=b_spatial[0],
             g_out_a=g_out_a[0], conv_w=conv_w[0], conv_b=conv_b[0], dt_bias=dt_bias[0], a_log=a_log[0],
             d_skip=d_skip[0], g_out_b=g_out_b[0])
    bp, lp, _ = x_prompt.shape
    bs, ls, _ = x_sample.shape
    tp, ts = bp * lp, bs * ls

    ada = _ada(jnp.concatenate([c_prompt, c_sample], axis=0), w_ada[0], b_ada[0][None, :])
    ada = ada.reshape(bp + bs, 6, 1, D_MODEL)
    ada_p = [ada[:bp, j] for j in range(6)]
    ada_s = [ada[bp:, j] for j in range(6)]

    fw = _front_weights(p)
    mixed_p, conv_p, ssm_p = _prompt_mixer(x_prompt, ada_p[0], ada_p[1], fw, p)
    mixed_s, v_s, conv_s, ssm_s = _sample_mixer(x_sample, ada_s[0], ada_s[1], state_ssm[0], state_conv[0], fw, p)

    w_out_b = w_out[0].astype(BF16)
    g_ffn2 = g_ffn[0][None, :]
    wr_t = w_router[0].T
    wr_hi = wr_t.astype(BF16)
    wr_lo = (wr_t - wr_hi.astype(F32)).astype(BF16)
    br = b_router[0][:, None]
    tps = lp // TOK_TILE
    sbs = TOK_TILE // ls
    xp3 = x_prompt.reshape(bp * tps, TOK_TILE, D_MODEL)
    x1_p, h2p_p, ids_p = _post(mixed_p, xp3, ada_p[2], ada_p[4], ada_p[3], w_out_b, g_ffn2, wr_hi, wr_lo, br,
                               1, TOK_TILE, tps)
    x1_s, h2p_s, ids_s = _post(mixed_s, x_sample, ada_s[2], ada_s[4], ada_s[3], w_out_b, g_ffn2, wr_hi, wr_lo,
                               br, sbs, ls, 1)

    n_blocks = (tp + ts) // TOK_TILE
    max_groups = (tp + ts) * TOP_K // ROW_GROUP + n_blocks * N_EXPERTS + N_EXPERTS * (TILE_GROUPS - 1)
    n_tiles = -(-max_groups // TILE_GROUPS)
    n_tiles_all = n_tiles + LOCAL_GROUPS // TILE_GROUPS
    lr, gdst, tile_start = _plan(jnp.concatenate([ids_p, ids_s], axis=1), n_tiles * TILE_GROUPS)

    xs = jnp.zeros((n_tiles_all * ROW_TILE, ROW_W), BF16)
    xs = _dispatch(gdst, lr, h2p_p, h2p_s, xs)
    y = _experts(tile_start[0], xs, w_gate[0], b_gate[0], w_up[0], b_up[0], w_down[0], b_down[0])

    gf = g_final[None, :]
    y_p = _combine(gdst, lr, y, x1_p, ada_p[5], gf, 1, TOK_TILE, tps, 0).reshape(bp, lp, D_MODEL)
    y_s = _combine(gdst, lr, y, x1_s, ada_s[5], gf, sbs, ls, 1, tp // TOK_TILE)

    return (y_p, y_s, ssm_p[None], conv_p[None], ssm_s[None], conv_s[None], v_s.reshape(1, bs, ls, A_WIDTH))
```

```python
import functools
import math

import numpy as np
import jax
import jax.numpy as jnp
from jax import lax
from jax.experimental import pallas as pl
from jax.experimental.pallas import tpu as pltpu

F32 = jnp.float32
BF16 = jnp.bfloat16
I32 = jnp.int32

D_MODEL = 1024
A_WIDTH = 512
A_HEADS = 4
A_HEAD_DIM = 128
CHUNK = 128
B_WIDTH = 512
SSD_HEAD_DIM = 64
SSD_HEADS = 8
SSD_GROUPS = 2
SSD_STATE = 128
GROUP_W = B_WIDTH // SSD_GROUPS
CONV_K = 4
CONV_DIM = 1024
CONV_PAD = 8
N_EXPERTS = 32
TOP_K = 4
SWIGLU_LIMIT = 7.0
SWIGLU_ALPHA = 1.702
EPS = 1e-6
LANES = 128

TOK_TILE = 512
SAMPLE_SEQ_TILE = 16
ROW_TILE = 256
ROW_GROUP = 16
TILE_GROUPS = ROW_TILE // ROW_GROUP
LOCAL_GROUPS = TOK_TILE * TOP_K // ROW_GROUP + N_EXPERTS
LOCAL_ROWS = LOCAL_GROUPS * ROW_GROUP
GDST_LANES = 256
ROW_W = D_MODEL + LANES
INFO_ROWS = 16
MASK_ROWS = 256
VMEM_LIMIT = 56 * 1024 * 1024


def _dot(a, b):
    return jnp.dot(a, b, preferred_element_type=F32)


def _dot_nt(a, b):
    return lax.dot_general(a, b, (((1,), (1,)), ((), ())), preferred_element_type=F32)


def _split(x):
    hi = x.astype(BF16)
    lo = (x - hi.astype(F32)).astype(BF16)
    return hi, lo


def _dot_exact_l(t, x):
    hi, lo = _split(x)
    return _dot(t, hi) + _dot(t, lo)


def _dot_exact_r(x, t):
    hi, lo = _split(x)
    return _dot(hi, t) + _dot(lo, t)


def _silu(x):
    return x * jax.nn.sigmoid(x)


def _gelu(x):
    return 0.5 * x * (1.0 + lax.erf(x * (1.0 / math.sqrt(2.0))))


def _softplus(x):
    return jnp.maximum(x, 0.0) + jnp.log1p(jnp.exp(-jnp.abs(x)))


def _rms(x, g):
    return x * lax.rsqrt(jnp.mean(x * x, axis=-1, keepdims=True) + EPS) * g


def _ada_kernel(c_ref, w_ref, b_ref, o_ref):
    s_hi, s_lo = _split(_silu(c_ref[...]))
    w_hi, w_lo = _split(w_ref[...])
    o_ref[...] = _dot(s_hi, w_hi) + _dot(s_lo, w_hi) + _dot(s_hi, w_lo) + b_ref[...]


def _ada(c_all, w_ada, b_ada):
    m = c_all.shape[0]
    n = w_ada.shape[1]
    bn = 512
    return pl.pallas_call(
        _ada_kernel,
        out_shape=jax.ShapeDtypeStruct((m, n), F32),
        grid=(n // bn,),
        in_specs=[pl.BlockSpec((m, D_MODEL), lambda j: (0, 0)),
                  pl.BlockSpec((D_MODEL, bn), lambda j: (0, j)),
                  pl.BlockSpec((1, bn), lambda j: (0, j))],
        out_specs=pl.BlockSpec((m, bn), lambda j: (0, j)),
        compiler_params=pltpu.CompilerParams(dimension_semantics=("arbitrary",), vmem_limit_bytes=VMEM_LIMIT),
        name="ada",
    )(c_all, w_ada, b_ada)


def _mixer_front(x3, sh, sc, prev, refs, xp_ref):
    (g_mix, w_uvz, w_xbc, w_dt, w_dtt, wbd, bias_sp, g_v, g_oa, conv_w, conv_b, dt_bias, dt_bias_t, a_row, a_col) = refs
    sb, l, _ = x3.shape
    tm = sb * l
    xn = x3 * lax.rsqrt(jnp.mean(x3 * x3, axis=-1, keepdims=True) + EPS) * g_mix[...]
    h = (xn * (1.0 + sc) + sh).reshape(tm, D_MODEL)
    hb = h.astype(BF16)
    uvz = _dot(hb, w_uvz[...])
    xbc = _dot(hb, w_xbc[...])
    dt_raw = _dot(hb, w_dt[...])
    dtt_raw = _dot_nt(w_dtt[...], hb)

    u = _gelu(uvz[:, :A_WIDTH])
    vg = _gelu(uvz[:, A_WIDTH:2 * A_WIDTH])
    z = uvz[:, 2 * A_WIDTH:]
    v_parts, s_parts = [], []
    for hd in range(A_HEADS):
        sl = slice(hd * A_HEAD_DIM, (hd + 1) * A_HEAD_DIM)
        vh = _rms(vg[:, sl], g_v[:, sl])
        v_parts.append(vh)
        s_parts.append(_dot(wbd[hd], vh.astype(BF16)))
    v = jnp.concatenate(v_parts, axis=1)
    s_a = jnp.concatenate(s_parts, axis=1) + bias_sp[...]
    out_a = _rms(u * s_a, g_oa[...])

    xp_ref[:, 0:CONV_PAD, :] = prev
    xp_ref[:, CONV_PAD:, :] = xbc.reshape(sb, l, CONV_DIM)
    acc = conv_b[...]
    for k in range(CONV_K):
        off = CONV_PAD - (CONV_K - 1) + k
        acc = acc + xp_ref[:, off:off + l, :] * conv_w[k:k + 1, :]
    xc = _silu(acc).reshape(tm, CONV_DIM)
    dt = _softplus(dt_raw + dt_bias[...])
    dtt = _softplus(dtt_raw + dt_bias_t[...])
    d_a = dt * a_row[...]
    d_at = dtt * a_col[...]
    return out_a, v, z, xc, dt, d_a, d_at


def _ssd_chunk(xs, bm, cm, dt, d_a, d_at, cref):
    tril, triu, ones, expand, mask = cref
    cs = _dot_exact_l(tril[...], d_a)
    cs_t = _dot_exact_r(d_at, triu[...])
    cs_tot = _dot_exact_l(ones[...], d_a)
    vals = jnp.concatenate([dt, jnp.exp(cs_tot - cs), jnp.exp(cs)], axis=0)
    vals_e = _dot_exact_r(vals, expand[...])
    n = xs.shape[0]
    dt_e, dte_e, e_e = vals_e[:n], vals_e[n:2 * n], vals_e[2 * n:]
    xdt = xs * dt_e
    xdtd = xdt * dte_e
    msk = mask[...] > 0.5
    row_lt_half = lax.broadcasted_iota(I32, (2 * n, LANES), 0) < n
    lane_lt_half = lax.broadcasted_iota(I32, (2 * n, LANES), 1) < SSD_HEAD_DIM
    y_parts = []
    for g in range(SSD_GROUPS):
        cb = _dot_nt(cm[:, g * SSD_STATE:(g + 1) * SSD_STATE].astype(BF16),
                     bm[:, g * SSD_STATE:(g + 1) * SSD_STATE].astype(BF16))
        for hp in range(SSD_HEADS // SSD_GROUPS // 2):
            h0 = g * (SSD_HEADS // SSD_GROUPS) + 2 * hp
            ms = []
            for hh in (h0, h0 + 1):
                diff = cs[:, hh:hh + 1] - cs_t[hh:hh + 1, :]
                ms.append((cb * jnp.where(msk, jnp.exp(jnp.where(msk, diff, 0.0)), 0.0)).astype(BF16))
            pair = xdt[:, h0 * SSD_HEAD_DIM:(h0 + 2) * SSD_HEAD_DIM]
            rhs = jnp.where(row_lt_half == lane_lt_half, jnp.concatenate([pair, pair], axis=0), 0.0).astype(BF16)
            y_parts.append(_dot(jnp.concatenate(ms, axis=1), rhs))
    y_diag = jnp.concatenate(y_parts, axis=1)
    return y_diag, e_e, xdtd, cs_tot


def _mixer_back(y, xs, z, out_a, dskip_e, g_ob):
    y = y + xs * dskip_e
    gated = y * _silu(z)
    parts = [_rms(gated[:, g * GROUP_W:(g + 1) * GROUP_W], g_ob[:, g * GROUP_W:(g + 1) * GROUP_W])
             for g in range(SSD_GROUPS)]
    return jnp.concatenate([out_a] + parts, axis=1).astype(BF16)


N_FRONT = 15
N_SSD = 5


def _prompt_mixer_kernel(tiles_per_seq, x_ref, sh_ref, sc_ref, *rest):
    front = rest[:N_FRONT]
    cref = rest[N_FRONT:N_FRONT + N_SSD]
    dskip_e, g_ob = rest[N_FRONT + N_SSD:N_FRONT + N_SSD + 2]
    mixed_ref, conv_out_ref, ssm_out_ref = rest[N_FRONT + N_SSD + 2:N_FRONT + N_SSD + 5]
    xp_ref, carry_ref, st_ref = rest[N_FRONT + N_SSD + 5:]
    i = pl.program_id(0)
    first = (i % tiles_per_seq) == 0

    @pl.when(first)
    def _():
        carry_ref[...] = jnp.zeros_like(carry_ref)
        st_ref[...] = jnp.zeros_like(st_ref)

    x3 = x_ref[...]
    l = x3.shape[1]
    out_a, _, z, xc, dt, d_a, d_at = _mixer_front(x3, sh_ref[...], sc_ref[...], carry_ref[...], front, xp_ref)
    carry_ref[...] = xp_ref[:, l:l + CONV_PAD, :]
    xs = xc[:, :B_WIDTH]
    y_rows = []
    for c in range(l // CHUNK):
        r = slice(c * CHUNK, (c + 1) * CHUNK)
        bm = xc[r, B_WIDTH:B_WIDTH + SSD_GROUPS * SSD_STATE]
        cm = xc[r, B_WIDTH + SSD_GROUPS * SSD_STATE:]
        y_diag, e_e, xdtd, _ = _ssd_chunk(xs[r], bm, cm, dt[r], d_a[r], d_at[:, r], cref)
        st = st_ref[...]
        y_off, upd = [], []
        for g in range(SSD_GROUPS):
            gs = slice(g * GROUP_W, (g + 1) * GROUP_W)
            ns = slice(g * SSD_STATE, (g + 1) * SSD_STATE)
            y_off.append(_dot(cm[:, ns].astype(BF16), st[:, gs].astype(BF16)))
            upd.append(_dot(bm[:, ns].T.astype(BF16), xdtd[:, gs].astype(BF16)))
        y_rows.append(y_diag + jnp.concatenate(y_off, axis=1) * e_e)
        st_ref[...] = st * e_e[CHUNK - 1:CHUNK, :] + jnp.concatenate(upd, axis=1)
    y = jnp.concatenate(y_rows, axis=0)
    mixed_ref[...] = _mixer_back(y, xs, z, out_a, dskip_e[...], g_ob[...])

    @pl.when((i % tiles_per_seq) == tiles_per_seq - 1)
    def _():
        conv_out_ref[...] = xp_ref[:, l + CONV_PAD - (CONV_K - 1):l + CONV_PAD, :]
        ssm_out_ref[0] = st_ref[...].T


def _sample_mixer_kernel(x_ref, sh_ref, sc_ref, prev_ref, ssm0_ref, *rest):
    front = rest[:N_FRONT]
    cref = rest[N_FRONT:N_FRONT + N_SSD]
    dskip_e, g_ob, selseq = rest[N_FRONT + N_SSD:N_FRONT + N_SSD + 3]
    mixed_ref, v_ref, conv_out_ref, ssm_out_ref = rest[N_FRONT + N_SSD + 3:N_FRONT + N_SSD + 7]
    xp_ref, yoff_ref, cbf_ref, bbf_ref, t1_ref, dtab_ref = rest[N_FRONT + N_SSD + 7:]
    x3 = x_ref[...]
    sb, l, _ = x3.shape
    tm = sb * l
    out_a, v, z, xc, dt, d_a, d_at = _mixer_front(x3, sh_ref[...], sc_ref[...], prev_ref[...], front, xp_ref)
    v_ref[...] = v
    conv_out_ref[...] = xp_ref[:, l + CONV_PAD - (CONV_K - 1):l + CONV_PAD, :]
    xs = xc[:, :B_WIDTH]
    bm = xc[:, B_WIDTH:B_WIDTH + SSD_GROUPS * SSD_STATE]
    cm = xc[:, B_WIDTH + SSD_GROUPS * SSD_STATE:]
    y_diag, e_e, xdtd, _ = _ssd_chunk(xs, bm, cm, dt, d_a, d_at, cref)

    e_tot = jnp.exp(_dot_exact_l(selseq[...], d_a))
    for hh in range(SSD_HEADS):
        dtab_ref[hh] = jnp.broadcast_to(e_tot[:, hh:hh + 1], (sb, LANES))
    cbf_ref[...] = cm
    bbf_ref[...] = bm
    for g in range(SSD_GROUPS):
        t1_ref[g] = xdtd[:, g * GROUP_W:(g + 1) * GROUP_W].T.astype(BF16)
    seq_of_row = lax.broadcasted_iota(I32, (tm, SSD_STATE), 0) // l
    heads_per_group = SSD_HEADS // SSD_GROUPS

    def body(j, carry):
        r0 = pl.multiple_of(j * l, l)
        s0 = ssm0_ref[j]
        for g in range(SSD_GROUPS):
            ns = slice(g * SSD_STATE, (g + 1) * SSD_STATE)
            s0g = s0[g * heads_per_group:(g + 1) * heads_per_group].reshape(GROUP_W, SSD_STATE)
            cj = cbf_ref[pl.ds(r0, l), ns].astype(BF16)
            yoff_ref[pl.ds(r0, l), g * GROUP_W:(g + 1) * GROUP_W] = _dot_nt(cj, s0g.astype(BF16))
            bmask = jnp.where(seq_of_row == j, bbf_ref[:, ns], 0.0).astype(BF16)
            upd = _dot(t1_ref[g], bmask)
            for hq in range(heads_per_group):
                hh = g * heads_per_group + hq
                dec = dtab_ref[hh, pl.ds(j, 1), :]
                ssm_out_ref[j, hh] = s0[hh] * dec + upd[hq * SSD_HEAD_DIM:(hq + 1) * SSD_HEAD_DIM]
        return carry

    lax.fori_loop(0, sb, body, 0)
    y = y_diag + yoff_ref[...] * e_e
    mixed_ref[...] = _mixer_back(y, xs, z, out_a, dskip_e[...], g_ob[...])


def _const_spec(a):
    nd = a.ndim
    return pl.BlockSpec(a.shape, lambda i, _nd=nd: (0,) * _nd)


def _spatial_consts(w_spatial, b_spatial, cl, tm):
    w = jnp.where(jnp.tril(jnp.ones((cl, cl), bool)), w_spatial[:, :cl, :cl], 0.0)
    eye = jnp.eye(tm // cl, dtype=F32)
    wbd = jnp.einsum("ab,hts->hatbs", eye, w).reshape(A_HEADS, tm, tm).astype(BF16)
    bias = jnp.tile(jnp.repeat(b_spatial[:, :cl].T, A_HEAD_DIM, axis=1), (tm // cl, 1))
    return wbd, bias


def _ssd_consts(cl):
    r = np.arange(CHUNK)
    same = (r[:, None] // cl) == (r[None, :] // cl)
    tril = same & (r[:, None] >= r[None, :])
    expand = np.zeros((LANES, B_WIDTH), np.float32)
    for hh in range(SSD_HEADS):
        expand[hh, hh * SSD_HEAD_DIM:(hh + 1) * SSD_HEAD_DIM] = 1.0
    return (jnp.asarray(tril, BF16), jnp.asarray(tril.T, BF16), jnp.asarray(same, BF16),
            jnp.asarray(expand, BF16), jnp.asarray(tril, F32))


def _front_weights(p):
    w_in = p["w_in"]
    c0, c1 = 3 * A_WIDTH, 3 * A_WIDTH + CONV_DIM
    w_dt = w_in[:, c1:]
    pad8 = lambda v: jnp.pad(v, (0, LANES - SSD_HEADS))
    a = -jnp.exp(p["a_log"])
    return dict(
        g_mix=p["g_mix"][None, :],
        w_uvz=w_in[:, :c0].astype(BF16),
        w_xbc=w_in[:, c0:c1].astype(BF16),
        w_dt=jnp.pad(w_dt, ((0, 0), (0, LANES - SSD_HEADS))).astype(BF16),
        w_dtt=w_dt.T.astype(BF16),
        g_v=p["g_v_a"][None, :], g_oa=p["g_out_a"][None, :],
        conv_w=p["conv_w"], conv_b=p["conv_b"][None, :],
        dt_bias=pad8(p["dt_bias"])[None, :], dt_bias_t=p["dt_bias"][:, None],
        a_row=pad8(a)[None, :], a_col=a[:, None],
        dskip_e=jnp.repeat(p["d_skip"], SSD_HEAD_DIM)[None, :],
        g_ob=p["g_out_b"][None, :],
    )


def _front_list(fw, wbd, bias_sp):
    return [fw["g_mix"], fw["w_uvz"], fw["w_xbc"], fw["w_dt"], fw["w_dtt"], wbd, bias_sp, fw["g_v"], fw["g_oa"],
            fw["conv_w"], fw["conv_b"], fw["dt_bias"], fw["dt_bias_t"], fw["a_row"], fw["a_col"]]


def _prompt_mixer(x, sh, sc, fw, p):
    nseq, lseq, _ = x.shape
    tps = lseq // TOK_TILE
    nt = nseq * tps
    x4 = x.reshape(nt, TOK_TILE, D_MODEL)
    wbd, bias_sp = _spatial_consts(p["w_spatial"], p["b_spatial"], CHUNK, TOK_TILE)
    consts = _front_list(fw, wbd, bias_sp) + list(_ssd_consts(CHUNK)) + [fw["dskip_e"], fw["g_ob"]]
    seq_spec = pl.BlockSpec((1, 1, D_MODEL), lambda i: (i // tps, 0, 0))
    mixed, conv_new, ssm_new = pl.pallas_call(
        functools.partial(_prompt_mixer_kernel, tps),
        out_shape=(jax.ShapeDtypeStruct((nt * TOK_TILE, D_MODEL), BF16),
                   jax.ShapeDtypeStruct((nseq, CONV_K - 1, CONV_DIM), F32),
                   jax.ShapeDtypeStruct((nseq, B_WIDTH, SSD_STATE), F32)),
        grid=(nt,),
        in_specs=[pl.BlockSpec((1, TOK_TILE, D_MODEL), lambda i: (i, 0, 0)), seq_spec, seq_spec]
                 + [_const_spec(a) for a in consts],
        out_specs=(pl.BlockSpec((TOK_TILE, D_MODEL), lambda i: (i, 0)),
                   pl.BlockSpec((1, CONV_K - 1, CONV_DIM), lambda i: (i // tps, 0, 0)),
                   pl.BlockSpec((1, B_WIDTH, SSD_STATE), lambda i: (i // tps, 0, 0))),
        scratch_shapes=[pltpu.VMEM((1, TOK_TILE + CONV_PAD, CONV_DIM), F32),
                        pltpu.VMEM((1, CONV_PAD, CONV_DIM), F32),
                        pltpu.VMEM((SSD_STATE, B_WIDTH), F32)],
        compiler_params=pltpu.CompilerParams(dimension_semantics=("arbitrary",), vmem_limit_bytes=VMEM_LIMIT),
        name="prompt_mixer",
    )(x4, sh, sc, *consts)
    return mixed, conv_new, ssm_new.reshape(nseq, SSD_HEADS, SSD_HEAD_DIM, SSD_STATE)


def _sample_mixer(x, sh, sc, state_ssm, state_conv, fw, p):
    nseq, l, _ = x.shape
    sb = SAMPLE_SEQ_TILE
    tm = sb * l
    assert tm == CHUNK
    wbd, bias_sp = _spatial_consts(p["w_spatial"], p["b_spatial"], l, tm)
    selseq = jnp.asarray((np.arange(tm)[None, :] // l) == np.arange(sb)[:, None], BF16)
    consts = _front_list(fw, wbd, bias_sp) + list(_ssd_consts(l)) + [fw["dskip_e"], fw["g_ob"], selseq]
    prev = jnp.pad(state_conv, ((0, 0), (CONV_PAD - (CONV_K - 1), 0), (0, 0)))
    seq_spec = pl.BlockSpec((sb, 1, D_MODEL), lambda i: (i, 0, 0))
    ssm_spec = pl.BlockSpec((sb, SSD_HEADS, SSD_HEAD_DIM, SSD_STATE), lambda i: (i, 0, 0, 0))
    return pl.pallas_call(
        _sample_mixer_kernel,
        out_shape=(jax.ShapeDtypeStruct((nseq * l, D_MODEL), BF16),
                   jax.ShapeDtypeStruct((nseq * l, A_WIDTH), F32),
                   jax.ShapeDtypeStruct((nseq, CONV_K - 1, CONV_DIM), F32),
                   jax.ShapeDtypeStruct(state_ssm.shape, F32)),
        grid=(nseq // sb,),
        in_specs=[pl.BlockSpec((sb, l, D_MODEL), lambda i: (i, 0, 0)), seq_spec, seq_spec,
                  pl.BlockSpec((sb, CONV_PAD, CONV_DIM), lambda i: (i, 0, 0)), ssm_spec]
                 + [_const_spec(a) for a in consts],
        out_specs=(pl.BlockSpec((tm, D_MODEL), lambda i: (i, 0)),
                   pl.BlockSpec((tm, A_WIDTH), lambda i: (i, 0)),
                   pl.BlockSpec((sb, CONV_K - 1, CONV_DIM), lambda i: (i, 0, 0)),
                   ssm_spec),
        scratch_shapes=[pltpu.VMEM((sb, l + CONV_PAD, CONV_DIM), F32),
                        pltpu.VMEM((tm, B_WIDTH), F32),
                        pltpu.VMEM((tm, SSD_GROUPS * SSD_STATE), F32),
                        pltpu.VMEM((tm, SSD_GROUPS * SSD_STATE), F32),
                        pltpu.VMEM((SSD_GROUPS, GROUP_W, tm), BF16),
                        pltpu.VMEM((SSD_HEADS, sb, LANES), F32)],
        compiler_params=pltpu.CompilerParams(dimension_semantics=("arbitrary",), vmem_limit_bytes=VMEM_LIMIT),
        name="sample_mixer",
    )(x, sh, sc, prev, state_ssm, *consts)


def _post_kernel(mixed_ref, x_ref, gt_ref, sc_ref, sh_ref, w_out_ref, g_ffn_ref, wr_hi_ref, wr_lo_ref, br_ref,
                 x1_ref, h2p_ref, ids_ref):
    x3 = x_ref[...]
    sb, l, _ = x3.shape
    tm = sb * l
    mix = _dot(mixed_ref[...], w_out_ref[...]).reshape(sb, l, D_MODEL)
    x1 = x3 + gt_ref[...] * mix
    x1_ref[...] = x1
    xn = x1 * lax.rsqrt(jnp.mean(x1 * x1, axis=-1, keepdims=True) + EPS) * g_ffn_ref[...]
    h2 = (xn * (1.0 + sc_ref[...]) + sh_ref[...]).reshape(tm, D_MODEL)
    h_hi, h_lo = _split(h2)
    h2p_ref[:, :D_MODEL] = h_hi
    logits = (_dot_nt(wr_hi_ref[...], h_hi) + _dot_nt(wr_hi_ref[...], h_lo) + _dot_nt(wr_lo_ref[...], h_hi)
              + br_ref[...])
    e_iota = lax.broadcasted_iota(I32, logits.shape, 0)
    vals, idxs = [], []
    for _ in range(TOP_K):
        m = jnp.max(logits, axis=0, keepdims=True)
        idx = jnp.min(jnp.where(logits == m, e_iota, N_EXPERTS), axis=0, keepdims=True)
        vals.append(m)
        idxs.append(idx)
        logits = jnp.where(e_iota == idx, -jnp.inf, logits)
    ex = [jnp.exp(v - vals[0]) for v in vals]
    tot = ex[0] + ex[1] + ex[2] + ex[3]
    ids = jnp.concatenate(idxs, axis=0)
    ids_ref[...] = ids
    wts = jnp.concatenate([e / tot for e in ex], axis=0)
    w_hi = wts.astype(BF16).astype(F32)
    info = jnp.concatenate([ids.astype(F32), w_hi, wts - w_hi, jnp.zeros((TOP_K, tm), F32)], axis=0).astype(BF16)
    r = lax.broadcasted_iota(I32, (INFO_ROWS, LANES), 0)
    c = lax.broadcasted_iota(I32, (INFO_ROWS, LANES), 1)
    place = jnp.where(r == c, 1.0, 0.0).astype(BF16)
    h2p_ref[:, D_MODEL:] = lax.dot_general(info, place, (((0,), (0,)), ((), ())),
                                           preferred_element_type=F32).astype(BF16)


def _post(mixed, x, gt, sc, sh, w_out_b, g_ffn, wr_hi, wr_lo, br, sb, l, seq_div):
    n3, _, _ = x.shape
    nblk = n3 // sb
    tm = sb * l
    t = n3 * l
    ada_spec = pl.BlockSpec((sb, 1, D_MODEL), lambda i: (i // seq_div, 0, 0))
    consts = [w_out_b, g_ffn, wr_hi, wr_lo, br]
    return pl.pallas_call(
        _post_kernel,
        out_shape=(jax.ShapeDtypeStruct(x.shape, F32),
                   jax.ShapeDtypeStruct((t, ROW_W), BF16),
                   jax.ShapeDtypeStruct((TOP_K, t), I32)),
        grid=(nblk,),
        in_specs=[pl.BlockSpec((tm, D_MODEL), lambda i: (i, 0)),
                  pl.BlockSpec((sb, l, D_MODEL), lambda i: (i, 0, 0)), ada_spec, ada_spec, ada_spec]
                 + [_const_spec(a) for a in consts],
        out_specs=(pl.BlockSpec((sb, l, D_MODEL), lambda i: (i, 0, 0)),
                   pl.BlockSpec((tm, ROW_W), lambda i: (i, 0)),
                   pl.BlockSpec((TOP_K, tm), lambda i: (0, i))),
        compiler_params=pltpu.CompilerParams(dimension_semantics=("arbitrary",), vmem_limit_bytes=VMEM_LIMIT),
        name="post",
    )(mixed, x, gt, sc, sh, *consts)


def _strict_upper(n):
    r = lax.broadcasted_iota(I32, (n, n), 0)
    c = lax.broadcasted_iota(I32, (n, n), 1)
    return jnp.where(r < c, 1.0, 0.0).astype(BF16)


def _expert_prefix(col):
    r = lax.broadcasted_iota(I32, (N_EXPERTS, N_EXPERTS), 0)
    c = lax.broadcasted_iota(I32, (N_EXPERTS, N_EXPERTS), 1)
    as_row = jnp.sum(jnp.where(r == c, col, 0.0), axis=0, keepdims=True)
    return jnp.sum(jnp.where(c < r, as_row, 0.0), axis=1, keepdims=True)


def _plan_kernel(dump_group, ids_ref, lr_ref, gdst_ref, tile_e_ref, seg_ref, gb_ref):
    ph = pl.program_id(0)
    b = pl.program_id(1)
    ids = ids_ref[...]
    tm = ids.shape[1]
    e_iota = lax.broadcasted_iota(I32, (N_EXPERTS, tm), 0)
    onehot = [ids[k:k + 1, :] == e_iota for k in range(TOP_K)]
    sel = (onehot[0] | onehot[1]) | (onehot[2] | onehot[3])
    m = jnp.where(sel, 1.0, 0.0)
    rowsum = jnp.sum(m, axis=1, keepdims=True)
    blk_lane = lax.broadcasted_iota(I32, (N_EXPERTS, LANES), 1)

    @pl.when((ph == 0) & (b == 0))
    def _():
        seg_ref[...] = jnp.zeros_like(seg_ref)

    @pl.when(ph == 0)
    def _():
        seg = jnp.ceil(rowsum * (1.0 / ROW_GROUP))
        seg_ref[...] = jnp.where(blk_lane == b, seg, seg_ref[...])

    @pl.when((ph == 1) & (b == 0))
    def _():
        seg = seg_ref[...]
        tot = jnp.sum(seg, axis=1, keepdims=True)
        padded = jnp.ceil(tot * (1.0 / TILE_GROUPS)) * TILE_GROUPS
        gstart = _expert_prefix(padded)
        gb_ref[...] = gstart + _dot(seg.astype(BF16), _strict_upper(LANES))
        r = lax.broadcasted_iota(I32, (N_EXPERTS, LANES), 0)
        c = lax.broadcasted_iota(I32, (N_EXPERTS, LANES), 1)
        starts = jnp.sum(jnp.where(r == c, gstart, 0.0), axis=0, keepdims=True)
        n_used = jnp.sum(padded, axis=0, keepdims=True)
        lane = lax.broadcasted_iota(I32, (1, LANES), 1)
        tile_e_ref[...] = (jnp.where(lane == N_EXPERTS, n_used, starts) * (1.0 / TILE_GROUPS)).astype(I32)

    @pl.when(ph == 1)
    def _():
        seg_b = jnp.sum(jnp.where(blk_lane == b, seg_ref[...], 0.0), axis=1, keepdims=True)
        gb_b = jnp.sum(jnp.where(blk_lane == b, gb_ref[...], 0.0), axis=1, keepdims=True)
        loc_b = _expert_prefix(seg_b)
        before = _dot(m.astype(BF16), _strict_upper(tm)) + loc_b * ROW_GROUP
        lr_ref[...] = jnp.concatenate(
            [jnp.sum(jnp.where(onehot[k], before, 0.0), axis=0, keepdims=True) for k in range(TOP_K)],
            axis=0).astype(I32)
        g = lax.broadcasted_iota(I32, (N_EXPERTS, GDST_LANES), 1).astype(F32)
        inside = (loc_b <= g) & (g < loc_b + seg_b)
        dst = jnp.sum(jnp.where(inside, gb_b + g - loc_b, 0.0), axis=0, keepdims=True)
        used = jnp.sum(jnp.where(inside, 1.0, 0.0), axis=0, keepdims=True) > 0.5
        gdst_ref[0] = jnp.where(used, dst, dump_group + g[0:1, :]).astype(I32)


def _plan(ids, dump_group):
    t = ids.shape[1]
    tm = TOK_TILE
    nb = t // tm
    assert nb <= LANES
    return pl.pallas_call(
        functools.partial(_plan_kernel, float(dump_group)),
        out_shape=(jax.ShapeDtypeStruct((TOP_K, t), I32),
                   jax.ShapeDtypeStruct((nb, 1, GDST_LANES), I32),
                   jax.ShapeDtypeStruct((1, LANES), I32)),
        grid=(2, nb),
        in_specs=[pl.BlockSpec((TOP_K, tm), lambda ph, b: (0, b))],
        out_specs=(pl.BlockSpec((TOP_K, tm), lambda ph, b: (0, b * ph)),
                   pl.BlockSpec((1, 1, GDST_LANES), lambda ph, b: (b * ph, 0, 0)),
                   pl.BlockSpec((1, LANES), lambda ph, b: (0, 0))),
        scratch_shapes=[pltpu.VMEM((N_EXPERTS, LANES), F32), pltpu.VMEM((N_EXPERTS, LANES), F32)],
        compiler_params=pltpu.CompilerParams(dimension_semantics=("arbitrary", "arbitrary")),
        name="plan",
    )(ids)


def _sort_matrix(lr, c):
    r_iota = lax.broadcasted_iota(I32, (MASK_ROWS, lr.shape[1]), 0) + c * MASK_ROWS
    p = jnp.where(r_iota == lr[TOP_K - 1:TOP_K, :], 1.0, 0.0)
    for k in range(TOP_K - 1):
        p = jnp.where(r_iota == lr[k:k + 1, :], 1.0, p)
    return p.astype(BF16)


def _group_copy(loc_ref, far_ref, gdst_ref, g, sem, to_far):
    dst = pl.multiple_of(gdst_ref[0, 0, g] * ROW_GROUP, ROW_GROUP)
    near = loc_ref.at[pl.ds(g * ROW_GROUP, ROW_GROUP)]
    far = far_ref.at[pl.ds(dst, ROW_GROUP)]
    return pltpu.make_async_copy(near, far, sem) if to_far else pltpu.make_async_copy(far, near, sem)


def _dispatch_kernel(n_first, gdst_ref, lr_ref, ha_ref, hb_ref, xs_in_ref, xs_ref, h_ref, loc_ref, sem):
    del xs_in_ref
    i = pl.program_id(0)

    @pl.when(i < n_first)
    def _():
        h_ref[...] = ha_ref[...]

    @pl.when(i >= n_first)
    def _():
        h_ref[...] = hb_ref[...]

    lr = lr_ref[...]
    for c in range(LOCAL_ROWS // MASK_ROWS):
        loc_ref[c * MASK_ROWS:(c + 1) * MASK_ROWS, :] = _dot(_sort_matrix(lr, c), h_ref[...]).astype(BF16)
    copies = [_group_copy(loc_ref, xs_ref, gdst_ref, g, sem, True) for g in range(LOCAL_GROUPS)]
    for c in copies:
        c.start()
    for c in copies:
        c.wait()


def _dispatch(gdst, lr, h_a, h_b, xs):
    tm = TOK_TILE
    na, nb2 = h_a.shape[0] // tm, h_b.shape[0] // tm
    return pl.pallas_call(
        functools.partial(_dispatch_kernel, na),
        out_shape=jax.ShapeDtypeStruct(xs.shape, xs.dtype),
        grid=(na + nb2,),
        in_specs=[pl.BlockSpec((1, 1, GDST_LANES), lambda i: (i, 0, 0), memory_space=pltpu.SMEM),
                  pl.BlockSpec((TOP_K, tm), lambda i: (0, i)),
                  pl.BlockSpec((tm, ROW_W), lambda i: (jnp.minimum(i, na - 1), 0)),
                  pl.BlockSpec((tm, ROW_W), lambda i: (jnp.maximum(i - na, 0), 0)),
                  pl.BlockSpec(memory_space=pl.ANY)],
        out_specs=pl.BlockSpec(memory_space=pl.ANY),
        scratch_shapes=[pltpu.VMEM((tm, ROW_W), BF16), pltpu.VMEM((LOCAL_ROWS, ROW_W), BF16),
                        pltpu.SemaphoreType.DMA],
        input_output_aliases={4: 0},
        compiler_params=pltpu.CompilerParams(dimension_semantics=("arbitrary",), vmem_limit_bytes=VMEM_LIMIT),
        name="dispatch",
    )(gdst, lr, h_a, h_b, xs)


def _expert_kernel(n_tiles_all, ts_ref, xs_ref, wg_ref, wu_ref, wd_ref, bg_ref, bu_ref, bd_ref, y_ref,
                   wstage, wgb, wub, wdb, xbuf, ybuf, sem_w, sem_in, sem_out):
    e = pl.program_id(0)
    t0 = ts_ref[e]
    nt = ts_ref[e + 1] - t0
    wslot = e % 2

    def w_copies(ex, slot):
        return [pltpu.make_async_copy(w_ref.at[ex], wstage.at[slot, j], sem_w.at[slot])
                for j, w_ref in enumerate((wg_ref, wu_ref, wd_ref))]

    def in_copy(t, slot):
        rows = pl.ds(pl.multiple_of((t0 + t) * ROW_TILE, ROW_TILE), ROW_TILE)
        return pltpu.make_async_copy(xs_ref.at[rows], xbuf.at[slot], sem_in.at[slot])

    def out_copy(tile, slot):
        rows = pl.ds(pl.multiple_of(tile * ROW_TILE, ROW_TILE), ROW_TILE)
        return pltpu.make_async_copy(ybuf.at[slot], y_ref.at[rows], sem_out.at[slot])

    @pl.when((e == 0) & (nt > 0))
    def _():
        for c in w_copies(0, 0):
            c.start()

    @pl.when(nt > 0)
    def _():
        in_copy(0, 0).start()

    @pl.when(nt > 1)
    def _():
        in_copy(1, 1).start()

    @pl.when(e + 1 < N_EXPERTS)
    def _():
        @pl.when(ts_ref[e + 2] > ts_ref[e + 1])
        def _():
            for c in w_copies(e + 1, 1 - wslot):
                c.start(priority=1)

    @pl.when(nt > 0)
    def _():
        for c in w_copies(e, wslot):
            c.wait()
        wgb[...] = wstage[wslot, 0].astype(BF16)
        wub[...] = wstage[wslot, 1].astype(BF16)
        wdb[...] = wstage[wslot, 2].astype(BF16)
        e_f = e.astype(F32)

        def body(t, carry):
            slot = t % 2
            in_copy(t, slot).wait()

            @pl.when(t >= 2)
            def _():
                out_copy(t0 + t - 2, slot).wait()

            xw = xbuf[slot]
            x = xw[:, :D_MODEL]
            info = xw[:, D_MODEL:].astype(F32)
            w_row = jnp.zeros((ROW_TILE, 1), F32)
            for k in range(TOP_K):
                wk = info[:, TOP_K + k:TOP_K + k + 1] + info[:, 2 * TOP_K + k:2 * TOP_K + k + 1]
                w_row = w_row + jnp.where(info[:, k:k + 1] == e_f, wk, 0.0)
            g = jnp.minimum(_dot(x, wgb[...]) + bg_ref[0], SWIGLU_LIMIT)
            u = jnp.clip(_dot(x, wub[...]) + bu_ref[0], -SWIGLU_LIMIT, SWIGLU_LIMIT)
            act = g * jax.nn.sigmoid(SWIGLU_ALPHA * g) * (u + 1.0)
            ybuf[slot] = ((_dot(act.astype(BF16), wdb[...]) + bd_ref[0]) * w_row).astype(BF16)
            out_copy(t0 + t, slot).start()

            @pl.when(t + 2 < nt)
            def _():
                in_copy(t + 2, slot).start()

            return carry

        lax.fori_loop(0, nt, body, 0)

        @pl.when(nt >= 2)
        def _():
            out_copy(t0 + nt - 2, nt % 2).wait()

        out_copy(t0 + nt - 1, (nt - 1) % 2).wait()

    @pl.when(e == pl.num_programs(0) - 1)
    def _():
        n_used = ts_ref[N_EXPERTS]
        ybuf[0] = jnp.zeros((ROW_TILE, D_MODEL), BF16)

        def zbody(tile, carry):
            c = out_copy(tile, 0)
            c.start()
            c.wait()
            return carry

        lax.fori_loop(n_used, n_tiles_all, zbody, 0)


def _experts(tile_start, xs, w_gate, b_gate, w_up, b_up, w_down, b_down):
    n_rows = xs.shape[0]
    b_spec = pl.BlockSpec((1, 1, D_MODEL), lambda e, ts: (e, 0, 0))
    any_spec = pl.BlockSpec(memory_space=pl.ANY)
    return pl.pallas_call(
        functools.partial(_expert_kernel, n_rows // ROW_TILE),
        out_shape=jax.ShapeDtypeStruct((n_rows, D_MODEL), BF16),
        grid_spec=pltpu.PrefetchScalarGridSpec(
            num_scalar_prefetch=1,
            grid=(N_EXPERTS,),
            in_specs=[any_spec, any_spec, any_spec, any_spec, b_spec, b_spec, b_spec],
            out_specs=any_spec,
            scratch_shapes=[pltpu.VMEM((2, 3, D_MODEL, D_MODEL), F32)]
                           + [pltpu.VMEM((D_MODEL, D_MODEL), BF16)] * 3
                           + [pltpu.VMEM((2, ROW_TILE, ROW_W), BF16), pltpu.VMEM((2, ROW_TILE, D_MODEL), BF16),
                              pltpu.SemaphoreType.DMA((2,)), pltpu.SemaphoreType.DMA((2,)),
                              pltpu.SemaphoreType.DMA((2,))],
        ),
        compiler_params=pltpu.CompilerParams(dimension_semantics=("arbitrary",), vmem_limit_bytes=VMEM_LIMIT),
        name="experts",
    )(tile_start, xs, w_gate, w_up, w_down, b_gate[:, None, :], b_up[:, None, :], b_down[:, None, :])


def _combine_kernel(gdst_ref, lr_ref, y_ref, x1_ref, gt_ref, gf_ref, out_ref, loc_ref, sem):
    x1 = x1_ref[...]
    sb, l, _ = x1.shape
    copies = [_group_copy(loc_ref, y_ref, gdst_ref, g, sem, False) for g in range(LOCAL_GROUPS)]
    for c in copies:
        c.start()
    for c in copies:
        c.wait()
    lr = lr_ref[...]
    moe = jnp.zeros((sb * l, D_MODEL), F32)
    for c in range(LOCAL_ROWS // MASK_ROWS):
        rows = loc_ref[c * MASK_ROWS:(c + 1) * MASK_ROWS, :]
        moe = moe + lax.dot_general(_sort_matrix(lr, c), rows, (((0,), (0,)), ((), ())),
                                    preferred_element_type=F32)
    x2 = x1 + gt_ref[...] * moe.reshape(sb, l, D_MODEL)
    out_ref[...] = x2 * lax.rsqrt(jnp.mean(x2 * x2, axis=-1, keepdims=True) + EPS) * gf_ref[...]


def _combine(gdst, lr, y, x1, gt, g_final, sb, l, seq_div, blk_off):
    n3 = x1.shape[0]
    nblk = n3 // sb
    tm = sb * l
    assert tm == TOK_TILE
    return pl.pallas_call(
        _combine_kernel,
        out_shape=jax.ShapeDtypeStruct(x1.shape, F32),
        grid=(nblk,),
        in_specs=[pl.BlockSpec((1, 1, GDST_LANES), lambda i: (i + blk_off, 0, 0), memory_space=pltpu.SMEM),
                  pl.BlockSpec((TOP_K, tm), lambda i: (0, i + blk_off)),
                  pl.BlockSpec(memory_space=pl.ANY),
                  pl.BlockSpec((sb, l, D_MODEL), lambda i: (i, 0, 0)),
                  pl.BlockSpec((sb, 1, D_MODEL), lambda i: (i // seq_div, 0, 0)),
                  pl.BlockSpec((1, D_MODEL), lambda i: (0, 0))],
        out_specs=pl.BlockSpec((sb, l, D_MODEL), lambda i: (i, 0, 0)),
        scratch_shapes=[pltpu.VMEM((LOCAL_ROWS, D_MODEL), BF16), pltpu.SemaphoreType.DMA],
        compiler_params=pltpu.CompilerParams(dimension_semantics=("arbitrary",), vmem_limit_bytes=VMEM_LIMIT),
        name="combine",
    )(gdst, lr, y, x1, gt, g_final)


def kernel(x_prompt, x_sample, c_prompt, c_sample, state_ssm, state_conv, w_ada, b_ada, g_mix, w_in, g_v_a, w_spatial, b_spatial, g_out_a, conv_w, conv_b, dt_bias, a_log, d_skip, g_out_b, w_out, g_ffn, w_router, b_router, w_gate, b_gate, w_up, b_up, w_down, b_down, g_final):
    assert w_ada.shape[0] == 1, "single-layer step"
    p = dict(w_in=w_in[0], g_mix=g_mix[0], g_v_a=g_v_a[0], w_spatial=w_spatial[0], b_spatial=b_spatial[0],
             g_out_a=g_out_a[0], conv_w=conv_w[0], conv_b=conv_b[0], dt_bias=dt_bias[0], a_log=a_log[0],
             d_skip=d_skip[0], g_out_b=g_out_b[0])
    bp, lp, _ = x_prompt.shape
    bs, ls, _ = x_sample.shape
    tp, ts = bp * lp, bs * ls

    ada = _ada(jnp.concatenate([c_prompt, c_sample], axis=0), w_ada[0], b_ada[0][None, :])
    ada = ada.reshape(bp + bs, 6, 1, D_MODEL)
    ada_p = [ada[:bp, j] for j in range(6)]
    ada_s = [ada[bp:, j] for j in range(6)]

    fw = _front_weights(p)
    mixed_p, conv_p, ssm_p = _prompt_mixer(x_prompt, ada_p[0], ada_p[1], fw, p)
    mixed_s, v_s, conv_s, ssm_s = _sample_mixer(x_sample, ada_s[0], ada_s[1], state_ssm[0], state_conv[0], fw, p)

    w_out_b = w_out[0].astype(BF16)
    g_ffn2 = g_ffn[0][None, :]
    wr_t = w_router[0].T
    wr_hi = wr_t.astype(BF16)
    wr_lo = (wr_t - wr_hi.astype(F32)).astype(BF16)
    br = b_router[0][:, None]
    tps = lp // TOK_TILE
    sbs = TOK_TILE // ls
    xp3 = x_prompt.reshape(bp * tps, TOK_TILE, D_MODEL)
    x1_p, h2p_p, ids_p = _post(mixed_p, xp3, ada_p[2], ada_p[4], ada_p[3], w_out_b, g_ffn2, wr_hi, wr_lo, br,
                               1, TOK_TILE, tps)
    x1_s, h2p_s, ids_s = _post(mixed_s, x_sample, ada_s[2], ada_s[4], ada_s[3], w_out_b, g_ffn2, wr_hi, wr_lo,
                               br, sbs, ls, 1)

    n_blocks = (tp + ts) // TOK_TILE
    max_groups = (tp + ts) * TOP_K // ROW_GROUP + n_blocks * N_EXPERTS + N_EXPERTS * (TILE_GROUPS - 1)
    n_tiles = -(-max_groups // TILE_GROUPS)
    n_tiles_all = n_tiles + LOCAL_GROUPS // TILE_GROUPS
    lr, gdst, tile_start = _plan(jnp.concatenate([ids_p, ids_s], axis=1), n_tiles * TILE_GROUPS)

    xs = jnp.zeros((n_tiles_all * ROW_TILE, ROW_W), BF16)
    xs = _dispatch(gdst, lr, h2p_p, h2p_s, xs)
    y = _experts(tile_start[0], xs, w_gate[0], b_gate[0], w_up[0], b_up[0], w_down[0], b_down[0])

    gf = g_final[None, :]
    y_p = _combine(gdst, lr, y, x1_p, ada_p[5], gf, 1, TOK_TILE, tps, 0).reshape(bp, lp, D_MODEL)
    y_s = _combine(gdst, lr, y, x1_s, ada_s[5], gf, sbs, ls, 1, tp // TOK_TILE)

    return (y_p, y_s, ssm_p[None], conv_p[None], ssm_s[None], conv_s[None], v_s.reshape(1, bs, ls, A_WIDTH))
```

```python
import functools
import math

import numpy as np
import jax
import jax.numpy as jnp
from jax import lax
from jax.experimental import pallas as pl
from jax.experimental.pallas import tpu as pltpu

F32 = jnp.float32
BF16 = jnp.bfloat16
I32 = jnp.int32

D_MODEL = 1024
A_WIDTH = 512
A_HEADS = 4
A_HEAD_DIM = 128
CHUNK = 128
B_WIDTH = 512
SSD_HEAD_DIM = 64
SSD_HEADS = 8
SSD_GROUPS = 2
SSD_STATE = 128
GROUP_W = B_WIDTH // SSD_GROUPS
CONV_K = 4
CONV_DIM = 1024
CONV_PAD = 8
N_EXPERTS = 32
TOP_K = 4
SWIGLU_LIMIT = 7.0
SWIGLU_ALPHA = 1.702
EPS = 1e-6
LANES = 128

TOK_TILE = 512
SAMPLE_SEQ_TILE = 16
ROW_TILE = 256
ROW_GROUP = 16
TILE_GROUPS = ROW_TILE // ROW_GROUP
LOCAL_GROUPS = TOK_TILE * TOP_K // ROW_GROUP + N_EXPERTS
LOCAL_ROWS = LOCAL_GROUPS * ROW_GROUP
GDST_LANES = 256
ROW_W = D_MODEL + LANES
INFO_ROWS = 16
MASK_ROWS = 256
META_ROWS = 8
VMEM_LIMIT = 56 * 1024 * 1024


def _dot(a, b):
    return jnp.dot(a, b, preferred_element_type=F32)


def _dot_nt(a, b):
    return lax.dot_general(a, b, (((1,), (1,)), ((), ())), preferred_element_type=F32)


def _split(x):
    hi = x.astype(BF16)
    lo = (x - hi.astype(F32)).astype(BF16)
    return hi, lo


def _dot_exact_l(t, x):
    hi, lo = _split(x)
    return _dot(t, hi) + _dot(t, lo)


def _dot_exact_r(x, t):
    hi, lo = _split(x)
    return _dot(hi, t) + _dot(lo, t)


def _silu(x):
    return x * jax.nn.sigmoid(x)


def _gelu(x):
    return 0.5 * x * (1.0 + lax.erf(x * (1.0 / math.sqrt(2.0))))


def _softplus(x):
    return jnp.maximum(x, 0.0) + jnp.log1p(jnp.exp(-jnp.abs(x)))


def _rms(x, g):
    return x * lax.rsqrt(jnp.mean(x * x, axis=-1, keepdims=True) + EPS) * g


def _ada_kernel(c_ref, w_ref, b_ref, o_ref):
    s_hi, s_lo = _split(_silu(c_ref[...]))
    w_hi, w_lo = _split(w_ref[...])
    o_ref[...] = _dot(s_hi, w_hi) + _dot(s_lo, w_hi) + _dot(s_hi, w_lo) + b_ref[...]


def _ada(c_all, w_ada, b_ada):
    m = c_all.shape[0]
    n = w_ada.shape[1]
    bn = 512
    return pl.pallas_call(
        _ada_kernel,
        out_shape=jax.ShapeDtypeStruct((m, n), F32),
        grid=(n // bn,),
        in_specs=[pl.BlockSpec((m, D_MODEL), lambda j: (0, 0)),
                  pl.BlockSpec((D_MODEL, bn), lambda j: (0, j)),
                  pl.BlockSpec((1, bn), lambda j: (0, j))],
        out_specs=pl.BlockSpec((m, bn), lambda j: (0, j)),
        compiler_params=pltpu.CompilerParams(dimension_semantics=("arbitrary",), vmem_limit_bytes=VMEM_LIMIT),
        name="ada",
    )(c_all, w_ada, b_ada)


def _mixer_front(x3, sh, sc, prev, refs, xp_ref):
    (g_mix, w_uvz, w_xbc, w_dt, w_dtt, wbd, bias_sp, g_v, g_oa, conv_w, conv_b, dt_bias, dt_bias_t, a_row, a_col) = refs
    sb, l, _ = x3.shape
    tm = sb * l
    xn = x3 * lax.rsqrt(jnp.mean(x3 * x3, axis=-1, keepdims=True) + EPS) * g_mix[...]
    h = (xn * (1.0 + sc) + sh).reshape(tm, D_MODEL)
    hb = h.astype(BF16)
    uvz = _dot(hb, w_uvz[...])
    xbc = _dot(hb, w_xbc[...])
    dt_raw = _dot(hb, w_dt[...])
    dtt_raw = _dot_nt(w_dtt[...], hb)

    u = _gelu(uvz[:, :A_WIDTH])
    vg = _gelu(uvz[:, A_WIDTH:2 * A_WIDTH])
    z = uvz[:, 2 * A_WIDTH:]
    v_parts, s_parts = [], []
    for hd in range(A_HEADS):
        sl = slice(hd * A_HEAD_DIM, (hd + 1) * A_HEAD_DIM)
        vh = _rms(vg[:, sl], g_v[:, sl])
        v_parts.append(vh)
        s_parts.append(_dot(wbd[hd], vh.astype(BF16)))
    v = jnp.concatenate(v_parts, axis=1)
    s_a = jnp.concatenate(s_parts, axis=1) + bias_sp[...]
    out_a = _rms(u * s_a, g_oa[...])

    xp_ref[:, 0:CONV_PAD, :] = prev
    xp_ref[:, CONV_PAD:, :] = xbc.reshape(sb, l, CONV_DIM)
    acc = conv_b[...]
    for k in range(CONV_K):
        off = CONV_PAD - (CONV_K - 1) + k
        acc = acc + xp_ref[:, off:off + l, :] * conv_w[k:k + 1, :]
    xc = _silu(acc).reshape(tm, CONV_DIM)
    dt = _softplus(dt_raw + dt_bias[...])
    dtt = _softplus(dtt_raw + dt_bias_t[...])
    d_a = dt * a_row[...]
    d_at = dtt * a_col[...]
    return out_a, v, z, xc, dt, d_a, d_at


def _ssd_chunk(xs, bm, cm, dt, d_a, d_at, cref):
    tril, triu, ones, expand, mask = cref
    cs = _dot_exact_l(tril[...], d_a)
    cs_t = _dot_exact_r(d_at, triu[...])
    cs_tot = _dot_exact_l(ones[...], d_a)
    vals = jnp.concatenate([dt, jnp.exp(cs_tot - cs), jnp.exp(cs)], axis=0)
    vals_e = _dot_exact_r(vals, expand[...])
    n = xs.shape[0]
    dt_e, dte_e, e_e = vals_e[:n], vals_e[n:2 * n], vals_e[2 * n:]
    xdt = xs * dt_e
    xdtd = xdt * dte_e
    msk = mask[...] > 0.5
    row_lt_half = lax.broadcasted_iota(I32, (2 * n, LANES), 0) < n
    lane_lt_half = lax.broadcasted_iota(I32, (2 * n, LANES), 1) < SSD_HEAD_DIM
    y_parts = []
    for g in range(SSD_GROUPS):
        cb = _dot_nt(cm[:, g * SSD_STATE:(g + 1) * SSD_STATE].astype(BF16),
                     bm[:, g * SSD_STATE:(g + 1) * SSD_STATE].astype(BF16))
        for hp in range(SSD_HEADS // SSD_GROUPS // 2):
            h0 = g * (SSD_HEADS // SSD_GROUPS) + 2 * hp
            ms = []
            for hh in (h0, h0 + 1):
                diff = cs[:, hh:hh + 1] - cs_t[hh:hh + 1, :]
                ms.append((cb * jnp.where(msk, jnp.exp(jnp.where(msk, diff, 0.0)), 0.0)).astype(BF16))
            pair = xdt[:, h0 * SSD_HEAD_DIM:(h0 + 2) * SSD_HEAD_DIM]
            rhs = jnp.where(row_lt_half == lane_lt_half, jnp.concatenate([pair, pair], axis=0), 0.0).astype(BF16)
            y_parts.append(_dot(jnp.concatenate(ms, axis=1), rhs))
    y_diag = jnp.concatenate(y_parts, axis=1)
    return y_diag, e_e, xdtd, cs_tot


def _mixer_back(y, xs, z, out_a, dskip_e, g_ob):
    y = y + xs * dskip_e
    gated = y * _silu(z)
    parts = [_rms(gated[:, g * GROUP_W:(g + 1) * GROUP_W], g_ob[:, g * GROUP_W:(g + 1) * GROUP_W])
             for g in range(SSD_GROUPS)]
    return jnp.concatenate([out_a] + parts, axis=1).astype(BF16)


N_FRONT = 15
N_SSD = 5


def _prompt_mixer_kernel(tiles_per_seq, x_ref, sh_ref, sc_ref, *rest):
    front = rest[:N_FRONT]
    cref = rest[N_FRONT:N_FRONT + N_SSD]
    dskip_e, g_ob = rest[N_FRONT + N_SSD:N_FRONT + N_SSD + 2]
    mixed_ref, conv_out_ref, ssm_out_ref = rest[N_FRONT + N_SSD + 2:N_FRONT + N_SSD + 5]
    xp_ref, carry_ref, st_ref = rest[N_FRONT + N_SSD + 5:]
    i = pl.program_id(0)
    first = (i % tiles_per_seq) == 0

    @pl.when(first)
    def _():
        carry_ref[...] = jnp.zeros_like(carry_ref)
        st_ref[...] = jnp.zeros_like(st_ref)

    x3 = x_ref[...]
    l = x3.shape[1]
    out_a, _, z, xc, dt, d_a, d_at = _mixer_front(x3, sh_ref[...], sc_ref[...], carry_ref[...], front, xp_ref)
    carry_ref[...] = xp_ref[:, l:l + CONV_PAD, :]
    xs = xc[:, :B_WIDTH]
    y_rows = []
    for c in range(l // CHUNK):
        r = slice(c * CHUNK, (c + 1) * CHUNK)
        bm = xc[r, B_WIDTH:B_WIDTH + SSD_GROUPS * SSD_STATE]
        cm = xc[r, B_WIDTH + SSD_GROUPS * SSD_STATE:]
        y_diag, e_e, xdtd, _ = _ssd_chunk(xs[r], bm, cm, dt[r], d_a[r], d_at[:, r], cref)
        st = st_ref[...]
        y_off, upd = [], []
        for g in range(SSD_GROUPS):
            gs = slice(g * GROUP_W, (g + 1) * GROUP_W)
            ns = slice(g * SSD_STATE, (g + 1) * SSD_STATE)
            y_off.append(_dot(cm[:, ns].astype(BF16), st[:, gs].astype(BF16)))
            upd.append(_dot(bm[:, ns].T.astype(BF16), xdtd[:, gs].astype(BF16)))
        y_rows.append(y_diag + jnp.concatenate(y_off, axis=1) * e_e)
        st_ref[...] = st * e_e[CHUNK - 1:CHUNK, :] + jnp.concatenate(upd, axis=1)
    y = jnp.concatenate(y_rows, axis=0)
    mixed_ref[...] = _mixer_back(y, xs, z, out_a, dskip_e[...], g_ob[...])

    @pl.when((i % tiles_per_seq) == tiles_per_seq - 1)
    def _():
        conv_out_ref[...] = xp_ref[:, l + CONV_PAD - (CONV_K - 1):l + CONV_PAD, :]
        ssm_out_ref[0] = st_ref[...].T


def _sample_mixer_kernel(x_ref, sh_ref, sc_ref, prev_ref, ssm0_ref, *rest):
    front = rest[:N_FRONT]
    cref = rest[N_FRONT:N_FRONT + N_SSD]
    dskip_e, g_ob, selseq = rest[N_FRONT + N_SSD:N_FRONT + N_SSD + 3]
    mixed_ref, v_ref, conv_out_ref, ssm_out_ref = rest[N_FRONT + N_SSD + 3:N_FRONT + N_SSD + 7]
    xp_ref, yoff_ref, cbf_ref, bbf_ref, t1_ref, dtab_ref = rest[N_FRONT + N_SSD + 7:]
    x3 = x_ref[...]
    sb, l, _ = x3.shape
    tm = sb * l
    out_a, v, z, xc, dt, d_a, d_at = _mixer_front(x3, sh_ref[...], sc_ref[...], prev_ref[...], front, xp_ref)
    v_ref[...] = v
    conv_out_ref[...] = xp_ref[:, l + CONV_PAD - (CONV_K - 1):l + CONV_PAD, :]
    xs = xc[:, :B_WIDTH]
    bm = xc[:, B_WIDTH:B_WIDTH + SSD_GROUPS * SSD_STATE]
    cm = xc[:, B_WIDTH + SSD_GROUPS * SSD_STATE:]
    y_diag, e_e, xdtd, _ = _ssd_chunk(xs, bm, cm, dt, d_a, d_at, cref)

    e_tot = jnp.exp(_dot_exact_l(selseq[...], d_a))
    for hh in range(SSD_HEADS):
        dtab_ref[hh] = jnp.broadcast_to(e_tot[:, hh:hh + 1], (sb, LANES))
    cbf_ref[...] = cm
    bbf_ref[...] = bm
    for g in range(SSD_GROUPS):
        t1_ref[g] = xdtd[:, g * GROUP_W:(g + 1) * GROUP_W].T.astype(BF16)
    seq_of_row = lax.broadcasted_iota(I32, (tm, SSD_STATE), 0) // l
    heads_per_group = SSD_HEADS // SSD_GROUPS

    def body(j, carry):
        r0 = pl.multiple_of(j * l, l)
        s0 = ssm0_ref[j]
        for g in range(SSD_GROUPS):
            ns = slice(g * SSD_STATE, (g + 1) * SSD_STATE)
            s0g = s0[g * heads_per_group:(g + 1) * heads_per_group].reshape(GROUP_W, SSD_STATE)
            cj = cbf_ref[pl.ds(r0, l), ns].astype(BF16)
            yoff_ref[pl.ds(r0, l), g * GROUP_W:(g + 1) * GROUP_W] = _dot_nt(cj, s0g.astype(BF16))
            bmask = jnp.where(seq_of_row == j, bbf_ref[:, ns], 0.0).astype(BF16)
            upd = _dot(t1_ref[g], bmask)
            for hq in range(heads_per_group):
                hh = g * heads_per_group + hq
                dec = dtab_ref[hh, pl.ds(j, 1), :]
                ssm_out_ref[j, hh] = s0[hh] * dec + upd[hq * SSD_HEAD_DIM:(hq + 1) * SSD_HEAD_DIM]
        return carry

    lax.fori_loop(0, sb, body, 0)
    y = y_diag + yoff_ref[...] * e_e
    mixed_ref[...] = _mixer_back(y, xs, z, out_a, dskip_e[...], g_ob[...])


def _const_spec(a):
    nd = a.ndim
    return pl.BlockSpec(a.shape, lambda i, _nd=nd: (0,) * _nd)


def _spatial_consts(w_spatial, b_spatial, cl, tm):
    w = jnp.where(jnp.tril(jnp.ones((cl, cl), bool)), w_spatial[:, :cl, :cl], 0.0)
    eye = jnp.eye(tm // cl, dtype=F32)
    wbd = jnp.einsum("ab,hts->hatbs", eye, w).reshape(A_HEADS, tm, tm).astype(BF16)
    bias = jnp.tile(jnp.repeat(b_spatial[:, :cl].T, A_HEAD_DIM, axis=1), (tm // cl, 1))
    return wbd, bias


def _ssd_consts(cl):
    r = np.arange(CHUNK)
    same = (r[:, None] // cl) == (r[None, :] // cl)
    tril = same & (r[:, None] >= r[None, :])
    expand = np.zeros((LANES, B_WIDTH), np.float32)
    for hh in range(SSD_HEADS):
        expand[hh, hh * SSD_HEAD_DIM:(hh + 1) * SSD_HEAD_DIM] = 1.0
    return (jnp.asarray(tril, BF16), jnp.asarray(tril.T, BF16), jnp.asarray(same, BF16),
            jnp.asarray(expand, BF16), jnp.asarray(tril, F32))


def _front_weights(p):
    w_in = p["w_in"]
    c0, c1 = 3 * A_WIDTH, 3 * A_WIDTH + CONV_DIM
    w_dt = w_in[:, c1:]
    pad8 = lambda v: jnp.pad(v, (0, LANES - SSD_HEADS))
    a = -jnp.exp(p["a_log"])
    return dict(
        g_mix=p["g_mix"][None, :],
        w_uvz=w_in[:, :c0].astype(BF16),
        w_xbc=w_in[:, c0:c1].astype(BF16),
        w_dt=jnp.pad(w_dt, ((0, 0), (0, LANES - SSD_HEADS))).astype(BF16),
        w_dtt=w_dt.T.astype(BF16),
        g_v=p["g_v_a"][None, :], g_oa=p["g_out_a"][None, :],
        conv_w=p["conv_w"], conv_b=p["conv_b"][None, :],
        dt_bias=pad8(p["dt_bias"])[None, :], dt_bias_t=p["dt_bias"][:, None],
        a_row=pad8(a)[None, :], a_col=a[:, None],
        dskip_e=jnp.repeat(p["d_skip"], SSD_HEAD_DIM)[None, :],
        g_ob=p["g_out_b"][None, :],
    )


def _front_list(fw, wbd, bias_sp):
    return [fw["g_mix"], fw["w_uvz"], fw["w_xbc"], fw["w_dt"], fw["w_dtt"], wbd, bias_sp, fw["g_v"], fw["g_oa"],
            fw["conv_w"], fw["conv_b"], fw["dt_bias"], fw["dt_bias_t"], fw["a_row"], fw["a_col"]]


def _prompt_mixer(x, sh, sc, fw, p):
    nseq, lseq, _ = x.shape
    tps = lseq // TOK_TILE
    nt = nseq * tps
    x4 = x.reshape(nt, TOK_TILE, D_MODEL)
    wbd, bias_sp = _spatial_consts(p["w_spatial"], p["b_spatial"], CHUNK, TOK_TILE)
    consts = _front_list(fw, wbd, bias_sp) + list(_ssd_consts(CHUNK)) + [fw["dskip_e"], fw["g_ob"]]
    seq_spec = pl.BlockSpec((1, 1, D_MODEL), lambda i: (i // tps, 0, 0))
    mixed, conv_new, ssm_new = pl.pallas_call(
        functools.partial(_prompt_mixer_kernel, tps),
        out_shape=(jax.ShapeDtypeStruct((nt * TOK_TILE, D_MODEL), BF16),
                   jax.ShapeDtypeStruct((nseq, CONV_K - 1, CONV_DIM), F32),
                   jax.ShapeDtypeStruct((nseq, B_WIDTH, SSD_STATE), F32)),
        grid=(nt,),
        in_specs=[pl.BlockSpec((1, TOK_TILE, D_MODEL), lambda i: (i, 0, 0)), seq_spec, seq_spec]
                 + [_const_spec(a) for a in consts],
        out_specs=(pl.BlockSpec((TOK_TILE, D_MODEL), lambda i: (i, 0)),
                   pl.BlockSpec((1, CONV_K - 1, CONV_DIM), lambda i: (i // tps, 0, 0)),
                   pl.BlockSpec((1, B_WIDTH, SSD_STATE), lambda i: (i // tps, 0, 0))),
        scratch_shapes=[pltpu.VMEM((1, TOK_TILE + CONV_PAD, CONV_DIM), F32),
                        pltpu.VMEM((1, CONV_PAD, CONV_DIM), F32),
                        pltpu.VMEM((SSD_STATE, B_WIDTH), F32)],
        compiler_params=pltpu.CompilerParams(dimension_semantics=("arbitrary",), vmem_limit_bytes=VMEM_LIMIT),
        name="prompt_mixer",
    )(x4, sh, sc, *consts)
    return mixed, conv_new, ssm_new.reshape(nseq, SSD_HEADS, SSD_HEAD_DIM, SSD_STATE)


def _sample_mixer(x, sh, sc, state_ssm, state_conv, fw, p):
    nseq, l, _ = x.shape
    sb = SAMPLE_SEQ_TILE
    tm = sb * l
    assert tm == CHUNK
    wbd, bias_sp = _spatial_consts(p["w_spatial"], p["b_spatial"], l, tm)
    selseq = jnp.asarray((np.arange(tm)[None, :] // l) == np.arange(sb)[:, None], BF16)
    consts = _front_list(fw, wbd, bias_sp) + list(_ssd_consts(l)) + [fw["dskip_e"], fw["g_ob"], selseq]
    prev = jnp.pad(state_conv, ((0, 0), (CONV_PAD - (CONV_K - 1), 0), (0, 0)))
    seq_spec = pl.BlockSpec((sb, 1, D_MODEL), lambda i: (i, 0, 0))
    ssm_spec = pl.BlockSpec((sb, SSD_HEADS, SSD_HEAD_DIM, SSD_STATE), lambda i: (i, 0, 0, 0))
    return pl.pallas_call(
        _sample_mixer_kernel,
        out_shape=(jax.ShapeDtypeStruct((nseq * l, D_MODEL), BF16),
                   jax.ShapeDtypeStruct((nseq * l, A_WIDTH), F32),
                   jax.ShapeDtypeStruct((nseq, CONV_K - 1, CONV_DIM), F32),
                   jax.ShapeDtypeStruct(state_ssm.shape, F32)),
        grid=(nseq // sb,),
        in_specs=[pl.BlockSpec((sb, l, D_MODEL), lambda i: (i, 0, 0)), seq_spec, seq_spec,
                  pl.BlockSpec((sb, CONV_PAD, CONV_DIM), lambda i: (i, 0, 0)), ssm_spec]
                 + [_const_spec(a) for a in consts],
        out_specs=(pl.BlockSpec((tm, D_MODEL), lambda i: (i, 0)),
                   pl.BlockSpec((tm, A_WIDTH), lambda i: (i, 0)),
                   pl.BlockSpec((sb, CONV_K - 1, CONV_DIM), lambda i: (i, 0, 0)),
                   ssm_spec),
        scratch_shapes=[pltpu.VMEM((sb, l + CONV_PAD, CONV_DIM), F32),
                        pltpu.VMEM((tm, B_WIDTH), F32),
                        pltpu.VMEM((tm, SSD_GROUPS * SSD_STATE), F32),
                        pltpu.VMEM((tm, SSD_GROUPS * SSD_STATE), F32),
                        pltpu.VMEM((SSD_GROUPS, GROUP_W, tm), BF16),
                        pltpu.VMEM((SSD_HEADS, sb, LANES), F32)],
        compiler_params=pltpu.CompilerParams(dimension_semantics=("arbitrary",), vmem_limit_bytes=VMEM_LIMIT),
        name="sample_mixer",
    )(x, sh, sc, prev, state_ssm, *consts)


def _post_kernel(mixed_ref, x_ref, gt_ref, sc_ref, sh_ref, w_out_ref, g_ffn_ref, wr_hi_ref, wr_lo_ref, br_ref,
                 x1_ref, h2p_ref, ids_ref):
    x3 = x_ref[...]
    sb, l, _ = x3.shape
    tm = sb * l
    mix = _dot(mixed_ref[...], w_out_ref[...]).reshape(sb, l, D_MODEL)
    x1 = x3 + gt_ref[...] * mix
    x1_ref[...] = x1
    xn = x1 * lax.rsqrt(jnp.mean(x1 * x1, axis=-1, keepdims=True) + EPS) * g_ffn_ref[...]
    h2 = (xn * (1.0 + sc_ref[...]) + sh_ref[...]).reshape(tm, D_MODEL)
    h_hi, h_lo = _split(h2)
    h2p_ref[:, :D_MODEL] = h_hi
    logits = (_dot_nt(wr_hi_ref[...], h_hi) + _dot_nt(wr_hi_ref[...], h_lo) + _dot_nt(wr_lo_ref[...], h_hi)
              + br_ref[...])
    e_iota = lax.broadcasted_iota(I32, logits.shape, 0)
    vals, idxs = [], []
    for _ in range(TOP_K):
        m = jnp.max(logits, axis=0, keepdims=True)
        idx = jnp.min(jnp.where(logits == m, e_iota, N_EXPERTS), axis=0, keepdims=True)
        vals.append(m)
        idxs.append(idx)
        logits = jnp.where(e_iota == idx, -jnp.inf, logits)
    ex = [jnp.exp(v - vals[0]) for v in vals]
    tot = ex[0] + ex[1] + ex[2] + ex[3]
    ids = jnp.concatenate(idxs, axis=0)
    ids_ref[...] = ids
    wts = jnp.concatenate([e / tot for e in ex], axis=0)
    w_hi = wts.astype(BF16).astype(F32)
    info = jnp.concatenate([ids.astype(F32), w_hi, wts - w_hi, jnp.zeros((TOP_K, tm), F32)], axis=0).astype(BF16)
    r = lax.broadcasted_iota(I32, (INFO_ROWS, LANES), 0)
    c = lax.broadcasted_iota(I32, (INFO_ROWS, LANES), 1)
    place = jnp.where(r == c, 1.0, 0.0).astype(BF16)
    h2p_ref[:, D_MODEL:] = lax.dot_general(info, place, (((0,), (0,)), ((), ())),
                                           preferred_element_type=F32).astype(BF16)


def _post(mixed, x, gt, sc, sh, w_out_b, g_ffn, wr_hi, wr_lo, br, sb, l, seq_div):
    n3, _, _ = x.shape
    nblk = n3 // sb
    tm = sb * l
    t = n3 * l
    ada_spec = pl.BlockSpec((sb, 1, D_MODEL), lambda i: (i // seq_div, 0, 0))
    consts = [w_out_b, g_ffn, wr_hi, wr_lo, br]
    return pl.pallas_call(
        _post_kernel,
        out_shape=(jax.ShapeDtypeStruct(x.shape, F32),
                   jax.ShapeDtypeStruct((t, ROW_W), BF16),
                   jax.ShapeDtypeStruct((TOP_K, t), I32)),
        grid=(nblk,),
        in_specs=[pl.BlockSpec((tm, D_MODEL), lambda i: (i, 0)),
                  pl.BlockSpec((sb, l, D_MODEL), lambda i: (i, 0, 0)), ada_spec, ada_spec, ada_spec]
                 + [_const_spec(a) for a in consts],
        out_specs=(pl.BlockSpec((sb, l, D_MODEL), lambda i: (i, 0, 0)),
                   pl.BlockSpec((tm, ROW_W), lambda i: (i, 0)),
                   pl.BlockSpec((TOP_K, tm), lambda i: (0, i))),
        compiler_params=pltpu.CompilerParams(dimension_semantics=("arbitrary",), vmem_limit_bytes=VMEM_LIMIT),
        name="post",
    )(mixed, x, gt, sc, sh, *consts)


def _strict_upper(n):
    r = lax.broadcasted_iota(I32, (n, n), 0)
    c = lax.broadcasted_iota(I32, (n, n), 1)
    return jnp.where(r < c, 1.0, 0.0).astype(BF16)


def _expert_prefix(col):
    r = lax.broadcasted_iota(I32, (N_EXPERTS, N_EXPERTS), 0)
    c = lax.broadcasted_iota(I32, (N_EXPERTS, N_EXPERTS), 1)
    as_row = jnp.sum(jnp.where(r == c, col, 0.0), axis=0, keepdims=True)
    return jnp.sum(jnp.where(c < r, as_row, 0.0), axis=1, keepdims=True)


def _plan_kernel(dump_group, ids_ref, lr_ref, gdst_ref, tile_e_ref, seg_ref, gb_ref):
    ph = pl.program_id(0)
    b = pl.program_id(1)
    ids = ids_ref[...]
    tm = ids.shape[1]
    e_iota = lax.broadcasted_iota(I32, (N_EXPERTS, tm), 0)
    onehot = [ids[k:k + 1, :] == e_iota for k in range(TOP_K)]
    sel = (onehot[0] | onehot[1]) | (onehot[2] | onehot[3])
    m = jnp.where(sel, 1.0, 0.0)
    rowsum = jnp.sum(m, axis=1, keepdims=True)
    blk_lane = lax.broadcasted_iota(I32, (N_EXPERTS, LANES), 1)

    @pl.when((ph == 0) & (b == 0))
    def _():
        seg_ref[...] = jnp.zeros_like(seg_ref)

    @pl.when(ph == 0)
    def _():
        seg = jnp.ceil(rowsum * (1.0 / ROW_GROUP))
        seg_ref[...] = jnp.where(blk_lane == b, seg, seg_ref[...])

    @pl.when((ph == 1) & (b == 0))
    def _():
        seg = seg_ref[...]
        tot = jnp.sum(seg, axis=1, keepdims=True)
        padded = jnp.ceil(tot * (1.0 / TILE_GROUPS)) * TILE_GROUPS
        gstart = _expert_prefix(padded)
        gb_ref[...] = gstart + _dot(seg.astype(BF16), _strict_upper(LANES))
        r = lax.broadcasted_iota(I32, (N_EXPERTS, LANES), 0)
        c = lax.broadcasted_iota(I32, (N_EXPERTS, LANES), 1)
        as_row = lambda col: jnp.sum(jnp.where(r == c, col, 0.0), axis=0, keepdims=True)
        n_used = jnp.sum(padded, axis=0, keepdims=True)
        lane = lax.broadcasted_iota(I32, (1, LANES), 1)
        tiles = jnp.where(lane == N_EXPERTS, n_used, as_row(gstart)) * (1.0 / TILE_GROUPS)
        meta = jnp.concatenate([tiles, as_row(gstart + tot), as_row(padded - tot),
                                jnp.zeros((META_ROWS - 3, LANES), F32)], axis=0)
        tile_e_ref[...] = meta.astype(I32)

    @pl.when(ph == 1)
    def _():
        seg_b = jnp.sum(jnp.where(blk_lane == b, seg_ref[...], 0.0), axis=1, keepdims=True)
        gb_b = jnp.sum(jnp.where(blk_lane == b, gb_ref[...], 0.0), axis=1, keepdims=True)
        loc_b = _expert_prefix(seg_b)
        before = _dot(m.astype(BF16), _strict_upper(tm)) + loc_b * ROW_GROUP
        lr_ref[...] = jnp.concatenate(
            [jnp.sum(jnp.where(onehot[k], before, 0.0), axis=0, keepdims=True) for k in range(TOP_K)],
            axis=0).astype(I32)
        g = lax.broadcasted_iota(I32, (N_EXPERTS, GDST_LANES), 1).astype(F32)
        inside = (loc_b <= g) & (g < loc_b + seg_b)
        dst = jnp.sum(jnp.where(inside, gb_b + g - loc_b, 0.0), axis=0, keepdims=True)
        used = jnp.sum(jnp.where(inside, 1.0, 0.0), axis=0, keepdims=True) > 0.5
        dump = dump_group + (b % 2).astype(F32) * LOCAL_GROUPS + g[0:1, :]
        gdst_ref[0] = jnp.where(used, dst, dump).astype(I32)


def _plan(ids, dump_group):
    t = ids.shape[1]
    tm = TOK_TILE
    nb = t // tm
    assert nb <= LANES
    return pl.pallas_call(
        functools.partial(_plan_kernel, float(dump_group)),
        out_shape=(jax.ShapeDtypeStruct((TOP_K, t), I32),
                   jax.ShapeDtypeStruct((nb, 1, GDST_LANES), I32),
                   jax.ShapeDtypeStruct((META_ROWS, LANES), I32)),
        grid=(2, nb),
        in_specs=[pl.BlockSpec((TOP_K, tm), lambda ph, b: (0, b))],
        out_specs=(pl.BlockSpec((TOP_K, tm), lambda ph, b: (0, b * ph)),
                   pl.BlockSpec((1, 1, GDST_LANES), lambda ph, b: (b * ph, 0, 0)),
                   pl.BlockSpec((META_ROWS, LANES), lambda ph, b: (0, 0))),
        scratch_shapes=[pltpu.VMEM((N_EXPERTS, LANES), F32), pltpu.VMEM((N_EXPERTS, LANES), F32)],
        compiler_params=pltpu.CompilerParams(dimension_semantics=("arbitrary", "arbitrary")),
        name="plan",
    )(ids)


def _sort_matrix(lr, c):
    r_iota = lax.broadcasted_iota(I32, (MASK_ROWS, lr.shape[1]), 0) + c * MASK_ROWS
    p = jnp.where(r_iota == lr[TOP_K - 1:TOP_K, :], 1.0, 0.0)
    for k in range(TOP_K - 1):
        p = jnp.where(r_iota == lr[k:k + 1, :], 1.0, p)
    return p.astype(BF16)


def _group_copies(loc_ref, slot, far_ref, gdst_ref, sem, to_far):
    copies = []
    for g in range(LOCAL_GROUPS):
        dst = pl.multiple_of(gdst_ref[0, 0, g] * ROW_GROUP, ROW_GROUP)
        near = loc_ref.at[slot, pl.ds(g * ROW_GROUP, ROW_GROUP)]
        far = far_ref.at[pl.ds(dst, ROW_GROUP)]
        copies.append(pltpu.make_async_copy(near, far, sem.at[slot]) if to_far
                      else pltpu.make_async_copy(far, near, sem.at[slot]))
    return copies


def _dispatch_kernel(n_first, n_tiles_all, gdst_ref, meta_ref, lr_ref, ha_ref, hb_ref, xs_ref,
                     h_ref, loc_ref, zero_ref, sem, zsem):
    i = pl.program_id(0)
    last = pl.num_programs(0) - 1
    slot = i % 2

    @pl.when(i < n_first)
    def _():
        h_ref[...] = ha_ref[...]

    @pl.when(i >= n_first)
    def _():
        h_ref[...] = hb_ref[...]

    lr = lr_ref[...]
    for c in range(LOCAL_ROWS // MASK_ROWS):
        loc_ref[slot, c * MASK_ROWS:(c + 1) * MASK_ROWS, :] = _dot(_sort_matrix(lr, c), h_ref[...]).astype(BF16)
    for c in _group_copies(loc_ref, slot, xs_ref, gdst_ref, sem, True):
        c.start()

    @pl.when(i > 0)
    def _():
        for c in _group_copies(loc_ref, 1 - slot, xs_ref, gdst_ref, sem, True):
            c.wait()

    @pl.when(i == last)
    def _():
        for c in _group_copies(loc_ref, slot, xs_ref, gdst_ref, sem, True):
            c.wait()
        zero_ref[...] = jnp.zeros_like(zero_ref)

        def pad_copy(e, j):
            row = pl.multiple_of((meta_ref[1, e] + j) * ROW_GROUP, ROW_GROUP)
            return pltpu.make_async_copy(zero_ref.at[pl.ds(0, ROW_GROUP)], xs_ref.at[pl.ds(row, ROW_GROUP)], zsem)

        def tile_copy(t):
            row = pl.multiple_of(t * ROW_TILE, ROW_TILE)
            return pltpu.make_async_copy(zero_ref, xs_ref.at[pl.ds(row, ROW_TILE)], zsem)

        def pads(fn):
            def body(e, carry):
                for j in range(TILE_GROUPS - 1):
                    @pl.when(j < meta_ref[2, e])
                    def _():
                        fn(pad_copy(e, j))
                return carry
            lax.fori_loop(0, N_EXPERTS, body, 0)

        def tiles(fn):
            def body(t, carry):
                fn(tile_copy(t))
                return carry
            lax.fori_loop(meta_ref[0, N_EXPERTS], n_tiles_all, body, 0)

        pads(lambda c: c.start())
        tiles(lambda c: c.start())
        pads(lambda c: c.wait())
        tiles(lambda c: c.wait())


def _dispatch(gdst, meta, lr, h_a, h_b, n_tiles_all):
    tm = TOK_TILE
    na, nb2 = h_a.shape[0] // tm, h_b.shape[0] // tm
    return pl.pallas_call(
        functools.partial(_dispatch_kernel, na, n_tiles_all),
        out_shape=jax.ShapeDtypeStruct((n_tiles_all * ROW_TILE, ROW_W), BF16),
        grid=(na + nb2,),
        in_specs=[pl.BlockSpec((1, 1, GDST_LANES), lambda i: (i, 0, 0), memory_space=pltpu.SMEM),
                  pl.BlockSpec((META_ROWS, LANES), lambda i: (0, 0), memory_space=pltpu.SMEM),
                  pl.BlockSpec((TOP_K, tm), lambda i: (0, i)),
                  pl.BlockSpec((tm, ROW_W), lambda i: (jnp.minimum(i, na - 1), 0)),
                  pl.BlockSpec((tm, ROW_W), lambda i: (jnp.maximum(i - na, 0), 0))],
        out_specs=pl.BlockSpec(memory_space=pl.ANY),
        scratch_shapes=[pltpu.VMEM((tm, ROW_W), BF16), pltpu.VMEM((2, LOCAL_ROWS, ROW_W), BF16),
                        pltpu.VMEM((ROW_TILE, ROW_W), BF16),
                        pltpu.SemaphoreType.DMA((2,)), pltpu.SemaphoreType.DMA],
        compiler_params=pltpu.CompilerParams(dimension_semantics=("arbitrary",), vmem_limit_bytes=VMEM_LIMIT),
        name="dispatch",
    )(gdst, meta, lr, h_a, h_b)


def _expert_kernel(n_tiles_all, ts_ref, xs_ref, wg_ref, wu_ref, wd_ref, bg_ref, bu_ref, bd_ref, y_ref,
                   wstage, wgb, wub, wdb, xbuf, ybuf, sem_w, sem_in, sem_out):
    e = pl.program_id(0)
    t0 = ts_ref[e]
    nt = ts_ref[e + 1] - t0
    wslot = e % 2

    def w_copies(ex, slot):
        return [pltpu.make_async_copy(w_ref.at[ex], wstage.at[slot, j], sem_w.at[slot])
                for j, w_ref in enumerate((wg_ref, wu_ref, wd_ref))]

    def in_copy(t, slot):
        rows = pl.ds(pl.multiple_of((t0 + t) * ROW_TILE, ROW_TILE), ROW_TILE)
        return pltpu.make_async_copy(xs_ref.at[rows], xbuf.at[slot], sem_in.at[slot])

    def out_copy(tile, slot):
        rows = pl.ds(pl.multiple_of(tile * ROW_TILE, ROW_TILE), ROW_TILE)
        return pltpu.make_async_copy(ybuf.at[slot], y_ref.at[rows], sem_out.at[slot])

    @pl.when((e == 0) & (nt > 0))
    def _():
        for c in w_copies(0, 0):
            c.start()

    @pl.when(nt > 0)
    def _():
        in_copy(0, 0).start()

    @pl.when(nt > 1)
    def _():
        in_copy(1, 1).start()

    @pl.when(e + 1 < N_EXPERTS)
    def _():
        @pl.when(ts_ref[e + 2] > ts_ref[e + 1])
        def _():
            for c in w_copies(e + 1, 1 - wslot):
                c.start(priority=1)

    @pl.when(nt > 0)
    def _():
        for c in w_copies(e, wslot):
            c.wait()
        wgb[...] = wstage[wslot, 0].astype(BF16)
        wub[...] = wstage[wslot, 1].astype(BF16)
        wdb[...] = wstage[wslot, 2].astype(BF16)
        e_f = e.astype(F32)

        def body(t, carry):
            slot = t % 2
            in_copy(t, slot).wait()

            @pl.when(t >= 2)
            def _():
                out_copy(t0 + t - 2, slot).wait()

            xw = xbuf[slot]
            x = xw[:, :D_MODEL]
            info = xw[:, D_MODEL:].astype(F32)
            w_row = jnp.zeros((ROW_TILE, 1), F32)
            for k in range(TOP_K):
                wk = info[:, TOP_K + k:TOP_K + k + 1] + info[:, 2 * TOP_K + k:2 * TOP_K + k + 1]
                w_row = w_row + jnp.where(info[:, k:k + 1] == e_f, wk, 0.0)
            g = jnp.minimum(_dot(x, wgb[...]) + bg_ref[0], SWIGLU_LIMIT)
            u = jnp.clip(_dot(x, wub[...]) + bu_ref[0], -SWIGLU_LIMIT, SWIGLU_LIMIT)
            act = g * jax.nn.sigmoid(SWIGLU_ALPHA * g) * (u + 1.0)
            ybuf[slot] = ((_dot(act.astype(BF16), wdb[...]) + bd_ref[0]) * w_row).astype(BF16)
            out_copy(t0 + t, slot).start()

            @pl.when(t + 2 < nt)
            def _():
                in_copy(t + 2, slot).start()

            return carry

        lax.fori_loop(0, nt, body, 0)

        @pl.when(nt >= 2)
        def _():
            out_copy(t0 + nt - 2, nt % 2).wait()

        out_copy(t0 + nt - 1, (nt - 1) % 2).wait()

    @pl.when(e == pl.num_programs(0) - 1)
    def _():
        n_used = ts_ref[N_EXPERTS]
        ybuf[0] = jnp.zeros((ROW_TILE, D_MODEL), BF16)

        def zbody(tile, carry):
            c = out_copy(tile, 0)
            c.start()
            c.wait()
            return carry

        lax.fori_loop(n_used, n_tiles_all, zbody, 0)


def _experts(tile_start, xs, w_gate, b_gate, w_up, b_up, w_down, b_down):
    n_rows = xs.shape[0]
    b_spec = pl.BlockSpec((1, 1, D_MODEL), lambda e, ts: (e, 0, 0))
    any_spec = pl.BlockSpec(memory_space=pl.ANY)
    return pl.pallas_call(
        functools.partial(_expert_kernel, n_rows // ROW_TILE),
        out_shape=jax.ShapeDtypeStruct((n_rows, D_MODEL), BF16),
        grid_spec=pltpu.PrefetchScalarGridSpec(
            num_scalar_prefetch=1,
            grid=(N_EXPERTS,),
            in_specs=[any_spec, any_spec, any_spec, any_spec, b_spec, b_spec, b_spec],
            out_specs=any_spec,
            scratch_shapes=[pltpu.VMEM((2, 3, D_MODEL, D_MODEL), F32)]
                           + [pltpu.VMEM((D_MODEL, D_MODEL), BF16)] * 3
                           + [pltpu.VMEM((2, ROW_TILE, ROW_W), BF16), pltpu.VMEM((2, ROW_TILE, D_MODEL), BF16),
                              pltpu.SemaphoreType.DMA((2,)), pltpu.SemaphoreType.DMA((2,)),
                              pltpu.SemaphoreType.DMA((2,))],
        ),
        compiler_params=pltpu.CompilerParams(dimension_semantics=("arbitrary",), vmem_limit_bytes=VMEM_LIMIT),
        name="experts",
    )(tile_start, xs, w_gate, w_up, w_down, b_gate[:, None, :], b_up[:, None, :], b_down[:, None, :])


def _combine_kernel(gdst_ref, gdst_next_ref, lr_ref, y_ref, x1_ref, gt_ref, gf_ref, out_ref, loc_ref, sem):
    i = pl.program_id(0)
    slot = i % 2
    x1 = x1_ref[...]
    sb, l, _ = x1.shape

    @pl.when(i == 0)
    def _():
        for c in _group_copies(loc_ref, slot, y_ref, gdst_ref, sem, False):
            c.start()

    @pl.when(i + 1 < pl.num_programs(0))
    def _():
        for c in _group_copies(loc_ref, 1 - slot, y_ref, gdst_next_ref, sem, False):
            c.start()

    for c in _group_copies(loc_ref, slot, y_ref, gdst_ref, sem, False):
        c.wait()
    lr = lr_ref[...]
    moe = jnp.zeros((sb * l, D_MODEL), F32)
    for c in range(LOCAL_ROWS // MASK_ROWS):
        rows = loc_ref[slot, c * MASK_ROWS:(c + 1) * MASK_ROWS, :]
        moe = moe + lax.dot_general(_sort_matrix(lr, c), rows, (((0,), (0,)), ((), ())),
                                    preferred_element_type=F32)
    x2 = x1 + gt_ref[...] * moe.reshape(sb, l, D_MODEL)
    out_ref[...] = x2 * lax.rsqrt(jnp.mean(x2 * x2, axis=-1, keepdims=True) + EPS) * gf_ref[...]


def _combine(gdst, lr, y, x1, gt, g_final, sb, l, seq_div, blk_off):
    n3 = x1.shape[0]
    nblk = n3 // sb
    tm = sb * l
    assert tm == TOK_TILE
    return pl.pallas_call(
        _combine_kernel,
        out_shape=jax.ShapeDtypeStruct(x1.shape, F32),
        grid=(nblk,),
        in_specs=[pl.BlockSpec((1, 1, GDST_LANES), lambda i: (i + blk_off, 0, 0), memory_space=pltpu.SMEM),
                  pl.BlockSpec((1, 1, GDST_LANES), lambda i: (jnp.minimum(i + 1, nblk - 1) + blk_off, 0, 0),
                               memory_space=pltpu.SMEM),
                  pl.BlockSpec((TOP_K, tm), lambda i: (0, i + blk_off)),
                  pl.BlockSpec(memory_space=pl.ANY),
                  pl.BlockSpec((sb, l, D_MODEL), lambda i: (i, 0, 0)),
                  pl.BlockSpec((sb, 1, D_MODEL), lambda i: (i // seq_div, 0, 0)),
                  pl.BlockSpec((1, D_MODEL), lambda i: (0, 0))],
        out_specs=pl.BlockSpec((sb, l, D_MODEL), lambda i: (i, 0, 0)),
        scratch_shapes=[pltpu.VMEM((2, LOCAL_ROWS, D_MODEL), BF16), pltpu.SemaphoreType.DMA((2,))],
        compiler_params=pltpu.CompilerParams(dimension_semantics=("arbitrary",), vmem_limit_bytes=VMEM_LIMIT),
        name="combine",
    )(gdst, gdst, lr, y, x1, gt, g_final)


def kernel(x_prompt, x_sample, c_prompt, c_sample, state_ssm, state_conv, w_ada, b_ada, g_mix, w_in, g_v_a, w_spatial, b_spatial, g_out_a, conv_w, conv_b, dt_bias, a_log, d_skip, g_out_b, w_out, g_ffn, w_router, b_router, w_gate, b_gate, w_up, b_up, w_down, b_down, g_final):
    assert w_ada.shape[0] == 1, "single-layer step"
    p = dict(w_in=w_in[0], g_mix=g_mix[0], g_v_a=g_v_a[0], w_spatial=w_spatial[0], b_spatial=b_spatial[0],
             g_out_a=g_out_a[0], conv_w=conv_w[0], conv_b=conv_b[0], dt_bias=dt_bias[0], a_log=a_log[0],
             d_skip=d_skip[0], g_out_b=g_out_b[0])
    bp, lp, _ = x_prompt.shape
    bs, ls, _ = x_sample.shape
    tp, ts = bp * lp, bs * ls

    ada = _ada(jnp.concatenate([c_prompt, c_sample], axis=0), w_ada[0], b_ada[0][None, :])
    ada = ada.reshape(bp + bs, 6, 1, D_MODEL)
    ada_p = [ada[:bp, j] for j in range(6)]
    ada_s = [ada[bp:, j] for j in range(6)]

    fw = _front_weights(p)
    mixed_p, conv_p, ssm_p = _prompt_mixer(x_prompt, ada_p[0], ada_p[1], fw, p)
    mixed_s, v_s, conv_s, ssm_s = _sample_mixer(x_sample, ada_s[0], ada_s[1], state_ssm[0], state_conv[0], fw, p)

    w_out_b = w_out[0].astype(BF16)
    g_ffn2 = g_ffn[0][None, :]
    wr_t = w_router[0].T
    wr_hi = wr_t.astype(BF16)
    wr_lo = (wr_t - wr_hi.astype(F32)).astype(BF16)
    br = b_router[0][:, None]
    tps = lp // TOK_TILE
    sbs = TOK_TILE // ls
    xp3 = x_prompt.reshape(bp * tps, TOK_TILE, D_MODEL)
    x1_p, h2p_p, ids_p = _post(mixed_p, xp3, ada_p[2], ada_p[4], ada_p[3], w_out_b, g_ffn2, wr_hi, wr_lo, br,
                               1, TOK_TILE, tps)
    x1_s, h2p_s, ids_s = _post(mixed_s, x_sample, ada_s[2], ada_s[4], ada_s[3], w_out_b, g_ffn2, wr_hi, wr_lo,
                               br, sbs, ls, 1)

    n_blocks = (tp + ts) // TOK_TILE
    max_groups = (tp + ts) * TOP_K // ROW_GROUP + n_blocks * N_EXPERTS + N_EXPERTS * (TILE_GROUPS - 1)
    n_tiles = -(-max_groups // TILE_GROUPS)
    n_tiles_all = n_tiles + 2 * LOCAL_GROUPS // TILE_GROUPS
    lr, gdst, meta = _plan(jnp.concatenate([ids_p, ids_s], axis=1), n_tiles * TILE_GROUPS)

    xs = _dispatch(gdst, meta, lr, h2p_p, h2p_s, n_tiles_all)
    y = _experts(meta[0], xs, w_gate[0], b_gate[0], w_up[0], b_up[0], w_down[0], b_down[0])

    gf = g_final[None, :]
    y_p = _combine(gdst, lr, y, x1_p, ada_p[5], gf, 1, TOK_TILE, tps, 0).reshape(bp, lp, D_MODEL)
    y_s = _combine(gdst, lr, y, x1_s, ada_s[5], gf, sbs, ls, 1, tp // TOK_TILE)

    return (y_p, y_s, ssm_p[None], conv_p[None], ssm_s[None], conv_s[None], v_s.reshape(1, bs, ls, A_WIDTH))
```

```python
import functools
import math

import numpy as np
import jax
import jax.numpy as jnp
from jax import lax
from jax.experimental import pallas as pl
from jax.experimental.pallas import tpu as pltpu

F32 = jnp.float32
BF16 = jnp.bfloat16
I32 = jnp.int32

D_MODEL = 1024
A_WIDTH = 512
A_HEADS = 4
A_HEAD_DIM = 128
CHUNK = 128
B_WIDTH = 512
SSD_HEAD_DIM = 64
SSD_HEADS = 8
SSD_GROUPS = 2
SSD_STATE = 128
GROUP_W = B_WIDTH // SSD_GROUPS
CONV_K = 4
CONV_DIM = 1024
CONV_PAD = 8
N_EXPERTS = 32
TOP_K = 4
SWIGLU_LIMIT = 7.0
SWIGLU_ALPHA = 1.702
EPS = 1e-6
LANES = 128

TOK_TILE = 512
SAMPLE_SEQ_TILE = 16
ROW_TILE = 256
ROW_GROUP = 16
TILE_GROUPS = ROW_TILE // ROW_GROUP
LOCAL_GROUPS = TOK_TILE * TOP_K // ROW_GROUP + N_EXPERTS
LOCAL_ROWS = LOCAL_GROUPS * ROW_GROUP
GDST_LANES = 256
ROW_W = D_MODEL + LANES
INFO_ROWS = 16
MASK_ROWS = 256
META_ROWS = 8
TILE_SLOTS = 4
VMEM_LIMIT = 56 * 1024 * 1024


def _dot(a, b):
    return jnp.dot(a, b, preferred_element_type=F32)


def _dot_nt(a, b):
    return lax.dot_general(a, b, (((1,), (1,)), ((), ())), preferred_element_type=F32)


def _split(x):
    hi = x.astype(BF16)
    lo = (x - hi.astype(F32)).astype(BF16)
    return hi, lo


def _dot_exact_l(t, x):
    hi, lo = _split(x)
    return _dot(t, hi) + _dot(t, lo)


def _dot_exact_r(x, t):
    hi, lo = _split(x)
    return _dot(hi, t) + _dot(lo, t)


def _silu(x):
    return x * jax.nn.sigmoid(x)


def _gelu(x):
    return 0.5 * x * (1.0 + lax.erf(x * (1.0 / math.sqrt(2.0))))


def _softplus(x):
    return jnp.maximum(x, 0.0) + jnp.log1p(jnp.exp(-jnp.abs(x)))


def _rms(x, g):
    return x * lax.rsqrt(jnp.mean(x * x, axis=-1, keepdims=True) + EPS) * g


def _ada_kernel(c_ref, w_ref, b_ref, o_ref):
    s_hi, s_lo = _split(_silu(c_ref[...]))
    w_hi, w_lo = _split(w_ref[...])
    o_ref[...] = _dot(s_hi, w_hi) + _dot(s_lo, w_hi) + _dot(s_hi, w_lo) + b_ref[...]


def _ada(c_all, w_ada, b_ada):
    m = c_all.shape[0]
    n = w_ada.shape[1]
    bn = 512
    return pl.pallas_call(
        _ada_kernel,
        out_shape=jax.ShapeDtypeStruct((m, n), F32),
        grid=(n // bn,),
        in_specs=[pl.BlockSpec((m, D_MODEL), lambda j: (0, 0)),
                  pl.BlockSpec((D_MODEL, bn), lambda j: (0, j)),
                  pl.BlockSpec((1, bn), lambda j: (0, j))],
        out_specs=pl.BlockSpec((m, bn), lambda j: (0, j)),
        compiler_params=pltpu.CompilerParams(dimension_semantics=("arbitrary",), vmem_limit_bytes=VMEM_LIMIT),
        name="ada",
    )(c_all, w_ada, b_ada)


def _mixer_front(x3, sh, sc, prev, refs, xp_ref):
    (g_mix, w_uvz, w_xbc, w_dt, w_dtt, wbd, bias_sp, g_v, g_oa, conv_w, conv_b, dt_bias, dt_bias_t, a_row, a_col) = refs
    sb, l, _ = x3.shape
    tm = sb * l
    xn = x3 * lax.rsqrt(jnp.mean(x3 * x3, axis=-1, keepdims=True) + EPS) * g_mix[...]
    h = (xn * (1.0 + sc) + sh).reshape(tm, D_MODEL)
    hb = h.astype(BF16)
    uvz = _dot(hb, w_uvz[...])
    xbc = _dot(hb, w_xbc[...])
    dt_raw = _dot(hb, w_dt[...])
    dtt_raw = _dot_nt(w_dtt[...], hb)

    u = _gelu(uvz[:, :A_WIDTH])
    vg = _gelu(uvz[:, A_WIDTH:2 * A_WIDTH])
    z = uvz[:, 2 * A_WIDTH:]
    v_parts, s_parts = [], []
    for hd in range(A_HEADS):
        sl = slice(hd * A_HEAD_DIM, (hd + 1) * A_HEAD_DIM)
        vh = _rms(vg[:, sl], g_v[:, sl])
        v_parts.append(vh)
        s_parts.append(_dot(wbd[hd], vh.astype(BF16)))
    v = jnp.concatenate(v_parts, axis=1)
    s_a = jnp.concatenate(s_parts, axis=1) + bias_sp[...]
    out_a = _rms(u * s_a, g_oa[...])

    xp_ref[:, 0:CONV_PAD, :] = prev
    xp_ref[:, CONV_PAD:, :] = xbc.reshape(sb, l, CONV_DIM)
    acc = conv_b[...]
    for k in range(CONV_K):
        off = CONV_PAD - (CONV_K - 1) + k
        acc = acc + xp_ref[:, off:off + l, :] * conv_w[k:k + 1, :]
    xc = _silu(acc).reshape(tm, CONV_DIM)
    dt = _softplus(dt_raw + dt_bias[...])
    dtt = _softplus(dtt_raw + dt_bias_t[...])
    d_a = dt * a_row[...]
    d_at = dtt * a_col[...]
    return out_a, v, z, xc, dt, d_a, d_at


def _ssd_chunk(xs, bm, cm, dt, d_a, d_at, cref):
    tril, triu, ones, expand, mask = cref
    cs = _dot_exact_l(tril[...], d_a)
    cs_t = _dot_exact_r(d_at, triu[...])
    cs_tot = _dot_exact_l(ones[...], d_a)
    vals = jnp.concatenate([dt, jnp.exp(cs_tot - cs), jnp.exp(cs)], axis=0)
    vals_e = _dot_exact_r(vals, expand[...])
    n = xs.shape[0]
    dt_e, dte_e, e_e = vals_e[:n], vals_e[n:2 * n], vals_e[2 * n:]
    xdt = xs * dt_e
    xdtd = xdt * dte_e
    msk = mask[...] > 0.5
    row_lt_half = lax.broadcasted_iota(I32, (2 * n, LANES), 0) < n
    lane_lt_half = lax.broadcasted_iota(I32, (2 * n, LANES), 1) < SSD_HEAD_DIM
    y_parts = []
    for g in range(SSD_GROUPS):
        cb = _dot_nt(cm[:, g * SSD_STATE:(g + 1) * SSD_STATE].astype(BF16),
                     bm[:, g * SSD_STATE:(g + 1) * SSD_STATE].astype(BF16))
        for hp in range(SSD_HEADS // SSD_GROUPS // 2):
            h0 = g * (SSD_HEADS // SSD_GROUPS) + 2 * hp
            ms = []
            for hh in (h0, h0 + 1):
                diff = cs[:, hh:hh + 1] - cs_t[hh:hh + 1, :]
                ms.append((cb * jnp.where(msk, jnp.exp(jnp.where(msk, diff, 0.0)), 0.0)).astype(BF16))
            pair = xdt[:, h0 * SSD_HEAD_DIM:(h0 + 2) * SSD_HEAD_DIM]
            rhs = jnp.where(row_lt_half == lane_lt_half, jnp.concatenate([pair, pair], axis=0), 0.0).astype(BF16)
            y_parts.append(_dot(jnp.concatenate(ms, axis=1), rhs))
    y_diag = jnp.concatenate(y_parts, axis=1)
    return y_diag, e_e, xdtd, cs_tot


def _mixer_back(y, xs, z, out_a, dskip_e, g_ob):
    y = y + xs * dskip_e
    gated = y * _silu(z)
    parts = [_rms(gated[:, g * GROUP_W:(g + 1) * GROUP_W], g_ob[:, g * GROUP_W:(g + 1) * GROUP_W])
             for g in range(SSD_GROUPS)]
    return jnp.concatenate([out_a] + parts, axis=1).astype(BF16)


N_FRONT = 15
N_SSD = 5


def _prompt_mixer_kernel(tiles_per_seq, x_ref, sh_ref, sc_ref, *rest):
    front = rest[:N_FRONT]
    cref = rest[N_FRONT:N_FRONT + N_SSD]
    dskip_e, g_ob = rest[N_FRONT + N_SSD:N_FRONT + N_SSD + 2]
    mixed_ref, conv_out_ref, ssm_out_ref = rest[N_FRONT + N_SSD + 2:N_FRONT + N_SSD + 5]
    xp_ref, carry_ref, st_ref = rest[N_FRONT + N_SSD + 5:]
    i = pl.program_id(0)
    first = (i % tiles_per_seq) == 0

    @pl.when(first)
    def _():
        carry_ref[...] = jnp.zeros_like(carry_ref)
        st_ref[...] = jnp.zeros_like(st_ref)

    x3 = x_ref[...]
    l = x3.shape[1]
    out_a, _, z, xc, dt, d_a, d_at = _mixer_front(x3, sh_ref[...], sc_ref[...], carry_ref[...], front, xp_ref)
    carry_ref[...] = xp_ref[:, l:l + CONV_PAD, :]
    xs = xc[:, :B_WIDTH]
    y_rows = []
    for c in range(l // CHUNK):
        r = slice(c * CHUNK, (c + 1) * CHUNK)
        bm = xc[r, B_WIDTH:B_WIDTH + SSD_GROUPS * SSD_STATE]
        cm = xc[r, B_WIDTH + SSD_GROUPS * SSD_STATE:]
        y_diag, e_e, xdtd, _ = _ssd_chunk(xs[r], bm, cm, dt[r], d_a[r], d_at[:, r], cref)
        st = st_ref[...]
        y_off, upd = [], []
        for g in range(SSD_GROUPS):
            gs = slice(g * GROUP_W, (g + 1) * GROUP_W)
            ns = slice(g * SSD_STATE, (g + 1) * SSD_STATE)
            y_off.append(_dot(cm[:, ns].astype(BF16), st[:, gs].astype(BF16)))
            upd.append(_dot(bm[:, ns].T.astype(BF16), xdtd[:, gs].astype(BF16)))
        y_rows.append(y_diag + jnp.concatenate(y_off, axis=1) * e_e)
        st_ref[...] = st * e_e[CHUNK - 1:CHUNK, :] + jnp.concatenate(upd, axis=1)
    y = jnp.concatenate(y_rows, axis=0)
    mixed_ref[...] = _mixer_back(y, xs, z, out_a, dskip_e[...], g_ob[...])

    @pl.when((i % tiles_per_seq) == tiles_per_seq - 1)
    def _():
        conv_out_ref[...] = xp_ref[:, l + CONV_PAD - (CONV_K - 1):l + CONV_PAD, :]
        ssm_out_ref[0] = st_ref[...].T


def _sample_mixer_kernel(x_ref, sh_ref, sc_ref, prev_ref, ssm0_ref, *rest):
    front = rest[:N_FRONT]
    cref = rest[N_FRONT:N_FRONT + N_SSD]
    dskip_e, g_ob, selseq = rest[N_FRONT + N_SSD:N_FRONT + N_SSD + 3]
    mixed_ref, v_ref, conv_out_ref, ssm_out_ref = rest[N_FRONT + N_SSD + 3:N_FRONT + N_SSD + 7]
    xp_ref, yoff_ref, cbf_ref, bbf_ref, t1_ref, dtab_ref = rest[N_FRONT + N_SSD + 7:]
    x3 = x_ref[...]
    sb, l, _ = x3.shape
    tm = sb * l
    out_a, v, z, xc, dt, d_a, d_at = _mixer_front(x3, sh_ref[...], sc_ref[...], prev_ref[...], front, xp_ref)
    v_ref[...] = v
    conv_out_ref[...] = xp_ref[:, l + CONV_PAD - (CONV_K - 1):l + CONV_PAD, :]
    xs = xc[:, :B_WIDTH]
    bm = xc[:, B_WIDTH:B_WIDTH + SSD_GROUPS * SSD_STATE]
    cm = xc[:, B_WIDTH + SSD_GROUPS * SSD_STATE:]
    y_diag, e_e, xdtd, _ = _ssd_chunk(xs, bm, cm, dt, d_a, d_at, cref)

    e_tot = jnp.exp(_dot_exact_l(selseq[...], d_a))
    for hh in range(SSD_HEADS):
        dtab_ref[hh] = jnp.broadcast_to(e_tot[:, hh:hh + 1], (sb, LANES))
    cbf_ref[...] = cm
    bbf_ref[...] = bm
    for g in range(SSD_GROUPS):
        t1_ref[g] = xdtd[:, g * GROUP_W:(g + 1) * GROUP_W].T.astype(BF16)
    seq_of_row = lax.broadcasted_iota(I32, (tm, SSD_STATE), 0) // l
    heads_per_group = SSD_HEADS // SSD_GROUPS

    def body(j, carry):
        r0 = pl.multiple_of(j * l, l)
        s0 = ssm0_ref[j]
        for g in range(SSD_GROUPS):
            ns = slice(g * SSD_STATE, (g + 1) * SSD_STATE)
            s0g = s0[g * heads_per_group:(g + 1) * heads_per_group].reshape(GROUP_W, SSD_STATE)
            cj = cbf_ref[pl.ds(r0, l), ns].astype(BF16)
            yoff_ref[pl.ds(r0, l), g * GROUP_W:(g + 1) * GROUP_W] = _dot_nt(cj, s0g.astype(BF16))
            bmask = jnp.where(seq_of_row == j, bbf_ref[:, ns], 0.0).astype(BF16)
            upd = _dot(t1_ref[g], bmask)
            for hq in range(heads_per_group):
                hh = g * heads_per_group + hq
                dec = dtab_ref[hh, pl.ds(j, 1), :]
                ssm_out_ref[j, hh] = s0[hh] * dec + upd[hq * SSD_HEAD_DIM:(hq + 1) * SSD_HEAD_DIM]
        return carry

    lax.fori_loop(0, sb, body, 0)
    y = y_diag + yoff_ref[...] * e_e
    mixed_ref[...] = _mixer_back(y, xs, z, out_a, dskip_e[...], g_ob[...])


def _const_spec(a):
    nd = a.ndim
    return pl.BlockSpec(a.shape, lambda i, _nd=nd: (0,) * _nd)


def _spatial_consts(w_spatial, b_spatial, cl, tm):
    w = jnp.where(jnp.tril(jnp.ones((cl, cl), bool)), w_spatial[:, :cl, :cl], 0.0)
    eye = jnp.eye(tm // cl, dtype=F32)
    wbd = jnp.einsum("ab,hts->hatbs", eye, w).reshape(A_HEADS, tm, tm).astype(BF16)
    bias = jnp.tile(jnp.repeat(b_spatial[:, :cl].T, A_HEAD_DIM, axis=1), (tm // cl, 1))
    return wbd, bias


def _ssd_consts(cl):
    r = np.arange(CHUNK)
    same = (r[:, None] // cl) == (r[None, :] // cl)
    tril = same & (r[:, None] >= r[None, :])
    expand = np.zeros((LANES, B_WIDTH), np.float32)
    for hh in range(SSD_HEADS):
        expand[hh, hh * SSD_HEAD_DIM:(hh + 1) * SSD_HEAD_DIM] = 1.0
    return (jnp.asarray(tril, BF16), jnp.asarray(tril.T, BF16), jnp.asarray(same, BF16),
            jnp.asarray(expand, BF16), jnp.asarray(tril, F32))


def _front_weights(p):
    w_in = p["w_in"]
    c0, c1 = 3 * A_WIDTH, 3 * A_WIDTH + CONV_DIM
    w_dt = w_in[:, c1:]
    pad8 = lambda v: jnp.pad(v, (0, LANES - SSD_HEADS))
    a = -jnp.exp(p["a_log"])
    return dict(
        g_mix=p["g_mix"][None, :],
        w_uvz=w_in[:, :c0].astype(BF16),
        w_xbc=w_in[:, c0:c1].astype(BF16),
        w_dt=jnp.pad(w_dt, ((0, 0), (0, LANES - SSD_HEADS))).astype(BF16),
        w_dtt=w_dt.T.astype(BF16),
        g_v=p["g_v_a"][None, :], g_oa=p["g_out_a"][None, :],
        conv_w=p["conv_w"], conv_b=p["conv_b"][None, :],
        dt_bias=pad8(p["dt_bias"])[None, :], dt_bias_t=p["dt_bias"][:, None],
        a_row=pad8(a)[None, :], a_col=a[:, None],
        dskip_e=jnp.repeat(p["d_skip"], SSD_HEAD_DIM)[None, :],
        g_ob=p["g_out_b"][None, :],
    )


def _front_list(fw, wbd, bias_sp):
    return [fw["g_mix"], fw["w_uvz"], fw["w_xbc"], fw["w_dt"], fw["w_dtt"], wbd, bias_sp, fw["g_v"], fw["g_oa"],
            fw["conv_w"], fw["conv_b"], fw["dt_bias"], fw["dt_bias_t"], fw["a_row"], fw["a_col"]]


def _prompt_mixer(x, sh, sc, fw, p):
    nseq, lseq, _ = x.shape
    tps = lseq // TOK_TILE
    nt = nseq * tps
    x4 = x.reshape(nt, TOK_TILE, D_MODEL)
    wbd, bias_sp = _spatial_consts(p["w_spatial"], p["b_spatial"], CHUNK, TOK_TILE)
    consts = _front_list(fw, wbd, bias_sp) + list(_ssd_consts(CHUNK)) + [fw["dskip_e"], fw["g_ob"]]
    seq_spec = pl.BlockSpec((1, 1, D_MODEL), lambda i: (i // tps, 0, 0))
    mixed, conv_new, ssm_new = pl.pallas_call(
        functools.partial(_prompt_mixer_kernel, tps),
        out_shape=(jax.ShapeDtypeStruct((nt * TOK_TILE, D_MODEL), BF16),
                   jax.ShapeDtypeStruct((nseq, CONV_K - 1, CONV_DIM), F32),
                   jax.ShapeDtypeStruct((nseq, B_WIDTH, SSD_STATE), F32)),
        grid=(nt,),
        in_specs=[pl.BlockSpec((1, TOK_TILE, D_MODEL), lambda i: (i, 0, 0)), seq_spec, seq_spec]
                 + [_const_spec(a) for a in consts],
        out_specs=(pl.BlockSpec((TOK_TILE, D_MODEL), lambda i: (i, 0)),
                   pl.BlockSpec((1, CONV_K - 1, CONV_DIM), lambda i: (i // tps, 0, 0)),
                   pl.BlockSpec((1, B_WIDTH, SSD_STATE), lambda i: (i // tps, 0, 0))),
        scratch_shapes=[pltpu.VMEM((1, TOK_TILE + CONV_PAD, CONV_DIM), F32),
                        pltpu.VMEM((1, CONV_PAD, CONV_DIM), F32),
                        pltpu.VMEM((SSD_STATE, B_WIDTH), F32)],
        compiler_params=pltpu.CompilerParams(dimension_semantics=("arbitrary",), vmem_limit_bytes=VMEM_LIMIT),
        name="prompt_mixer",
    )(x4, sh, sc, *consts)
    return mixed, conv_new, ssm_new.reshape(nseq, SSD_HEADS, SSD_HEAD_DIM, SSD_STATE)


def _sample_mixer(x, sh, sc, state_ssm, state_conv, fw, p):
    nseq, l, _ = x.shape
    sb = SAMPLE_SEQ_TILE
    tm = sb * l
    assert tm == CHUNK
    wbd, bias_sp = _spatial_consts(p["w_spatial"], p["b_spatial"], l, tm)
    selseq = jnp.asarray((np.arange(tm)[None, :] // l) == np.arange(sb)[:, None], BF16)
    consts = _front_list(fw, wbd, bias_sp) + list(_ssd_consts(l)) + [fw["dskip_e"], fw["g_ob"], selseq]
    prev = jnp.pad(state_conv, ((0, 0), (CONV_PAD - (CONV_K - 1), 0), (0, 0)))
    seq_spec = pl.BlockSpec((sb, 1, D_MODEL), lambda i: (i, 0, 0))
    ssm_spec = pl.BlockSpec((sb, SSD_HEADS, SSD_HEAD_DIM, SSD_STATE), lambda i: (i, 0, 0, 0))
    return pl.pallas_call(
        _sample_mixer_kernel,
        out_shape=(jax.ShapeDtypeStruct((nseq * l, D_MODEL), BF16),
                   jax.ShapeDtypeStruct((nseq * l, A_WIDTH), F32),
                   jax.ShapeDtypeStruct((nseq, CONV_K - 1, CONV_DIM), F32),
                   jax.ShapeDtypeStruct(state_ssm.shape, F32)),
        grid=(nseq // sb,),
        in_specs=[pl.BlockSpec((sb, l, D_MODEL), lambda i: (i, 0, 0)), seq_spec, seq_spec,
                  pl.BlockSpec((sb, CONV_PAD, CONV_DIM), lambda i: (i, 0, 0)), ssm_spec]
                 + [_const_spec(a) for a in consts],
        out_specs=(pl.BlockSpec((tm, D_MODEL), lambda i: (i, 0)),
                   pl.BlockSpec((tm, A_WIDTH), lambda i: (i, 0)),
                   pl.BlockSpec((sb, CONV_K - 1, CONV_DIM), lambda i: (i, 0, 0)),
                   ssm_spec),
        scratch_shapes=[pltpu.VMEM((sb, l + CONV_PAD, CONV_DIM), F32),
                        pltpu.VMEM((tm, B_WIDTH), F32),
                        pltpu.VMEM((tm, SSD_GROUPS * SSD_STATE), F32),
                        pltpu.VMEM((tm, SSD_GROUPS * SSD_STATE), F32),
                        pltpu.VMEM((SSD_GROUPS, GROUP_W, tm), BF16),
                        pltpu.VMEM((SSD_HEADS, sb, LANES), F32)],
        compiler_params=pltpu.CompilerParams(dimension_semantics=("arbitrary",), vmem_limit_bytes=VMEM_LIMIT),
        name="sample_mixer",
    )(x, sh, sc, prev, state_ssm, *consts)


def _post_kernel(mixed_ref, x_ref, gt_ref, sc_ref, sh_ref, w_out_ref, g_ffn_ref, wr_hi_ref, wr_lo_ref, br_ref,
                 x1_ref, h2p_ref, ids_ref):
    x3 = x_ref[...]
    sb, l, _ = x3.shape
    tm = sb * l
    mix = _dot(mixed_ref[...], w_out_ref[...]).reshape(sb, l, D_MODEL)
    x1 = x3 + gt_ref[...] * mix
    x1_ref[...] = x1
    xn = x1 * lax.rsqrt(jnp.mean(x1 * x1, axis=-1, keepdims=True) + EPS) * g_ffn_ref[...]
    h2 = (xn * (1.0 + sc_ref[...]) + sh_ref[...]).reshape(tm, D_MODEL)
    h_hi, h_lo = _split(h2)
    h2p_ref[:, :D_MODEL] = h_hi
    logits = (_dot_nt(wr_hi_ref[...], h_hi) + _dot_nt(wr_hi_ref[...], h_lo) + _dot_nt(wr_lo_ref[...], h_hi)
              + br_ref[...])
    e_iota = lax.broadcasted_iota(I32, logits.shape, 0)
    vals, idxs = [], []
    for _ in range(TOP_K):
        m = jnp.max(logits, axis=0, keepdims=True)
        idx = jnp.min(jnp.where(logits == m, e_iota, N_EXPERTS), axis=0, keepdims=True)
        vals.append(m)
        idxs.append(idx)
        logits = jnp.where(e_iota == idx, -jnp.inf, logits)
    ex = [jnp.exp(v - vals[0]) for v in vals]
    tot = ex[0] + ex[1] + ex[2] + ex[3]
    ids = jnp.concatenate(idxs, axis=0)
    ids_ref[...] = ids
    wts = jnp.concatenate([e / tot for e in ex], axis=0)
    w_hi = wts.astype(BF16).astype(F32)
    info = jnp.concatenate([ids.astype(F32), w_hi, wts - w_hi, jnp.zeros((TOP_K, tm), F32)], axis=0).astype(BF16)
    r = lax.broadcasted_iota(I32, (INFO_ROWS, LANES), 0)
    c = lax.broadcasted_iota(I32, (INFO_ROWS, LANES), 1)
    place = jnp.where(r == c, 1.0, 0.0).astype(BF16)
    h2p_ref[:, D_MODEL:] = lax.dot_general(info, place, (((0,), (0,)), ((), ())),
                                           preferred_element_type=F32).astype(BF16)


def _post(mixed, x, gt, sc, sh, w_out_b, g_ffn, wr_hi, wr_lo, br, sb, l, seq_div):
    n3, _, _ = x.shape
    nblk = n3 // sb
    tm = sb * l
    t = n3 * l
    ada_spec = pl.BlockSpec((sb, 1, D_MODEL), lambda i: (i // seq_div, 0, 0))
    consts = [w_out_b, g_ffn, wr_hi, wr_lo, br]
    return pl.pallas_call(
        _post_kernel,
        out_shape=(jax.ShapeDtypeStruct(x.shape, F32),
                   jax.ShapeDtypeStruct((t, ROW_W), BF16),
                   jax.ShapeDtypeStruct((TOP_K, t), I32)),
        grid=(nblk,),
        in_specs=[pl.BlockSpec((tm, D_MODEL), lambda i: (i, 0)),
                  pl.BlockSpec((sb, l, D_MODEL), lambda i: (i, 0, 0)), ada_spec, ada_spec, ada_spec]
                 + [_const_spec(a) for a in consts],
        out_specs=(pl.BlockSpec((sb, l, D_MODEL), lambda i: (i, 0, 0)),
                   pl.BlockSpec((tm, ROW_W), lambda i: (i, 0)),
                   pl.BlockSpec((TOP_K, tm), lambda i: (0, i))),
        compiler_params=pltpu.CompilerParams(dimension_semantics=("arbitrary",), vmem_limit_bytes=VMEM_LIMIT),
        name="post",
    )(mixed, x, gt, sc, sh, *consts)


def _strict_upper(n):
    r = lax.broadcasted_iota(I32, (n, n), 0)
    c = lax.broadcasted_iota(I32, (n, n), 1)
    return jnp.where(r < c, 1.0, 0.0).astype(BF16)


def _expert_prefix(col):
    r = lax.broadcasted_iota(I32, (N_EXPERTS, N_EXPERTS), 0)
    c = lax.broadcasted_iota(I32, (N_EXPERTS, N_EXPERTS), 1)
    as_row = jnp.sum(jnp.where(r == c, col, 0.0), axis=0, keepdims=True)
    return jnp.sum(jnp.where(c < r, as_row, 0.0), axis=1, keepdims=True)


def _plan_kernel(dump_group, ids_ref, lr_ref, gdst_ref, tile_e_ref, seg_ref, gb_ref):
    ph = pl.program_id(0)
    b = pl.program_id(1)
    ids = ids_ref[...]
    tm = ids.shape[1]
    e_iota = lax.broadcasted_iota(I32, (N_EXPERTS, tm), 0)
    onehot = [ids[k:k + 1, :] == e_iota for k in range(TOP_K)]
    sel = (onehot[0] | onehot[1]) | (onehot[2] | onehot[3])
    m = jnp.where(sel, 1.0, 0.0)
    rowsum = jnp.sum(m, axis=1, keepdims=True)
    blk_lane = lax.broadcasted_iota(I32, (N_EXPERTS, LANES), 1)

    @pl.when((ph == 0) & (b == 0))
    def _():
        seg_ref[...] = jnp.zeros_like(seg_ref)

    @pl.when(ph == 0)
    def _():
        seg = jnp.ceil(rowsum * (1.0 / ROW_GROUP))
        seg_ref[...] = jnp.where(blk_lane == b, seg, seg_ref[...])

    @pl.when((ph == 1) & (b == 0))
    def _():
        seg = seg_ref[...]
        tot = jnp.sum(seg, axis=1, keepdims=True)
        padded = jnp.ceil(tot * (1.0 / TILE_GROUPS)) * TILE_GROUPS
        gstart = _expert_prefix(padded)
        gb_ref[...] = gstart + _dot(seg.astype(BF16), _strict_upper(LANES))
        r = lax.broadcasted_iota(I32, (N_EXPERTS, LANES), 0)
        c = lax.broadcasted_iota(I32, (N_EXPERTS, LANES), 1)
        as_row = lambda col: jnp.sum(jnp.where(r == c, col, 0.0), axis=0, keepdims=True)
        n_used = jnp.sum(padded, axis=0, keepdims=True)
        lane = lax.broadcasted_iota(I32, (1, LANES), 1)
        tiles = jnp.where(lane == N_EXPERTS, n_used, as_row(gstart)) * (1.0 / TILE_GROUPS)
        meta = jnp.concatenate([tiles, as_row(gstart + tot), as_row(padded - tot),
                                jnp.zeros((META_ROWS - 3, LANES), F32)], axis=0)
        tile_e_ref[...] = meta.astype(I32)

    @pl.when(ph == 1)
    def _():
        seg_b = jnp.sum(jnp.where(blk_lane == b, seg_ref[...], 0.0), axis=1, keepdims=True)
        gb_b = jnp.sum(jnp.where(blk_lane == b, gb_ref[...], 0.0), axis=1, keepdims=True)
        loc_b = _expert_prefix(seg_b)
        before = _dot(m.astype(BF16), _strict_upper(tm)) + loc_b * ROW_GROUP
        lr_ref[...] = jnp.concatenate(
            [jnp.sum(jnp.where(onehot[k], before, 0.0), axis=0, keepdims=True) for k in range(TOP_K)],
            axis=0).astype(I32)
        g = lax.broadcasted_iota(I32, (N_EXPERTS, GDST_LANES), 1).astype(F32)
        inside = (loc_b <= g) & (g < loc_b + seg_b)
        dst = jnp.sum(jnp.where(inside, gb_b + g - loc_b, 0.0), axis=0, keepdims=True)
        used = jnp.sum(jnp.where(inside, 1.0, 0.0), axis=0, keepdims=True) > 0.5
        dump = dump_group + (b % 2).astype(F32) * LOCAL_GROUPS + g[0:1, :]
        gdst_ref[0] = jnp.where(used, dst, dump).astype(I32)


def _plan(ids, dump_group):
    t = ids.shape[1]
    tm = TOK_TILE
    nb = t // tm
    assert nb <= LANES
    return pl.pallas_call(
        functools.partial(_plan_kernel, float(dump_group)),
        out_shape=(jax.ShapeDtypeStruct((TOP_K, t), I32),
                   jax.ShapeDtypeStruct((nb, 1, GDST_LANES), I32),
                   jax.ShapeDtypeStruct((META_ROWS, LANES), I32)),
        grid=(2, nb),
        in_specs=[pl.BlockSpec((TOP_K, tm), lambda ph, b: (0, b))],
        out_specs=(pl.BlockSpec((TOP_K, tm), lambda ph, b: (0, b * ph)),
                   pl.BlockSpec((1, 1, GDST_LANES), lambda ph, b: (b * ph, 0, 0)),
                   pl.BlockSpec((META_ROWS, LANES), lambda ph, b: (0, 0))),
        scratch_shapes=[pltpu.VMEM((N_EXPERTS, LANES), F32), pltpu.VMEM((N_EXPERTS, LANES), F32)],
        compiler_params=pltpu.CompilerParams(dimension_semantics=("arbitrary", "arbitrary")),
        name="plan",
    )(ids)


def _sort_matrix(lr, c):
    r_iota = lax.broadcasted_iota(I32, (MASK_ROWS, lr.shape[1]), 0) + c * MASK_ROWS
    p = jnp.where(r_iota == lr[TOP_K - 1:TOP_K, :], 1.0, 0.0)
    for k in range(TOP_K - 1):
        p = jnp.where(r_iota == lr[k:k + 1, :], 1.0, p)
    return p.astype(BF16)


def _group_copies(loc_ref, slot, far_ref, gdst_ref, sem, to_far):
    copies = []
    for g in range(LOCAL_GROUPS):
        dst = pl.multiple_of(gdst_ref[0, 0, g] * ROW_GROUP, ROW_GROUP)
        near = loc_ref.at[slot, pl.ds(g * ROW_GROUP, ROW_GROUP)]
        far = far_ref.at[pl.ds(dst, ROW_GROUP)]
        copies.append(pltpu.make_async_copy(near, far, sem.at[slot]) if to_far
                      else pltpu.make_async_copy(far, near, sem.at[slot]))
    return copies


def _dispatch_kernel(n_first, n_tiles_all, gdst_ref, meta_ref, lr_ref, ha_ref, hb_ref, xs_ref,
                     h_ref, loc_ref, zero_ref, sem, zsem):
    i = pl.program_id(0)
    last = pl.num_programs(0) - 1
    slot = i % 2

    @pl.when(i < n_first)
    def _():
        h_ref[...] = ha_ref[...]

    @pl.when(i >= n_first)
    def _():
        h_ref[...] = hb_ref[...]

    lr = lr_ref[...]
    for c in range(LOCAL_ROWS // MASK_ROWS):
        loc_ref[slot, c * MASK_ROWS:(c + 1) * MASK_ROWS, :] = _dot(_sort_matrix(lr, c), h_ref[...]).astype(BF16)
    for c in _group_copies(loc_ref, slot, xs_ref, gdst_ref, sem, True):
        c.start()

    @pl.when(i > 0)
    def _():
        for c in _group_copies(loc_ref, 1 - slot, xs_ref, gdst_ref, sem, True):
            c.wait()

    @pl.when(i == last)
    def _():
        for c in _group_copies(loc_ref, slot, xs_ref, gdst_ref, sem, True):
            c.wait()
        zero_ref[...] = jnp.zeros_like(zero_ref)

        def pad_copy(e, j):
            row = pl.multiple_of((meta_ref[1, e] + j) * ROW_GROUP, ROW_GROUP)
            return pltpu.make_async_copy(zero_ref.at[pl.ds(0, ROW_GROUP)], xs_ref.at[pl.ds(row, ROW_GROUP)], zsem)

        def tile_copy(t):
            row = pl.multiple_of(t * ROW_TILE, ROW_TILE)
            return pltpu.make_async_copy(zero_ref, xs_ref.at[pl.ds(row, ROW_TILE)], zsem)

        def pads(fn):
            def body(e, carry):
                for j in range(TILE_GROUPS - 1):
                    @pl.when(j < meta_ref[2, e])
                    def _():
                        fn(pad_copy(e, j))
                return carry
            lax.fori_loop(0, N_EXPERTS, body, 0)

        def tiles(fn):
            def body(t, carry):
                fn(tile_copy(t))
                return carry
            lax.fori_loop(meta_ref[0, N_EXPERTS], n_tiles_all, body, 0)

        pads(lambda c: c.start())
        tiles(lambda c: c.start())
        pads(lambda c: c.wait())
        tiles(lambda c: c.wait())


def _dispatch(gdst, meta, lr, h_a, h_b, n_tiles_all):
    tm = TOK_TILE
    na, nb2 = h_a.shape[0] // tm, h_b.shape[0] // tm
    return pl.pallas_call(
        functools.partial(_dispatch_kernel, na, n_tiles_all),
        out_shape=jax.ShapeDtypeStruct((n_tiles_all * ROW_TILE, ROW_W), BF16),
        grid=(na + nb2,),
        in_specs=[pl.BlockSpec((1, 1, GDST_LANES), lambda i: (i, 0, 0), memory_space=pltpu.SMEM),
                  pl.BlockSpec((META_ROWS, LANES), lambda i: (0, 0), memory_space=pltpu.SMEM),
                  pl.BlockSpec((TOP_K, tm), lambda i: (0, i)),
                  pl.BlockSpec((tm, ROW_W), lambda i: (jnp.minimum(i, na - 1), 0)),
                  pl.BlockSpec((tm, ROW_W), lambda i: (jnp.maximum(i - na, 0), 0))],
        out_specs=pl.BlockSpec(memory_space=pl.ANY),
        scratch_shapes=[pltpu.VMEM((tm, ROW_W), BF16), pltpu.VMEM((2, LOCAL_ROWS, ROW_W), BF16),
                        pltpu.VMEM((ROW_TILE, ROW_W), BF16),
                        pltpu.SemaphoreType.DMA((2,)), pltpu.SemaphoreType.DMA],
        compiler_params=pltpu.CompilerParams(dimension_semantics=("arbitrary",), vmem_limit_bytes=VMEM_LIMIT),
        name="dispatch",
    )(gdst, meta, lr, h_a, h_b)


def _expert_kernel(n_tiles_all, ts_ref, xs_ref, wg_ref, wu_ref, wd_ref, bg_ref, bu_ref, bd_ref, y_ref,
                   wstage, wgb, wub, wdb, xbuf, ybuf, sem_w, sem_in, sem_out):
    e = pl.program_id(0)
    t0 = ts_ref[e]
    nt = ts_ref[e + 1] - t0
    wslot = e % 2

    def w_copies(ex, slot):
        return [pltpu.make_async_copy(w_ref.at[ex], wstage.at[slot, j], sem_w.at[slot])
                for j, w_ref in enumerate((wg_ref, wu_ref, wd_ref))]

    def in_copy(t, slot):
        rows = pl.ds(pl.multiple_of((t0 + t) * ROW_TILE, ROW_TILE), ROW_TILE)
        return pltpu.make_async_copy(xs_ref.at[rows], xbuf.at[slot], sem_in.at[slot])

    def out_copy(tile, slot):
        rows = pl.ds(pl.multiple_of(tile * ROW_TILE, ROW_TILE), ROW_TILE)
        return pltpu.make_async_copy(ybuf.at[slot], y_ref.at[rows], sem_out.at[slot])

    @pl.when((e == 0) & (nt > 0))
    def _():
        for c in w_copies(0, 0):
            c.start()

    for j in range(TILE_SLOTS):
        @pl.when(nt > j)
        def _():
            in_copy(j, j).start()

    @pl.when(e + 1 < N_EXPERTS)
    def _():
        @pl.when(ts_ref[e + 2] > ts_ref[e + 1])
        def _():
            for c in w_copies(e + 1, 1 - wslot):
                c.start(priority=1)

    @pl.when(nt > 0)
    def _():
        for c in w_copies(e, wslot):
            c.wait()
        wgb[...] = wstage[wslot, 0].astype(BF16)
        wub[...] = wstage[wslot, 1].astype(BF16)
        wdb[...] = wstage[wslot, 2].astype(BF16)
        e_f = e.astype(F32)

        def acquire(t):
            slot = t % TILE_SLOTS
            in_copy(t, slot).wait()

            @pl.when(t >= TILE_SLOTS)
            def _():
                out_copy(t0 + t - TILE_SLOTS, slot).wait()

        def compute(t):
            slot = t % TILE_SLOTS
            xw = xbuf[slot]
            x = xw[:, :D_MODEL]
            info = xw[:, D_MODEL:].astype(F32)
            w_row = jnp.zeros((ROW_TILE, 1), F32)
            for k in range(TOP_K):
                wk = info[:, TOP_K + k:TOP_K + k + 1] + info[:, 2 * TOP_K + k:2 * TOP_K + k + 1]
                w_row = w_row + jnp.where(info[:, k:k + 1] == e_f, wk, 0.0)
            g = jnp.minimum(_dot(x, wgb[...]) + bg_ref[0], SWIGLU_LIMIT)
            u = jnp.clip(_dot(x, wub[...]) + bu_ref[0], -SWIGLU_LIMIT, SWIGLU_LIMIT)
            act = g * jax.nn.sigmoid(SWIGLU_ALPHA * g) * (u + 1.0)
            ybuf[slot] = ((_dot(act.astype(BF16), wdb[...]) + bd_ref[0]) * w_row).astype(BF16)

        def release(t):
            slot = t % TILE_SLOTS
            out_copy(t0 + t, slot).start()

            @pl.when(t + TILE_SLOTS < nt)
            def _():
                in_copy(t + TILE_SLOTS, slot).start()

        def pair(p, carry):
            ta, tb = 2 * p, 2 * p + 1
            acquire(ta)
            acquire(tb)
            compute(ta)
            compute(tb)
            release(ta)
            release(tb)
            return carry

        lax.fori_loop(0, nt // 2, pair, 0)

        @pl.when(nt % 2 == 1)
        def _():
            acquire(nt - 1)
            compute(nt - 1)
            release(nt - 1)

        for j in range(1, TILE_SLOTS + 1):
            @pl.when(nt >= j)
            def _():
                out_copy(t0 + nt - j, (nt - j) % TILE_SLOTS).wait()

    @pl.when(e == pl.num_programs(0) - 1)
    def _():
        n_used = ts_ref[N_EXPERTS]
        ybuf[0] = jnp.zeros((ROW_TILE, D_MODEL), BF16)

        def zbody(tile, carry):
            c = out_copy(tile, 0)
            c.start()
            c.wait()
            return carry

        lax.fori_loop(n_used, n_tiles_all, zbody, 0)


def _experts(tile_start, xs, w_gate, b_gate, w_up, b_up, w_down, b_down):
    n_rows = xs.shape[0]
    b_spec = pl.BlockSpec((1, 1, D_MODEL), lambda e, ts: (e, 0, 0))
    any_spec = pl.BlockSpec(memory_space=pl.ANY)
    return pl.pallas_call(
        functools.partial(_expert_kernel, n_rows // ROW_TILE),
        out_shape=jax.ShapeDtypeStruct((n_rows, D_MODEL), BF16),
        grid_spec=pltpu.PrefetchScalarGridSpec(
            num_scalar_prefetch=1,
            grid=(N_EXPERTS,),
            in_specs=[any_spec, any_spec, any_spec, any_spec, b_spec, b_spec, b_spec],
            out_specs=any_spec,
            scratch_shapes=[pltpu.VMEM((2, 3, D_MODEL, D_MODEL), F32)]
                           + [pltpu.VMEM((D_MODEL, D_MODEL), BF16)] * 3
                           + [pltpu.VMEM((TILE_SLOTS, ROW_TILE, ROW_W), BF16),
                              pltpu.VMEM((TILE_SLOTS, ROW_TILE, D_MODEL), BF16),
                              pltpu.SemaphoreType.DMA((2,)), pltpu.SemaphoreType.DMA((TILE_SLOTS,)),
                              pltpu.SemaphoreType.DMA((TILE_SLOTS,))],
        ),
        compiler_params=pltpu.CompilerParams(dimension_semantics=("arbitrary",), vmem_limit_bytes=VMEM_LIMIT),
        name="experts",
    )(tile_start, xs, w_gate, w_up, w_down, b_gate[:, None, :], b_up[:, None, :], b_down[:, None, :])


def _combine_kernel(gdst_ref, gdst_next_ref, lr_ref, y_ref, x1_ref, gt_ref, gf_ref, out_ref, loc_ref, sem):
    i = pl.program_id(0)
    slot = i % 2
    x1 = x1_ref[...]
    sb, l, _ = x1.shape

    @pl.when(i == 0)
    def _():
        for c in _group_copies(loc_ref, slot, y_ref, gdst_ref, sem, False):
            c.start()

    @pl.when(i + 1 < pl.num_programs(0))
    def _():
        for c in _group_copies(loc_ref, 1 - slot, y_ref, gdst_next_ref, sem, False):
            c.start()

    for c in _group_copies(loc_ref, slot, y_ref, gdst_ref, sem, False):
        c.wait()
    lr = lr_ref[...]
    moe = jnp.zeros((sb * l, D_MODEL), F32)
    for c in range(LOCAL_ROWS // MASK_ROWS):
        rows = loc_ref[slot, c * MASK_ROWS:(c + 1) * MASK_ROWS, :]
        moe = moe + lax.dot_general(_sort_matrix(lr, c), rows, (((0,), (0,)), ((), ())),
                                    preferred_element_type=F32)
    x2 = x1 + gt_ref[...] * moe.reshape(sb, l, D_MODEL)
    out_ref[...] = x2 * lax.rsqrt(jnp.mean(x2 * x2, axis=-1, keepdims=True) + EPS) * gf_ref[...]


def _combine(gdst, lr, y, x1, gt, g_final, sb, l, seq_div, blk_off):
    n3 = x1.shape[0]
    nblk = n3 // sb
    tm = sb * l
    assert tm == TOK_TILE
    return pl.pallas_call(
        _combine_kernel,
        out_shape=jax.ShapeDtypeStruct(x1.shape, F32),
        grid=(nblk,),
        in_specs=[pl.BlockSpec((1, 1, GDST_LANES), lambda i: (i + blk_off, 0, 0), memory_space=pltpu.SMEM),
                  pl.BlockSpec((1, 1, GDST_LANES), lambda i: (jnp.minimum(i + 1, nblk - 1) + blk_off, 0, 0),
                               memory_space=pltpu.SMEM),
                  pl.BlockSpec((TOP_K, tm), lambda i: (0, i + blk_off)),
                  pl.BlockSpec(memory_space=pl.ANY),
                  pl.BlockSpec((sb, l, D_MODEL), lambda i: (i, 0, 0)),
                  pl.BlockSpec((sb, 1, D_MODEL), lambda i: (i // seq_div, 0, 0)),
                  pl.BlockSpec((1, D_MODEL), lambda i: (0, 0))],
        out_specs=pl.BlockSpec((sb, l, D_MODEL), lambda i: (i, 0, 0)),
        scratch_shapes=[pltpu.VMEM((2, LOCAL_ROWS, D_MODEL), BF16), pltpu.SemaphoreType.DMA((2,))],
        compiler_params=pltpu.CompilerParams(dimension_semantics=("arbitrary",), vmem_limit_bytes=VMEM_LIMIT),
        name="combine",
    )(gdst, gdst, lr, y, x1, gt, g_final)


def kernel(x_prompt, x_sample, c_prompt, c_sample, state_ssm, state_conv, w_ada, b_ada, g_mix, w_in, g_v_a, w_spatial, b_spatial, g_out_a, conv_w, conv_b, dt_bias, a_log, d_skip, g_out_b, w_out, g_ffn, w_router, b_router, w_gate, b_gate, w_up, b_up, w_down, b_down, g_final):
    assert w_ada.shape[0] == 1, "single-layer step"
    p = dict(w_in=w_in[0], g_mix=g_mix[0], g_v_a=g_v_a[0], w_spatial=w_spatial[0], b_spatial=b_spatial[0],
             g_out_a=g_out_a[0], conv_w=conv_w[0], conv_b=conv_b[0], dt_bias=dt_bias[0], a_log=a_log[0],
             d_skip=d_skip[0], g_out_b=g_out_b[0])
    bp, lp, _ = x_prompt.shape
    bs, ls, _ = x_sample.shape
    tp, ts = bp * lp, bs * ls

    ada = _ada(jnp.concatenate([c_prompt, c_sample], axis=0), w_ada[0], b_ada[0][None, :])
    ada = ada.reshape(bp + bs, 6, 1, D_MODEL)
    ada_p = [ada[:bp, j] for j in range(6)]
    ada_s = [ada[bp:, j] for j in range(6)]

    fw = _front_weights(p)
    mixed_p, conv_p, ssm_p = _prompt_mixer(x_prompt, ada_p[0], ada_p[1], fw, p)
    mixed_s, v_s, conv_s, ssm_s = _sample_mixer(x_sample, ada_s[0], ada_s[1], state_ssm[0], state_conv[0], fw, p)

    w_out_b = w_out[0].astype(BF16)
    g_ffn2 = g_ffn[0][None, :]
    wr_t = w_router[0].T
    wr_hi = wr_t.astype(BF16)
    wr_lo = (wr_t - wr_hi.astype(F32)).astype(BF16)
    br = b_router[0][:, None]
    tps = lp // TOK_TILE
    sbs = TOK_TILE // ls
    xp3 = x_prompt.reshape(bp * tps, TOK_TILE, D_MODEL)
    x1_p, h2p_p, ids_p = _post(mixed_p, xp3, ada_p[2], ada_p[4], ada_p[3], w_out_b, g_ffn2, wr_hi, wr_lo, br,
                               1, TOK_TILE, tps)
    x1_s, h2p_s, ids_s = _post(mixed_s, x_sample, ada_s[2], ada_s[4], ada_s[3], w_out_b, g_ffn2, wr_hi, wr_lo,
                               br, sbs, ls, 1)

    n_blocks = (tp + ts) // TOK_TILE
    max_groups = (tp + ts) * TOP_K // ROW_GROUP + n_blocks * N_EXPERTS + N_EXPERTS * (TILE_GROUPS - 1)
    n_tiles = -(-max_groups // TILE_GROUPS)
    n_tiles_all = n_tiles + 2 * LOCAL_GROUPS // TILE_GROUPS
    lr, gdst, meta = _plan(jnp.concatenate([ids_p, ids_s], axis=1), n_tiles * TILE_GROUPS)

    xs = _dispatch(gdst, meta, lr, h2p_p, h2p_s, n_tiles_all)
    y = _experts(meta[0], xs, w_gate[0], b_gate[0], w_up[0], b_up[0], w_down[0], b_down[0])

    gf = g_final[None, :]
    y_p = _combine(gdst, lr, y, x1_p, ada_p[5], gf, 1, TOK_TILE, tps, 0).reshape(bp, lp, D_MODEL)
    y_s = _combine(gdst, lr, y, x1_s, ada_s[5], gf, sbs, ls, 1, tp // TOK_TILE)

    return (y_p, y_s, ssm_p[None], conv_p[None], ssm_s[None], conv_s[None], v_s.reshape(1, bs, ls, A_WIDTH))
```

```python
import functools
import math

import numpy as np
import jax
import jax.numpy as jnp
from jax import lax
from jax.experimental import pallas as pl
from jax.experimental.pallas import tpu as pltpu

F32 = jnp.float32
BF16 = jnp.bfloat16
I32 = jnp.int32

D_MODEL = 1024
A_WIDTH = 512
A_HEADS = 4
A_HEAD_DIM = 128
CHUNK = 128
B_WIDTH = 512
SSD_HEAD_DIM = 64
SSD_HEADS = 8
SSD_GROUPS = 2
SSD_STATE = 128
GROUP_W = B_WIDTH // SSD_GROUPS
CONV_K = 4
CONV_DIM = 1024
CONV_PAD = 8
N_EXPERTS = 32
TOP_K = 4
SWIGLU_LIMIT = 7.0
SWIGLU_ALPHA = 1.702
EPS = 1e-6
LANES = 128

TOK_TILE = 512
SAMPLE_SEQ_TILE = 16
ROW_TILE = 256
ROW_GROUP = 16
TILE_GROUPS = ROW_TILE // ROW_GROUP
LOCAL_GROUPS = TOK_TILE * TOP_K // ROW_GROUP + N_EXPERTS
LOCAL_ROWS = LOCAL_GROUPS * ROW_GROUP
GDST_LANES = 256
ROW_W = D_MODEL + LANES
INFO_ROWS = 16
MASK_ROWS = 256
META_ROWS = 8
TILE_SLOTS = 4
VMEM_LIMIT = 56 * 1024 * 1024


def _dot(a, b):
    return jnp.dot(a, b, preferred_element_type=F32)


def _dot_nt(a, b):
    return lax.dot_general(a, b, (((1,), (1,)), ((), ())), preferred_element_type=F32)


def _split(x):
    hi = x.astype(BF16)
    lo = (x - hi.astype(F32)).astype(BF16)
    return hi, lo


def _dot_exact_l(t, x):
    hi, lo = _split(x)
    return _dot(t, hi) + _dot(t, lo)


def _dot_exact_r(x, t):
    hi, lo = _split(x)
    return _dot(hi, t) + _dot(lo, t)


def _silu(x):
    return x * jax.nn.sigmoid(x)


def _gelu(x):
    return 0.5 * x * (1.0 + lax.erf(x * (1.0 / math.sqrt(2.0))))


def _softplus(x):
    return jnp.maximum(x, 0.0) + jnp.log1p(jnp.exp(-jnp.abs(x)))


def _rms(x, g):
    return x * lax.rsqrt(jnp.mean(x * x, axis=-1, keepdims=True) + EPS) * g


def _ada_kernel(c_ref, w_ref, b_ref, o_ref):
    s_hi, s_lo = _split(_silu(c_ref[...]))
    w_hi, w_lo = _split(w_ref[...])
    o_ref[...] = _dot(s_hi, w_hi) + _dot(s_lo, w_hi) + _dot(s_hi, w_lo) + b_ref[...]


def _ada(c_all, w_ada, b_ada):
    m = c_all.shape[0]
    n = w_ada.shape[1]
    bn = 512
    return pl.pallas_call(
        _ada_kernel,
        out_shape=jax.ShapeDtypeStruct((m, n), F32),
        grid=(n // bn,),
        in_specs=[pl.BlockSpec((m, D_MODEL), lambda j: (0, 0)),
                  pl.BlockSpec((D_MODEL, bn), lambda j: (0, j)),
                  pl.BlockSpec((1, bn), lambda j: (0, j))],
        out_specs=pl.BlockSpec((m, bn), lambda j: (0, j)),
        compiler_params=pltpu.CompilerParams(dimension_semantics=("arbitrary",), vmem_limit_bytes=VMEM_LIMIT),
        name="ada",
    )(c_all, w_ada, b_ada)


def _mixer_front(x3, sh, sc, prev, refs, xp_ref):
    (g_mix, w_uvz, w_xbc, w_dt, w_dtt, wbd, bias_sp, g_v, g_oa, conv_w, conv_b, dt_bias, dt_bias_t, a_row, a_col) = refs
    sb, l, _ = x3.shape
    tm = sb * l
    xn = x3 * lax.rsqrt(jnp.mean(x3 * x3, axis=-1, keepdims=True) + EPS) * g_mix[...]
    h = (xn * (1.0 + sc) + sh).reshape(tm, D_MODEL)
    hb = h.astype(BF16)
    uvz = _dot(hb, w_uvz[...])
    xbc = _dot(hb, w_xbc[...])
    dt_raw = _dot(hb, w_dt[...])
    dtt_raw = _dot_nt(w_dtt[...], hb)

    u = _gelu(uvz[:, :A_WIDTH])
    vg = _gelu(uvz[:, A_WIDTH:2 * A_WIDTH])
    z = uvz[:, 2 * A_WIDTH:]
    v_parts, s_parts = [], []
    for hd in range(A_HEADS):
        sl = slice(hd * A_HEAD_DIM, (hd + 1) * A_HEAD_DIM)
        vh = _rms(vg[:, sl], g_v[:, sl])
        v_parts.append(vh)
        s_parts.append(_dot(wbd[hd], vh.astype(BF16)))
    v = jnp.concatenate(v_parts, axis=1)
    s_a = jnp.concatenate(s_parts, axis=1) + bias_sp[...]
    out_a = _rms(u * s_a, g_oa[...])

    xp_ref[:, 0:CONV_PAD, :] = prev
    xp_ref[:, CONV_PAD:, :] = xbc.reshape(sb, l, CONV_DIM)
    acc = conv_b[...]
    for k in range(CONV_K):
        off = CONV_PAD - (CONV_K - 1) + k
        acc = acc + xp_ref[:, off:off + l, :] * conv_w[k:k + 1, :]
    xc = _silu(acc).reshape(tm, CONV_DIM)
    dt = _softplus(dt_raw + dt_bias[...])
    dtt = _softplus(dtt_raw + dt_bias_t[...])
    d_a = dt * a_row[...]
    d_at = dtt * a_col[...]
    return out_a, v, z, xc, dt, d_a, d_at


def _ssd_chunk(xs, bm, cm, dt, d_a, d_at, cref):
    tril, triu, ones, expand, mask = cref
    cs = _dot_exact_l(tril[...], d_a)
    cs_t = _dot_exact_r(d_at, triu[...])
    cs_tot = _dot_exact_l(ones[...], d_a)
    vals = jnp.concatenate([dt, jnp.exp(cs_tot - cs), jnp.exp(cs)], axis=0)
    vals_e = _dot_exact_r(vals, expand[...])
    n = xs.shape[0]
    dt_e, dte_e, e_e = vals_e[:n], vals_e[n:2 * n], vals_e[2 * n:]
    xdt = xs * dt_e
    xdtd = xdt * dte_e
    msk = mask[...] > 0.5
    row_lt_half = lax.broadcasted_iota(I32, (2 * n, LANES), 0) < n
    lane_lt_half = lax.broadcasted_iota(I32, (2 * n, LANES), 1) < SSD_HEAD_DIM
    y_parts = []
    for g in range(SSD_GROUPS):
        cb = _dot_nt(cm[:, g * SSD_STATE:(g + 1) * SSD_STATE].astype(BF16),
                     bm[:, g * SSD_STATE:(g + 1) * SSD_STATE].astype(BF16))
        for hp in range(SSD_HEADS // SSD_GROUPS // 2):
            h0 = g * (SSD_HEADS // SSD_GROUPS) + 2 * hp
            ms = []
            for hh in (h0, h0 + 1):
                diff = cs[:, hh:hh + 1] - cs_t[hh:hh + 1, :]
                ms.append((cb * jnp.where(msk, jnp.exp(jnp.where(msk, diff, 0.0)), 0.0)).astype(BF16))
            pair = xdt[:, h0 * SSD_HEAD_DIM:(h0 + 2) * SSD_HEAD_DIM]
            rhs = jnp.where(row_lt_half == lane_lt_half, jnp.concatenate([pair, pair], axis=0), 0.0).astype(BF16)
            y_parts.append(_dot(jnp.concatenate(ms, axis=1), rhs))
    y_diag = jnp.concatenate(y_parts, axis=1)
    return y_diag, e_e, xdtd, cs_tot


def _mixer_back(y, xs, z, out_a, dskip_e, g_ob):
    y = y + xs * dskip_e
    gated = y * _silu(z)
    parts = [_rms(gated[:, g * GROUP_W:(g + 1) * GROUP_W], g_ob[:, g * GROUP_W:(g + 1) * GROUP_W])
             for g in range(SSD_GROUPS)]
    return jnp.concatenate([out_a] + parts, axis=1).astype(BF16)


N_FRONT = 15
N_SSD = 5


def _prompt_mixer_kernel(tiles_per_seq, x_ref, sh_ref, sc_ref, *rest):
    front = rest[:N_FRONT]
    cref = rest[N_FRONT:N_FRONT + N_SSD]
    dskip_e, g_ob = rest[N_FRONT + N_SSD:N_FRONT + N_SSD + 2]
    mixed_ref, conv_out_ref, ssm_out_ref = rest[N_FRONT + N_SSD + 2:N_FRONT + N_SSD + 5]
    xp_ref, carry_ref, st_ref = rest[N_FRONT + N_SSD + 5:]
    i = pl.program_id(0)
    first = (i % tiles_per_seq) == 0

    @pl.when(first)
    def _():
        carry_ref[...] = jnp.zeros_like(carry_ref)
        st_ref[...] = jnp.zeros_like(st_ref)

    x3 = x_ref[...]
    l = x3.shape[1]
    out_a, _, z, xc, dt, d_a, d_at = _mixer_front(x3, sh_ref[...], sc_ref[...], carry_ref[...], front, xp_ref)
    carry_ref[...] = xp_ref[:, l:l + CONV_PAD, :]
    xs = xc[:, :B_WIDTH]
    y_rows = []
    for c in range(l // CHUNK):
        r = slice(c * CHUNK, (c + 1) * CHUNK)
        bm = xc[r, B_WIDTH:B_WIDTH + SSD_GROUPS * SSD_STATE]
        cm = xc[r, B_WIDTH + SSD_GROUPS * SSD_STATE:]
        y_diag, e_e, xdtd, _ = _ssd_chunk(xs[r], bm, cm, dt[r], d_a[r], d_at[:, r], cref)
        st = st_ref[...]
        y_off, upd = [], []
        for g in range(SSD_GROUPS):
            gs = slice(g * GROUP_W, (g + 1) * GROUP_W)
            ns = slice(g * SSD_STATE, (g + 1) * SSD_STATE)
            y_off.append(_dot(cm[:, ns].astype(BF16), st[:, gs].astype(BF16)))
            upd.append(_dot(bm[:, ns].T.astype(BF16), xdtd[:, gs].astype(BF16)))
        y_rows.append(y_diag + jnp.concatenate(y_off, axis=1) * e_e)
        st_ref[...] = st * e_e[CHUNK - 1:CHUNK, :] + jnp.concatenate(upd, axis=1)
    y = jnp.concatenate(y_rows, axis=0)
    mixed_ref[...] = _mixer_back(y, xs, z, out_a, dskip_e[...], g_ob[...])

    @pl.when((i % tiles_per_seq) == tiles_per_seq - 1)
    def _():
        conv_out_ref[...] = xp_ref[:, l + CONV_PAD - (CONV_K - 1):l + CONV_PAD, :]
        ssm_out_ref[0] = st_ref[...].T


def _sample_mixer_kernel(x_ref, sh_ref, sc_ref, prev_ref, ssm0_ref, *rest):
    front = rest[:N_FRONT]
    cref = rest[N_FRONT:N_FRONT + N_SSD]
    dskip_e, g_ob, selseq = rest[N_FRONT + N_SSD:N_FRONT + N_SSD + 3]
    mixed_ref, v_ref, conv_out_ref, ssm_out_ref = rest[N_FRONT + N_SSD + 3:N_FRONT + N_SSD + 7]
    xp_ref, yoff_ref, cbf_ref, bbf_ref, t1_ref, dtab_ref = rest[N_FRONT + N_SSD + 7:]
    x3 = x_ref[...]
    sb, l, _ = x3.shape
    tm = sb * l
    out_a, v, z, xc, dt, d_a, d_at = _mixer_front(x3, sh_ref[...], sc_ref[...], prev_ref[...], front, xp_ref)
    v_ref[...] = v
    conv_out_ref[...] = xp_ref[:, l + CONV_PAD - (CONV_K - 1):l + CONV_PAD, :]
    xs = xc[:, :B_WIDTH]
    bm = xc[:, B_WIDTH:B_WIDTH + SSD_GROUPS * SSD_STATE]
    cm = xc[:, B_WIDTH + SSD_GROUPS * SSD_STATE:]
    y_diag, e_e, xdtd, _ = _ssd_chunk(xs, bm, cm, dt, d_a, d_at, cref)

    e_tot = jnp.exp(_dot_exact_l(selseq[...], d_a))
    for hh in range(SSD_HEADS):
        dtab_ref[hh] = jnp.broadcast_to(e_tot[:, hh:hh + 1], (sb, LANES))
    cbf_ref[...] = cm
    bbf_ref[...] = bm
    for g in range(SSD_GROUPS):
        t1_ref[g] = xdtd[:, g * GROUP_W:(g + 1) * GROUP_W].T.astype(BF16)
    seq_of_row = lax.broadcasted_iota(I32, (tm, SSD_STATE), 0) // l
    heads_per_group = SSD_HEADS // SSD_GROUPS

    def body(j, carry):
        r0 = pl.multiple_of(j * l, l)
        s0 = ssm0_ref[j]
        for g in range(SSD_GROUPS):
            ns = slice(g * SSD_STATE, (g + 1) * SSD_STATE)
            s0g = s0[g * heads_per_group:(g + 1) * heads_per_group].reshape(GROUP_W, SSD_STATE)
            cj = cbf_ref[pl.ds(r0, l), ns].astype(BF16)
            yoff_ref[pl.ds(r0, l), g * GROUP_W:(g + 1) * GROUP_W] = _dot_nt(cj, s0g.astype(BF16))
            bmask = jnp.where(seq_of_row == j, bbf_ref[:, ns], 0.0).astype(BF16)
            upd = _dot(t1_ref[g], bmask)
            for hq in range(heads_per_group):
                hh = g * heads_per_group + hq
                dec = dtab_ref[hh, pl.ds(j, 1), :]
                ssm_out_ref[j, hh] = s0[hh] * dec + upd[hq * SSD_HEAD_DIM:(hq + 1) * SSD_HEAD_DIM]
        return carry

    lax.fori_loop(0, sb, body, 0)
    y = y_diag + yoff_ref[...] * e_e
    mixed_ref[...] = _mixer_back(y, xs, z, out_a, dskip_e[...], g_ob[...])


def _const_spec(a):
    nd = a.ndim
    return pl.BlockSpec(a.shape, lambda i, _nd=nd: (0,) * _nd)


def _spatial_consts(w_spatial, b_spatial, cl, tm):
    w = jnp.where(jnp.tril(jnp.ones((cl, cl), bool)), w_spatial[:, :cl, :cl], 0.0)
    eye = jnp.eye(tm // cl, dtype=F32)
    wbd = jnp.einsum("ab,hts->hatbs", eye, w).reshape(A_HEADS, tm, tm).astype(BF16)
    bias = jnp.tile(jnp.repeat(b_spatial[:, :cl].T, A_HEAD_DIM, axis=1), (tm // cl, 1))
    return wbd, bias


def _ssd_consts(cl):
    r = np.arange(CHUNK)
    same = (r[:, None] // cl) == (r[None, :] // cl)
    tril = same & (r[:, None] >= r[None, :])
    expand = np.zeros((LANES, B_WIDTH), np.float32)
    for hh in range(SSD_HEADS):
        expand[hh, hh * SSD_HEAD_DIM:(hh + 1) * SSD_HEAD_DIM] = 1.0
    return (jnp.asarray(tril, BF16), jnp.asarray(tril.T, BF16), jnp.asarray(same, BF16),
            jnp.asarray(expand, BF16), jnp.asarray(tril, F32))


def _front_weights(p):
    w_in = p["w_in"]
    c0, c1 = 3 * A_WIDTH, 3 * A_WIDTH + CONV_DIM
    w_dt = w_in[:, c1:]
    pad8 = lambda v: jnp.pad(v, (0, LANES - SSD_HEADS))
    a = -jnp.exp(p["a_log"])
    return dict(
        g_mix=p["g_mix"][None, :],
        w_uvz=w_in[:, :c0].astype(BF16),
        w_xbc=w_in[:, c0:c1].astype(BF16),
        w_dt=jnp.pad(w_dt, ((0, 0), (0, LANES - SSD_HEADS))).astype(BF16),
        w_dtt=w_dt.T.astype(BF16),
        g_v=p["g_v_a"][None, :], g_oa=p["g_out_a"][None, :],
        conv_w=p["conv_w"], conv_b=p["conv_b"][None, :],
        dt_bias=pad8(p["dt_bias"])[None, :], dt_bias_t=p["dt_bias"][:, None],
        a_row=pad8(a)[None, :], a_col=a[:, None],
        dskip_e=jnp.repeat(p["d_skip"], SSD_HEAD_DIM)[None, :],
        g_ob=p["g_out_b"][None, :],
    )


def _front_list(fw, wbd, bias_sp):
    return [fw["g_mix"], fw["w_uvz"], fw["w_xbc"], fw["w_dt"], fw["w_dtt"], wbd, bias_sp, fw["g_v"], fw["g_oa"],
            fw["conv_w"], fw["conv_b"], fw["dt_bias"], fw["dt_bias_t"], fw["a_row"], fw["a_col"]]


def _prompt_mixer(x, sh, sc, fw, p):
    nseq, lseq, _ = x.shape
    tps = lseq // TOK_TILE
    nt = nseq * tps
    x4 = x.reshape(nt, TOK_TILE, D_MODEL)
    wbd, bias_sp = _spatial_consts(p["w_spatial"], p["b_spatial"], CHUNK, TOK_TILE)
    consts = _front_list(fw, wbd, bias_sp) + list(_ssd_consts(CHUNK)) + [fw["dskip_e"], fw["g_ob"]]
    seq_spec = pl.BlockSpec((1, 1, D_MODEL), lambda i: (i // tps, 0, 0))
    mixed, conv_new, ssm_new = pl.pallas_call(
        functools.partial(_prompt_mixer_kernel, tps),
        out_shape=(jax.ShapeDtypeStruct((nt * TOK_TILE, D_MODEL), BF16),
                   jax.ShapeDtypeStruct((nseq, CONV_K - 1, CONV_DIM), F32),
                   jax.ShapeDtypeStruct((nseq, B_WIDTH, SSD_STATE), F32)),
        grid=(nt,),
        in_specs=[pl.BlockSpec((1, TOK_TILE, D_MODEL), lambda i: (i, 0, 0)), seq_spec, seq_spec]
                 + [_const_spec(a) for a in consts],
        out_specs=(pl.BlockSpec((TOK_TILE, D_MODEL), lambda i: (i, 0)),
                   pl.BlockSpec((1, CONV_K - 1, CONV_DIM), lambda i: (i // tps, 0, 0)),
                   pl.BlockSpec((1, B_WIDTH, SSD_STATE), lambda i: (i // tps, 0, 0))),
        scratch_shapes=[pltpu.VMEM((1, TOK_TILE + CONV_PAD, CONV_DIM), F32),
                        pltpu.VMEM((1, CONV_PAD, CONV_DIM), F32),
                        pltpu.VMEM((SSD_STATE, B_WIDTH), F32)],
        compiler_params=pltpu.CompilerParams(dimension_semantics=("arbitrary",), vmem_limit_bytes=VMEM_LIMIT),
        name="prompt_mixer",
    )(x4, sh, sc, *consts)
    return mixed, conv_new, ssm_new.reshape(nseq, SSD_HEADS, SSD_HEAD_DIM, SSD_STATE)


def _sample_mixer(x, sh, sc, state_ssm, state_conv, fw, p):
    nseq, l, _ = x.shape
    sb = SAMPLE_SEQ_TILE
    tm = sb * l
    assert tm == CHUNK
    wbd, bias_sp = _spatial_consts(p["w_spatial"], p["b_spatial"], l, tm)
    selseq = jnp.asarray((np.arange(tm)[None, :] // l) == np.arange(sb)[:, None], BF16)
    consts = _front_list(fw, wbd, bias_sp) + list(_ssd_consts(l)) + [fw["dskip_e"], fw["g_ob"], selseq]
    prev = jnp.pad(state_conv, ((0, 0), (CONV_PAD - (CONV_K - 1), 0), (0, 0)))
    seq_spec = pl.BlockSpec((sb, 1, D_MODEL), lambda i: (i, 0, 0))
    ssm_spec = pl.BlockSpec((sb, SSD_HEADS, SSD_HEAD_DIM, SSD_STATE), lambda i: (i, 0, 0, 0))
    return pl.pallas_call(
        _sample_mixer_kernel,
        out_shape=(jax.ShapeDtypeStruct((nseq * l, D_MODEL), BF16),
                   jax.ShapeDtypeStruct((nseq * l, A_WIDTH), F32),
                   jax.ShapeDtypeStruct((nseq, CONV_K - 1, CONV_DIM), F32),
                   jax.ShapeDtypeStruct(state_ssm.shape, F32)),
        grid=(nseq // sb,),
        in_specs=[pl.BlockSpec((sb, l, D_MODEL), lambda i: (i, 0, 0)), seq_spec, seq_spec,
                  pl.BlockSpec((sb, CONV_PAD, CONV_DIM), lambda i: (i, 0, 0)), ssm_spec]
                 + [_const_spec(a) for a in consts],
        out_specs=(pl.BlockSpec((tm, D_MODEL), lambda i: (i, 0)),
                   pl.BlockSpec((tm, A_WIDTH), lambda i: (i, 0)),
                   pl.BlockSpec((sb, CONV_K - 1, CONV_DIM), lambda i: (i, 0, 0)),
                   ssm_spec),
        scratch_shapes=[pltpu.VMEM((sb, l + CONV_PAD, CONV_DIM), F32),
                        pltpu.VMEM((tm, B_WIDTH), F32),
                        pltpu.VMEM((tm, SSD_GROUPS * SSD_STATE), F32),
                        pltpu.VMEM((tm, SSD_GROUPS * SSD_STATE), F32),
                        pltpu.VMEM((SSD_GROUPS, GROUP_W, tm), BF16),
                        pltpu.VMEM((SSD_HEADS, sb, LANES), F32)],
        compiler_params=pltpu.CompilerParams(dimension_semantics=("arbitrary",), vmem_limit_bytes=VMEM_LIMIT),
        name="sample_mixer",
    )(x, sh, sc, prev, state_ssm, *consts)


def _post_kernel(mixed_ref, x_ref, gt_ref, sc_ref, sh_ref, w_out_ref, g_ffn_ref, wr_hi_ref, wr_lo_ref, br_ref,
                 x1_ref, h2p_ref, ids_ref):
    x3 = x_ref[...]
    sb, l, _ = x3.shape
    tm = sb * l
    mix = _dot(mixed_ref[...], w_out_ref[...]).reshape(sb, l, D_MODEL)
    x1 = x3 + gt_ref[...] * mix
    x1_ref[...] = x1
    xn = x1 * lax.rsqrt(jnp.mean(x1 * x1, axis=-1, keepdims=True) + EPS) * g_ffn_ref[...]
    h2 = (xn * (1.0 + sc_ref[...]) + sh_ref[...]).reshape(tm, D_MODEL)
    h_hi, h_lo = _split(h2)
    h2p_ref[:, :D_MODEL] = h_hi
    logits = (_dot_nt(wr_hi_ref[...], h_hi) + _dot_nt(wr_hi_ref[...], h_lo) + _dot_nt(wr_lo_ref[...], h_hi)
              + br_ref[...])
    e_iota = lax.broadcasted_iota(I32, logits.shape, 0)
    vals, idxs = [], []
    for _ in range(TOP_K):
        m = jnp.max(logits, axis=0, keepdims=True)
        idx = jnp.min(jnp.where(logits == m, e_iota, N_EXPERTS), axis=0, keepdims=True)
        vals.append(m)
        idxs.append(idx)
        logits = jnp.where(e_iota == idx, -jnp.inf, logits)
    ex = [jnp.exp(v - vals[0]) for v in vals]
    tot = ex[0] + ex[1] + ex[2] + ex[3]
    ids = jnp.concatenate(idxs, axis=0)
    ids_ref[...] = ids
    wts = jnp.concatenate([e / tot for e in ex], axis=0)
    w_hi = wts.astype(BF16).astype(F32)
    info = jnp.concatenate([ids.astype(F32), w_hi, wts - w_hi, jnp.zeros((TOP_K, tm), F32)], axis=0).astype(BF16)
    r = lax.broadcasted_iota(I32, (INFO_ROWS, LANES), 0)
    c = lax.broadcasted_iota(I32, (INFO_ROWS, LANES), 1)
    place = jnp.where(r == c, 1.0, 0.0).astype(BF16)
    h2p_ref[:, D_MODEL:] = lax.dot_general(info, place, (((0,), (0,)), ((), ())),
                                           preferred_element_type=F32).astype(BF16)


def _post(mixed, x, gt, sc, sh, w_out_b, g_ffn, wr_hi, wr_lo, br, sb, l, seq_div):
    n3, _, _ = x.shape
    nblk = n3 // sb
    tm = sb * l
    t = n3 * l
    ada_spec = pl.BlockSpec((sb, 1, D_MODEL), lambda i: (i // seq_div, 0, 0))
    consts = [w_out_b, g_ffn, wr_hi, wr_lo, br]
    return pl.pallas_call(
        _post_kernel,
        out_shape=(jax.ShapeDtypeStruct(x.shape, F32),
                   jax.ShapeDtypeStruct((t, ROW_W), BF16),
                   jax.ShapeDtypeStruct((TOP_K, t), I32)),
        grid=(nblk,),
        in_specs=[pl.BlockSpec((tm, D_MODEL), lambda i: (i, 0)),
                  pl.BlockSpec((sb, l, D_MODEL), lambda i: (i, 0, 0)), ada_spec, ada_spec, ada_spec]
                 + [_const_spec(a) for a in consts],
        out_specs=(pl.BlockSpec((sb, l, D_MODEL), lambda i: (i, 0, 0)),
                   pl.BlockSpec((tm, ROW_W), lambda i: (i, 0)),
                   pl.BlockSpec((TOP_K, tm), lambda i: (0, i))),
        compiler_params=pltpu.CompilerParams(dimension_semantics=("arbitrary",), vmem_limit_bytes=VMEM_LIMIT),
        name="post",
    )(mixed, x, gt, sc, sh, *consts)


def _strict_upper(n):
    r = lax.broadcasted_iota(I32, (n, n), 0)
    c = lax.broadcasted_iota(I32, (n, n), 1)
    return jnp.where(r < c, 1.0, 0.0).astype(BF16)


def _expert_prefix(col):
    r = lax.broadcasted_iota(I32, (N_EXPERTS, N_EXPERTS), 0)
    c = lax.broadcasted_iota(I32, (N_EXPERTS, N_EXPERTS), 1)
    as_row = jnp.sum(jnp.where(r == c, col, 0.0), axis=0, keepdims=True)
    return jnp.sum(jnp.where(c < r, as_row, 0.0), axis=1, keepdims=True)


def _plan_kernel(dump_group, ids_ref, lr_ref, gdst_ref, tile_e_ref):
    tm = TOK_TILE
    nb = ids_ref.shape[1] // tm
    blk_lane = lax.broadcasted_iota(I32, (N_EXPERTS, LANES), 1)

    def block_masks(b):
        ids = ids_ref[:, pl.ds(pl.multiple_of(b * tm, tm), tm)]
        e_iota = lax.broadcasted_iota(I32, (N_EXPERTS, tm), 0)
        onehot = [ids[k:k + 1, :] == e_iota for k in range(TOP_K)]
        sel = (onehot[0] | onehot[1]) | (onehot[2] | onehot[3])
        return onehot, jnp.where(sel, 1.0, 0.0)

    def count(b, seg):
        _, m = block_masks(b)
        seg_b = jnp.ceil(jnp.sum(m, axis=1, keepdims=True) * (1.0 / ROW_GROUP))
        return jnp.where(blk_lane == b, seg_b, seg)

    seg = lax.fori_loop(0, nb, count, jnp.zeros((N_EXPERTS, LANES), F32))
    tot = jnp.sum(seg, axis=1, keepdims=True)
    padded = jnp.ceil(tot * (1.0 / TILE_GROUPS)) * TILE_GROUPS
    gstart = _expert_prefix(padded)
    gb = gstart + _dot(seg.astype(BF16), _strict_upper(LANES))
    r = lax.broadcasted_iota(I32, (N_EXPERTS, LANES), 0)
    as_row = lambda col: jnp.sum(jnp.where(r == blk_lane, col, 0.0), axis=0, keepdims=True)
    n_used = jnp.sum(padded, axis=0, keepdims=True)
    lane = lax.broadcasted_iota(I32, (1, LANES), 1)
    tiles = jnp.where(lane == N_EXPERTS, n_used, as_row(gstart)) * (1.0 / TILE_GROUPS)
    meta = jnp.concatenate([tiles, as_row(gstart + tot), as_row(padded - tot),
                            jnp.zeros((META_ROWS - 3, LANES), F32)], axis=0)
    tile_e_ref[...] = meta.astype(I32)
    upper = _strict_upper(tm)

    def place(b, carry):
        onehot, m = block_masks(b)
        seg_b = jnp.sum(jnp.where(blk_lane == b, seg, 0.0), axis=1, keepdims=True)
        gb_b = jnp.sum(jnp.where(blk_lane == b, gb, 0.0), axis=1, keepdims=True)
        loc_b = _expert_prefix(seg_b)
        before = _dot(m.astype(BF16), upper) + loc_b * ROW_GROUP
        lr_ref[:, pl.ds(pl.multiple_of(b * tm, tm), tm)] = jnp.concatenate(
            [jnp.sum(jnp.where(onehot[k], before, 0.0), axis=0, keepdims=True) for k in range(TOP_K)],
            axis=0).astype(I32)
        g = lax.broadcasted_iota(I32, (N_EXPERTS, GDST_LANES), 1).astype(F32)
        inside = (loc_b <= g) & (g < loc_b + seg_b)
        dst = jnp.sum(jnp.where(inside, gb_b + g - loc_b, 0.0), axis=0, keepdims=True)
        used = jnp.sum(jnp.where(inside, 1.0, 0.0), axis=0, keepdims=True) > 0.5
        dump = dump_group + lax.convert_element_type(b % 2, F32) * LOCAL_GROUPS + g[0:1, :]
        gdst_ref[b] = jnp.where(used, dst, dump).astype(I32)
        return carry

    lax.fori_loop(0, nb, place, 0)


def _plan(ids, dump_group):
    t = ids.shape[1]
    nb = t // TOK_TILE
    assert nb <= LANES
    return pl.pallas_call(
        functools.partial(_plan_kernel, float(dump_group)),
        out_shape=(jax.ShapeDtypeStruct((TOP_K, t), I32),
                   jax.ShapeDtypeStruct((nb, 1, GDST_LANES), I32),
                   jax.ShapeDtypeStruct((META_ROWS, LANES), I32)),
        name="plan",
    )(ids)


def _sort_matrix(lr, c):
    r_iota = lax.broadcasted_iota(I32, (MASK_ROWS, lr.shape[1]), 0) + c * MASK_ROWS
    p = jnp.where(r_iota == lr[TOP_K - 1:TOP_K, :], 1.0, 0.0)
    for k in range(TOP_K - 1):
        p = jnp.where(r_iota == lr[k:k + 1, :], 1.0, p)
    return p.astype(BF16)


def _group_copies(loc_ref, slot, far_ref, gdst_ref, sem, to_far):
    copies = []
    for g in range(LOCAL_GROUPS):
        dst = pl.multiple_of(gdst_ref[0, 0, g] * ROW_GROUP, ROW_GROUP)
        near = loc_ref.at[slot, pl.ds(g * ROW_GROUP, ROW_GROUP)]
        far = far_ref.at[pl.ds(dst, ROW_GROUP)]
        copies.append(pltpu.make_async_copy(near, far, sem.at[slot]) if to_far
                      else pltpu.make_async_copy(far, near, sem.at[slot]))
    return copies


def _dispatch_kernel(n_first, n_tiles_all, gdst_ref, meta_ref, lr_ref, ha_ref, hb_ref, xs_ref,
                     h_ref, loc_ref, zero_ref, sem, zsem):
    i = pl.program_id(0)
    last = pl.num_programs(0) - 1
    slot = i % 2

    @pl.when(i < n_first)
    def _():
        h_ref[...] = ha_ref[...]

    @pl.when(i >= n_first)
    def _():
        h_ref[...] = hb_ref[...]

    lr = lr_ref[...]
    copies = _group_copies(loc_ref, slot, xs_ref, gdst_ref, sem, True)
    n_chunks = LOCAL_ROWS // MASK_ROWS
    per_chunk = LOCAL_GROUPS // n_chunks
    for c in range(n_chunks):
        loc_ref[slot, c * MASK_ROWS:(c + 1) * MASK_ROWS, :] = _dot(_sort_matrix(lr, c), h_ref[...]).astype(BF16)
        for cp in copies[c * per_chunk:(c + 1) * per_chunk]:
            cp.start()

    @pl.when(i > 0)
    def _():
        for c in _group_copies(loc_ref, 1 - slot, xs_ref, gdst_ref, sem, True):
            c.wait()

    @pl.when(i == last)
    def _():
        for c in _group_copies(loc_ref, slot, xs_ref, gdst_ref, sem, True):
            c.wait()
        zero_ref[...] = jnp.zeros_like(zero_ref)

        def pad_copy(e, j):
            row = pl.multiple_of((meta_ref[1, e] + j) * ROW_GROUP, ROW_GROUP)
            return pltpu.make_async_copy(zero_ref.at[pl.ds(0, ROW_GROUP)], xs_ref.at[pl.ds(row, ROW_GROUP)], zsem)

        def tile_copy(t):
            row = pl.multiple_of(t * ROW_TILE, ROW_TILE)
            return pltpu.make_async_copy(zero_ref, xs_ref.at[pl.ds(row, ROW_TILE)], zsem)

        def pads(fn):
            def body(e, carry):
                for j in range(TILE_GROUPS - 1):
                    @pl.when(j < meta_ref[2, e])
                    def _():
                        fn(pad_copy(e, j))
                return carry
            lax.fori_loop(0, N_EXPERTS, body, 0)

        def tiles(fn):
            def body(t, carry):
                fn(tile_copy(t))
                return carry
            lax.fori_loop(meta_ref[0, N_EXPERTS], n_tiles_all, body, 0)

        pads(lambda c: c.start())
        tiles(lambda c: c.start())
        pads(lambda c: c.wait())
        tiles(lambda c: c.wait())


def _dispatch(gdst, meta, lr, h_a, h_b, n_tiles_all):
    tm = TOK_TILE
    na, nb2 = h_a.shape[0] // tm, h_b.shape[0] // tm
    return pl.pallas_call(
        functools.partial(_dispatch_kernel, na, n_tiles_all),
        out_shape=jax.ShapeDtypeStruct((n_tiles_all * ROW_TILE, ROW_W), BF16),
        grid=(na + nb2,),
        in_specs=[pl.BlockSpec((1, 1, GDST_LANES), lambda i: (i, 0, 0), memory_space=pltpu.SMEM),
                  pl.BlockSpec((META_ROWS, LANES), lambda i: (0, 0), memory_space=pltpu.SMEM),
                  pl.BlockSpec((TOP_K, tm), lambda i: (0, i)),
                  pl.BlockSpec((tm, ROW_W), lambda i: (jnp.minimum(i, na - 1), 0)),
                  pl.BlockSpec((tm, ROW_W), lambda i: (jnp.maximum(i - na, 0), 0))],
        out_specs=pl.BlockSpec(memory_space=pl.ANY),
        scratch_shapes=[pltpu.VMEM((tm, ROW_W), BF16), pltpu.VMEM((2, LOCAL_ROWS, ROW_W), BF16),
                        pltpu.VMEM((ROW_TILE, ROW_W), BF16),
                        pltpu.SemaphoreType.DMA((2,)), pltpu.SemaphoreType.DMA],
        compiler_params=pltpu.CompilerParams(dimension_semantics=("arbitrary",), vmem_limit_bytes=VMEM_LIMIT),
        name="dispatch",
    )(gdst, meta, lr, h_a, h_b)


def _expert_kernel(n_tiles_all, ts_ref, xs_ref, wg_ref, wu_ref, wd_ref, bg_ref, bu_ref, bd_ref, y_ref,
                   wstage, wgb, wub, wdb, xbuf, ybuf, sem_w, sem_in, sem_out):
    e = pl.program_id(0)
    t0 = ts_ref[e]
    nt = ts_ref[e + 1] - t0
    wslot = e % 2

    def w_copies(ex, slot):
        return [pltpu.make_async_copy(w_ref.at[ex], wstage.at[slot, j], sem_w.at[slot])
                for j, w_ref in enumerate((wg_ref, wu_ref, wd_ref))]

    def in_copy(t, slot):
        rows = pl.ds(pl.multiple_of((t0 + t) * ROW_TILE, ROW_TILE), ROW_TILE)
        return pltpu.make_async_copy(xs_ref.at[rows], xbuf.at[slot], sem_in.at[slot])

    def out_copy(tile, slot):
        rows = pl.ds(pl.multiple_of(tile * ROW_TILE, ROW_TILE), ROW_TILE)
        return pltpu.make_async_copy(ybuf.at[slot], y_ref.at[rows], sem_out.at[slot])

    @pl.when((e == 0) & (nt > 0))
    def _():
        for c in w_copies(0, 0):
            c.start()

    for j in range(TILE_SLOTS):
        @pl.when(nt > j)
        def _():
            in_copy(j, j).start()

    @pl.when(e + 1 < N_EXPERTS)
    def _():
        @pl.when(ts_ref[e + 2] > ts_ref[e + 1])
        def _():
            for c in w_copies(e + 1, 1 - wslot):
                c.start(priority=1)

    @pl.when(nt > 0)
    def _():
        for c in w_copies(e, wslot):
            c.wait()
        wgb[...] = wstage[wslot, 0].astype(BF16)
        wub[...] = wstage[wslot, 1].astype(BF16)
        wdb[...] = wstage[wslot, 2].astype(BF16)
        e_f = e.astype(F32)

        def acquire(t):
            slot = t % TILE_SLOTS
            in_copy(t, slot).wait()

            @pl.when(t >= TILE_SLOTS)
            def _():
                out_copy(t0 + t - TILE_SLOTS, slot).wait()

        def compute(t):
            slot = t % TILE_SLOTS
            xw = xbuf[slot]
            x = xw[:, :D_MODEL]
            info = xw[:, D_MODEL:].astype(F32)
            w_row = jnp.zeros((ROW_TILE, 1), F32)
            for k in range(TOP_K):
                wk = info[:, TOP_K + k:TOP_K + k + 1] + info[:, 2 * TOP_K + k:2 * TOP_K + k + 1]
                w_row = w_row + jnp.where(info[:, k:k + 1] == e_f, wk, 0.0)
            g = jnp.minimum(_dot(x, wgb[...]) + bg_ref[0], SWIGLU_LIMIT)
            u = jnp.clip(_dot(x, wub[...]) + bu_ref[0], -SWIGLU_LIMIT, SWIGLU_LIMIT)
            act = g * jax.nn.sigmoid(SWIGLU_ALPHA * g) * (u + 1.0)
            ybuf[slot] = ((_dot(act.astype(BF16), wdb[...]) + bd_ref[0]) * w_row).astype(BF16)

        def release(t):
            slot = t % TILE_SLOTS
            out_copy(t0 + t, slot).start()

            @pl.when(t + TILE_SLOTS < nt)
            def _():
                in_copy(t + TILE_SLOTS, slot).start()

        def pair(p, carry):
            ta, tb = 2 * p, 2 * p + 1
            acquire(ta)
            acquire(tb)
            compute(ta)
            compute(tb)
            release(ta)
            release(tb)
            return carry

        lax.fori_loop(0, nt // 2, pair, 0)

        @pl.when(nt % 2 == 1)
        def _():
            acquire(nt - 1)
            compute(nt - 1)
            release(nt - 1)

        for j in range(1, TILE_SLOTS + 1):
            @pl.when(nt >= j)
            def _():
                out_copy(t0 + nt - j, (nt - j) % TILE_SLOTS).wait()

    @pl.when(e == pl.num_programs(0) - 1)
    def _():
        n_used = ts_ref[N_EXPERTS]
        ybuf[0] = jnp.zeros((ROW_TILE, D_MODEL), BF16)

        def zbody(tile, carry):
            c = out_copy(tile, 0)
            c.start()
            c.wait()
            return carry

        lax.fori_loop(n_used, n_tiles_all, zbody, 0)


def _experts(tile_start, xs, w_gate, b_gate, w_up, b_up, w_down, b_down):
    n_rows = xs.shape[0]
    b_spec = pl.BlockSpec((1, 1, D_MODEL), lambda e, ts: (e, 0, 0))
    any_spec = pl.BlockSpec(memory_space=pl.ANY)
    return pl.pallas_call(
        functools.partial(_expert_kernel, n_rows // ROW_TILE),
        out_shape=jax.ShapeDtypeStruct((n_rows, D_MODEL), BF16),
        grid_spec=pltpu.PrefetchScalarGridSpec(
            num_scalar_prefetch=1,
            grid=(N_EXPERTS,),
            in_specs=[any_spec, any_spec, any_spec, any_spec, b_spec, b_spec, b_spec],
            out_specs=any_spec,
            scratch_shapes=[pltpu.VMEM((2, 3, D_MODEL, D_MODEL), F32)]
                           + [pltpu.VMEM((D_MODEL, D_MODEL), BF16)] * 3
                           + [pltpu.VMEM((TILE_SLOTS, ROW_TILE, ROW_W), BF16),
                              pltpu.VMEM((TILE_SLOTS, ROW_TILE, D_MODEL), BF16),
                              pltpu.SemaphoreType.DMA((2,)), pltpu.SemaphoreType.DMA((TILE_SLOTS,)),
                              pltpu.SemaphoreType.DMA((TILE_SLOTS,))],
        ),
        compiler_params=pltpu.CompilerParams(dimension_semantics=("arbitrary",), vmem_limit_bytes=VMEM_LIMIT),
        name="experts",
    )(tile_start, xs, w_gate, w_up, w_down, b_gate[:, None, :], b_up[:, None, :], b_down[:, None, :])


def _combine_kernel(gdst_ref, gdst_next_ref, lr_ref, y_ref, x1_ref, gt_ref, gf_ref, out_ref, loc_ref, sem):
    i = pl.program_id(0)
    slot = i % 2
    x1 = x1_ref[...]
    sb, l, _ = x1.shape

    @pl.when(i == 0)
    def _():
        for c in _group_copies(loc_ref, slot, y_ref, gdst_ref, sem, False):
            c.start()

    for c in _group_copies(loc_ref, slot, y_ref, gdst_ref, sem, False):
        c.wait()

    prefetch = _group_copies(loc_ref, 1 - slot, y_ref, gdst_next_ref, sem, False)
    n_chunks = LOCAL_ROWS // MASK_ROWS
    per_chunk = LOCAL_GROUPS // n_chunks
    lr = lr_ref[...]
    moe = jnp.zeros((sb * l, D_MODEL), F32)
    for c in range(n_chunks):
        rows = loc_ref[slot, c * MASK_ROWS:(c + 1) * MASK_ROWS, :]
        moe = moe + lax.dot_general(_sort_matrix(lr, c), rows, (((0,), (0,)), ((), ())),
                                    preferred_element_type=F32)
        for cp in prefetch[c * per_chunk:(c + 1) * per_chunk]:
            cp.start()
    x2 = x1 + gt_ref[...] * moe.reshape(sb, l, D_MODEL)
    out_ref[...] = x2 * lax.rsqrt(jnp.mean(x2 * x2, axis=-1, keepdims=True) + EPS) * gf_ref[...]

    @pl.when(i == pl.num_programs(0) - 1)
    def _():
        for c in prefetch:
            c.wait()


def _combine(gdst, lr, y, x1, gt, g_final, sb, l, seq_div, blk_off):
    n3 = x1.shape[0]
    nblk = n3 // sb
    tm = sb * l
    assert tm == TOK_TILE
    return pl.pallas_call(
        _combine_kernel,
        out_shape=jax.ShapeDtypeStruct(x1.shape, F32),
        grid=(nblk,),
        in_specs=[pl.BlockSpec((1, 1, GDST_LANES), lambda i: (i + blk_off, 0, 0), memory_space=pltpu.SMEM),
                  pl.BlockSpec((1, 1, GDST_LANES), lambda i: (jnp.minimum(i + 1, nblk - 1) + blk_off, 0, 0),
                               memory_space=pltpu.SMEM),
                  pl.BlockSpec((TOP_K, tm), lambda i: (0, i + blk_off)),
                  pl.BlockSpec(memory_space=pl.ANY),
                  pl.BlockSpec((sb, l, D_MODEL), lambda i: (i, 0, 0)),
                  pl.BlockSpec((sb, 1, D_MODEL), lambda i: (i // seq_div, 0, 0)),
                  pl.BlockSpec((1, D_MODEL), lambda i: (0, 0))],
        out_specs=pl.BlockSpec((sb, l, D_MODEL), lambda i: (i, 0, 0)),
        scratch_shapes=[pltpu.VMEM((2, LOCAL_ROWS, D_MODEL), BF16), pltpu.SemaphoreType.DMA((2,))],
        compiler_params=pltpu.CompilerParams(dimension_semantics=("arbitrary",), vmem_limit_bytes=VMEM_LIMIT),
        name="combine",
    )(gdst, gdst, lr, y, x1, gt, g_final)


def kernel(x_prompt, x_sample, c_prompt, c_sample, state_ssm, state_conv, w_ada, b_ada, g_mix, w_in, g_v_a, w_spatial, b_spatial, g_out_a, conv_w, conv_b, dt_bias, a_log, d_skip, g_out_b, w_out, g_ffn, w_router, b_router, w_gate, b_gate, w_up, b_up, w_down, b_down, g_final):
    assert w_ada.shape[0] == 1, "single-layer step"
    p = dict(w_in=w_in[0], g_mix=g_mix[0], g_v_a=g_v_a[0], w_spatial=w_spatial[0], b_spatial=b_spatial[0],
             g_out_a=g_out_a[0], conv_w=conv_w[0], conv_b=conv_b[0], dt_bias=dt_bias[0], a_log=a_log[0],
             d_skip=d_skip[0], g_out_b=g_out_b[0])
    bp, lp, _ = x_prompt.shape
    bs, ls, _ = x_sample.shape
    tp, ts = bp * lp, bs * ls

    ada = _ada(jnp.concatenate([c_prompt, c_sample], axis=0), w_ada[0], b_ada[0][None, :])
    ada = ada.reshape(bp + bs, 6, 1, D_MODEL)
    ada_p = [ada[:bp, j] for j in range(6)]
    ada_s = [ada[bp:, j] for j in range(6)]

    fw = _front_weights(p)
    mixed_p, conv_p, ssm_p = _prompt_mixer(x_prompt, ada_p[0], ada_p[1], fw, p)
    mixed_s, v_s, conv_s, ssm_s = _sample_mixer(x_sample, ada_s[0], ada_s[1], state_ssm[0], state_conv[0], fw, p)

    w_out_b = w_out[0].astype(BF16)
    g_ffn2 = g_ffn[0][None, :]
    wr_t = w_router[0].T
    wr_hi = wr_t.astype(BF16)
    wr_lo = (wr_t - wr_hi.astype(F32)).astype(BF16)
    br = b_router[0][:, None]
    tps = lp // TOK_TILE
    sbs = TOK_TILE // ls
    xp3 = x_prompt.reshape(bp * tps, TOK_TILE, D_MODEL)
    x1_p, h2p_p, ids_p = _post(mixed_p, xp3, ada_p[2], ada_p[4], ada_p[3], w_out_b, g_ffn2, wr_hi, wr_lo, br,
                               1, TOK_TILE, tps)
    x1_s, h2p_s, ids_s = _post(mixed_s, x_sample, ada_s[2], ada_s[4], ada_s[3], w_out_b, g_ffn2, wr_hi, wr_lo,
                               br, sbs, ls, 1)

    n_blocks = (tp + ts) // TOK_TILE
    max_groups = (tp + ts) * TOP_K // ROW_GROUP + n_blocks * N_EXPERTS + N_EXPERTS * (TILE_GROUPS - 1)
    n_tiles = -(-max_groups // TILE_GROUPS)
    n_tiles_all = n_tiles + 2 * LOCAL_GROUPS // TILE_GROUPS
    lr, gdst, meta = _plan(jnp.concatenate([ids_p, ids_s], axis=1), n_tiles * TILE_GROUPS)

    xs = _dispatch(gdst, meta, lr, h2p_p, h2p_s, n_tiles_all)
    y = _experts(meta[0], xs, w_gate[0], b_gate[0], w_up[0], b_up[0], w_down[0], b_down[0])

    gf = g_final[None, :]
    y_p = _combine(gdst, lr, y, x1_p, ada_p[5], gf, 1, TOK_TILE, tps, 0).reshape(bp, lp, D_MODEL)
    y_s = _combine(gdst, lr, y, x1_s, ada_s[5], gf, sbs, ls, 1, tp // TOK_TILE)

    return (y_p, y_s, ssm_p[None], conv_p[None], ssm_s[None], conv_s[None], v_s.reshape(1, bs, ls, A_WIDTH))
```

```python
import functools
import math

import numpy as np
import jax
import jax.numpy as jnp
from jax import lax
from jax.experimental import pallas as pl
from jax.experimental.pallas import tpu as pltpu

F32 = jnp.float32
BF16 = jnp.bfloat16
I32 = jnp.int32

D_MODEL = 1024
A_WIDTH = 512
A_HEADS = 4
A_HEAD_DIM = 128
CHUNK = 128
B_WIDTH = 512
SSD_HEAD_DIM = 64
SSD_HEADS = 8
SSD_GROUPS = 2
SSD_STATE = 128
GROUP_W = B_WIDTH // SSD_GROUPS
CONV_K = 4
CONV_DIM = 1024
CONV_PAD = 8
N_EXPERTS = 32
TOP_K = 4
SWIGLU_LIMIT = 7.0
SWIGLU_ALPHA = 1.702
EPS = 1e-6
LANES = 128

TOK_TILE = 512
SAMPLE_SEQ_TILE = 16
ROW_TILE = 256
ROW_GROUP = 16
TILE_GROUPS = ROW_TILE // ROW_GROUP
LOCAL_GROUPS = TOK_TILE * TOP_K // ROW_GROUP + N_EXPERTS
LOCAL_ROWS = LOCAL_GROUPS * ROW_GROUP
GDST_LANES = 256
ROW_W = D_MODEL + LANES
INFO_ROWS = 16
MASK_ROWS = 256
META_ROWS = 8
PROMPT_TILE = 512
PROMPT_SPLITS = 2
TILE_SLOTS = 4
VMEM_LIMIT = 56 * 1024 * 1024


def _dot(a, b):
    return jnp.dot(a, b, preferred_element_type=F32)


def _dot_nt(a, b):
    return lax.dot_general(a, b, (((1,), (1,)), ((), ())), preferred_element_type=F32)


def _split(x):
    hi = x.astype(BF16)
    lo = (x - hi.astype(F32)).astype(BF16)
    return hi, lo


def _dot_exact_l(t, x):
    hi, lo = _split(x)
    return _dot(t, hi) + _dot(t, lo)


def _dot_exact_r(x, t):
    hi, lo = _split(x)
    return _dot(hi, t) + _dot(lo, t)


def _silu(x):
    return x * jax.nn.sigmoid(x)


def _gelu(x):
    return 0.5 * x * (1.0 + lax.erf(x * (1.0 / math.sqrt(2.0))))


def _softplus(x):
    return jnp.maximum(x, 0.0) + jnp.log1p(jnp.exp(-jnp.abs(x)))


def _rms(x, g):
    return x * lax.rsqrt(jnp.mean(x * x, axis=-1, keepdims=True) + EPS) * g


def _ada_kernel(c_ref, w_ref, b_ref, o_ref):
    s_hi, s_lo = _split(_silu(c_ref[...]))
    w_hi, w_lo = _split(w_ref[...])
    o_ref[...] = _dot(s_hi, w_hi) + _dot(s_lo, w_hi) + _dot(s_hi, w_lo) + b_ref[...]


def _ada(c_all, w_ada, b_ada):
    m = c_all.shape[0]
    n = w_ada.shape[1]
    bn = 512
    return pl.pallas_call(
        _ada_kernel,
        out_shape=jax.ShapeDtypeStruct((m, n), F32),
        grid=(n // bn,),
        in_specs=[pl.BlockSpec((m, D_MODEL), lambda j: (0, 0)),
                  pl.BlockSpec((D_MODEL, bn), lambda j: (0, j)),
                  pl.BlockSpec((1, bn), lambda j: (0, j))],
        out_specs=pl.BlockSpec((m, bn), lambda j: (0, j)),
        compiler_params=pltpu.CompilerParams(dimension_semantics=("arbitrary",), vmem_limit_bytes=VMEM_LIMIT),
        name="ada",
    )(c_all, w_ada, b_ada)


def _mixer_front(x3, sh, sc, prev, refs, xp_ref):
    (g_mix, w_uvz, w_xbc, w_dt, w_dtt, wbd, bias_sp, g_v, g_oa, conv_w, conv_b, dt_bias, dt_bias_t, a_row, a_col) = refs
    sb, l, _ = x3.shape
    tm = sb * l
    xn = x3 * lax.rsqrt(jnp.mean(x3 * x3, axis=-1, keepdims=True) + EPS) * g_mix[...]
    h = (xn * (1.0 + sc) + sh).reshape(tm, D_MODEL)
    hb = h.astype(BF16)
    uvz = _dot(hb, w_uvz[...])
    xbc = _dot(hb, w_xbc[...])
    dt_raw = _dot(hb, w_dt[...])
    dtt_raw = _dot_nt(w_dtt[...], hb)

    u = _gelu(uvz[:, :A_WIDTH])
    vg = _gelu(uvz[:, A_WIDTH:2 * A_WIDTH])
    z = uvz[:, 2 * A_WIDTH:]
    v_parts, s_parts = [], []
    for hd in range(A_HEADS):
        sl = slice(hd * A_HEAD_DIM, (hd + 1) * A_HEAD_DIM)
        vh = _rms(vg[:, sl], g_v[:, sl])
        v_parts.append(vh)
        s_parts.append(_dot(wbd[hd, :tm, :tm], vh.astype(BF16)))
    v = jnp.concatenate(v_parts, axis=1)
    s_a = jnp.concatenate(s_parts, axis=1) + bias_sp[:tm, :]
    out_a = _rms(u * s_a, g_oa[...])

    xp_ref[:, 0:CONV_PAD, :] = prev
    xp_ref[:, CONV_PAD:, :] = xbc.reshape(sb, l, CONV_DIM)
    acc = conv_b[...]
    for k in range(CONV_K):
        off = CONV_PAD - (CONV_K - 1) + k
        acc = acc + xp_ref[:, off:off + l, :] * conv_w[k:k + 1, :]
    xc = _silu(acc).reshape(tm, CONV_DIM)
    dt = _softplus(dt_raw + dt_bias[...])
    dtt = _softplus(dtt_raw + dt_bias_t[...])
    d_a = dt * a_row[...]
    d_at = dtt * a_col[...]
    return out_a, v, z, xc, dt, d_a, d_at


def _ssd_chunk(xs, bm, cm, dt, d_a, d_at, cref):
    tril, triu, ones, expand, mask = cref
    cs = _dot_exact_l(tril[...], d_a)
    cs_t = _dot_exact_r(d_at, triu[...])
    cs_tot = _dot_exact_l(ones[...], d_a)
    vals = jnp.concatenate([dt, jnp.exp(cs_tot - cs), jnp.exp(cs)], axis=0)
    vals_e = _dot_exact_r(vals, expand[...])
    n = xs.shape[0]
    dt_e, dte_e, e_e = vals_e[:n], vals_e[n:2 * n], vals_e[2 * n:]
    xdt = xs * dt_e
    xdtd = xdt * dte_e
    msk = mask[...] > 0.5
    row_lt_half = lax.broadcasted_iota(I32, (2 * n, LANES), 0) < n
    lane_lt_half = lax.broadcasted_iota(I32, (2 * n, LANES), 1) < SSD_HEAD_DIM
    y_parts = []
    for g in range(SSD_GROUPS):
        cb = _dot_nt(cm[:, g * SSD_STATE:(g + 1) * SSD_STATE].astype(BF16),
                     bm[:, g * SSD_STATE:(g + 1) * SSD_STATE].astype(BF16))
        for hp in range(SSD_HEADS // SSD_GROUPS // 2):
            h0 = g * (SSD_HEADS // SSD_GROUPS) + 2 * hp
            ms = []
            for hh in (h0, h0 + 1):
                diff = cs[:, hh:hh + 1] - cs_t[hh:hh + 1, :]
                ms.append((cb * jnp.where(msk, jnp.exp(jnp.where(msk, diff, 0.0)), 0.0)).astype(BF16))
            pair = xdt[:, h0 * SSD_HEAD_DIM:(h0 + 2) * SSD_HEAD_DIM]
            rhs = jnp.where(row_lt_half == lane_lt_half, jnp.concatenate([pair, pair], axis=0), 0.0).astype(BF16)
            y_parts.append(_dot(jnp.concatenate(ms, axis=1), rhs))
    y_diag = jnp.concatenate(y_parts, axis=1)
    return y_diag, e_e, xdtd, cs_tot


def _mixer_back(y, xs, z, out_a, dskip_e, g_ob):
    y = y + xs * dskip_e
    gated = y * _silu(z)
    parts = [_rms(gated[:, g * GROUP_W:(g + 1) * GROUP_W], g_ob[:, g * GROUP_W:(g + 1) * GROUP_W])
             for g in range(SSD_GROUPS)]
    return jnp.concatenate([out_a] + parts, axis=1).astype(BF16)


N_FRONT = 15
N_SSD = 5


def _prompt_mixer_kernel(tiles_per_seq, x_ref, sh_ref, sc_ref, *rest):
    front = rest[:N_FRONT]
    cref = rest[N_FRONT:N_FRONT + N_SSD]
    dskip_e, g_ob = rest[N_FRONT + N_SSD:N_FRONT + N_SSD + 2]
    mixed_ref, conv_out_ref, ssm_out_ref = rest[N_FRONT + N_SSD + 2:N_FRONT + N_SSD + 5]
    xp_ref, carry_ref, st_ref = rest[N_FRONT + N_SSD + 5:]
    i = pl.program_id(0)
    first = (i % tiles_per_seq) == 0

    @pl.when(first)
    def _():
        carry_ref[...] = jnp.zeros_like(carry_ref)
        st_ref[...] = jnp.zeros_like(st_ref)

    l = x_ref.shape[1]
    hl = l // PROMPT_SPLITS
    parts = []
    prev = carry_ref[...]
    for hh in range(PROMPT_SPLITS):
        xp_h = xp_ref.at[hh]
        parts.append(_mixer_front(x_ref[:, hh * hl:(hh + 1) * hl, :], sh_ref[...], sc_ref[...], prev, front, xp_h))
        prev = xp_h[:, hl:hl + CONV_PAD, :]
    carry_ref[...] = prev
    out_a, _, z, xc, dt, d_a = [jnp.concatenate([p[j] for p in parts], axis=0) for j in range(6)]
    d_at = jnp.concatenate([p[6] for p in parts], axis=1)
    xs = xc[:, :B_WIDTH]
    y_rows = []
    for c in range(l // CHUNK):
        r = slice(c * CHUNK, (c + 1) * CHUNK)
        bm = xc[r, B_WIDTH:B_WIDTH + SSD_GROUPS * SSD_STATE]
        cm = xc[r, B_WIDTH + SSD_GROUPS * SSD_STATE:]
        y_diag, e_e, xdtd, _ = _ssd_chunk(xs[r], bm, cm, dt[r], d_a[r], d_at[:, r], cref)
        st = st_ref[...]
        y_off, upd = [], []
        for g in range(SSD_GROUPS):
            gs = slice(g * GROUP_W, (g + 1) * GROUP_W)
            ns = slice(g * SSD_STATE, (g + 1) * SSD_STATE)
            y_off.append(_dot(cm[:, ns].astype(BF16), st[:, gs].astype(BF16)))
            upd.append(_dot(bm[:, ns].T.astype(BF16), xdtd[:, gs].astype(BF16)))
        y_rows.append(y_diag + jnp.concatenate(y_off, axis=1) * e_e)
        st_ref[...] = st * e_e[CHUNK - 1:CHUNK, :] + jnp.concatenate(upd, axis=1)
    y = jnp.concatenate(y_rows, axis=0)
    mixed_ref[...] = _mixer_back(y, xs, z, out_a, dskip_e[...], g_ob[...])

    @pl.when((i % tiles_per_seq) == tiles_per_seq - 1)
    def _():
        conv_out_ref[...] = xp_ref[PROMPT_SPLITS - 1, :, hl + CONV_PAD - (CONV_K - 1):hl + CONV_PAD, :]
        ssm_out_ref[0] = st_ref[...].T


def _sample_mixer_kernel(x_ref, sh_ref, sc_ref, prev_ref, ssm0_ref, *rest):
    front = rest[:N_FRONT]
    cref = rest[N_FRONT:N_FRONT + N_SSD]
    dskip_e, g_ob, selseq = rest[N_FRONT + N_SSD:N_FRONT + N_SSD + 3]
    mixed_ref, v_ref, conv_out_ref, ssm_out_ref = rest[N_FRONT + N_SSD + 3:N_FRONT + N_SSD + 7]
    xp_ref, yoff_ref, cbf_ref, bbf_ref, t1_ref, dtab_ref = rest[N_FRONT + N_SSD + 7:]
    x3 = x_ref[...]
    sb, l, _ = x3.shape
    tm = sb * l
    out_a, v, z, xc, dt, d_a, d_at = _mixer_front(x3, sh_ref[...], sc_ref[...], prev_ref[...], front, xp_ref)
    v_ref[...] = v
    conv_out_ref[...] = xp_ref[:, l + CONV_PAD - (CONV_K - 1):l + CONV_PAD, :]
    xs = xc[:, :B_WIDTH]
    bm = xc[:, B_WIDTH:B_WIDTH + SSD_GROUPS * SSD_STATE]
    cm = xc[:, B_WIDTH + SSD_GROUPS * SSD_STATE:]
    y_diag, e_e, xdtd, _ = _ssd_chunk(xs, bm, cm, dt, d_a, d_at, cref)

    e_tot = jnp.exp(_dot_exact_l(selseq[...], d_a))
    for hh in range(SSD_HEADS):
        dtab_ref[hh] = jnp.broadcast_to(e_tot[:, hh:hh + 1], (sb, LANES))
    cbf_ref[...] = cm
    bbf_ref[...] = bm
    for g in range(SSD_GROUPS):
        t1_ref[g] = xdtd[:, g * GROUP_W:(g + 1) * GROUP_W].T.astype(BF16)
    seq_of_row = lax.broadcasted_iota(I32, (tm, SSD_STATE), 0) // l
    heads_per_group = SSD_HEADS // SSD_GROUPS

    def body(j, carry):
        r0 = pl.multiple_of(j * l, l)
        s0 = ssm0_ref[j]
        for g in range(SSD_GROUPS):
            ns = slice(g * SSD_STATE, (g + 1) * SSD_STATE)
            s0g = s0[g * heads_per_group:(g + 1) * heads_per_group].reshape(GROUP_W, SSD_STATE)
            cj = cbf_ref[pl.ds(r0, l), ns].astype(BF16)
            yoff_ref[pl.ds(r0, l), g * GROUP_W:(g + 1) * GROUP_W] = _dot_nt(cj, s0g.astype(BF16))
            bmask = jnp.where(seq_of_row == j, bbf_ref[:, ns], 0.0).astype(BF16)
            upd = _dot(t1_ref[g], bmask)
            for hq in range(heads_per_group):
                hh = g * heads_per_group + hq
                dec = dtab_ref[hh, pl.ds(j, 1), :]
                ssm_out_ref[j, hh] = s0[hh] * dec + upd[hq * SSD_HEAD_DIM:(hq + 1) * SSD_HEAD_DIM]
        return carry

    lax.fori_loop(0, sb, body, 0)
    y = y_diag + yoff_ref[...] * e_e
    mixed_ref[...] = _mixer_back(y, xs, z, out_a, dskip_e[...], g_ob[...])


def _const_spec(a):
    nd = a.ndim
    return pl.BlockSpec(a.shape, lambda i, _nd=nd: (0,) * _nd)


def _spatial_consts(w_spatial, b_spatial, cl, tm):
    w = jnp.where(jnp.tril(jnp.ones((cl, cl), bool)), w_spatial[:, :cl, :cl], 0.0)
    eye = jnp.eye(tm // cl, dtype=F32)
    wbd = jnp.einsum("ab,hts->hatbs", eye, w).reshape(A_HEADS, tm, tm).astype(BF16)
    bias = jnp.tile(jnp.repeat(b_spatial[:, :cl].T, A_HEAD_DIM, axis=1), (tm // cl, 1))
    return wbd, bias


def _ssd_consts(cl):
    r = np.arange(CHUNK)
    same = (r[:, None] // cl) == (r[None, :] // cl)
    tril = same & (r[:, None] >= r[None, :])
    expand = np.zeros((LANES, B_WIDTH), np.float32)
    for hh in range(SSD_HEADS):
        expand[hh, hh * SSD_HEAD_DIM:(hh + 1) * SSD_HEAD_DIM] = 1.0
    return (jnp.asarray(tril, BF16), jnp.asarray(tril.T, BF16), jnp.asarray(same, BF16),
            jnp.asarray(expand, BF16), jnp.asarray(tril, F32))


def _front_weights(p):
    w_in = p["w_in"]
    c0, c1 = 3 * A_WIDTH, 3 * A_WIDTH + CONV_DIM
    w_dt = w_in[:, c1:]
    pad8 = lambda v: jnp.pad(v, (0, LANES - SSD_HEADS))
    a = -jnp.exp(p["a_log"])
    return dict(
        g_mix=p["g_mix"][None, :],
        w_uvz=w_in[:, :c0].astype(BF16),
        w_xbc=w_in[:, c0:c1].astype(BF16),
        w_dt=jnp.pad(w_dt, ((0, 0), (0, LANES - SSD_HEADS))).astype(BF16),
        w_dtt=w_dt.T.astype(BF16),
        g_v=p["g_v_a"][None, :], g_oa=p["g_out_a"][None, :],
        conv_w=p["conv_w"], conv_b=p["conv_b"][None, :],
        dt_bias=pad8(p["dt_bias"])[None, :], dt_bias_t=p["dt_bias"][:, None],
        a_row=pad8(a)[None, :], a_col=a[:, None],
        dskip_e=jnp.repeat(p["d_skip"], SSD_HEAD_DIM)[None, :],
        g_ob=p["g_out_b"][None, :],
    )


def _front_list(fw, wbd, bias_sp):
    return [fw["g_mix"], fw["w_uvz"], fw["w_xbc"], fw["w_dt"], fw["w_dtt"], wbd, bias_sp, fw["g_v"], fw["g_oa"],
            fw["conv_w"], fw["conv_b"], fw["dt_bias"], fw["dt_bias_t"], fw["a_row"], fw["a_col"]]


def _prompt_mixer(x, sh, sc, fw, p):
    nseq, lseq, _ = x.shape
    tile = PROMPT_TILE
    tps = lseq // tile
    nt = nseq * tps
    x4 = x.reshape(nt, tile, D_MODEL)
    wbd, bias_sp = _spatial_consts(p["w_spatial"], p["b_spatial"], CHUNK, tile // PROMPT_SPLITS)
    consts = _front_list(fw, wbd, bias_sp) + list(_ssd_consts(CHUNK)) + [fw["dskip_e"], fw["g_ob"]]
    seq_spec = pl.BlockSpec((1, 1, D_MODEL), lambda i: (i // tps, 0, 0))
    mixed, conv_new, ssm_new = pl.pallas_call(
        functools.partial(_prompt_mixer_kernel, tps),
        out_shape=(jax.ShapeDtypeStruct((nt * tile, D_MODEL), BF16),
                   jax.ShapeDtypeStruct((nseq, CONV_K - 1, CONV_DIM), F32),
                   jax.ShapeDtypeStruct((nseq, B_WIDTH, SSD_STATE), F32)),
        grid=(nt,),
        in_specs=[pl.BlockSpec((1, tile, D_MODEL), lambda i: (i, 0, 0)), seq_spec, seq_spec]
                 + [_const_spec(a) for a in consts],
        out_specs=(pl.BlockSpec((tile, D_MODEL), lambda i: (i, 0)),
                   pl.BlockSpec((1, CONV_K - 1, CONV_DIM), lambda i: (i // tps, 0, 0)),
                   pl.BlockSpec((1, B_WIDTH, SSD_STATE), lambda i: (i // tps, 0, 0))),
        scratch_shapes=[pltpu.VMEM((PROMPT_SPLITS, 1, tile // PROMPT_SPLITS + CONV_PAD, CONV_DIM), F32),
                        pltpu.VMEM((1, CONV_PAD, CONV_DIM), F32),
                        pltpu.VMEM((SSD_STATE, B_WIDTH), F32)],
        compiler_params=pltpu.CompilerParams(dimension_semantics=("arbitrary",), vmem_limit_bytes=VMEM_LIMIT),
        name="prompt_mixer",
    )(x4, sh, sc, *consts)
    return mixed, conv_new, ssm_new.reshape(nseq, SSD_HEADS, SSD_HEAD_DIM, SSD_STATE)


def _sample_mixer(x, sh, sc, state_ssm, state_conv, fw, p):
    nseq, l, _ = x.shape
    sb = SAMPLE_SEQ_TILE
    tm = sb * l
    assert tm == CHUNK
    wbd, bias_sp = _spatial_consts(p["w_spatial"], p["b_spatial"], l, tm)
    selseq = jnp.asarray((np.arange(tm)[None, :] // l) == np.arange(sb)[:, None], BF16)
    consts = _front_list(fw, wbd, bias_sp) + list(_ssd_consts(l)) + [fw["dskip_e"], fw["g_ob"], selseq]
    prev = jnp.pad(state_conv, ((0, 0), (CONV_PAD - (CONV_K - 1), 0), (0, 0)))
    seq_spec = pl.BlockSpec((sb, 1, D_MODEL), lambda i: (i, 0, 0))
    ssm_spec = pl.BlockSpec((sb, SSD_HEADS, SSD_HEAD_DIM, SSD_STATE), lambda i: (i, 0, 0, 0))
    return pl.pallas_call(
        _sample_mixer_kernel,
        out_shape=(jax.ShapeDtypeStruct((nseq * l, D_MODEL), BF16),
                   jax.ShapeDtypeStruct((nseq * l, A_WIDTH), F32),
                   jax.ShapeDtypeStruct((nseq, CONV_K - 1, CONV_DIM), F32),
                   jax.ShapeDtypeStruct(state_ssm.shape, F32)),
        grid=(nseq // sb,),
        in_specs=[pl.BlockSpec((sb, l, D_MODEL), lambda i: (i, 0, 0)), seq_spec, seq_spec,
                  pl.BlockSpec((sb, CONV_PAD, CONV_DIM), lambda i: (i, 0, 0)), ssm_spec]
                 + [_const_spec(a) for a in consts],
        out_specs=(pl.BlockSpec((tm, D_MODEL), lambda i: (i, 0)),
                   pl.BlockSpec((tm, A_WIDTH), lambda i: (i, 0)),
                   pl.BlockSpec((sb, CONV_K - 1, CONV_DIM), lambda i: (i, 0, 0)),
                   ssm_spec),
        scratch_shapes=[pltpu.VMEM((sb, l + CONV_PAD, CONV_DIM), F32),
                        pltpu.VMEM((tm, B_WIDTH), F32),
                        pltpu.VMEM((tm, SSD_GROUPS * SSD_STATE), F32),
                        pltpu.VMEM((tm, SSD_GROUPS * SSD_STATE), F32),
                        pltpu.VMEM((SSD_GROUPS, GROUP_W, tm), BF16),
                        pltpu.VMEM((SSD_HEADS, sb, LANES), F32)],
        compiler_params=pltpu.CompilerParams(dimension_semantics=("arbitrary",), vmem_limit_bytes=VMEM_LIMIT),
        name="sample_mixer",
    )(x, sh, sc, prev, state_ssm, *consts)


def _post_kernel(mixed_ref, x_ref, gt_ref, sc_ref, sh_ref, w_out_ref, g_ffn_ref, wr_hi_ref, wr_lo_ref, br_ref,
                 x1_ref, h2p_ref, ids_ref):
    x3 = x_ref[...]
    sb, l, _ = x3.shape
    tm = sb * l
    mix = _dot(mixed_ref[...], w_out_ref[...]).reshape(sb, l, D_MODEL)
    x1 = x3 + gt_ref[...] * mix
    x1_ref[...] = x1
    xn = x1 * lax.rsqrt(jnp.mean(x1 * x1, axis=-1, keepdims=True) + EPS) * g_ffn_ref[...]
    h2 = (xn * (1.0 + sc_ref[...]) + sh_ref[...]).reshape(tm, D_MODEL)
    h_hi, h_lo = _split(h2)
    h2p_ref[:, :D_MODEL] = h_hi
    logits = (_dot_nt(wr_hi_ref[...], h_hi) + _dot_nt(wr_hi_ref[...], h_lo) + _dot_nt(wr_lo_ref[...], h_hi)
              + br_ref[...])
    e_iota = lax.broadcasted_iota(I32, logits.shape, 0)
    vals, idxs = [], []
    for _ in range(TOP_K):
        m = jnp.max(logits, axis=0, keepdims=True)
        idx = jnp.min(jnp.where(logits == m, e_iota, N_EXPERTS), axis=0, keepdims=True)
        vals.append(m)
        idxs.append(idx)
        logits = jnp.where(e_iota == idx, -jnp.inf, logits)
    ex = [jnp.exp(v - vals[0]) for v in vals]
    tot = ex[0] + ex[1] + ex[2] + ex[3]
    ids = jnp.concatenate(idxs, axis=0)
    ids_ref[...] = ids
    wts = jnp.concatenate([e / tot for e in ex], axis=0)
    w_hi = wts.astype(BF16).astype(F32)
    info = jnp.concatenate([ids.astype(F32), w_hi, wts - w_hi, jnp.zeros((TOP_K, tm), F32)], axis=0).astype(BF16)
    r = lax.broadcasted_iota(I32, (INFO_ROWS, LANES), 0)
    c = lax.broadcasted_iota(I32, (INFO_ROWS, LANES), 1)
    place = jnp.where(r == c, 1.0, 0.0).astype(BF16)
    h2p_ref[:, D_MODEL:] = lax.dot_general(info, place, (((0,), (0,)), ((), ())),
                                           preferred_element_type=F32).astype(BF16)


def _post(mixed, x, gt, sc, sh, w_out_b, g_ffn, wr_hi, wr_lo, br, sb, l, seq_div):
    n3, _, _ = x.shape
    nblk = n3 // sb
    tm = sb * l
    t = n3 * l
    ada_spec = pl.BlockSpec((sb, 1, D_MODEL), lambda i: (i // seq_div, 0, 0))
    consts = [w_out_b, g_ffn, wr_hi, wr_lo, br]
    return pl.pallas_call(
        _post_kernel,
        out_shape=(jax.ShapeDtypeStruct(x.shape, F32),
                   jax.ShapeDtypeStruct((t, ROW_W), BF16),
                   jax.ShapeDtypeStruct((TOP_K, t), I32)),
        grid=(nblk,),
        in_specs=[pl.BlockSpec((tm, D_MODEL), lambda i: (i, 0)),
                  pl.BlockSpec((sb, l, D_MODEL), lambda i: (i, 0, 0)), ada_spec, ada_spec, ada_spec]
                 + [_const_spec(a) for a in consts],
        out_specs=(pl.BlockSpec((sb, l, D_MODEL), lambda i: (i, 0, 0)),
                   pl.BlockSpec((tm, ROW_W), lambda i: (i, 0)),
                   pl.BlockSpec((TOP_K, tm), lambda i: (0, i))),
        compiler_params=pltpu.CompilerParams(dimension_semantics=("arbitrary",), vmem_limit_bytes=VMEM_LIMIT),
        name="post",
    )(mixed, x, gt, sc, sh, *consts)


def _strict_upper(n):
    r = lax.broadcasted_iota(I32, (n, n), 0)
    c = lax.broadcasted_iota(I32, (n, n), 1)
    return jnp.where(r < c, 1.0, 0.0).astype(BF16)


def _expert_prefix(col):
    r = lax.broadcasted_iota(I32, (N_EXPERTS, N_EXPERTS), 0)
    c = lax.broadcasted_iota(I32, (N_EXPERTS, N_EXPERTS), 1)
    as_row = jnp.sum(jnp.where(r == c, col, 0.0), axis=0, keepdims=True)
    return jnp.sum(jnp.where(c < r, as_row, 0.0), axis=1, keepdims=True)


def _plan_kernel(dump_group, ids_ref, lr_ref, gdst_ref, tile_e_ref):
    tm = TOK_TILE
    nb = ids_ref.shape[1] // tm
    blk_lane = lax.broadcasted_iota(I32, (N_EXPERTS, LANES), 1)

    def block_masks(b):
        ids = ids_ref[:, pl.ds(pl.multiple_of(b * tm, tm), tm)]
        e_iota = lax.broadcasted_iota(I32, (N_EXPERTS, tm), 0)
        onehot = [ids[k:k + 1, :] == e_iota for k in range(TOP_K)]
        sel = (onehot[0] | onehot[1]) | (onehot[2] | onehot[3])
        return onehot, jnp.where(sel, 1.0, 0.0)

    def count(b, seg):
        _, m = block_masks(b)
        seg_b = jnp.ceil(jnp.sum(m, axis=1, keepdims=True) * (1.0 / ROW_GROUP))
        return jnp.where(blk_lane == b, seg_b, seg)

    seg = lax.fori_loop(0, nb, count, jnp.zeros((N_EXPERTS, LANES), F32))
    tot = jnp.sum(seg, axis=1, keepdims=True)
    padded = jnp.ceil(tot * (1.0 / TILE_GROUPS)) * TILE_GROUPS
    gstart = _expert_prefix(padded)
    gb = gstart + _dot(seg.astype(BF16), _strict_upper(LANES))
    r = lax.broadcasted_iota(I32, (N_EXPERTS, LANES), 0)
    as_row = lambda col: jnp.sum(jnp.where(r == blk_lane, col, 0.0), axis=0, keepdims=True)
    n_used = jnp.sum(padded, axis=0, keepdims=True)
    lane = lax.broadcasted_iota(I32, (1, LANES), 1)
    tiles = jnp.where(lane == N_EXPERTS, n_used, as_row(gstart)) * (1.0 / TILE_GROUPS)
    meta = jnp.concatenate([tiles, as_row(gstart + tot), as_row(padded - tot),
                            jnp.zeros((META_ROWS - 3, LANES), F32)], axis=0)
    tile_e_ref[...] = meta.astype(I32)
    upper = _strict_upper(tm)

    def place(b, carry):
        onehot, m = block_masks(b)
        seg_b = jnp.sum(jnp.where(blk_lane == b, seg, 0.0), axis=1, keepdims=True)
        gb_b = jnp.sum(jnp.where(blk_lane == b, gb, 0.0), axis=1, keepdims=True)
        loc_b = _expert_prefix(seg_b)
        before = _dot(m.astype(BF16), upper) + loc_b * ROW_GROUP
        lr_ref[:, pl.ds(pl.multiple_of(b * tm, tm), tm)] = jnp.concatenate(
            [jnp.sum(jnp.where(onehot[k], before, 0.0), axis=0, keepdims=True) for k in range(TOP_K)],
            axis=0).astype(I32)
        g = lax.broadcasted_iota(I32, (N_EXPERTS, GDST_LANES), 1).astype(F32)
        inside = (loc_b <= g) & (g < loc_b + seg_b)
        dst = jnp.sum(jnp.where(inside, gb_b + g - loc_b, 0.0), axis=0, keepdims=True)
        used = jnp.sum(jnp.where(inside, 1.0, 0.0), axis=0, keepdims=True) > 0.5
        dump = dump_group + lax.convert_element_type(b % 2, F32) * LOCAL_GROUPS + g[0:1, :]
        gdst_ref[b] = jnp.where(used, dst, dump).astype(I32)
        return carry

    lax.fori_loop(0, nb, place, 0)


def _plan(ids, dump_group):
    t = ids.shape[1]
    nb = t // TOK_TILE
    assert nb <= LANES
    return pl.pallas_call(
        functools.partial(_plan_kernel, float(dump_group)),
        out_shape=(jax.ShapeDtypeStruct((TOP_K, t), I32),
                   jax.ShapeDtypeStruct((nb, 1, GDST_LANES), I32),
                   jax.ShapeDtypeStruct((META_ROWS, LANES), I32)),
        name="plan",
    )(ids)


def _sort_matrix(lr, c):
    r_iota = lax.broadcasted_iota(I32, (MASK_ROWS, lr.shape[1]), 0) + c * MASK_ROWS
    p = jnp.where(r_iota == lr[TOP_K - 1:TOP_K, :], 1.0, 0.0)
    for k in range(TOP_K - 1):
        p = jnp.where(r_iota == lr[k:k + 1, :], 1.0, p)
    return p.astype(BF16)


def _group_copies(loc_ref, slot, far_ref, gdst_ref, sem, to_far):
    copies = []
    for g in range(LOCAL_GROUPS):
        dst = pl.multiple_of(gdst_ref[0, 0, g] * ROW_GROUP, ROW_GROUP)
        near = loc_ref.at[slot, pl.ds(g * ROW_GROUP, ROW_GROUP)]
        far = far_ref.at[pl.ds(dst, ROW_GROUP)]
        copies.append(pltpu.make_async_copy(near, far, sem.at[slot]) if to_far
                      else pltpu.make_async_copy(far, near, sem.at[slot]))
    return copies


def _wait_groups(loc_ref, slot, sem):
    pltpu.make_async_copy(loc_ref.at[slot], loc_ref.at[slot], sem.at[slot]).wait()


def _dispatch_kernel(n_first, n_tiles_all, gdst_ref, meta_ref, lr_ref, ha_ref, hb_ref, xs_ref,
                     h_ref, loc_ref, zero_ref, sem, zsem):
    i = pl.program_id(0)
    last = pl.num_programs(0) - 1
    slot = i % 2

    @pl.when(i < n_first)
    def _():
        h_ref[...] = ha_ref[...]

    @pl.when(i >= n_first)
    def _():
        h_ref[...] = hb_ref[...]

    lr = lr_ref[...]
    copies = _group_copies(loc_ref, slot, xs_ref, gdst_ref, sem, True)
    n_chunks = LOCAL_ROWS // MASK_ROWS
    per_chunk = LOCAL_GROUPS // n_chunks
    for c in range(n_chunks):
        loc_ref[slot, c * MASK_ROWS:(c + 1) * MASK_ROWS, :] = _dot(_sort_matrix(lr, c), h_ref[...]).astype(BF16)
        for cp in copies[c * per_chunk:(c + 1) * per_chunk]:
            cp.start()

    @pl.when(i > 0)
    def _():
        _wait_groups(loc_ref, 1 - slot, sem)

    @pl.when(i == last)
    def _():
        _wait_groups(loc_ref, slot, sem)
        zero_ref[...] = jnp.zeros_like(zero_ref)

        def pad_copy(e, j):
            row = pl.multiple_of((meta_ref[1, e] + j) * ROW_GROUP, ROW_GROUP)
            return pltpu.make_async_copy(zero_ref.at[pl.ds(0, ROW_GROUP)], xs_ref.at[pl.ds(row, ROW_GROUP)], zsem)

        def tile_copy(t):
            row = pl.multiple_of(t * ROW_TILE, ROW_TILE)
            return pltpu.make_async_copy(zero_ref, xs_ref.at[pl.ds(row, ROW_TILE)], zsem)

        def pads(fn):
            def body(e, carry):
                for j in range(TILE_GROUPS - 1):
                    @pl.when(j < meta_ref[2, e])
                    def _():
                        fn(pad_copy(e, j))
                return carry
            lax.fori_loop(0, N_EXPERTS, body, 0)

        def tiles(fn):
            def body(t, carry):
                fn(tile_copy(t))
                return carry
            lax.fori_loop(meta_ref[0, N_EXPERTS], n_tiles_all, body, 0)

        pads(lambda c: c.start())
        tiles(lambda c: c.start())
        pads(lambda c: c.wait())
        tiles(lambda c: c.wait())


def _dispatch(gdst, meta, lr, h_a, h_b, n_tiles_all):
    tm = TOK_TILE
    na, nb2 = h_a.shape[0] // tm, h_b.shape[0] // tm
    return pl.pallas_call(
        functools.partial(_dispatch_kernel, na, n_tiles_all),
        out_shape=jax.ShapeDtypeStruct((n_tiles_all * ROW_TILE, ROW_W), BF16),
        grid=(na + nb2,),
        in_specs=[pl.BlockSpec((1, 1, GDST_LANES), lambda i: (i, 0, 0), memory_space=pltpu.SMEM),
                  pl.BlockSpec((META_ROWS, LANES), lambda i: (0, 0), memory_space=pltpu.SMEM),
                  pl.BlockSpec((TOP_K, tm), lambda i: (0, i)),
                  pl.BlockSpec((tm, ROW_W), lambda i: (jnp.minimum(i, na - 1), 0)),
                  pl.BlockSpec((tm, ROW_W), lambda i: (jnp.maximum(i - na, 0), 0))],
        out_specs=pl.BlockSpec(memory_space=pl.ANY),
        scratch_shapes=[pltpu.VMEM((tm, ROW_W), BF16), pltpu.VMEM((2, LOCAL_ROWS, ROW_W), BF16),
                        pltpu.VMEM((ROW_TILE, ROW_W), BF16),
                        pltpu.SemaphoreType.DMA((2,)), pltpu.SemaphoreType.DMA],
        compiler_params=pltpu.CompilerParams(dimension_semantics=("arbitrary",), vmem_limit_bytes=VMEM_LIMIT),
        name="dispatch",
    )(gdst, meta, lr, h_a, h_b)


def _expert_kernel(n_tiles_all, ts_ref, xs_ref, wg_ref, wu_ref, wd_ref, bg_ref, bu_ref, bd_ref, y_ref,
                   wstage, wgb, wub, wdb, xbuf, ybuf, sem_w, sem_in, sem_out):
    e = pl.program_id(0)
    t0 = ts_ref[e]
    nt = ts_ref[e + 1] - t0
    wslot = e % 2

    def w_copies(ex, slot):
        return [pltpu.make_async_copy(w_ref.at[ex], wstage.at[slot, j], sem_w.at[slot])
                for j, w_ref in enumerate((wg_ref, wu_ref, wd_ref))]

    def in_copy(t, slot):
        rows = pl.ds(pl.multiple_of((t0 + t) * ROW_TILE, ROW_TILE), ROW_TILE)
        return pltpu.make_async_copy(xs_ref.at[rows], xbuf.at[slot], sem_in.at[slot])

    def out_copy(tile, slot):
        rows = pl.ds(pl.multiple_of(tile * ROW_TILE, ROW_TILE), ROW_TILE)
        return pltpu.make_async_copy(ybuf.at[slot], y_ref.at[rows], sem_out.at[slot])

    @pl.when((e == 0) & (nt > 0))
    def _():
        for c in w_copies(0, 0):
            c.start()

    for j in range(TILE_SLOTS):
        @pl.when(nt > j)
        def _():
            in_copy(j, j).start()

    @pl.when(e + 1 < N_EXPERTS)
    def _():
        @pl.when(ts_ref[e + 2] > ts_ref[e + 1])
        def _():
            for c in w_copies(e + 1, 1 - wslot):
                c.start(priority=1)

    @pl.when(nt > 0)
    def _():
        for c in w_copies(e, wslot):
            c.wait()
        wgb[...] = wstage[wslot, 0].astype(BF16)
        wub[...] = wstage[wslot, 1].astype(BF16)
        wdb[...] = wstage[wslot, 2].astype(BF16)
        e_f = e.astype(F32)

        def acquire(t):
            slot = t % TILE_SLOTS
            in_copy(t, slot).wait()

            @pl.when(t >= TILE_SLOTS)
            def _():
                out_copy(t0 + t - TILE_SLOTS, slot).wait()

        def compute(t):
            slot = t % TILE_SLOTS
            xw = xbuf[slot]
            x = xw[:, :D_MODEL]
            info = xw[:, D_MODEL:].astype(F32)
            w_row = jnp.zeros((ROW_TILE, 1), F32)
            for k in range(TOP_K):
                wk = info[:, TOP_K + k:TOP_K + k + 1] + info[:, 2 * TOP_K + k:2 * TOP_K + k + 1]
                w_row = w_row + jnp.where(info[:, k:k + 1] == e_f, wk, 0.0)
            g = jnp.minimum(_dot(x, wgb[...]) + bg_ref[0], SWIGLU_LIMIT)
            u = jnp.clip(_dot(x, wub[...]) + bu_ref[0], -SWIGLU_LIMIT, SWIGLU_LIMIT)
            act = g * jax.nn.sigmoid(SWIGLU_ALPHA * g) * (u + 1.0)
            ybuf[slot] = ((_dot(act.astype(BF16), wdb[...]) + bd_ref[0]) * w_row).astype(BF16)

        def release(t):
            slot = t % TILE_SLOTS
            out_copy(t0 + t, slot).start()

            @pl.when(t + TILE_SLOTS < nt)
            def _():
                in_copy(t + TILE_SLOTS, slot).start()

        def pair(p, carry):
            ta, tb = 2 * p, 2 * p + 1
            acquire(ta)
            acquire(tb)
            compute(ta)
            compute(tb)
            release(ta)
            release(tb)
            return carry

        lax.fori_loop(0, nt // 2, pair, 0)

        @pl.when(nt % 2 == 1)
        def _():
            acquire(nt - 1)
            compute(nt - 1)
            release(nt - 1)

        for j in range(1, TILE_SLOTS + 1):
            @pl.when(nt >= j)
            def _():
                out_copy(t0 + nt - j, (nt - j) % TILE_SLOTS).wait()

    @pl.when(e == pl.num_programs(0) - 1)
    def _():
        n_used = ts_ref[N_EXPERTS]
        ybuf[0] = jnp.zeros((ROW_TILE, D_MODEL), BF16)

        def zstart(tile, carry):
            out_copy(tile, 0).start()
            return carry

        def zwait(tile, carry):
            out_copy(tile, 0).wait()
            return carry

        lax.fori_loop(n_used, n_tiles_all, zstart, 0)
        lax.fori_loop(n_used, n_tiles_all, zwait, 0)


def _experts(tile_start, xs, w_gate, b_gate, w_up, b_up, w_down, b_down):
    n_rows = xs.shape[0]
    b_spec = pl.BlockSpec((1, 1, D_MODEL), lambda e, ts: (e, 0, 0))
    any_spec = pl.BlockSpec(memory_space=pl.ANY)
    return pl.pallas_call(
        functools.partial(_expert_kernel, n_rows // ROW_TILE),
        out_shape=jax.ShapeDtypeStruct((n_rows, D_MODEL), BF16),
        grid_spec=pltpu.PrefetchScalarGridSpec(
            num_scalar_prefetch=1,
            grid=(N_EXPERTS,),
            in_specs=[any_spec, any_spec, any_spec, any_spec, b_spec, b_spec, b_spec],
            out_specs=any_spec,
            scratch_shapes=[pltpu.VMEM((2, 3, D_MODEL, D_MODEL), F32)]
                           + [pltpu.VMEM((D_MODEL, D_MODEL), BF16)] * 3
                           + [pltpu.VMEM((TILE_SLOTS, ROW_TILE, ROW_W), BF16),
                              pltpu.VMEM((TILE_SLOTS, ROW_TILE, D_MODEL), BF16),
                              pltpu.SemaphoreType.DMA((2,)), pltpu.SemaphoreType.DMA((TILE_SLOTS,)),
                              pltpu.SemaphoreType.DMA((TILE_SLOTS,))],
        ),
        compiler_params=pltpu.CompilerParams(dimension_semantics=("arbitrary",), vmem_limit_bytes=VMEM_LIMIT),
        name="experts",
    )(tile_start, xs, w_gate, w_up, w_down, b_gate[:, None, :], b_up[:, None, :], b_down[:, None, :])


def _combine_kernel(gdst_ref, gdst_next_ref, lr_ref, y_ref, x1_ref, gt_ref, gf_ref, out_ref, loc_ref, sem):
    i = pl.program_id(0)
    slot = i % 2
    x1 = x1_ref[...]
    sb, l, _ = x1.shape

    @pl.when(i == 0)
    def _():
        for c in _group_copies(loc_ref, slot, y_ref, gdst_ref, sem, False):
            c.start()

    _wait_groups(loc_ref, slot, sem)

    prefetch = _group_copies(loc_ref, 1 - slot, y_ref, gdst_next_ref, sem, False)
    n_chunks = LOCAL_ROWS // MASK_ROWS
    per_chunk = LOCAL_GROUPS // n_chunks
    lr = lr_ref[...]
    moe = jnp.zeros((sb * l, D_MODEL), F32)
    for c in range(n_chunks):
        rows = loc_ref[slot, c * MASK_ROWS:(c + 1) * MASK_ROWS, :]
        moe = moe + lax.dot_general(_sort_matrix(lr, c), rows, (((0,), (0,)), ((), ())),
                                    preferred_element_type=F32)
        for cp in prefetch[c * per_chunk:(c + 1) * per_chunk]:
            cp.start()
    x2 = x1 + gt_ref[...] * moe.reshape(sb, l, D_MODEL)
    out_ref[...] = x2 * lax.rsqrt(jnp.mean(x2 * x2, axis=-1, keepdims=True) + EPS) * gf_ref[...]

    @pl.when(i == pl.num_programs(0) - 1)
    def _():
        _wait_groups(loc_ref, 1 - slot, sem)


def _combine(gdst, lr, y, x1, gt, g_final, sb, l, seq_div, blk_off):
    n3 = x1.shape[0]
    nblk = n3 // sb
    tm = sb * l
    assert tm == TOK_TILE
    return pl.pallas_call(
        _combine_kernel,
        out_shape=jax.ShapeDtypeStruct(x1.shape, F32),
        grid=(nblk,),
        in_specs=[pl.BlockSpec((1, 1, GDST_LANES), lambda i: (i + blk_off, 0, 0), memory_space=pltpu.SMEM),
                  pl.BlockSpec((1, 1, GDST_LANES), lambda i: (jnp.minimum(i + 1, nblk - 1) + blk_off, 0, 0),
                               memory_space=pltpu.SMEM),
                  pl.BlockSpec((TOP_K, tm), lambda i: (0, i + blk_off)),
                  pl.BlockSpec(memory_space=pl.ANY),
                  pl.BlockSpec((sb, l, D_MODEL), lambda i: (i, 0, 0)),
                  pl.BlockSpec((sb, 1, D_MODEL), lambda i: (i // seq_div, 0, 0)),
                  pl.BlockSpec((1, D_MODEL), lambda i: (0, 0))],
        out_specs=pl.BlockSpec((sb, l, D_MODEL), lambda i: (i, 0, 0)),
        scratch_shapes=[pltpu.VMEM((2, LOCAL_ROWS, D_MODEL), BF16), pltpu.SemaphoreType.DMA((2,))],
        compiler_params=pltpu.CompilerParams(dimension_semantics=("arbitrary",), vmem_limit_bytes=VMEM_LIMIT),
        name="combine",
    )(gdst, gdst, lr, y, x1, gt, g_final)


def kernel(x_prompt, x_sample, c_prompt, c_sample, state_ssm, state_conv, w_ada, b_ada, g_mix, w_in, g_v_a, w_spatial, b_spatial, g_out_a, conv_w, conv_b, dt_bias, a_log, d_skip, g_out_b, w_out, g_ffn, w_router, b_router, w_gate, b_gate, w_up, b_up, w_down, b_down, g_final):
    assert w_ada.shape[0] == 1, "single-layer step"
    p = dict(w_in=w_in[0], g_mix=g_mix[0], g_v_a=g_v_a[0], w_spatial=w_spatial[0], b_spatial=b_spatial[0],
             g_out_a=g_out_a[0], conv_w=conv_w[0], conv_b=conv_b[0], dt_bias=dt_bias[0], a_log=a_log[0],
             d_skip=d_skip[0], g_out_b=g_out_b[0])
    bp, lp, _ = x_prompt.shape
    bs, ls, _ = x_sample.shape
    tp, ts = bp * lp, bs * ls

    ada = _ada(jnp.concatenate([c_prompt, c_sample], axis=0), w_ada[0], b_ada[0][None, :])
    ada = ada.reshape(bp + bs, 6, 1, D_MODEL)
    ada_p = [ada[:bp, j] for j in range(6)]
    ada_s = [ada[bp:, j] for j in range(6)]

    fw = _front_weights(p)
    mixed_p, conv_p, ssm_p = _prompt_mixer(x_prompt, ada_p[0], ada_p[1], fw, p)
    mixed_s, v_s, conv_s, ssm_s = _sample_mixer(x_sample, ada_s[0], ada_s[1], state_ssm[0], state_conv[0], fw, p)

    w_out_b = w_out[0].astype(BF16)
    g_ffn2 = g_ffn[0][None, :]
    wr_t = w_router[0].T
    wr_hi = wr_t.astype(BF16)
    wr_lo = (wr_t - wr_hi.astype(F32)).astype(BF16)
    br = b_router[0][:, None]
    tps = lp // TOK_TILE
    sbs = TOK_TILE // ls
    xp3 = x_prompt.reshape(bp * tps, TOK_TILE, D_MODEL)
    x1_p, h2p_p, ids_p = _post(mixed_p, xp3, ada_p[2], ada_p[4], ada_p[3], w_out_b, g_ffn2, wr_hi, wr_lo, br,
                               1, TOK_TILE, tps)
    x1_s, h2p_s, ids_s = _post(mixed_s, x_sample, ada_s[2], ada_s[4], ada_s[3], w_out_b, g_ffn2, wr_hi, wr_lo,
                               br, sbs, ls, 1)

    n_blocks = (tp + ts) // TOK_TILE
    max_groups = (tp + ts) * TOP_K // ROW_GROUP + n_blocks * N_EXPERTS + N_EXPERTS * (TILE_GROUPS - 1)
    n_tiles = -(-max_groups // TILE_GROUPS)
    n_tiles_all = n_tiles + 2 * LOCAL_GROUPS // TILE_GROUPS
    lr, gdst, meta = _plan(jnp.concatenate([ids_p, ids_s], axis=1), n_tiles * TILE_GROUPS)

    xs = _dispatch(gdst, meta, lr, h2p_p, h2p_s, n_tiles_all)
    y = _experts(meta[0], xs, w_gate[0], b_gate[0], w_up[0], b_up[0], w_down[0], b_down[0])

    gf = g_final[None, :]
    y_p = _combine(gdst, lr, y, x1_p, ada_p[5], gf, 1, TOK_TILE, tps, 0).reshape(bp, lp, D_MODEL)
    y_s = _combine(gdst, lr, y, x1_s, ada_s[5], gf, sbs, ls, 1, tp // TOK_TILE)

    return (y_p, y_s, ssm_p[None], conv_p[None], ssm_s[None], conv_s[None], v_s.reshape(1, bs, ls, A_WIDTH))
```

```python
import functools
import math

import numpy as np
import jax
import jax.numpy as jnp
from jax import lax
from jax.experimental import pallas as pl
from jax.experimental.pallas import tpu as pltpu

F32 = jnp.float32
BF16 = jnp.bfloat16
I32 = jnp.int32

D_MODEL = 1024
A_WIDTH = 512
A_HEADS = 4
A_HEAD_DIM = 128
CHUNK = 128
B_WIDTH = 512
SSD_HEAD_DIM = 64
SSD_HEADS = 8
SSD_GROUPS = 2
SSD_STATE = 128
GROUP_W = B_WIDTH // SSD_GROUPS
CONV_K = 4
CONV_DIM = 1024
CONV_PAD = 8
N_EXPERTS = 32
TOP_K = 4
SWIGLU_LIMIT = 7.0
SWIGLU_ALPHA = 1.702
EPS = 1e-6
DECAY_MASKED = -1e30
LANES = 128

TOK_TILE = 512
SAMPLE_SEQ_TILE = 16
SEQ_UNROLL = 4
ROW_TILE = 256
ROW_GROUP = 16
TILE_GROUPS = ROW_TILE // ROW_GROUP
LOCAL_GROUPS = TOK_TILE * TOP_K // ROW_GROUP + N_EXPERTS
LOCAL_ROWS = LOCAL_GROUPS * ROW_GROUP
GDST_LANES = 256
ROW_W = D_MODEL + LANES
INFO_ROWS = 16
MASK_ROWS = 256
META_ROWS = 8
PROMPT_TILE = 512
PROMPT_SPLITS = 2
TILE_SLOTS = 4
VMEM_LIMIT = 56 * 1024 * 1024


def _dot(a, b):
    return jnp.dot(a, b, preferred_element_type=F32)


def _dot_nt(a, b):
    return lax.dot_general(a, b, (((1,), (1,)), ((), ())), preferred_element_type=F32)


def _split(x):
    hi = x.astype(BF16)
    lo = (x - hi.astype(F32)).astype(BF16)
    return hi, lo


def _dot_exact_l(t, x):
    hi, lo = _split(x)
    return _dot(t, hi) + _dot(t, lo)


def _dot_exact_r(x, t):
    hi, lo = _split(x)
    return _dot(hi, t) + _dot(lo, t)


def _silu(x):
    return x * jax.nn.sigmoid(x)


def _gelu(x):
    return 0.5 * x * (1.0 + lax.erf(x * (1.0 / math.sqrt(2.0))))


def _softplus(x):
    return jnp.maximum(x, 0.0) + jnp.log1p(jnp.exp(-jnp.abs(x)))


def _rms(x, g):
    return x * lax.rsqrt(jnp.mean(x * x, axis=-1, keepdims=True) + EPS) * g


def _ada_kernel(c_ref, w_ref, b_ref, o_ref):
    s_hi, s_lo = _split(_silu(c_ref[...]))
    w_hi, w_lo = _split(w_ref[...])
    o_ref[...] = _dot(s_hi, w_hi) + _dot(s_lo, w_hi) + _dot(s_hi, w_lo) + b_ref[...]


def _ada(c_all, w_ada, b_ada):
    m = c_all.shape[0]
    n = w_ada.shape[1]
    bn = 512
    return pl.pallas_call(
        _ada_kernel,
        out_shape=jax.ShapeDtypeStruct((m, n), F32),
        grid=(n // bn,),
        in_specs=[pl.BlockSpec((m, D_MODEL), lambda j: (0, 0)),
                  pl.BlockSpec((D_MODEL, bn), lambda j: (0, j)),
                  pl.BlockSpec((1, bn), lambda j: (0, j))],
        out_specs=pl.BlockSpec((m, bn), lambda j: (0, j)),
        compiler_params=pltpu.CompilerParams(dimension_semantics=("arbitrary",), vmem_limit_bytes=VMEM_LIMIT),
        name="ada",
    )(c_all, w_ada, b_ada)


def _mixer_front(x3, sh, sc, prev, refs, xp_ref):
    (g_mix, w_uvz, w_xbc, w_dt, wbd, bias_sp, g_v, g_oa, conv_w, conv_b, dt_bias, a_row) = refs
    sb, l, _ = x3.shape
    tm = sb * l
    xn = x3 * lax.rsqrt(jnp.mean(x3 * x3, axis=-1, keepdims=True) + EPS) * g_mix[...]
    h = (xn * (1.0 + sc) + sh).reshape(tm, D_MODEL)
    hb = h.astype(BF16)
    uvz = _dot(hb, w_uvz[...])
    xbc = _dot(hb, w_xbc[...])
    dt_raw = _dot(hb, w_dt[...])

    u = _gelu(uvz[:, :A_WIDTH])
    vg = _gelu(uvz[:, A_WIDTH:2 * A_WIDTH])
    z = uvz[:, 2 * A_WIDTH:]
    v_parts, s_parts = [], []
    for hd in range(A_HEADS):
        sl = slice(hd * A_HEAD_DIM, (hd + 1) * A_HEAD_DIM)
        vh = _rms(vg[:, sl], g_v[:, sl])
        v_parts.append(vh)
        s_parts.append(_dot(wbd[hd, :tm, :tm], vh.astype(BF16)))
    v = jnp.concatenate(v_parts, axis=1)
    s_a = jnp.concatenate(s_parts, axis=1) + bias_sp[:tm, :]
    out_a = _rms(u * s_a, g_oa[...])

    xp_ref[:, 0:CONV_PAD, :] = prev
    xp_ref[:, CONV_PAD:, :] = xbc.reshape(sb, l, CONV_DIM)
    xp = xp_ref[...]
    acc = conv_b[...] + xp[:, CONV_PAD:, :] * conv_w[CONV_K - 1:CONV_K, :]
    for s in range(1, CONV_K):
        back = pltpu.roll(xp, s, axis=1)[:, CONV_PAD:, :]
        acc = acc + back * conv_w[CONV_K - 1 - s:CONV_K - s, :]
    xc = _silu(acc).reshape(tm, CONV_DIM)
    dt = _softplus(dt_raw + dt_bias[...])
    d_a = dt * a_row[...]
    return out_a, v, z, xc, dt, d_a


def _ssd_chunk(xs, bm, cm, dt, d_a, cref):
    tril, ones, expand, neg_mask = cref
    cs = _dot_exact_l(tril[...], d_a)
    cs_t = cs.T
    cs_tot = _dot_exact_l(ones[...], d_a)
    vals = jnp.concatenate([dt, jnp.exp(cs_tot - cs), jnp.exp(cs)], axis=0)
    vals_e = _dot_exact_r(vals, expand[...])
    n = xs.shape[0]
    dt_e, dte_e, e_e = vals_e[:n], vals_e[n:2 * n], vals_e[2 * n:]
    xdt = xs * dt_e
    xdtd = xdt * dte_e
    neg = neg_mask[...]
    row_lt_half = lax.broadcasted_iota(I32, (2 * n, LANES), 0) < n
    lane_lt_half = lax.broadcasted_iota(I32, (2 * n, LANES), 1) < SSD_HEAD_DIM
    y_parts = []
    for g in range(SSD_GROUPS):
        cb = _dot_nt(cm[:, g * SSD_STATE:(g + 1) * SSD_STATE].astype(BF16),
                     bm[:, g * SSD_STATE:(g + 1) * SSD_STATE].astype(BF16))
        for hp in range(SSD_HEADS // SSD_GROUPS // 2):
            h0 = g * (SSD_HEADS // SSD_GROUPS) + 2 * hp
            ms = []
            for hh in (h0, h0 + 1):
                diff = cs[:, hh:hh + 1] - cs_t[hh:hh + 1, :]
                ms.append((cb * jnp.exp(diff + neg)).astype(BF16))
            pair = xdt[:, h0 * SSD_HEAD_DIM:(h0 + 2) * SSD_HEAD_DIM]
            rhs = jnp.where(row_lt_half == lane_lt_half, jnp.concatenate([pair, pair], axis=0), 0.0).astype(BF16)
            y_parts.append(_dot(jnp.concatenate(ms, axis=1), rhs))
    y_diag = jnp.concatenate(y_parts, axis=1)
    return y_diag, e_e, xdtd, cs_tot


def _mixer_back(y, xs, z, out_a, dskip_e, g_ob):
    y = y + xs * dskip_e
    gated = y * _silu(z)
    parts = [_rms(gated[:, g * GROUP_W:(g + 1) * GROUP_W], g_ob[:, g * GROUP_W:(g + 1) * GROUP_W])
             for g in range(SSD_GROUPS)]
    return jnp.concatenate([out_a] + parts, axis=1).astype(BF16)


N_FRONT = 12
N_SSD = 4


def _prompt_mixer_kernel(tiles_per_seq, x_ref, sh_ref, sc_ref, *rest):
    front = rest[:N_FRONT]
    cref = rest[N_FRONT:N_FRONT + N_SSD]
    dskip_e, g_ob = rest[N_FRONT + N_SSD:N_FRONT + N_SSD + 2]
    mixed_ref, conv_out_ref, ssm_out_ref = rest[N_FRONT + N_SSD + 2:N_FRONT + N_SSD + 5]
    xp_ref, carry_ref, st_ref = rest[N_FRONT + N_SSD + 5:]
    i = pl.program_id(0)
    first = (i % tiles_per_seq) == 0

    @pl.when(first)
    def _():
        carry_ref[...] = jnp.zeros_like(carry_ref)
        st_ref[...] = jnp.zeros_like(st_ref)

    l = x_ref.shape[1]
    hl = l // PROMPT_SPLITS
    parts = []
    prev = carry_ref[...]
    for hh in range(PROMPT_SPLITS):
        xp_h = xp_ref.at[hh]
        parts.append(_mixer_front(x_ref[:, hh * hl:(hh + 1) * hl, :], sh_ref[...], sc_ref[...], prev, front, xp_h))
        prev = xp_h[:, hl:hl + CONV_PAD, :]
    carry_ref[...] = prev
    out_a, _, z, xc, dt, d_a = [jnp.concatenate([p[j] for p in parts], axis=0) for j in range(6)]
    xs = xc[:, :B_WIDTH]
    y_rows = []
    for c in range(l // CHUNK):
        r = slice(c * CHUNK, (c + 1) * CHUNK)
        bm = xc[r, B_WIDTH:B_WIDTH + SSD_GROUPS * SSD_STATE]
        cm = xc[r, B_WIDTH + SSD_GROUPS * SSD_STATE:]
        y_diag, e_e, xdtd, _ = _ssd_chunk(xs[r], bm, cm, dt[r], d_a[r], cref)
        st = st_ref[...]
        y_off, upd = [], []
        for g in range(SSD_GROUPS):
            gs = slice(g * GROUP_W, (g + 1) * GROUP_W)
            ns = slice(g * SSD_STATE, (g + 1) * SSD_STATE)
            y_off.append(_dot(cm[:, ns].astype(BF16), st[:, gs].astype(BF16)))
            upd.append(_dot(bm[:, ns].T.astype(BF16), xdtd[:, gs].astype(BF16)))
        y_rows.append(y_diag + jnp.concatenate(y_off, axis=1) * e_e)
        st_ref[...] = st * e_e[CHUNK - 1:CHUNK, :] + jnp.concatenate(upd, axis=1)
    y = jnp.concatenate(y_rows, axis=0)
    mixed_ref[...] = _mixer_back(y, xs, z, out_a, dskip_e[...], g_ob[...])

    @pl.when((i % tiles_per_seq) == tiles_per_seq - 1)
    def _():
        conv_out_ref[...] = xp_ref[PROMPT_SPLITS - 1, :, hl + CONV_PAD - (CONV_K - 1):hl + CONV_PAD, :]
        ssm_out_ref[0] = st_ref[...].T


def _sample_mixer_kernel(x_ref, sh_ref, sc_ref, prev_ref, ssm0_ref, *rest):
    front = rest[:N_FRONT]
    cref = rest[N_FRONT:N_FRONT + N_SSD]
    dskip_e, g_ob, selseq = rest[N_FRONT + N_SSD:N_FRONT + N_SSD + 3]
    mixed_ref, v_ref, conv_out_ref, ssm_out_ref = rest[N_FRONT + N_SSD + 3:N_FRONT + N_SSD + 7]
    xp_ref, yoff_ref, cbf_ref, bbf_ref, t1_ref, dtab_ref = rest[N_FRONT + N_SSD + 7:]
    x3 = x_ref[...]
    sb, l, _ = x3.shape
    tm = sb * l
    out_a, v, z, xc, dt, d_a = _mixer_front(x3, sh_ref[...], sc_ref[...], prev_ref[...], front, xp_ref)
    v_ref[...] = v
    conv_out_ref[...] = xp_ref[:, l + CONV_PAD - (CONV_K - 1):l + CONV_PAD, :]
    xs = xc[:, :B_WIDTH]
    bm = xc[:, B_WIDTH:B_WIDTH + SSD_GROUPS * SSD_STATE]
    cm = xc[:, B_WIDTH + SSD_GROUPS * SSD_STATE:]
    y_diag, e_e, xdtd, _ = _ssd_chunk(xs, bm, cm, dt, d_a, cref)

    e_tot = jnp.exp(_dot_exact_l(selseq[...], d_a))
    for hh in range(SSD_HEADS):
        dtab_ref[hh] = jnp.broadcast_to(e_tot[:, hh:hh + 1], (sb, LANES))
    cbf_ref[...] = cm
    bbf_ref[...] = bm
    for g in range(SSD_GROUPS):
        t1_ref[g] = xdtd[:, g * GROUP_W:(g + 1) * GROUP_W].T.astype(BF16)
    seq_of_row = lax.broadcasted_iota(I32, (tm, SSD_STATE), 0) // l
    heads_per_group = SSD_HEADS // SSD_GROUPS

    def one_seq(j):
        r0 = pl.multiple_of(j * l, l)
        s0 = ssm0_ref[j]
        for g in range(SSD_GROUPS):
            ns = slice(g * SSD_STATE, (g + 1) * SSD_STATE)
            s0g = s0[g * heads_per_group:(g + 1) * heads_per_group].reshape(GROUP_W, SSD_STATE)
            cj = cbf_ref[pl.ds(r0, l), ns].astype(BF16)
            yoff_ref[pl.ds(r0, l), g * GROUP_W:(g + 1) * GROUP_W] = _dot_nt(cj, s0g.astype(BF16))
            bmask = jnp.where(seq_of_row == j, bbf_ref[:, ns], 0.0).astype(BF16)
            upd = _dot(t1_ref[g], bmask)
            for hq in range(heads_per_group):
                hh = g * heads_per_group + hq
                dec = dtab_ref[hh, pl.ds(j, 1), :]
                ssm_out_ref[j, hh] = s0[hh] * dec + upd[hq * SSD_HEAD_DIM:(hq + 1) * SSD_HEAD_DIM]

    def body(jj, carry):
        for u in range(SEQ_UNROLL):
            one_seq(jj * SEQ_UNROLL + u)
        return carry

    lax.fori_loop(0, sb // SEQ_UNROLL, body, 0)
    y = y_diag + yoff_ref[...] * e_e
    mixed_ref[...] = _mixer_back(y, xs, z, out_a, dskip_e[...], g_ob[...])


def _const_spec(a):
    nd = a.ndim
    return pl.BlockSpec(a.shape, lambda i, _nd=nd: (0,) * _nd)


def _spatial_consts(w_spatial, b_spatial, cl, tm):
    w = jnp.where(jnp.tril(jnp.ones((cl, cl), bool)), w_spatial[:, :cl, :cl], 0.0)
    eye = jnp.eye(tm // cl, dtype=F32)
    wbd = jnp.einsum("ab,hts->hatbs", eye, w).reshape(A_HEADS, tm, tm).astype(BF16)
    bias = jnp.tile(jnp.repeat(b_spatial[:, :cl].T, A_HEAD_DIM, axis=1), (tm // cl, 1))
    return wbd, bias


def _ssd_consts(cl):
    r = np.arange(CHUNK)
    same = (r[:, None] // cl) == (r[None, :] // cl)
    tril = same & (r[:, None] >= r[None, :])
    expand = np.zeros((LANES, B_WIDTH), np.float32)
    for hh in range(SSD_HEADS):
        expand[hh, hh * SSD_HEAD_DIM:(hh + 1) * SSD_HEAD_DIM] = 1.0
    return (jnp.asarray(tril, BF16), jnp.asarray(same, BF16), jnp.asarray(expand, BF16),
            jnp.asarray(np.where(tril, 0.0, DECAY_MASKED), F32))


def _front_weights(p):
    w_in = p["w_in"]
    c0, c1 = 3 * A_WIDTH, 3 * A_WIDTH + CONV_DIM
    w_dt = w_in[:, c1:]
    pad8 = lambda v: jnp.pad(v, (0, LANES - SSD_HEADS))
    a = -jnp.exp(p["a_log"])
    return dict(
        g_mix=p["g_mix"][None, :],
        w_uvz=w_in[:, :c0].astype(BF16),
        w_xbc=w_in[:, c0:c1].astype(BF16),
        w_dt=jnp.pad(w_dt, ((0, 0), (0, LANES - SSD_HEADS))).astype(BF16),
        g_v=p["g_v_a"][None, :], g_oa=p["g_out_a"][None, :],
        conv_w=p["conv_w"], conv_b=p["conv_b"][None, :],
        dt_bias=pad8(p["dt_bias"])[None, :],
        a_row=pad8(a)[None, :],
        dskip_e=jnp.repeat(p["d_skip"], SSD_HEAD_DIM)[None, :],
        g_ob=p["g_out_b"][None, :],
    )


def _front_list(fw, wbd, bias_sp):
    return [fw["g_mix"], fw["w_uvz"], fw["w_xbc"], fw["w_dt"], wbd, bias_sp, fw["g_v"], fw["g_oa"],
            fw["conv_w"], fw["conv_b"], fw["dt_bias"], fw["a_row"]]


def _prompt_mixer(x, sh, sc, fw, p):
    nseq, lseq, _ = x.shape
    tile = PROMPT_TILE
    tps = lseq // tile
    nt = nseq * tps
    x4 = x.reshape(nt, tile, D_MODEL)
    wbd, bias_sp = _spatial_consts(p["w_spatial"], p["b_spatial"], CHUNK, tile // PROMPT_SPLITS)
    consts = _front_list(fw, wbd, bias_sp) + list(_ssd_consts(CHUNK)) + [fw["dskip_e"], fw["g_ob"]]
    seq_spec = pl.BlockSpec((1, 1, D_MODEL), lambda i: (i // tps, 0, 0))
    mixed, conv_new, ssm_new = pl.pallas_call(
        functools.partial(_prompt_mixer_kernel, tps),
        out_shape=(jax.ShapeDtypeStruct((nt * tile, D_MODEL), BF16),
                   jax.ShapeDtypeStruct((nseq, CONV_K - 1, CONV_DIM), F32),
                   jax.ShapeDtypeStruct((nseq, B_WIDTH, SSD_STATE), F32)),
        grid=(nt,),
        in_specs=[pl.BlockSpec((1, tile, D_MODEL), lambda i: (i, 0, 0)), seq_spec, seq_spec]
                 + [_const_spec(a) for a in consts],
        out_specs=(pl.BlockSpec((tile, D_MODEL), lambda i: (i, 0)),
                   pl.BlockSpec((1, CONV_K - 1, CONV_DIM), lambda i: (i // tps, 0, 0)),
                   pl.BlockSpec((1, B_WIDTH, SSD_STATE), lambda i: (i // tps, 0, 0))),
        scratch_shapes=[pltpu.VMEM((PROMPT_SPLITS, 1, tile // PROMPT_SPLITS + CONV_PAD, CONV_DIM), F32),
                        pltpu.VMEM((1, CONV_PAD, CONV_DIM), F32),
                        pltpu.VMEM((SSD_STATE, B_WIDTH), F32)],
        compiler_params=pltpu.CompilerParams(dimension_semantics=("arbitrary",), vmem_limit_bytes=VMEM_LIMIT),
        name="prompt_mixer",
    )(x4, sh, sc, *consts)
    return mixed, conv_new, ssm_new.reshape(nseq, SSD_HEADS, SSD_HEAD_DIM, SSD_STATE)


def _sample_mixer(x, sh, sc, state_ssm, state_conv, fw, p):
    nseq, l, _ = x.shape
    sb = SAMPLE_SEQ_TILE
    tm = sb * l
    assert tm == CHUNK
    wbd, bias_sp = _spatial_consts(p["w_spatial"], p["b_spatial"], l, tm)
    selseq = jnp.asarray((np.arange(tm)[None, :] // l) == np.arange(sb)[:, None], BF16)
    consts = _front_list(fw, wbd, bias_sp) + list(_ssd_consts(l)) + [fw["dskip_e"], fw["g_ob"], selseq]
    prev = jnp.pad(state_conv, ((0, 0), (CONV_PAD - (CONV_K - 1), 0), (0, 0)))
    seq_spec = pl.BlockSpec((sb, 1, D_MODEL), lambda i: (i, 0, 0))
    ssm_spec = pl.BlockSpec((sb, SSD_HEADS, SSD_HEAD_DIM, SSD_STATE), lambda i: (i, 0, 0, 0))
    return pl.pallas_call(
        _sample_mixer_kernel,
        out_shape=(jax.ShapeDtypeStruct((nseq * l, D_MODEL), BF16),
                   jax.ShapeDtypeStruct((nseq * l, A_WIDTH), F32),
                   jax.ShapeDtypeStruct((nseq, CONV_K - 1, CONV_DIM), F32),
                   jax.ShapeDtypeStruct(state_ssm.shape, F32)),
        grid=(nseq // sb,),
        in_specs=[pl.BlockSpec((sb, l, D_MODEL), lambda i: (i, 0, 0)), seq_spec, seq_spec,
                  pl.BlockSpec((sb, CONV_PAD, CONV_DIM), lambda i: (i, 0, 0)), ssm_spec]
                 + [_const_spec(a) for a in consts],
        out_specs=(pl.BlockSpec((tm, D_MODEL), lambda i: (i, 0)),
                   pl.BlockSpec((tm, A_WIDTH), lambda i: (i, 0)),
                   pl.BlockSpec((sb, CONV_K - 1, CONV_DIM), lambda i: (i, 0, 0)),
                   ssm_spec),
        scratch_shapes=[pltpu.VMEM((sb, l + CONV_PAD, CONV_DIM), F32),
                        pltpu.VMEM((tm, B_WIDTH), F32),
                        pltpu.VMEM((tm, SSD_GROUPS * SSD_STATE), F32),
                        pltpu.VMEM((tm, SSD_GROUPS * SSD_STATE), F32),
                        pltpu.VMEM((SSD_GROUPS, GROUP_W, tm), BF16),
                        pltpu.VMEM((SSD_HEADS, sb, LANES), F32)],
        compiler_params=pltpu.CompilerParams(dimension_semantics=("arbitrary",), vmem_limit_bytes=VMEM_LIMIT),
        name="sample_mixer",
    )(x, sh, sc, prev, state_ssm, *consts)


def _post_kernel(mixed_ref, x_ref, gt_ref, sc_ref, sh_ref, w_out_ref, g_ffn_ref, wr_hi_ref, wr_lo_ref, br_ref,
                 x1_ref, h2p_ref, ids_ref):
    x3 = x_ref[...]
    sb, l, _ = x3.shape
    tm = sb * l
    mix = _dot(mixed_ref[...], w_out_ref[...]).reshape(sb, l, D_MODEL)
    x1 = x3 + gt_ref[...] * mix
    x1_ref[...] = x1
    xn = x1 * lax.rsqrt(jnp.mean(x1 * x1, axis=-1, keepdims=True) + EPS) * g_ffn_ref[...]
    h2 = (xn * (1.0 + sc_ref[...]) + sh_ref[...]).reshape(tm, D_MODEL)
    h_hi, h_lo = _split(h2)
    h2p_ref[:, :D_MODEL] = h_hi
    logits = (_dot_nt(wr_hi_ref[...], h_hi) + _dot_nt(wr_hi_ref[...], h_lo) + _dot_nt(wr_lo_ref[...], h_hi)
              + br_ref[...])
    e_iota = lax.broadcasted_iota(I32, logits.shape, 0)
    vals, idxs = [], []
    for _ in range(TOP_K):
        m = jnp.max(logits, axis=0, keepdims=True)
        idx = jnp.min(jnp.where(logits == m, e_iota, N_EXPERTS), axis=0, keepdims=True)
        vals.append(m)
        idxs.append(idx)
        logits = jnp.where(e_iota == idx, -jnp.inf, logits)
    ex = [jnp.exp(v - vals[0]) for v in vals]
    tot = ex[0] + ex[1] + ex[2] + ex[3]
    ids = jnp.concatenate(idxs, axis=0)
    ids_ref[...] = ids
    wts = jnp.concatenate([e / tot for e in ex], axis=0)
    w_hi = wts.astype(BF16).astype(F32)
    info = jnp.concatenate([ids.astype(F32), w_hi, wts - w_hi, jnp.zeros((TOP_K, tm), F32)], axis=0).astype(BF16)
    r = lax.broadcasted_iota(I32, (INFO_ROWS, LANES), 0)
    c = lax.broadcasted_iota(I32, (INFO_ROWS, LANES), 1)
    place = jnp.where(r == c, 1.0, 0.0).astype(BF16)
    h2p_ref[:, D_MODEL:] = lax.dot_general(info, place, (((0,), (0,)), ((), ())),
                                           preferred_element_type=F32).astype(BF16)


def _post(mixed, x, gt, sc, sh, w_out_b, g_ffn, wr_hi, wr_lo, br, sb, l, seq_div):
    n3, _, _ = x.shape
    nblk = n3 // sb
    tm = sb * l
    t = n3 * l
    ada_spec = pl.BlockSpec((sb, 1, D_MODEL), lambda i: (i // seq_div, 0, 0))
    consts = [w_out_b, g_ffn, wr_hi, wr_lo, br]
    return pl.pallas_call(
        _post_kernel,
        out_shape=(jax.ShapeDtypeStruct(x.shape, F32),
                   jax.ShapeDtypeStruct((t, ROW_W), BF16),
                   jax.ShapeDtypeStruct((TOP_K, t), I32)),
        grid=(nblk,),
        in_specs=[pl.BlockSpec((tm, D_MODEL), lambda i: (i, 0)),
                  pl.BlockSpec((sb, l, D_MODEL), lambda i: (i, 0, 0)), ada_spec, ada_spec, ada_spec]
                 + [_const_spec(a) for a in consts],
        out_specs=(pl.BlockSpec((sb, l, D_MODEL), lambda i: (i, 0, 0)),
                   pl.BlockSpec((tm, ROW_W), lambda i: (i, 0)),
                   pl.BlockSpec((TOP_K, tm), lambda i: (0, i))),
        compiler_params=pltpu.CompilerParams(dimension_semantics=("arbitrary",), vmem_limit_bytes=VMEM_LIMIT),
        name="post",
    )(mixed, x, gt, sc, sh, *consts)


def _strict_upper(n):
    r = lax.broadcasted_iota(I32, (n, n), 0)
    c = lax.broadcasted_iota(I32, (n, n), 1)
    return jnp.where(r < c, 1.0, 0.0).astype(BF16)


def _expert_prefix(col):
    r = lax.broadcasted_iota(I32, (N_EXPERTS, N_EXPERTS), 0)
    c = lax.broadcasted_iota(I32, (N_EXPERTS, N_EXPERTS), 1)
    as_row = jnp.sum(jnp.where(r == c, col, 0.0), axis=0, keepdims=True)
    return jnp.sum(jnp.where(c < r, as_row, 0.0), axis=1, keepdims=True)


def _plan_kernel(dump_group, ids_ref, lr_ref, gdst_ref, tile_e_ref):
    tm = TOK_TILE
    nb = ids_ref.shape[1] // tm
    blk_lane = lax.broadcasted_iota(I32, (N_EXPERTS, LANES), 1)

    def block_masks(b):
        ids = ids_ref[:, pl.ds(pl.multiple_of(b * tm, tm), tm)]
        e_iota = lax.broadcasted_iota(I32, (N_EXPERTS, tm), 0)
        onehot = [ids[k:k + 1, :] == e_iota for k in range(TOP_K)]
        sel = (onehot[0] | onehot[1]) | (onehot[2] | onehot[3])
        return onehot, jnp.where(sel, 1.0, 0.0)

    def count(b, seg):
        _, m = block_masks(b)
        seg_b = jnp.ceil(jnp.sum(m, axis=1, keepdims=True) * (1.0 / ROW_GROUP))
        return jnp.where(blk_lane == b, seg_b, seg)

    seg = lax.fori_loop(0, nb, count, jnp.zeros((N_EXPERTS, LANES), F32))
    tot = jnp.sum(seg, axis=1, keepdims=True)
    padded = jnp.ceil(tot * (1.0 / TILE_GROUPS)) * TILE_GROUPS
    gstart = _expert_prefix(padded)
    gb = gstart + _dot(seg.astype(BF16), _strict_upper(LANES))
    r = lax.broadcasted_iota(I32, (N_EXPERTS, LANES), 0)
    as_row = lambda col: jnp.sum(jnp.where(r == blk_lane, col, 0.0), axis=0, keepdims=True)
    n_used = jnp.sum(padded, axis=0, keepdims=True)
    lane = lax.broadcasted_iota(I32, (1, LANES), 1)
    tiles = jnp.where(lane == N_EXPERTS, n_used, as_row(gstart)) * (1.0 / TILE_GROUPS)
    meta = jnp.concatenate([tiles, as_row(gstart + tot), as_row(padded - tot),
                            jnp.zeros((META_ROWS - 3, LANES), F32)], axis=0)
    tile_e_ref[...] = meta.astype(I32)
    upper = _strict_upper(tm)

    def place(b, carry):
        onehot, m = block_masks(b)
        seg_b = jnp.sum(jnp.where(blk_lane == b, seg, 0.0), axis=1, keepdims=True)
        gb_b = jnp.sum(jnp.where(blk_lane == b, gb, 0.0), axis=1, keepdims=True)
        loc_b = _expert_prefix(seg_b)
        before = _dot(m.astype(BF16), upper) + loc_b * ROW_GROUP
        lr_ref[:, pl.ds(pl.multiple_of(b * tm, tm), tm)] = jnp.concatenate(
            [jnp.sum(jnp.where(onehot[k], before, 0.0), axis=0, keepdims=True) for k in range(TOP_K)],
            axis=0).astype(I32)
        g = lax.broadcasted_iota(I32, (N_EXPERTS, GDST_LANES), 1).astype(F32)
        inside = (loc_b <= g) & (g < loc_b + seg_b)
        dst = jnp.sum(jnp.where(inside, gb_b + g - loc_b, 0.0), axis=0, keepdims=True)
        used = jnp.sum(jnp.where(inside, 1.0, 0.0), axis=0, keepdims=True) > 0.5
        dump = dump_group + lax.convert_element_type(b % 2, F32) * LOCAL_GROUPS + g[0:1, :]
        gdst_ref[b] = jnp.where(used, dst, dump).astype(I32)
        return carry

    lax.fori_loop(0, nb, place, 0)


def _plan(ids, dump_group):
    t = ids.shape[1]
    nb = t // TOK_TILE
    assert nb <= LANES
    return pl.pallas_call(
        functools.partial(_plan_kernel, float(dump_group)),
        out_shape=(jax.ShapeDtypeStruct((TOP_K, t), I32),
                   jax.ShapeDtypeStruct((nb, 1, GDST_LANES), I32),
                   jax.ShapeDtypeStruct((META_ROWS, LANES), I32)),
        name="plan",
    )(ids)


def _sort_matrix(lr, c):
    r_iota = lax.broadcasted_iota(I32, (MASK_ROWS, lr.shape[1]), 0) + c * MASK_ROWS
    p = jnp.where(r_iota == lr[TOP_K - 1:TOP_K, :], 1.0, 0.0)
    for k in range(TOP_K - 1):
        p = jnp.where(r_iota == lr[k:k + 1, :], 1.0, p)
    return p.astype(BF16)


def _group_copies(loc_ref, slot, far_ref, gdst_ref, sem, to_far):
    copies = []
    for g in range(LOCAL_GROUPS):
        dst = pl.multiple_of(gdst_ref[0, 0, g] * ROW_GROUP, ROW_GROUP)
        near = loc_ref.at[slot, pl.ds(g * ROW_GROUP, ROW_GROUP)]
        far = far_ref.at[pl.ds(dst, ROW_GROUP)]
        copies.append(pltpu.make_async_copy(near, far, sem.at[slot]) if to_far
                      else pltpu.make_async_copy(far, near, sem.at[slot]))
    return copies


def _wait_groups(loc_ref, slot, sem):
    pltpu.make_async_copy(loc_ref.at[slot], loc_ref.at[slot], sem.at[slot]).wait()


def _dispatch_kernel(n_first, n_tiles_all, gdst_ref, meta_ref, lr_ref, ha_ref, hb_ref, xs_ref,
                     h_ref, loc_ref, zero_ref, sem, zsem):
    i = pl.program_id(0)
    last = pl.num_programs(0) - 1
    slot = i % 2

    @pl.when(i < n_first)
    def _():
        h_ref[...] = ha_ref[...]

    @pl.when(i >= n_first)
    def _():
        h_ref[...] = hb_ref[...]

    lr = lr_ref[...]
    copies = _group_copies(loc_ref, slot, xs_ref, gdst_ref, sem, True)
    n_chunks = LOCAL_ROWS // MASK_ROWS
    per_chunk = LOCAL_GROUPS // n_chunks
    for c in range(n_chunks):
        loc_ref[slot, c * MASK_ROWS:(c + 1) * MASK_ROWS, :] = _dot(_sort_matrix(lr, c), h_ref[...]).astype(BF16)
        for cp in copies[c * per_chunk:(c + 1) * per_chunk]:
            cp.start()

    @pl.when(i > 0)
    def _():
        _wait_groups(loc_ref, 1 - slot, sem)

    @pl.when(i == last)
    def _():
        _wait_groups(loc_ref, slot, sem)
        zero_ref[...] = jnp.zeros_like(zero_ref)

        def pad_copy(e, j):
            row = pl.multiple_of((meta_ref[1, e] + j) * ROW_GROUP, ROW_GROUP)
            return pltpu.make_async_copy(zero_ref.at[pl.ds(0, ROW_GROUP)], xs_ref.at[pl.ds(row, ROW_GROUP)], zsem)

        def tile_copy(t):
            row = pl.multiple_of(t * ROW_TILE, ROW_TILE)
            return pltpu.make_async_copy(zero_ref, xs_ref.at[pl.ds(row, ROW_TILE)], zsem)

        def pads(fn):
            def body(e, carry):
                for j in range(TILE_GROUPS - 1):
                    @pl.when(j < meta_ref[2, e])
                    def _():
                        fn(pad_copy(e, j))
                return carry
            lax.fori_loop(0, N_EXPERTS, body, 0)

        def tiles(fn):
            def body(t, carry):
                fn(tile_copy(t))
                return carry
            lax.fori_loop(meta_ref[0, N_EXPERTS], n_tiles_all, body, 0)

        pads(lambda c: c.start())
        tiles(lambda c: c.start())
        pads(lambda c: c.wait())
        tiles(lambda c: c.wait())


def _dispatch(gdst, meta, lr, h_a, h_b, n_tiles_all):
    tm = TOK_TILE
    na, nb2 = h_a.shape[0] // tm, h_b.shape[0] // tm
    return pl.pallas_call(
        functools.partial(_dispatch_kernel, na, n_tiles_all),
        out_shape=jax.ShapeDtypeStruct((n_tiles_all * ROW_TILE, ROW_W), BF16),
        grid=(na + nb2,),
        in_specs=[pl.BlockSpec((1, 1, GDST_LANES), lambda i: (i, 0, 0), memory_space=pltpu.SMEM),
                  pl.BlockSpec((META_ROWS, LANES), lambda i: (0, 0), memory_space=pltpu.SMEM),
                  pl.BlockSpec((TOP_K, tm), lambda i: (0, i)),
                  pl.BlockSpec((tm, ROW_W), lambda i: (jnp.minimum(i, na - 1), 0)),
                  pl.BlockSpec((tm, ROW_W), lambda i: (jnp.maximum(i - na, 0), 0))],
        out_specs=pl.BlockSpec(memory_space=pl.ANY),
        scratch_shapes=[pltpu.VMEM((tm, ROW_W), BF16), pltpu.VMEM((2, LOCAL_ROWS, ROW_W), BF16),
                        pltpu.VMEM((ROW_TILE, ROW_W), BF16),
                        pltpu.SemaphoreType.DMA((2,)), pltpu.SemaphoreType.DMA],
        compiler_params=pltpu.CompilerParams(dimension_semantics=("arbitrary",), vmem_limit_bytes=VMEM_LIMIT),
        name="dispatch",
    )(gdst, meta, lr, h_a, h_b)


def _expert_kernel(n_tiles_all, ts_ref, xs_ref, wg_ref, wu_ref, wd_ref, bg_ref, bu_ref, bd_ref, y_ref,
                   wstage, wgb, wub, wdb, xbuf, ybuf, sem_w, sem_in, sem_out):
    e = pl.program_id(0)
    t0 = ts_ref[e]
    nt = ts_ref[e + 1] - t0
    wslot = e % 2

    def w_copies(ex, slot):
        return [pltpu.make_async_copy(w_ref.at[ex], wstage.at[slot, j], sem_w.at[slot])
                for j, w_ref in enumerate((wg_ref, wu_ref, wd_ref))]

    def in_copy(t, slot):
        rows = pl.ds(pl.multiple_of((t0 + t) * ROW_TILE, ROW_TILE), ROW_TILE)
        return pltpu.make_async_copy(xs_ref.at[rows], xbuf.at[slot], sem_in.at[slot])

    def out_copy(tile, slot):
        rows = pl.ds(pl.multiple_of(tile * ROW_TILE, ROW_TILE), ROW_TILE)
        return pltpu.make_async_copy(ybuf.at[slot], y_ref.at[rows], sem_out.at[slot])

    @pl.when((e == 0) & (nt > 0))
    def _():
        for c in w_copies(0, 0):
            c.start()

    for j in range(TILE_SLOTS):
        @pl.when(nt > j)
        def _():
            in_copy(j, j).start()

    @pl.when(e + 1 < N_EXPERTS)
    def _():
        @pl.when(ts_ref[e + 2] > ts_ref[e + 1])
        def _():
            for c in w_copies(e + 1, 1 - wslot):
                c.start(priority=1)

    @pl.when(nt > 0)
    def _():
        for c in w_copies(e, wslot):
            c.wait()
        wgb[...] = wstage[wslot, 0].astype(BF16)
        wub[...] = wstage[wslot, 1].astype(BF16)
        wdb[...] = wstage[wslot, 2].astype(BF16)
        e_f = e.astype(F32)

        def acquire(t):
            slot = t % TILE_SLOTS
            in_copy(t, slot).wait()

            @pl.when(t >= TILE_SLOTS)
            def _():
                out_copy(t0 + t - TILE_SLOTS, slot).wait()

        def compute(t):
            slot = t % TILE_SLOTS
            xw = xbuf[slot]
            x = xw[:, :D_MODEL]
            info = xw[:, D_MODEL:].astype(F32)
            w_row = jnp.zeros((ROW_TILE, 1), F32)
            for k in range(TOP_K):
                wk = info[:, TOP_K + k:TOP_K + k + 1] + info[:, 2 * TOP_K + k:2 * TOP_K + k + 1]
                w_row = w_row + jnp.where(info[:, k:k + 1] == e_f, wk, 0.0)
            g = jnp.minimum(_dot(x, wgb[...]) + bg_ref[0], SWIGLU_LIMIT)
            u = jnp.clip(_dot(x, wub[...]) + bu_ref[0], -SWIGLU_LIMIT, SWIGLU_LIMIT)
            act = g * jax.nn.sigmoid(SWIGLU_ALPHA * g) * (u + 1.0)
            ybuf[slot] = ((_dot(act.astype(BF16), wdb[...]) + bd_ref[0]) * w_row).astype(BF16)

        def release(t):
            slot = t % TILE_SLOTS
            out_copy(t0 + t, slot).start()

            @pl.when(t + TILE_SLOTS < nt)
            def _():
                in_copy(t + TILE_SLOTS, slot).start()

        def pair(p, carry):
            ta, tb = 2 * p, 2 * p + 1
            acquire(ta)
            acquire(tb)
            compute(ta)
            compute(tb)
            release(ta)
            release(tb)
            return carry

        lax.fori_loop(0, nt // 2, pair, 0)

        @pl.when(nt % 2 == 1)
        def _():
            acquire(nt - 1)
            compute(nt - 1)
            release(nt - 1)

        for j in range(1, TILE_SLOTS + 1):
            @pl.when(nt >= j)
            def _():
                out_copy(t0 + nt - j, (nt - j) % TILE_SLOTS).wait()

    @pl.when(e == pl.num_programs(0) - 1)
    def _():
        n_used = ts_ref[N_EXPERTS]
        ybuf[0] = jnp.zeros((ROW_TILE, D_MODEL), BF16)

        def zstart(tile, carry):
            out_copy(tile, 0).start()
            return carry

        def zwait(tile, carry):
            out_copy(tile, 0).wait()
            return carry

        lax.fori_loop(n_used, n_tiles_all, zstart, 0)
        lax.fori_loop(n_used, n_tiles_all, zwait, 0)


def _experts(tile_start, xs, w_gate, b_gate, w_up, b_up, w_down, b_down):
    n_rows = xs.shape[0]
    b_spec = pl.BlockSpec((1, 1, D_MODEL), lambda e, ts: (e, 0, 0))
    any_spec = pl.BlockSpec(memory_space=pl.ANY)
    return pl.pallas_call(
        functools.partial(_expert_kernel, n_rows // ROW_TILE),
        out_shape=jax.ShapeDtypeStruct((n_rows, D_MODEL), BF16),
        grid_spec=pltpu.PrefetchScalarGridSpec(
            num_scalar_prefetch=1,
            grid=(N_EXPERTS,),
            in_specs=[any_spec, any_spec, any_spec, any_spec, b_spec, b_spec, b_spec],
            out_specs=any_spec,
            scratch_shapes=[pltpu.VMEM((2, 3, D_MODEL, D_MODEL), F32)]
                           + [pltpu.VMEM((D_MODEL, D_MODEL), BF16)] * 3
                           + [pltpu.VMEM((TILE_SLOTS, ROW_TILE, ROW_W), BF16),
                              pltpu.VMEM((TILE_SLOTS, ROW_TILE, D_MODEL), BF16),
                              pltpu.SemaphoreType.DMA((2,)), pltpu.SemaphoreType.DMA((TILE_SLOTS,)),
                              pltpu.SemaphoreType.DMA((TILE_SLOTS,))],
        ),
        compiler_params=pltpu.CompilerParams(dimension_semantics=("arbitrary",), vmem_limit_bytes=VMEM_LIMIT),
        name="experts",
    )(tile_start, xs, w_gate, w_up, w_down, b_gate[:, None, :], b_up[:, None, :], b_down[:, None, :])


def _combine_kernel(gdst_ref, gdst_next_ref, lr_ref, y_ref, x1_ref, gt_ref, gf_ref, out_ref, loc_ref, sem):
    i = pl.program_id(0)
    slot = i % 2
    x1 = x1_ref[...]
    sb, l, _ = x1.shape

    @pl.when(i == 0)
    def _():
        for c in _group_copies(loc_ref, slot, y_ref, gdst_ref, sem, False):
            c.start()

    _wait_groups(loc_ref, slot, sem)

    prefetch = _group_copies(loc_ref, 1 - slot, y_ref, gdst_next_ref, sem, False)
    n_chunks = LOCAL_ROWS // MASK_ROWS
    per_chunk = LOCAL_GROUPS // n_chunks
    lr = lr_ref[...]
    moe = jnp.zeros((sb * l, D_MODEL), F32)
    for c in range(n_chunks):
        rows = loc_ref[slot, c * MASK_ROWS:(c + 1) * MASK_ROWS, :]
        moe = moe + lax.dot_general(_sort_matrix(lr, c), rows, (((0,), (0,)), ((), ())),
                                    preferred_element_type=F32)
        for cp in prefetch[c * per_chunk:(c + 1) * per_chunk]:
            cp.start()
    x2 = x1 + gt_ref[...] * moe.reshape(sb, l, D_MODEL)
    out_ref[...] = x2 * lax.rsqrt(jnp.mean(x2 * x2, axis=-1, keepdims=True) + EPS) * gf_ref[...]

    @pl.when(i == pl.num_programs(0) - 1)
    def _():
        _wait_groups(loc_ref, 1 - slot, sem)


def _combine(gdst, lr, y, x1, gt, g_final, sb, l, seq_div, blk_off):
    n3 = x1.shape[0]
    nblk = n3 // sb
    tm = sb * l
    assert tm == TOK_TILE
    return pl.pallas_call(
        _combine_kernel,
        out_shape=jax.ShapeDtypeStruct(x1.shape, F32),
        grid=(nblk,),
        in_specs=[pl.BlockSpec((1, 1, GDST_LANES), lambda i: (i + blk_off, 0, 0), memory_space=pltpu.SMEM),
                  pl.BlockSpec((1, 1, GDST_LANES), lambda i: (jnp.minimum(i + 1, nblk - 1) + blk_off, 0, 0),
                               memory_space=pltpu.SMEM),
                  pl.BlockSpec((TOP_K, tm), lambda i: (0, i + blk_off)),
                  pl.BlockSpec(memory_space=pl.ANY),
                  pl.BlockSpec((sb, l, D_MODEL), lambda i: (i, 0, 0)),
                  pl.BlockSpec((sb, 1, D_MODEL), lambda i: (i // seq_div, 0, 0)),
                  pl.BlockSpec((1, D_MODEL), lambda i: (0, 0))],
        out_specs=pl.BlockSpec((sb, l, D_MODEL), lambda i: (i, 0, 0)),
        scratch_shapes=[pltpu.VMEM((2, LOCAL_ROWS, D_MODEL), BF16), pltpu.SemaphoreType.DMA((2,))],
        compiler_params=pltpu.CompilerParams(dimension_semantics=("arbitrary",), vmem_limit_bytes=VMEM_LIMIT),
        name="combine",
    )(gdst, gdst, lr, y, x1, gt, g_final)


def kernel(x_prompt, x_sample, c_prompt, c_sample, state_ssm, state_conv, w_ada, b_ada, g_mix, w_in, g_v_a, w_spatial, b_spatial, g_out_a, conv_w, conv_b, dt_bias, a_log, d_skip, g_out_b, w_out, g_ffn, w_router, b_router, w_gate, b_gate, w_up, b_up, w_down, b_down, g_final):
    assert w_ada.shape[0] == 1, "single-layer step"
    p = dict(w_in=w_in[0], g_mix=g_mix[0], g_v_a=g_v_a[0], w_spatial=w_spatial[0], b_spatial=b_spatial[0],
             g_out_a=g_out_a[0], conv_w=conv_w[0], conv_b=conv_b[0], dt_bias=dt_bias[0], a_log=a_log[0],
             d_skip=d_skip[0], g_out_b=g_out_b[0])
    bp, lp, _ = x_prompt.shape
    bs, ls, _ = x_sample.shape
    tp, ts = bp * lp, bs * ls

    ada = _ada(jnp.concatenate([c_prompt, c_sample], axis=0), w_ada[0], b_ada[0][None, :])
    ada = ada.reshape(bp + bs, 6, 1, D_MODEL)
    ada_p = [ada[:bp, j] for j in range(6)]
    ada_s = [ada[bp:, j] for j in range(6)]

    fw = _front_weights(p)
    mixed_p, conv_p, ssm_p = _prompt_mixer(x_prompt, ada_p[0], ada_p[1], fw, p)
    mixed_s, v_s, conv_s, ssm_s = _sample_mixer(x_sample, ada_s[0], ada_s[1], state_ssm[0], state_conv[0], fw, p)

    w_out_b = w_out[0].astype(BF16)
    g_ffn2 = g_ffn[0][None, :]
    wr_t = w_router[0].T
    wr_hi = wr_t.astype(BF16)
    wr_lo = (wr_t - wr_hi.astype(F32)).astype(BF16)
    br = b_router[0][:, None]
    tps = lp // TOK_TILE
    sbs = TOK_TILE // ls
    xp3 = x_prompt.reshape(bp * tps, TOK_TILE, D_MODEL)
    x1_p, h2p_p, ids_p = _post(mixed_p, xp3, ada_p[2], ada_p[4], ada_p[3], w_out_b, g_ffn2, wr_hi, wr_lo, br,
                               1, TOK_TILE, tps)
    x1_s, h2p_s, ids_s = _post(mixed_s, x_sample, ada_s[2], ada_s[4], ada_s[3], w_out_b, g_ffn2, wr_hi, wr_lo,
                               br, sbs, ls, 1)

    n_blocks = (tp + ts) // TOK_TILE
    max_groups = (tp + ts) * TOP_K // ROW_GROUP + n_blocks * N_EXPERTS + N_EXPERTS * (TILE_GROUPS - 1)
    n_tiles = -(-max_groups // TILE_GROUPS)
    n_tiles_all = n_tiles + 2 * LOCAL_GROUPS // TILE_GROUPS
    lr, gdst, meta = _plan(jnp.concatenate([ids_p, ids_s], axis=1), n_tiles * TILE_GROUPS)

    xs = _dispatch(gdst, meta, lr, h2p_p, h2p_s, n_tiles_all)
    y = _experts(meta[0], xs, w_gate[0], b_gate[0], w_up[0], b_up[0], w_down[0], b_down[0])

    gf = g_final[None, :]
    y_p = _combine(gdst, lr, y, x1_p, ada_p[5], gf, 1, TOK_TILE, tps, 0).reshape(bp, lp, D_MODEL)
    y_s = _combine(gdst, lr, y, x1_s, ada_s[5], gf, sbs, ls, 1, tp // TOK_TILE)

    return (y_p, y_s, ssm_p[None], conv_p[None], ssm_s[None], conv_s[None], v_s.reshape(1, bs, ls, A_WIDTH))
```

```python
import functools
import math

import numpy as np
import jax
import jax.numpy as jnp
from jax import lax
from jax.experimental import pallas as pl
from jax.experimental.pallas import tpu as pltpu

F32 = jnp.float32
BF16 = jnp.bfloat16
I32 = jnp.int32

D_MODEL = 1024
A_WIDTH = 512
A_HEADS = 4
A_HEAD_DIM = 128
CHUNK = 128
B_WIDTH = 512
SSD_HEAD_DIM = 64
SSD_HEADS = 8
SSD_GROUPS = 2
SSD_STATE = 128
GROUP_W = B_WIDTH // SSD_GROUPS
CONV_K = 4
CONV_DIM = 1024
CONV_PAD = 8
N_EXPERTS = 32
TOP_K = 4
SWIGLU_LIMIT = 7.0
SWIGLU_ALPHA = 1.702
EPS = 1e-6
DECAY_MASKED = -1e30
LANES = 128

TOK_TILE = 512
SAMPLE_SEQ_TILE = 16
SEQ_UNROLL = 4
ROW_TILE = 256
ROW_GROUP = 16
TILE_GROUPS = ROW_TILE // ROW_GROUP
LOCAL_GROUPS = TOK_TILE * TOP_K // ROW_GROUP + N_EXPERTS
LOCAL_ROWS = LOCAL_GROUPS * ROW_GROUP
GDST_LANES = 256
ROW_W = D_MODEL + LANES
INFO_ROWS = 16
MASK_ROWS = 256
META_ROWS = 8
PROMPT_TILE = 512
PROMPT_SPLITS = 2
TILE_SLOTS = 4
VMEM_LIMIT = 56 * 1024 * 1024


def _dot(a, b):
    return jnp.dot(a, b, preferred_element_type=F32)


def _dot_nt(a, b):
    return lax.dot_general(a, b, (((1,), (1,)), ((), ())), preferred_element_type=F32)


def _split(x):
    hi = x.astype(BF16)
    lo = (x - hi.astype(F32)).astype(BF16)
    return hi, lo


def _dot_exact_l(t, x):
    hi, lo = _split(x)
    return _dot(t, hi) + _dot(t, lo)


def _dot_exact_r(x, t):
    hi, lo = _split(x)
    return _dot(hi, t) + _dot(lo, t)


def _silu(x):
    return x * jax.nn.sigmoid(x)


def _gelu(x):
    return 0.5 * x * (1.0 + lax.erf(x * (1.0 / math.sqrt(2.0))))


def _softplus(x):
    return jnp.maximum(x, 0.0) + jnp.log1p(jnp.exp(-jnp.abs(x)))


def _rms(x, g):
    return x * lax.rsqrt(jnp.mean(x * x, axis=-1, keepdims=True) + EPS) * g


def _ada_kernel(c_ref, w_ref, b_ref, o_ref):
    s_hi, s_lo = _split(_silu(c_ref[...]))
    w_hi, w_lo = _split(w_ref[...])
    o_ref[...] = _dot(s_hi, w_hi) + _dot(s_lo, w_hi) + _dot(s_hi, w_lo) + b_ref[...]


def _ada(c_all, w_ada, b_ada):
    m = c_all.shape[0]
    n = w_ada.shape[1]
    bn = 512
    return pl.pallas_call(
        _ada_kernel,
        out_shape=jax.ShapeDtypeStruct((m, n), F32),
        grid=(n // bn,),
        in_specs=[pl.BlockSpec((m, D_MODEL), lambda j: (0, 0)),
                  pl.BlockSpec((D_MODEL, bn), lambda j: (0, j)),
                  pl.BlockSpec((1, bn), lambda j: (0, j))],
        out_specs=pl.BlockSpec((m, bn), lambda j: (0, j)),
        compiler_params=pltpu.CompilerParams(dimension_semantics=("arbitrary",), vmem_limit_bytes=VMEM_LIMIT),
        name="ada",
    )(c_all, w_ada, b_ada)


def _mixer_front(x3, sh, sc, prev, refs, xp_ref):
    (g_mix, w_uvz, w_xbc, w_dt, wbd, bias_sp, g_v, g_oa, conv_w, conv_b, dt_bias, a_row) = refs
    sb, l, _ = x3.shape
    tm = sb * l
    xn = x3 * lax.rsqrt(jnp.mean(x3 * x3, axis=-1, keepdims=True) + EPS) * g_mix[...]
    h = (xn * (1.0 + sc) + sh).reshape(tm, D_MODEL)
    hb = h.astype(BF16)
    uvz = _dot(hb, w_uvz[...])
    xbc = _dot(hb, w_xbc[...])
    dt_raw = _dot(hb, w_dt[...])

    u = _gelu(uvz[:, :A_WIDTH])
    vg = _gelu(uvz[:, A_WIDTH:2 * A_WIDTH])
    z = uvz[:, 2 * A_WIDTH:]
    v_parts, s_parts = [], []
    for hd in range(A_HEADS):
        sl = slice(hd * A_HEAD_DIM, (hd + 1) * A_HEAD_DIM)
        vh = _rms(vg[:, sl], g_v[:, sl])
        v_parts.append(vh)
        s_parts.append(_dot(wbd[hd, :tm, :tm], vh.astype(BF16)))
    v = jnp.concatenate(v_parts, axis=1)
    s_a = jnp.concatenate(s_parts, axis=1) + bias_sp[:tm, :]
    out_a = _rms(u * s_a, g_oa[...])

    xp_ref[:, 0:CONV_PAD, :] = prev
    xp_ref[:, CONV_PAD:, :] = xbc.reshape(sb, l, CONV_DIM)
    xp = xp_ref[...]
    acc = conv_b[...] + xp[:, CONV_PAD:, :] * conv_w[CONV_K - 1:CONV_K, :]
    for s in range(1, CONV_K):
        back = pltpu.roll(xp, s, axis=1)[:, CONV_PAD:, :]
        acc = acc + back * conv_w[CONV_K - 1 - s:CONV_K - s, :]
    xc = _silu(acc).reshape(tm, CONV_DIM)
    dt = _softplus(dt_raw + dt_bias[...])
    d_a = dt * a_row[...]
    return out_a, v, z, xc, dt, d_a


def _ssd_chunk(xs, bm, cm, dt, d_a, cref):
    tril, ones, expand, neg_mask = cref
    cs = _dot_exact_l(tril[...], d_a)
    cs_t = cs.T
    cs_tot = _dot_exact_l(ones[...], d_a)
    vals = jnp.concatenate([dt, jnp.exp(cs_tot - cs), jnp.exp(cs)], axis=0)
    vals_e = _dot_exact_r(vals, expand[...])
    n = xs.shape[0]
    dt_e, dte_e, e_e = vals_e[:n], vals_e[n:2 * n], vals_e[2 * n:]
    xdt = xs * dt_e
    xdtd = xdt * dte_e
    neg = neg_mask[...]
    row_lt_half = lax.broadcasted_iota(I32, (2 * n, LANES), 0) < n
    lane_lt_half = lax.broadcasted_iota(I32, (2 * n, LANES), 1) < SSD_HEAD_DIM
    y_parts = []
    for g in range(SSD_GROUPS):
        cb = _dot_nt(cm[:, g * SSD_STATE:(g + 1) * SSD_STATE].astype(BF16),
                     bm[:, g * SSD_STATE:(g + 1) * SSD_STATE].astype(BF16))
        for hp in range(SSD_HEADS // SSD_GROUPS // 2):
            h0 = g * (SSD_HEADS // SSD_GROUPS) + 2 * hp
            ms = []
            for hh in (h0, h0 + 1):
                diff = cs[:, hh:hh + 1] - cs_t[hh:hh + 1, :]
                ms.append((cb * jnp.exp(diff + neg)).astype(BF16))
            pair = xdt[:, h0 * SSD_HEAD_DIM:(h0 + 2) * SSD_HEAD_DIM]
            rhs = jnp.where(row_lt_half == lane_lt_half, jnp.concatenate([pair, pair], axis=0), 0.0).astype(BF16)
            y_parts.append(_dot(jnp.concatenate(ms, axis=1), rhs))
    y_diag = jnp.concatenate(y_parts, axis=1)
    return y_diag, e_e, xdtd, cs_tot


def _mixer_back(y, xs, z, out_a, dskip_e, g_ob):
    y = y + xs * dskip_e
    gated = y * _silu(z)
    parts = [_rms(gated[:, g * GROUP_W:(g + 1) * GROUP_W], g_ob[:, g * GROUP_W:(g + 1) * GROUP_W])
             for g in range(SSD_GROUPS)]
    return jnp.concatenate([out_a] + parts, axis=1).astype(BF16)


N_FRONT = 12
N_SSD = 4


def _prompt_mixer_kernel(tiles_per_seq, x_ref, sh_ref, sc_ref, *rest):
    front = rest[:N_FRONT]
    cref = rest[N_FRONT:N_FRONT + N_SSD]
    dskip_e, g_ob = rest[N_FRONT + N_SSD:N_FRONT + N_SSD + 2]
    mixed_ref, conv_out_ref, ssm_out_ref = rest[N_FRONT + N_SSD + 2:N_FRONT + N_SSD + 5]
    xp_ref, carry_ref, st_ref = rest[N_FRONT + N_SSD + 5:]
    i = pl.program_id(0)
    first = (i % tiles_per_seq) == 0

    @pl.when(first)
    def _():
        carry_ref[...] = jnp.zeros_like(carry_ref)
        st_ref[...] = jnp.zeros_like(st_ref)

    l = x_ref.shape[1]
    hl = l // PROMPT_SPLITS
    parts = []
    prev = carry_ref[...]
    for hh in range(PROMPT_SPLITS):
        xp_h = xp_ref.at[hh]
        parts.append(_mixer_front(x_ref[:, hh * hl:(hh + 1) * hl, :], sh_ref[...], sc_ref[...], prev, front, xp_h))
        prev = xp_h[:, hl:hl + CONV_PAD, :]
    carry_ref[...] = prev
    out_a, _, z, xc, dt, d_a = [jnp.concatenate([p[j] for p in parts], axis=0) for j in range(6)]
    xs = xc[:, :B_WIDTH]
    y_rows = []
    for c in range(l // CHUNK):
        r = slice(c * CHUNK, (c + 1) * CHUNK)
        bm = xc[r, B_WIDTH:B_WIDTH + SSD_GROUPS * SSD_STATE]
        cm = xc[r, B_WIDTH + SSD_GROUPS * SSD_STATE:]
        y_diag, e_e, xdtd, _ = _ssd_chunk(xs[r], bm, cm, dt[r], d_a[r], cref)
        st = st_ref[...]
        y_off, upd = [], []
        for g in range(SSD_GROUPS):
            gs = slice(g * GROUP_W, (g + 1) * GROUP_W)
            ns = slice(g * SSD_STATE, (g + 1) * SSD_STATE)
            y_off.append(_dot(cm[:, ns].astype(BF16), st[:, gs].astype(BF16)))
            upd.append(_dot(bm[:, ns].T.astype(BF16), xdtd[:, gs].astype(BF16)))
        y_rows.append(y_diag + jnp.concatenate(y_off, axis=1) * e_e)
        st_ref[...] = st * e_e[CHUNK - 1:CHUNK, :] + jnp.concatenate(upd, axis=1)
    y = jnp.concatenate(y_rows, axis=0)
    mixed_ref[...] = _mixer_back(y, xs, z, out_a, dskip_e[...], g_ob[...])

    @pl.when((i % tiles_per_seq) == tiles_per_seq - 1)
    def _():
        conv_out_ref[...] = xp_ref[PROMPT_SPLITS - 1, :, hl + CONV_PAD - (CONV_K - 1):hl + CONV_PAD, :]
        ssm_out_ref[0] = st_ref[...].T


def _sample_mixer_kernel(x_ref, sh_ref, sc_ref, prev_ref, ssm0_ref, *rest):
    front = rest[:N_FRONT]
    cref = rest[N_FRONT:N_FRONT + N_SSD]
    dskip_e, g_ob, selseq = rest[N_FRONT + N_SSD:N_FRONT + N_SSD + 3]
    mixed_ref, v_ref, conv_out_ref, ssm_out_ref = rest[N_FRONT + N_SSD + 3:N_FRONT + N_SSD + 7]
    xp_ref, yoff_ref, cbf_ref, bbf_ref, t1_ref, dtab_ref = rest[N_FRONT + N_SSD + 7:]
    x3 = x_ref[...]
    sb, l, _ = x3.shape
    tm = sb * l
    out_a, v, z, xc, dt, d_a = _mixer_front(x3, sh_ref[...], sc_ref[...], prev_ref[...], front, xp_ref)
    v_ref[...] = v
    conv_out_ref[...] = xp_ref[:, l + CONV_PAD - (CONV_K - 1):l + CONV_PAD, :]
    xs = xc[:, :B_WIDTH]
    bm = xc[:, B_WIDTH:B_WIDTH + SSD_GROUPS * SSD_STATE]
    cm = xc[:, B_WIDTH + SSD_GROUPS * SSD_STATE:]
    y_diag, e_e, xdtd, _ = _ssd_chunk(xs, bm, cm, dt, d_a, cref)

    e_tot = jnp.exp(_dot_exact_l(selseq[...], d_a))
    for hh in range(SSD_HEADS):
        dtab_ref[hh] = jnp.broadcast_to(e_tot[:, hh:hh + 1], (sb, LANES))
    cbf_ref[...] = cm
    bbf_ref[...] = bm
    for g in range(SSD_GROUPS):
        t1_ref[g] = xdtd[:, g * GROUP_W:(g + 1) * GROUP_W].T.astype(BF16)
    seq_of_row = lax.broadcasted_iota(I32, (tm, SSD_STATE), 0) // l
    heads_per_group = SSD_HEADS // SSD_GROUPS

    def one_seq(j):
        r0 = pl.multiple_of(j * l, l)
        s0 = ssm0_ref[j]
        for g in range(SSD_GROUPS):
            ns = slice(g * SSD_STATE, (g + 1) * SSD_STATE)
            s0g = s0[g * heads_per_group:(g + 1) * heads_per_group].reshape(GROUP_W, SSD_STATE)
            cj = cbf_ref[pl.ds(r0, l), ns].astype(BF16)
            yoff_ref[pl.ds(r0, l), g * GROUP_W:(g + 1) * GROUP_W] = _dot_nt(cj, s0g.astype(BF16))
            bmask = jnp.where(seq_of_row == j, bbf_ref[:, ns], 0.0).astype(BF16)
            upd = _dot(t1_ref[g], bmask)
            for hq in range(heads_per_group):
                hh = g * heads_per_group + hq
                dec = dtab_ref[hh, pl.ds(j, 1), :]
                ssm_out_ref[j, hh] = s0[hh] * dec + upd[hq * SSD_HEAD_DIM:(hq + 1) * SSD_HEAD_DIM]

    def body(jj, carry):
        for u in range(SEQ_UNROLL):
            one_seq(jj * SEQ_UNROLL + u)
        return carry

    lax.fori_loop(0, sb // SEQ_UNROLL, body, 0)
    y = y_diag + yoff_ref[...] * e_e
    mixed_ref[...] = _mixer_back(y, xs, z, out_a, dskip_e[...], g_ob[...])


def _ada_spec(ref, sb, div):
    _, term, row0 = ref
    assert row0 % sb == 0
    return pl.BlockSpec((sb, 1, D_MODEL), lambda i: (row0 // sb + i // div, 0, term))


def _const_spec(a):
    nd = a.ndim
    return pl.BlockSpec(a.shape, lambda i, _nd=nd: (0,) * _nd)


def _spatial_consts(w_spatial, b_spatial, cl, tm):
    w = jnp.where(jnp.tril(jnp.ones((cl, cl), bool)), w_spatial[:, :cl, :cl], 0.0)
    eye = jnp.eye(tm // cl, dtype=F32)
    wbd = jnp.einsum("ab,hts->hatbs", eye, w).reshape(A_HEADS, tm, tm).astype(BF16)
    bias = jnp.tile(jnp.repeat(b_spatial[:, :cl].T, A_HEAD_DIM, axis=1), (tm // cl, 1))
    return wbd, bias


def _ssd_consts(cl):
    r = np.arange(CHUNK)
    same = (r[:, None] // cl) == (r[None, :] // cl)
    tril = same & (r[:, None] >= r[None, :])
    expand = np.zeros((LANES, B_WIDTH), np.float32)
    for hh in range(SSD_HEADS):
        expand[hh, hh * SSD_HEAD_DIM:(hh + 1) * SSD_HEAD_DIM] = 1.0
    return (jnp.asarray(tril, BF16), jnp.asarray(same, BF16), jnp.asarray(expand, BF16),
            jnp.asarray(np.where(tril, 0.0, DECAY_MASKED), F32))


def _front_weights(p):
    w_in = p["w_in"]
    c0, c1 = 3 * A_WIDTH, 3 * A_WIDTH + CONV_DIM
    w_dt = w_in[:, c1:]
    pad8 = lambda v: jnp.pad(v, (0, LANES - SSD_HEADS))
    a = -jnp.exp(p["a_log"])
    return dict(
        g_mix=p["g_mix"][None, :],
        w_uvz=w_in[:, :c0].astype(BF16),
        w_xbc=w_in[:, c0:c1].astype(BF16),
        w_dt=jnp.pad(w_dt, ((0, 0), (0, LANES - SSD_HEADS))).astype(BF16),
        g_v=p["g_v_a"][None, :], g_oa=p["g_out_a"][None, :],
        conv_w=p["conv_w"], conv_b=p["conv_b"][None, :],
        dt_bias=pad8(p["dt_bias"])[None, :],
        a_row=pad8(a)[None, :],
        dskip_e=jnp.repeat(p["d_skip"], SSD_HEAD_DIM)[None, :],
        g_ob=p["g_out_b"][None, :],
    )


def _front_list(fw, wbd, bias_sp):
    return [fw["g_mix"], fw["w_uvz"], fw["w_xbc"], fw["w_dt"], wbd, bias_sp, fw["g_v"], fw["g_oa"],
            fw["conv_w"], fw["conv_b"], fw["dt_bias"], fw["a_row"]]


def _prompt_mixer(x, sh, sc, fw, p):
    nseq, lseq, _ = x.shape
    tile = PROMPT_TILE
    tps = lseq // tile
    nt = nseq * tps
    x4 = x.reshape(nt, tile, D_MODEL)
    wbd, bias_sp = _spatial_consts(p["w_spatial"], p["b_spatial"], CHUNK, tile // PROMPT_SPLITS)
    consts = _front_list(fw, wbd, bias_sp) + list(_ssd_consts(CHUNK)) + [fw["dskip_e"], fw["g_ob"]]
    mixed, conv_new, ssm_new = pl.pallas_call(
        functools.partial(_prompt_mixer_kernel, tps),
        out_shape=(jax.ShapeDtypeStruct((nt * tile, D_MODEL), BF16),
                   jax.ShapeDtypeStruct((nseq, CONV_K - 1, CONV_DIM), F32),
                   jax.ShapeDtypeStruct((nseq, B_WIDTH, SSD_STATE), F32)),
        grid=(nt,),
        in_specs=[pl.BlockSpec((1, tile, D_MODEL), lambda i: (i, 0, 0)), _ada_spec(sh, 1, tps), _ada_spec(sc, 1, tps)]
                 + [_const_spec(a) for a in consts],
        out_specs=(pl.BlockSpec((tile, D_MODEL), lambda i: (i, 0)),
                   pl.BlockSpec((1, CONV_K - 1, CONV_DIM), lambda i: (i // tps, 0, 0)),
                   pl.BlockSpec((1, B_WIDTH, SSD_STATE), lambda i: (i // tps, 0, 0))),
        scratch_shapes=[pltpu.VMEM((PROMPT_SPLITS, 1, tile // PROMPT_SPLITS + CONV_PAD, CONV_DIM), F32),
                        pltpu.VMEM((1, CONV_PAD, CONV_DIM), F32),
                        pltpu.VMEM((SSD_STATE, B_WIDTH), F32)],
        compiler_params=pltpu.CompilerParams(dimension_semantics=("arbitrary",), vmem_limit_bytes=VMEM_LIMIT),
        name="prompt_mixer",
    )(x4, sh[0], sc[0], *consts)
    return mixed, conv_new, ssm_new.reshape(nseq, SSD_HEADS, SSD_HEAD_DIM, SSD_STATE)


def _sample_mixer(x, sh, sc, state_ssm, state_conv, fw, p):
    nseq, l, _ = x.shape
    sb = SAMPLE_SEQ_TILE
    tm = sb * l
    assert tm == CHUNK
    wbd, bias_sp = _spatial_consts(p["w_spatial"], p["b_spatial"], l, tm)
    selseq = jnp.asarray((np.arange(tm)[None, :] // l) == np.arange(sb)[:, None], BF16)
    consts = _front_list(fw, wbd, bias_sp) + list(_ssd_consts(l)) + [fw["dskip_e"], fw["g_ob"], selseq]
    prev = jnp.pad(state_conv, ((0, 0), (CONV_PAD - (CONV_K - 1), 0), (0, 0)))
    ssm_spec = pl.BlockSpec((sb, SSD_HEADS, SSD_HEAD_DIM, SSD_STATE), lambda i: (i, 0, 0, 0))
    return pl.pallas_call(
        _sample_mixer_kernel,
        out_shape=(jax.ShapeDtypeStruct((nseq * l, D_MODEL), BF16),
                   jax.ShapeDtypeStruct((nseq * l, A_WIDTH), F32),
                   jax.ShapeDtypeStruct((nseq, CONV_K - 1, CONV_DIM), F32),
                   jax.ShapeDtypeStruct(state_ssm.shape, F32)),
        grid=(nseq // sb,),
        in_specs=[pl.BlockSpec((sb, l, D_MODEL), lambda i: (i, 0, 0)), _ada_spec(sh, sb, 1), _ada_spec(sc, sb, 1),
                  pl.BlockSpec((sb, CONV_PAD, CONV_DIM), lambda i: (i, 0, 0)), ssm_spec]
                 + [_const_spec(a) for a in consts],
        out_specs=(pl.BlockSpec((tm, D_MODEL), lambda i: (i, 0)),
                   pl.BlockSpec((tm, A_WIDTH), lambda i: (i, 0)),
                   pl.BlockSpec((sb, CONV_K - 1, CONV_DIM), lambda i: (i, 0, 0)),
                   ssm_spec),
        scratch_shapes=[pltpu.VMEM((sb, l + CONV_PAD, CONV_DIM), F32),
                        pltpu.VMEM((tm, B_WIDTH), F32),
                        pltpu.VMEM((tm, SSD_GROUPS * SSD_STATE), F32),
                        pltpu.VMEM((tm, SSD_GROUPS * SSD_STATE), F32),
                        pltpu.VMEM((SSD_GROUPS, GROUP_W, tm), BF16),
                        pltpu.VMEM((SSD_HEADS, sb, LANES), F32)],
        compiler_params=pltpu.CompilerParams(dimension_semantics=("arbitrary",), vmem_limit_bytes=VMEM_LIMIT),
        name="sample_mixer",
    )(x, sh[0], sc[0], prev, state_ssm, *consts)


def _post_kernel(mixed_ref, x_ref, gt_ref, sc_ref, sh_ref, w_out_ref, g_ffn_ref, wr_both_ref, br_ref,
                 x1_ref, h2p_ref, ids_ref):
    x3 = x_ref[...]
    sb, l, _ = x3.shape
    tm = sb * l
    mix = _dot(mixed_ref[...], w_out_ref[...]).reshape(sb, l, D_MODEL)
    x1 = x3 + gt_ref[...] * mix
    x1_ref[...] = x1
    xn = x1 * lax.rsqrt(jnp.mean(x1 * x1, axis=-1, keepdims=True) + EPS) * g_ffn_ref[...]
    h2 = (xn * (1.0 + sc_ref[...]) + sh_ref[...]).reshape(tm, D_MODEL)
    h_hi, h_lo = _split(h2)
    h2p_ref[:, :D_MODEL] = h_hi
    both = _dot_nt(wr_both_ref[...], h_hi)
    logits = (both[:N_EXPERTS] + both[N_EXPERTS:] + _dot_nt(wr_both_ref[:N_EXPERTS, :], h_lo)
              + br_ref[...])
    e_iota = lax.broadcasted_iota(I32, logits.shape, 0)
    vals, idxs = [], []
    for _ in range(TOP_K):
        m = jnp.max(logits, axis=0, keepdims=True)
        idx = jnp.min(jnp.where(logits == m, e_iota, N_EXPERTS), axis=0, keepdims=True)
        vals.append(m)
        idxs.append(idx)
        logits = jnp.where(e_iota == idx, -jnp.inf, logits)
    ex = [jnp.exp(v - vals[0]) for v in vals]
    tot = ex[0] + ex[1] + ex[2] + ex[3]
    ids = jnp.concatenate(idxs, axis=0)
    ids_ref[...] = ids
    wts = jnp.concatenate([e / tot for e in ex], axis=0)
    w_hi = wts.astype(BF16).astype(F32)
    info = jnp.concatenate([ids.astype(F32), w_hi, wts - w_hi, jnp.zeros((TOP_K, tm), F32)], axis=0).astype(BF16)
    r = lax.broadcasted_iota(I32, (INFO_ROWS, LANES), 0)
    c = lax.broadcasted_iota(I32, (INFO_ROWS, LANES), 1)
    place = jnp.where(r == c, 1.0, 0.0).astype(BF16)
    h2p_ref[:, D_MODEL:] = lax.dot_general(info, place, (((0,), (0,)), ((), ())),
                                           preferred_element_type=F32).astype(BF16)


def _post(mixed, x, gt, sc, sh, w_out_b, g_ffn, wr_both, br, sb, l, seq_div):
    n3, _, _ = x.shape
    nblk = n3 // sb
    tm = sb * l
    t = n3 * l
    consts = [w_out_b, g_ffn, wr_both, br]
    return pl.pallas_call(
        _post_kernel,
        out_shape=(jax.ShapeDtypeStruct(x.shape, F32),
                   jax.ShapeDtypeStruct((t, ROW_W), BF16),
                   jax.ShapeDtypeStruct((TOP_K, t), I32)),
        grid=(nblk,),
        in_specs=[pl.BlockSpec((tm, D_MODEL), lambda i: (i, 0)),
                  pl.BlockSpec((sb, l, D_MODEL), lambda i: (i, 0, 0)),
                  _ada_spec(gt, sb, seq_div), _ada_spec(sc, sb, seq_div), _ada_spec(sh, sb, seq_div)]
                 + [_const_spec(a) for a in consts],
        out_specs=(pl.BlockSpec((sb, l, D_MODEL), lambda i: (i, 0, 0)),
                   pl.BlockSpec((tm, ROW_W), lambda i: (i, 0)),
                   pl.BlockSpec((TOP_K, tm), lambda i: (0, i))),
        compiler_params=pltpu.CompilerParams(dimension_semantics=("arbitrary",), vmem_limit_bytes=VMEM_LIMIT),
        name="post",
    )(mixed, x, gt[0], sc[0], sh[0], *consts)


def _strict_upper(n):
    r = lax.broadcasted_iota(I32, (n, n), 0)
    c = lax.broadcasted_iota(I32, (n, n), 1)
    return jnp.where(r < c, 1.0, 0.0).astype(BF16)


def _expert_prefix(col):
    r = lax.broadcasted_iota(I32, (N_EXPERTS, N_EXPERTS), 0)
    c = lax.broadcasted_iota(I32, (N_EXPERTS, N_EXPERTS), 1)
    as_row = jnp.sum(jnp.where(r == c, col, 0.0), axis=0, keepdims=True)
    return jnp.sum(jnp.where(c < r, as_row, 0.0), axis=1, keepdims=True)


def _plan_kernel(dump_group, ids_ref, lr_ref, gdst_ref, tile_e_ref):
    tm = TOK_TILE
    nb = ids_ref.shape[1] // tm
    blk_lane = lax.broadcasted_iota(I32, (N_EXPERTS, LANES), 1)

    def block_masks(b):
        ids = ids_ref[:, pl.ds(pl.multiple_of(b * tm, tm), tm)]
        e_iota = lax.broadcasted_iota(I32, (N_EXPERTS, tm), 0)
        onehot = [ids[k:k + 1, :] == e_iota for k in range(TOP_K)]
        sel = (onehot[0] | onehot[1]) | (onehot[2] | onehot[3])
        return onehot, jnp.where(sel, 1.0, 0.0)

    def count(b, seg):
        _, m = block_masks(b)
        seg_b = jnp.ceil(jnp.sum(m, axis=1, keepdims=True) * (1.0 / ROW_GROUP))
        return jnp.where(blk_lane == b, seg_b, seg)

    seg = lax.fori_loop(0, nb, count, jnp.zeros((N_EXPERTS, LANES), F32))
    tot = jnp.sum(seg, axis=1, keepdims=True)
    padded = jnp.ceil(tot * (1.0 / TILE_GROUPS)) * TILE_GROUPS
    gstart = _expert_prefix(padded)
    gb = gstart + _dot(seg.astype(BF16), _strict_upper(LANES))
    r = lax.broadcasted_iota(I32, (N_EXPERTS, LANES), 0)
    as_row = lambda col: jnp.sum(jnp.where(r == blk_lane, col, 0.0), axis=0, keepdims=True)
    n_used = jnp.sum(padded, axis=0, keepdims=True)
    lane = lax.broadcasted_iota(I32, (1, LANES), 1)
    tiles = jnp.where(lane == N_EXPERTS, n_used, as_row(gstart)) * (1.0 / TILE_GROUPS)
    meta = jnp.concatenate([tiles, as_row(gstart + tot), as_row(padded - tot),
                            jnp.zeros((META_ROWS - 3, LANES), F32)], axis=0)
    tile_e_ref[...] = meta.astype(I32)
    upper = _strict_upper(tm)

    def place(b, carry):
        onehot, m = block_masks(b)
        seg_b = jnp.sum(jnp.where(blk_lane == b, seg, 0.0), axis=1, keepdims=True)
        gb_b = jnp.sum(jnp.where(blk_lane == b, gb, 0.0), axis=1, keepdims=True)
        loc_b = _expert_prefix(seg_b)
        before = _dot(m.astype(BF16), upper) + loc_b * ROW_GROUP
        lr_ref[:, pl.ds(pl.multiple_of(b * tm, tm), tm)] = jnp.concatenate(
            [jnp.sum(jnp.where(onehot[k], before, 0.0), axis=0, keepdims=True) for k in range(TOP_K)],
            axis=0).astype(I32)
        g = lax.broadcasted_iota(I32, (N_EXPERTS, GDST_LANES), 1).astype(F32)
        inside = (loc_b <= g) & (g < loc_b + seg_b)
        dst = jnp.sum(jnp.where(inside, gb_b + g - loc_b, 0.0), axis=0, keepdims=True)
        used = jnp.sum(jnp.where(inside, 1.0, 0.0), axis=0, keepdims=True) > 0.5
        dump = dump_group + lax.convert_element_type(b % 2, F32) * LOCAL_GROUPS + g[0:1, :]
        gdst_ref[b] = jnp.where(used, dst, dump).astype(I32)
        return carry

    lax.fori_loop(0, nb, place, 0)


def _plan(ids, dump_group):
    t = ids.shape[1]
    nb = t // TOK_TILE
    assert nb <= LANES
    return pl.pallas_call(
        functools.partial(_plan_kernel, float(dump_group)),
        out_shape=(jax.ShapeDtypeStruct((TOP_K, t), I32),
                   jax.ShapeDtypeStruct((nb, 1, GDST_LANES), I32),
                   jax.ShapeDtypeStruct((META_ROWS, LANES), I32)),
        name="plan",
    )(ids)


def _sort_matrix(lr, c):
    r_iota = lax.broadcasted_iota(I32, (MASK_ROWS, lr.shape[1]), 0) + c * MASK_ROWS
    p = jnp.where(r_iota == lr[TOP_K - 1:TOP_K, :], 1.0, 0.0)
    for k in range(TOP_K - 1):
        p = jnp.where(r_iota == lr[k:k + 1, :], 1.0, p)
    return p.astype(BF16)


def _group_copies(loc_ref, slot, far_ref, gdst_ref, sem, to_far):
    copies = []
    for g in range(LOCAL_GROUPS):
        dst = pl.multiple_of(gdst_ref[0, 0, g] * ROW_GROUP, ROW_GROUP)
        near = loc_ref.at[slot, pl.ds(g * ROW_GROUP, ROW_GROUP)]
        far = far_ref.at[pl.ds(dst, ROW_GROUP)]
        copies.append(pltpu.make_async_copy(near, far, sem.at[slot]) if to_far
                      else pltpu.make_async_copy(far, near, sem.at[slot]))
    return copies


def _wait_groups(loc_ref, slot, sem):
    pltpu.make_async_copy(loc_ref.at[slot], loc_ref.at[slot], sem.at[slot]).wait()


def _dispatch_kernel(n_first, n_tiles_all, gdst_ref, meta_ref, lr_ref, ha_ref, hb_ref, xs_ref,
                     h_ref, loc_ref, zero_ref, sem, zsem):
    i = pl.program_id(0)
    last = pl.num_programs(0) - 1
    slot = i % 2

    @pl.when(i < n_first)
    def _():
        h_ref[...] = ha_ref[...]

    @pl.when(i >= n_first)
    def _():
        h_ref[...] = hb_ref[...]

    lr = lr_ref[...]
    copies = _group_copies(loc_ref, slot, xs_ref, gdst_ref, sem, True)
    n_chunks = LOCAL_ROWS // MASK_ROWS
    per_chunk = LOCAL_GROUPS // n_chunks
    for c in range(n_chunks):
        loc_ref[slot, c * MASK_ROWS:(c + 1) * MASK_ROWS, :] = _dot(_sort_matrix(lr, c), h_ref[...]).astype(BF16)
        for cp in copies[c * per_chunk:(c + 1) * per_chunk]:
            cp.start()

    @pl.when(i > 0)
    def _():
        _wait_groups(loc_ref, 1 - slot, sem)

    @pl.when(i == last)
    def _():
        _wait_groups(loc_ref, slot, sem)
        zero_ref[...] = jnp.zeros_like(zero_ref)

        def pad_copy(e, j):
            row = pl.multiple_of((meta_ref[1, e] + j) * ROW_GROUP, ROW_GROUP)
            return pltpu.make_async_copy(zero_ref.at[pl.ds(0, ROW_GROUP)], xs_ref.at[pl.ds(row, ROW_GROUP)], zsem)

        def tile_copy(t):
            row = pl.multiple_of(t * ROW_TILE, ROW_TILE)
            return pltpu.make_async_copy(zero_ref, xs_ref.at[pl.ds(row, ROW_TILE)], zsem)

        def pads(fn):
            def body(e, carry):
                for j in range(TILE_GROUPS - 1):
                    @pl.when(j < meta_ref[2, e])
                    def _():
                        fn(pad_copy(e, j))
                return carry
            lax.fori_loop(0, N_EXPERTS, body, 0)

        def tiles(fn):
            def body(t, carry):
                fn(tile_copy(t))
                return carry
            lax.fori_loop(meta_ref[0, N_EXPERTS], n_tiles_all, body, 0)

        pads(lambda c: c.start())
        tiles(lambda c: c.start())
        pads(lambda c: c.wait())
        tiles(lambda c: c.wait())


def _dispatch(gdst, meta, lr, h_a, h_b, n_tiles_all):
    tm = TOK_TILE
    na, nb2 = h_a.shape[0] // tm, h_b.shape[0] // tm
    return pl.pallas_call(
        functools.partial(_dispatch_kernel, na, n_tiles_all),
        out_shape=jax.ShapeDtypeStruct((n_tiles_all * ROW_TILE, ROW_W), BF16),
        grid=(na + nb2,),
        in_specs=[pl.BlockSpec((1, 1, GDST_LANES), lambda i: (i, 0, 0), memory_space=pltpu.SMEM),
                  pl.BlockSpec((META_ROWS, LANES), lambda i: (0, 0), memory_space=pltpu.SMEM),
                  pl.BlockSpec((TOP_K, tm), lambda i: (0, i)),
                  pl.BlockSpec((tm, ROW_W), lambda i: (jnp.minimum(i, na - 1), 0)),
                  pl.BlockSpec((tm, ROW_W), lambda i: (jnp.maximum(i - na, 0), 0))],
        out_specs=pl.BlockSpec(memory_space=pl.ANY),
        scratch_shapes=[pltpu.VMEM((tm, ROW_W), BF16), pltpu.VMEM((2, LOCAL_ROWS, ROW_W), BF16),
                        pltpu.VMEM((ROW_TILE, ROW_W), BF16),
                        pltpu.SemaphoreType.DMA((2,)), pltpu.SemaphoreType.DMA],
        compiler_params=pltpu.CompilerParams(dimension_semantics=("arbitrary",), vmem_limit_bytes=VMEM_LIMIT),
        name="dispatch",
    )(gdst, meta, lr, h_a, h_b)


def _expert_kernel(n_tiles_all, ts_ref, xs_ref, wg_ref, wu_ref, wd_ref, bg_ref, bu_ref, bd_ref, y_ref,
                   wstage, wgb, wub, wdb, xbuf, ybuf, sem_w, sem_in, sem_out):
    e = pl.program_id(0)
    t0 = ts_ref[e]
    nt = ts_ref[e + 1] - t0
    wslot = e % 2

    def w_copies(ex, slot):
        return [pltpu.make_async_copy(w_ref.at[ex], wstage.at[slot, j], sem_w.at[slot])
                for j, w_ref in enumerate((wg_ref, wu_ref, wd_ref))]

    def in_copy(t, slot):
        rows = pl.ds(pl.multiple_of((t0 + t) * ROW_TILE, ROW_TILE), ROW_TILE)
        return pltpu.make_async_copy(xs_ref.at[rows], xbuf.at[slot], sem_in.at[slot])

    def out_copy(tile, slot):
        rows = pl.ds(pl.multiple_of(tile * ROW_TILE, ROW_TILE), ROW_TILE)
        return pltpu.make_async_copy(ybuf.at[slot], y_ref.at[rows], sem_out.at[slot])

    @pl.when((e == 0) & (nt > 0))
    def _():
        for c in w_copies(0, 0):
            c.start()

    for j in range(TILE_SLOTS):
        @pl.when(nt > j)
        def _():
            in_copy(j, j).start()

    @pl.when(e + 1 < N_EXPERTS)
    def _():
        @pl.when(ts_ref[e + 2] > ts_ref[e + 1])
        def _():
            for c in w_copies(e + 1, 1 - wslot):
                c.start(priority=1)

    @pl.when(nt > 0)
    def _():
        for c in w_copies(e, wslot):
            c.wait()
        wgb[...] = wstage[wslot, 0].astype(BF16)
        wub[...] = wstage[wslot, 1].astype(BF16)
        wdb[...] = wstage[wslot, 2].astype(BF16)
        e_f = e.astype(F32)

        def acquire(t):
            slot = t % TILE_SLOTS
            in_copy(t, slot).wait()

            @pl.when(t >= TILE_SLOTS)
            def _():
                out_copy(t0 + t - TILE_SLOTS, slot).wait()

        def compute(t):
            slot = t % TILE_SLOTS
            xw = xbuf[slot]
            x = xw[:, :D_MODEL]
            info = xw[:, D_MODEL:].astype(F32)
            w_row = jnp.zeros((ROW_TILE, 1), F32)
            for k in range(TOP_K):
                wk = info[:, TOP_K + k:TOP_K + k + 1] + info[:, 2 * TOP_K + k:2 * TOP_K + k + 1]
                w_row = w_row + jnp.where(info[:, k:k + 1] == e_f, wk, 0.0)
            g = jnp.minimum(_dot(x, wgb[...]) + bg_ref[0], SWIGLU_LIMIT)
            u = jnp.clip(_dot(x, wub[...]) + bu_ref[0], -SWIGLU_LIMIT, SWIGLU_LIMIT)
            act = g * jax.nn.sigmoid(SWIGLU_ALPHA * g) * (u + 1.0)
            ybuf[slot] = ((_dot(act.astype(BF16), wdb[...]) + bd_ref[0]) * w_row).astype(BF16)

        def release(t):
            slot = t % TILE_SLOTS
            out_copy(t0 + t, slot).start()

            @pl.when(t + TILE_SLOTS < nt)
            def _():
                in_copy(t + TILE_SLOTS, slot).start()

        def pair(p, carry):
            ta, tb = 2 * p, 2 * p + 1
            acquire(ta)
            acquire(tb)
            compute(ta)
            compute(tb)
            release(ta)
            release(tb)
            return carry

        lax.fori_loop(0, nt // 2, pair, 0)

        @pl.when(nt % 2 == 1)
        def _():
            acquire(nt - 1)
            compute(nt - 1)
            release(nt - 1)

        for j in range(1, TILE_SLOTS + 1):
            @pl.when(nt >= j)
            def _():
                out_copy(t0 + nt - j, (nt - j) % TILE_SLOTS).wait()

    @pl.when(e == pl.num_programs(0) - 1)
    def _():
        n_used = ts_ref[N_EXPERTS]
        ybuf[0] = jnp.zeros((ROW_TILE, D_MODEL), BF16)

        def zstart(tile, carry):
            out_copy(tile, 0).start()
            return carry

        def zwait(tile, carry):
            out_copy(tile, 0).wait()
            return carry

        lax.fori_loop(n_used, n_tiles_all, zstart, 0)
        lax.fori_loop(n_used, n_tiles_all, zwait, 0)


def _experts(tile_start, xs, w_gate, b_gate, w_up, b_up, w_down, b_down):
    n_rows = xs.shape[0]
    b_spec = pl.BlockSpec((1, 1, D_MODEL), lambda e, ts: (e, 0, 0))
    any_spec = pl.BlockSpec(memory_space=pl.ANY)
    return pl.pallas_call(
        functools.partial(_expert_kernel, n_rows // ROW_TILE),
        out_shape=jax.ShapeDtypeStruct((n_rows, D_MODEL), BF16),
        grid_spec=pltpu.PrefetchScalarGridSpec(
            num_scalar_prefetch=1,
            grid=(N_EXPERTS,),
            in_specs=[any_spec, any_spec, any_spec, any_spec, b_spec, b_spec, b_spec],
            out_specs=any_spec,
            scratch_shapes=[pltpu.VMEM((2, 3, D_MODEL, D_MODEL), F32)]
                           + [pltpu.VMEM((D_MODEL, D_MODEL), BF16)] * 3
                           + [pltpu.VMEM((TILE_SLOTS, ROW_TILE, ROW_W), BF16),
                              pltpu.VMEM((TILE_SLOTS, ROW_TILE, D_MODEL), BF16),
                              pltpu.SemaphoreType.DMA((2,)), pltpu.SemaphoreType.DMA((TILE_SLOTS,)),
                              pltpu.SemaphoreType.DMA((TILE_SLOTS,))],
        ),
        compiler_params=pltpu.CompilerParams(dimension_semantics=("arbitrary",), vmem_limit_bytes=VMEM_LIMIT),
        name="experts",
    )(tile_start, xs, w_gate, w_up, w_down, b_gate[:, None, :], b_up[:, None, :], b_down[:, None, :])


def _combine_kernel(gdst_ref, gdst_next_ref, lr_ref, y_ref, x1_ref, gt_ref, gf_ref, out_ref, loc_ref, sem):
    i = pl.program_id(0)
    slot = i % 2
    x1 = x1_ref[...]
    sb, l, _ = x1.shape

    @pl.when(i == 0)
    def _():
        for c in _group_copies(loc_ref, slot, y_ref, gdst_ref, sem, False):
            c.start()

    _wait_groups(loc_ref, slot, sem)

    prefetch = _group_copies(loc_ref, 1 - slot, y_ref, gdst_next_ref, sem, False)
    n_chunks = LOCAL_ROWS // MASK_ROWS
    per_chunk = LOCAL_GROUPS // n_chunks
    lr = lr_ref[...]
    moe = jnp.zeros((sb * l, D_MODEL), F32)
    for c in range(n_chunks):
        rows = loc_ref[slot, c * MASK_ROWS:(c + 1) * MASK_ROWS, :]
        moe = moe + lax.dot_general(_sort_matrix(lr, c), rows, (((0,), (0,)), ((), ())),
                                    preferred_element_type=F32)
        for cp in prefetch[c * per_chunk:(c + 1) * per_chunk]:
            cp.start()
    x2 = x1 + gt_ref[...] * moe.reshape(sb, l, D_MODEL)
    out_ref[...] = x2 * lax.rsqrt(jnp.mean(x2 * x2, axis=-1, keepdims=True) + EPS) * gf_ref[...]

    @pl.when(i == pl.num_programs(0) - 1)
    def _():
        _wait_groups(loc_ref, 1 - slot, sem)


def _combine(gdst, lr, y, x1, gt, g_final, sb, l, seq_div, blk_off):
    n3 = x1.shape[0]
    nblk = n3 // sb
    tm = sb * l
    assert tm == TOK_TILE
    return pl.pallas_call(
        _combine_kernel,
        out_shape=jax.ShapeDtypeStruct(x1.shape, F32),
        grid=(nblk,),
        in_specs=[pl.BlockSpec((1, 1, GDST_LANES), lambda i: (i + blk_off, 0, 0), memory_space=pltpu.SMEM),
                  pl.BlockSpec((1, 1, GDST_LANES), lambda i: (jnp.minimum(i + 1, nblk - 1) + blk_off, 0, 0),
                               memory_space=pltpu.SMEM),
                  pl.BlockSpec((TOP_K, tm), lambda i: (0, i + blk_off)),
                  pl.BlockSpec(memory_space=pl.ANY),
                  pl.BlockSpec((sb, l, D_MODEL), lambda i: (i, 0, 0)),
                  _ada_spec(gt, sb, seq_div),
                  pl.BlockSpec((1, D_MODEL), lambda i: (0, 0))],
        out_specs=pl.BlockSpec((sb, l, D_MODEL), lambda i: (i, 0, 0)),
        scratch_shapes=[pltpu.VMEM((2, LOCAL_ROWS, D_MODEL), BF16), pltpu.SemaphoreType.DMA((2,))],
        compiler_params=pltpu.CompilerParams(dimension_semantics=("arbitrary",), vmem_limit_bytes=VMEM_LIMIT),
        name="combine",
    )(gdst, gdst, lr, y, x1, gt[0], g_final)


def kernel(x_prompt, x_sample, c_prompt, c_sample, state_ssm, state_conv, w_ada, b_ada, g_mix, w_in, g_v_a, w_spatial, b_spatial, g_out_a, conv_w, conv_b, dt_bias, a_log, d_skip, g_out_b, w_out, g_ffn, w_router, b_router, w_gate, b_gate, w_up, b_up, w_down, b_down, g_final):
    assert w_ada.shape[0] == 1, "single-layer step"
    p = dict(w_in=w_in[0], g_mix=g_mix[0], g_v_a=g_v_a[0], w_spatial=w_spatial[0], b_spatial=b_spatial[0],
             g_out_a=g_out_a[0], conv_w=conv_w[0], conv_b=conv_b[0], dt_bias=dt_bias[0], a_log=a_log[0],
             d_skip=d_skip[0], g_out_b=g_out_b[0])
    bp, lp, _ = x_prompt.shape
    bs, ls, _ = x_sample.shape
    tp, ts = bp * lp, bs * ls

    ada = _ada(jnp.concatenate([c_sample, c_prompt], axis=0), w_ada[0], b_ada[0][None, :])
    ada = ada.reshape(bs + bp, 1, 6 * D_MODEL)
    ada_s = [(ada, j, 0) for j in range(6)]
    ada_p = [(ada, j, bs) for j in range(6)]

    fw = _front_weights(p)
    mixed_p, conv_p, ssm_p = _prompt_mixer(x_prompt, ada_p[0], ada_p[1], fw, p)
    mixed_s, v_s, conv_s, ssm_s = _sample_mixer(x_sample, ada_s[0], ada_s[1], state_ssm[0], state_conv[0], fw, p)

    w_out_b = w_out[0].astype(BF16)
    g_ffn2 = g_ffn[0][None, :]
    wr_t = w_router[0].T
    wr_hi = wr_t.astype(BF16)
    wr_both = jnp.concatenate([wr_hi, (wr_t - wr_hi.astype(F32)).astype(BF16)], axis=0)
    br = b_router[0][:, None]
    tps = lp // TOK_TILE
    sbs = TOK_TILE // ls
    xp3 = x_prompt.reshape(bp * tps, TOK_TILE, D_MODEL)
    x1_p, h2p_p, ids_p = _post(mixed_p, xp3, ada_p[2], ada_p[4], ada_p[3], w_out_b, g_ffn2, wr_both, br,
                               1, TOK_TILE, tps)
    x1_s, h2p_s, ids_s = _post(mixed_s, x_sample, ada_s[2], ada_s[4], ada_s[3], w_out_b, g_ffn2, wr_both, br,
                               sbs, ls, 1)

    n_blocks = (tp + ts) // TOK_TILE
    max_groups = (tp + ts) * TOP_K // ROW_GROUP + n_blocks * N_EXPERTS + N_EXPERTS * (TILE_GROUPS - 1)
    n_tiles = -(-max_groups // TILE_GROUPS)
    n_tiles_all = n_tiles + 2 * LOCAL_GROUPS // TILE_GROUPS
    lr, gdst, meta = _plan(jnp.concatenate([ids_p, ids_s], axis=1), n_tiles * TILE_GROUPS)

    xs = _dispatch(gdst, meta, lr, h2p_p, h2p_s, n_tiles_all)
    y = _experts(meta[0], xs, w_gate[0], b_gate[0], w_up[0], b_up[0], w_down[0], b_down[0])

    gf = g_final[None, :]
    y_p = _combine(gdst, lr, y, x1_p, ada_p[5], gf, 1, TOK_TILE, tps, 0).reshape(bp, lp, D_MODEL)
    y_s = _combine(gdst, lr, y, x1_s, ada_s[5], gf, sbs, ls, 1, tp // TOK_TILE)

    return (y_p, y_s, ssm_p[None], conv_p[None], ssm_s[None], conv_s[None], v_s.reshape(1, bs, ls, A_WIDTH))
```

```python
import functools
import math

import numpy as np
import jax
import jax.numpy as jnp
from jax import lax
from jax.experimental import pallas as pl
from jax.experimental.pallas import tpu as pltpu

F32 = jnp.float32
BF16 = jnp.bfloat16
I32 = jnp.int32

D_MODEL = 1024
A_WIDTH = 512
A_HEADS = 4
A_HEAD_DIM = 128
CHUNK = 128
B_WIDTH = 512
SSD_HEAD_DIM = 64
SSD_HEADS = 8
SSD_GROUPS = 2
SSD_STATE = 128
GROUP_W = B_WIDTH // SSD_GROUPS
CONV_K = 4
CONV_DIM = 1024
CONV_PAD = 8
N_EXPERTS = 32
TOP_K = 4
SWIGLU_LIMIT = 7.0
SWIGLU_ALPHA = 1.702
EPS = 1e-6
DECAY_MASKED = -1e30
LANES = 128

TOK_TILE = 512
SAMPLE_SEQ_TILE = 16
SEQ_UNROLL = 4
ROW_TILE = 256
ROW_GROUP = 16
TILE_GROUPS = ROW_TILE // ROW_GROUP
LOCAL_GROUPS = TOK_TILE * TOP_K // ROW_GROUP + N_EXPERTS
LOCAL_ROWS = LOCAL_GROUPS * ROW_GROUP
GDST_LANES = 256
ROW_W = D_MODEL + LANES
INFO_ROWS = 16
MASK_ROWS = 256
META_ROWS = 8
PROMPT_TILE = 512
PROMPT_SPLITS = 2
TILE_SLOTS = 4
VMEM_LIMIT = 56 * 1024 * 1024


def _dot(a, b):
    return jnp.dot(a, b, preferred_element_type=F32)


def _dot_nt(a, b):
    return lax.dot_general(a, b, (((1,), (1,)), ((), ())), preferred_element_type=F32)


def _split(x):
    hi = x.astype(BF16)
    lo = (x - hi.astype(F32)).astype(BF16)
    return hi, lo


def _dot_exact_l(t, x):
    hi, lo = _split(x)
    return _dot(t, hi) + _dot(t, lo)


def _dot_exact_r(x, t):
    hi, lo = _split(x)
    return _dot(hi, t) + _dot(lo, t)


def _silu(x):
    return x * jax.nn.sigmoid(x)


def _gelu(x):
    return 0.5 * x * (1.0 + lax.erf(x * (1.0 / math.sqrt(2.0))))


def _softplus(x):
    return jnp.maximum(x, 0.0) + jnp.log1p(jnp.exp(-jnp.abs(x)))


def _rms(x, g):
    return x * lax.rsqrt(jnp.mean(x * x, axis=-1, keepdims=True) + EPS) * g


def _ada_kernel(c_ref, w_ref, b_ref, o_ref):
    s_hi, s_lo = _split(_silu(c_ref[...]))
    w_hi, w_lo = _split(w_ref[...])
    o_ref[...] = _dot(s_hi, w_hi) + _dot(s_lo, w_hi) + _dot(s_hi, w_lo) + b_ref[...]


def _ada(c_all, w_ada, b_ada):
    m = c_all.shape[0]
    n = w_ada.shape[1]
    bn = 1024
    return pl.pallas_call(
        _ada_kernel,
        out_shape=jax.ShapeDtypeStruct((m, n), F32),
        grid=(n // bn,),
        in_specs=[pl.BlockSpec((m, D_MODEL), lambda j: (0, 0)),
                  pl.BlockSpec((D_MODEL, bn), lambda j: (0, j)),
                  pl.BlockSpec((1, bn), lambda j: (0, j))],
        out_specs=pl.BlockSpec((m, bn), lambda j: (0, j)),
        compiler_params=pltpu.CompilerParams(dimension_semantics=("arbitrary",), vmem_limit_bytes=VMEM_LIMIT),
        name="ada",
    )(c_all, w_ada, b_ada)


def _mixer_front(x3, sh, sc, prev, refs, xp_ref):
    (g_mix, w_uvz, w_xbc, w_dt, wbd, bias_sp, g_v, g_oa, conv_w, conv_b, dt_bias, a_row) = refs
    sb, l, _ = x3.shape
    tm = sb * l
    xn = x3 * lax.rsqrt(jnp.mean(x3 * x3, axis=-1, keepdims=True) + EPS) * g_mix[...]
    h = (xn * (1.0 + sc) + sh).reshape(tm, D_MODEL)
    hb = h.astype(BF16)
    uvz = _dot(hb, w_uvz[...])
    xbc = _dot(hb, w_xbc[...])
    dt_raw = _dot(hb, w_dt[...])

    u = _gelu(uvz[:, :A_WIDTH])
    vg = _gelu(uvz[:, A_WIDTH:2 * A_WIDTH])
    z = uvz[:, 2 * A_WIDTH:]
    v_parts, s_parts = [], []
    for hd in range(A_HEADS):
        sl = slice(hd * A_HEAD_DIM, (hd + 1) * A_HEAD_DIM)
        vh = _rms(vg[:, sl], g_v[:, sl])
        v_parts.append(vh)
        vb = vh.astype(BF16)
        s_parts.append(jnp.concatenate([_dot(wbd[hd], vb[r0:r0 + CHUNK]) + bias_sp[:, sl]
                                        for r0 in range(0, tm, CHUNK)], axis=0))
    v = jnp.concatenate(v_parts, axis=1)
    s_a = jnp.concatenate(s_parts, axis=1)
    out_a = _rms(u * s_a, g_oa[...])

    xp_ref[:, 0:CONV_PAD, :] = prev
    xp_ref[:, CONV_PAD:, :] = xbc.reshape(sb, l, CONV_DIM)
    xp = xp_ref[...]
    acc = conv_b[...] + xp[:, CONV_PAD:, :] * conv_w[CONV_K - 1:CONV_K, :]
    for s in range(1, CONV_K):
        back = pltpu.roll(xp, s, axis=1)[:, CONV_PAD:, :]
        acc = acc + back * conv_w[CONV_K - 1 - s:CONV_K - s, :]
    xc = _silu(acc).reshape(tm, CONV_DIM)
    dt = _softplus(dt_raw + dt_bias[...])
    d_a = dt * a_row[...]
    return out_a, v, z, xc, dt, d_a


def _ssd_chunk(xs, bm, cm, dt, d_a, cref):
    tril, ones, expand, neg_mask = cref
    cs = _dot_exact_l(tril[...], d_a)
    cs_t = cs.T
    dt_t = dt.T
    cs_tot = _dot_exact_l(ones[...], d_a)
    vals = jnp.concatenate([dt * jnp.exp(cs_tot - cs), jnp.exp(cs)], axis=0)
    vals_e = _dot_exact_r(vals, expand[...])
    n = xs.shape[0]
    w_e, e_e = vals_e[:n], vals_e[n:]
    xdtd = xs * w_e
    neg = neg_mask[...]
    row_lt_half = lax.broadcasted_iota(I32, (2 * n, LANES), 0) < n
    lane_lt_half = lax.broadcasted_iota(I32, (2 * n, LANES), 1) < SSD_HEAD_DIM
    y_parts = []
    for g in range(SSD_GROUPS):
        cb = _dot_nt(cm[:, g * SSD_STATE:(g + 1) * SSD_STATE].astype(BF16),
                     bm[:, g * SSD_STATE:(g + 1) * SSD_STATE].astype(BF16))
        for hp in range(SSD_HEADS // SSD_GROUPS // 2):
            h0 = g * (SSD_HEADS // SSD_GROUPS) + 2 * hp
            ms = []
            for hh in (h0, h0 + 1):
                diff = cs[:, hh:hh + 1] - cs_t[hh:hh + 1, :]
                ms.append((cb * jnp.exp(diff + neg) * dt_t[hh:hh + 1, :]).astype(BF16))
            pair = xs[:, h0 * SSD_HEAD_DIM:(h0 + 2) * SSD_HEAD_DIM]
            rhs = jnp.where(row_lt_half == lane_lt_half, jnp.concatenate([pair, pair], axis=0), 0.0).astype(BF16)
            y_parts.append(_dot(jnp.concatenate(ms, axis=1), rhs))
    y_diag = jnp.concatenate(y_parts, axis=1)
    return y_diag, e_e, xdtd, cs_tot


def _mixer_back(y, xs, z, out_a, dskip_e, g_ob):
    y = y + xs * dskip_e
    gated = y * _silu(z)
    parts = [_rms(gated[:, g * GROUP_W:(g + 1) * GROUP_W], g_ob[:, g * GROUP_W:(g + 1) * GROUP_W])
             for g in range(SSD_GROUPS)]
    return jnp.concatenate([out_a] + parts, axis=1).astype(BF16)


N_FRONT = 12
N_SSD = 4


def _prompt_mixer_kernel(tiles_per_seq, x_ref, sh_ref, sc_ref, *rest):
    front = rest[:N_FRONT]
    cref = rest[N_FRONT:N_FRONT + N_SSD]
    dskip_e, g_ob = rest[N_FRONT + N_SSD:N_FRONT + N_SSD + 2]
    mixed_ref, conv_out_ref, ssm_out_ref = rest[N_FRONT + N_SSD + 2:N_FRONT + N_SSD + 5]
    xp_ref, carry_ref, st_ref = rest[N_FRONT + N_SSD + 5:]
    i = pl.program_id(0)
    first = (i % tiles_per_seq) == 0

    @pl.when(first)
    def _():
        carry_ref[...] = jnp.zeros_like(carry_ref)
        st_ref[...] = jnp.zeros_like(st_ref)

    l = x_ref.shape[1]
    hl = l // PROMPT_SPLITS
    parts = []
    prev = carry_ref[...]
    for hh in range(PROMPT_SPLITS):
        xp_h = xp_ref.at[hh]
        parts.append(_mixer_front(x_ref[:, hh * hl:(hh + 1) * hl, :], sh_ref[...], sc_ref[...], prev, front, xp_h))
        prev = xp_h[:, hl:hl + CONV_PAD, :]
    carry_ref[...] = prev
    out_a, _, z, xc, dt, d_a = [jnp.concatenate([p[j] for p in parts], axis=0) for j in range(6)]
    xs = xc[:, :B_WIDTH]
    y_rows = []
    for c in range(l // CHUNK):
        r = slice(c * CHUNK, (c + 1) * CHUNK)
        bm = xc[r, B_WIDTH:B_WIDTH + SSD_GROUPS * SSD_STATE]
        cm = xc[r, B_WIDTH + SSD_GROUPS * SSD_STATE:]
        y_diag, e_e, xdtd, _ = _ssd_chunk(xs[r], bm, cm, dt[r], d_a[r], cref)
        st = st_ref[...]
        y_off, upd = [], []
        for g in range(SSD_GROUPS):
            gs = slice(g * GROUP_W, (g + 1) * GROUP_W)
            ns = slice(g * SSD_STATE, (g + 1) * SSD_STATE)
            y_off.append(_dot(cm[:, ns].astype(BF16), st[:, gs].astype(BF16)))
            upd.append(_dot(bm[:, ns].T.astype(BF16), xdtd[:, gs].astype(BF16)))
        y_rows.append(y_diag + jnp.concatenate(y_off, axis=1) * e_e)
        st_ref[...] = st * e_e[CHUNK - 1:CHUNK, :] + jnp.concatenate(upd, axis=1)
    y = jnp.concatenate(y_rows, axis=0)
    mixed_ref[...] = _mixer_back(y, xs, z, out_a, dskip_e[...], g_ob[...])

    @pl.when((i % tiles_per_seq) == tiles_per_seq - 1)
    def _():
        conv_out_ref[...] = xp_ref[PROMPT_SPLITS - 1, :, hl + CONV_PAD - (CONV_K - 1):hl + CONV_PAD, :]
        ssm_out_ref[0] = st_ref[...].T


def _sample_mixer_kernel(x_ref, sh_ref, sc_ref, prev_ref, ssm0_ref, *rest):
    front = rest[:N_FRONT]
    cref = rest[N_FRONT:N_FRONT + N_SSD]
    dskip_e, g_ob, selseq = rest[N_FRONT + N_SSD:N_FRONT + N_SSD + 3]
    mixed_ref, v_ref, conv_out_ref, ssm_out_ref = rest[N_FRONT + N_SSD + 3:N_FRONT + N_SSD + 7]
    xp_ref, yoff_ref, cbf_ref, bbf_ref, t1_ref, dtab_ref = rest[N_FRONT + N_SSD + 7:]
    x3 = x_ref[...]
    sb, l, _ = x3.shape
    tm = sb * l
    out_a, v, z, xc, dt, d_a = _mixer_front(x3, sh_ref[...], sc_ref[...], prev_ref[...], front, xp_ref)
    v_ref[...] = v
    conv_out_ref[...] = xp_ref[:, l + CONV_PAD - (CONV_K - 1):l + CONV_PAD, :]
    xs = xc[:, :B_WIDTH]
    bm = xc[:, B_WIDTH:B_WIDTH + SSD_GROUPS * SSD_STATE]
    cm = xc[:, B_WIDTH + SSD_GROUPS * SSD_STATE:]
    y_diag, e_e, xdtd, _ = _ssd_chunk(xs, bm, cm, dt, d_a, cref)

    e_tot = jnp.exp(_dot_exact_l(selseq[...], d_a))
    for hh in range(SSD_HEADS):
        dtab_ref[hh] = jnp.broadcast_to(e_tot[:, hh:hh + 1], (sb, LANES))
    cbf_ref[...] = cm
    bbf_ref[...] = bm
    for g in range(SSD_GROUPS):
        t1_ref[g] = xdtd[:, g * GROUP_W:(g + 1) * GROUP_W].T.astype(BF16)
    seq_of_row = lax.broadcasted_iota(I32, (tm, SSD_STATE), 0) // l
    heads_per_group = SSD_HEADS // SSD_GROUPS

    def one_seq(j):
        r0 = pl.multiple_of(j * l, l)
        s0 = ssm0_ref[j]
        for g in range(SSD_GROUPS):
            ns = slice(g * SSD_STATE, (g + 1) * SSD_STATE)
            s0g = s0[g * heads_per_group:(g + 1) * heads_per_group].reshape(GROUP_W, SSD_STATE)
            cj = cbf_ref[pl.ds(r0, l), ns].astype(BF16)
            yoff_ref[pl.ds(r0, l), g * GROUP_W:(g + 1) * GROUP_W] = _dot_nt(cj, s0g.astype(BF16))
            bmask = jnp.where(seq_of_row == j, bbf_ref[:, ns], 0.0).astype(BF16)
            upd = _dot(t1_ref[g], bmask)
            for hq in range(heads_per_group):
                hh = g * heads_per_group + hq
                dec = dtab_ref[hh, pl.ds(j, 1), :]
                ssm_out_ref[j, hh] = s0[hh] * dec + upd[hq * SSD_HEAD_DIM:(hq + 1) * SSD_HEAD_DIM]

    def body(jj, carry):
        for u in range(SEQ_UNROLL):
            one_seq(jj * SEQ_UNROLL + u)
        return carry

    lax.fori_loop(0, sb // SEQ_UNROLL, body, 0)
    y = y_diag + yoff_ref[...] * e_e
    mixed_ref[...] = _mixer_back(y, xs, z, out_a, dskip_e[...], g_ob[...])


def _ada_spec(ref, sb, div):
    _, term, row0 = ref
    assert row0 % sb == 0
    return pl.BlockSpec((sb, 1, D_MODEL), lambda i: (row0 // sb + i // div, 0, term))


def _const_spec(a):
    nd = a.ndim
    return pl.BlockSpec(a.shape, lambda i, _nd=nd: (0,) * _nd)


def _spatial_consts(w_spatial, b_spatial, cl, tm):
    w = jnp.where(jnp.tril(jnp.ones((cl, cl), bool)), w_spatial[:, :cl, :cl], 0.0)
    eye = jnp.eye(tm // cl, dtype=F32)
    wbd = jnp.einsum("ab,hts->hatbs", eye, w).reshape(A_HEADS, tm, tm).astype(BF16)
    bias = jnp.tile(jnp.repeat(b_spatial[:, :cl].T, A_HEAD_DIM, axis=1), (tm // cl, 1))
    return wbd, bias


def _ssd_consts(cl):
    r = np.arange(CHUNK)
    same = (r[:, None] // cl) == (r[None, :] // cl)
    tril = same & (r[:, None] >= r[None, :])
    expand = np.zeros((LANES, B_WIDTH), np.float32)
    for hh in range(SSD_HEADS):
        expand[hh, hh * SSD_HEAD_DIM:(hh + 1) * SSD_HEAD_DIM] = 1.0
    return (jnp.asarray(tril, BF16), jnp.asarray(same, BF16), jnp.asarray(expand, BF16),
            jnp.asarray(np.where(tril, 0.0, DECAY_MASKED), F32))


def _front_weights(p):
    w_in = p["w_in"]
    c0, c1 = 3 * A_WIDTH, 3 * A_WIDTH + CONV_DIM
    w_dt = w_in[:, c1:]
    pad8 = lambda v: jnp.pad(v, (0, LANES - SSD_HEADS))
    a = -jnp.exp(p["a_log"])
    return dict(
        g_mix=p["g_mix"][None, :],
        w_uvz=w_in[:, :c0].astype(BF16),
        w_xbc=w_in[:, c0:c1].astype(BF16),
        w_dt=jnp.pad(w_dt, ((0, 0), (0, LANES - SSD_HEADS))).astype(BF16),
        g_v=p["g_v_a"][None, :], g_oa=p["g_out_a"][None, :],
        conv_w=p["conv_w"], conv_b=p["conv_b"][None, :],
        dt_bias=pad8(p["dt_bias"])[None, :],
        a_row=pad8(a)[None, :],
        dskip_e=jnp.repeat(p["d_skip"], SSD_HEAD_DIM)[None, :],
        g_ob=p["g_out_b"][None, :],
    )


def _front_list(fw, wbd, bias_sp):
    return [fw["g_mix"], fw["w_uvz"], fw["w_xbc"], fw["w_dt"], wbd, bias_sp, fw["g_v"], fw["g_oa"],
            fw["conv_w"], fw["conv_b"], fw["dt_bias"], fw["a_row"]]


def _prompt_mixer(x, sh, sc, fw, p):
    nseq, lseq, _ = x.shape
    tile = PROMPT_TILE
    tps = lseq // tile
    nt = nseq * tps
    x4 = x.reshape(nt, tile, D_MODEL)
    wbd, bias_sp = _spatial_consts(p["w_spatial"], p["b_spatial"], CHUNK, CHUNK)
    consts = _front_list(fw, wbd, bias_sp) + list(_ssd_consts(CHUNK)) + [fw["dskip_e"], fw["g_ob"]]
    mixed, conv_new, ssm_new = pl.pallas_call(
        functools.partial(_prompt_mixer_kernel, tps),
        out_shape=(jax.ShapeDtypeStruct((nt * tile, D_MODEL), BF16),
                   jax.ShapeDtypeStruct((nseq, CONV_K - 1, CONV_DIM), F32),
                   jax.ShapeDtypeStruct((nseq, B_WIDTH, SSD_STATE), F32)),
        grid=(nt,),
        in_specs=[pl.BlockSpec((1, tile, D_MODEL), lambda i: (i, 0, 0)), _ada_spec(sh, 1, tps), _ada_spec(sc, 1, tps)]
                 + [_const_spec(a) for a in consts],
        out_specs=(pl.BlockSpec((tile, D_MODEL), lambda i: (i, 0)),
                   pl.BlockSpec((1, CONV_K - 1, CONV_DIM), lambda i: (i // tps, 0, 0)),
                   pl.BlockSpec((1, B_WIDTH, SSD_STATE), lambda i: (i // tps, 0, 0))),
        scratch_shapes=[pltpu.VMEM((PROMPT_SPLITS, 1, tile // PROMPT_SPLITS + CONV_PAD, CONV_DIM), F32),
                        pltpu.VMEM((1, CONV_PAD, CONV_DIM), F32),
                        pltpu.VMEM((SSD_STATE, B_WIDTH), F32)],
        compiler_params=pltpu.CompilerParams(dimension_semantics=("arbitrary",), vmem_limit_bytes=VMEM_LIMIT),
        name="prompt_mixer",
    )(x4, sh[0], sc[0], *consts)
    return mixed, conv_new, ssm_new.reshape(nseq, SSD_HEADS, SSD_HEAD_DIM, SSD_STATE)


def _sample_mixer(x, sh, sc, state_ssm, state_conv, fw, p):
    nseq, l, _ = x.shape
    sb = SAMPLE_SEQ_TILE
    tm = sb * l
    assert tm == CHUNK
    wbd, bias_sp = _spatial_consts(p["w_spatial"], p["b_spatial"], l, tm)
    selseq = jnp.asarray((np.arange(tm)[None, :] // l) == np.arange(sb)[:, None], BF16)
    consts = _front_list(fw, wbd, bias_sp) + list(_ssd_consts(l)) + [fw["dskip_e"], fw["g_ob"], selseq]
    prev = jnp.pad(state_conv, ((0, 0), (CONV_PAD - (CONV_K - 1), 0), (0, 0)))
    ssm_spec = pl.BlockSpec((sb, SSD_HEADS, SSD_HEAD_DIM, SSD_STATE), lambda i: (i, 0, 0, 0))
    return pl.pallas_call(
        _sample_mixer_kernel,
        out_shape=(jax.ShapeDtypeStruct((nseq * l, D_MODEL), BF16),
                   jax.ShapeDtypeStruct((nseq * l, A_WIDTH), F32),
                   jax.ShapeDtypeStruct((nseq, CONV_K - 1, CONV_DIM), F32),
                   jax.ShapeDtypeStruct(state_ssm.shape, F32)),
        grid=(nseq // sb,),
        in_specs=[pl.BlockSpec((sb, l, D_MODEL), lambda i: (i, 0, 0)), _ada_spec(sh, sb, 1), _ada_spec(sc, sb, 1),
                  pl.BlockSpec((sb, CONV_PAD, CONV_DIM), lambda i: (i, 0, 0)), ssm_spec]
                 + [_const_spec(a) for a in consts],
        out_specs=(pl.BlockSpec((tm, D_MODEL), lambda i: (i, 0)),
                   pl.BlockSpec((tm, A_WIDTH), lambda i: (i, 0)),
                   pl.BlockSpec((sb, CONV_K - 1, CONV_DIM), lambda i: (i, 0, 0)),
                   ssm_spec),
        scratch_shapes=[pltpu.VMEM((sb, l + CONV_PAD, CONV_DIM), F32),
                        pltpu.VMEM((tm, B_WIDTH), F32),
                        pltpu.VMEM((tm, SSD_GROUPS * SSD_STATE), F32),
                        pltpu.VMEM((tm, SSD_GROUPS * SSD_STATE), F32),
                        pltpu.VMEM((SSD_GROUPS, GROUP_W, tm), BF16),
                        pltpu.VMEM((SSD_HEADS, sb, LANES), F32)],
        compiler_params=pltpu.CompilerParams(dimension_semantics=("arbitrary",), vmem_limit_bytes=VMEM_LIMIT),
        name="sample_mixer",
    )(x, sh[0], sc[0], prev, state_ssm, *consts)


def _post_kernel(mixed_ref, x_ref, gt_ref, sc_ref, sh_ref, w_out_ref, g_ffn_ref, wr_both_ref, br_ref,
                 x1_ref, h2p_ref, ids_ref):
    x3 = x_ref[...]
    sb, l, _ = x3.shape
    tm = sb * l
    mix = _dot(mixed_ref[...], w_out_ref[...]).reshape(sb, l, D_MODEL)
    x1 = x3 + gt_ref[...] * mix
    x1_ref[...] = x1
    xn = x1 * lax.rsqrt(jnp.mean(x1 * x1, axis=-1, keepdims=True) + EPS) * g_ffn_ref[...]
    h2 = (xn * (1.0 + sc_ref[...]) + sh_ref[...]).reshape(tm, D_MODEL)
    h_hi, h_lo = _split(h2)
    h2p_ref[:, :D_MODEL] = h_hi
    both = _dot_nt(wr_both_ref[...], h_hi)
    logits = (both[:N_EXPERTS] + both[N_EXPERTS:] + _dot_nt(wr_both_ref[:N_EXPERTS, :], h_lo)
              + br_ref[...])
    e_iota = lax.broadcasted_iota(I32, logits.shape, 0)
    vals, idxs = [], []
    for _ in range(TOP_K):
        m = jnp.max(logits, axis=0, keepdims=True)
        idx = jnp.min(jnp.where(logits == m, e_iota, N_EXPERTS), axis=0, keepdims=True)
        vals.append(m)
        idxs.append(idx)
        logits = jnp.where(e_iota == idx, -jnp.inf, logits)
    ex = [jnp.exp(v - vals[0]) for v in vals]
    tot = ex[0] + ex[1] + ex[2] + ex[3]
    ids = jnp.concatenate(idxs, axis=0)
    ids_ref[...] = ids
    wts = jnp.concatenate([e / tot for e in ex], axis=0)
    w_hi = wts.astype(BF16).astype(F32)
    info = jnp.concatenate([ids.astype(F32), w_hi, wts - w_hi, jnp.zeros((TOP_K, tm), F32)], axis=0).astype(BF16)
    r = lax.broadcasted_iota(I32, (INFO_ROWS, LANES), 0)
    c = lax.broadcasted_iota(I32, (INFO_ROWS, LANES), 1)
    place = jnp.where(r == c, 1.0, 0.0).astype(BF16)
    h2p_ref[:, D_MODEL:] = lax.dot_general(info, place, (((0,), (0,)), ((), ())),
                                           preferred_element_type=F32).astype(BF16)


def _post(mixed, x, gt, sc, sh, w_out_b, g_ffn, wr_both, br, sb, l, seq_div):
    n3, _, _ = x.shape
    nblk = n3 // sb
    tm = sb * l
    t = n3 * l
    consts = [w_out_b, g_ffn, wr_both, br]
    return pl.pallas_call(
        _post_kernel,
        out_shape=(jax.ShapeDtypeStruct(x.shape, F32),
                   jax.ShapeDtypeStruct((t, ROW_W), BF16),
                   jax.ShapeDtypeStruct((TOP_K, t), I32)),
        grid=(nblk,),
        in_specs=[pl.BlockSpec((tm, D_MODEL), lambda i: (i, 0)),
                  pl.BlockSpec((sb, l, D_MODEL), lambda i: (i, 0, 0)),
                  _ada_spec(gt, sb, seq_div), _ada_spec(sc, sb, seq_div), _ada_spec(sh, sb, seq_div)]
                 + [_const_spec(a) for a in consts],
        out_specs=(pl.BlockSpec((sb, l, D_MODEL), lambda i: (i, 0, 0)),
                   pl.BlockSpec((tm, ROW_W), lambda i: (i, 0)),
                   pl.BlockSpec((TOP_K, tm), lambda i: (0, i))),
        compiler_params=pltpu.CompilerParams(dimension_semantics=("arbitrary",), vmem_limit_bytes=VMEM_LIMIT),
        name="post",
    )(mixed, x, gt[0], sc[0], sh[0], *consts)


def _strict_upper(n):
    r = lax.broadcasted_iota(I32, (n, n), 0)
    c = lax.broadcasted_iota(I32, (n, n), 1)
    return jnp.where(r < c, 1.0, 0.0).astype(BF16)


def _expert_prefix(col):
    r = lax.broadcasted_iota(I32, (N_EXPERTS, N_EXPERTS), 0)
    c = lax.broadcasted_iota(I32, (N_EXPERTS, N_EXPERTS), 1)
    as_row = jnp.sum(jnp.where(r == c, col, 0.0), axis=0, keepdims=True)
    return jnp.sum(jnp.where(c < r, as_row, 0.0), axis=1, keepdims=True)


def _plan_kernel(dump_group, ids_ref, lr_ref, gdst_ref, tile_e_ref):
    tm = TOK_TILE
    nb = ids_ref.shape[1] // tm
    blk_lane = lax.broadcasted_iota(I32, (N_EXPERTS, LANES), 1)

    def block_masks(b):
        ids = ids_ref[:, pl.ds(pl.multiple_of(b * tm, tm), tm)]
        e_iota = lax.broadcasted_iota(I32, (N_EXPERTS, tm), 0)
        onehot = [ids[k:k + 1, :] == e_iota for k in range(TOP_K)]
        sel = (onehot[0] | onehot[1]) | (onehot[2] | onehot[3])
        return onehot, jnp.where(sel, 1.0, 0.0)

    def count(b, seg):
        _, m = block_masks(b)
        seg_b = jnp.ceil(jnp.sum(m, axis=1, keepdims=True) * (1.0 / ROW_GROUP))
        return jnp.where(blk_lane == b, seg_b, seg)

    seg = lax.fori_loop(0, nb, count, jnp.zeros((N_EXPERTS, LANES), F32))
    tot = jnp.sum(seg, axis=1, keepdims=True)
    padded = jnp.ceil(tot * (1.0 / TILE_GROUPS)) * TILE_GROUPS
    gstart = _expert_prefix(padded)
    gb = gstart + _dot(seg.astype(BF16), _strict_upper(LANES))
    r = lax.broadcasted_iota(I32, (N_EXPERTS, LANES), 0)
    as_row = lambda col: jnp.sum(jnp.where(r == blk_lane, col, 0.0), axis=0, keepdims=True)
    n_used = jnp.sum(padded, axis=0, keepdims=True)
    lane = lax.broadcasted_iota(I32, (1, LANES), 1)
    tiles = jnp.where(lane == N_EXPERTS, n_used, as_row(gstart)) * (1.0 / TILE_GROUPS)
    meta = jnp.concatenate([tiles, as_row(gstart + tot), as_row(padded - tot),
                            jnp.zeros((META_ROWS - 3, LANES), F32)], axis=0)
    tile_e_ref[...] = meta.astype(I32)
    upper = _strict_upper(tm)

    def place(b, carry):
        onehot, m = block_masks(b)
        seg_b = jnp.sum(jnp.where(blk_lane == b, seg, 0.0), axis=1, keepdims=True)
        gb_b = jnp.sum(jnp.where(blk_lane == b, gb, 0.0), axis=1, keepdims=True)
        loc_b = _expert_prefix(seg_b)
        before = _dot(m.astype(BF16), upper) + loc_b * ROW_GROUP
        lr_ref[:, pl.ds(pl.multiple_of(b * tm, tm), tm)] = jnp.concatenate(
            [jnp.sum(jnp.where(onehot[k], before, 0.0), axis=0, keepdims=True) for k in range(TOP_K)],
            axis=0).astype(I32)
        g = lax.broadcasted_iota(I32, (N_EXPERTS, GDST_LANES), 1).astype(F32)
        inside = (loc_b <= g) & (g < loc_b + seg_b)
        dst = jnp.sum(jnp.where(inside, gb_b + g - loc_b, 0.0), axis=0, keepdims=True)
        used = jnp.sum(jnp.where(inside, 1.0, 0.0), axis=0, keepdims=True) > 0.5
        dump = dump_group + lax.convert_element_type(b % 2, F32) * LOCAL_GROUPS + g[0:1, :]
        gdst_ref[b] = jnp.where(used, dst, dump).astype(I32)
        return carry

    lax.fori_loop(0, nb, place, 0)


def _plan(ids, dump_group):
    t = ids.shape[1]
    nb = t // TOK_TILE
    assert nb <= LANES
    return pl.pallas_call(
        functools.partial(_plan_kernel, float(dump_group)),
        out_shape=(jax.ShapeDtypeStruct((TOP_K, t), I32),
                   jax.ShapeDtypeStruct((nb, 1, GDST_LANES), I32),
                   jax.ShapeDtypeStruct((META_ROWS, LANES), I32)),
        name="plan",
    )(ids)


def _sort_matrix(lr, c):
    r_iota = lax.broadcasted_iota(I32, (MASK_ROWS, lr.shape[1]), 0) + c * MASK_ROWS
    p = jnp.where(r_iota == lr[TOP_K - 1:TOP_K, :], 1.0, 0.0)
    for k in range(TOP_K - 1):
        p = jnp.where(r_iota == lr[k:k + 1, :], 1.0, p)
    return p.astype(BF16)


def _group_copies(loc_ref, slot, far_ref, gdst_ref, sem, to_far):
    copies = []
    for g in range(LOCAL_GROUPS):
        dst = pl.multiple_of(gdst_ref[0, 0, g] * ROW_GROUP, ROW_GROUP)
        near = loc_ref.at[slot, pl.ds(g * ROW_GROUP, ROW_GROUP)]
        far = far_ref.at[pl.ds(dst, ROW_GROUP)]
        copies.append(pltpu.make_async_copy(near, far, sem.at[slot]) if to_far
                      else pltpu.make_async_copy(far, near, sem.at[slot]))
    return copies


def _wait_groups(loc_ref, slot, sem):
    pltpu.make_async_copy(loc_ref.at[slot], loc_ref.at[slot], sem.at[slot]).wait()


def _dispatch_kernel(n_first, n_tiles_all, gdst_ref, meta_ref, lr_ref, ha_ref, hb_ref, xs_ref,
                     h_ref, loc_ref, zero_ref, sem, zsem):
    i = pl.program_id(0)
    last = pl.num_programs(0) - 1
    slot = i % 2

    @pl.when(i < n_first)
    def _():
        h_ref[...] = ha_ref[...]

    @pl.when(i >= n_first)
    def _():
        h_ref[...] = hb_ref[...]

    lr = lr_ref[...]
    copies = _group_copies(loc_ref, slot, xs_ref, gdst_ref, sem, True)
    n_chunks = LOCAL_ROWS // MASK_ROWS
    per_chunk = LOCAL_GROUPS // n_chunks
    for c in range(n_chunks):
        loc_ref[slot, c * MASK_ROWS:(c + 1) * MASK_ROWS, :] = _dot(_sort_matrix(lr, c), h_ref[...]).astype(BF16)
        for cp in copies[c * per_chunk:(c + 1) * per_chunk]:
            cp.start()

    @pl.when(i > 0)
    def _():
        _wait_groups(loc_ref, 1 - slot, sem)

    @pl.when(i == last)
    def _():
        _wait_groups(loc_ref, slot, sem)
        zero_ref[...] = jnp.zeros_like(zero_ref)

        def pad_copy(e, j):
            row = pl.multiple_of((meta_ref[1, e] + j) * ROW_GROUP, ROW_GROUP)
            return pltpu.make_async_copy(zero_ref.at[pl.ds(0, ROW_GROUP)], xs_ref.at[pl.ds(row, ROW_GROUP)], zsem)

        def tile_copy(t):
            row = pl.multiple_of(t * ROW_TILE, ROW_TILE)
            return pltpu.make_async_copy(zero_ref, xs_ref.at[pl.ds(row, ROW_TILE)], zsem)

        def pads(fn):
            def body(e, carry):
                for j in range(TILE_GROUPS - 1):
                    @pl.when(j < meta_ref[2, e])
                    def _():
                        fn(pad_copy(e, j))
                return carry
            lax.fori_loop(0, N_EXPERTS, body, 0)

        def tiles(fn):
            def body(t, carry):
                fn(tile_copy(t))
                return carry
            lax.fori_loop(meta_ref[0, N_EXPERTS], n_tiles_all, body, 0)

        pads(lambda c: c.start())
        tiles(lambda c: c.start())
        pads(lambda c: c.wait())
        tiles(lambda c: c.wait())


def _dispatch(gdst, meta, lr, h_a, h_b, n_tiles_all):
    tm = TOK_TILE
    na, nb2 = h_a.shape[0] // tm, h_b.shape[0] // tm
    return pl.pallas_call(
        functools.partial(_dispatch_kernel, na, n_tiles_all),
        out_shape=jax.ShapeDtypeStruct((n_tiles_all * ROW_TILE, ROW_W), BF16),
        grid=(na + nb2,),
        in_specs=[pl.BlockSpec((1, 1, GDST_LANES), lambda i: (i, 0, 0), memory_space=pltpu.SMEM),
                  pl.BlockSpec((META_ROWS, LANES), lambda i: (0, 0), memory_space=pltpu.SMEM),
                  pl.BlockSpec((TOP_K, tm), lambda i: (0, i)),
                  pl.BlockSpec((tm, ROW_W), lambda i: (jnp.minimum(i, na - 1), 0)),
                  pl.BlockSpec((tm, ROW_W), lambda i: (jnp.maximum(i - na, 0), 0))],
        out_specs=pl.BlockSpec(memory_space=pl.ANY),
        scratch_shapes=[pltpu.VMEM((tm, ROW_W), BF16), pltpu.VMEM((2, LOCAL_ROWS, ROW_W), BF16),
                        pltpu.VMEM((ROW_TILE, ROW_W), BF16),
                        pltpu.SemaphoreType.DMA((2,)), pltpu.SemaphoreType.DMA],
        compiler_params=pltpu.CompilerParams(dimension_semantics=("arbitrary",), vmem_limit_bytes=VMEM_LIMIT),
        name="dispatch",
    )(gdst, meta, lr, h_a, h_b)


def _expert_kernel(n_tiles_all, ts_ref, xs_ref, wg_ref, wu_ref, wd_ref, bg_ref, bu_ref, bd_ref, y_ref,
                   wstage, wgb, wub, wdb, xbuf, ybuf, sem_w, sem_in, sem_out):
    e = pl.program_id(0)
    t0 = ts_ref[e]
    nt = ts_ref[e + 1] - t0
    wslot = e % 2

    def w_copies(ex, slot):
        return [pltpu.make_async_copy(w_ref.at[ex], wstage.at[slot, j], sem_w.at[slot])
                for j, w_ref in enumerate((wg_ref, wu_ref, wd_ref))]

    def in_copy(t, slot):
        rows = pl.ds(pl.multiple_of((t0 + t) * ROW_TILE, ROW_TILE), ROW_TILE)
        return pltpu.make_async_copy(xs_ref.at[rows], xbuf.at[slot], sem_in.at[slot])

    def out_copy(tile, slot):
        rows = pl.ds(pl.multiple_of(tile * ROW_TILE, ROW_TILE), ROW_TILE)
        return pltpu.make_async_copy(ybuf.at[slot], y_ref.at[rows], sem_out.at[slot])

    @pl.when((e == 0) & (nt > 0))
    def _():
        for c in w_copies(0, 0):
            c.start()

    for j in range(TILE_SLOTS):
        @pl.when(nt > j)
        def _():
            in_copy(j, j).start()

    @pl.when(e + 1 < N_EXPERTS)
    def _():
        @pl.when(ts_ref[e + 2] > ts_ref[e + 1])
        def _():
            for c in w_copies(e + 1, 1 - wslot):
                c.start(priority=1)

    @pl.when(nt > 0)
    def _():
        for c in w_copies(e, wslot):
            c.wait()
        wgb[...] = wstage[wslot, 0].astype(BF16)
        wub[...] = wstage[wslot, 1].astype(BF16)
        wdb[...] = wstage[wslot, 2].astype(BF16)
        e_f = e.astype(F32)

        def acquire(t):
            slot = t % TILE_SLOTS
            in_copy(t, slot).wait()

            @pl.when(t >= TILE_SLOTS)
            def _():
                out_copy(t0 + t - TILE_SLOTS, slot).wait()

        def compute(t):
            slot = t % TILE_SLOTS
            xw = xbuf[slot]
            x = xw[:, :D_MODEL]
            info = xw[:, D_MODEL:].astype(F32)
            w_row = jnp.zeros((ROW_TILE, 1), F32)
            for k in range(TOP_K):
                wk = info[:, TOP_K + k:TOP_K + k + 1] + info[:, 2 * TOP_K + k:2 * TOP_K + k + 1]
                w_row = w_row + jnp.where(info[:, k:k + 1] == e_f, wk, 0.0)
            g = jnp.minimum(_dot(x, wgb[...]) + bg_ref[0], SWIGLU_LIMIT)
            u = jnp.clip(_dot(x, wub[...]) + bu_ref[0], -SWIGLU_LIMIT, SWIGLU_LIMIT)
            act = g * jax.nn.sigmoid(SWIGLU_ALPHA * g) * (u + 1.0)
            ybuf[slot] = ((_dot(act.astype(BF16), wdb[...]) + bd_ref[0]) * w_row).astype(BF16)

        def release(t):
            slot = t % TILE_SLOTS
            out_copy(t0 + t, slot).start()

            @pl.when(t + TILE_SLOTS < nt)
            def _():
                in_copy(t + TILE_SLOTS, slot).start()

        def pair(p, carry):
            ta, tb = 2 * p, 2 * p + 1
            acquire(ta)
            acquire(tb)
            compute(ta)
            compute(tb)
            release(ta)
            release(tb)
            return carry

        lax.fori_loop(0, nt // 2, pair, 0)

        @pl.when(nt % 2 == 1)
        def _():
            acquire(nt - 1)
            compute(nt - 1)
            release(nt - 1)

        for j in range(1, TILE_SLOTS + 1):
            @pl.when(nt >= j)
            def _():
                out_copy(t0 + nt - j, (nt - j) % TILE_SLOTS).wait()

    @pl.when(e == pl.num_programs(0) - 1)
    def _():
        n_used = ts_ref[N_EXPERTS]
        ybuf[0] = jnp.zeros((ROW_TILE, D_MODEL), BF16)

        def zstart(tile, carry):
            out_copy(tile, 0).start()
            return carry

        def zwait(tile, carry):
            out_copy(tile, 0).wait()
            return carry

        lax.fori_loop(n_used, n_tiles_all, zstart, 0)
        lax.fori_loop(n_used, n_tiles_all, zwait, 0)


def _experts(tile_start, xs, w_gate, b_gate, w_up, b_up, w_down, b_down):
    n_rows = xs.shape[0]
    b_spec = pl.BlockSpec((1, 1, D_MODEL), lambda e, ts: (e, 0, 0))
    any_spec = pl.BlockSpec(memory_space=pl.ANY)
    return pl.pallas_call(
        functools.partial(_expert_kernel, n_rows // ROW_TILE),
        out_shape=jax.ShapeDtypeStruct((n_rows, D_MODEL), BF16),
        grid_spec=pltpu.PrefetchScalarGridSpec(
            num_scalar_prefetch=1,
            grid=(N_EXPERTS,),
            in_specs=[any_spec, any_spec, any_spec, any_spec, b_spec, b_spec, b_spec],
            out_specs=any_spec,
            scratch_shapes=[pltpu.VMEM((2, 3, D_MODEL, D_MODEL), F32)]
                           + [pltpu.VMEM((D_MODEL, D_MODEL), BF16)] * 3
                           + [pltpu.VMEM((TILE_SLOTS, ROW_TILE, ROW_W), BF16),
                              pltpu.VMEM((TILE_SLOTS, ROW_TILE, D_MODEL), BF16),
                              pltpu.SemaphoreType.DMA((2,)), pltpu.SemaphoreType.DMA((TILE_SLOTS,)),
                              pltpu.SemaphoreType.DMA((TILE_SLOTS,))],
        ),
        compiler_params=pltpu.CompilerParams(dimension_semantics=("arbitrary",), vmem_limit_bytes=VMEM_LIMIT),
        name="experts",
    )(tile_start, xs, w_gate, w_up, w_down, b_gate[:, None, :], b_up[:, None, :], b_down[:, None, :])


def _combine_kernel(gdst_ref, gdst_next_ref, lr_ref, y_ref, x1_ref, gt_ref, gf_ref, out_ref, loc_ref, sem):
    i = pl.program_id(0)
    slot = i % 2
    x1 = x1_ref[...]
    sb, l, _ = x1.shape

    @pl.when(i == 0)
    def _():
        for c in _group_copies(loc_ref, slot, y_ref, gdst_ref, sem, False):
            c.start()

    _wait_groups(loc_ref, slot, sem)

    prefetch = _group_copies(loc_ref, 1 - slot, y_ref, gdst_next_ref, sem, False)
    n_chunks = LOCAL_ROWS // MASK_ROWS
    per_chunk = LOCAL_GROUPS // n_chunks
    lr = lr_ref[...]
    moe = jnp.zeros((sb * l, D_MODEL), F32)
    for c in range(n_chunks):
        rows = loc_ref[slot, c * MASK_ROWS:(c + 1) * MASK_ROWS, :]
        moe = moe + lax.dot_general(_sort_matrix(lr, c), rows, (((0,), (0,)), ((), ())),
                                    preferred_element_type=F32)
        for cp in prefetch[c * per_chunk:(c + 1) * per_chunk]:
            cp.start()
    x2 = x1 + gt_ref[...] * moe.reshape(sb, l, D_MODEL)
    out_ref[...] = x2 * lax.rsqrt(jnp.mean(x2 * x2, axis=-1, keepdims=True) + EPS) * gf_ref[...]

    @pl.when(i == pl.num_programs(0) - 1)
    def _():
        _wait_groups(loc_ref, 1 - slot, sem)


def _combine(gdst, lr, y, x1, gt, g_final, sb, l, seq_div, blk_off):
    n3 = x1.shape[0]
    nblk = n3 // sb
    tm = sb * l
    assert tm == TOK_TILE
    return pl.pallas_call(
        _combine_kernel,
        out_shape=jax.ShapeDtypeStruct(x1.shape, F32),
        grid=(nblk,),
        in_specs=[pl.BlockSpec((1, 1, GDST_LANES), lambda i: (i + blk_off, 0, 0), memory_space=pltpu.SMEM),
                  pl.BlockSpec((1, 1, GDST_LANES), lambda i: (jnp.minimum(i + 1, nblk - 1) + blk_off, 0, 0),
                               memory_space=pltpu.SMEM),
                  pl.BlockSpec((TOP_K, tm), lambda i: (0, i + blk_off)),
                  pl.BlockSpec(memory_space=pl.ANY),
                  pl.BlockSpec((sb, l, D_MODEL), lambda i: (i, 0, 0)),
                  _ada_spec(gt, sb, seq_div),
                  pl.BlockSpec((1, D_MODEL), lambda i: (0, 0))],
        out_specs=pl.BlockSpec((sb, l, D_MODEL), lambda i: (i, 0, 0)),
        scratch_shapes=[pltpu.VMEM((2, LOCAL_ROWS, D_MODEL), BF16), pltpu.SemaphoreType.DMA((2,))],
        compiler_params=pltpu.CompilerParams(dimension_semantics=("arbitrary",), vmem_limit_bytes=VMEM_LIMIT),
        name="combine",
    )(gdst, gdst, lr, y, x1, gt[0], g_final)


def kernel(x_prompt, x_sample, c_prompt, c_sample, state_ssm, state_conv, w_ada, b_ada, g_mix, w_in, g_v_a, w_spatial, b_spatial, g_out_a, conv_w, conv_b, dt_bias, a_log, d_skip, g_out_b, w_out, g_ffn, w_router, b_router, w_gate, b_gate, w_up, b_up, w_down, b_down, g_final):
    assert w_ada.shape[0] == 1, "single-layer step"
    p = dict(w_in=w_in[0], g_mix=g_mix[0], g_v_a=g_v_a[0], w_spatial=w_spatial[0], b_spatial=b_spatial[0],
             g_out_a=g_out_a[0], conv_w=conv_w[0], conv_b=conv_b[0], dt_bias=dt_bias[0], a_log=a_log[0],
             d_skip=d_skip[0], g_out_b=g_out_b[0])
    bp, lp, _ = x_prompt.shape
    bs, ls, _ = x_sample.shape
    tp, ts = bp * lp, bs * ls

    ada = _ada(jnp.concatenate([c_sample, c_prompt], axis=0), w_ada[0], b_ada[0][None, :])
    ada = ada.reshape(bs + bp, 1, 6 * D_MODEL)
    ada_s = [(ada, j, 0) for j in range(6)]
    ada_p = [(ada, j, bs) for j in range(6)]

    fw = _front_weights(p)
    mixed_p, conv_p, ssm_p = _prompt_mixer(x_prompt, ada_p[0], ada_p[1], fw, p)
    mixed_s, v_s, conv_s, ssm_s = _sample_mixer(x_sample, ada_s[0], ada_s[1], state_ssm[0], state_conv[0], fw, p)

    w_out_b = w_out[0].astype(BF16)
    g_ffn2 = g_ffn[0][None, :]
    wr_t = w_router[0].T
    wr_hi = wr_t.astype(BF16)
    wr_both = jnp.concatenate([wr_hi, (wr_t - wr_hi.astype(F32)).astype(BF16)], axis=0)
    br = b_router[0][:, None]
    tps = lp // TOK_TILE
    sbs = TOK_TILE // ls
    xp3 = x_prompt.reshape(bp * tps, TOK_TILE, D_MODEL)
    x1_p, h2p_p, ids_p = _post(mixed_p, xp3, ada_p[2], ada_p[4], ada_p[3], w_out_b, g_ffn2, wr_both, br,
                               1, TOK_TILE, tps)
    x1_s, h2p_s, ids_s = _post(mixed_s, x_sample, ada_s[2], ada_s[4], ada_s[3], w_out_b, g_ffn2, wr_both, br,
                               sbs, ls, 1)

    n_blocks = (tp + ts) // TOK_TILE
    max_groups = (tp + ts) * TOP_K // ROW_GROUP + n_blocks * N_EXPERTS + N_EXPERTS * (TILE_GROUPS - 1)
    n_tiles = -(-max_groups // TILE_GROUPS)
    n_tiles_all = n_tiles + 2 * LOCAL_GROUPS // TILE_GROUPS
    lr, gdst, meta = _plan(jnp.concatenate([ids_p, ids_s], axis=1), n_tiles * TILE_GROUPS)

    xs = _dispatch(gdst, meta, lr, h2p_p, h2p_s, n_tiles_all)
    y = _experts(meta[0], xs, w_gate[0], b_gate[0], w_up[0], b_up[0], w_down[0], b_down[0])

    gf = g_final[None, :]
    y_p = _combine(gdst, lr, y, x1_p, ada_p[5], gf, 1, TOK_TILE, tps, 0).reshape(bp, lp, D_MODEL)
    y_s = _combine(gdst, lr, y, x1_s, ada_s[5], gf, sbs, ls, 1, tp // TOK_TILE)

    return (y_p, y_s, ssm_p[None], conv_p[None], ssm_s[None], conv_s[None], v_s.reshape(1, bs, ls, A_WIDTH))
```

```python
import functools
import math

import numpy as np
import jax
import jax.numpy as jnp
from jax import lax
from jax.experimental import pallas as pl
from jax.experimental.pallas import tpu as pltpu

F32 = jnp.float32
BF16 = jnp.bfloat16
I32 = jnp.int32

D_MODEL = 1024
A_WIDTH = 512
A_HEADS = 4
A_HEAD_DIM = 128
CHUNK = 128
B_WIDTH = 512
SSD_HEAD_DIM = 64
SSD_HEADS = 8
SSD_GROUPS = 2
SSD_STATE = 128
GROUP_W = B_WIDTH // SSD_GROUPS
CONV_K = 4
CONV_DIM = 1024
CONV_PAD = 8
N_EXPERTS = 32
TOP_K = 4
SWIGLU_LIMIT = 7.0
SWIGLU_ALPHA = 1.702
EPS = 1e-6
DECAY_MASKED = -1e30
LANES = 128

TOK_TILE = 512
SAMPLE_SEQ_TILE = 16
SEQ_UNROLL = 8
ROW_TILE = 256
ROW_GROUP = 16
TILE_GROUPS = ROW_TILE // ROW_GROUP
LOCAL_GROUPS = TOK_TILE * TOP_K // ROW_GROUP + N_EXPERTS
LOCAL_ROWS = LOCAL_GROUPS * ROW_GROUP
GDST_LANES = -(-LOCAL_GROUPS // LANES) * LANES
ROW_W = D_MODEL + LANES
INFO_ROWS = 16
MASK_ROWS = 256
META_ROWS = 8
POST_TILE = 1024
PROMPT_TILE = 512
PROMPT_SPLITS = 2
TILE_SLOTS = 4
VMEM_LIMIT = 56 * 1024 * 1024


def _dot(a, b):
    return jnp.dot(a, b, preferred_element_type=F32)


def _dot_nt(a, b):
    return lax.dot_general(a, b, (((1,), (1,)), ((), ())), preferred_element_type=F32)


def _split(x):
    hi = x.astype(BF16)
    lo = (x - hi.astype(F32)).astype(BF16)
    return hi, lo


def _dot_exact_l(t, x):
    hi, lo = _split(x)
    return _dot(t, hi) + _dot(t, lo)


def _dot_exact_r(x, t):
    hi, lo = _split(x)
    return _dot(hi, t) + _dot(lo, t)


def _silu(x):
    return x * jax.nn.sigmoid(x)


def _gelu(x):
    return 0.5 * x * (1.0 + lax.erf(x * (1.0 / math.sqrt(2.0))))


def _softplus(x):
    return jnp.maximum(x, 0.0) + jnp.log1p(jnp.exp(-jnp.abs(x)))


def _rms(x, g):
    return x * lax.rsqrt(jnp.mean(x * x, axis=-1, keepdims=True) + EPS) * g


def _ada_kernel(c_ref, w_ref, b_ref, o_ref):
    s_hi, s_lo = _split(_silu(c_ref[...]))
    w_hi, w_lo = _split(w_ref[...])
    o_ref[...] = _dot(s_hi, w_hi) + _dot(s_lo, w_hi) + _dot(s_hi, w_lo) + b_ref[...]


def _ada(c_all, w_ada, b_ada):
    m = c_all.shape[0]
    n = w_ada.shape[1]
    bn = 1024
    return pl.pallas_call(
        _ada_kernel,
        out_shape=jax.ShapeDtypeStruct((m, n), F32),
        grid=(n // bn,),
        in_specs=[pl.BlockSpec((m, D_MODEL), lambda j: (0, 0)),
                  pl.BlockSpec((D_MODEL, bn), lambda j: (0, j)),
                  pl.BlockSpec((1, bn), lambda j: (0, j))],
        out_specs=pl.BlockSpec((m, bn), lambda j: (0, j)),
        compiler_params=pltpu.CompilerParams(dimension_semantics=("arbitrary",), vmem_limit_bytes=VMEM_LIMIT),
        name="ada",
    )(c_all, w_ada, b_ada)


def _mixer_front(x3, sh, sc, prev, refs, xp_ref):
    (g_mix, w_uvz, w_xbc, w_dt, wbd, bias_sp, g_v, g_oa, conv_w, conv_b, dt_bias, a_row) = refs
    sb, l, _ = x3.shape
    tm = sb * l
    xn = x3 * lax.rsqrt(jnp.mean(x3 * x3, axis=-1, keepdims=True) + EPS) * g_mix[...]
    h = (xn * (1.0 + sc) + sh).reshape(tm, D_MODEL)
    hb = h.astype(BF16)
    uvz = _dot(hb, w_uvz[...])
    xbc = _dot(hb, w_xbc[...])
    dt_raw = _dot(hb, w_dt[...])

    u = _gelu(uvz[:, :A_WIDTH])
    vg = _gelu(uvz[:, A_WIDTH:2 * A_WIDTH])
    z = uvz[:, 2 * A_WIDTH:]
    v_parts, s_parts = [], []
    for hd in range(A_HEADS):
        sl = slice(hd * A_HEAD_DIM, (hd + 1) * A_HEAD_DIM)
        vh = _rms(vg[:, sl], g_v[:, sl])
        v_parts.append(vh)
        vb = vh.astype(BF16)
        s_parts.append(jnp.concatenate([_dot(wbd[hd], vb[r0:r0 + CHUNK]) + bias_sp[:, sl]
                                        for r0 in range(0, tm, CHUNK)], axis=0))
    v = jnp.concatenate(v_parts, axis=1)
    s_a = jnp.concatenate(s_parts, axis=1)
    out_a = _rms(u * s_a, g_oa[...])

    xp_ref[:, 0:CONV_PAD, :] = prev
    xp_ref[:, CONV_PAD:, :] = xbc.reshape(sb, l, CONV_DIM)
    xp = xp_ref[...]
    acc = conv_b[...] + xp[:, CONV_PAD:, :] * conv_w[CONV_K - 1:CONV_K, :]
    for s in range(1, CONV_K):
        back = pltpu.roll(xp, s, axis=1)[:, CONV_PAD:, :]
        acc = acc + back * conv_w[CONV_K - 1 - s:CONV_K - s, :]
    xc = _silu(acc).reshape(tm, CONV_DIM)
    dt = _softplus(dt_raw + dt_bias[...])
    d_a = dt * a_row[...]
    return out_a, v, z, xc, dt, d_a


def _ssd_chunk(xs, bm, cm, dt, d_a, cref):
    tril, ones, expand, neg_mask = cref
    cs = _dot_exact_l(tril[...], d_a)
    cs_t = cs.T
    dt_t = dt.T
    cs_tot = _dot_exact_l(ones[...], d_a)
    vals = jnp.concatenate([dt * jnp.exp(cs_tot - cs), jnp.exp(cs)], axis=0)
    vals_e = _dot_exact_r(vals, expand[...])
    n = xs.shape[0]
    w_e, e_e = vals_e[:n], vals_e[n:]
    xdtd = xs * w_e
    neg = neg_mask[...]
    row_lt_half = lax.broadcasted_iota(I32, (2 * n, LANES), 0) < n
    lane_lt_half = lax.broadcasted_iota(I32, (2 * n, LANES), 1) < SSD_HEAD_DIM
    y_parts = []
    for g in range(SSD_GROUPS):
        cb = _dot_nt(cm[:, g * SSD_STATE:(g + 1) * SSD_STATE].astype(BF16),
                     bm[:, g * SSD_STATE:(g + 1) * SSD_STATE].astype(BF16))
        for hp in range(SSD_HEADS // SSD_GROUPS // 2):
            h0 = g * (SSD_HEADS // SSD_GROUPS) + 2 * hp
            ms = []
            for hh in (h0, h0 + 1):
                diff = cs[:, hh:hh + 1] - cs_t[hh:hh + 1, :]
                ms.append((cb * jnp.exp(diff + neg) * dt_t[hh:hh + 1, :]).astype(BF16))
            pair = xs[:, h0 * SSD_HEAD_DIM:(h0 + 2) * SSD_HEAD_DIM]
            rhs = jnp.where(row_lt_half == lane_lt_half, jnp.concatenate([pair, pair], axis=0), 0.0).astype(BF16)
            y_parts.append(_dot(jnp.concatenate(ms, axis=1), rhs))
    y_diag = jnp.concatenate(y_parts, axis=1)
    return y_diag, e_e, xdtd, cs_tot


def _mixer_back(y, xs, z, out_a, dskip_e, g_ob):
    y = y + xs * dskip_e
    gated = y * _silu(z)
    parts = [_rms(gated[:, g * GROUP_W:(g + 1) * GROUP_W], g_ob[:, g * GROUP_W:(g + 1) * GROUP_W])
             for g in range(SSD_GROUPS)]
    return jnp.concatenate([out_a] + parts, axis=1).astype(BF16)


N_FRONT = 12
N_SSD = 4


def _prompt_mixer_kernel(tiles_per_seq, x_ref, sh_ref, sc_ref, *rest):
    front = rest[:N_FRONT]
    cref = rest[N_FRONT:N_FRONT + N_SSD]
    dskip_e, g_ob = rest[N_FRONT + N_SSD:N_FRONT + N_SSD + 2]
    mixed_ref, conv_out_ref, ssm_out_ref = rest[N_FRONT + N_SSD + 2:N_FRONT + N_SSD + 5]
    xp_ref, carry_ref, st_ref = rest[N_FRONT + N_SSD + 5:]
    i = pl.program_id(0)
    first = (i % tiles_per_seq) == 0

    @pl.when(first)
    def _():
        carry_ref[...] = jnp.zeros_like(carry_ref)
        st_ref[...] = jnp.zeros_like(st_ref)

    l = x_ref.shape[1]
    hl = l // PROMPT_SPLITS
    parts = []
    prev = carry_ref[...]
    for hh in range(PROMPT_SPLITS):
        xp_h = xp_ref.at[hh]
        parts.append(_mixer_front(x_ref[:, hh * hl:(hh + 1) * hl, :], sh_ref[...], sc_ref[...], prev, front, xp_h))
        prev = xp_h[:, hl:hl + CONV_PAD, :]
    carry_ref[...] = prev
    out_a, _, z, xc, dt, d_a = [jnp.concatenate([p[j] for p in parts], axis=0) for j in range(6)]
    xs = xc[:, :B_WIDTH]
    y_rows = []
    for c in range(l // CHUNK):
        r = slice(c * CHUNK, (c + 1) * CHUNK)
        bm = xc[r, B_WIDTH:B_WIDTH + SSD_GROUPS * SSD_STATE]
        cm = xc[r, B_WIDTH + SSD_GROUPS * SSD_STATE:]
        y_diag, e_e, xdtd, _ = _ssd_chunk(xs[r], bm, cm, dt[r], d_a[r], cref)
        st = st_ref[...]
        y_off, upd = [], []
        for g in range(SSD_GROUPS):
            gs = slice(g * GROUP_W, (g + 1) * GROUP_W)
            ns = slice(g * SSD_STATE, (g + 1) * SSD_STATE)
            y_off.append(_dot(cm[:, ns].astype(BF16), st[:, gs].astype(BF16)))
            upd.append(_dot(bm[:, ns].T.astype(BF16), xdtd[:, gs].astype(BF16)))
        y_rows.append(y_diag + jnp.concatenate(y_off, axis=1) * e_e)
        st_ref[...] = st * e_e[CHUNK - 1:CHUNK, :] + jnp.concatenate(upd, axis=1)
    y = jnp.concatenate(y_rows, axis=0)
    mixed_ref[...] = _mixer_back(y, xs, z, out_a, dskip_e[...], g_ob[...])

    @pl.when((i % tiles_per_seq) == tiles_per_seq - 1)
    def _():
        conv_out_ref[...] = xp_ref[PROMPT_SPLITS - 1, :, hl + CONV_PAD - (CONV_K - 1):hl + CONV_PAD, :]
        ssm_out_ref[0] = st_ref[...].T


def _sample_mixer_kernel(x_ref, sh_ref, sc_ref, prev_ref, ssm0_ref, *rest):
    front = rest[:N_FRONT]
    cref = rest[N_FRONT:N_FRONT + N_SSD]
    dskip_e, g_ob, selseq = rest[N_FRONT + N_SSD:N_FRONT + N_SSD + 3]
    mixed_ref, v_ref, conv_out_ref, ssm_out_ref = rest[N_FRONT + N_SSD + 3:N_FRONT + N_SSD + 7]
    xp_ref, yoff_ref, cbf_ref, bbf_ref, t1_ref, dtab_ref = rest[N_FRONT + N_SSD + 7:]
    x3 = x_ref[...]
    sb, l, _ = x3.shape
    tm = sb * l
    out_a, v, z, xc, dt, d_a = _mixer_front(x3, sh_ref[...], sc_ref[...], prev_ref[...], front, xp_ref)
    v_ref[...] = v
    conv_out_ref[...] = xp_ref[:, l + CONV_PAD - (CONV_K - 1):l + CONV_PAD, :]
    xs = xc[:, :B_WIDTH]
    bm = xc[:, B_WIDTH:B_WIDTH + SSD_GROUPS * SSD_STATE]
    cm = xc[:, B_WIDTH + SSD_GROUPS * SSD_STATE:]
    y_diag, e_e, xdtd, _ = _ssd_chunk(xs, bm, cm, dt, d_a, cref)

    e_tot = jnp.exp(_dot_exact_l(selseq[...], d_a))
    for hh in range(SSD_HEADS):
        dtab_ref[hh] = jnp.broadcast_to(e_tot[:, hh:hh + 1], (sb, LANES))
    cbf_ref[...] = cm
    bbf_ref[...] = bm
    for g in range(SSD_GROUPS):
        t1_ref[g] = xdtd[:, g * GROUP_W:(g + 1) * GROUP_W].T.astype(BF16)
    seq_of_row = lax.broadcasted_iota(I32, (tm, SSD_STATE), 0) // l
    heads_per_group = SSD_HEADS // SSD_GROUPS

    def one_seq(j):
        r0 = pl.multiple_of(j * l, l)
        s0 = ssm0_ref[j]
        for g in range(SSD_GROUPS):
            ns = slice(g * SSD_STATE, (g + 1) * SSD_STATE)
            s0g = s0[g * heads_per_group:(g + 1) * heads_per_group].reshape(GROUP_W, SSD_STATE)
            cj = cbf_ref[pl.ds(r0, l), ns].astype(BF16)
            yoff_ref[pl.ds(r0, l), g * GROUP_W:(g + 1) * GROUP_W] = _dot_nt(cj, s0g.astype(BF16))
            bmask = jnp.where(seq_of_row == j, bbf_ref[:, ns], 0.0).astype(BF16)
            upd = _dot(t1_ref[g], bmask)
            for hq in range(heads_per_group):
                hh = g * heads_per_group + hq
                dec = dtab_ref[hh, pl.ds(j, 1), :]
                ssm_out_ref[j, hh] = s0[hh] * dec + upd[hq * SSD_HEAD_DIM:(hq + 1) * SSD_HEAD_DIM]

    def body(jj, carry):
        for u in range(SEQ_UNROLL):
            one_seq(jj * SEQ_UNROLL + u)
        return carry

    lax.fori_loop(0, sb // SEQ_UNROLL, body, 0)
    y = y_diag + yoff_ref[...] * e_e
    mixed_ref[...] = _mixer_back(y, xs, z, out_a, dskip_e[...], g_ob[...])


def _ada_spec(ref, sb, div):
    _, term, row0 = ref
    assert row0 % sb == 0
    return pl.BlockSpec((sb, 1, D_MODEL), lambda i: (row0 // sb + i // div, 0, term))


def _const_spec(a):
    nd = a.ndim
    return pl.BlockSpec(a.shape, lambda i, _nd=nd: (0,) * _nd)


def _spatial_consts(w_spatial, b_spatial, cl, tm):
    w = jnp.where(jnp.tril(jnp.ones((cl, cl), bool)), w_spatial[:, :cl, :cl], 0.0)
    eye = jnp.eye(tm // cl, dtype=F32)
    wbd = jnp.einsum("ab,hts->hatbs", eye, w).reshape(A_HEADS, tm, tm).astype(BF16)
    bias = jnp.tile(jnp.repeat(b_spatial[:, :cl].T, A_HEAD_DIM, axis=1), (tm // cl, 1))
    return wbd, bias


def _ssd_consts(cl):
    r = np.arange(CHUNK)
    same = (r[:, None] // cl) == (r[None, :] // cl)
    tril = same & (r[:, None] >= r[None, :])
    expand = np.zeros((LANES, B_WIDTH), np.float32)
    for hh in range(SSD_HEADS):
        expand[hh, hh * SSD_HEAD_DIM:(hh + 1) * SSD_HEAD_DIM] = 1.0
    return (jnp.asarray(tril, BF16), jnp.asarray(same, BF16), jnp.asarray(expand, BF16),
            jnp.asarray(np.where(tril, 0.0, DECAY_MASKED), F32))


def _front_weights(p):
    w_in = p["w_in"]
    c0, c1 = 3 * A_WIDTH, 3 * A_WIDTH + CONV_DIM
    w_dt = w_in[:, c1:]
    pad8 = lambda v: jnp.pad(v, (0, LANES - SSD_HEADS))
    a = -jnp.exp(p["a_log"])
    return dict(
        g_mix=p["g_mix"][None, :],
        w_uvz=w_in[:, :c0].astype(BF16),
        w_xbc=w_in[:, c0:c1].astype(BF16),
        w_dt=jnp.pad(w_dt, ((0, 0), (0, LANES - SSD_HEADS))).astype(BF16),
        g_v=p["g_v_a"][None, :], g_oa=p["g_out_a"][None, :],
        conv_w=p["conv_w"], conv_b=p["conv_b"][None, :],
        dt_bias=pad8(p["dt_bias"])[None, :],
        a_row=pad8(a)[None, :],
        dskip_e=jnp.repeat(p["d_skip"], SSD_HEAD_DIM)[None, :],
        g_ob=p["g_out_b"][None, :],
    )


def _front_list(fw, wbd, bias_sp):
    return [fw["g_mix"], fw["w_uvz"], fw["w_xbc"], fw["w_dt"], wbd, bias_sp, fw["g_v"], fw["g_oa"],
            fw["conv_w"], fw["conv_b"], fw["dt_bias"], fw["a_row"]]


def _prompt_mixer(x, sh, sc, fw, p):
    nseq, lseq, _ = x.shape
    tile = PROMPT_TILE
    tps = lseq // tile
    nt = nseq * tps
    x4 = x.reshape(nt, tile, D_MODEL)
    wbd, bias_sp = _spatial_consts(p["w_spatial"], p["b_spatial"], CHUNK, CHUNK)
    consts = _front_list(fw, wbd, bias_sp) + list(_ssd_consts(CHUNK)) + [fw["dskip_e"], fw["g_ob"]]
    mixed, conv_new, ssm_new = pl.pallas_call(
        functools.partial(_prompt_mixer_kernel, tps),
        out_shape=(jax.ShapeDtypeStruct((nt * tile, D_MODEL), BF16),
                   jax.ShapeDtypeStruct((nseq, CONV_K - 1, CONV_DIM), F32),
                   jax.ShapeDtypeStruct((nseq, B_WIDTH, SSD_STATE), F32)),
        grid=(nt,),
        in_specs=[pl.BlockSpec((1, tile, D_MODEL), lambda i: (i, 0, 0)), _ada_spec(sh, 1, tps), _ada_spec(sc, 1, tps)]
                 + [_const_spec(a) for a in consts],
        out_specs=(pl.BlockSpec((tile, D_MODEL), lambda i: (i, 0)),
                   pl.BlockSpec((1, CONV_K - 1, CONV_DIM), lambda i: (i // tps, 0, 0)),
                   pl.BlockSpec((1, B_WIDTH, SSD_STATE), lambda i: (i // tps, 0, 0))),
        scratch_shapes=[pltpu.VMEM((PROMPT_SPLITS, 1, tile // PROMPT_SPLITS + CONV_PAD, CONV_DIM), F32),
                        pltpu.VMEM((1, CONV_PAD, CONV_DIM), F32),
                        pltpu.VMEM((SSD_STATE, B_WIDTH), F32)],
        compiler_params=pltpu.CompilerParams(dimension_semantics=("arbitrary",), vmem_limit_bytes=VMEM_LIMIT),
        name="prompt_mixer",
    )(x4, sh[0], sc[0], *consts)
    return mixed, conv_new, ssm_new.reshape(nseq, SSD_HEADS, SSD_HEAD_DIM, SSD_STATE)


def _sample_mixer(x, sh, sc, state_ssm, state_conv, fw, p):
    nseq, l, _ = x.shape
    sb = SAMPLE_SEQ_TILE
    tm = sb * l
    assert tm == CHUNK
    wbd, bias_sp = _spatial_consts(p["w_spatial"], p["b_spatial"], l, tm)
    selseq = jnp.asarray((np.arange(tm)[None, :] // l) == np.arange(sb)[:, None], BF16)
    consts = _front_list(fw, wbd, bias_sp) + list(_ssd_consts(l)) + [fw["dskip_e"], fw["g_ob"], selseq]
    prev = jnp.pad(state_conv, ((0, 0), (CONV_PAD - (CONV_K - 1), 0), (0, 0)))
    ssm_spec = pl.BlockSpec((sb, SSD_HEADS, SSD_HEAD_DIM, SSD_STATE), lambda i: (i, 0, 0, 0))
    return pl.pallas_call(
        _sample_mixer_kernel,
        out_shape=(jax.ShapeDtypeStruct((nseq * l, D_MODEL), BF16),
                   jax.ShapeDtypeStruct((nseq * l, A_WIDTH), F32),
                   jax.ShapeDtypeStruct((nseq, CONV_K - 1, CONV_DIM), F32),
                   jax.ShapeDtypeStruct(state_ssm.shape, F32)),
        grid=(nseq // sb,),
        in_specs=[pl.BlockSpec((sb, l, D_MODEL), lambda i: (i, 0, 0)), _ada_spec(sh, sb, 1), _ada_spec(sc, sb, 1),
                  pl.BlockSpec((sb, CONV_PAD, CONV_DIM), lambda i: (i, 0, 0)), ssm_spec]
                 + [_const_spec(a) for a in consts],
        out_specs=(pl.BlockSpec((tm, D_MODEL), lambda i: (i, 0)),
                   pl.BlockSpec((tm, A_WIDTH), lambda i: (i, 0)),
                   pl.BlockSpec((sb, CONV_K - 1, CONV_DIM), lambda i: (i, 0, 0)),
                   ssm_spec),
        scratch_shapes=[pltpu.VMEM((sb, l + CONV_PAD, CONV_DIM), F32),
                        pltpu.VMEM((tm, B_WIDTH), F32),
                        pltpu.VMEM((tm, SSD_GROUPS * SSD_STATE), F32),
                        pltpu.VMEM((tm, SSD_GROUPS * SSD_STATE), F32),
                        pltpu.VMEM((SSD_GROUPS, GROUP_W, tm), BF16),
                        pltpu.VMEM((SSD_HEADS, sb, LANES), F32)],
        compiler_params=pltpu.CompilerParams(dimension_semantics=("arbitrary",), vmem_limit_bytes=VMEM_LIMIT),
        name="sample_mixer",
    )(x, sh[0], sc[0], prev, state_ssm, *consts)


def _post_kernel(mixed_ref, x_ref, gt_ref, sc_ref, sh_ref, w_out_ref, g_ffn_ref, wr_both_ref, br_ref,
                 x1_ref, h2p_ref, ids_ref):
    x3 = x_ref[...]
    sb, l, _ = x3.shape
    tm = sb * l
    mix = _dot(mixed_ref[...], w_out_ref[...]).reshape(sb, l, D_MODEL)
    x1 = x3 + gt_ref[...] * mix
    x1_ref[...] = x1
    xn = x1 * lax.rsqrt(jnp.mean(x1 * x1, axis=-1, keepdims=True) + EPS) * g_ffn_ref[...]
    h2 = (xn * (1.0 + sc_ref[...]) + sh_ref[...]).reshape(tm, D_MODEL)
    h_hi, h_lo = _split(h2)
    h2p_ref[:, :D_MODEL] = h_hi
    both = _dot_nt(wr_both_ref[...], h_hi)
    logits = (both[:N_EXPERTS] + both[N_EXPERTS:] + _dot_nt(wr_both_ref[:N_EXPERTS, :], h_lo)
              + br_ref[...])
    e_iota = lax.broadcasted_iota(I32, logits.shape, 0)
    vals, idxs = [], []
    for _ in range(TOP_K):
        m = jnp.max(logits, axis=0, keepdims=True)
        idx = jnp.min(jnp.where(logits == m, e_iota, N_EXPERTS), axis=0, keepdims=True)
        vals.append(m)
        idxs.append(idx)
        logits = jnp.where(e_iota == idx, -jnp.inf, logits)
    ex = [jnp.exp(v - vals[0]) for v in vals]
    tot = ex[0] + ex[1] + ex[2] + ex[3]
    ids = jnp.concatenate(idxs, axis=0)
    ids_ref[...] = ids
    wts = jnp.concatenate([e / tot for e in ex], axis=0)
    w_hi = wts.astype(BF16).astype(F32)
    info = jnp.concatenate([ids.astype(F32), w_hi, wts - w_hi, jnp.zeros((TOP_K, tm), F32)], axis=0).astype(BF16)
    r = lax.broadcasted_iota(I32, (INFO_ROWS, LANES), 0)
    c = lax.broadcasted_iota(I32, (INFO_ROWS, LANES), 1)
    place = jnp.where(r == c, 1.0, 0.0).astype(BF16)
    h2p_ref[:, D_MODEL:] = lax.dot_general(info, place, (((0,), (0,)), ((), ())),
                                           preferred_element_type=F32).astype(BF16)


def _post(mixed, x, gt, sc, sh, w_out_b, g_ffn, wr_both, br, sb, l, seq_div):
    n3, _, _ = x.shape
    nblk = n3 // sb
    tm = sb * l
    t = n3 * l
    consts = [w_out_b, g_ffn, wr_both, br]
    return pl.pallas_call(
        _post_kernel,
        out_shape=(jax.ShapeDtypeStruct(x.shape, F32),
                   jax.ShapeDtypeStruct((t, ROW_W), BF16),
                   jax.ShapeDtypeStruct((TOP_K, t), I32)),
        grid=(nblk,),
        in_specs=[pl.BlockSpec((tm, D_MODEL), lambda i: (i, 0)),
                  pl.BlockSpec((sb, l, D_MODEL), lambda i: (i, 0, 0)),
                  _ada_spec(gt, sb, seq_div), _ada_spec(sc, sb, seq_div), _ada_spec(sh, sb, seq_div)]
                 + [_const_spec(a) for a in consts],
        out_specs=(pl.BlockSpec((sb, l, D_MODEL), lambda i: (i, 0, 0)),
                   pl.BlockSpec((tm, ROW_W), lambda i: (i, 0)),
                   pl.BlockSpec((TOP_K, tm), lambda i: (0, i))),
        compiler_params=pltpu.CompilerParams(dimension_semantics=("arbitrary",), vmem_limit_bytes=VMEM_LIMIT),
        name="post",
    )(mixed, x, gt[0], sc[0], sh[0], *consts)


def _strict_upper(n):
    r = lax.broadcasted_iota(I32, (n, n), 0)
    c = lax.broadcasted_iota(I32, (n, n), 1)
    return jnp.where(r < c, 1.0, 0.0).astype(BF16)


def _expert_prefix(col):
    r = lax.broadcasted_iota(I32, (N_EXPERTS, N_EXPERTS), 0)
    c = lax.broadcasted_iota(I32, (N_EXPERTS, N_EXPERTS), 1)
    as_row = jnp.sum(jnp.where(r == c, col, 0.0), axis=0, keepdims=True)
    return jnp.sum(jnp.where(c < r, as_row, 0.0), axis=1, keepdims=True)


def _plan_kernel(dump_group, ids_ref, lr_ref, gdst_ref, tile_e_ref):
    tm = TOK_TILE
    nb = ids_ref.shape[1] // tm
    blk_lane = lax.broadcasted_iota(I32, (N_EXPERTS, LANES), 1)

    def block_masks(b):
        ids = ids_ref[:, pl.ds(pl.multiple_of(b * tm, tm), tm)]
        e_iota = lax.broadcasted_iota(I32, (N_EXPERTS, tm), 0)
        onehot = [ids[k:k + 1, :] == e_iota for k in range(TOP_K)]
        sel = (onehot[0] | onehot[1]) | (onehot[2] | onehot[3])
        return onehot, jnp.where(sel, 1.0, 0.0)

    def count(b, seg):
        _, m = block_masks(b)
        seg_b = jnp.ceil(jnp.sum(m, axis=1, keepdims=True) * (1.0 / ROW_GROUP))
        return jnp.where(blk_lane == b, seg_b, seg)

    seg = lax.fori_loop(0, nb, count, jnp.zeros((N_EXPERTS, LANES), F32))
    tot = jnp.sum(seg, axis=1, keepdims=True)
    padded = jnp.ceil(tot * (1.0 / TILE_GROUPS)) * TILE_GROUPS
    gstart = _expert_prefix(padded)
    gb = gstart + _dot(seg.astype(BF16), _strict_upper(LANES))
    r = lax.broadcasted_iota(I32, (N_EXPERTS, LANES), 0)
    as_row = lambda col: jnp.sum(jnp.where(r == blk_lane, col, 0.0), axis=0, keepdims=True)
    n_used = jnp.sum(padded, axis=0, keepdims=True)
    lane = lax.broadcasted_iota(I32, (1, LANES), 1)
    tiles = jnp.where(lane == N_EXPERTS, n_used, as_row(gstart)) * (1.0 / TILE_GROUPS)
    meta = jnp.concatenate([tiles, as_row(gstart + tot), as_row(padded - tot),
                            jnp.zeros((META_ROWS - 3, LANES), F32)], axis=0)
    tile_e_ref[...] = meta.astype(I32)
    upper = _strict_upper(tm)

    def place(b, carry):
        onehot, m = block_masks(b)
        seg_b = jnp.sum(jnp.where(blk_lane == b, seg, 0.0), axis=1, keepdims=True)
        gb_b = jnp.sum(jnp.where(blk_lane == b, gb, 0.0), axis=1, keepdims=True)
        loc_b = _expert_prefix(seg_b)
        before = _dot(m.astype(BF16), upper) + loc_b * ROW_GROUP
        lr_ref[:, pl.ds(pl.multiple_of(b * tm, tm), tm)] = jnp.concatenate(
            [jnp.sum(jnp.where(onehot[k], before, 0.0), axis=0, keepdims=True) for k in range(TOP_K)],
            axis=0).astype(I32)
        g = lax.broadcasted_iota(I32, (N_EXPERTS, GDST_LANES), 1).astype(F32)
        inside = (loc_b <= g) & (g < loc_b + seg_b)
        dst = jnp.sum(jnp.where(inside, gb_b + g - loc_b, 0.0), axis=0, keepdims=True)
        used = jnp.sum(jnp.where(inside, 1.0, 0.0), axis=0, keepdims=True) > 0.5
        dump = dump_group + lax.convert_element_type(b % 2, F32) * LOCAL_GROUPS + g[0:1, :]
        gdst_ref[b] = jnp.where(used, dst, dump).astype(I32)
        return carry

    lax.fori_loop(0, nb, place, 0)


def _plan(ids, dump_group):
    t = ids.shape[1]
    nb = t // TOK_TILE
    assert nb <= LANES
    return pl.pallas_call(
        functools.partial(_plan_kernel, float(dump_group)),
        out_shape=(jax.ShapeDtypeStruct((TOP_K, t), I32),
                   jax.ShapeDtypeStruct((nb, 1, GDST_LANES), I32),
                   jax.ShapeDtypeStruct((META_ROWS, LANES), I32)),
        name="plan",
    )(ids)


def _sort_matrix(lr, c):
    r_iota = lax.broadcasted_iota(I32, (MASK_ROWS, lr.shape[1]), 0) + c * MASK_ROWS
    p = jnp.where(r_iota == lr[TOP_K - 1:TOP_K, :], 1.0, 0.0)
    for k in range(TOP_K - 1):
        p = jnp.where(r_iota == lr[k:k + 1, :], 1.0, p)
    return p.astype(BF16)


def _group_copies(loc_ref, slot, far_ref, gdst_ref, sem, to_far):
    copies = []
    for g in range(LOCAL_GROUPS):
        dst = pl.multiple_of(gdst_ref[0, 0, g] * ROW_GROUP, ROW_GROUP)
        near = loc_ref.at[slot, pl.ds(g * ROW_GROUP, ROW_GROUP)]
        far = far_ref.at[pl.ds(dst, ROW_GROUP)]
        copies.append(pltpu.make_async_copy(near, far, sem.at[slot]) if to_far
                      else pltpu.make_async_copy(far, near, sem.at[slot]))
    return copies


def _wait_groups(loc_ref, slot, sem):
    pltpu.make_async_copy(loc_ref.at[slot], loc_ref.at[slot], sem.at[slot]).wait()


def _dispatch_kernel(n_first, n_tiles_all, gdst_ref, meta_ref, lr_ref, ha_ref, hb_ref, xs_ref,
                     h_ref, loc_ref, zero_ref, sem, zsem):
    i = pl.program_id(0)
    last = pl.num_programs(0) - 1
    slot = i % 2

    @pl.when(i < n_first)
    def _():
        h_ref[...] = ha_ref[...]

    @pl.when(i >= n_first)
    def _():
        h_ref[...] = hb_ref[...]

    lr = lr_ref[...]
    copies = _group_copies(loc_ref, slot, xs_ref, gdst_ref, sem, True)
    n_chunks = LOCAL_ROWS // MASK_ROWS
    per_chunk = LOCAL_GROUPS // n_chunks
    for c in range(n_chunks):
        loc_ref[slot, c * MASK_ROWS:(c + 1) * MASK_ROWS, :] = _dot(_sort_matrix(lr, c), h_ref[...]).astype(BF16)
        for cp in copies[c * per_chunk:(c + 1) * per_chunk]:
            cp.start()

    @pl.when(i > 0)
    def _():
        _wait_groups(loc_ref, 1 - slot, sem)

    @pl.when(i == last)
    def _():
        _wait_groups(loc_ref, slot, sem)
        zero_ref[...] = jnp.zeros_like(zero_ref)

        def pad_copy(e, j):
            row = pl.multiple_of((meta_ref[1, e] + j) * ROW_GROUP, ROW_GROUP)
            return pltpu.make_async_copy(zero_ref.at[pl.ds(0, ROW_GROUP)], xs_ref.at[pl.ds(row, ROW_GROUP)], zsem)

        def tile_copy(t):
            row = pl.multiple_of(t * ROW_TILE, ROW_TILE)
            return pltpu.make_async_copy(zero_ref, xs_ref.at[pl.ds(row, ROW_TILE)], zsem)

        def pads(fn):
            def body(e, carry):
                for j in range(TILE_GROUPS - 1):
                    @pl.when(j < meta_ref[2, e])
                    def _():
                        fn(pad_copy(e, j))
                return carry
            lax.fori_loop(0, N_EXPERTS, body, 0)

        def tiles(fn):
            def body(t, carry):
                fn(tile_copy(t))
                return carry
            lax.fori_loop(meta_ref[0, N_EXPERTS], n_tiles_all, body, 0)

        pads(lambda c: c.start())
        tiles(lambda c: c.start())
        pads(lambda c: c.wait())
        tiles(lambda c: c.wait())


def _dispatch(gdst, meta, lr, h_a, h_b, n_tiles_all):
    tm = TOK_TILE
    na, nb2 = h_a.shape[0] // tm, h_b.shape[0] // tm
    return pl.pallas_call(
        functools.partial(_dispatch_kernel, na, n_tiles_all),
        out_shape=jax.ShapeDtypeStruct((n_tiles_all * ROW_TILE, ROW_W), BF16),
        grid=(na + nb2,),
        in_specs=[pl.BlockSpec((1, 1, GDST_LANES), lambda i: (i, 0, 0), memory_space=pltpu.SMEM),
                  pl.BlockSpec((META_ROWS, LANES), lambda i: (0, 0), memory_space=pltpu.SMEM),
                  pl.BlockSpec((TOP_K, tm), lambda i: (0, i)),
                  pl.BlockSpec((tm, ROW_W), lambda i: (jnp.minimum(i, na - 1), 0)),
                  pl.BlockSpec((tm, ROW_W), lambda i: (jnp.maximum(i - na, 0), 0))],
        out_specs=pl.BlockSpec(memory_space=pl.ANY),
        scratch_shapes=[pltpu.VMEM((tm, ROW_W), BF16), pltpu.VMEM((2, LOCAL_ROWS, ROW_W), BF16),
                        pltpu.VMEM((ROW_TILE, ROW_W), BF16),
                        pltpu.SemaphoreType.DMA((2,)), pltpu.SemaphoreType.DMA],
        compiler_params=pltpu.CompilerParams(dimension_semantics=("arbitrary",), vmem_limit_bytes=VMEM_LIMIT),
        name="dispatch",
    )(gdst, meta, lr, h_a, h_b)


def _expert_kernel(n_tiles_all, ts_ref, xs_ref, wg_ref, wu_ref, wd_ref, bg_ref, bu_ref, bd_ref, y_ref,
                   wstage, wgb, wub, wdb, xbuf, ybuf, sem_w, sem_in, sem_out):
    e = pl.program_id(0)
    t0 = ts_ref[e]
    nt = ts_ref[e + 1] - t0
    wslot = e % 2

    def w_copies(ex, slot):
        return [pltpu.make_async_copy(w_ref.at[ex], wstage.at[slot, j], sem_w.at[slot])
                for j, w_ref in enumerate((wg_ref, wu_ref, wd_ref))]

    def in_copy(t, slot):
        rows = pl.ds(pl.multiple_of((t0 + t) * ROW_TILE, ROW_TILE), ROW_TILE)
        return pltpu.make_async_copy(xs_ref.at[rows], xbuf.at[slot], sem_in.at[slot])

    def out_copy(tile, slot):
        rows = pl.ds(pl.multiple_of(tile * ROW_TILE, ROW_TILE), ROW_TILE)
        return pltpu.make_async_copy(ybuf.at[slot], y_ref.at[rows], sem_out.at[slot])

    @pl.when((e == 0) & (nt > 0))
    def _():
        for c in w_copies(0, 0):
            c.start()

    for j in range(TILE_SLOTS):
        @pl.when(nt > j)
        def _():
            in_copy(j, j).start()

    @pl.when(e + 1 < N_EXPERTS)
    def _():
        @pl.when(ts_ref[e + 2] > ts_ref[e + 1])
        def _():
            for c in w_copies(e + 1, 1 - wslot):
                c.start(priority=1)

    @pl.when(nt > 0)
    def _():
        for c in w_copies(e, wslot):
            c.wait()
        wgb[...] = wstage[wslot, 0].astype(BF16)
        wub[...] = wstage[wslot, 1].astype(BF16)
        wdb[...] = wstage[wslot, 2].astype(BF16)
        e_f = e.astype(F32)

        def acquire(t):
            slot = t % TILE_SLOTS
            in_copy(t, slot).wait()

            @pl.when(t >= TILE_SLOTS)
            def _():
                out_copy(t0 + t - TILE_SLOTS, slot).wait()

        def compute(t):
            slot = t % TILE_SLOTS
            xw = xbuf[slot]
            x = xw[:, :D_MODEL]
            info = xw[:, D_MODEL:].astype(F32)
            w_row = jnp.zeros((ROW_TILE, 1), F32)
            for k in range(TOP_K):
                wk = info[:, TOP_K + k:TOP_K + k + 1] + info[:, 2 * TOP_K + k:2 * TOP_K + k + 1]
                w_row = w_row + jnp.where(info[:, k:k + 1] == e_f, wk, 0.0)
            g = jnp.minimum(_dot(x, wgb[...]) + bg_ref[0], SWIGLU_LIMIT)
            u = jnp.clip(_dot(x, wub[...]) + bu_ref[0], -SWIGLU_LIMIT, SWIGLU_LIMIT)
            act = g * jax.nn.sigmoid(SWIGLU_ALPHA * g) * (u + 1.0)
            ybuf[slot] = ((_dot(act.astype(BF16), wdb[...]) + bd_ref[0]) * w_row).astype(BF16)

        def release(t):
            slot = t % TILE_SLOTS
            out_copy(t0 + t, slot).start()

            @pl.when(t + TILE_SLOTS < nt)
            def _():
                in_copy(t + TILE_SLOTS, slot).start()

        def pair(p, carry):
            ta, tb = 2 * p, 2 * p + 1
            acquire(ta)
            acquire(tb)
            compute(ta)
            compute(tb)
            release(ta)
            release(tb)
            return carry

        lax.fori_loop(0, nt // 2, pair, 0)

        @pl.when(nt % 2 == 1)
        def _():
            acquire(nt - 1)
            compute(nt - 1)
            release(nt - 1)

        for j in range(1, TILE_SLOTS + 1):
            @pl.when(nt >= j)
            def _():
                out_copy(t0 + nt - j, (nt - j) % TILE_SLOTS).wait()

    @pl.when(e == pl.num_programs(0) - 1)
    def _():
        n_used = ts_ref[N_EXPERTS]
        ybuf[0] = jnp.zeros((ROW_TILE, D_MODEL), BF16)

        def zstart(tile, carry):
            out_copy(tile, 0).start()
            return carry

        def zwait(tile, carry):
            out_copy(tile, 0).wait()
            return carry

        lax.fori_loop(n_used, n_tiles_all, zstart, 0)
        lax.fori_loop(n_used, n_tiles_all, zwait, 0)


def _experts(tile_start, xs, w_gate, b_gate, w_up, b_up, w_down, b_down):
    n_rows = xs.shape[0]
    b_spec = pl.BlockSpec((1, 1, D_MODEL), lambda e, ts: (e, 0, 0))
    any_spec = pl.BlockSpec(memory_space=pl.ANY)
    return pl.pallas_call(
        functools.partial(_expert_kernel, n_rows // ROW_TILE),
        out_shape=jax.ShapeDtypeStruct((n_rows, D_MODEL), BF16),
        grid_spec=pltpu.PrefetchScalarGridSpec(
            num_scalar_prefetch=1,
            grid=(N_EXPERTS,),
            in_specs=[any_spec, any_spec, any_spec, any_spec, b_spec, b_spec, b_spec],
            out_specs=any_spec,
            scratch_shapes=[pltpu.VMEM((2, 3, D_MODEL, D_MODEL), F32)]
                           + [pltpu.VMEM((D_MODEL, D_MODEL), BF16)] * 3
                           + [pltpu.VMEM((TILE_SLOTS, ROW_TILE, ROW_W), BF16),
                              pltpu.VMEM((TILE_SLOTS, ROW_TILE, D_MODEL), BF16),
                              pltpu.SemaphoreType.DMA((2,)), pltpu.SemaphoreType.DMA((TILE_SLOTS,)),
                              pltpu.SemaphoreType.DMA((TILE_SLOTS,))],
        ),
        compiler_params=pltpu.CompilerParams(dimension_semantics=("arbitrary",), vmem_limit_bytes=VMEM_LIMIT),
        name="experts",
    )(tile_start, xs, w_gate, w_up, w_down, b_gate[:, None, :], b_up[:, None, :], b_down[:, None, :])


def _combine_kernel(gdst_ref, gdst_next_ref, lr_ref, y_ref, x1_ref, gt_ref, gf_ref, out_ref, loc_ref, sem):
    i = pl.program_id(0)
    slot = i % 2
    x1 = x1_ref[...]
    sb, l, _ = x1.shape

    @pl.when(i == 0)
    def _():
        for c in _group_copies(loc_ref, slot, y_ref, gdst_ref, sem, False):
            c.start()

    _wait_groups(loc_ref, slot, sem)

    prefetch = _group_copies(loc_ref, 1 - slot, y_ref, gdst_next_ref, sem, False)
    n_chunks = LOCAL_ROWS // MASK_ROWS
    per_chunk = LOCAL_GROUPS // n_chunks
    lr = lr_ref[...]
    moe = jnp.zeros((sb * l, D_MODEL), F32)
    for c in range(n_chunks):
        rows = loc_ref[slot, c * MASK_ROWS:(c + 1) * MASK_ROWS, :]
        moe = moe + lax.dot_general(_sort_matrix(lr, c), rows, (((0,), (0,)), ((), ())),
                                    preferred_element_type=F32)
        for cp in prefetch[c * per_chunk:(c + 1) * per_chunk]:
            cp.start()
    x2 = x1 + gt_ref[...] * moe.reshape(sb, l, D_MODEL)
    out_ref[...] = x2 * lax.rsqrt(jnp.mean(x2 * x2, axis=-1, keepdims=True) + EPS) * gf_ref[...]

    @pl.when(i == pl.num_programs(0) - 1)
    def _():
        _wait_groups(loc_ref, 1 - slot, sem)


def _combine(gdst, lr, y, x1, gt, g_final, sb, l, seq_div, blk_off):
    n3 = x1.shape[0]
    nblk = n3 // sb
    tm = sb * l
    assert tm == TOK_TILE
    return pl.pallas_call(
        _combine_kernel,
        out_shape=jax.ShapeDtypeStruct(x1.shape, F32),
        grid=(nblk,),
        in_specs=[pl.BlockSpec((1, 1, GDST_LANES), lambda i: (i + blk_off, 0, 0), memory_space=pltpu.SMEM),
                  pl.BlockSpec((1, 1, GDST_LANES), lambda i: (jnp.minimum(i + 1, nblk - 1) + blk_off, 0, 0),
                               memory_space=pltpu.SMEM),
                  pl.BlockSpec((TOP_K, tm), lambda i: (0, i + blk_off)),
                  pl.BlockSpec(memory_space=pl.ANY),
                  pl.BlockSpec((sb, l, D_MODEL), lambda i: (i, 0, 0)),
                  _ada_spec(gt, sb, seq_div),
                  pl.BlockSpec((1, D_MODEL), lambda i: (0, 0))],
        out_specs=pl.BlockSpec((sb, l, D_MODEL), lambda i: (i, 0, 0)),
        scratch_shapes=[pltpu.VMEM((2, LOCAL_ROWS, D_MODEL), BF16), pltpu.SemaphoreType.DMA((2,))],
        compiler_params=pltpu.CompilerParams(dimension_semantics=("arbitrary",), vmem_limit_bytes=VMEM_LIMIT),
        name="combine",
    )(gdst, gdst, lr, y, x1, gt[0], g_final)


def kernel(x_prompt, x_sample, c_prompt, c_sample, state_ssm, state_conv, w_ada, b_ada, g_mix, w_in, g_v_a, w_spatial, b_spatial, g_out_a, conv_w, conv_b, dt_bias, a_log, d_skip, g_out_b, w_out, g_ffn, w_router, b_router, w_gate, b_gate, w_up, b_up, w_down, b_down, g_final):
    assert w_ada.shape[0] == 1, "single-layer step"
    p = dict(w_in=w_in[0], g_mix=g_mix[0], g_v_a=g_v_a[0], w_spatial=w_spatial[0], b_spatial=b_spatial[0],
             g_out_a=g_out_a[0], conv_w=conv_w[0], conv_b=conv_b[0], dt_bias=dt_bias[0], a_log=a_log[0],
             d_skip=d_skip[0], g_out_b=g_out_b[0])
    bp, lp, _ = x_prompt.shape
    bs, ls, _ = x_sample.shape
    tp, ts = bp * lp, bs * ls

    ada = _ada(jnp.concatenate([c_sample, c_prompt], axis=0), w_ada[0], b_ada[0][None, :])
    ada = ada.reshape(bs + bp, 1, 6 * D_MODEL)
    ada_s = [(ada, j, 0) for j in range(6)]
    ada_p = [(ada, j, bs) for j in range(6)]

    fw = _front_weights(p)
    mixed_p, conv_p, ssm_p = _prompt_mixer(x_prompt, ada_p[0], ada_p[1], fw, p)
    mixed_s, v_s, conv_s, ssm_s = _sample_mixer(x_sample, ada_s[0], ada_s[1], state_ssm[0], state_conv[0], fw, p)

    w_out_b = w_out[0].astype(BF16)
    g_ffn2 = g_ffn[0][None, :]
    wr_t = w_router[0].T
    wr_hi = wr_t.astype(BF16)
    wr_both = jnp.concatenate([wr_hi, (wr_t - wr_hi.astype(F32)).astype(BF16)], axis=0)
    br = b_router[0][:, None]
    tps = lp // TOK_TILE
    sbs = TOK_TILE // ls
    xp3 = x_prompt.reshape(tp // POST_TILE, POST_TILE, D_MODEL)
    x1_p, h2p_p, ids_p = _post(mixed_p, xp3, ada_p[2], ada_p[4], ada_p[3], w_out_b, g_ffn2, wr_both, br,
                               1, POST_TILE, lp // POST_TILE)
    x1_p = x1_p.reshape(bp * tps, TOK_TILE, D_MODEL)
    x1_s, h2p_s, ids_s = _post(mixed_s, x_sample, ada_s[2], ada_s[4], ada_s[3], w_out_b, g_ffn2, wr_both, br,
                               POST_TILE // ls, ls, 1)

    n_blocks = (tp + ts) // TOK_TILE
    max_groups = (tp + ts) * TOP_K // ROW_GROUP + n_blocks * N_EXPERTS + N_EXPERTS * (TILE_GROUPS - 1)
    n_tiles = -(-max_groups // TILE_GROUPS)
    n_tiles_all = n_tiles + 2 * LOCAL_GROUPS // TILE_GROUPS
    lr, gdst, meta = _plan(jnp.concatenate([ids_p, ids_s], axis=1), n_tiles * TILE_GROUPS)

    xs = _dispatch(gdst, meta, lr, h2p_p, h2p_s, n_tiles_all)
    y = _experts(meta[0], xs, w_gate[0], b_gate[0], w_up[0], b_up[0], w_down[0], b_down[0])

    gf = g_final[None, :]
    y_p = _combine(gdst, lr, y, x1_p, ada_p[5], gf, 1, TOK_TILE, tps, 0).reshape(bp, lp, D_MODEL)
    y_s = _combine(gdst, lr, y, x1_s, ada_s[5], gf, sbs, ls, 1, tp // TOK_TILE)

    return (y_p, y_s, ssm_p[None], conv_p[None], ssm_s[None], conv_s[None], v_s.reshape(1, bs, ls, A_WIDTH))
```

```python
import functools
import math

import numpy as np
import jax
import jax.numpy as jnp
from jax import lax
from jax.experimental import pallas as pl
from jax.experimental.pallas import tpu as pltpu

F32 = jnp.float32
BF16 = jnp.bfloat16
I32 = jnp.int32

D_MODEL = 1024
A_WIDTH = 512
A_HEADS = 4
A_HEAD_DIM = 128
CHUNK = 128
B_WIDTH = 512
SSD_HEAD_DIM = 64
SSD_HEADS = 8
SSD_GROUPS = 2
SSD_STATE = 128
GROUP_W = B_WIDTH // SSD_GROUPS
CONV_K = 4
CONV_DIM = 1024
CONV_PAD = 8
N_EXPERTS = 32
TOP_K = 4
SWIGLU_LIMIT = 7.0
SWIGLU_ALPHA = 1.702
EPS = 1e-6
DECAY_MASKED = -1e30
LANES = 128

TOK_TILE = 512
SAMPLE_SEQ_TILE = 16
SEQ_UNROLL = 8
ROW_TILE = 256
ROW_GROUP = 16
TILE_GROUPS = ROW_TILE // ROW_GROUP
LOCAL_GROUPS = TOK_TILE * TOP_K // ROW_GROUP + N_EXPERTS
LOCAL_ROWS = LOCAL_GROUPS * ROW_GROUP
GDST_LANES = -(-LOCAL_GROUPS // LANES) * LANES
ROW_W = D_MODEL + LANES
INFO_ROWS = 16
MASK_ROWS = 256
META_ROWS = 8
POST_TILE = 1024
PROMPT_TILE = 512
PROMPT_SPLITS = 2
TILE_GROUP = 4
TILE_SLOTS = 2 * TILE_GROUP
VMEM_LIMIT = 56 * 1024 * 1024


def _dot(a, b):
    return jnp.dot(a, b, preferred_element_type=F32)


def _dot_nt(a, b):
    return lax.dot_general(a, b, (((1,), (1,)), ((), ())), preferred_element_type=F32)


def _split(x):
    hi = x.astype(BF16)
    lo = (x - hi.astype(F32)).astype(BF16)
    return hi, lo


def _dot_exact_l(t, x):
    hi, lo = _split(x)
    return _dot(t, hi) + _dot(t, lo)


def _dot_exact_r(x, t):
    hi, lo = _split(x)
    return _dot(hi, t) + _dot(lo, t)


def _silu(x):
    return x * jax.nn.sigmoid(x)


def _gelu(x):
    return 0.5 * x * (1.0 + lax.erf(x * (1.0 / math.sqrt(2.0))))


def _softplus(x):
    return jnp.maximum(x, 0.0) + jnp.log1p(jnp.exp(-jnp.abs(x)))


def _flat(a):
    return a.reshape(a.shape[1], a.shape[2]) if a.shape[0] == 1 else a


def _rms(x, g):
    return x * lax.rsqrt(jnp.mean(x * x, axis=-1, keepdims=True) + EPS) * g


def _ada_kernel(c_ref, w_ref, b_ref, o_ref):
    s_hi, s_lo = _split(_silu(c_ref[...]))
    w_hi, w_lo = _split(w_ref[...])
    o_ref[...] = _dot(s_hi, w_hi) + _dot(s_lo, w_hi) + _dot(s_hi, w_lo) + b_ref[...]


def _ada(c_all, w_ada, b_ada):
    m = c_all.shape[0]
    n = w_ada.shape[1]
    bn = 1024
    return pl.pallas_call(
        _ada_kernel,
        out_shape=jax.ShapeDtypeStruct((m, n), F32),
        grid=(n // bn,),
        in_specs=[pl.BlockSpec((m, D_MODEL), lambda j: (0, 0)),
                  pl.BlockSpec((D_MODEL, bn), lambda j: (0, j)),
                  pl.BlockSpec((1, bn), lambda j: (0, j))],
        out_specs=pl.BlockSpec((m, bn), lambda j: (0, j)),
        compiler_params=pltpu.CompilerParams(dimension_semantics=("arbitrary",), vmem_limit_bytes=VMEM_LIMIT),
        name="ada",
    )(c_all, w_ada, b_ada)


def _mixer_front(x3, sh, sc, prev, refs, xp_ref):
    (g_mix, w_uvz, w_xbc, w_dt, wbd, bias_sp, g_v, g_oa, conv_w, conv_b, dt_bias, a_row) = refs
    sb, l, _ = x3.shape
    tm = sb * l
    h = (_rms(_flat(x3), g_mix[...]) * (1.0 + _flat(sc)) + _flat(sh)).reshape(tm, D_MODEL)
    hb = h.astype(BF16)
    uvz = _dot(hb, w_uvz[...])
    xbc = _dot(hb, w_xbc[...])
    dt_raw = _dot(hb, w_dt[...])

    u = _gelu(uvz[:, :A_WIDTH])
    vg = _gelu(uvz[:, A_WIDTH:2 * A_WIDTH])
    z = uvz[:, 2 * A_WIDTH:]
    v_parts, s_parts = [], []
    for hd in range(A_HEADS):
        sl = slice(hd * A_HEAD_DIM, (hd + 1) * A_HEAD_DIM)
        vh = _rms(vg[:, sl], g_v[:, sl])
        v_parts.append(vh)
        vb = vh.astype(BF16)
        s_parts.append(jnp.concatenate([_dot(wbd[hd], vb[r0:r0 + CHUNK]) + bias_sp[:, sl]
                                        for r0 in range(0, tm, CHUNK)], axis=0))
    v = jnp.concatenate(v_parts, axis=1)
    s_a = jnp.concatenate(s_parts, axis=1)
    out_a = _rms(u * s_a, g_oa[...])

    xp_ref[:, 0:CONV_PAD, :] = prev
    xp_ref[:, CONV_PAD:, :] = xbc.reshape(sb, l, CONV_DIM)
    xp = xp_ref[...]
    acc = conv_b[...] + xp[:, CONV_PAD:, :] * conv_w[CONV_K - 1:CONV_K, :]
    for s in range(1, CONV_K):
        back = pltpu.roll(xp, s, axis=1)[:, CONV_PAD:, :]
        acc = acc + back * conv_w[CONV_K - 1 - s:CONV_K - s, :]
    xc = _silu(acc).reshape(tm, CONV_DIM)
    dt = _softplus(dt_raw + dt_bias[...])
    d_a = dt * a_row[...]
    return out_a, v, z, xc, dt, d_a


def _ssd_chunk(xs, bm, cm, dt, d_a, cref):
    tril, ones, expand, neg_mask = cref
    cs = _dot_exact_l(tril[...], d_a)
    cs_t = cs.T
    dt_t = dt.T
    cs_tot = _dot_exact_l(ones[...], d_a)
    vals = jnp.concatenate([dt * jnp.exp(cs_tot - cs), jnp.exp(cs)], axis=0)
    vals_e = _dot_exact_r(vals, expand[...])
    n = xs.shape[0]
    w_e, e_e = vals_e[:n], vals_e[n:]
    xdtd = xs * w_e
    neg = neg_mask[...]
    row_lt_half = lax.broadcasted_iota(I32, (2 * n, LANES), 0) < n
    lane_lt_half = lax.broadcasted_iota(I32, (2 * n, LANES), 1) < SSD_HEAD_DIM
    y_parts = []
    for g in range(SSD_GROUPS):
        cb = _dot_nt(cm[:, g * SSD_STATE:(g + 1) * SSD_STATE].astype(BF16),
                     bm[:, g * SSD_STATE:(g + 1) * SSD_STATE].astype(BF16))
        for hp in range(SSD_HEADS // SSD_GROUPS // 2):
            h0 = g * (SSD_HEADS // SSD_GROUPS) + 2 * hp
            ms = []
            for hh in (h0, h0 + 1):
                diff = cs[:, hh:hh + 1] - cs_t[hh:hh + 1, :]
                ms.append((cb * jnp.exp(diff + neg) * dt_t[hh:hh + 1, :]).astype(BF16))
            pair = xs[:, h0 * SSD_HEAD_DIM:(h0 + 2) * SSD_HEAD_DIM]
            rhs = jnp.where(row_lt_half == lane_lt_half, jnp.concatenate([pair, pair], axis=0), 0.0).astype(BF16)
            y_parts.append(_dot(jnp.concatenate(ms, axis=1), rhs))
    y_diag = jnp.concatenate(y_parts, axis=1)
    return y_diag, e_e, xdtd, cs_tot


def _mixer_back(y, xs, z, out_a, dskip_e, g_ob):
    y = y + xs * dskip_e
    gated = y * _silu(z)
    parts = [_rms(gated[:, g * GROUP_W:(g + 1) * GROUP_W], g_ob[:, g * GROUP_W:(g + 1) * GROUP_W])
             for g in range(SSD_GROUPS)]
    return jnp.concatenate([out_a] + parts, axis=1).astype(BF16)


N_FRONT = 12
N_SSD = 4


def _prompt_mixer_kernel(tiles_per_seq, x_ref, sh_ref, sc_ref, *rest):
    front = rest[:N_FRONT]
    cref = rest[N_FRONT:N_FRONT + N_SSD]
    dskip_e, g_ob = rest[N_FRONT + N_SSD:N_FRONT + N_SSD + 2]
    mixed_ref, conv_out_ref, ssm_out_ref = rest[N_FRONT + N_SSD + 2:N_FRONT + N_SSD + 5]
    xp_ref, carry_ref, st_ref = rest[N_FRONT + N_SSD + 5:]
    i = pl.program_id(0)
    first = (i % tiles_per_seq) == 0

    @pl.when(first)
    def _():
        carry_ref[...] = jnp.zeros_like(carry_ref)
        st_ref[...] = jnp.zeros_like(st_ref)

    l = x_ref.shape[1]
    hl = l // PROMPT_SPLITS
    parts = []
    prev = carry_ref[...]
    for hh in range(PROMPT_SPLITS):
        xp_h = xp_ref.at[hh]
        parts.append(_mixer_front(x_ref[:, hh * hl:(hh + 1) * hl, :], sh_ref[...], sc_ref[...], prev, front, xp_h))
        prev = xp_h[:, hl:hl + CONV_PAD, :]
    carry_ref[...] = prev
    out_a, _, z, xc, dt, d_a = [jnp.concatenate([p[j] for p in parts], axis=0) for j in range(6)]
    xs = xc[:, :B_WIDTH]
    y_rows = []
    for c in range(l // CHUNK):
        r = slice(c * CHUNK, (c + 1) * CHUNK)
        bm = xc[r, B_WIDTH:B_WIDTH + SSD_GROUPS * SSD_STATE]
        cm = xc[r, B_WIDTH + SSD_GROUPS * SSD_STATE:]
        y_diag, e_e, xdtd, _ = _ssd_chunk(xs[r], bm, cm, dt[r], d_a[r], cref)
        st = st_ref[...]
        y_off, upd = [], []
        for g in range(SSD_GROUPS):
            gs = slice(g * GROUP_W, (g + 1) * GROUP_W)
            ns = slice(g * SSD_STATE, (g + 1) * SSD_STATE)
            y_off.append(_dot(cm[:, ns].astype(BF16), st[:, gs].astype(BF16)))
            upd.append(_dot(bm[:, ns].T.astype(BF16), xdtd[:, gs].astype(BF16)))
        y_rows.append(y_diag + jnp.concatenate(y_off, axis=1) * e_e)
        st_ref[...] = st * e_e[CHUNK - 1:CHUNK, :] + jnp.concatenate(upd, axis=1)
    y = jnp.concatenate(y_rows, axis=0)
    mixed_ref[...] = _mixer_back(y, xs, z, out_a, dskip_e[...], g_ob[...])

    @pl.when((i % tiles_per_seq) == tiles_per_seq - 1)
    def _():
        conv_out_ref[...] = xp_ref[PROMPT_SPLITS - 1, :, hl + CONV_PAD - (CONV_K - 1):hl + CONV_PAD, :]
        ssm_out_ref[0] = st_ref[...].T


def _sample_mixer_kernel(x_ref, sh_ref, sc_ref, prev_ref, ssm0_ref, *rest):
    front = rest[:N_FRONT]
    cref = rest[N_FRONT:N_FRONT + N_SSD]
    dskip_e, g_ob, selseq = rest[N_FRONT + N_SSD:N_FRONT + N_SSD + 3]
    mixed_ref, v_ref, conv_out_ref, ssm_out_ref = rest[N_FRONT + N_SSD + 3:N_FRONT + N_SSD + 7]
    xp_ref, yoff_ref, cbf_ref, bbf_ref, t1_ref, dtab_ref = rest[N_FRONT + N_SSD + 7:]
    x3 = x_ref[...]
    sb, l, _ = x3.shape
    tm = sb * l
    out_a, v, z, xc, dt, d_a = _mixer_front(x3, sh_ref[...], sc_ref[...], prev_ref[...], front, xp_ref)
    v_ref[...] = v
    conv_out_ref[...] = xp_ref[:, l + CONV_PAD - (CONV_K - 1):l + CONV_PAD, :]
    xs = xc[:, :B_WIDTH]
    bm = xc[:, B_WIDTH:B_WIDTH + SSD_GROUPS * SSD_STATE]
    cm = xc[:, B_WIDTH + SSD_GROUPS * SSD_STATE:]
    y_diag, e_e, xdtd, _ = _ssd_chunk(xs, bm, cm, dt, d_a, cref)

    e_tot = jnp.exp(_dot_exact_l(selseq[...], d_a))
    for hh in range(SSD_HEADS):
        dtab_ref[hh] = jnp.broadcast_to(e_tot[:, hh:hh + 1], (sb, LANES))
    cbf_ref[...] = cm
    bbf_ref[...] = bm
    for g in range(SSD_GROUPS):
        t1_ref[g] = xdtd[:, g * GROUP_W:(g + 1) * GROUP_W].T.astype(BF16)
    seq_of_row = lax.broadcasted_iota(I32, (tm, SSD_STATE), 0) // l
    heads_per_group = SSD_HEADS // SSD_GROUPS

    def one_seq(j):
        r0 = pl.multiple_of(j * l, l)
        s0 = ssm0_ref[j]
        for g in range(SSD_GROUPS):
            ns = slice(g * SSD_STATE, (g + 1) * SSD_STATE)
            s0g = s0[g * heads_per_group:(g + 1) * heads_per_group].reshape(GROUP_W, SSD_STATE)
            cj = cbf_ref[pl.ds(r0, l), ns].astype(BF16)
            yoff_ref[pl.ds(r0, l), g * GROUP_W:(g + 1) * GROUP_W] = _dot_nt(cj, s0g.astype(BF16))
            bmask = jnp.where(seq_of_row == j, bbf_ref[:, ns], 0.0).astype(BF16)
            upd = _dot(t1_ref[g], bmask)
            for hq in range(heads_per_group):
                hh = g * heads_per_group + hq
                dec = dtab_ref[hh, pl.ds(j, 1), :]
                ssm_out_ref[j, hh] = s0[hh] * dec + upd[hq * SSD_HEAD_DIM:(hq + 1) * SSD_HEAD_DIM]

    def body(jj, carry):
        for u in range(SEQ_UNROLL):
            one_seq(jj * SEQ_UNROLL + u)
        return carry

    lax.fori_loop(0, sb // SEQ_UNROLL, body, 0)
    y = y_diag + yoff_ref[...] * e_e
    mixed_ref[...] = _mixer_back(y, xs, z, out_a, dskip_e[...], g_ob[...])


def _ada_spec(ref, sb, div):
    _, term, row0 = ref
    assert row0 % sb == 0
    return pl.BlockSpec((sb, 1, D_MODEL), lambda i: (row0 // sb + i // div, 0, term))


def _const_spec(a):
    nd = a.ndim
    return pl.BlockSpec(a.shape, lambda i, _nd=nd: (0,) * _nd)


def _spatial_consts(w_spatial, b_spatial, cl, tm):
    w = jnp.where(jnp.tril(jnp.ones((cl, cl), bool)), w_spatial[:, :cl, :cl], 0.0)
    eye = jnp.eye(tm // cl, dtype=F32)
    wbd = jnp.einsum("ab,hts->hatbs", eye, w).reshape(A_HEADS, tm, tm).astype(BF16)
    bias = jnp.tile(jnp.repeat(b_spatial[:, :cl].T, A_HEAD_DIM, axis=1), (tm // cl, 1))
    return wbd, bias


def _ssd_consts(cl):
    r = np.arange(CHUNK)
    same = (r[:, None] // cl) == (r[None, :] // cl)
    tril = same & (r[:, None] >= r[None, :])
    expand = np.zeros((LANES, B_WIDTH), np.float32)
    for hh in range(SSD_HEADS):
        expand[hh, hh * SSD_HEAD_DIM:(hh + 1) * SSD_HEAD_DIM] = 1.0
    return (jnp.asarray(tril, BF16), jnp.asarray(same, BF16), jnp.asarray(expand, BF16),
            jnp.asarray(np.where(tril, 0.0, DECAY_MASKED), F32))


def _front_weights(p):
    w_in = p["w_in"]
    c0, c1 = 3 * A_WIDTH, 3 * A_WIDTH + CONV_DIM
    w_dt = w_in[:, c1:]
    pad8 = lambda v: jnp.pad(v, (0, LANES - SSD_HEADS))
    a = -jnp.exp(p["a_log"])
    return dict(
        g_mix=p["g_mix"][None, :],
        w_uvz=w_in[:, :c0].astype(BF16),
        w_xbc=w_in[:, c0:c1].astype(BF16),
        w_dt=jnp.pad(w_dt, ((0, 0), (0, LANES - SSD_HEADS))).astype(BF16),
        g_v=p["g_v_a"][None, :], g_oa=p["g_out_a"][None, :],
        conv_w=p["conv_w"], conv_b=p["conv_b"][None, :],
        dt_bias=pad8(p["dt_bias"])[None, :],
        a_row=pad8(a)[None, :],
        dskip_e=jnp.repeat(p["d_skip"], SSD_HEAD_DIM)[None, :],
        g_ob=p["g_out_b"][None, :],
    )


def _front_list(fw, wbd, bias_sp):
    return [fw["g_mix"], fw["w_uvz"], fw["w_xbc"], fw["w_dt"], wbd, bias_sp, fw["g_v"], fw["g_oa"],
            fw["conv_w"], fw["conv_b"], fw["dt_bias"], fw["a_row"]]


def _prompt_mixer(x, sh, sc, fw, p):
    nseq, lseq, _ = x.shape
    tile = PROMPT_TILE
    tps = lseq // tile
    nt = nseq * tps
    x4 = x.reshape(nt, tile, D_MODEL)
    wbd, bias_sp = _spatial_consts(p["w_spatial"], p["b_spatial"], CHUNK, CHUNK)
    consts = _front_list(fw, wbd, bias_sp) + list(_ssd_consts(CHUNK)) + [fw["dskip_e"], fw["g_ob"]]
    mixed, conv_new, ssm_new = pl.pallas_call(
        functools.partial(_prompt_mixer_kernel, tps),
        out_shape=(jax.ShapeDtypeStruct((nt * tile, D_MODEL), BF16),
                   jax.ShapeDtypeStruct((nseq, CONV_K - 1, CONV_DIM), F32),
                   jax.ShapeDtypeStruct((nseq, B_WIDTH, SSD_STATE), F32)),
        grid=(nt,),
        in_specs=[pl.BlockSpec((1, tile, D_MODEL), lambda i: (i, 0, 0)), _ada_spec(sh, 1, tps), _ada_spec(sc, 1, tps)]
                 + [_const_spec(a) for a in consts],
        out_specs=(pl.BlockSpec((tile, D_MODEL), lambda i: (i, 0)),
                   pl.BlockSpec((1, CONV_K - 1, CONV_DIM), lambda i: (i // tps, 0, 0)),
                   pl.BlockSpec((1, B_WIDTH, SSD_STATE), lambda i: (i // tps, 0, 0))),
        scratch_shapes=[pltpu.VMEM((PROMPT_SPLITS, 1, tile // PROMPT_SPLITS + CONV_PAD, CONV_DIM), F32),
                        pltpu.VMEM((1, CONV_PAD, CONV_DIM), F32),
                        pltpu.VMEM((SSD_STATE, B_WIDTH), F32)],
        compiler_params=pltpu.CompilerParams(dimension_semantics=("arbitrary",), vmem_limit_bytes=VMEM_LIMIT),
        name="prompt_mixer",
    )(x4, sh[0], sc[0], *consts)
    return mixed, conv_new, ssm_new.reshape(nseq, SSD_HEADS, SSD_HEAD_DIM, SSD_STATE)


def _sample_mixer(x, sh, sc, state_ssm, state_conv, fw, p):
    nseq, l, _ = x.shape
    sb = SAMPLE_SEQ_TILE
    tm = sb * l
    assert tm == CHUNK
    wbd, bias_sp = _spatial_consts(p["w_spatial"], p["b_spatial"], l, tm)
    selseq = jnp.asarray((np.arange(tm)[None, :] // l) == np.arange(sb)[:, None], BF16)
    consts = _front_list(fw, wbd, bias_sp) + list(_ssd_consts(l)) + [fw["dskip_e"], fw["g_ob"], selseq]
    prev = jnp.pad(state_conv, ((0, 0), (CONV_PAD - (CONV_K - 1), 0), (0, 0)))
    ssm_spec = pl.BlockSpec((sb, SSD_HEADS, SSD_HEAD_DIM, SSD_STATE), lambda i: (i, 0, 0, 0))
    return pl.pallas_call(
        _sample_mixer_kernel,
        out_shape=(jax.ShapeDtypeStruct((nseq * l, D_MODEL), BF16),
                   jax.ShapeDtypeStruct((nseq * l, A_WIDTH), F32),
                   jax.ShapeDtypeStruct((nseq, CONV_K - 1, CONV_DIM), F32),
                   jax.ShapeDtypeStruct(state_ssm.shape, F32)),
        grid=(nseq // sb,),
        in_specs=[pl.BlockSpec((sb, l, D_MODEL), lambda i: (i, 0, 0)), _ada_spec(sh, sb, 1), _ada_spec(sc, sb, 1),
                  pl.BlockSpec((sb, CONV_PAD, CONV_DIM), lambda i: (i, 0, 0)), ssm_spec]
                 + [_const_spec(a) for a in consts],
        out_specs=(pl.BlockSpec((tm, D_MODEL), lambda i: (i, 0)),
                   pl.BlockSpec((tm, A_WIDTH), lambda i: (i, 0)),
                   pl.BlockSpec((sb, CONV_K - 1, CONV_DIM), lambda i: (i, 0, 0)),
                   ssm_spec),
        scratch_shapes=[pltpu.VMEM((sb, l + CONV_PAD, CONV_DIM), F32),
                        pltpu.VMEM((tm, B_WIDTH), F32),
                        pltpu.VMEM((tm, SSD_GROUPS * SSD_STATE), F32),
                        pltpu.VMEM((tm, SSD_GROUPS * SSD_STATE), F32),
                        pltpu.VMEM((SSD_GROUPS, GROUP_W, tm), BF16),
                        pltpu.VMEM((SSD_HEADS, sb, LANES), F32)],
        compiler_params=pltpu.CompilerParams(dimension_semantics=("arbitrary",), vmem_limit_bytes=VMEM_LIMIT),
        name="sample_mixer",
    )(x, sh[0], sc[0], prev, state_ssm, *consts)


def _post_kernel(mixed_ref, x_ref, gt_ref, sc_ref, sh_ref, w_out_ref, g_ffn_ref, wr_both_ref, br_ref,
                 x1_ref, h2p_ref, ids_ref):
    sb, l, _ = x_ref.shape
    tm = sb * l
    x3 = _flat(x_ref[...])
    x1 = x3 + _flat(gt_ref[...]) * _dot(mixed_ref[...], w_out_ref[...]).reshape(x3.shape)
    x1_ref[...] = x1.reshape(x1_ref.shape)
    h2 = (_rms(x1, g_ffn_ref[...]) * (1.0 + _flat(sc_ref[...])) + _flat(sh_ref[...])).reshape(tm, D_MODEL)
    h_hi, h_lo = _split(h2)
    h2p_ref[:, :D_MODEL] = h_hi
    both = _dot_nt(wr_both_ref[...], h_hi)
    logits = (both[:N_EXPERTS] + both[N_EXPERTS:] + _dot_nt(wr_both_ref[:N_EXPERTS, :], h_lo)
              + br_ref[...])
    e_iota = lax.broadcasted_iota(I32, logits.shape, 0)
    vals, idxs = [], []
    for _ in range(TOP_K):
        m = jnp.max(logits, axis=0, keepdims=True)
        idx = jnp.min(jnp.where(logits == m, e_iota, N_EXPERTS), axis=0, keepdims=True)
        vals.append(m)
        idxs.append(idx)
        logits = jnp.where(e_iota == idx, -jnp.inf, logits)
    ex = [jnp.exp(v - vals[0]) for v in vals]
    tot = ex[0] + ex[1] + ex[2] + ex[3]
    ids = jnp.concatenate(idxs, axis=0)
    ids_ref[...] = ids
    wts = jnp.concatenate([e / tot for e in ex], axis=0)
    w_hi = wts.astype(BF16).astype(F32)
    info = jnp.concatenate([ids.astype(F32), w_hi, wts - w_hi, jnp.zeros((TOP_K, tm), F32)], axis=0).astype(BF16)
    r = lax.broadcasted_iota(I32, (INFO_ROWS, LANES), 0)
    c = lax.broadcasted_iota(I32, (INFO_ROWS, LANES), 1)
    place = jnp.where(r == c, 1.0, 0.0).astype(BF16)
    h2p_ref[:, D_MODEL:] = lax.dot_general(info, place, (((0,), (0,)), ((), ())),
                                           preferred_element_type=F32).astype(BF16)


def _post(mixed, x, gt, sc, sh, w_out_b, g_ffn, wr_both, br, sb, l, seq_div):
    n3, _, _ = x.shape
    nblk = n3 // sb
    tm = sb * l
    t = n3 * l
    consts = [w_out_b, g_ffn, wr_both, br]
    return pl.pallas_call(
        _post_kernel,
        out_shape=(jax.ShapeDtypeStruct(x.shape, F32),
                   jax.ShapeDtypeStruct((t, ROW_W), BF16),
                   jax.ShapeDtypeStruct((TOP_K, t), I32)),
        grid=(nblk,),
        in_specs=[pl.BlockSpec((tm, D_MODEL), lambda i: (i, 0)),
                  pl.BlockSpec((sb, l, D_MODEL), lambda i: (i, 0, 0)),
                  _ada_spec(gt, sb, seq_div), _ada_spec(sc, sb, seq_div), _ada_spec(sh, sb, seq_div)]
                 + [_const_spec(a) for a in consts],
        out_specs=(pl.BlockSpec((sb, l, D_MODEL), lambda i: (i, 0, 0)),
                   pl.BlockSpec((tm, ROW_W), lambda i: (i, 0)),
                   pl.BlockSpec((TOP_K, tm), lambda i: (0, i))),
        compiler_params=pltpu.CompilerParams(dimension_semantics=("arbitrary",), vmem_limit_bytes=VMEM_LIMIT),
        name="post",
    )(mixed, x, gt[0], sc[0], sh[0], *consts)


def _strict_upper(n):
    r = lax.broadcasted_iota(I32, (n, n), 0)
    c = lax.broadcasted_iota(I32, (n, n), 1)
    return jnp.where(r < c, 1.0, 0.0).astype(BF16)


def _expert_prefix(col):
    r = lax.broadcasted_iota(I32, (N_EXPERTS, N_EXPERTS), 0)
    c = lax.broadcasted_iota(I32, (N_EXPERTS, N_EXPERTS), 1)
    as_row = jnp.sum(jnp.where(r == c, col, 0.0), axis=0, keepdims=True)
    return jnp.sum(jnp.where(c < r, as_row, 0.0), axis=1, keepdims=True)


def _plan_kernel(dump_group, ids_ref, lr_ref, gdst_ref, tile_e_ref):
    tm = TOK_TILE
    nb = ids_ref.shape[1] // tm
    blk_lane = lax.broadcasted_iota(I32, (N_EXPERTS, LANES), 1)

    def block_masks(b):
        ids = ids_ref[:, pl.ds(pl.multiple_of(b * tm, tm), tm)]
        e_iota = lax.broadcasted_iota(I32, (N_EXPERTS, tm), 0)
        onehot = [ids[k:k + 1, :] == e_iota for k in range(TOP_K)]
        sel = (onehot[0] | onehot[1]) | (onehot[2] | onehot[3])
        return onehot, jnp.where(sel, 1.0, 0.0)

    def count(b, seg):
        _, m = block_masks(b)
        seg_b = jnp.ceil(jnp.sum(m, axis=1, keepdims=True) * (1.0 / ROW_GROUP))
        return jnp.where(blk_lane == b, seg_b, seg)

    seg = lax.fori_loop(0, nb, count, jnp.zeros((N_EXPERTS, LANES), F32))
    tot = jnp.sum(seg, axis=1, keepdims=True)
    padded = jnp.ceil(tot * (1.0 / TILE_GROUPS)) * TILE_GROUPS
    gstart = _expert_prefix(padded)
    gb = gstart + _dot(seg.astype(BF16), _strict_upper(LANES))
    r = lax.broadcasted_iota(I32, (N_EXPERTS, LANES), 0)
    as_row = lambda col: jnp.sum(jnp.where(r == blk_lane, col, 0.0), axis=0, keepdims=True)
    n_used = jnp.sum(padded, axis=0, keepdims=True)
    lane = lax.broadcasted_iota(I32, (1, LANES), 1)
    tiles = jnp.where(lane == N_EXPERTS, n_used, as_row(gstart)) * (1.0 / TILE_GROUPS)
    meta = jnp.concatenate([tiles, as_row(gstart + tot), as_row(padded - tot),
                            jnp.zeros((META_ROWS - 3, LANES), F32)], axis=0)
    tile_e_ref[...] = meta.astype(I32)
    upper = _strict_upper(tm)

    def place(b, carry):
        onehot, m = block_masks(b)
        seg_b = jnp.sum(jnp.where(blk_lane == b, seg, 0.0), axis=1, keepdims=True)
        gb_b = jnp.sum(jnp.where(blk_lane == b, gb, 0.0), axis=1, keepdims=True)
        loc_b = _expert_prefix(seg_b)
        before = _dot(m.astype(BF16), upper) + loc_b * ROW_GROUP
        lr_ref[:, pl.ds(pl.multiple_of(b * tm, tm), tm)] = jnp.concatenate(
            [jnp.sum(jnp.where(onehot[k], before, 0.0), axis=0, keepdims=True) for k in range(TOP_K)],
            axis=0).astype(I32)
        g = lax.broadcasted_iota(I32, (N_EXPERTS, GDST_LANES), 1).astype(F32)
        inside = (loc_b <= g) & (g < loc_b + seg_b)
        dst = jnp.sum(jnp.where(inside, gb_b + g - loc_b, 0.0), axis=0, keepdims=True)
        used = jnp.sum(jnp.where(inside, 1.0, 0.0), axis=0, keepdims=True) > 0.5
        dump = dump_group + lax.convert_element_type(b % 2, F32) * LOCAL_GROUPS + g[0:1, :]
        gdst_ref[b] = jnp.where(used, dst, dump).astype(I32)
        return carry

    lax.fori_loop(0, nb, place, 0)


def _plan(ids, dump_group):
    t = ids.shape[1]
    nb = t // TOK_TILE
    assert nb <= LANES
    return pl.pallas_call(
        functools.partial(_plan_kernel, float(dump_group)),
        out_shape=(jax.ShapeDtypeStruct((TOP_K, t), I32),
                   jax.ShapeDtypeStruct((nb, 1, GDST_LANES), I32),
                   jax.ShapeDtypeStruct((META_ROWS, LANES), I32)),
        name="plan",
    )(ids)


def _sort_matrix(lr, c):
    r_iota = lax.broadcasted_iota(I32, (MASK_ROWS, lr.shape[1]), 0) + c * MASK_ROWS
    p = jnp.where(r_iota == lr[TOP_K - 1:TOP_K, :], 1.0, 0.0)
    for k in range(TOP_K - 1):
        p = jnp.where(r_iota == lr[k:k + 1, :], 1.0, p)
    return p.astype(BF16)


def _group_copies(loc_ref, slot, far_ref, gdst_ref, sem, to_far):
    copies = []
    for g in range(LOCAL_GROUPS):
        dst = pl.multiple_of(gdst_ref[0, 0, g] * ROW_GROUP, ROW_GROUP)
        near = loc_ref.at[slot, pl.ds(g * ROW_GROUP, ROW_GROUP)]
        far = far_ref.at[pl.ds(dst, ROW_GROUP)]
        copies.append(pltpu.make_async_copy(near, far, sem.at[slot]) if to_far
                      else pltpu.make_async_copy(far, near, sem.at[slot]))
    return copies


def _wait_groups(loc_ref, slot, sem):
    pltpu.make_async_copy(loc_ref.at[slot], loc_ref.at[slot], sem.at[slot]).wait()


def _dispatch_kernel(n_first, n_tiles_all, gdst_ref, meta_ref, lr_ref, ha_ref, hb_ref, xs_ref,
                     h_ref, loc_ref, zero_ref, sem, zsem):
    i = pl.program_id(0)
    last = pl.num_programs(0) - 1
    slot = i % 2

    @pl.when(i < n_first)
    def _():
        h_ref[...] = ha_ref[...]

    @pl.when(i >= n_first)
    def _():
        h_ref[...] = hb_ref[...]

    lr = lr_ref[...]
    copies = _group_copies(loc_ref, slot, xs_ref, gdst_ref, sem, True)
    n_chunks = LOCAL_ROWS // MASK_ROWS
    per_chunk = LOCAL_GROUPS // n_chunks
    for c in range(n_chunks):
        loc_ref[slot, c * MASK_ROWS:(c + 1) * MASK_ROWS, :] = _dot(_sort_matrix(lr, c), h_ref[...]).astype(BF16)
        for cp in copies[c * per_chunk:(c + 1) * per_chunk]:
            cp.start()

    @pl.when(i > 0)
    def _():
        _wait_groups(loc_ref, 1 - slot, sem)

    @pl.when(i == last)
    def _():
        _wait_groups(loc_ref, slot, sem)
        zero_ref[...] = jnp.zeros_like(zero_ref)

        def pad_copy(e, j):
            row = pl.multiple_of((meta_ref[1, e] + j) * ROW_GROUP, ROW_GROUP)
            return pltpu.make_async_copy(zero_ref.at[pl.ds(0, ROW_GROUP)], xs_ref.at[pl.ds(row, ROW_GROUP)], zsem)

        def tile_copy(t):
            row = pl.multiple_of(t * ROW_TILE, ROW_TILE)
            return pltpu.make_async_copy(zero_ref, xs_ref.at[pl.ds(row, ROW_TILE)], zsem)

        def pads(fn):
            def body(e, carry):
                for j in range(TILE_GROUPS - 1):
                    @pl.when(j < meta_ref[2, e])
                    def _():
                        fn(pad_copy(e, j))
                return carry
            lax.fori_loop(0, N_EXPERTS, body, 0)

        def tiles(fn):
            def body(t, carry):
                fn(tile_copy(t))
                return carry
            lax.fori_loop(meta_ref[0, N_EXPERTS], n_tiles_all, body, 0)

        pads(lambda c: c.start())
        tiles(lambda c: c.start())
        pads(lambda c: c.wait())
        tiles(lambda c: c.wait())


def _dispatch(gdst, meta, lr, h_a, h_b, n_tiles_all):
    tm = TOK_TILE
    na, nb2 = h_a.shape[0] // tm, h_b.shape[0] // tm
    return pl.pallas_call(
        functools.partial(_dispatch_kernel, na, n_tiles_all),
        out_shape=jax.ShapeDtypeStruct((n_tiles_all * ROW_TILE, ROW_W), BF16),
        grid=(na + nb2,),
        in_specs=[pl.BlockSpec((1, 1, GDST_LANES), lambda i: (i, 0, 0), memory_space=pltpu.SMEM),
                  pl.BlockSpec((META_ROWS, LANES), lambda i: (0, 0), memory_space=pltpu.SMEM),
                  pl.BlockSpec((TOP_K, tm), lambda i: (0, i)),
                  pl.BlockSpec((tm, ROW_W), lambda i: (jnp.minimum(i, na - 1), 0)),
                  pl.BlockSpec((tm, ROW_W), lambda i: (jnp.maximum(i - na, 0), 0))],
        out_specs=pl.BlockSpec(memory_space=pl.ANY),
        scratch_shapes=[pltpu.VMEM((tm, ROW_W), BF16), pltpu.VMEM((2, LOCAL_ROWS, ROW_W), BF16),
                        pltpu.VMEM((ROW_TILE, ROW_W), BF16),
                        pltpu.SemaphoreType.DMA((2,)), pltpu.SemaphoreType.DMA],
        compiler_params=pltpu.CompilerParams(dimension_semantics=("arbitrary",), vmem_limit_bytes=VMEM_LIMIT),
        name="dispatch",
    )(gdst, meta, lr, h_a, h_b)


def _expert_kernel(n_tiles_all, ts_ref, xs_ref, wg_ref, wu_ref, wd_ref, bg_ref, bu_ref, bd_ref, y_ref,
                   wstage, wgb, wub, wdb, xbuf, ybuf, sem_w, sem_in, sem_out):
    e = pl.program_id(0)
    t0 = ts_ref[e]
    nt = ts_ref[e + 1] - t0
    wslot = e % 2

    def w_copies(ex, slot):
        return [pltpu.make_async_copy(w_ref.at[ex], wstage.at[slot, j], sem_w.at[slot])
                for j, w_ref in enumerate((wg_ref, wu_ref, wd_ref))]

    def in_copy(t, slot):
        rows = pl.ds(pl.multiple_of((t0 + t) * ROW_TILE, ROW_TILE), ROW_TILE)
        return pltpu.make_async_copy(xs_ref.at[rows], xbuf.at[slot], sem_in.at[slot])

    def out_copy(tile, slot):
        rows = pl.ds(pl.multiple_of(tile * ROW_TILE, ROW_TILE), ROW_TILE)
        return pltpu.make_async_copy(ybuf.at[slot], y_ref.at[rows], sem_out.at[slot])

    @pl.when((e == 0) & (nt > 0))
    def _():
        for c in w_copies(0, 0):
            c.start()

    for j in range(TILE_SLOTS):
        @pl.when(nt > j)
        def _():
            in_copy(j, j).start()

    @pl.when(e + 1 < N_EXPERTS)
    def _():
        @pl.when(ts_ref[e + 2] > ts_ref[e + 1])
        def _():
            for c in w_copies(e + 1, 1 - wslot):
                c.start(priority=1)

    @pl.when(nt > 0)
    def _():
        for c in w_copies(e, wslot):
            c.wait()
        wgb[...] = wstage[wslot, 0].astype(BF16)
        wub[...] = wstage[wslot, 1].astype(BF16)
        wdb[...] = wstage[wslot, 2].astype(BF16)
        e_f = e.astype(F32)

        def acquire(t):
            slot = t % TILE_SLOTS
            in_copy(t, slot).wait()

            @pl.when(t >= TILE_SLOTS)
            def _():
                out_copy(t0 + t - TILE_SLOTS, slot).wait()

        def compute(t):
            slot = t % TILE_SLOTS
            xw = xbuf[slot]
            x = xw[:, :D_MODEL]
            info = xw[:, D_MODEL:].astype(F32)
            w_row = jnp.zeros((ROW_TILE, 1), F32)
            for k in range(TOP_K):
                wk = info[:, TOP_K + k:TOP_K + k + 1] + info[:, 2 * TOP_K + k:2 * TOP_K + k + 1]
                w_row = w_row + jnp.where(info[:, k:k + 1] == e_f, wk, 0.0)
            g = jnp.minimum(_dot(x, wgb[...]) + bg_ref[0], SWIGLU_LIMIT)
            u = jnp.clip(_dot(x, wub[...]) + bu_ref[0], -SWIGLU_LIMIT, SWIGLU_LIMIT)
            act = g * jax.nn.sigmoid(SWIGLU_ALPHA * g) * (u + 1.0)
            ybuf[slot] = ((_dot(act.astype(BF16), wdb[...]) + bd_ref[0]) * w_row).astype(BF16)

        def release(t):
            slot = t % TILE_SLOTS
            out_copy(t0 + t, slot).start()

            @pl.when(t + TILE_SLOTS < nt)
            def _():
                in_copy(t + TILE_SLOTS, slot).start()

        def run_tiles(first, count):
            tiles = [first + u for u in range(count)]
            for t in tiles:
                acquire(t)
            for t in tiles:
                compute(t)
            for t in tiles:
                release(t)

        n_groups = nt // TILE_GROUP
        lax.fori_loop(0, n_groups, lambda p, c: (run_tiles(TILE_GROUP * p, TILE_GROUP), c)[1], 0)
        done = n_groups * TILE_GROUP
        size = TILE_GROUP // 2
        while size >= 1:
            take = ((nt - done) // size) % 2 == 1
            pl.when(take)(functools.partial(run_tiles, done, size))
            done = done + jnp.where(take, size, 0)
            size //= 2

        for j in range(1, TILE_SLOTS + 1):
            @pl.when(nt >= j)
            def _():
                out_copy(t0 + nt - j, (nt - j) % TILE_SLOTS).wait()

    @pl.when(e == pl.num_programs(0) - 1)
    def _():
        n_used = ts_ref[N_EXPERTS]
        ybuf[0] = jnp.zeros((ROW_TILE, D_MODEL), BF16)

        def zstart(tile, carry):
            out_copy(tile, 0).start()
            return carry

        def zwait(tile, carry):
            out_copy(tile, 0).wait()
            return carry

        lax.fori_loop(n_used, n_tiles_all, zstart, 0)
        lax.fori_loop(n_used, n_tiles_all, zwait, 0)


def _experts(tile_start, xs, w_gate, b_gate, w_up, b_up, w_down, b_down):
    n_rows = xs.shape[0]
    b_spec = pl.BlockSpec((1, 1, D_MODEL), lambda e, ts: (e, 0, 0))
    any_spec = pl.BlockSpec(memory_space=pl.ANY)
    return pl.pallas_call(
        functools.partial(_expert_kernel, n_rows // ROW_TILE),
        out_shape=jax.ShapeDtypeStruct((n_rows, D_MODEL), BF16),
        grid_spec=pltpu.PrefetchScalarGridSpec(
            num_scalar_prefetch=1,
            grid=(N_EXPERTS,),
            in_specs=[any_spec, any_spec, any_spec, any_spec, b_spec, b_spec, b_spec],
            out_specs=any_spec,
            scratch_shapes=[pltpu.VMEM((2, 3, D_MODEL, D_MODEL), F32)]
                           + [pltpu.VMEM((D_MODEL, D_MODEL), BF16)] * 3
                           + [pltpu.VMEM((TILE_SLOTS, ROW_TILE, ROW_W), BF16),
                              pltpu.VMEM((TILE_SLOTS, ROW_TILE, D_MODEL), BF16),
                              pltpu.SemaphoreType.DMA((2,)), pltpu.SemaphoreType.DMA((TILE_SLOTS,)),
                              pltpu.SemaphoreType.DMA((TILE_SLOTS,))],
        ),
        compiler_params=pltpu.CompilerParams(dimension_semantics=("arbitrary",), vmem_limit_bytes=VMEM_LIMIT),
        name="experts",
    )(tile_start, xs, w_gate, w_up, w_down, b_gate[:, None, :], b_up[:, None, :], b_down[:, None, :])


def _combine_kernel(gdst_ref, gdst_next_ref, lr_ref, y_ref, x1_ref, gt_ref, gf_ref, out_ref, loc_ref, sem):
    i = pl.program_id(0)
    slot = i % 2
    sb, l, _ = x1_ref.shape
    x1 = _flat(x1_ref[...])

    @pl.when(i == 0)
    def _():
        for c in _group_copies(loc_ref, slot, y_ref, gdst_ref, sem, False):
            c.start()

    _wait_groups(loc_ref, slot, sem)

    prefetch = _group_copies(loc_ref, 1 - slot, y_ref, gdst_next_ref, sem, False)
    n_chunks = LOCAL_ROWS // MASK_ROWS
    per_chunk = LOCAL_GROUPS // n_chunks
    lr = lr_ref[...]
    moe = jnp.zeros((sb * l, D_MODEL), F32)
    for c in range(n_chunks):
        rows = loc_ref[slot, c * MASK_ROWS:(c + 1) * MASK_ROWS, :]
        moe = moe + lax.dot_general(_sort_matrix(lr, c), rows, (((0,), (0,)), ((), ())),
                                    preferred_element_type=F32)
        for cp in prefetch[c * per_chunk:(c + 1) * per_chunk]:
            cp.start()
    x2 = x1 + _flat(gt_ref[...]) * moe.reshape(x1.shape)
    out_ref[...] = _rms(x2, gf_ref[...]).reshape(out_ref.shape)

    @pl.when(i == pl.num_programs(0) - 1)
    def _():
        _wait_groups(loc_ref, 1 - slot, sem)


def _combine(gdst, lr, y, x1, gt, g_final, sb, l, seq_div, blk_off):
    n3 = x1.shape[0]
    nblk = n3 // sb
    tm = sb * l
    assert tm == TOK_TILE
    return pl.pallas_call(
        _combine_kernel,
        out_shape=jax.ShapeDtypeStruct(x1.shape, F32),
        grid=(nblk,),
        in_specs=[pl.BlockSpec((1, 1, GDST_LANES), lambda i: (i + blk_off, 0, 0), memory_space=pltpu.SMEM),
                  pl.BlockSpec((1, 1, GDST_LANES), lambda i: (jnp.minimum(i + 1, nblk - 1) + blk_off, 0, 0),
                               memory_space=pltpu.SMEM),
                  pl.BlockSpec((TOP_K, tm), lambda i: (0, i + blk_off)),
                  pl.BlockSpec(memory_space=pl.ANY),
                  pl.BlockSpec((sb, l, D_MODEL), lambda i: (i, 0, 0)),
                  _ada_spec(gt, sb, seq_div),
                  pl.BlockSpec((1, D_MODEL), lambda i: (0, 0))],
        out_specs=pl.BlockSpec((sb, l, D_MODEL), lambda i: (i, 0, 0)),
        scratch_shapes=[pltpu.VMEM((2, LOCAL_ROWS, D_MODEL), BF16), pltpu.SemaphoreType.DMA((2,))],
        compiler_params=pltpu.CompilerParams(dimension_semantics=("arbitrary",), vmem_limit_bytes=VMEM_LIMIT),
        name="combine",
    )(gdst, gdst, lr, y, x1, gt[0], g_final)


def kernel(x_prompt, x_sample, c_prompt, c_sample, state_ssm, state_conv, w_ada, b_ada, g_mix, w_in, g_v_a, w_spatial, b_spatial, g_out_a, conv_w, conv_b, dt_bias, a_log, d_skip, g_out_b, w_out, g_ffn, w_router, b_router, w_gate, b_gate, w_up, b_up, w_down, b_down, g_final):
    assert w_ada.shape[0] == 1, "single-layer step"
    p = dict(w_in=w_in[0], g_mix=g_mix[0], g_v_a=g_v_a[0], w_spatial=w_spatial[0], b_spatial=b_spatial[0],
             g_out_a=g_out_a[0], conv_w=conv_w[0], conv_b=conv_b[0], dt_bias=dt_bias[0], a_log=a_log[0],
             d_skip=d_skip[0], g_out_b=g_out_b[0])
    bp, lp, _ = x_prompt.shape
    bs, ls, _ = x_sample.shape
    tp, ts = bp * lp, bs * ls

    ada = _ada(jnp.concatenate([c_sample, c_prompt], axis=0), w_ada[0], b_ada[0][None, :])
    ada = ada.reshape(bs + bp, 1, 6 * D_MODEL)
    ada_s = [(ada, j, 0) for j in range(6)]
    ada_p = [(ada, j, bs) for j in range(6)]

    fw = _front_weights(p)
    mixed_p, conv_p, ssm_p = _prompt_mixer(x_prompt, ada_p[0], ada_p[1], fw, p)
    mixed_s, v_s, conv_s, ssm_s = _sample_mixer(x_sample, ada_s[0], ada_s[1], state_ssm[0], state_conv[0], fw, p)

    w_out_b = w_out[0].astype(BF16)
    g_ffn2 = g_ffn[0][None, :]
    wr_t = w_router[0].T
    wr_hi = wr_t.astype(BF16)
    wr_both = jnp.concatenate([wr_hi, (wr_t - wr_hi.astype(F32)).astype(BF16)], axis=0)
    br = b_router[0][:, None]
    tps = lp // TOK_TILE
    sbs = TOK_TILE // ls
    xp3 = x_prompt.reshape(tp // POST_TILE, POST_TILE, D_MODEL)
    x1_p, h2p_p, ids_p = _post(mixed_p, xp3, ada_p[2], ada_p[4], ada_p[3], w_out_b, g_ffn2, wr_both, br,
                               1, POST_TILE, lp // POST_TILE)
    x1_p = x1_p.reshape(bp * tps, TOK_TILE, D_MODEL)
    x1_s, h2p_s, ids_s = _post(mixed_s, x_sample, ada_s[2], ada_s[4], ada_s[3], w_out_b, g_ffn2, wr_both, br,
                               POST_TILE // ls, ls, 1)

    n_blocks = (tp + ts) // TOK_TILE
    max_groups = (tp + ts) * TOP_K // ROW_GROUP + n_blocks * N_EXPERTS + N_EXPERTS * (TILE_GROUPS - 1)
    n_tiles = -(-max_groups // TILE_GROUPS)
    n_tiles_all = n_tiles + 2 * LOCAL_GROUPS // TILE_GROUPS
    lr, gdst, meta = _plan(jnp.concatenate([ids_p, ids_s], axis=1), n_tiles * TILE_GROUPS)

    xs = _dispatch(gdst, meta, lr, h2p_p, h2p_s, n_tiles_all)
    y = _experts(meta[0], xs, w_gate[0], b_gate[0], w_up[0], b_up[0], w_down[0], b_down[0])

    gf = g_final[None, :]
    y_p = _combine(gdst, lr, y, x1_p, ada_p[5], gf, 1, TOK_TILE, tps, 0).reshape(bp, lp, D_MODEL)
    y_s = _combine(gdst, lr, y, x1_s, ada_s[5], gf, sbs, ls, 1, tp // TOK_TILE)

    return (y_p, y_s, ssm_p[None], conv_p[None], ssm_s[None], conv_s[None], v_s.reshape(1, bs, ls, A_WIDTH))
```

```python
import functools
import math

import numpy as np
import jax
import jax.numpy as jnp
from jax import lax
from jax.experimental import pallas as pl
from jax.experimental.pallas import tpu as pltpu

F32 = jnp.float32
BF16 = jnp.bfloat16
I32 = jnp.int32

D_MODEL = 1024
A_WIDTH = 512
A_HEADS = 4
A_HEAD_DIM = 128
CHUNK = 128
B_WIDTH = 512
SSD_HEAD_DIM = 64
SSD_HEADS = 8
SSD_GROUPS = 2
SSD_STATE = 128
GROUP_W = B_WIDTH // SSD_GROUPS
CONV_K = 4
CONV_DIM = 1024
CONV_PAD = 8
N_EXPERTS = 32
TOP_K = 4
SWIGLU_LIMIT = 7.0
SWIGLU_ALPHA = 1.702
EPS = 1e-6
DECAY_MASKED = -1e30
LANES = 128

TOK_TILE = 512
SAMPLE_SEQ_TILE = 16
SEQ_UNROLL = 8
ROW_TILE = 256
ROW_GROUP = 16
TILE_GROUPS = ROW_TILE // ROW_GROUP
LOCAL_GROUPS = TOK_TILE * TOP_K // ROW_GROUP + N_EXPERTS
LOCAL_ROWS = LOCAL_GROUPS * ROW_GROUP
GDST_LANES = -(-LOCAL_GROUPS // LANES) * LANES
ROW_W = D_MODEL + LANES
INFO_ROWS = 16
MASK_ROWS = 256
META_ROWS = 8
POST_TILE = 1024
PROMPT_TILE = 512
PROMPT_SPLITS = 2
TILE_SLOTS = 4
VMEM_LIMIT = 56 * 1024 * 1024


def _dot(a, b):
    return jnp.dot(a, b, preferred_element_type=F32)


def _dot_nt(a, b):
    return lax.dot_general(a, b, (((1,), (1,)), ((), ())), preferred_element_type=F32)


def _split(x):
    hi = x.astype(BF16)
    lo = (x - hi.astype(F32)).astype(BF16)
    return hi, lo


def _dot_exact_l(t, x):
    hi, lo = _split(x)
    return _dot(t, hi) + _dot(t, lo)


def _dot_exact_r(x, t):
    hi, lo = _split(x)
    return _dot(hi, t) + _dot(lo, t)


def _silu(x):
    return x * jax.nn.sigmoid(x)


def _gelu(x):
    return 0.5 * x * (1.0 + lax.erf(x * (1.0 / math.sqrt(2.0))))


def _softplus(x):
    return jnp.maximum(x, 0.0) + jnp.log1p(jnp.exp(-jnp.abs(x)))


def _rms(x, g):
    return x * lax.rsqrt(jnp.mean(x * x, axis=-1, keepdims=True) + EPS) * g


def _ada_kernel(c_ref, w_ref, b_ref, o_ref):
    s_hi, s_lo = _split(_silu(c_ref[...]))
    w_hi, w_lo = _split(w_ref[...])
    o_ref[...] = _dot(s_hi, w_hi) + _dot(s_lo, w_hi) + _dot(s_hi, w_lo) + b_ref[...]


def _ada(c_all, w_ada, b_ada):
    m = c_all.shape[0]
    n = w_ada.shape[1]
    bn = 1024
    return pl.pallas_call(
        _ada_kernel,
        out_shape=jax.ShapeDtypeStruct((m, n), F32),
        grid=(n // bn,),
        in_specs=[pl.BlockSpec((m, D_MODEL), lambda j: (0, 0)),
                  pl.BlockSpec((D_MODEL, bn), lambda j: (0, j)),
                  pl.BlockSpec((1, bn), lambda j: (0, j))],
        out_specs=pl.BlockSpec((m, bn), lambda j: (0, j)),
        compiler_params=pltpu.CompilerParams(dimension_semantics=("arbitrary",), vmem_limit_bytes=VMEM_LIMIT),
        name="ada",
    )(c_all, w_ada, b_ada)


def _mixer_front(x3, sh, sc, prev, refs, xp_ref):
    (g_mix, w_uvz, w_xbc, w_dt, wbd, bias_sp, g_v, g_oa, conv_w, conv_b, dt_bias, a_row) = refs
    sb, l, _ = x3.shape
    tm = sb * l
    xn = x3 * lax.rsqrt(jnp.mean(x3 * x3, axis=-1, keepdims=True) + EPS) * g_mix[...]
    h = (xn * (1.0 + sc) + sh).reshape(tm, D_MODEL)
    hb = h.astype(BF16)
    uvz = _dot(hb, w_uvz[...])
    xbc = _dot(hb, w_xbc[...])
    dt_raw = _dot(hb, w_dt[...])

    u = _gelu(uvz[:, :A_WIDTH])
    vg = _gelu(uvz[:, A_WIDTH:2 * A_WIDTH])
    z = uvz[:, 2 * A_WIDTH:]
    v_parts, s_parts = [], []
    for hd in range(A_HEADS):
        sl = slice(hd * A_HEAD_DIM, (hd + 1) * A_HEAD_DIM)
        vh = _rms(vg[:, sl], g_v[:, sl])
        v_parts.append(vh)
        vb = vh.astype(BF16)
        s_parts.append(jnp.concatenate([_dot(wbd[hd], vb[r0:r0 + CHUNK]) + bias_sp[:, sl]
                                        for r0 in range(0, tm, CHUNK)], axis=0))
    v = jnp.concatenate(v_parts, axis=1)
    s_a = jnp.concatenate(s_parts, axis=1)
    out_a = _rms(u * s_a, g_oa[...])

    xp_ref[:, 0:CONV_PAD, :] = prev
    xp_ref[:, CONV_PAD:, :] = xbc.reshape(sb, l, CONV_DIM)
    xp = xp_ref[...]
    acc = conv_b[...] + xp[:, CONV_PAD:, :] * conv_w[CONV_K - 1:CONV_K, :]
    for s in range(1, CONV_K):
        back = pltpu.roll(xp, s, axis=1)[:, CONV_PAD:, :]
        acc = acc + back * conv_w[CONV_K - 1 - s:CONV_K - s, :]
    xc = _silu(acc).reshape(tm, CONV_DIM)
    dt = _softplus(dt_raw + dt_bias[...])
    d_a = dt * a_row[...]
    return out_a, v, z, xc, dt, d_a


def _ssd_chunk(xs, bm, cm, dt, d_a, cref):
    tril, ones, expand, neg_mask = cref
    cs = _dot_exact_l(tril[...], d_a)
    cs_t = cs.T
    dt_t = dt.T
    cs_tot = _dot_exact_l(ones[...], d_a)
    vals = jnp.concatenate([dt * jnp.exp(cs_tot - cs), jnp.exp(cs)], axis=0)
    vals_e = _dot_exact_r(vals, expand[...])
    n = xs.shape[0]
    w_e, e_e = vals_e[:n], vals_e[n:]
    xdtd = xs * w_e
    neg = neg_mask[...]
    row_lt_half = lax.broadcasted_iota(I32, (2 * n, LANES), 0) < n
    lane_lt_half = lax.broadcasted_iota(I32, (2 * n, LANES), 1) < SSD_HEAD_DIM
    y_parts = []
    for g in range(SSD_GROUPS):
        cb = _dot_nt(cm[:, g * SSD_STATE:(g + 1) * SSD_STATE].astype(BF16),
                     bm[:, g * SSD_STATE:(g + 1) * SSD_STATE].astype(BF16))
        for hp in range(SSD_HEADS // SSD_GROUPS // 2):
            h0 = g * (SSD_HEADS // SSD_GROUPS) + 2 * hp
            ms = []
            for hh in (h0, h0 + 1):
                diff = cs[:, hh:hh + 1] - cs_t[hh:hh + 1, :]
                ms.append((cb * jnp.exp(diff + neg) * dt_t[hh:hh + 1, :]).astype(BF16))
            pair = xs[:, h0 * SSD_HEAD_DIM:(h0 + 2) * SSD_HEAD_DIM]
            rhs = jnp.where(row_lt_half == lane_lt_half, jnp.concatenate([pair, pair], axis=0), 0.0).astype(BF16)
            y_parts.append(_dot(jnp.concatenate(ms, axis=1), rhs))
    y_diag = jnp.concatenate(y_parts, axis=1)
    return y_diag, e_e, xdtd, cs_tot


def _mixer_back(y, xs, z, out_a, dskip_e, g_ob):
    y = y + xs * dskip_e
    gated = y * _silu(z)
    parts = [_rms(gated[:, g * GROUP_W:(g + 1) * GROUP_W], g_ob[:, g * GROUP_W:(g + 1) * GROUP_W])
             for g in range(SSD_GROUPS)]
    return jnp.concatenate([out_a] + parts, axis=1).astype(BF16)


N_FRONT = 12
N_SSD = 4


def _prompt_mixer_kernel(tiles_per_seq, x_ref, sh_ref, sc_ref, *rest):
    front = rest[:N_FRONT]
    cref = rest[N_FRONT:N_FRONT + N_SSD]
    dskip_e, g_ob = rest[N_FRONT + N_SSD:N_FRONT + N_SSD + 2]
    mixed_ref, conv_out_ref, ssm_out_ref = rest[N_FRONT + N_SSD + 2:N_FRONT + N_SSD + 5]
    xp_ref, carry_ref, st_ref = rest[N_FRONT + N_SSD + 5:]
    i = pl.program_id(0)
    first = (i % tiles_per_seq) == 0

    @pl.when(first)
    def _():
        carry_ref[...] = jnp.zeros_like(carry_ref)
        st_ref[...] = jnp.zeros_like(st_ref)

    l = x_ref.shape[1]
    hl = l // PROMPT_SPLITS
    parts = []
    prev = carry_ref[...]
    for hh in range(PROMPT_SPLITS):
        xp_h = xp_ref.at[hh]
        parts.append(_mixer_front(x_ref[:, hh * hl:(hh + 1) * hl, :], sh_ref[...], sc_ref[...], prev, front, xp_h))
        prev = xp_h[:, hl:hl + CONV_PAD, :]
    carry_ref[...] = prev
    out_a, _, z, xc, dt, d_a = [jnp.concatenate([p[j] for p in parts], axis=0) for j in range(6)]
    xs = xc[:, :B_WIDTH]
    y_rows = []
    for c in range(l // CHUNK):
        r = slice(c * CHUNK, (c + 1) * CHUNK)
        bm = xc[r, B_WIDTH:B_WIDTH + SSD_GROUPS * SSD_STATE]
        cm = xc[r, B_WIDTH + SSD_GROUPS * SSD_STATE:]
        y_diag, e_e, xdtd, _ = _ssd_chunk(xs[r], bm, cm, dt[r], d_a[r], cref)
        st = st_ref[...]
        y_off, upd = [], []
        for g in range(SSD_GROUPS):
            gs = slice(g * GROUP_W, (g + 1) * GROUP_W)
            ns = slice(g * SSD_STATE, (g + 1) * SSD_STATE)
            y_off.append(_dot(cm[:, ns].astype(BF16), st[:, gs].astype(BF16)))
            upd.append(_dot(bm[:, ns].T.astype(BF16), xdtd[:, gs].astype(BF16)))
        y_rows.append(y_diag + jnp.concatenate(y_off, axis=1) * e_e)
        st_ref[...] = st * e_e[CHUNK - 1:CHUNK, :] + jnp.concatenate(upd, axis=1)
    y = jnp.concatenate(y_rows, axis=0)
    mixed_ref[...] = _mixer_back(y, xs, z, out_a, dskip_e[...], g_ob[...])

    @pl.when((i % tiles_per_seq) == tiles_per_seq - 1)
    def _():
        conv_out_ref[...] = xp_ref[PROMPT_SPLITS - 1, :, hl + CONV_PAD - (CONV_K - 1):hl + CONV_PAD, :]
        ssm_out_ref[0] = st_ref[...].T


def _sample_mixer_kernel(x_ref, sh_ref, sc_ref, prev_ref, ssm0_ref, *rest):
    front = rest[:N_FRONT]
    cref = rest[N_FRONT:N_FRONT + N_SSD]
    dskip_e, g_ob, selseq = rest[N_FRONT + N_SSD:N_FRONT + N_SSD + 3]
    mixed_ref, v_ref, conv_out_ref, ssm_out_ref = rest[N_FRONT + N_SSD + 3:N_FRONT + N_SSD + 7]
    xp_ref, yoff_ref, cbf_ref, bbf_ref, t1_ref, dtab_ref = rest[N_FRONT + N_SSD + 7:]
    x3 = x_ref[...]
    sb, l, _ = x3.shape
    tm = sb * l
    out_a, v, z, xc, dt, d_a = _mixer_front(x3, sh_ref[...], sc_ref[...], prev_ref[...], front, xp_ref)
    v_ref[...] = v
    conv_out_ref[...] = xp_ref[:, l + CONV_PAD - (CONV_K - 1):l + CONV_PAD, :]
    xs = xc[:, :B_WIDTH]
    bm = xc[:, B_WIDTH:B_WIDTH + SSD_GROUPS * SSD_STATE]
    cm = xc[:, B_WIDTH + SSD_GROUPS * SSD_STATE:]
    y_diag, e_e, xdtd, _ = _ssd_chunk(xs, bm, cm, dt, d_a, cref)

    e_tot = jnp.exp(_dot_exact_l(selseq[...], d_a))
    for hh in range(SSD_HEADS):
        dtab_ref[hh] = jnp.broadcast_to(e_tot[:, hh:hh + 1], (sb, LANES))
    cbf_ref[...] = cm
    bbf_ref[...] = bm
    for g in range(SSD_GROUPS):
        t1_ref[g] = xdtd[:, g * GROUP_W:(g + 1) * GROUP_W].T.astype(BF16)
    seq_of_row = lax.broadcasted_iota(I32, (tm, SSD_STATE), 0) // l
    heads_per_group = SSD_HEADS // SSD_GROUPS

    def one_seq(j):
        r0 = pl.multiple_of(j * l, l)
        s0 = ssm0_ref[j]
        for g in range(SSD_GROUPS):
            ns = slice(g * SSD_STATE, (g + 1) * SSD_STATE)
            s0g = s0[g * heads_per_group:(g + 1) * heads_per_group].reshape(GROUP_W, SSD_STATE)
            cj = cbf_ref[pl.ds(r0, l), ns].astype(BF16)
            yoff_ref[pl.ds(r0, l), g * GROUP_W:(g + 1) * GROUP_W] = _dot_nt(cj, s0g.astype(BF16))
            bmask = jnp.where(seq_of_row == j, bbf_ref[:, ns], 0.0).astype(BF16)
            upd = _dot(t1_ref[g], bmask)
            for hq in range(heads_per_group):
                hh = g * heads_per_group + hq
                dec = dtab_ref[hh, pl.ds(j, 1), :]
                ssm_out_ref[j, hh] = s0[hh] * dec + upd[hq * SSD_HEAD_DIM:(hq + 1) * SSD_HEAD_DIM]

    def body(jj, carry):
        for u in range(SEQ_UNROLL):
            one_seq(jj * SEQ_UNROLL + u)
        return carry

    lax.fori_loop(0, sb // SEQ_UNROLL, body, 0)
    y = y_diag + yoff_ref[...] * e_e
    mixed_ref[...] = _mixer_back(y, xs, z, out_a, dskip_e[...], g_ob[...])


def _ada_spec(ref, sb, div):
    _, term, row0 = ref
    assert row0 % sb == 0
    return pl.BlockSpec((sb, 1, D_MODEL), lambda i: (row0 // sb + i // div, 0, term))


def _const_spec(a):
    nd = a.ndim
    return pl.BlockSpec(a.shape, lambda i, _nd=nd: (0,) * _nd)


def _spatial_consts(w_spatial, b_spatial, cl, tm):
    w = jnp.where(jnp.tril(jnp.ones((cl, cl), bool)), w_spatial[:, :cl, :cl], 0.0)
    eye = jnp.eye(tm // cl, dtype=F32)
    wbd = jnp.einsum("ab,hts->hatbs", eye, w).reshape(A_HEADS, tm, tm).astype(BF16)
    bias = jnp.tile(jnp.repeat(b_spatial[:, :cl].T, A_HEAD_DIM, axis=1), (tm // cl, 1))
    return wbd, bias


def _ssd_consts(cl):
    r = np.arange(CHUNK)
    same = (r[:, None] // cl) == (r[None, :] // cl)
    tril = same & (r[:, None] >= r[None, :])
    expand = np.zeros((LANES, B_WIDTH), np.float32)
    for hh in range(SSD_HEADS):
        expand[hh, hh * SSD_HEAD_DIM:(hh + 1) * SSD_HEAD_DIM] = 1.0
    return (jnp.asarray(tril, BF16), jnp.asarray(same, BF16), jnp.asarray(expand, BF16),
            jnp.asarray(np.where(tril, 0.0, DECAY_MASKED), F32))


def _front_weights(p):
    w_in = p["w_in"]
    c0, c1 = 3 * A_WIDTH, 3 * A_WIDTH + CONV_DIM
    w_dt = w_in[:, c1:]
    pad8 = lambda v: jnp.pad(v, (0, LANES - SSD_HEADS))
    a = -jnp.exp(p["a_log"])
    return dict(
        g_mix=p["g_mix"][None, :],
        w_uvz=w_in[:, :c0].astype(BF16),
        w_xbc=w_in[:, c0:c1].astype(BF16),
        w_dt=jnp.pad(w_dt, ((0, 0), (0, LANES - SSD_HEADS))).astype(BF16),
        g_v=p["g_v_a"][None, :], g_oa=p["g_out_a"][None, :],
        conv_w=p["conv_w"], conv_b=p["conv_b"][None, :],
        dt_bias=pad8(p["dt_bias"])[None, :],
        a_row=pad8(a)[None, :],
        dskip_e=jnp.repeat(p["d_skip"], SSD_HEAD_DIM)[None, :],
        g_ob=p["g_out_b"][None, :],
    )


def _front_list(fw, wbd, bias_sp):
    return [fw["g_mix"], fw["w_uvz"], fw["w_xbc"], fw["w_dt"], wbd, bias_sp, fw["g_v"], fw["g_oa"],
            fw["conv_w"], fw["conv_b"], fw["dt_bias"], fw["a_row"]]


def _prompt_mixer(x, sh, sc, fw, p):
    nseq, lseq, _ = x.shape
    tile = PROMPT_TILE
    tps = lseq // tile
    nt = nseq * tps
    x4 = x.reshape(nt, tile, D_MODEL)
    wbd, bias_sp = _spatial_consts(p["w_spatial"], p["b_spatial"], CHUNK, CHUNK)
    consts = _front_list(fw, wbd, bias_sp) + list(_ssd_consts(CHUNK)) + [fw["dskip_e"], fw["g_ob"]]
    mixed, conv_new, ssm_new = pl.pallas_call(
        functools.partial(_prompt_mixer_kernel, tps),
        out_shape=(jax.ShapeDtypeStruct((nt * tile, D_MODEL), BF16),
                   jax.ShapeDtypeStruct((nseq, CONV_K - 1, CONV_DIM), F32),
                   jax.ShapeDtypeStruct((nseq, B_WIDTH, SSD_STATE), F32)),
        grid=(nt,),
        in_specs=[pl.BlockSpec((1, tile, D_MODEL), lambda i: (i, 0, 0)), _ada_spec(sh, 1, tps), _ada_spec(sc, 1, tps)]
                 + [_const_spec(a) for a in consts],
        out_specs=(pl.BlockSpec((tile, D_MODEL), lambda i: (i, 0)),
                   pl.BlockSpec((1, CONV_K - 1, CONV_DIM), lambda i: (i // tps, 0, 0)),
                   pl.BlockSpec((1, B_WIDTH, SSD_STATE), lambda i: (i // tps, 0, 0))),
        scratch_shapes=[pltpu.VMEM((PROMPT_SPLITS, 1, tile // PROMPT_SPLITS + CONV_PAD, CONV_DIM), F32),
                        pltpu.VMEM((1, CONV_PAD, CONV_DIM), F32),
                        pltpu.VMEM((SSD_STATE, B_WIDTH), F32)],
        compiler_params=pltpu.CompilerParams(dimension_semantics=("arbitrary",), vmem_limit_bytes=VMEM_LIMIT),
        name="prompt_mixer",
    )(x4, sh[0], sc[0], *consts)
    return mixed, conv_new, ssm_new.reshape(nseq, SSD_HEADS, SSD_HEAD_DIM, SSD_STATE)


def _sample_mixer(x, sh, sc, state_ssm, state_conv, fw, p):
    nseq, l, _ = x.shape
    sb = SAMPLE_SEQ_TILE
    tm = sb * l
    assert tm == CHUNK
    wbd, bias_sp = _spatial_consts(p["w_spatial"], p["b_spatial"], l, tm)
    selseq = jnp.asarray((np.arange(tm)[None, :] // l) == np.arange(sb)[:, None], BF16)
    consts = _front_list(fw, wbd, bias_sp) + list(_ssd_consts(l)) + [fw["dskip_e"], fw["g_ob"], selseq]
    prev = jnp.pad(state_conv, ((0, 0), (CONV_PAD - (CONV_K - 1), 0), (0, 0)))
    ssm_spec = pl.BlockSpec((sb, SSD_HEADS, SSD_HEAD_DIM, SSD_STATE), lambda i: (i, 0, 0, 0))
    return pl.pallas_call(
        _sample_mixer_kernel,
        out_shape=(jax.ShapeDtypeStruct((nseq * l, D_MODEL), BF16),
                   jax.ShapeDtypeStruct((nseq * l, A_WIDTH), F32),
                   jax.ShapeDtypeStruct((nseq, CONV_K - 1, CONV_DIM), F32),
                   jax.ShapeDtypeStruct(state_ssm.shape, F32)),
        grid=(nseq // sb,),
        in_specs=[pl.BlockSpec((sb, l, D_MODEL), lambda i: (i, 0, 0)), _ada_spec(sh, sb, 1), _ada_spec(sc, sb, 1),
                  pl.BlockSpec((sb, CONV_PAD, CONV_DIM), lambda i: (i, 0, 0)), ssm_spec]
                 + [_const_spec(a) for a in consts],
        out_specs=(pl.BlockSpec((tm, D_MODEL), lambda i: (i, 0)),
                   pl.BlockSpec((tm, A_WIDTH), lambda i: (i, 0)),
                   pl.BlockSpec((sb, CONV_K - 1, CONV_DIM), lambda i: (i, 0, 0)),
                   ssm_spec),
        scratch_shapes=[pltpu.VMEM((sb, l + CONV_PAD, CONV_DIM), F32),
                        pltpu.VMEM((tm, B_WIDTH), F32),
                        pltpu.VMEM((tm, SSD_GROUPS * SSD_STATE), F32),
                        pltpu.VMEM((tm, SSD_GROUPS * SSD_STATE), F32),
                        pltpu.VMEM((SSD_GROUPS, GROUP_W, tm), BF16),
                        pltpu.VMEM((SSD_HEADS, sb, LANES), F32)],
        compiler_params=pltpu.CompilerParams(dimension_semantics=("arbitrary",), vmem_limit_bytes=VMEM_LIMIT),
        name="sample_mixer",
    )(x, sh[0], sc[0], prev, state_ssm, *consts)


def _post_kernel(mixed_ref, x_ref, gt_ref, sc_ref, sh_ref, w_out_ref, g_ffn_ref, wr_both_ref, br_ref,
                 x1_ref, h2p_ref, ids_ref):
    x3 = x_ref[...]
    sb, l, _ = x3.shape
    tm = sb * l
    mix = _dot(mixed_ref[...], w_out_ref[...]).reshape(sb, l, D_MODEL)
    x1 = x3 + gt_ref[...] * mix
    x1_ref[...] = x1
    xn = x1 * lax.rsqrt(jnp.mean(x1 * x1, axis=-1, keepdims=True) + EPS) * g_ffn_ref[...]
    h2 = (xn * (1.0 + sc_ref[...]) + sh_ref[...]).reshape(tm, D_MODEL)
    h_hi, h_lo = _split(h2)
    h2p_ref[:, :D_MODEL] = h_hi
    both = _dot_nt(wr_both_ref[...], h_hi)
    logits = (both[:N_EXPERTS] + both[N_EXPERTS:] + _dot_nt(wr_both_ref[:N_EXPERTS, :], h_lo)
              + br_ref[...])
    e_iota = lax.broadcasted_iota(I32, logits.shape, 0)
    vals, idxs = [], []
    for _ in range(TOP_K):
        m = jnp.max(logits, axis=0, keepdims=True)
        idx = jnp.min(jnp.where(logits == m, e_iota, N_EXPERTS), axis=0, keepdims=True)
        vals.append(m)
        idxs.append(idx)
        logits = jnp.where(e_iota == idx, -jnp.inf, logits)
    ex = [jnp.exp(v - vals[0]) for v in vals]
    tot = ex[0] + ex[1] + ex[2] + ex[3]
    ids = jnp.concatenate(idxs, axis=0)
    ids_ref[...] = ids
    wts = jnp.concatenate([e / tot for e in ex], axis=0)
    w_hi = wts.astype(BF16).astype(F32)
    info = jnp.concatenate([ids.astype(F32), w_hi, wts - w_hi, jnp.zeros((TOP_K, tm), F32)], axis=0).astype(BF16)
    r = lax.broadcasted_iota(I32, (INFO_ROWS, LANES), 0)
    c = lax.broadcasted_iota(I32, (INFO_ROWS, LANES), 1)
    place = jnp.where(r == c, 1.0, 0.0).astype(BF16)
    h2p_ref[:, D_MODEL:] = lax.dot_general(info, place, (((0,), (0,)), ((), ())),
                                           preferred_element_type=F32).astype(BF16)


def _post(mixed, x, gt, sc, sh, w_out_b, g_ffn, wr_both, br, sb, l, seq_div):
    n3, _, _ = x.shape
    nblk = n3 // sb
    tm = sb * l
    t = n3 * l
    consts = [w_out_b, g_ffn, wr_both, br]
    return pl.pallas_call(
        _post_kernel,
        out_shape=(jax.ShapeDtypeStruct(x.shape, F32),
                   jax.ShapeDtypeStruct((t, ROW_W), BF16),
                   jax.ShapeDtypeStruct((TOP_K, t), I32)),
        grid=(nblk,),
        in_specs=[pl.BlockSpec((tm, D_MODEL), lambda i: (i, 0)),
                  pl.BlockSpec((sb, l, D_MODEL), lambda i: (i, 0, 0)),
                  _ada_spec(gt, sb, seq_div), _ada_spec(sc, sb, seq_div), _ada_spec(sh, sb, seq_div)]
                 + [_const_spec(a) for a in consts],
        out_specs=(pl.BlockSpec((sb, l, D_MODEL), lambda i: (i, 0, 0)),
                   pl.BlockSpec((tm, ROW_W), lambda i: (i, 0)),
                   pl.BlockSpec((TOP_K, tm), lambda i: (0, i))),
        compiler_params=pltpu.CompilerParams(dimension_semantics=("arbitrary",), vmem_limit_bytes=VMEM_LIMIT),
        name="post",
    )(mixed, x, gt[0], sc[0], sh[0], *consts)


def _strict_upper(n):
    r = lax.broadcasted_iota(I32, (n, n), 0)
    c = lax.broadcasted_iota(I32, (n, n), 1)
    return jnp.where(r < c, 1.0, 0.0).astype(BF16)


def _expert_prefix(col):
    r = lax.broadcasted_iota(I32, (N_EXPERTS, N_EXPERTS), 0)
    c = lax.broadcasted_iota(I32, (N_EXPERTS, N_EXPERTS), 1)
    as_row = jnp.sum(jnp.where(r == c, col, 0.0), axis=0, keepdims=True)
    return jnp.sum(jnp.where(c < r, as_row, 0.0), axis=1, keepdims=True)


def _plan_kernel(dump_group, ids_ref, lr_ref, gdst_ref, tile_e_ref):
    tm = TOK_TILE
    nb = ids_ref.shape[1] // tm
    blk_lane = lax.broadcasted_iota(I32, (N_EXPERTS, LANES), 1)

    def block_masks(b):
        ids = ids_ref[:, pl.ds(pl.multiple_of(b * tm, tm), tm)]
        e_iota = lax.broadcasted_iota(I32, (N_EXPERTS, tm), 0)
        onehot = [ids[k:k + 1, :] == e_iota for k in range(TOP_K)]
        sel = (onehot[0] | onehot[1]) | (onehot[2] | onehot[3])
        return onehot, jnp.where(sel, 1.0, 0.0)

    def count(b, seg):
        _, m = block_masks(b)
        seg_b = jnp.ceil(jnp.sum(m, axis=1, keepdims=True) * (1.0 / ROW_GROUP))
        return jnp.where(blk_lane == b, seg_b, seg)

    seg = lax.fori_loop(0, nb, count, jnp.zeros((N_EXPERTS, LANES), F32))
    tot = jnp.sum(seg, axis=1, keepdims=True)
    padded = jnp.ceil(tot * (1.0 / TILE_GROUPS)) * TILE_GROUPS
    gstart = _expert_prefix(padded)
    gb = gstart + _dot(seg.astype(BF16), _strict_upper(LANES))
    r = lax.broadcasted_iota(I32, (N_EXPERTS, LANES), 0)
    as_row = lambda col: jnp.sum(jnp.where(r == blk_lane, col, 0.0), axis=0, keepdims=True)
    n_used = jnp.sum(padded, axis=0, keepdims=True)
    lane = lax.broadcasted_iota(I32, (1, LANES), 1)
    tiles = jnp.where(lane == N_EXPERTS, n_used, as_row(gstart)) * (1.0 / TILE_GROUPS)
    meta = jnp.concatenate([tiles, as_row(gstart + tot), as_row(padded - tot),
                            jnp.zeros((META_ROWS - 3, LANES), F32)], axis=0)
    tile_e_ref[...] = meta.astype(I32)
    upper = _strict_upper(tm)

    def place(b, carry):
        onehot, m = block_masks(b)
        seg_b = jnp.sum(jnp.where(blk_lane == b, seg, 0.0), axis=1, keepdims=True)
        gb_b = jnp.sum(jnp.where(blk_lane == b, gb, 0.0), axis=1, keepdims=True)
        loc_b = _expert_prefix(seg_b)
        before = _dot(m.astype(BF16), upper) + loc_b * ROW_GROUP
        lr_ref[:, pl.ds(pl.multiple_of(b * tm, tm), tm)] = jnp.concatenate(
            [jnp.sum(jnp.where(onehot[k], before, 0.0), axis=0, keepdims=True) for k in range(TOP_K)],
            axis=0).astype(I32)
        g = lax.broadcasted_iota(I32, (N_EXPERTS, GDST_LANES), 1).astype(F32)
        inside = (loc_b <= g) & (g < loc_b + seg_b)
        dst = jnp.sum(jnp.where(inside, gb_b + g - loc_b, 0.0), axis=0, keepdims=True)
        used = jnp.sum(jnp.where(inside, 1.0, 0.0), axis=0, keepdims=True) > 0.5
        dump = dump_group + lax.convert_element_type(b % 2, F32) * LOCAL_GROUPS + g[0:1, :]
        n_local = jnp.sum(seg_b, axis=0, keepdims=True)
        gdst_ref[b] = jnp.where(g[0:1, :] == GDST_LANES - 1, n_local, jnp.where(used, dst, dump)).astype(I32)
        return carry

    lax.fori_loop(0, nb, place, 0)


def _plan(ids, dump_group):
    t = ids.shape[1]
    nb = t // TOK_TILE
    assert nb <= LANES
    return pl.pallas_call(
        functools.partial(_plan_kernel, float(dump_group)),
        out_shape=(jax.ShapeDtypeStruct((TOP_K, t), I32),
                   jax.ShapeDtypeStruct((nb, 1, GDST_LANES), I32),
                   jax.ShapeDtypeStruct((META_ROWS, LANES), I32)),
        name="plan",
    )(ids)


def _sort_matrix(lr, c):
    r_iota = lax.broadcasted_iota(I32, (MASK_ROWS, lr.shape[1]), 0) + c * MASK_ROWS
    p = jnp.where(r_iota == lr[TOP_K - 1:TOP_K, :], 1.0, 0.0)
    for k in range(TOP_K - 1):
        p = jnp.where(r_iota == lr[k:k + 1, :], 1.0, p)
    return p.astype(BF16)


def _group_copies(loc_ref, slot, far_ref, gdst_ref, sem, to_far):
    copies = []
    for g in range(LOCAL_GROUPS):
        dst = pl.multiple_of(gdst_ref[0, 0, g] * ROW_GROUP, ROW_GROUP)
        near = loc_ref.at[slot, pl.ds(g * ROW_GROUP, ROW_GROUP)]
        far = far_ref.at[pl.ds(dst, ROW_GROUP)]
        copies.append(pltpu.make_async_copy(near, far, sem.at[slot]) if to_far
                      else pltpu.make_async_copy(far, near, sem.at[slot]))
    return copies


N_CHUNKS = LOCAL_ROWS // MASK_ROWS
CHUNK_GROUPS = LOCAL_GROUPS // N_CHUNKS
SURE_GROUPS = LOCAL_GROUPS - CHUNK_GROUPS


def _start_all(copies):
    for c in copies:
        c.start()


def _uses_last_chunk(gdst_ref):
    return gdst_ref[0, 0, GDST_LANES - 1] > SURE_GROUPS


def _wait_groups(loc_ref, slot, sem, with_last):
    def wait(n_groups):
        part = loc_ref.at[slot, pl.ds(0, n_groups * ROW_GROUP)]
        pltpu.make_async_copy(part, part, sem.at[slot]).wait()

    pl.when(with_last)(functools.partial(wait, LOCAL_GROUPS))
    pl.when(jnp.logical_not(with_last))(functools.partial(wait, SURE_GROUPS))


def _dispatch_kernel(n_first, n_tiles_all, gdst_ref, gdst_prev_ref, meta_ref, lr_ref, ha_ref, hb_ref, xs_ref,
                     h_ref, loc_ref, zero_ref, sem, zsem):
    i = pl.program_id(0)
    last = pl.num_programs(0) - 1
    slot = i % 2
    with_last = _uses_last_chunk(gdst_ref)

    @pl.when(i < n_first)
    def _():
        h_ref[...] = ha_ref[...]

    @pl.when(i >= n_first)
    def _():
        h_ref[...] = hb_ref[...]

    lr = lr_ref[...]
    copies = _group_copies(loc_ref, slot, xs_ref, gdst_ref, sem, True)

    def sort_chunk(c):
        loc_ref[slot, c * MASK_ROWS:(c + 1) * MASK_ROWS, :] = _dot(_sort_matrix(lr, c), h_ref[...]).astype(BF16)
        for cp in copies[c * CHUNK_GROUPS:(c + 1) * CHUNK_GROUPS]:
            cp.start()

    for c in range(N_CHUNKS - 1):
        sort_chunk(c)
    pl.when(with_last)(functools.partial(sort_chunk, N_CHUNKS - 1))

    @pl.when(i > 0)
    def _():
        _wait_groups(loc_ref, 1 - slot, sem, _uses_last_chunk(gdst_prev_ref))

    @pl.when(i == last)
    def _():
        _wait_groups(loc_ref, slot, sem, with_last)
        zero_ref[...] = jnp.zeros_like(zero_ref)

        def pad_copy(e, j):
            row = pl.multiple_of((meta_ref[1, e] + j) * ROW_GROUP, ROW_GROUP)
            return pltpu.make_async_copy(zero_ref.at[pl.ds(0, ROW_GROUP)], xs_ref.at[pl.ds(row, ROW_GROUP)], zsem)

        def tile_copy(t):
            row = pl.multiple_of(t * ROW_TILE, ROW_TILE)
            return pltpu.make_async_copy(zero_ref, xs_ref.at[pl.ds(row, ROW_TILE)], zsem)

        def pads(fn):
            def body(e, carry):
                for j in range(TILE_GROUPS - 1):
                    @pl.when(j < meta_ref[2, e])
                    def _():
                        fn(pad_copy(e, j))
                return carry
            lax.fori_loop(0, N_EXPERTS, body, 0)

        def tiles(fn):
            def body(t, carry):
                fn(tile_copy(t))
                return carry
            lax.fori_loop(meta_ref[0, N_EXPERTS], n_tiles_all, body, 0)

        pads(lambda c: c.start())
        tiles(lambda c: c.start())
        pads(lambda c: c.wait())
        tiles(lambda c: c.wait())


def _dispatch(gdst, meta, lr, h_a, h_b, n_tiles_all):
    tm = TOK_TILE
    na, nb2 = h_a.shape[0] // tm, h_b.shape[0] // tm
    return pl.pallas_call(
        functools.partial(_dispatch_kernel, na, n_tiles_all),
        out_shape=jax.ShapeDtypeStruct((n_tiles_all * ROW_TILE, ROW_W), BF16),
        grid=(na + nb2,),
        in_specs=[pl.BlockSpec((1, 1, GDST_LANES), lambda i: (i, 0, 0), memory_space=pltpu.SMEM),
                  pl.BlockSpec((1, 1, GDST_LANES), lambda i: (jnp.maximum(i - 1, 0), 0, 0), memory_space=pltpu.SMEM),
                  pl.BlockSpec((META_ROWS, LANES), lambda i: (0, 0), memory_space=pltpu.SMEM),
                  pl.BlockSpec((TOP_K, tm), lambda i: (0, i)),
                  pl.BlockSpec((tm, ROW_W), lambda i: (jnp.minimum(i, na - 1), 0)),
                  pl.BlockSpec((tm, ROW_W), lambda i: (jnp.maximum(i - na, 0), 0))],
        out_specs=pl.BlockSpec(memory_space=pl.ANY),
        scratch_shapes=[pltpu.VMEM((tm, ROW_W), BF16), pltpu.VMEM((2, LOCAL_ROWS, ROW_W), BF16),
                        pltpu.VMEM((ROW_TILE, ROW_W), BF16),
                        pltpu.SemaphoreType.DMA((2,)), pltpu.SemaphoreType.DMA],
        compiler_params=pltpu.CompilerParams(dimension_semantics=("arbitrary",), vmem_limit_bytes=VMEM_LIMIT),
        name="dispatch",
    )(gdst, gdst, meta, lr, h_a, h_b)


def _expert_kernel(n_tiles_all, ts_ref, xs_ref, wg_ref, wu_ref, wd_ref, bg_ref, bu_ref, bd_ref, y_ref,
                   wstage, wgb, wub, wdb, xbuf, ybuf, sem_w, sem_in, sem_out):
    e = pl.program_id(0)
    t0 = ts_ref[e]
    nt = ts_ref[e + 1] - t0
    wslot = e % 2

    def w_copies(ex, slot):
        return [pltpu.make_async_copy(w_ref.at[ex], wstage.at[slot, j], sem_w.at[slot])
                for j, w_ref in enumerate((wg_ref, wu_ref, wd_ref))]

    def in_copy(t, slot):
        rows = pl.ds(pl.multiple_of((t0 + t) * ROW_TILE, ROW_TILE), ROW_TILE)
        return pltpu.make_async_copy(xs_ref.at[rows], xbuf.at[slot], sem_in.at[slot])

    def out_copy(tile, slot):
        rows = pl.ds(pl.multiple_of(tile * ROW_TILE, ROW_TILE), ROW_TILE)
        return pltpu.make_async_copy(ybuf.at[slot], y_ref.at[rows], sem_out.at[slot])

    @pl.when((e == 0) & (nt > 0))
    def _():
        for c in w_copies(0, 0):
            c.start()

    for j in range(TILE_SLOTS):
        @pl.when(nt > j)
        def _():
            in_copy(j, j).start()

    @pl.when(e + 1 < N_EXPERTS)
    def _():
        @pl.when(ts_ref[e + 2] > ts_ref[e + 1])
        def _():
            for c in w_copies(e + 1, 1 - wslot):
                c.start(priority=1)

    @pl.when(nt > 0)
    def _():
        for c in w_copies(e, wslot):
            c.wait()
        wgb[...] = wstage[wslot, 0].astype(BF16)
        wub[...] = wstage[wslot, 1].astype(BF16)
        wdb[...] = wstage[wslot, 2].astype(BF16)
        e_f = e.astype(F32)

        def acquire(t):
            slot = t % TILE_SLOTS
            in_copy(t, slot).wait()

            @pl.when(t >= TILE_SLOTS)
            def _():
                out_copy(t0 + t - TILE_SLOTS, slot).wait()

        def compute(t):
            slot = t % TILE_SLOTS
            xw = xbuf[slot]
            x = xw[:, :D_MODEL]
            info = xw[:, D_MODEL:].astype(F32)
            w_row = jnp.zeros((ROW_TILE, 1), F32)
            for k in range(TOP_K):
                wk = info[:, TOP_K + k:TOP_K + k + 1] + info[:, 2 * TOP_K + k:2 * TOP_K + k + 1]
                w_row = w_row + jnp.where(info[:, k:k + 1] == e_f, wk, 0.0)
            g = jnp.minimum(_dot(x, wgb[...]) + bg_ref[0], SWIGLU_LIMIT)
            u = jnp.clip(_dot(x, wub[...]) + bu_ref[0], -SWIGLU_LIMIT, SWIGLU_LIMIT)
            act = g * jax.nn.sigmoid(SWIGLU_ALPHA * g) * (u + 1.0)
            ybuf[slot] = ((_dot(act.astype(BF16), wdb[...]) + bd_ref[0]) * w_row).astype(BF16)

        def release(t):
            slot = t % TILE_SLOTS
            out_copy(t0 + t, slot).start()

            @pl.when(t + TILE_SLOTS < nt)
            def _():
                in_copy(t + TILE_SLOTS, slot).start()

        def pair(p, carry):
            ta, tb = 2 * p, 2 * p + 1
            acquire(ta)
            acquire(tb)
            compute(ta)
            compute(tb)
            release(ta)
            release(tb)
            return carry

        lax.fori_loop(0, nt // 2, pair, 0)

        @pl.when(nt % 2 == 1)
        def _():
            acquire(nt - 1)
            compute(nt - 1)
            release(nt - 1)

        for j in range(1, TILE_SLOTS + 1):
            @pl.when(nt >= j)
            def _():
                out_copy(t0 + nt - j, (nt - j) % TILE_SLOTS).wait()

    @pl.when(e == pl.num_programs(0) - 1)
    def _():
        n_used = ts_ref[N_EXPERTS]
        ybuf[0] = jnp.zeros((ROW_TILE, D_MODEL), BF16)

        def zstart(tile, carry):
            out_copy(tile, 0).start()
            return carry

        def zwait(tile, carry):
            out_copy(tile, 0).wait()
            return carry

        lax.fori_loop(n_used, n_tiles_all, zstart, 0)
        lax.fori_loop(n_used, n_tiles_all, zwait, 0)


def _experts(tile_start, xs, w_gate, b_gate, w_up, b_up, w_down, b_down):
    n_rows = xs.shape[0]
    b_spec = pl.BlockSpec((1, 1, D_MODEL), lambda e, ts: (e, 0, 0))
    any_spec = pl.BlockSpec(memory_space=pl.ANY)
    return pl.pallas_call(
        functools.partial(_expert_kernel, n_rows // ROW_TILE),
        out_shape=jax.ShapeDtypeStruct((n_rows, D_MODEL), BF16),
        grid_spec=pltpu.PrefetchScalarGridSpec(
            num_scalar_prefetch=1,
            grid=(N_EXPERTS,),
            in_specs=[any_spec, any_spec, any_spec, any_spec, b_spec, b_spec, b_spec],
            out_specs=any_spec,
            scratch_shapes=[pltpu.VMEM((2, 3, D_MODEL, D_MODEL), F32)]
                           + [pltpu.VMEM((D_MODEL, D_MODEL), BF16)] * 3
                           + [pltpu.VMEM((TILE_SLOTS, ROW_TILE, ROW_W), BF16),
                              pltpu.VMEM((TILE_SLOTS, ROW_TILE, D_MODEL), BF16),
                              pltpu.SemaphoreType.DMA((2,)), pltpu.SemaphoreType.DMA((TILE_SLOTS,)),
                              pltpu.SemaphoreType.DMA((TILE_SLOTS,))],
        ),
        compiler_params=pltpu.CompilerParams(dimension_semantics=("arbitrary",), vmem_limit_bytes=VMEM_LIMIT),
        name="experts",
    )(tile_start, xs, w_gate, w_up, w_down, b_gate[:, None, :], b_up[:, None, :], b_down[:, None, :])


def _combine_kernel(gdst_ref, gdst_next_ref, lr_ref, y_ref, x1_ref, gt_ref, gf_ref, out_ref, loc_ref, sem):
    i = pl.program_id(0)
    slot = i % 2
    x1 = x1_ref[...]
    sb, l, _ = x1.shape

    with_last = _uses_last_chunk(gdst_ref)
    next_with_last = _uses_last_chunk(gdst_next_ref)

    @pl.when(i == 0)
    def _():
        first = _group_copies(loc_ref, slot, y_ref, gdst_ref, sem, False)
        _start_all(first[:SURE_GROUPS])
        pl.when(with_last)(functools.partial(_start_all, first[SURE_GROUPS:]))

    _wait_groups(loc_ref, slot, sem, with_last)

    prefetch = _group_copies(loc_ref, 1 - slot, y_ref, gdst_next_ref, sem, False)
    lr = lr_ref[...]

    def unsort_chunk(c, acc):
        rows = loc_ref[slot, c * MASK_ROWS:(c + 1) * MASK_ROWS, :]
        return acc + lax.dot_general(_sort_matrix(lr, c), rows, (((0,), (0,)), ((), ())),
                                     preferred_element_type=F32)

    moe = jnp.zeros((sb * l, D_MODEL), F32)
    for c in range(N_CHUNKS - 1):
        moe = unsort_chunk(c, moe)
        for cp in prefetch[c * CHUNK_GROUPS:(c + 1) * CHUNK_GROUPS]:
            cp.start()
    moe = lax.cond(with_last, functools.partial(unsort_chunk, N_CHUNKS - 1), lambda acc: acc, moe)
    pl.when(next_with_last)(functools.partial(_start_all, prefetch[SURE_GROUPS:]))
    x2 = x1 + gt_ref[...] * moe.reshape(sb, l, D_MODEL)
    out_ref[...] = x2 * lax.rsqrt(jnp.mean(x2 * x2, axis=-1, keepdims=True) + EPS) * gf_ref[...]

    @pl.when(i == pl.num_programs(0) - 1)
    def _():
        _wait_groups(loc_ref, 1 - slot, sem, next_with_last)


def _combine(gdst, lr, y, x1, gt, g_final, sb, l, seq_div, blk_off):
    n3 = x1.shape[0]
    nblk = n3 // sb
    tm = sb * l
    assert tm == TOK_TILE
    return pl.pallas_call(
        _combine_kernel,
        out_shape=jax.ShapeDtypeStruct(x1.shape, F32),
        grid=(nblk,),
        in_specs=[pl.BlockSpec((1, 1, GDST_LANES), lambda i: (i + blk_off, 0, 0), memory_space=pltpu.SMEM),
                  pl.BlockSpec((1, 1, GDST_LANES), lambda i: (jnp.minimum(i + 1, nblk - 1) + blk_off, 0, 0),
                               memory_space=pltpu.SMEM),
                  pl.BlockSpec((TOP_K, tm), lambda i: (0, i + blk_off)),
                  pl.BlockSpec(memory_space=pl.ANY),
                  pl.BlockSpec((sb, l, D_MODEL), lambda i: (i, 0, 0)),
                  _ada_spec(gt, sb, seq_div),
                  pl.BlockSpec((1, D_MODEL), lambda i: (0, 0))],
        out_specs=pl.BlockSpec((sb, l, D_MODEL), lambda i: (i, 0, 0)),
        scratch_shapes=[pltpu.VMEM((2, LOCAL_ROWS, D_MODEL), BF16), pltpu.SemaphoreType.DMA((2,))],
        compiler_params=pltpu.CompilerParams(dimension_semantics=("arbitrary",), vmem_limit_bytes=VMEM_LIMIT),
        name="combine",
    )(gdst, gdst, lr, y, x1, gt[0], g_final)


def kernel(x_prompt, x_sample, c_prompt, c_sample, state_ssm, state_conv, w_ada, b_ada, g_mix, w_in, g_v_a, w_spatial, b_spatial, g_out_a, conv_w, conv_b, dt_bias, a_log, d_skip, g_out_b, w_out, g_ffn, w_router, b_router, w_gate, b_gate, w_up, b_up, w_down, b_down, g_final):
    assert w_ada.shape[0] == 1, "single-layer step"
    p = dict(w_in=w_in[0], g_mix=g_mix[0], g_v_a=g_v_a[0], w_spatial=w_spatial[0], b_spatial=b_spatial[0],
             g_out_a=g_out_a[0], conv_w=conv_w[0], conv_b=conv_b[0], dt_bias=dt_bias[0], a_log=a_log[0],
             d_skip=d_skip[0], g_out_b=g_out_b[0])
    bp, lp, _ = x_prompt.shape
    bs, ls, _ = x_sample.shape
    tp, ts = bp * lp, bs * ls

    ada = _ada(jnp.concatenate([c_sample, c_prompt], axis=0), w_ada[0], b_ada[0][None, :])
    ada = ada.reshape(bs + bp, 1, 6 * D_MODEL)
    ada_s = [(ada, j, 0) for j in range(6)]
    ada_p = [(ada, j, bs) for j in range(6)]

    fw = _front_weights(p)
    mixed_p, conv_p, ssm_p = _prompt_mixer(x_prompt, ada_p[0], ada_p[1], fw, p)
    mixed_s, v_s, conv_s, ssm_s = _sample_mixer(x_sample, ada_s[0], ada_s[1], state_ssm[0], state_conv[0], fw, p)

    w_out_b = w_out[0].astype(BF16)
    g_ffn2 = g_ffn[0][None, :]
    wr_t = w_router[0].T
    wr_hi = wr_t.astype(BF16)
    wr_both = jnp.concatenate([wr_hi, (wr_t - wr_hi.astype(F32)).astype(BF16)], axis=0)
    br = b_router[0][:, None]
    tps = lp // TOK_TILE
    sbs = TOK_TILE // ls
    xp3 = x_prompt.reshape(tp // POST_TILE, POST_TILE, D_MODEL)
    x1_p, h2p_p, ids_p = _post(mixed_p, xp3, ada_p[2], ada_p[4], ada_p[3], w_out_b, g_ffn2, wr_both, br,
                               1, POST_TILE, lp // POST_TILE)
    x1_p = x1_p.reshape(bp * tps, TOK_TILE, D_MODEL)
    x1_s, h2p_s, ids_s = _post(mixed_s, x_sample, ada_s[2], ada_s[4], ada_s[3], w_out_b, g_ffn2, wr_both, br,
                               POST_TILE // ls, ls, 1)

    n_blocks = (tp + ts) // TOK_TILE
    max_groups = (tp + ts) * TOP_K // ROW_GROUP + n_blocks * N_EXPERTS + N_EXPERTS * (TILE_GROUPS - 1)
    n_tiles = -(-max_groups // TILE_GROUPS)
    n_tiles_all = n_tiles + 2 * LOCAL_GROUPS // TILE_GROUPS
    lr, gdst, meta = _plan(jnp.concatenate([ids_p, ids_s], axis=1), n_tiles * TILE_GROUPS)

    xs = _dispatch(gdst, meta, lr, h2p_p, h2p_s, n_tiles_all)
    y = _experts(meta[0], xs, w_gate[0], b_gate[0], w_up[0], b_up[0], w_down[0], b_down[0])

    gf = g_final[None, :]
    y_p = _combine(gdst, lr, y, x1_p, ada_p[5], gf, 1, TOK_TILE, tps, 0).reshape(bp, lp, D_MODEL)
    y_s = _combine(gdst, lr, y, x1_s, ada_s[5], gf, sbs, ls, 1, tp // TOK_TILE)

    return (y_p, y_s, ssm_p[None], conv_p[None], ssm_s[None], conv_s[None], v_s.reshape(1, bs, ls, A_WIDTH))
```

```python
import functools
import math

import numpy as np
import jax
import jax.numpy as jnp
from jax import lax
from jax.experimental import pallas as pl
from jax.experimental.pallas import tpu as pltpu

F32 = jnp.float32
BF16 = jnp.bfloat16
I32 = jnp.int32

D_MODEL = 1024
A_WIDTH = 512
A_HEADS = 4
A_HEAD_DIM = 128
CHUNK = 128
B_WIDTH = 512
SSD_HEAD_DIM = 64
SSD_HEADS = 8
SSD_GROUPS = 2
SSD_STATE = 128
GROUP_W = B_WIDTH // SSD_GROUPS
CONV_K = 4
CONV_DIM = 1024
CONV_PAD = 8
N_EXPERTS = 32
TOP_K = 4
SWIGLU_LIMIT = 7.0
SWIGLU_ALPHA = 1.702
EPS = 1e-6
DECAY_MASKED = -1e30
LANES = 128

TOK_TILE = 512
SAMPLE_SEQ_TILE = 16
SEQ_UNROLL = 8
ROW_TILE = 256
ROW_GROUP = 16
TILE_GROUPS = ROW_TILE // ROW_GROUP
LOCAL_GROUPS = TOK_TILE * TOP_K // ROW_GROUP + N_EXPERTS
LOCAL_ROWS = LOCAL_GROUPS * ROW_GROUP
GDST_LANES = -(-LOCAL_GROUPS // LANES) * LANES
ROW_W = D_MODEL + LANES
INFO_ROWS = 16
MASK_ROWS = 256
META_ROWS = 8
POST_TILE = 1024
PROMPT_TILE = 512
PROMPT_SPLITS = 2
TILE_SLOTS = 4
VMEM_LIMIT = 56 * 1024 * 1024


def _dot(a, b):
    return jnp.dot(a, b, preferred_element_type=F32)


def _dot_nt(a, b):
    return lax.dot_general(a, b, (((1,), (1,)), ((), ())), preferred_element_type=F32)


def _split(x):
    hi = x.astype(BF16)
    lo = (x - hi.astype(F32)).astype(BF16)
    return hi, lo


def _dot_exact_l(t, x):
    hi, lo = _split(x)
    return _dot(t, hi) + _dot(t, lo)


def _dot_exact_r(x, t):
    hi, lo = _split(x)
    return _dot(hi, t) + _dot(lo, t)


def _silu(x):
    return x * jax.nn.sigmoid(x)


def _gelu(x):
    return 0.5 * x * (1.0 + lax.erf(x * (1.0 / math.sqrt(2.0))))


def _softplus(x):
    return jnp.maximum(x, 0.0) + jnp.log1p(jnp.exp(-jnp.abs(x)))


def _rms(x, g):
    return x * lax.rsqrt(jnp.mean(x * x, axis=-1, keepdims=True) + EPS) * g


def _ada_kernel(c_ref, w_ref, b_ref, o_ref):
    s_hi, s_lo = _split(_silu(c_ref[...]))
    w_hi, w_lo = _split(w_ref[...])
    o_ref[...] = _dot(s_hi, w_hi) + _dot(s_lo, w_hi) + _dot(s_hi, w_lo) + b_ref[...]


def _ada(c_all, w_ada, b_ada):
    m = c_all.shape[0]
    n = w_ada.shape[1]
    bn = 1024
    return pl.pallas_call(
        _ada_kernel,
        out_shape=jax.ShapeDtypeStruct((m, n), F32),
        grid=(n // bn,),
        in_specs=[pl.BlockSpec((m, D_MODEL), lambda j: (0, 0)),
                  pl.BlockSpec((D_MODEL, bn), lambda j: (0, j)),
                  pl.BlockSpec((1, bn), lambda j: (0, j))],
        out_specs=pl.BlockSpec((m, bn), lambda j: (0, j)),
        compiler_params=pltpu.CompilerParams(dimension_semantics=("arbitrary",), vmem_limit_bytes=VMEM_LIMIT),
        name="ada",
    )(c_all, w_ada, b_ada)


def _mixer_front(x3, sh, sc, prev, refs, xp_ref):
    (g_mix, w_uvz, w_xbc, w_dt, wbd, bias_sp, g_v, g_oa, conv_w, conv_b, dt_bias, a_row) = refs
    sb, l, _ = x3.shape
    tm = sb * l
    xn = x3 * lax.rsqrt(jnp.mean(x3 * x3, axis=-1, keepdims=True) + EPS) * g_mix[...]
    h = (xn * (1.0 + sc) + sh).reshape(tm, D_MODEL)
    hb = h.astype(BF16)
    uvz = _dot(hb, w_uvz[...])
    xbc = _dot(hb, w_xbc[...])
    dt_raw = _dot(hb, w_dt[...])

    u = _gelu(uvz[:, :A_WIDTH])
    vg = _gelu(uvz[:, A_WIDTH:2 * A_WIDTH])
    z = uvz[:, 2 * A_WIDTH:]
    v_parts, s_parts = [], []
    for hd in range(A_HEADS):
        sl = slice(hd * A_HEAD_DIM, (hd + 1) * A_HEAD_DIM)
        vh = _rms(vg[:, sl], g_v[:, sl])
        v_parts.append(vh)
        vb = vh.astype(BF16)
        s_parts.append(jnp.concatenate([_dot(wbd[hd], vb[r0:r0 + CHUNK]) + bias_sp[:, sl]
                                        for r0 in range(0, tm, CHUNK)], axis=0))
    v = jnp.concatenate(v_parts, axis=1)
    s_a = jnp.concatenate(s_parts, axis=1)
    out_a = _rms(u * s_a, g_oa[...])

    xp_ref[:, 0:CONV_PAD, :] = prev
    xp_ref[:, CONV_PAD:, :] = xbc.reshape(sb, l, CONV_DIM)
    xp = xp_ref[...]
    acc = conv_b[...] + xp[:, CONV_PAD:, :] * conv_w[CONV_K - 1:CONV_K, :]
    for s in range(1, CONV_K):
        back = pltpu.roll(xp, s, axis=1)[:, CONV_PAD:, :]
        acc = acc + back * conv_w[CONV_K - 1 - s:CONV_K - s, :]
    xc = _silu(acc).reshape(tm, CONV_DIM)
    dt = _softplus(dt_raw + dt_bias[...])
    d_a = dt * a_row[...]
    return out_a, v, z, xc, dt, d_a


def _ssd_chunk(xs, bm, cm, dt, d_a, cref):
    tril, ones, expand, neg_mask = cref
    cs = _dot_exact_l(tril[...], d_a)
    cs_t = cs.T
    dt_t = dt.T
    cs_tot = _dot_exact_l(ones[...], d_a)
    vals = jnp.concatenate([dt * jnp.exp(cs_tot - cs), jnp.exp(cs)], axis=0)
    vals_e = _dot_exact_r(vals, expand[...])
    n = xs.shape[0]
    w_e, e_e = vals_e[:n], vals_e[n:]
    xdtd = xs * w_e
    neg = neg_mask[...]
    row_lt_half = lax.broadcasted_iota(I32, (2 * n, LANES), 0) < n
    lane_lt_half = lax.broadcasted_iota(I32, (2 * n, LANES), 1) < SSD_HEAD_DIM
    y_parts = []
    for g in range(SSD_GROUPS):
        cb = _dot_nt(cm[:, g * SSD_STATE:(g + 1) * SSD_STATE].astype(BF16),
                     bm[:, g * SSD_STATE:(g + 1) * SSD_STATE].astype(BF16))
        for hp in range(SSD_HEADS // SSD_GROUPS // 2):
            h0 = g * (SSD_HEADS // SSD_GROUPS) + 2 * hp
            ms = []
            for hh in (h0, h0 + 1):
                diff = cs[:, hh:hh + 1] - cs_t[hh:hh + 1, :]
                ms.append((cb * jnp.exp(diff + neg) * dt_t[hh:hh + 1, :]).astype(BF16))
            pair = xs[:, h0 * SSD_HEAD_DIM:(h0 + 2) * SSD_HEAD_DIM]
            rhs = jnp.where(row_lt_half == lane_lt_half, jnp.concatenate([pair, pair], axis=0), 0.0).astype(BF16)
            y_parts.append(_dot(jnp.concatenate(ms, axis=1), rhs))
    y_diag = jnp.concatenate(y_parts, axis=1)
    return y_diag, e_e, xdtd, cs_tot


def _mixer_back(y, xs, z, out_a, dskip_e, g_ob):
    y = y + xs * dskip_e
    gated = y * _silu(z)
    parts = [_rms(gated[:, g * GROUP_W:(g + 1) * GROUP_W], g_ob[:, g * GROUP_W:(g + 1) * GROUP_W])
             for g in range(SSD_GROUPS)]
    return jnp.concatenate([out_a] + parts, axis=1).astype(BF16)


N_FRONT = 12
N_SSD = 4


def _prompt_mixer_kernel(tiles_per_seq, x_ref, sh_ref, sc_ref, *rest):
    front = rest[:N_FRONT]
    cref = rest[N_FRONT:N_FRONT + N_SSD]
    dskip_e, g_ob = rest[N_FRONT + N_SSD:N_FRONT + N_SSD + 2]
    mixed_ref, conv_out_ref, ssm_out_ref = rest[N_FRONT + N_SSD + 2:N_FRONT + N_SSD + 5]
    xp_ref, carry_ref, st_ref = rest[N_FRONT + N_SSD + 5:]
    i = pl.program_id(0)
    first = (i % tiles_per_seq) == 0

    @pl.when(first)
    def _():
        carry_ref[...] = jnp.zeros_like(carry_ref)
        st_ref[...] = jnp.zeros_like(st_ref)

    l = x_ref.shape[1]
    hl = l // PROMPT_SPLITS
    parts = []
    prev = carry_ref[...]
    for hh in range(PROMPT_SPLITS):
        xp_h = xp_ref.at[hh]
        parts.append(_mixer_front(x_ref[:, hh * hl:(hh + 1) * hl, :], sh_ref[...], sc_ref[...], prev, front, xp_h))
        prev = xp_h[:, hl:hl + CONV_PAD, :]
    carry_ref[...] = prev
    out_a, _, z, xc, dt, d_a = [jnp.concatenate([p[j] for p in parts], axis=0) for j in range(6)]
    xs = xc[:, :B_WIDTH]
    y_rows = []
    for c in range(l // CHUNK):
        r = slice(c * CHUNK, (c + 1) * CHUNK)
        bm = xc[r, B_WIDTH:B_WIDTH + SSD_GROUPS * SSD_STATE]
        cm = xc[r, B_WIDTH + SSD_GROUPS * SSD_STATE:]
        y_diag, e_e, xdtd, _ = _ssd_chunk(xs[r], bm, cm, dt[r], d_a[r], cref)
        st = st_ref[...]
        y_off, upd = [], []
        for g in range(SSD_GROUPS):
            gs = slice(g * GROUP_W, (g + 1) * GROUP_W)
            ns = slice(g * SSD_STATE, (g + 1) * SSD_STATE)
            y_off.append(_dot(cm[:, ns].astype(BF16), st[:, gs].astype(BF16)))
            upd.append(_dot(bm[:, ns].T.astype(BF16), xdtd[:, gs].astype(BF16)))
        y_rows.append(y_diag + jnp.concatenate(y_off, axis=1) * e_e)
        st_ref[...] = st * e_e[CHUNK - 1:CHUNK, :] + jnp.concatenate(upd, axis=1)
    y = jnp.concatenate(y_rows, axis=0)
    mixed_ref[...] = _mixer_back(y, xs, z, out_a, dskip_e[...], g_ob[...])

    @pl.when((i % tiles_per_seq) == tiles_per_seq - 1)
    def _():
        conv_out_ref[...] = xp_ref[PROMPT_SPLITS - 1, :, hl + CONV_PAD - (CONV_K - 1):hl + CONV_PAD, :]
        ssm_out_ref[0] = st_ref[...].T


def _sample_mixer_kernel(x_ref, sh_ref, sc_ref, prev_ref, ssm0_ref, *rest):
    front = rest[:N_FRONT]
    cref = rest[N_FRONT:N_FRONT + N_SSD]
    dskip_e, g_ob, selseq = rest[N_FRONT + N_SSD:N_FRONT + N_SSD + 3]
    mixed_ref, v_ref, conv_out_ref, ssm_out_ref = rest[N_FRONT + N_SSD + 3:N_FRONT + N_SSD + 7]
    xp_ref, yoff_ref, cbf_ref, bbf_ref, t1_ref, dtab_ref = rest[N_FRONT + N_SSD + 7:]
    x3 = x_ref[...]
    sb, l, _ = x3.shape
    tm = sb * l
    out_a, v, z, xc, dt, d_a = _mixer_front(x3, sh_ref[...], sc_ref[...], prev_ref[...], front, xp_ref)
    v_ref[...] = v
    conv_out_ref[...] = xp_ref[:, l + CONV_PAD - (CONV_K - 1):l + CONV_PAD, :]
    xs = xc[:, :B_WIDTH]
    bm = xc[:, B_WIDTH:B_WIDTH + SSD_GROUPS * SSD_STATE]
    cm = xc[:, B_WIDTH + SSD_GROUPS * SSD_STATE:]
    y_diag, e_e, xdtd, _ = _ssd_chunk(xs, bm, cm, dt, d_a, cref)

    e_tot = jnp.exp(_dot_exact_l(selseq[...], d_a))
    for hh in range(SSD_HEADS):
        dtab_ref[hh] = jnp.broadcast_to(e_tot[:, hh:hh + 1], (sb, LANES))
    cbf_ref[...] = cm
    bbf_ref[...] = bm
    for g in range(SSD_GROUPS):
        t1_ref[g] = xdtd[:, g * GROUP_W:(g + 1) * GROUP_W].T.astype(BF16)
    seq_of_row = lax.broadcasted_iota(I32, (tm, SSD_STATE), 0) // l
    heads_per_group = SSD_HEADS // SSD_GROUPS

    def one_seq(j):
        r0 = pl.multiple_of(j * l, l)
        s0 = ssm0_ref[j]
        for g in range(SSD_GROUPS):
            ns = slice(g * SSD_STATE, (g + 1) * SSD_STATE)
            s0g = s0[g * heads_per_group:(g + 1) * heads_per_group].reshape(GROUP_W, SSD_STATE)
            cj = cbf_ref[pl.ds(r0, l), ns].astype(BF16)
            yoff_ref[pl.ds(r0, l), g * GROUP_W:(g + 1) * GROUP_W] = _dot_nt(cj, s0g.astype(BF16))
            bmask = jnp.where(seq_of_row == j, bbf_ref[:, ns], 0.0).astype(BF16)
            upd = _dot(t1_ref[g], bmask)
            for hq in range(heads_per_group):
                hh = g * heads_per_group + hq
                dec = dtab_ref[hh, pl.ds(j, 1), :]
                ssm_out_ref[j, hh] = s0[hh] * dec + upd[hq * SSD_HEAD_DIM:(hq + 1) * SSD_HEAD_DIM]

    def body(jj, carry):
        for u in range(SEQ_UNROLL):
            one_seq(jj * SEQ_UNROLL + u)
        return carry

    lax.fori_loop(0, sb // SEQ_UNROLL, body, 0)
    y = y_diag + yoff_ref[...] * e_e
    mixed_ref[...] = _mixer_back(y, xs, z, out_a, dskip_e[...], g_ob[...])


def _ada_spec(ref, sb, div):
    _, term, row0 = ref
    assert row0 % sb == 0
    return pl.BlockSpec((sb, 1, D_MODEL), lambda i: (row0 // sb + i // div, 0, term))


def _const_spec(a):
    nd = a.ndim
    return pl.BlockSpec(a.shape, lambda i, _nd=nd: (0,) * _nd)


def _spatial_consts(w_spatial, b_spatial, cl, tm):
    w = jnp.where(jnp.tril(jnp.ones((cl, cl), bool)), w_spatial[:, :cl, :cl], 0.0)
    eye = jnp.eye(tm // cl, dtype=F32)
    wbd = jnp.einsum("ab,hts->hatbs", eye, w).reshape(A_HEADS, tm, tm).astype(BF16)
    bias = jnp.tile(jnp.repeat(b_spatial[:, :cl].T, A_HEAD_DIM, axis=1), (tm // cl, 1))
    return wbd, bias


def _ssd_consts(cl):
    r = np.arange(CHUNK)
    same = (r[:, None] // cl) == (r[None, :] // cl)
    tril = same & (r[:, None] >= r[None, :])
    expand = np.zeros((LANES, B_WIDTH), np.float32)
    for hh in range(SSD_HEADS):
        expand[hh, hh * SSD_HEAD_DIM:(hh + 1) * SSD_HEAD_DIM] = 1.0
    return (jnp.asarray(tril, BF16), jnp.asarray(same, BF16), jnp.asarray(expand, BF16),
            jnp.asarray(np.where(tril, 0.0, DECAY_MASKED), F32))


def _front_weights(p):
    w_in = p["w_in"]
    c0, c1 = 3 * A_WIDTH, 3 * A_WIDTH + CONV_DIM
    w_dt = w_in[:, c1:]
    pad8 = lambda v: jnp.pad(v, (0, LANES - SSD_HEADS))
    a = -jnp.exp(p["a_log"])
    return dict(
        g_mix=p["g_mix"][None, :],
        w_uvz=w_in[:, :c0].astype(BF16),
        w_xbc=w_in[:, c0:c1].astype(BF16),
        w_dt=jnp.pad(w_dt, ((0, 0), (0, LANES - SSD_HEADS))).astype(BF16),
        g_v=p["g_v_a"][None, :], g_oa=p["g_out_a"][None, :],
        conv_w=p["conv_w"], conv_b=p["conv_b"][None, :],
        dt_bias=pad8(p["dt_bias"])[None, :],
        a_row=pad8(a)[None, :],
        dskip_e=jnp.repeat(p["d_skip"], SSD_HEAD_DIM)[None, :],
        g_ob=p["g_out_b"][None, :],
    )


def _front_list(fw, wbd, bias_sp):
    return [fw["g_mix"], fw["w_uvz"], fw["w_xbc"], fw["w_dt"], wbd, bias_sp, fw["g_v"], fw["g_oa"],
            fw["conv_w"], fw["conv_b"], fw["dt_bias"], fw["a_row"]]


def _prompt_mixer(x, sh, sc, fw, p):
    nseq, lseq, _ = x.shape
    tile = PROMPT_TILE
    tps = lseq // tile
    nt = nseq * tps
    x4 = x.reshape(nt, tile, D_MODEL)
    wbd, bias_sp = _spatial_consts(p["w_spatial"], p["b_spatial"], CHUNK, CHUNK)
    consts = _front_list(fw, wbd, bias_sp) + list(_ssd_consts(CHUNK)) + [fw["dskip_e"], fw["g_ob"]]
    mixed, conv_new, ssm_new = pl.pallas_call(
        functools.partial(_prompt_mixer_kernel, tps),
        out_shape=(jax.ShapeDtypeStruct((nt * tile, D_MODEL), BF16),
                   jax.ShapeDtypeStruct((nseq, CONV_K - 1, CONV_DIM), F32),
                   jax.ShapeDtypeStruct((nseq, B_WIDTH, SSD_STATE), F32)),
        grid=(nt,),
        in_specs=[pl.BlockSpec((1, tile, D_MODEL), lambda i: (i, 0, 0)), _ada_spec(sh, 1, tps), _ada_spec(sc, 1, tps)]
                 + [_const_spec(a) for a in consts],
        out_specs=(pl.BlockSpec((tile, D_MODEL), lambda i: (i, 0)),
                   pl.BlockSpec((1, CONV_K - 1, CONV_DIM), lambda i: (i // tps, 0, 0)),
                   pl.BlockSpec((1, B_WIDTH, SSD_STATE), lambda i: (i // tps, 0, 0))),
        scratch_shapes=[pltpu.VMEM((PROMPT_SPLITS, 1, tile // PROMPT_SPLITS + CONV_PAD, CONV_DIM), F32),
                        pltpu.VMEM((1, CONV_PAD, CONV_DIM), F32),
                        pltpu.VMEM((SSD_STATE, B_WIDTH), F32)],
        compiler_params=pltpu.CompilerParams(dimension_semantics=("arbitrary",), vmem_limit_bytes=VMEM_LIMIT),
        name="prompt_mixer",
    )(x4, sh[0], sc[0], *consts)
    return mixed, conv_new, ssm_new.reshape(nseq, SSD_HEADS, SSD_HEAD_DIM, SSD_STATE)


def _sample_mixer(x, sh, sc, state_ssm, state_conv, fw, p):
    nseq, l, _ = x.shape
    sb = SAMPLE_SEQ_TILE
    tm = sb * l
    assert tm == CHUNK
    wbd, bias_sp = _spatial_consts(p["w_spatial"], p["b_spatial"], l, tm)
    selseq = jnp.asarray((np.arange(tm)[None, :] // l) == np.arange(sb)[:, None], BF16)
    consts = _front_list(fw, wbd, bias_sp) + list(_ssd_consts(l)) + [fw["dskip_e"], fw["g_ob"], selseq]
    prev = jnp.pad(state_conv, ((0, 0), (CONV_PAD - (CONV_K - 1), 0), (0, 0)))
    ssm_spec = pl.BlockSpec((sb, SSD_HEADS, SSD_HEAD_DIM, SSD_STATE), lambda i: (i, 0, 0, 0))
    return pl.pallas_call(
        _sample_mixer_kernel,
        out_shape=(jax.ShapeDtypeStruct((nseq * l, D_MODEL), BF16),
                   jax.ShapeDtypeStruct((nseq * l, A_WIDTH), F32),
                   jax.ShapeDtypeStruct((nseq, CONV_K - 1, CONV_DIM), F32),
                   jax.ShapeDtypeStruct(state_ssm.shape, F32)),
        grid=(nseq // sb,),
        in_specs=[pl.BlockSpec((sb, l, D_MODEL), lambda i: (i, 0, 0)), _ada_spec(sh, sb, 1), _ada_spec(sc, sb, 1),
                  pl.BlockSpec((sb, CONV_PAD, CONV_DIM), lambda i: (i, 0, 0)), ssm_spec]
                 + [_const_spec(a) for a in consts],
        out_specs=(pl.BlockSpec((tm, D_MODEL), lambda i: (i, 0)),
                   pl.BlockSpec((tm, A_WIDTH), lambda i: (i, 0)),
                   pl.BlockSpec((sb, CONV_K - 1, CONV_DIM), lambda i: (i, 0, 0)),
                   ssm_spec),
        scratch_shapes=[pltpu.VMEM((sb, l + CONV_PAD, CONV_DIM), F32),
                        pltpu.VMEM((tm, B_WIDTH), F32),
                        pltpu.VMEM((tm, SSD_GROUPS * SSD_STATE), F32),
                        pltpu.VMEM((tm, SSD_GROUPS * SSD_STATE), F32),
                        pltpu.VMEM((SSD_GROUPS, GROUP_W, tm), BF16),
                        pltpu.VMEM((SSD_HEADS, sb, LANES), F32)],
        compiler_params=pltpu.CompilerParams(dimension_semantics=("arbitrary",), vmem_limit_bytes=VMEM_LIMIT),
        name="sample_mixer",
    )(x, sh[0], sc[0], prev, state_ssm, *consts)


def _post_kernel(mixed_ref, x_ref, gt_ref, sc_ref, sh_ref, w_out_ref, g_ffn_ref, wr_both_ref, br_ref,
                 x1_ref, h2p_ref, ids_ref):
    x3 = x_ref[...]
    sb, l, _ = x3.shape
    tm = sb * l
    mix = _dot(mixed_ref[...], w_out_ref[...]).reshape(sb, l, D_MODEL)
    x1 = x3 + gt_ref[...] * mix
    x1_ref[...] = x1
    xn = x1 * lax.rsqrt(jnp.mean(x1 * x1, axis=-1, keepdims=True) + EPS) * g_ffn_ref[...]
    h2 = (xn * (1.0 + sc_ref[...]) + sh_ref[...]).reshape(tm, D_MODEL)
    h_hi, h_lo = _split(h2)
    h2p_ref[:, :D_MODEL] = h_hi
    both = _dot_nt(wr_both_ref[...], h_hi)
    logits = (both[:N_EXPERTS] + both[N_EXPERTS:] + _dot_nt(wr_both_ref[:N_EXPERTS, :], h_lo)
              + br_ref[...])
    e_iota = lax.broadcasted_iota(I32, logits.shape, 0)
    vals, idxs = [], []
    for _ in range(TOP_K):
        m = jnp.max(logits, axis=0, keepdims=True)
        idx = jnp.min(jnp.where(logits == m, e_iota, N_EXPERTS), axis=0, keepdims=True)
        vals.append(m)
        idxs.append(idx)
        logits = jnp.where(e_iota == idx, -jnp.inf, logits)
    ex = [jnp.exp(v - vals[0]) for v in vals]
    tot = ex[0] + ex[1] + ex[2] + ex[3]
    ids = jnp.concatenate(idxs, axis=0)
    ids_ref[...] = ids
    wts = jnp.concatenate([e / tot for e in ex], axis=0)
    w_hi = wts.astype(BF16).astype(F32)
    info = jnp.concatenate([ids.astype(F32), w_hi, wts - w_hi, jnp.zeros((TOP_K, tm), F32)], axis=0).astype(BF16)
    r = lax.broadcasted_iota(I32, (INFO_ROWS, LANES), 0)
    c = lax.broadcasted_iota(I32, (INFO_ROWS, LANES), 1)
    place = jnp.where(r == c, 1.0, 0.0).astype(BF16)
    h2p_ref[:, D_MODEL:] = lax.dot_general(info, place, (((0,), (0,)), ((), ())),
                                           preferred_element_type=F32).astype(BF16)


def _post(mixed, x, gt, sc, sh, w_out_b, g_ffn, wr_both, br, sb, l, seq_div):
    n3, _, _ = x.shape
    nblk = n3 // sb
    tm = sb * l
    t = n3 * l
    consts = [w_out_b, g_ffn, wr_both, br]
    return pl.pallas_call(
        _post_kernel,
        out_shape=(jax.ShapeDtypeStruct(x.shape, F32),
                   jax.ShapeDtypeStruct((t, ROW_W), BF16),
                   jax.ShapeDtypeStruct((TOP_K, t), I32)),
        grid=(nblk,),
        in_specs=[pl.BlockSpec((tm, D_MODEL), lambda i: (i, 0)),
                  pl.BlockSpec((sb, l, D_MODEL), lambda i: (i, 0, 0)),
                  _ada_spec(gt, sb, seq_div), _ada_spec(sc, sb, seq_div), _ada_spec(sh, sb, seq_div)]
                 + [_const_spec(a) for a in consts],
        out_specs=(pl.BlockSpec((sb, l, D_MODEL), lambda i: (i, 0, 0)),
                   pl.BlockSpec((tm, ROW_W), lambda i: (i, 0)),
                   pl.BlockSpec((TOP_K, tm), lambda i: (0, i))),
        compiler_params=pltpu.CompilerParams(dimension_semantics=("arbitrary",), vmem_limit_bytes=VMEM_LIMIT),
        name="post",
    )(mixed, x, gt[0], sc[0], sh[0], *consts)


def _strict_upper(n):
    r = lax.broadcasted_iota(I32, (n, n), 0)
    c = lax.broadcasted_iota(I32, (n, n), 1)
    return jnp.where(r < c, 1.0, 0.0).astype(BF16)


def _expert_prefix(col):
    r = lax.broadcasted_iota(I32, (N_EXPERTS, N_EXPERTS), 0)
    c = lax.broadcasted_iota(I32, (N_EXPERTS, N_EXPERTS), 1)
    as_row = jnp.sum(jnp.where(r == c, col, 0.0), axis=0, keepdims=True)
    return jnp.sum(jnp.where(c < r, as_row, 0.0), axis=1, keepdims=True)


def _plan_kernel(dump_group, ids_ref, lr_ref, gdst_ref, tile_e_ref):
    tm = TOK_TILE
    nb = ids_ref.shape[1] // tm
    blk_lane = lax.broadcasted_iota(I32, (N_EXPERTS, LANES), 1)

    def block_masks(b):
        ids = ids_ref[:, pl.ds(pl.multiple_of(b * tm, tm), tm)]
        e_iota = lax.broadcasted_iota(I32, (N_EXPERTS, tm), 0)
        onehot = [ids[k:k + 1, :] == e_iota for k in range(TOP_K)]
        sel = (onehot[0] | onehot[1]) | (onehot[2] | onehot[3])
        return onehot, jnp.where(sel, 1.0, 0.0)

    def count(b, seg):
        _, m = block_masks(b)
        seg_b = jnp.ceil(jnp.sum(m, axis=1, keepdims=True) * (1.0 / ROW_GROUP))
        return jnp.where(blk_lane == b, seg_b, seg)

    seg = lax.fori_loop(0, nb, count, jnp.zeros((N_EXPERTS, LANES), F32))
    tot = jnp.sum(seg, axis=1, keepdims=True)
    padded = jnp.ceil(tot * (1.0 / TILE_GROUPS)) * TILE_GROUPS
    gstart = _expert_prefix(padded)
    gb = gstart + _dot(seg.astype(BF16), _strict_upper(LANES))
    r = lax.broadcasted_iota(I32, (N_EXPERTS, LANES), 0)
    as_row = lambda col: jnp.sum(jnp.where(r == blk_lane, col, 0.0), axis=0, keepdims=True)
    n_used = jnp.sum(padded, axis=0, keepdims=True)
    lane = lax.broadcasted_iota(I32, (1, LANES), 1)
    tiles = jnp.where(lane == N_EXPERTS, n_used, as_row(gstart)) * (1.0 / TILE_GROUPS)
    meta = jnp.concatenate([tiles, as_row(gstart + tot), as_row(padded - tot),
                            jnp.zeros((META_ROWS - 3, LANES), F32)], axis=0)
    tile_e_ref[...] = meta.astype(I32)
    upper = _strict_upper(tm)

    def place(b, carry):
        onehot, m = block_masks(b)
        seg_b = jnp.sum(jnp.where(blk_lane == b, seg, 0.0), axis=1, keepdims=True)
        gb_b = jnp.sum(jnp.where(blk_lane == b, gb, 0.0), axis=1, keepdims=True)
        loc_b = _expert_prefix(seg_b)
        before = _dot(m.astype(BF16), upper) + loc_b * ROW_GROUP
        lr_ref[:, pl.ds(pl.multiple_of(b * tm, tm), tm)] = jnp.concatenate(
            [jnp.sum(jnp.where(onehot[k], before, 0.0), axis=0, keepdims=True) for k in range(TOP_K)],
            axis=0).astype(I32)
        g = lax.broadcasted_iota(I32, (N_EXPERTS, GDST_LANES), 1).astype(F32)
        inside = (loc_b <= g) & (g < loc_b + seg_b)
        dst = jnp.sum(jnp.where(inside, gb_b + g - loc_b, 0.0), axis=0, keepdims=True)
        used = jnp.sum(jnp.where(inside, 1.0, 0.0), axis=0, keepdims=True) > 0.5
        dump = dump_group + lax.convert_element_type(b % 2, F32) * LOCAL_GROUPS + g[0:1, :]
        n_local = jnp.sum(seg_b, axis=0, keepdims=True)
        gdst_ref[b] = jnp.where(g[0:1, :] == GDST_LANES - 1, n_local, jnp.where(used, dst, dump)).astype(I32)
        return carry

    lax.fori_loop(0, nb, place, 0)


def _plan(ids, dump_group):
    t = ids.shape[1]
    nb = t // TOK_TILE
    assert nb <= LANES
    return pl.pallas_call(
        functools.partial(_plan_kernel, float(dump_group)),
        out_shape=(jax.ShapeDtypeStruct((TOP_K, t), I32),
                   jax.ShapeDtypeStruct((nb, 1, GDST_LANES), I32),
                   jax.ShapeDtypeStruct((META_ROWS, LANES), I32)),
        name="plan",
    )(ids)


def _sort_matrix(lr, c):
    r_iota = lax.broadcasted_iota(I32, (MASK_ROWS, lr.shape[1]), 0) + c * MASK_ROWS
    p = jnp.where(r_iota == lr[TOP_K - 1:TOP_K, :], 1.0, 0.0)
    for k in range(TOP_K - 1):
        p = jnp.where(r_iota == lr[k:k + 1, :], 1.0, p)
    return p.astype(BF16)


def _group_copies(loc_ref, slot, far_ref, gdst_ref, sem, to_far):
    copies = []
    for g in range(LOCAL_GROUPS):
        dst = pl.multiple_of(gdst_ref[0, 0, g] * ROW_GROUP, ROW_GROUP)
        near = loc_ref.at[slot, pl.ds(g * ROW_GROUP, ROW_GROUP)]
        far = far_ref.at[pl.ds(dst, ROW_GROUP)]
        copies.append(pltpu.make_async_copy(near, far, sem.at[slot]) if to_far
                      else pltpu.make_async_copy(far, near, sem.at[slot]))
    return copies


N_CHUNKS = LOCAL_ROWS // MASK_ROWS
CHUNK_GROUPS = LOCAL_GROUPS // N_CHUNKS
SURE_GROUPS = LOCAL_GROUPS - CHUNK_GROUPS


def _start_all(copies):
    for c in copies:
        c.start()


def _uses_last_chunk(gdst_ref):
    return gdst_ref[0, 0, GDST_LANES - 1] > SURE_GROUPS


def _wait_groups(loc_ref, slot, sem, with_last=None):
    def wait(n_groups):
        part = loc_ref.at[slot, pl.ds(0, n_groups * ROW_GROUP)]
        pltpu.make_async_copy(part, part, sem.at[slot]).wait()

    if with_last is None:
        wait(LOCAL_GROUPS)
        return
    pl.when(with_last)(functools.partial(wait, LOCAL_GROUPS))
    pl.when(jnp.logical_not(with_last))(functools.partial(wait, SURE_GROUPS))


def _dispatch_kernel(n_first, n_tiles_all, gdst_ref, gdst_prev_ref, meta_ref, lr_ref, ha_ref, hb_ref, xs_ref,
                     h_ref, loc_ref, zero_ref, sem, zsem):
    i = pl.program_id(0)
    last = pl.num_programs(0) - 1
    slot = i % 2
    with_last = _uses_last_chunk(gdst_ref)

    @pl.when(i < n_first)
    def _():
        h_ref[...] = ha_ref[...]

    @pl.when(i >= n_first)
    def _():
        h_ref[...] = hb_ref[...]

    lr = lr_ref[...]
    copies = _group_copies(loc_ref, slot, xs_ref, gdst_ref, sem, True)

    def sort_chunk(c):
        loc_ref[slot, c * MASK_ROWS:(c + 1) * MASK_ROWS, :] = _dot(_sort_matrix(lr, c), h_ref[...]).astype(BF16)
        for cp in copies[c * CHUNK_GROUPS:(c + 1) * CHUNK_GROUPS]:
            cp.start()

    for c in range(N_CHUNKS - 1):
        sort_chunk(c)
    pl.when(with_last)(functools.partial(sort_chunk, N_CHUNKS - 1))

    @pl.when(i > 0)
    def _():
        _wait_groups(loc_ref, 1 - slot, sem, _uses_last_chunk(gdst_prev_ref))

    @pl.when(i == last)
    def _():
        _wait_groups(loc_ref, slot, sem, with_last)
        zero_ref[...] = jnp.zeros_like(zero_ref)

        def pad_copy(e, j):
            row = pl.multiple_of((meta_ref[1, e] + j) * ROW_GROUP, ROW_GROUP)
            return pltpu.make_async_copy(zero_ref.at[pl.ds(0, ROW_GROUP)], xs_ref.at[pl.ds(row, ROW_GROUP)], zsem)

        def tile_copy(t):
            row = pl.multiple_of(t * ROW_TILE, ROW_TILE)
            return pltpu.make_async_copy(zero_ref, xs_ref.at[pl.ds(row, ROW_TILE)], zsem)

        def pads(fn):
            def body(e, carry):
                for j in range(TILE_GROUPS - 1):
                    @pl.when(j < meta_ref[2, e])
                    def _():
                        fn(pad_copy(e, j))
                return carry
            lax.fori_loop(0, N_EXPERTS, body, 0)

        def tiles(fn):
            def body(t, carry):
                fn(tile_copy(t))
                return carry
            lax.fori_loop(meta_ref[0, N_EXPERTS], n_tiles_all, body, 0)

        pads(lambda c: c.start())
        tiles(lambda c: c.start())
        pads(lambda c: c.wait())
        tiles(lambda c: c.wait())


def _dispatch(gdst, meta, lr, h_a, h_b, n_tiles_all):
    tm = TOK_TILE
    na, nb2 = h_a.shape[0] // tm, h_b.shape[0] // tm
    return pl.pallas_call(
        functools.partial(_dispatch_kernel, na, n_tiles_all),
        out_shape=jax.ShapeDtypeStruct((n_tiles_all * ROW_TILE, ROW_W), BF16),
        grid=(na + nb2,),
        in_specs=[pl.BlockSpec((1, 1, GDST_LANES), lambda i: (i, 0, 0), memory_space=pltpu.SMEM),
                  pl.BlockSpec((1, 1, GDST_LANES), lambda i: (jnp.maximum(i - 1, 0), 0, 0), memory_space=pltpu.SMEM),
                  pl.BlockSpec((META_ROWS, LANES), lambda i: (0, 0), memory_space=pltpu.SMEM),
                  pl.BlockSpec((TOP_K, tm), lambda i: (0, i)),
                  pl.BlockSpec((tm, ROW_W), lambda i: (jnp.minimum(i, na - 1), 0)),
                  pl.BlockSpec((tm, ROW_W), lambda i: (jnp.maximum(i - na, 0), 0))],
        out_specs=pl.BlockSpec(memory_space=pl.ANY),
        scratch_shapes=[pltpu.VMEM((tm, ROW_W), BF16), pltpu.VMEM((2, LOCAL_ROWS, ROW_W), BF16),
                        pltpu.VMEM((ROW_TILE, ROW_W), BF16),
                        pltpu.SemaphoreType.DMA((2,)), pltpu.SemaphoreType.DMA],
        compiler_params=pltpu.CompilerParams(dimension_semantics=("arbitrary",), vmem_limit_bytes=VMEM_LIMIT),
        name="dispatch",
    )(gdst, gdst, meta, lr, h_a, h_b)


def _expert_kernel(n_tiles_all, ts_ref, xs_ref, wg_ref, wu_ref, wd_ref, bg_ref, bu_ref, bd_ref, y_ref,
                   wstage, wgb, wub, wdb, xbuf, ybuf, sem_w, sem_in, sem_out):
    e = pl.program_id(0)
    t0 = ts_ref[e]
    nt = ts_ref[e + 1] - t0
    wslot = e % 2

    def w_copies(ex, slot):
        return [pltpu.make_async_copy(w_ref.at[ex], wstage.at[slot, j], sem_w.at[slot])
                for j, w_ref in enumerate((wg_ref, wu_ref, wd_ref))]

    def in_copy(t, slot):
        rows = pl.ds(pl.multiple_of((t0 + t) * ROW_TILE, ROW_TILE), ROW_TILE)
        return pltpu.make_async_copy(xs_ref.at[rows], xbuf.at[slot], sem_in.at[slot])

    def out_copy(tile, slot):
        rows = pl.ds(pl.multiple_of(tile * ROW_TILE, ROW_TILE), ROW_TILE)
        return pltpu.make_async_copy(ybuf.at[slot], y_ref.at[rows], sem_out.at[slot])

    @pl.when((e == 0) & (nt > 0))
    def _():
        for c in w_copies(0, 0):
            c.start()

    for j in range(TILE_SLOTS):
        @pl.when(nt > j)
        def _():
            in_copy(j, j).start()

    @pl.when(e + 1 < N_EXPERTS)
    def _():
        @pl.when(ts_ref[e + 2] > ts_ref[e + 1])
        def _():
            for c in w_copies(e + 1, 1 - wslot):
                c.start(priority=1)

    @pl.when(nt > 0)
    def _():
        for c in w_copies(e, wslot):
            c.wait()
        wgb[...] = wstage[wslot, 0].astype(BF16)
        wub[...] = wstage[wslot, 1].astype(BF16)
        wdb[...] = wstage[wslot, 2].astype(BF16)
        e_f = e.astype(F32)

        def acquire(t):
            slot = t % TILE_SLOTS
            in_copy(t, slot).wait()

            @pl.when(t >= TILE_SLOTS)
            def _():
                out_copy(t0 + t - TILE_SLOTS, slot).wait()

        def compute(t):
            slot = t % TILE_SLOTS
            xw = xbuf[slot]
            x = xw[:, :D_MODEL]
            info = xw[:, D_MODEL:].astype(F32)
            w_row = jnp.zeros((ROW_TILE, 1), F32)
            for k in range(TOP_K):
                wk = info[:, TOP_K + k:TOP_K + k + 1] + info[:, 2 * TOP_K + k:2 * TOP_K + k + 1]
                w_row = w_row + jnp.where(info[:, k:k + 1] == e_f, wk, 0.0)
            g = jnp.minimum(_dot(x, wgb[...]) + bg_ref[0], SWIGLU_LIMIT)
            u = jnp.clip(_dot(x, wub[...]) + bu_ref[0], -SWIGLU_LIMIT, SWIGLU_LIMIT)
            act = g * jax.nn.sigmoid(SWIGLU_ALPHA * g) * (u + 1.0)
            ybuf[slot] = ((_dot(act.astype(BF16), wdb[...]) + bd_ref[0]) * w_row).astype(BF16)

        def release(t):
            slot = t % TILE_SLOTS
            out_copy(t0 + t, slot).start()

            @pl.when(t + TILE_SLOTS < nt)
            def _():
                in_copy(t + TILE_SLOTS, slot).start()

        def pair(p, carry):
            ta, tb = 2 * p, 2 * p + 1
            acquire(ta)
            acquire(tb)
            compute(ta)
            compute(tb)
            release(ta)
            release(tb)
            return carry

        lax.fori_loop(0, nt // 2, pair, 0)

        @pl.when(nt % 2 == 1)
        def _():
            acquire(nt - 1)
            compute(nt - 1)
            release(nt - 1)

        for j in range(1, TILE_SLOTS + 1):
            @pl.when(nt >= j)
            def _():
                out_copy(t0 + nt - j, (nt - j) % TILE_SLOTS).wait()

    @pl.when(e == pl.num_programs(0) - 1)
    def _():
        n_used = ts_ref[N_EXPERTS]
        ybuf[0] = jnp.zeros((ROW_TILE, D_MODEL), BF16)

        def zstart(tile, carry):
            out_copy(tile, 0).start()
            return carry

        def zwait(tile, carry):
            out_copy(tile, 0).wait()
            return carry

        lax.fori_loop(n_used, n_tiles_all, zstart, 0)
        lax.fori_loop(n_used, n_tiles_all, zwait, 0)


def _experts(tile_start, xs, w_gate, b_gate, w_up, b_up, w_down, b_down):
    n_rows = xs.shape[0]
    b_spec = pl.BlockSpec((1, 1, D_MODEL), lambda e, ts: (e, 0, 0))
    any_spec = pl.BlockSpec(memory_space=pl.ANY)
    return pl.pallas_call(
        functools.partial(_expert_kernel, n_rows // ROW_TILE),
        out_shape=jax.ShapeDtypeStruct((n_rows, D_MODEL), BF16),
        grid_spec=pltpu.PrefetchScalarGridSpec(
            num_scalar_prefetch=1,
            grid=(N_EXPERTS,),
            in_specs=[any_spec, any_spec, any_spec, any_spec, b_spec, b_spec, b_spec],
            out_specs=any_spec,
            scratch_shapes=[pltpu.VMEM((2, 3, D_MODEL, D_MODEL), F32)]
                           + [pltpu.VMEM((D_MODEL, D_MODEL), BF16)] * 3
                           + [pltpu.VMEM((TILE_SLOTS, ROW_TILE, ROW_W), BF16),
                              pltpu.VMEM((TILE_SLOTS, ROW_TILE, D_MODEL), BF16),
                              pltpu.SemaphoreType.DMA((2,)), pltpu.SemaphoreType.DMA((TILE_SLOTS,)),
                              pltpu.SemaphoreType.DMA((TILE_SLOTS,))],
        ),
        compiler_params=pltpu.CompilerParams(dimension_semantics=("arbitrary",), vmem_limit_bytes=VMEM_LIMIT),
        name="experts",
    )(tile_start, xs, w_gate, w_up, w_down, b_gate[:, None, :], b_up[:, None, :], b_down[:, None, :])


def _combine_kernel(gdst_ref, gdst_next_ref, lr_ref, y_ref, x1_ref, gt_ref, gf_ref, out_ref, loc_ref, sem):
    i = pl.program_id(0)
    slot = i % 2
    x1 = x1_ref[...]
    sb, l, _ = x1.shape

    @pl.when(i == 0)
    def _():
        _start_all(_group_copies(loc_ref, slot, y_ref, gdst_ref, sem, False))

    _wait_groups(loc_ref, slot, sem)

    prefetch = _group_copies(loc_ref, 1 - slot, y_ref, gdst_next_ref, sem, False)
    lr = lr_ref[...]
    moe = jnp.zeros((sb * l, D_MODEL), F32)
    for c in range(N_CHUNKS):
        rows = loc_ref[slot, c * MASK_ROWS:(c + 1) * MASK_ROWS, :]
        moe = moe + lax.dot_general(_sort_matrix(lr, c), rows, (((0,), (0,)), ((), ())),
                                    preferred_element_type=F32)
        _start_all(prefetch[c * CHUNK_GROUPS:(c + 1) * CHUNK_GROUPS])
    x2 = x1 + gt_ref[...] * moe.reshape(sb, l, D_MODEL)
    out_ref[...] = x2 * lax.rsqrt(jnp.mean(x2 * x2, axis=-1, keepdims=True) + EPS) * gf_ref[...]

    @pl.when(i == pl.num_programs(0) - 1)
    def _():
        _wait_groups(loc_ref, 1 - slot, sem)


def _combine(gdst, lr, y, x1, gt, g_final, sb, l, seq_div, blk_off):
    n3 = x1.shape[0]
    nblk = n3 // sb
    tm = sb * l
    assert tm == TOK_TILE
    return pl.pallas_call(
        _combine_kernel,
        out_shape=jax.ShapeDtypeStruct(x1.shape, F32),
        grid=(nblk,),
        in_specs=[pl.BlockSpec((1, 1, GDST_LANES), lambda i: (i + blk_off, 0, 0), memory_space=pltpu.SMEM),
                  pl.BlockSpec((1, 1, GDST_LANES), lambda i: (jnp.minimum(i + 1, nblk - 1) + blk_off, 0, 0),
                               memory_space=pltpu.SMEM),
                  pl.BlockSpec((TOP_K, tm), lambda i: (0, i + blk_off)),
                  pl.BlockSpec(memory_space=pl.ANY),
                  pl.BlockSpec((sb, l, D_MODEL), lambda i: (i, 0, 0)),
                  _ada_spec(gt, sb, seq_div),
                  pl.BlockSpec((1, D_MODEL), lambda i: (0, 0))],
        out_specs=pl.BlockSpec((sb, l, D_MODEL), lambda i: (i, 0, 0)),
        scratch_shapes=[pltpu.VMEM((2, LOCAL_ROWS, D_MODEL), BF16), pltpu.SemaphoreType.DMA((2,))],
        compiler_params=pltpu.CompilerParams(dimension_semantics=("arbitrary",), vmem_limit_bytes=VMEM_LIMIT),
        name="combine",
    )(gdst, gdst, lr, y, x1, gt[0], g_final)


def kernel(x_prompt, x_sample, c_prompt, c_sample, state_ssm, state_conv, w_ada, b_ada, g_mix, w_in, g_v_a, w_spatial, b_spatial, g_out_a, conv_w, conv_b, dt_bias, a_log, d_skip, g_out_b, w_out, g_ffn, w_router, b_router, w_gate, b_gate, w_up, b_up, w_down, b_down, g_final):
    assert w_ada.shape[0] == 1, "single-layer step"
    p = dict(w_in=w_in[0], g_mix=g_mix[0], g_v_a=g_v_a[0], w_spatial=w_spatial[0], b_spatial=b_spatial[0],
             g_out_a=g_out_a[0], conv_w=conv_w[0], conv_b=conv_b[0], dt_bias=dt_bias[0], a_log=a_log[0],
             d_skip=d_skip[0], g_out_b=g_out_b[0])
    bp, lp, _ = x_prompt.shape
    bs, ls, _ = x_sample.shape
    tp, ts = bp * lp, bs * ls

    ada = _ada(jnp.concatenate([c_sample, c_prompt], axis=0), w_ada[0], b_ada[0][None, :])
    ada = ada.reshape(bs + bp, 1, 6 * D_MODEL)
    ada_s = [(ada, j, 0) for j in range(6)]
    ada_p = [(ada, j, bs) for j in range(6)]

    fw = _front_weights(p)
    mixed_p, conv_p, ssm_p = _prompt_mixer(x_prompt, ada_p[0], ada_p[1], fw, p)
    mixed_s, v_s, conv_s, ssm_s = _sample_mixer(x_sample, ada_s[0], ada_s[1], state_ssm[0], state_conv[0], fw, p)

    w_out_b = w_out[0].astype(BF16)
    g_ffn2 = g_ffn[0][None, :]
    wr_t = w_router[0].T
    wr_hi = wr_t.astype(BF16)
    wr_both = jnp.concatenate([wr_hi, (wr_t - wr_hi.astype(F32)).astype(BF16)], axis=0)
    br = b_router[0][:, None]
    tps = lp // TOK_TILE
    sbs = TOK_TILE // ls
    xp3 = x_prompt.reshape(tp // POST_TILE, POST_TILE, D_MODEL)
    x1_p, h2p_p, ids_p = _post(mixed_p, xp3, ada_p[2], ada_p[4], ada_p[3], w_out_b, g_ffn2, wr_both, br,
                               1, POST_TILE, lp // POST_TILE)
    x1_p = x1_p.reshape(bp * tps, TOK_TILE, D_MODEL)
    x1_s, h2p_s, ids_s = _post(mixed_s, x_sample, ada_s[2], ada_s[4], ada_s[3], w_out_b, g_ffn2, wr_both, br,
                               POST_TILE // ls, ls, 1)

    n_blocks = (tp + ts) // TOK_TILE
    max_groups = (tp + ts) * TOP_K // ROW_GROUP + n_blocks * N_EXPERTS + N_EXPERTS * (TILE_GROUPS - 1)
    n_tiles = -(-max_groups // TILE_GROUPS)
    n_tiles_all = n_tiles + 2 * LOCAL_GROUPS // TILE_GROUPS
    lr, gdst, meta = _plan(jnp.concatenate([ids_p, ids_s], axis=1), n_tiles * TILE_GROUPS)

    xs = _dispatch(gdst, meta, lr, h2p_p, h2p_s, n_tiles_all)
    y = _experts(meta[0], xs, w_gate[0], b_gate[0], w_up[0], b_up[0], w_down[0], b_down[0])

    gf = g_final[None, :]
    y_p = _combine(gdst, lr, y, x1_p, ada_p[5], gf, 1, TOK_TILE, tps, 0).reshape(bp, lp, D_MODEL)
    y_s = _combine(gdst, lr, y, x1_s, ada_s[5], gf, sbs, ls, 1, tp // TOK_TILE)

    return (y_p, y_s, ssm_p[None], conv_p[None], ssm_s[None], conv_s[None], v_s.reshape(1, bs, ls, A_WIDTH))
```

```python
import functools
import math

import numpy as np
import jax
import jax.numpy as jnp
from jax import lax
from jax.experimental import pallas as pl
from jax.experimental.pallas import tpu as pltpu

F32 = jnp.float32
BF16 = jnp.bfloat16
I32 = jnp.int32

D_MODEL = 1024
A_WIDTH = 512
A_HEADS = 4
A_HEAD_DIM = 128
CHUNK = 128
B_WIDTH = 512
SSD_HEAD_DIM = 64
SSD_HEADS = 8
SSD_GROUPS = 2
SSD_STATE = 128
GROUP_W = B_WIDTH // SSD_GROUPS
CONV_K = 4
CONV_DIM = 1024
CONV_PAD = 8
N_EXPERTS = 32
TOP_K = 4
SWIGLU_LIMIT = 7.0
SWIGLU_ALPHA = 1.702
EPS = 1e-6
DECAY_MASKED = -1e30
LANES = 128

TOK_TILE = 512
SAMPLE_SEQ_TILE = 16
SEQ_UNROLL = 8
ROW_TILE = 256
ROW_GROUP = 16
TILE_GROUPS = ROW_TILE // ROW_GROUP
LOCAL_GROUPS = TOK_TILE * TOP_K // ROW_GROUP + N_EXPERTS
LOCAL_ROWS = LOCAL_GROUPS * ROW_GROUP
GDST_LANES = -(-LOCAL_GROUPS // LANES) * LANES
ROW_W = D_MODEL + LANES
INFO_ROWS = 16
MASK_ROWS = 256
META_ROWS = 8
POST_TILE = 1024
PROMPT_TILE = 512
PROMPT_SPLITS = 2
TILE_SLOTS = 4
VMEM_LIMIT = 56 * 1024 * 1024


def _dot(a, b):
    return jnp.dot(a, b, preferred_element_type=F32)


def _dot_nt(a, b):
    return lax.dot_general(a, b, (((1,), (1,)), ((), ())), preferred_element_type=F32)


def _split(x):
    hi = x.astype(BF16)
    lo = (x - hi.astype(F32)).astype(BF16)
    return hi, lo


def _dot_exact_l(t, x):
    hi, lo = _split(x)
    return _dot(t, hi) + _dot(t, lo)


def _dot_exact_r(x, t):
    hi, lo = _split(x)
    return _dot(hi, t) + _dot(lo, t)


def _silu(x):
    return x * jax.nn.sigmoid(x)


def _gelu(x):
    return 0.5 * x * (1.0 + lax.erf(x * (1.0 / math.sqrt(2.0))))


def _softplus(x):
    return jnp.maximum(x, 0.0) + jnp.log1p(jnp.exp(-jnp.abs(x)))


def _rms(x, g):
    return x * lax.rsqrt(jnp.mean(x * x, axis=-1, keepdims=True) + EPS) * g


def _ada_kernel(c_ref, w_ref, b_ref, o_ref):
    s_hi, s_lo = _split(_silu(c_ref[...]))
    w_hi, w_lo = _split(w_ref[...])
    o_ref[...] = _dot(s_hi, w_hi) + _dot(s_lo, w_hi) + _dot(s_hi, w_lo) + b_ref[...]


def _ada(c_all, w_ada, b_ada):
    m = c_all.shape[0]
    n = w_ada.shape[1]
    bn = 1024
    return pl.pallas_call(
        _ada_kernel,
        out_shape=jax.ShapeDtypeStruct((m, n), F32),
        grid=(n // bn,),
        in_specs=[pl.BlockSpec((m, D_MODEL), lambda j: (0, 0)),
                  pl.BlockSpec((D_MODEL, bn), lambda j: (0, j)),
                  pl.BlockSpec((1, bn), lambda j: (0, j))],
        out_specs=pl.BlockSpec((m, bn), lambda j: (0, j)),
        compiler_params=pltpu.CompilerParams(dimension_semantics=("arbitrary",), vmem_limit_bytes=VMEM_LIMIT),
        name="ada",
    )(c_all, w_ada, b_ada)


def _mixer_front(x3, sh, sc, prev, refs, xp_ref):
    (g_mix, w_uvz, w_xbc, w_dt, wbd, bias_sp, g_v, g_oa, conv_w, conv_b, dt_bias, a_row) = refs
    sb, l, _ = x3.shape
    tm = sb * l
    xn = x3 * lax.rsqrt(jnp.mean(x3 * x3, axis=-1, keepdims=True) + EPS) * g_mix[...]
    h = (xn * (1.0 + sc) + sh).reshape(tm, D_MODEL)
    hb = h.astype(BF16)
    uvz = _dot(hb, w_uvz[...])
    xbc = _dot(hb, w_xbc[...])
    dt_raw = _dot(hb, w_dt[...])

    u = _gelu(uvz[:, :A_WIDTH])
    vg = _gelu(uvz[:, A_WIDTH:2 * A_WIDTH])
    z = uvz[:, 2 * A_WIDTH:]
    v_parts, s_parts = [], []
    for hd in range(A_HEADS):
        sl = slice(hd * A_HEAD_DIM, (hd + 1) * A_HEAD_DIM)
        vh = _rms(vg[:, sl], g_v[:, sl])
        v_parts.append(vh)
        vb = vh.astype(BF16)
        s_parts.append(jnp.concatenate([_dot(wbd[hd], vb[r0:r0 + CHUNK]) + bias_sp[:, sl]
                                        for r0 in range(0, tm, CHUNK)], axis=0))
    v = jnp.concatenate(v_parts, axis=1)
    s_a = jnp.concatenate(s_parts, axis=1)
    out_a = _rms(u * s_a, g_oa[...])

    xp_ref[:, 0:CONV_PAD, :] = prev
    xp_ref[:, CONV_PAD:, :] = xbc.reshape(sb, l, CONV_DIM)
    xp = xp_ref[...]
    acc = conv_b[...] + xp[:, CONV_PAD:, :] * conv_w[CONV_K - 1:CONV_K, :]
    for s in range(1, CONV_K):
        back = pltpu.roll(xp, s, axis=1)[:, CONV_PAD:, :]
        acc = acc + back * conv_w[CONV_K - 1 - s:CONV_K - s, :]
    xc = _silu(acc).reshape(tm, CONV_DIM)
    dt = _softplus(dt_raw + dt_bias[...])
    d_a = dt * a_row[...]
    return out_a, v, z, xc, dt, d_a


def _ssd_chunk(xs, bm, cm, dt, d_a, cref):
    tril, ones, expand, neg_mask = cref
    cs = _dot_exact_l(tril[...], d_a)
    cs_t = cs.T
    dt_t = dt.T
    cs_tot = _dot_exact_l(ones[...], d_a)
    vals = jnp.concatenate([dt * jnp.exp(cs_tot - cs), jnp.exp(cs)], axis=0)
    vals_e = _dot_exact_r(vals, expand[...])
    n = xs.shape[0]
    w_e, e_e = vals_e[:n], vals_e[n:]
    xdtd = xs * w_e
    neg = neg_mask[...]
    row_lt_half = lax.broadcasted_iota(I32, (2 * n, LANES), 0) < n
    lane_lt_half = lax.broadcasted_iota(I32, (2 * n, LANES), 1) < SSD_HEAD_DIM
    y_parts = []
    for g in range(SSD_GROUPS):
        cb = _dot_nt(cm[:, g * SSD_STATE:(g + 1) * SSD_STATE].astype(BF16),
                     bm[:, g * SSD_STATE:(g + 1) * SSD_STATE].astype(BF16))
        for hp in range(SSD_HEADS // SSD_GROUPS // 2):
            h0 = g * (SSD_HEADS // SSD_GROUPS) + 2 * hp
            ms = []
            for hh in (h0, h0 + 1):
                diff = cs[:, hh:hh + 1] - cs_t[hh:hh + 1, :]
                ms.append((cb * jnp.exp(diff + neg) * dt_t[hh:hh + 1, :]).astype(BF16))
            pair = xs[:, h0 * SSD_HEAD_DIM:(h0 + 2) * SSD_HEAD_DIM]
            rhs = jnp.where(row_lt_half == lane_lt_half, jnp.concatenate([pair, pair], axis=0), 0.0).astype(BF16)
            y_parts.append(_dot(jnp.concatenate(ms, axis=1), rhs))
    y_diag = jnp.concatenate(y_parts, axis=1)
    return y_diag, e_e, xdtd, cs_tot


def _mixer_back(y, xs, z, out_a, dskip_e, g_ob):
    y = y + xs * dskip_e
    gated = y * _silu(z)
    parts = [_rms(gated[:, g * GROUP_W:(g + 1) * GROUP_W], g_ob[:, g * GROUP_W:(g + 1) * GROUP_W])
             for g in range(SSD_GROUPS)]
    return jnp.concatenate([out_a] + parts, axis=1).astype(BF16)


N_FRONT = 12
N_SSD = 4


def _prompt_mixer_kernel(tiles_per_seq, x_ref, sh_ref, sc_ref, *rest):
    front = rest[:N_FRONT]
    cref = rest[N_FRONT:N_FRONT + N_SSD]
    dskip_e, g_ob = rest[N_FRONT + N_SSD:N_FRONT + N_SSD + 2]
    mixed_ref, conv_out_ref, ssm_out_ref = rest[N_FRONT + N_SSD + 2:N_FRONT + N_SSD + 5]
    xp_ref, carry_ref, st_ref = rest[N_FRONT + N_SSD + 5:]
    i = pl.program_id(0)
    first = (i % tiles_per_seq) == 0

    @pl.when(first)
    def _():
        carry_ref[...] = jnp.zeros_like(carry_ref)
        st_ref[...] = jnp.zeros_like(st_ref)

    l = x_ref.shape[1]
    hl = l // PROMPT_SPLITS
    parts = []
    prev = carry_ref[...]
    for hh in range(PROMPT_SPLITS):
        xp_h = xp_ref.at[hh]
        parts.append(_mixer_front(x_ref[:, hh * hl:(hh + 1) * hl, :], sh_ref[...], sc_ref[...], prev, front, xp_h))
        prev = xp_h[:, hl:hl + CONV_PAD, :]
    carry_ref[...] = prev
    out_a, _, z, xc, dt, d_a = [jnp.concatenate([p[j] for p in parts], axis=0) for j in range(6)]
    xs = xc[:, :B_WIDTH]
    y_rows = []
    for c in range(l // CHUNK):
        r = slice(c * CHUNK, (c + 1) * CHUNK)
        bm = xc[r, B_WIDTH:B_WIDTH + SSD_GROUPS * SSD_STATE]
        cm = xc[r, B_WIDTH + SSD_GROUPS * SSD_STATE:]
        y_diag, e_e, xdtd, _ = _ssd_chunk(xs[r], bm, cm, dt[r], d_a[r], cref)
        st = st_ref[...]
        y_off, upd = [], []
        for g in range(SSD_GROUPS):
            gs = slice(g * GROUP_W, (g + 1) * GROUP_W)
            ns = slice(g * SSD_STATE, (g + 1) * SSD_STATE)
            y_off.append(_dot(cm[:, ns].astype(BF16), st[:, gs].astype(BF16)))
            upd.append(_dot(bm[:, ns].T.astype(BF16), xdtd[:, gs].astype(BF16)))
        y_rows.append(y_diag + jnp.concatenate(y_off, axis=1) * e_e)
        st_ref[...] = st * e_e[CHUNK - 1:CHUNK, :] + jnp.concatenate(upd, axis=1)
    y = jnp.concatenate(y_rows, axis=0)
    mixed_ref[...] = _mixer_back(y, xs, z, out_a, dskip_e[...], g_ob[...])

    @pl.when((i % tiles_per_seq) == tiles_per_seq - 1)
    def _():
        conv_out_ref[...] = xp_ref[PROMPT_SPLITS - 1, :, hl + CONV_PAD - (CONV_K - 1):hl + CONV_PAD, :]
        ssm_out_ref[0] = st_ref[...].T


def _sample_mixer_kernel(x_ref, sh_ref, sc_ref, prev_ref, ssm0_ref, *rest):
    front = rest[:N_FRONT]
    cref = rest[N_FRONT:N_FRONT + N_SSD]
    dskip_e, g_ob, selseq = rest[N_FRONT + N_SSD:N_FRONT + N_SSD + 3]
    mixed_ref, v_ref, conv_out_ref, ssm_out_ref = rest[N_FRONT + N_SSD + 3:N_FRONT + N_SSD + 7]
    xp_ref, yoff_ref, cbf_ref, bbf_ref, t1_ref, dtab_ref = rest[N_FRONT + N_SSD + 7:]
    x3 = x_ref[...]
    sb, l, _ = x3.shape
    tm = sb * l
    out_a, v, z, xc, dt, d_a = _mixer_front(x3, sh_ref[...], sc_ref[...], prev_ref[...], front, xp_ref)
    v_ref[...] = v
    conv_out_ref[...] = xp_ref[:, l + CONV_PAD - (CONV_K - 1):l + CONV_PAD, :]
    xs = xc[:, :B_WIDTH]
    bm = xc[:, B_WIDTH:B_WIDTH + SSD_GROUPS * SSD_STATE]
    cm = xc[:, B_WIDTH + SSD_GROUPS * SSD_STATE:]
    y_diag, e_e, xdtd, _ = _ssd_chunk(xs, bm, cm, dt, d_a, cref)

    e_tot = jnp.exp(_dot_exact_l(selseq[...], d_a))
    for hh in range(SSD_HEADS):
        dtab_ref[hh] = jnp.broadcast_to(e_tot[:, hh:hh + 1], (sb, LANES))
    cbf_ref[...] = cm
    bbf_ref[...] = bm
    for g in range(SSD_GROUPS):
        t1_ref[g] = xdtd[:, g * GROUP_W:(g + 1) * GROUP_W].T.astype(BF16)
    seq_of_row = lax.broadcasted_iota(I32, (tm, SSD_STATE), 0) // l
    heads_per_group = SSD_HEADS // SSD_GROUPS

    def one_seq(j):
        r0 = pl.multiple_of(j * l, l)
        s0 = ssm0_ref[j]
        for g in range(SSD_GROUPS):
            ns = slice(g * SSD_STATE, (g + 1) * SSD_STATE)
            s0g = s0[g * heads_per_group:(g + 1) * heads_per_group].reshape(GROUP_W, SSD_STATE)
            cj = cbf_ref[pl.ds(r0, l), ns].astype(BF16)
            yoff_ref[pl.ds(r0, l), g * GROUP_W:(g + 1) * GROUP_W] = _dot_nt(cj, s0g.astype(BF16))
            bmask = jnp.where(seq_of_row == j, bbf_ref[:, ns], 0.0).astype(BF16)
            upd = _dot(t1_ref[g], bmask)
            for hq in range(heads_per_group):
                hh = g * heads_per_group + hq
                dec = dtab_ref[hh, pl.ds(j, 1), :]
                ssm_out_ref[j, hh] = s0[hh] * dec + upd[hq * SSD_HEAD_DIM:(hq + 1) * SSD_HEAD_DIM]

    def body(jj, carry):
        for u in range(SEQ_UNROLL):
            one_seq(jj * SEQ_UNROLL + u)
        return carry

    lax.fori_loop(0, sb // SEQ_UNROLL, body, 0)
    y = y_diag + yoff_ref[...] * e_e
    mixed_ref[...] = _mixer_back(y, xs, z, out_a, dskip_e[...], g_ob[...])


def _ada_spec(ref, sb, div):
    _, term, row0 = ref
    assert row0 % sb == 0
    return pl.BlockSpec((sb, 1, D_MODEL), lambda i: (row0 // sb + i // div, 0, term))


def _const_spec(a):
    nd = a.ndim
    return pl.BlockSpec(a.shape, lambda i, _nd=nd: (0,) * _nd)


def _spatial_consts(w_spatial, b_spatial, cl, tm):
    w = jnp.where(jnp.tril(jnp.ones((cl, cl), bool)), w_spatial[:, :cl, :cl], 0.0)
    eye = jnp.eye(tm // cl, dtype=F32)
    wbd = jnp.einsum("ab,hts->hatbs", eye, w).reshape(A_HEADS, tm, tm).astype(BF16)
    bias = jnp.tile(jnp.repeat(b_spatial[:, :cl].T, A_HEAD_DIM, axis=1), (tm // cl, 1))
    return wbd, bias


def _ssd_consts(cl):
    r = np.arange(CHUNK)
    same = (r[:, None] // cl) == (r[None, :] // cl)
    tril = same & (r[:, None] >= r[None, :])
    expand = np.zeros((LANES, B_WIDTH), np.float32)
    for hh in range(SSD_HEADS):
        expand[hh, hh * SSD_HEAD_DIM:(hh + 1) * SSD_HEAD_DIM] = 1.0
    return (jnp.asarray(tril, BF16), jnp.asarray(same, BF16), jnp.asarray(expand, BF16),
            jnp.asarray(np.where(tril, 0.0, DECAY_MASKED), F32))


def _front_weights(p):
    w_in = p["w_in"]
    c0, c1 = 3 * A_WIDTH, 3 * A_WIDTH + CONV_DIM
    w_dt = w_in[:, c1:]
    pad8 = lambda v: jnp.pad(v, (0, LANES - SSD_HEADS))
    a = -jnp.exp(p["a_log"])
    return dict(
        g_mix=p["g_mix"][None, :],
        w_uvz=w_in[:, :c0].astype(BF16),
        w_xbc=w_in[:, c0:c1].astype(BF16),
        w_dt=jnp.pad(w_dt, ((0, 0), (0, LANES - SSD_HEADS))).astype(BF16),
        g_v=p["g_v_a"][None, :], g_oa=p["g_out_a"][None, :],
        conv_w=p["conv_w"], conv_b=p["conv_b"][None, :],
        dt_bias=pad8(p["dt_bias"])[None, :],
        a_row=pad8(a)[None, :],
        dskip_e=jnp.repeat(p["d_skip"], SSD_HEAD_DIM)[None, :],
        g_ob=p["g_out_b"][None, :],
    )


def _front_list(fw, wbd, bias_sp):
    return [fw["g_mix"], fw["w_uvz"], fw["w_xbc"], fw["w_dt"], wbd, bias_sp, fw["g_v"], fw["g_oa"],
            fw["conv_w"], fw["conv_b"], fw["dt_bias"], fw["a_row"]]


def _prompt_mixer(x, sh, sc, fw, p):
    nseq, lseq, _ = x.shape
    tile = PROMPT_TILE
    tps = lseq // tile
    nt = nseq * tps
    x4 = x.reshape(nt, tile, D_MODEL)
    wbd, bias_sp = _spatial_consts(p["w_spatial"], p["b_spatial"], CHUNK, CHUNK)
    consts = _front_list(fw, wbd, bias_sp) + list(_ssd_consts(CHUNK)) + [fw["dskip_e"], fw["g_ob"]]
    mixed, conv_new, ssm_new = pl.pallas_call(
        functools.partial(_prompt_mixer_kernel, tps),
        out_shape=(jax.ShapeDtypeStruct((nt * tile, D_MODEL), BF16),
                   jax.ShapeDtypeStruct((nseq, CONV_K - 1, CONV_DIM), F32),
                   jax.ShapeDtypeStruct((nseq, B_WIDTH, SSD_STATE), F32)),
        grid=(nt,),
        in_specs=[pl.BlockSpec((1, tile, D_MODEL), lambda i: (i, 0, 0)), _ada_spec(sh, 1, tps), _ada_spec(sc, 1, tps)]
                 + [_const_spec(a) for a in consts],
        out_specs=(pl.BlockSpec((tile, D_MODEL), lambda i: (i, 0)),
                   pl.BlockSpec((1, CONV_K - 1, CONV_DIM), lambda i: (i // tps, 0, 0)),
                   pl.BlockSpec((1, B_WIDTH, SSD_STATE), lambda i: (i // tps, 0, 0))),
        scratch_shapes=[pltpu.VMEM((PROMPT_SPLITS, 1, tile // PROMPT_SPLITS + CONV_PAD, CONV_DIM), F32),
                        pltpu.VMEM((1, CONV_PAD, CONV_DIM), F32),
                        pltpu.VMEM((SSD_STATE, B_WIDTH), F32)],
        compiler_params=pltpu.CompilerParams(dimension_semantics=("arbitrary",), vmem_limit_bytes=VMEM_LIMIT),
        name="prompt_mixer",
    )(x4, sh[0], sc[0], *consts)
    return mixed, conv_new, ssm_new.reshape(nseq, SSD_HEADS, SSD_HEAD_DIM, SSD_STATE)


def _sample_mixer(x, sh, sc, state_ssm, state_conv, fw, p):
    nseq, l, _ = x.shape
    sb = SAMPLE_SEQ_TILE
    tm = sb * l
    assert tm == CHUNK
    wbd, bias_sp = _spatial_consts(p["w_spatial"], p["b_spatial"], l, tm)
    selseq = jnp.asarray((np.arange(tm)[None, :] // l) == np.arange(sb)[:, None], BF16)
    consts = _front_list(fw, wbd, bias_sp) + list(_ssd_consts(l)) + [fw["dskip_e"], fw["g_ob"], selseq]
    prev = jnp.pad(state_conv, ((0, 0), (CONV_PAD - (CONV_K - 1), 0), (0, 0)))
    ssm_spec = pl.BlockSpec((sb, SSD_HEADS, SSD_HEAD_DIM, SSD_STATE), lambda i: (i, 0, 0, 0))
    return pl.pallas_call(
        _sample_mixer_kernel,
        out_shape=(jax.ShapeDtypeStruct((nseq * l, D_MODEL), BF16),
                   jax.ShapeDtypeStruct((nseq * l, A_WIDTH), F32),
                   jax.ShapeDtypeStruct((nseq, CONV_K - 1, CONV_DIM), F32),
                   jax.ShapeDtypeStruct(state_ssm.shape, F32)),
        grid=(nseq // sb,),
        in_specs=[pl.BlockSpec((sb, l, D_MODEL), lambda i: (i, 0, 0)), _ada_spec(sh, sb, 1), _ada_spec(sc, sb, 1),
                  pl.BlockSpec((sb, CONV_PAD, CONV_DIM), lambda i: (i, 0, 0)), ssm_spec]
                 + [_const_spec(a) for a in consts],
        out_specs=(pl.BlockSpec((tm, D_MODEL), lambda i: (i, 0)),
                   pl.BlockSpec((tm, A_WIDTH), lambda i: (i, 0)),
                   pl.BlockSpec((sb, CONV_K - 1, CONV_DIM), lambda i: (i, 0, 0)),
                   ssm_spec),
        scratch_shapes=[pltpu.VMEM((sb, l + CONV_PAD, CONV_DIM), F32),
                        pltpu.VMEM((tm, B_WIDTH), F32),
                        pltpu.VMEM((tm, SSD_GROUPS * SSD_STATE), F32),
                        pltpu.VMEM((tm, SSD_GROUPS * SSD_STATE), F32),
                        pltpu.VMEM((SSD_GROUPS, GROUP_W, tm), BF16),
                        pltpu.VMEM((SSD_HEADS, sb, LANES), F32)],
        compiler_params=pltpu.CompilerParams(dimension_semantics=("arbitrary",), vmem_limit_bytes=VMEM_LIMIT),
        name="sample_mixer",
    )(x, sh[0], sc[0], prev, state_ssm, *consts)


def _post_kernel(mixed_ref, x_ref, gt_ref, sc_ref, sh_ref, w_out_ref, g_ffn_ref, wr_both_ref, br_ref,
                 x1_ref, h2p_ref, ids_ref):
    x3 = x_ref[...]
    sb, l, _ = x3.shape
    tm = sb * l
    mix = _dot(mixed_ref[...], w_out_ref[...]).reshape(sb, l, D_MODEL)
    x1 = x3 + gt_ref[...] * mix
    x1_ref[...] = x1
    xn = x1 * lax.rsqrt(jnp.mean(x1 * x1, axis=-1, keepdims=True) + EPS) * g_ffn_ref[...]
    h2 = (xn * (1.0 + sc_ref[...]) + sh_ref[...]).reshape(tm, D_MODEL)
    h_hi, h_lo = _split(h2)
    h2p_ref[:, :D_MODEL] = h_hi
    both = _dot_nt(wr_both_ref[...], h_hi)
    logits = (both[:N_EXPERTS] + both[N_EXPERTS:] + _dot_nt(wr_both_ref[:N_EXPERTS, :], h_lo)
              + br_ref[...])
    e_iota = lax.broadcasted_iota(I32, logits.shape, 0)
    vals, idxs = [], []
    for _ in range(TOP_K):
        m = jnp.max(logits, axis=0, keepdims=True)
        idx = jnp.min(jnp.where(logits == m, e_iota, N_EXPERTS), axis=0, keepdims=True)
        vals.append(m)
        idxs.append(idx)
        logits = jnp.where(e_iota == idx, -jnp.inf, logits)
    ex = [jnp.exp(v - vals[0]) for v in vals]
    tot = ex[0] + ex[1] + ex[2] + ex[3]
    ids = jnp.concatenate(idxs, axis=0)
    ids_ref[...] = ids
    wts = jnp.concatenate([e / tot for e in ex], axis=0)
    w_hi = wts.astype(BF16).astype(F32)
    info = jnp.concatenate([ids.astype(F32), w_hi, wts - w_hi, jnp.zeros((TOP_K, tm), F32)], axis=0).astype(BF16)
    r = lax.broadcasted_iota(I32, (INFO_ROWS, LANES), 0)
    c = lax.broadcasted_iota(I32, (INFO_ROWS, LANES), 1)
    place = jnp.where(r == c, 1.0, 0.0).astype(BF16)
    h2p_ref[:, D_MODEL:] = lax.dot_general(info, place, (((0,), (0,)), ((), ())),
                                           preferred_element_type=F32).astype(BF16)


def _post(mixed, x, gt, sc, sh, w_out_b, g_ffn, wr_both, br, sb, l, seq_div):
    n3, _, _ = x.shape
    nblk = n3 // sb
    tm = sb * l
    t = n3 * l
    consts = [w_out_b, g_ffn, wr_both, br]
    return pl.pallas_call(
        _post_kernel,
        out_shape=(jax.ShapeDtypeStruct(x.shape, F32),
                   jax.ShapeDtypeStruct((t, ROW_W), BF16),
                   jax.ShapeDtypeStruct((TOP_K, t), I32)),
        grid=(nblk,),
        in_specs=[pl.BlockSpec((tm, D_MODEL), lambda i: (i, 0)),
                  pl.BlockSpec((sb, l, D_MODEL), lambda i: (i, 0, 0)),
                  _ada_spec(gt, sb, seq_div), _ada_spec(sc, sb, seq_div), _ada_spec(sh, sb, seq_div)]
                 + [_const_spec(a) for a in consts],
        out_specs=(pl.BlockSpec((sb, l, D_MODEL), lambda i: (i, 0, 0)),
                   pl.BlockSpec((tm, ROW_W), lambda i: (i, 0)),
                   pl.BlockSpec((TOP_K, tm), lambda i: (0, i))),
        compiler_params=pltpu.CompilerParams(dimension_semantics=("arbitrary",), vmem_limit_bytes=VMEM_LIMIT),
        name="post",
    )(mixed, x, gt[0], sc[0], sh[0], *consts)


def _strict_upper(n):
    r = lax.broadcasted_iota(I32, (n, n), 0)
    c = lax.broadcasted_iota(I32, (n, n), 1)
    return jnp.where(r < c, 1.0, 0.0).astype(BF16)


def _expert_prefix(col):
    r = lax.broadcasted_iota(I32, (N_EXPERTS, N_EXPERTS), 0)
    c = lax.broadcasted_iota(I32, (N_EXPERTS, N_EXPERTS), 1)
    as_row = jnp.sum(jnp.where(r == c, col, 0.0), axis=0, keepdims=True)
    return jnp.sum(jnp.where(c < r, as_row, 0.0), axis=1, keepdims=True)


def _plan_kernel(dump_group, ids_ref, lr_ref, gdst_ref, tile_e_ref):
    tm = TOK_TILE
    nb = ids_ref.shape[1] // tm
    blk_lane = lax.broadcasted_iota(I32, (N_EXPERTS, LANES), 1)

    def block_masks(b):
        ids = ids_ref[:, pl.ds(pl.multiple_of(b * tm, tm), tm)]
        e_iota = lax.broadcasted_iota(I32, (N_EXPERTS, tm), 0)
        onehot = [ids[k:k + 1, :] == e_iota for k in range(TOP_K)]
        sel = (onehot[0] | onehot[1]) | (onehot[2] | onehot[3])
        return onehot, jnp.where(sel, 1.0, 0.0)

    def count(b, seg):
        _, m = block_masks(b)
        seg_b = jnp.ceil(jnp.sum(m, axis=1, keepdims=True) * (1.0 / ROW_GROUP))
        return jnp.where(blk_lane == b, seg_b, seg)

    seg = lax.fori_loop(0, nb, count, jnp.zeros((N_EXPERTS, LANES), F32))
    tot = jnp.sum(seg, axis=1, keepdims=True)
    padded = jnp.ceil(tot * (1.0 / TILE_GROUPS)) * TILE_GROUPS
    gstart = _expert_prefix(padded)
    gb = gstart + _dot(seg.astype(BF16), _strict_upper(LANES))
    r = lax.broadcasted_iota(I32, (N_EXPERTS, LANES), 0)
    as_row = lambda col: jnp.sum(jnp.where(r == blk_lane, col, 0.0), axis=0, keepdims=True)
    n_used = jnp.sum(padded, axis=0, keepdims=True)
    lane = lax.broadcasted_iota(I32, (1, LANES), 1)
    tiles = jnp.where(lane == N_EXPERTS, n_used, as_row(gstart)) * (1.0 / TILE_GROUPS)
    meta = jnp.concatenate([tiles, as_row(gstart + tot), as_row(padded - tot),
                            jnp.zeros((META_ROWS - 3, LANES), F32)], axis=0)
    tile_e_ref[...] = meta.astype(I32)
    upper = _strict_upper(tm)

    def place(b, carry):
        onehot, m = block_masks(b)
        seg_b = jnp.sum(jnp.where(blk_lane == b, seg, 0.0), axis=1, keepdims=True)
        gb_b = jnp.sum(jnp.where(blk_lane == b, gb, 0.0), axis=1, keepdims=True)
        loc_b = _expert_prefix(seg_b)
        before = _dot(m.astype(BF16), upper) + loc_b * ROW_GROUP
        lr_ref[:, pl.ds(pl.multiple_of(b * tm, tm), tm)] = jnp.concatenate(
            [jnp.sum(jnp.where(onehot[k], before, 0.0), axis=0, keepdims=True) for k in range(TOP_K)],
            axis=0).astype(I32)
        g = lax.broadcasted_iota(I32, (N_EXPERTS, GDST_LANES), 1).astype(F32)
        inside = (loc_b <= g) & (g < loc_b + seg_b)
        dst = jnp.sum(jnp.where(inside, gb_b + g - loc_b, 0.0), axis=0, keepdims=True)
        used = jnp.sum(jnp.where(inside, 1.0, 0.0), axis=0, keepdims=True) > 0.5
        dump = dump_group + lax.convert_element_type(b % 2, F32) * LOCAL_GROUPS + g[0:1, :]
        n_local = jnp.sum(seg_b, axis=0, keepdims=True)
        gdst_ref[b] = jnp.where(g[0:1, :] == GDST_LANES - 1, n_local, jnp.where(used, dst, dump)).astype(I32)
        return carry

    lax.fori_loop(0, nb, place, 0)


def _plan(ids, dump_group):
    t = ids.shape[1]
    nb = t // TOK_TILE
    assert nb <= LANES
    return pl.pallas_call(
        functools.partial(_plan_kernel, float(dump_group)),
        out_shape=(jax.ShapeDtypeStruct((TOP_K, t), I32),
                   jax.ShapeDtypeStruct((nb, 1, GDST_LANES), I32),
                   jax.ShapeDtypeStruct((META_ROWS, LANES), I32)),
        name="plan",
    )(ids)


def _sort_matrix(lr, c):
    r_iota = lax.broadcasted_iota(I32, (MASK_ROWS, lr.shape[1]), 0) + c * MASK_ROWS
    p = jnp.where(r_iota == lr[TOP_K - 1:TOP_K, :], 1.0, 0.0)
    for k in range(TOP_K - 1):
        p = jnp.where(r_iota == lr[k:k + 1, :], 1.0, p)
    return p.astype(BF16)


def _group_copies(loc_ref, slot, far_ref, gdst_ref, sem, to_far):
    copies = []
    for g in range(LOCAL_GROUPS):
        dst = pl.multiple_of(gdst_ref[0, 0, g] * ROW_GROUP, ROW_GROUP)
        near = loc_ref.at[slot, pl.ds(g * ROW_GROUP, ROW_GROUP)]
        far = far_ref.at[pl.ds(dst, ROW_GROUP)]
        copies.append(pltpu.make_async_copy(near, far, sem.at[slot]) if to_far
                      else pltpu.make_async_copy(far, near, sem.at[slot]))
    return copies


N_CHUNKS = LOCAL_ROWS // MASK_ROWS
CHUNK_GROUPS = LOCAL_GROUPS // N_CHUNKS
SURE_GROUPS = LOCAL_GROUPS - CHUNK_GROUPS


def _start_all(copies):
    for c in copies:
        c.start()


def _uses_last_chunk(gdst_ref):
    return gdst_ref[0, 0, GDST_LANES - 1] > SURE_GROUPS


def _wait_groups(loc_ref, slot, sem, with_last=None):
    def wait(n_groups):
        part = loc_ref.at[slot, pl.ds(0, n_groups * ROW_GROUP)]
        pltpu.make_async_copy(part, part, sem.at[slot]).wait()

    if with_last is None:
        wait(LOCAL_GROUPS)
        return
    pl.when(with_last)(functools.partial(wait, LOCAL_GROUPS))
    pl.when(jnp.logical_not(with_last))(functools.partial(wait, SURE_GROUPS))


def _dispatch_kernel(n_first, first_dump_tile, n_tiles_all, gdst_ref, gdst_prev_ref, meta_ref, lr_ref, ha_ref, hb_ref,
                     xs_ref, h_ref, loc_ref, zero_ref, sem, zsem):
    i = pl.program_id(0)
    last = pl.num_programs(0) - 1
    slot = i % 2
    with_last = _uses_last_chunk(gdst_ref)

    @pl.when(i < n_first)
    def _():
        h_ref[...] = ha_ref[...]

    @pl.when(i >= n_first)
    def _():
        h_ref[...] = hb_ref[...]

    lr = lr_ref[...]
    copies = _group_copies(loc_ref, slot, xs_ref, gdst_ref, sem, True)

    def sort_chunk(c):
        loc_ref[slot, c * MASK_ROWS:(c + 1) * MASK_ROWS, :] = _dot(_sort_matrix(lr, c), h_ref[...]).astype(BF16)
        for cp in copies[c * CHUNK_GROUPS:(c + 1) * CHUNK_GROUPS]:
            cp.start()

    for c in range(N_CHUNKS - 1):
        sort_chunk(c)
    pl.when(with_last)(functools.partial(sort_chunk, N_CHUNKS - 1))

    @pl.when(i > 0)
    def _():
        _wait_groups(loc_ref, 1 - slot, sem, _uses_last_chunk(gdst_prev_ref))

    def pad_copy(e, j):
        row = pl.multiple_of((meta_ref[1, e] + j) * ROW_GROUP, ROW_GROUP)
        return pltpu.make_async_copy(zero_ref.at[pl.ds(0, ROW_GROUP)], xs_ref.at[pl.ds(row, ROW_GROUP)], zsem)

    def tile_copy(t):
        row = pl.multiple_of(t * ROW_TILE, ROW_TILE)
        return pltpu.make_async_copy(zero_ref, xs_ref.at[pl.ds(row, ROW_TILE)], zsem)

    def pads(fn):
        def body(e, carry):
            for j in range(TILE_GROUPS - 1):
                @pl.when(j < meta_ref[2, e])
                def _():
                    fn(pad_copy(e, j))
            return carry
        lax.fori_loop(0, N_EXPERTS, body, 0)

    def tiles(fn, lo, hi):
        lax.fori_loop(lo, hi, lambda t, carry: (fn(tile_copy(t)), carry)[1], 0)

    n_used = meta_ref[0, N_EXPERTS]

    @pl.when(i == 0)
    def _():
        zero_ref[...] = jnp.zeros_like(zero_ref)
        pads(lambda c: c.start(priority=1))
        tiles(lambda c: c.start(priority=1), n_used, first_dump_tile)

    @pl.when(i == last)
    def _():
        _wait_groups(loc_ref, slot, sem, with_last)
        tiles(lambda c: c.start(), first_dump_tile, n_tiles_all)
        pads(lambda c: c.wait())
        tiles(lambda c: c.wait(), n_used, n_tiles_all)


def _dispatch(gdst, meta, lr, h_a, h_b, n_tiles_all):
    tm = TOK_TILE
    na, nb2 = h_a.shape[0] // tm, h_b.shape[0] // tm
    return pl.pallas_call(
        functools.partial(_dispatch_kernel, na, n_tiles_all - 2 * LOCAL_GROUPS // TILE_GROUPS, n_tiles_all),
        out_shape=jax.ShapeDtypeStruct((n_tiles_all * ROW_TILE, ROW_W), BF16),
        grid=(na + nb2,),
        in_specs=[pl.BlockSpec((1, 1, GDST_LANES), lambda i: (i, 0, 0), memory_space=pltpu.SMEM),
                  pl.BlockSpec((1, 1, GDST_LANES), lambda i: (jnp.maximum(i - 1, 0), 0, 0), memory_space=pltpu.SMEM),
                  pl.BlockSpec((META_ROWS, LANES), lambda i: (0, 0), memory_space=pltpu.SMEM),
                  pl.BlockSpec((TOP_K, tm), lambda i: (0, i)),
                  pl.BlockSpec((tm, ROW_W), lambda i: (jnp.minimum(i, na - 1), 0)),
                  pl.BlockSpec((tm, ROW_W), lambda i: (jnp.maximum(i - na, 0), 0))],
        out_specs=pl.BlockSpec(memory_space=pl.ANY),
        scratch_shapes=[pltpu.VMEM((tm, ROW_W), BF16), pltpu.VMEM((2, LOCAL_ROWS, ROW_W), BF16),
                        pltpu.VMEM((ROW_TILE, ROW_W), BF16),
                        pltpu.SemaphoreType.DMA((2,)), pltpu.SemaphoreType.DMA],
        compiler_params=pltpu.CompilerParams(dimension_semantics=("arbitrary",), vmem_limit_bytes=VMEM_LIMIT),
        name="dispatch",
    )(gdst, gdst, meta, lr, h_a, h_b)


def _expert_kernel(n_tiles_all, ts_ref, xs_ref, wg_ref, wu_ref, wd_ref, bg_ref, bu_ref, bd_ref, y_ref,
                   wstage, wgb, wub, wdb, xbuf, ybuf, zbuf, sem_w, sem_in, sem_out, zsem):
    e = pl.program_id(0)
    t0 = ts_ref[e]
    nt = ts_ref[e + 1] - t0
    wslot = e % 2

    def w_copies(ex, slot):
        return [pltpu.make_async_copy(w_ref.at[ex], wstage.at[slot, j], sem_w.at[slot])
                for j, w_ref in enumerate((wg_ref, wu_ref, wd_ref))]

    def in_copy(t, slot):
        rows = pl.ds(pl.multiple_of((t0 + t) * ROW_TILE, ROW_TILE), ROW_TILE)
        return pltpu.make_async_copy(xs_ref.at[rows], xbuf.at[slot], sem_in.at[slot])

    def out_copy(tile, slot):
        rows = pl.ds(pl.multiple_of(tile * ROW_TILE, ROW_TILE), ROW_TILE)
        return pltpu.make_async_copy(ybuf.at[slot], y_ref.at[rows], sem_out.at[slot])

    @pl.when((e == 0) & (nt > 0))
    def _():
        for c in w_copies(0, 0):
            c.start()

    for j in range(TILE_SLOTS):
        @pl.when(nt > j)
        def _():
            in_copy(j, j).start()

    @pl.when(e + 1 < N_EXPERTS)
    def _():
        @pl.when(ts_ref[e + 2] > ts_ref[e + 1])
        def _():
            for c in w_copies(e + 1, 1 - wslot):
                c.start(priority=1)

    @pl.when(nt > 0)
    def _():
        for c in w_copies(e, wslot):
            c.wait()
        wgb[...] = wstage[wslot, 0].astype(BF16)
        wub[...] = wstage[wslot, 1].astype(BF16)
        wdb[...] = wstage[wslot, 2].astype(BF16)
        e_f = e.astype(F32)

        def acquire(t):
            slot = t % TILE_SLOTS
            in_copy(t, slot).wait()

            @pl.when(t >= TILE_SLOTS)
            def _():
                out_copy(t0 + t - TILE_SLOTS, slot).wait()

        def compute(t):
            slot = t % TILE_SLOTS
            xw = xbuf[slot]
            x = xw[:, :D_MODEL]
            info = xw[:, D_MODEL:].astype(F32)
            w_row = jnp.zeros((ROW_TILE, 1), F32)
            for k in range(TOP_K):
                wk = info[:, TOP_K + k:TOP_K + k + 1] + info[:, 2 * TOP_K + k:2 * TOP_K + k + 1]
                w_row = w_row + jnp.where(info[:, k:k + 1] == e_f, wk, 0.0)
            g = jnp.minimum(_dot(x, wgb[...]) + bg_ref[0], SWIGLU_LIMIT)
            u = jnp.clip(_dot(x, wub[...]) + bu_ref[0], -SWIGLU_LIMIT, SWIGLU_LIMIT)
            act = g * jax.nn.sigmoid(SWIGLU_ALPHA * g) * (u + 1.0)
            ybuf[slot] = ((_dot(act.astype(BF16), wdb[...]) + bd_ref[0]) * w_row).astype(BF16)

        def release(t):
            slot = t % TILE_SLOTS
            out_copy(t0 + t, slot).start()

            @pl.when(t + TILE_SLOTS < nt)
            def _():
                in_copy(t + TILE_SLOTS, slot).start()

        def pair(p, carry):
            ta, tb = 2 * p, 2 * p + 1
            acquire(ta)
            acquire(tb)
            compute(ta)
            compute(tb)
            release(ta)
            release(tb)
            return carry

        lax.fori_loop(0, nt // 2, pair, 0)

        @pl.when(nt % 2 == 1)
        def _():
            acquire(nt - 1)
            compute(nt - 1)
            release(nt - 1)

        for j in range(1, TILE_SLOTS + 1):
            @pl.when(nt >= j)
            def _():
                out_copy(t0 + nt - j, (nt - j) % TILE_SLOTS).wait()

    def zero_copy(tile):
        rows = pl.ds(pl.multiple_of(tile * ROW_TILE, ROW_TILE), ROW_TILE)
        return pltpu.make_async_copy(zbuf, y_ref.at[rows], zsem)

    def for_unused_tiles(fn):
        lax.fori_loop(ts_ref[N_EXPERTS], n_tiles_all, lambda tile, c: (fn(zero_copy(tile)), c)[1], 0)

    @pl.when(e == 0)
    def _():
        zbuf[...] = jnp.zeros_like(zbuf)
        for_unused_tiles(lambda c: c.start(priority=1))

    @pl.when(e == pl.num_programs(0) - 1)
    def _():
        for_unused_tiles(lambda c: c.wait())


def _experts(tile_start, xs, w_gate, b_gate, w_up, b_up, w_down, b_down):
    n_rows = xs.shape[0]
    b_spec = pl.BlockSpec((1, 1, D_MODEL), lambda e, ts: (e, 0, 0))
    any_spec = pl.BlockSpec(memory_space=pl.ANY)
    return pl.pallas_call(
        functools.partial(_expert_kernel, n_rows // ROW_TILE),
        out_shape=jax.ShapeDtypeStruct((n_rows, D_MODEL), BF16),
        grid_spec=pltpu.PrefetchScalarGridSpec(
            num_scalar_prefetch=1,
            grid=(N_EXPERTS,),
            in_specs=[any_spec, any_spec, any_spec, any_spec, b_spec, b_spec, b_spec],
            out_specs=any_spec,
            scratch_shapes=[pltpu.VMEM((2, 3, D_MODEL, D_MODEL), F32)]
                           + [pltpu.VMEM((D_MODEL, D_MODEL), BF16)] * 3
                           + [pltpu.VMEM((TILE_SLOTS, ROW_TILE, ROW_W), BF16),
                              pltpu.VMEM((TILE_SLOTS, ROW_TILE, D_MODEL), BF16),
                              pltpu.VMEM((ROW_TILE, D_MODEL), BF16),
                              pltpu.SemaphoreType.DMA((2,)), pltpu.SemaphoreType.DMA((TILE_SLOTS,)),
                              pltpu.SemaphoreType.DMA((TILE_SLOTS,)), pltpu.SemaphoreType.DMA],
        ),
        compiler_params=pltpu.CompilerParams(dimension_semantics=("arbitrary",), vmem_limit_bytes=VMEM_LIMIT),
        name="experts",
    )(tile_start, xs, w_gate, w_up, w_down, b_gate[:, None, :], b_up[:, None, :], b_down[:, None, :])


def _combine_kernel(gdst_ref, gdst_next_ref, lr_ref, y_ref, x1_ref, gt_ref, gf_ref, out_ref, loc_ref, sem):
    i = pl.program_id(0)
    slot = i % 2
    x1 = x1_ref[...]
    sb, l, _ = x1.shape

    @pl.when(i == 0)
    def _():
        _start_all(_group_copies(loc_ref, slot, y_ref, gdst_ref, sem, False))

    _wait_groups(loc_ref, slot, sem)

    prefetch = _group_copies(loc_ref, 1 - slot, y_ref, gdst_next_ref, sem, False)
    lr = lr_ref[...]
    moe = jnp.zeros((sb * l, D_MODEL), F32)
    for c in range(N_CHUNKS):
        rows = loc_ref[slot, c * MASK_ROWS:(c + 1) * MASK_ROWS, :]
        moe = moe + lax.dot_general(_sort_matrix(lr, c), rows, (((0,), (0,)), ((), ())),
                                    preferred_element_type=F32)
        _start_all(prefetch[c * CHUNK_GROUPS:(c + 1) * CHUNK_GROUPS])
    x2 = x1 + gt_ref[...] * moe.reshape(sb, l, D_MODEL)
    out_ref[...] = x2 * lax.rsqrt(jnp.mean(x2 * x2, axis=-1, keepdims=True) + EPS) * gf_ref[...]

    @pl.when(i == pl.num_programs(0) - 1)
    def _():
        _wait_groups(loc_ref, 1 - slot, sem)


def _combine(gdst, lr, y, x1, gt, g_final, sb, l, seq_div, blk_off):
    n3 = x1.shape[0]
    nblk = n3 // sb
    tm = sb * l
    assert tm == TOK_TILE
    return pl.pallas_call(
        _combine_kernel,
        out_shape=jax.ShapeDtypeStruct(x1.shape, F32),
        grid=(nblk,),
        in_specs=[pl.BlockSpec((1, 1, GDST_LANES), lambda i: (i + blk_off, 0, 0), memory_space=pltpu.SMEM),
                  pl.BlockSpec((1, 1, GDST_LANES), lambda i: (jnp.minimum(i + 1, nblk - 1) + blk_off, 0, 0),
                               memory_space=pltpu.SMEM),
                  pl.BlockSpec((TOP_K, tm), lambda i: (0, i + blk_off)),
                  pl.BlockSpec(memory_space=pl.ANY),
                  pl.BlockSpec((sb, l, D_MODEL), lambda i: (i, 0, 0)),
                  _ada_spec(gt, sb, seq_div),
                  pl.BlockSpec((1, D_MODEL), lambda i: (0, 0))],
        out_specs=pl.BlockSpec((sb, l, D_MODEL), lambda i: (i, 0, 0)),
        scratch_shapes=[pltpu.VMEM((2, LOCAL_ROWS, D_MODEL), BF16), pltpu.SemaphoreType.DMA((2,))],
        compiler_params=pltpu.CompilerParams(dimension_semantics=("arbitrary",), vmem_limit_bytes=VMEM_LIMIT),
        name="combine",
    )(gdst, gdst, lr, y, x1, gt[0], g_final)


def kernel(x_prompt, x_sample, c_prompt, c_sample, state_ssm, state_conv, w_ada, b_ada, g_mix, w_in, g_v_a, w_spatial, b_spatial, g_out_a, conv_w, conv_b, dt_bias, a_log, d_skip, g_out_b, w_out, g_ffn, w_router, b_router, w_gate, b_gate, w_up, b_up, w_down, b_down, g_final):
    assert w_ada.shape[0] == 1, "single-layer step"
    p = dict(w_in=w_in[0], g_mix=g_mix[0], g_v_a=g_v_a[0], w_spatial=w_spatial[0], b_spatial=b_spatial[0],
             g_out_a=g_out_a[0], conv_w=conv_w[0], conv_b=conv_b[0], dt_bias=dt_bias[0], a_log=a_log[0],
             d_skip=d_skip[0], g_out_b=g_out_b[0])
    bp, lp, _ = x_prompt.shape
    bs, ls, _ = x_sample.shape
    tp, ts = bp * lp, bs * ls

    ada = _ada(jnp.concatenate([c_sample, c_prompt], axis=0), w_ada[0], b_ada[0][None, :])
    ada = ada.reshape(bs + bp, 1, 6 * D_MODEL)
    ada_s = [(ada, j, 0) for j in range(6)]
    ada_p = [(ada, j, bs) for j in range(6)]

    fw = _front_weights(p)
    mixed_p, conv_p, ssm_p = _prompt_mixer(x_prompt, ada_p[0], ada_p[1], fw, p)
    mixed_s, v_s, conv_s, ssm_s = _sample_mixer(x_sample, ada_s[0], ada_s[1], state_ssm[0], state_conv[0], fw, p)

    w_out_b = w_out[0].astype(BF16)
    g_ffn2 = g_ffn[0][None, :]
    wr_t = w_router[0].T
    wr_hi = wr_t.astype(BF16)
    wr_both = jnp.concatenate([wr_hi, (wr_t - wr_hi.astype(F32)).astype(BF16)], axis=0)
    br = b_router[0][:, None]
    tps = lp // TOK_TILE
    sbs = TOK_TILE // ls
    xp3 = x_prompt.reshape(tp // POST_TILE, POST_TILE, D_MODEL)
    x1_p, h2p_p, ids_p = _post(mixed_p, xp3, ada_p[2], ada_p[4], ada_p[3], w_out_b, g_ffn2, wr_both, br,
                               1, POST_TILE, lp // POST_TILE)
    x1_p = x1_p.reshape(bp * tps, TOK_TILE, D_MODEL)
    x1_s, h2p_s, ids_s = _post(mixed_s, x_sample, ada_s[2], ada_s[4], ada_s[3], w_out_b, g_ffn2, wr_both, br,
                               POST_TILE // ls, ls, 1)

    n_blocks = (tp + ts) // TOK_TILE
    max_groups = (tp + ts) * TOP_K // ROW_GROUP + n_blocks * N_EXPERTS + N_EXPERTS * (TILE_GROUPS - 1)
    n_tiles = -(-max_groups // TILE_GROUPS)
    n_tiles_all = n_tiles + 2 * LOCAL_GROUPS // TILE_GROUPS
    lr, gdst, meta = _plan(jnp.concatenate([ids_p, ids_s], axis=1), n_tiles * TILE_GROUPS)

    xs = _dispatch(gdst, meta, lr, h2p_p, h2p_s, n_tiles_all)
    y = _experts(meta[0], xs, w_gate[0], b_gate[0], w_up[0], b_up[0], w_down[0], b_down[0])

    gf = g_final[None, :]
    y_p = _combine(gdst, lr, y, x1_p, ada_p[5], gf, 1, TOK_TILE, tps, 0).reshape(bp, lp, D_MODEL)
    y_s = _combine(gdst, lr, y, x1_s, ada_s[5], gf, sbs, ls, 1, tp // TOK_TILE)

    return (y_p, y_s, ssm_p[None], conv_p[None], ssm_s[None], conv_s[None], v_s.reshape(1, bs, ls, A_WIDTH))
```

```python
import functools
import math

import numpy as np
import jax
import jax.numpy as jnp
from jax import lax
from jax.experimental import pallas as pl
from jax.experimental.pallas import tpu as pltpu

F32 = jnp.float32
BF16 = jnp.bfloat16
I32 = jnp.int32

D_MODEL = 1024
A_WIDTH = 512
A_HEADS = 4
A_HEAD_DIM = 128
CHUNK = 128
B_WIDTH = 512
SSD_HEAD_DIM = 64
SSD_HEADS = 8
SSD_GROUPS = 2
SSD_STATE = 128
GROUP_W = B_WIDTH // SSD_GROUPS
CONV_K = 4
CONV_DIM = 1024
CONV_PAD = 8
N_EXPERTS = 32
TOP_K = 4
SWIGLU_LIMIT = 7.0
SWIGLU_ALPHA = 1.702
EPS = 1e-6
DECAY_MASKED = -1e30
LANES = 128

TOK_TILE = 512
SAMPLE_SEQ_TILE = 16
SEQ_UNROLL = 8
ROW_TILE = 256
ROW_GROUP = 16
TILE_GROUPS = ROW_TILE // ROW_GROUP
LOCAL_GROUPS = TOK_TILE * TOP_K // ROW_GROUP + N_EXPERTS
LOCAL_ROWS = LOCAL_GROUPS * ROW_GROUP
GDST_LANES = -(-LOCAL_GROUPS // LANES) * LANES
ROW_W = D_MODEL + LANES
INFO_ROWS = 16
MASK_ROWS = 256
META_ROWS = 8
POST_TILE = 1024
PROMPT_TILE = 512
PROMPT_SPLITS = 2
TILE_SLOTS = 4
VMEM_LIMIT = 56 * 1024 * 1024


def _dot(a, b):
    return jnp.dot(a, b, preferred_element_type=F32)


def _dot_nt(a, b):
    return lax.dot_general(a, b, (((1,), (1,)), ((), ())), preferred_element_type=F32)


def _split(x):
    hi = x.astype(BF16)
    lo = (x - hi.astype(F32)).astype(BF16)
    return hi, lo


def _dot_exact_l(t, x):
    hi, lo = _split(x)
    return _dot(t, hi) + _dot(t, lo)


def _dot_exact_r(x, t):
    hi, lo = _split(x)
    return _dot(hi, t) + _dot(lo, t)


def _silu(x):
    return x * jax.nn.sigmoid(x)


def _gelu(x):
    return 0.5 * x * (1.0 + lax.erf(x * (1.0 / math.sqrt(2.0))))


def _softplus(x):
    return jnp.maximum(x, 0.0) + jnp.log1p(jnp.exp(-jnp.abs(x)))


def _rms(x, g):
    return x * lax.rsqrt(jnp.mean(x * x, axis=-1, keepdims=True) + EPS) * g


def _ada_kernel(c_ref, w_ref, b_ref, o_ref):
    s_hi, s_lo = _split(_silu(c_ref[...]))
    w_hi, w_lo = _split(w_ref[...])
    res = _dot(s_hi, w_hi) + _dot(s_lo, w_hi) + _dot(s_hi, w_lo) + b_ref[...]
    o_ref[...] = res.reshape(o_ref.shape)


def _ada(c_all, w_ada, b_ada):
    m = c_all.shape[0]
    n = w_ada.shape[1]
    bn = 1024
    return pl.pallas_call(
        _ada_kernel,
        out_shape=jax.ShapeDtypeStruct((m, 1, n), F32),
        grid=(n // bn,),
        in_specs=[pl.BlockSpec((m, D_MODEL), lambda j: (0, 0)),
                  pl.BlockSpec((D_MODEL, bn), lambda j: (0, j)),
                  pl.BlockSpec((1, bn), lambda j: (0, j))],
        out_specs=pl.BlockSpec((m, 1, bn), lambda j: (0, 0, j)),
        compiler_params=pltpu.CompilerParams(dimension_semantics=("arbitrary",), vmem_limit_bytes=VMEM_LIMIT),
        name="ada",
    )(c_all, w_ada, b_ada)


def _mixer_front(x3, sh, sc, prev, refs, xp_ref):
    (g_mix, w_uvz, w_xbc, w_dt, wbd, bias_sp, g_v, g_oa, conv_w, conv_b, dt_bias, a_row) = refs
    sb, l, _ = x3.shape
    tm = sb * l
    xn = x3 * lax.rsqrt(jnp.mean(x3 * x3, axis=-1, keepdims=True) + EPS) * g_mix[...]
    h = (xn * (1.0 + sc) + sh).reshape(tm, D_MODEL)
    hb = h.astype(BF16)
    uvz = _dot(hb, w_uvz[...])
    xbc = _dot(hb, w_xbc[...])
    dt_raw = _dot(hb, w_dt[...])

    u = _gelu(uvz[:, :A_WIDTH])
    vg = _gelu(uvz[:, A_WIDTH:2 * A_WIDTH])
    z = uvz[:, 2 * A_WIDTH:]
    v_parts, s_parts = [], []
    for hd in range(A_HEADS):
        sl = slice(hd * A_HEAD_DIM, (hd + 1) * A_HEAD_DIM)
        vh = _rms(vg[:, sl], g_v[:, sl])
        v_parts.append(vh)
        vb = vh.astype(BF16)
        s_parts.append(jnp.concatenate([_dot(wbd[hd], vb[r0:r0 + CHUNK]) + bias_sp[:, sl]
                                        for r0 in range(0, tm, CHUNK)], axis=0))
    v = jnp.concatenate(v_parts, axis=1)
    s_a = jnp.concatenate(s_parts, axis=1)
    out_a = _rms(u * s_a, g_oa[...])

    xp_ref[:, 0:CONV_PAD, :] = prev
    xp_ref[:, CONV_PAD:, :] = xbc.reshape(sb, l, CONV_DIM)
    xp = xp_ref[...]
    acc = conv_b[...] + xp[:, CONV_PAD:, :] * conv_w[CONV_K - 1:CONV_K, :]
    for s in range(1, CONV_K):
        back = pltpu.roll(xp, s, axis=1)[:, CONV_PAD:, :]
        acc = acc + back * conv_w[CONV_K - 1 - s:CONV_K - s, :]
    xc = _silu(acc).reshape(tm, CONV_DIM)
    dt = _softplus(dt_raw + dt_bias[...])
    d_a = dt * a_row[...]
    return out_a, v, z, xc, dt, d_a


def _ssd_chunk(xs, bm, cm, dt, d_a, cref):
    tril, ones, expand, neg_mask = cref
    cs = _dot_exact_l(tril[...], d_a)
    cs_t = cs.T
    dt_t = dt.T
    cs_tot = _dot_exact_l(ones[...], d_a)
    vals = jnp.concatenate([dt * jnp.exp(cs_tot - cs), jnp.exp(cs)], axis=0)
    vals_e = _dot_exact_r(vals, expand[...])
    n = xs.shape[0]
    w_e, e_e = vals_e[:n], vals_e[n:]
    xdtd = xs * w_e
    neg = neg_mask[...]
    row_lt_half = lax.broadcasted_iota(I32, (2 * n, LANES), 0) < n
    lane_lt_half = lax.broadcasted_iota(I32, (2 * n, LANES), 1) < SSD_HEAD_DIM
    y_parts = []
    for g in range(SSD_GROUPS):
        cb = _dot_nt(cm[:, g * SSD_STATE:(g + 1) * SSD_STATE].astype(BF16),
                     bm[:, g * SSD_STATE:(g + 1) * SSD_STATE].astype(BF16))
        for hp in range(SSD_HEADS // SSD_GROUPS // 2):
            h0 = g * (SSD_HEADS // SSD_GROUPS) + 2 * hp
            ms = []
            for hh in (h0, h0 + 1):
                diff = cs[:, hh:hh + 1] - cs_t[hh:hh + 1, :]
                ms.append((cb * jnp.exp(diff + neg) * dt_t[hh:hh + 1, :]).astype(BF16))
            pair = xs[:, h0 * SSD_HEAD_DIM:(h0 + 2) * SSD_HEAD_DIM]
            rhs = jnp.where(row_lt_half == lane_lt_half, jnp.concatenate([pair, pair], axis=0), 0.0).astype(BF16)
            y_parts.append(_dot(jnp.concatenate(ms, axis=1), rhs))
    y_diag = jnp.concatenate(y_parts, axis=1)
    return y_diag, e_e, xdtd, cs_tot


def _mixer_back(y, xs, z, out_a, dskip_e, g_ob):
    y = y + xs * dskip_e
    gated = y * _silu(z)
    parts = [_rms(gated[:, g * GROUP_W:(g + 1) * GROUP_W], g_ob[:, g * GROUP_W:(g + 1) * GROUP_W])
             for g in range(SSD_GROUPS)]
    return jnp.concatenate([out_a] + parts, axis=1).astype(BF16)


N_FRONT = 12
N_SSD = 4


def _prompt_mixer_kernel(tiles_per_seq, x_ref, sh_ref, sc_ref, *rest):
    front = rest[:N_FRONT]
    cref = rest[N_FRONT:N_FRONT + N_SSD]
    dskip_e, g_ob = rest[N_FRONT + N_SSD:N_FRONT + N_SSD + 2]
    mixed_ref, conv_out_ref, ssm_out_ref = rest[N_FRONT + N_SSD + 2:N_FRONT + N_SSD + 5]
    xp_ref, carry_ref, st_ref = rest[N_FRONT + N_SSD + 5:]
    i = pl.program_id(0)
    first = (i % tiles_per_seq) == 0

    @pl.when(first)
    def _():
        carry_ref[...] = jnp.zeros_like(carry_ref)
        st_ref[...] = jnp.zeros_like(st_ref)

    l = x_ref.shape[1]
    hl = l // PROMPT_SPLITS
    parts = []
    prev = carry_ref[...]
    for hh in range(PROMPT_SPLITS):
        xp_h = xp_ref.at[hh]
        parts.append(_mixer_front(x_ref[:, hh * hl:(hh + 1) * hl, :], sh_ref[...], sc_ref[...], prev, front, xp_h))
        prev = xp_h[:, hl:hl + CONV_PAD, :]
    carry_ref[...] = prev
    out_a, _, z, xc, dt, d_a = [jnp.concatenate([p[j] for p in parts], axis=0) for j in range(6)]
    xs = xc[:, :B_WIDTH]
    y_rows = []
    for c in range(l // CHUNK):
        r = slice(c * CHUNK, (c + 1) * CHUNK)
        bm = xc[r, B_WIDTH:B_WIDTH + SSD_GROUPS * SSD_STATE]
        cm = xc[r, B_WIDTH + SSD_GROUPS * SSD_STATE:]
        y_diag, e_e, xdtd, _ = _ssd_chunk(xs[r], bm, cm, dt[r], d_a[r], cref)
        st = st_ref[...]
        y_off, upd = [], []
        for g in range(SSD_GROUPS):
            gs = slice(g * GROUP_W, (g + 1) * GROUP_W)
            ns = slice(g * SSD_STATE, (g + 1) * SSD_STATE)
            y_off.append(_dot(cm[:, ns].astype(BF16), st[:, gs].astype(BF16)))
            upd.append(_dot(bm[:, ns].T.astype(BF16), xdtd[:, gs].astype(BF16)))
        y_rows.append(y_diag + jnp.concatenate(y_off, axis=1) * e_e)
        st_ref[...] = st * e_e[CHUNK - 1:CHUNK, :] + jnp.concatenate(upd, axis=1)
    y = jnp.concatenate(y_rows, axis=0)
    mixed_ref[...] = _mixer_back(y, xs, z, out_a, dskip_e[...], g_ob[...])

    @pl.when((i % tiles_per_seq) == tiles_per_seq - 1)
    def _():
        conv_out_ref[...] = xp_ref[PROMPT_SPLITS - 1, :, hl + CONV_PAD - (CONV_K - 1):hl + CONV_PAD, :]
        ssm_out_ref[0] = st_ref[...].T


def _sample_mixer_kernel(x_ref, sh_ref, sc_ref, prev_ref, ssm0_ref, *rest):
    front = rest[:N_FRONT]
    cref = rest[N_FRONT:N_FRONT + N_SSD]
    dskip_e, g_ob, selseq = rest[N_FRONT + N_SSD:N_FRONT + N_SSD + 3]
    mixed_ref, v_ref, conv_out_ref, ssm_out_ref = rest[N_FRONT + N_SSD + 3:N_FRONT + N_SSD + 7]
    xp_ref, yoff_ref, cbf_ref, bbf_ref, t1_ref, dtab_ref = rest[N_FRONT + N_SSD + 7:]
    x3 = x_ref[...]
    sb, l, _ = x3.shape
    tm = sb * l
    out_a, v, z, xc, dt, d_a = _mixer_front(x3, sh_ref[...], sc_ref[...], prev_ref[...], front, xp_ref)
    v_ref[...] = v
    conv_out_ref[...] = xp_ref[:, l + CONV_PAD - (CONV_K - 1):l + CONV_PAD, :]
    xs = xc[:, :B_WIDTH]
    bm = xc[:, B_WIDTH:B_WIDTH + SSD_GROUPS * SSD_STATE]
    cm = xc[:, B_WIDTH + SSD_GROUPS * SSD_STATE:]
    y_diag, e_e, xdtd, _ = _ssd_chunk(xs, bm, cm, dt, d_a, cref)

    e_tot = jnp.exp(_dot_exact_l(selseq[...], d_a))
    for hh in range(SSD_HEADS):
        dtab_ref[hh] = jnp.broadcast_to(e_tot[:, hh:hh + 1], (sb, LANES))
    cbf_ref[...] = cm
    bbf_ref[...] = bm
    for g in range(SSD_GROUPS):
        t1_ref[g] = xdtd[:, g * GROUP_W:(g + 1) * GROUP_W].T.astype(BF16)
    seq_of_row = lax.broadcasted_iota(I32, (tm, SSD_STATE), 0) // l
    heads_per_group = SSD_HEADS // SSD_GROUPS

    def one_seq(j):
        r0 = pl.multiple_of(j * l, l)
        s0 = ssm0_ref[j]
        for g in range(SSD_GROUPS):
            ns = slice(g * SSD_STATE, (g + 1) * SSD_STATE)
            s0g = s0[g * heads_per_group:(g + 1) * heads_per_group].reshape(GROUP_W, SSD_STATE)
            cj = cbf_ref[pl.ds(r0, l), ns].astype(BF16)
            yoff_ref[pl.ds(r0, l), g * GROUP_W:(g + 1) * GROUP_W] = _dot_nt(cj, s0g.astype(BF16))
            bmask = jnp.where(seq_of_row == j, bbf_ref[:, ns], 0.0).astype(BF16)
            upd = _dot(t1_ref[g], bmask)
            for hq in range(heads_per_group):
                hh = g * heads_per_group + hq
                dec = dtab_ref[hh, pl.ds(j, 1), :]
                ssm_out_ref[j, hh] = s0[hh] * dec + upd[hq * SSD_HEAD_DIM:(hq + 1) * SSD_HEAD_DIM]

    def body(jj, carry):
        for u in range(SEQ_UNROLL):
            one_seq(jj * SEQ_UNROLL + u)
        return carry

    lax.fori_loop(0, sb // SEQ_UNROLL, body, 0)
    y = y_diag + yoff_ref[...] * e_e
    mixed_ref[...] = _mixer_back(y, xs, z, out_a, dskip_e[...], g_ob[...])


def _ada_spec(ref, sb, div):
    _, term, row0 = ref
    assert row0 % sb == 0
    return pl.BlockSpec((sb, 1, D_MODEL), lambda i: (row0 // sb + i // div, 0, term))


def _const_spec(a):
    nd = a.ndim
    return pl.BlockSpec(a.shape, lambda i, _nd=nd: (0,) * _nd)


def _spatial_consts(w_spatial, b_spatial, cl, tm):
    w = jnp.where(jnp.tril(jnp.ones((cl, cl), bool)), w_spatial[:, :cl, :cl], 0.0)
    eye = jnp.eye(tm // cl, dtype=F32)
    wbd = jnp.einsum("ab,hts->hatbs", eye, w).reshape(A_HEADS, tm, tm).astype(BF16)
    bias = jnp.tile(jnp.repeat(b_spatial[:, :cl].T, A_HEAD_DIM, axis=1), (tm // cl, 1))
    return wbd, bias


def _ssd_consts(cl):
    r = np.arange(CHUNK)
    same = (r[:, None] // cl) == (r[None, :] // cl)
    tril = same & (r[:, None] >= r[None, :])
    expand = np.zeros((LANES, B_WIDTH), np.float32)
    for hh in range(SSD_HEADS):
        expand[hh, hh * SSD_HEAD_DIM:(hh + 1) * SSD_HEAD_DIM] = 1.0
    return (jnp.asarray(tril, BF16), jnp.asarray(same, BF16), jnp.asarray(expand, BF16),
            jnp.asarray(np.where(tril, 0.0, DECAY_MASKED), F32))


def _front_weights(p):
    w_in = p["w_in"]
    c0, c1 = 3 * A_WIDTH, 3 * A_WIDTH + CONV_DIM
    w_dt = w_in[:, c1:]
    pad8 = lambda v: jnp.pad(v, (0, LANES - SSD_HEADS))
    a = -jnp.exp(p["a_log"])
    return dict(
        g_mix=p["g_mix"][None, :],
        w_uvz=w_in[:, :c0].astype(BF16),
        w_xbc=w_in[:, c0:c1].astype(BF16),
        w_dt=jnp.pad(w_dt, ((0, 0), (0, LANES - SSD_HEADS))).astype(BF16),
        g_v=p["g_v_a"][None, :], g_oa=p["g_out_a"][None, :],
        conv_w=p["conv_w"], conv_b=p["conv_b"][None, :],
        dt_bias=pad8(p["dt_bias"])[None, :],
        a_row=pad8(a)[None, :],
        dskip_e=jnp.repeat(p["d_skip"], SSD_HEAD_DIM)[None, :],
        g_ob=p["g_out_b"][None, :],
    )


def _front_list(fw, wbd, bias_sp):
    return [fw["g_mix"], fw["w_uvz"], fw["w_xbc"], fw["w_dt"], wbd, bias_sp, fw["g_v"], fw["g_oa"],
            fw["conv_w"], fw["conv_b"], fw["dt_bias"], fw["a_row"]]


def _prompt_mixer(x, sh, sc, fw, p):
    nseq, lseq, _ = x.shape
    tile = PROMPT_TILE
    tps = lseq // tile
    nt = nseq * tps
    x4 = x.reshape(nt, tile, D_MODEL)
    wbd, bias_sp = _spatial_consts(p["w_spatial"], p["b_spatial"], CHUNK, CHUNK)
    consts = _front_list(fw, wbd, bias_sp) + list(_ssd_consts(CHUNK)) + [fw["dskip_e"], fw["g_ob"]]
    mixed, conv_new, ssm_new = pl.pallas_call(
        functools.partial(_prompt_mixer_kernel, tps),
        out_shape=(jax.ShapeDtypeStruct((nt * tile, D_MODEL), BF16),
                   jax.ShapeDtypeStruct((nseq, CONV_K - 1, CONV_DIM), F32),
                   jax.ShapeDtypeStruct((nseq, B_WIDTH, SSD_STATE), F32)),
        grid=(nt,),
        in_specs=[pl.BlockSpec((1, tile, D_MODEL), lambda i: (i, 0, 0)), _ada_spec(sh, 1, tps), _ada_spec(sc, 1, tps)]
                 + [_const_spec(a) for a in consts],
        out_specs=(pl.BlockSpec((tile, D_MODEL), lambda i: (i, 0)),
                   pl.BlockSpec((1, CONV_K - 1, CONV_DIM), lambda i: (i // tps, 0, 0)),
                   pl.BlockSpec((1, B_WIDTH, SSD_STATE), lambda i: (i // tps, 0, 0))),
        scratch_shapes=[pltpu.VMEM((PROMPT_SPLITS, 1, tile // PROMPT_SPLITS + CONV_PAD, CONV_DIM), F32),
                        pltpu.VMEM((1, CONV_PAD, CONV_DIM), F32),
                        pltpu.VMEM((SSD_STATE, B_WIDTH), F32)],
        compiler_params=pltpu.CompilerParams(dimension_semantics=("arbitrary",), vmem_limit_bytes=VMEM_LIMIT),
        name="prompt_mixer",
    )(x4, sh[0], sc[0], *consts)
    return mixed, conv_new, ssm_new.reshape(nseq, SSD_HEADS, SSD_HEAD_DIM, SSD_STATE)


def _sample_mixer(x, sh, sc, state_ssm, state_conv, fw, p):
    nseq, l, _ = x.shape
    sb = SAMPLE_SEQ_TILE
    tm = sb * l
    assert tm == CHUNK
    wbd, bias_sp = _spatial_consts(p["w_spatial"], p["b_spatial"], l, tm)
    selseq = jnp.asarray((np.arange(tm)[None, :] // l) == np.arange(sb)[:, None], BF16)
    consts = _front_list(fw, wbd, bias_sp) + list(_ssd_consts(l)) + [fw["dskip_e"], fw["g_ob"], selseq]
    prev = jnp.pad(state_conv, ((0, 0), (CONV_PAD - (CONV_K - 1), 0), (0, 0)))
    ssm_spec = pl.BlockSpec((sb, SSD_HEADS, SSD_HEAD_DIM, SSD_STATE), lambda i: (i, 0, 0, 0))
    return pl.pallas_call(
        _sample_mixer_kernel,
        out_shape=(jax.ShapeDtypeStruct((nseq * l, D_MODEL), BF16),
                   jax.ShapeDtypeStruct((nseq * l, A_WIDTH), F32),
                   jax.ShapeDtypeStruct((nseq, CONV_K - 1, CONV_DIM), F32),
                   jax.ShapeDtypeStruct(state_ssm.shape, F32)),
        grid=(nseq // sb,),
        in_specs=[pl.BlockSpec((sb, l, D_MODEL), lambda i: (i, 0, 0)), _ada_spec(sh, sb, 1), _ada_spec(sc, sb, 1),
                  pl.BlockSpec((sb, CONV_PAD, CONV_DIM), lambda i: (i, 0, 0)), ssm_spec]
                 + [_const_spec(a) for a in consts],
        out_specs=(pl.BlockSpec((tm, D_MODEL), lambda i: (i, 0)),
                   pl.BlockSpec((tm, A_WIDTH), lambda i: (i, 0)),
                   pl.BlockSpec((sb, CONV_K - 1, CONV_DIM), lambda i: (i, 0, 0)),
                   ssm_spec),
        scratch_shapes=[pltpu.VMEM((sb, l + CONV_PAD, CONV_DIM), F32),
                        pltpu.VMEM((tm, B_WIDTH), F32),
                        pltpu.VMEM((tm, SSD_GROUPS * SSD_STATE), F32),
                        pltpu.VMEM((tm, SSD_GROUPS * SSD_STATE), F32),
                        pltpu.VMEM((SSD_GROUPS, GROUP_W, tm), BF16),
                        pltpu.VMEM((SSD_HEADS, sb, LANES), F32)],
        compiler_params=pltpu.CompilerParams(dimension_semantics=("arbitrary",), vmem_limit_bytes=VMEM_LIMIT),
        name="sample_mixer",
    )(x, sh[0], sc[0], prev, state_ssm, *consts)


def _post_kernel(mixed_ref, x_ref, gt_ref, sc_ref, sh_ref, w_out_ref, g_ffn_ref, wr_both_ref, br_ref,
                 x1_ref, h2p_ref, ids_ref):
    x3 = x_ref[...]
    sb, l, _ = x3.shape
    tm = sb * l
    mix = _dot(mixed_ref[...], w_out_ref[...]).reshape(sb, l, D_MODEL)
    x1 = x3 + gt_ref[...] * mix
    x1_ref[...] = x1
    xn = x1 * lax.rsqrt(jnp.mean(x1 * x1, axis=-1, keepdims=True) + EPS) * g_ffn_ref[...]
    h2 = (xn * (1.0 + sc_ref[...]) + sh_ref[...]).reshape(tm, D_MODEL)
    h_hi, h_lo = _split(h2)
    h2p_ref[:, :D_MODEL] = h_hi
    both = _dot_nt(wr_both_ref[...], h_hi)
    logits = (both[:N_EXPERTS] + both[N_EXPERTS:] + _dot_nt(wr_both_ref[:N_EXPERTS, :], h_lo)
              + br_ref[...])
    e_iota = lax.broadcasted_iota(I32, logits.shape, 0)
    vals, idxs = [], []
    for _ in range(TOP_K):
        m = jnp.max(logits, axis=0, keepdims=True)
        idx = jnp.min(jnp.where(logits == m, e_iota, N_EXPERTS), axis=0, keepdims=True)
        vals.append(m)
        idxs.append(idx)
        logits = jnp.where(e_iota == idx, -jnp.inf, logits)
    ex = [jnp.exp(v - vals[0]) for v in vals]
    tot = ex[0] + ex[1] + ex[2] + ex[3]
    ids = jnp.concatenate(idxs, axis=0)
    ids_ref[...] = ids
    wts = jnp.concatenate([e / tot for e in ex], axis=0)
    w_hi = wts.astype(BF16).astype(F32)
    info = jnp.concatenate([ids.astype(F32), w_hi, wts - w_hi, jnp.zeros((TOP_K, tm), F32)], axis=0).astype(BF16)
    r = lax.broadcasted_iota(I32, (INFO_ROWS, LANES), 0)
    c = lax.broadcasted_iota(I32, (INFO_ROWS, LANES), 1)
    place = jnp.where(r == c, 1.0, 0.0).astype(BF16)
    h2p_ref[:, D_MODEL:] = lax.dot_general(info, place, (((0,), (0,)), ((), ())),
                                           preferred_element_type=F32).astype(BF16)


def _post(mixed, x, gt, sc, sh, w_out_b, g_ffn, wr_both, br, sb, l, seq_div):
    n3, _, _ = x.shape
    nblk = n3 // sb
    tm = sb * l
    t = n3 * l
    consts = [w_out_b, g_ffn, wr_both, br]
    return pl.pallas_call(
        _post_kernel,
        out_shape=(jax.ShapeDtypeStruct(x.shape, F32),
                   jax.ShapeDtypeStruct((t, ROW_W), BF16),
                   jax.ShapeDtypeStruct((TOP_K, t), I32)),
        grid=(nblk,),
        in_specs=[pl.BlockSpec((tm, D_MODEL), lambda i: (i, 0)),
                  pl.BlockSpec((sb, l, D_MODEL), lambda i: (i, 0, 0)),
                  _ada_spec(gt, sb, seq_div), _ada_spec(sc, sb, seq_div), _ada_spec(sh, sb, seq_div)]
                 + [_const_spec(a) for a in consts],
        out_specs=(pl.BlockSpec((sb, l, D_MODEL), lambda i: (i, 0, 0)),
                   pl.BlockSpec((tm, ROW_W), lambda i: (i, 0)),
                   pl.BlockSpec((TOP_K, tm), lambda i: (0, i))),
        compiler_params=pltpu.CompilerParams(dimension_semantics=("arbitrary",), vmem_limit_bytes=VMEM_LIMIT),
        name="post",
    )(mixed, x, gt[0], sc[0], sh[0], *consts)


def _strict_upper(n):
    r = lax.broadcasted_iota(I32, (n, n), 0)
    c = lax.broadcasted_iota(I32, (n, n), 1)
    return jnp.where(r < c, 1.0, 0.0).astype(BF16)


def _expert_prefix(col):
    r = lax.broadcasted_iota(I32, (N_EXPERTS, N_EXPERTS), 0)
    c = lax.broadcasted_iota(I32, (N_EXPERTS, N_EXPERTS), 1)
    as_row = jnp.sum(jnp.where(r == c, col, 0.0), axis=0, keepdims=True)
    return jnp.sum(jnp.where(c < r, as_row, 0.0), axis=1, keepdims=True)


def _plan_kernel(dump_group, ids_ref, lr_ref, gdst_ref, tile_e_ref):
    tm = TOK_TILE
    nb = ids_ref.shape[1] // tm
    blk_lane = lax.broadcasted_iota(I32, (N_EXPERTS, LANES), 1)

    def block_masks(b):
        ids = ids_ref[:, pl.ds(pl.multiple_of(b * tm, tm), tm)]
        e_iota = lax.broadcasted_iota(I32, (N_EXPERTS, tm), 0)
        onehot = [ids[k:k + 1, :] == e_iota for k in range(TOP_K)]
        sel = (onehot[0] | onehot[1]) | (onehot[2] | onehot[3])
        return onehot, jnp.where(sel, 1.0, 0.0)

    def count(b, seg):
        _, m = block_masks(b)
        seg_b = jnp.ceil(jnp.sum(m, axis=1, keepdims=True) * (1.0 / ROW_GROUP))
        return jnp.where(blk_lane == b, seg_b, seg)

    seg = lax.fori_loop(0, nb, count, jnp.zeros((N_EXPERTS, LANES), F32))
    tot = jnp.sum(seg, axis=1, keepdims=True)
    padded = jnp.ceil(tot * (1.0 / TILE_GROUPS)) * TILE_GROUPS
    gstart = _expert_prefix(padded)
    gb = gstart + _dot(seg.astype(BF16), _strict_upper(LANES))
    r = lax.broadcasted_iota(I32, (N_EXPERTS, LANES), 0)
    as_row = lambda col: jnp.sum(jnp.where(r == blk_lane, col, 0.0), axis=0, keepdims=True)
    n_used = jnp.sum(padded, axis=0, keepdims=True)
    lane = lax.broadcasted_iota(I32, (1, LANES), 1)
    tiles = jnp.where(lane == N_EXPERTS, n_used, as_row(gstart)) * (1.0 / TILE_GROUPS)
    meta = jnp.concatenate([tiles, as_row(gstart + tot), as_row(padded - tot),
                            jnp.zeros((META_ROWS - 3, LANES), F32)], axis=0)
    tile_e_ref[...] = meta.astype(I32)
    upper = _strict_upper(tm)

    def place(b, carry):
        onehot, m = block_masks(b)
        seg_b = jnp.sum(jnp.where(blk_lane == b, seg, 0.0), axis=1, keepdims=True)
        gb_b = jnp.sum(jnp.where(blk_lane == b, gb, 0.0), axis=1, keepdims=True)
        loc_b = _expert_prefix(seg_b)
        before = _dot(m.astype(BF16), upper) + loc_b * ROW_GROUP
        lr_ref[:, pl.ds(pl.multiple_of(b * tm, tm), tm)] = jnp.concatenate(
            [jnp.sum(jnp.where(onehot[k], before, 0.0), axis=0, keepdims=True) for k in range(TOP_K)],
            axis=0).astype(I32)
        g = lax.broadcasted_iota(I32, (N_EXPERTS, GDST_LANES), 1).astype(F32)
        inside = (loc_b <= g) & (g < loc_b + seg_b)
        dst = jnp.sum(jnp.where(inside, gb_b + g - loc_b, 0.0), axis=0, keepdims=True)
        used = jnp.sum(jnp.where(inside, 1.0, 0.0), axis=0, keepdims=True) > 0.5
        dump = dump_group + lax.convert_element_type(b % 2, F32) * LOCAL_GROUPS + g[0:1, :]
        n_local = jnp.sum(seg_b, axis=0, keepdims=True)
        gdst_ref[b] = jnp.where(g[0:1, :] == GDST_LANES - 1, n_local, jnp.where(used, dst, dump)).astype(I32)
        return carry

    lax.fori_loop(0, nb, place, 0)


def _plan(ids, dump_group):
    t = ids.shape[1]
    nb = t // TOK_TILE
    assert nb <= LANES
    return pl.pallas_call(
        functools.partial(_plan_kernel, float(dump_group)),
        out_shape=(jax.ShapeDtypeStruct((TOP_K, t), I32),
                   jax.ShapeDtypeStruct((nb, 1, GDST_LANES), I32),
                   jax.ShapeDtypeStruct((META_ROWS, LANES), I32)),
        name="plan",
    )(ids)


def _sort_matrix(lr, c):
    r_iota = lax.broadcasted_iota(I32, (MASK_ROWS, lr.shape[1]), 0) + c * MASK_ROWS
    p = jnp.where(r_iota == lr[TOP_K - 1:TOP_K, :], 1.0, 0.0)
    for k in range(TOP_K - 1):
        p = jnp.where(r_iota == lr[k:k + 1, :], 1.0, p)
    return p.astype(BF16)


def _group_copies(loc_ref, slot, far_ref, gdst_ref, sem, to_far):
    copies = []
    for g in range(LOCAL_GROUPS):
        dst = pl.multiple_of(gdst_ref[0, 0, g] * ROW_GROUP, ROW_GROUP)
        near = loc_ref.at[slot, pl.ds(g * ROW_GROUP, ROW_GROUP)]
        far = far_ref.at[pl.ds(dst, ROW_GROUP)]
        copies.append(pltpu.make_async_copy(near, far, sem.at[slot]) if to_far
                      else pltpu.make_async_copy(far, near, sem.at[slot]))
    return copies


N_CHUNKS = LOCAL_ROWS // MASK_ROWS
CHUNK_GROUPS = LOCAL_GROUPS // N_CHUNKS
SURE_GROUPS = LOCAL_GROUPS - CHUNK_GROUPS


def _start_all(copies):
    for c in copies:
        c.start()


def _uses_last_chunk(gdst_ref):
    return gdst_ref[0, 0, GDST_LANES - 1] > SURE_GROUPS


def _wait_groups(loc_ref, slot, sem, with_last=None):
    def wait(n_groups):
        part = loc_ref.at[slot, pl.ds(0, n_groups * ROW_GROUP)]
        pltpu.make_async_copy(part, part, sem.at[slot]).wait()

    if with_last is None:
        wait(LOCAL_GROUPS)
        return
    pl.when(with_last)(functools.partial(wait, LOCAL_GROUPS))
    pl.when(jnp.logical_not(with_last))(functools.partial(wait, SURE_GROUPS))


def _dispatch_kernel(n_first, first_dump_tile, n_tiles_all, gdst_ref, gdst_prev_ref, meta_ref, lr_ref, ha_ref, hb_ref,
                     xs_ref, h_ref, loc_ref, zero_ref, sem, zsem):
    i = pl.program_id(0)
    last = pl.num_programs(0) - 1
    slot = i % 2
    with_last = _uses_last_chunk(gdst_ref)

    @pl.when(i < n_first)
    def _():
        h_ref[...] = ha_ref[...]

    @pl.when(i >= n_first)
    def _():
        h_ref[...] = hb_ref[...]

    lr = lr_ref[...]
    copies = _group_copies(loc_ref, slot, xs_ref, gdst_ref, sem, True)

    def sort_chunk(c):
        loc_ref[slot, c * MASK_ROWS:(c + 1) * MASK_ROWS, :] = _dot(_sort_matrix(lr, c), h_ref[...]).astype(BF16)
        for cp in copies[c * CHUNK_GROUPS:(c + 1) * CHUNK_GROUPS]:
            cp.start()

    for c in range(N_CHUNKS - 1):
        sort_chunk(c)
    pl.when(with_last)(functools.partial(sort_chunk, N_CHUNKS - 1))

    @pl.when(i > 0)
    def _():
        _wait_groups(loc_ref, 1 - slot, sem, _uses_last_chunk(gdst_prev_ref))

    def pad_copy(e, j):
        row = pl.multiple_of((meta_ref[1, e] + j) * ROW_GROUP, ROW_GROUP)
        return pltpu.make_async_copy(zero_ref.at[pl.ds(0, ROW_GROUP)], xs_ref.at[pl.ds(row, ROW_GROUP)], zsem)

    def tile_copy(t):
        row = pl.multiple_of(t * ROW_TILE, ROW_TILE)
        return pltpu.make_async_copy(zero_ref, xs_ref.at[pl.ds(row, ROW_TILE)], zsem)

    def pads(fn):
        def body(e, carry):
            for j in range(TILE_GROUPS - 1):
                @pl.when(j < meta_ref[2, e])
                def _():
                    fn(pad_copy(e, j))
            return carry
        lax.fori_loop(0, N_EXPERTS, body, 0)

    def tiles(fn, lo, hi):
        lax.fori_loop(lo, hi, lambda t, carry: (fn(tile_copy(t)), carry)[1], 0)

    n_used = meta_ref[0, N_EXPERTS]

    @pl.when(i == 0)
    def _():
        zero_ref[...] = jnp.zeros_like(zero_ref)
        pads(lambda c: c.start(priority=1))
        tiles(lambda c: c.start(priority=1), n_used, first_dump_tile)

    @pl.when(i == last)
    def _():
        _wait_groups(loc_ref, slot, sem, with_last)
        tiles(lambda c: c.start(), first_dump_tile, n_tiles_all)
        pads(lambda c: c.wait())
        tiles(lambda c: c.wait(), n_used, n_tiles_all)


def _dispatch(gdst, meta, lr, h_a, h_b, n_tiles_all):
    tm = TOK_TILE
    na, nb2 = h_a.shape[0] // tm, h_b.shape[0] // tm
    return pl.pallas_call(
        functools.partial(_dispatch_kernel, na, n_tiles_all - 2 * LOCAL_GROUPS // TILE_GROUPS, n_tiles_all),
        out_shape=jax.ShapeDtypeStruct((n_tiles_all * ROW_TILE, ROW_W), BF16),
        grid=(na + nb2,),
        in_specs=[pl.BlockSpec((1, 1, GDST_LANES), lambda i: (i, 0, 0), memory_space=pltpu.SMEM),
                  pl.BlockSpec((1, 1, GDST_LANES), lambda i: (jnp.maximum(i - 1, 0), 0, 0), memory_space=pltpu.SMEM),
                  pl.BlockSpec((META_ROWS, LANES), lambda i: (0, 0), memory_space=pltpu.SMEM),
                  pl.BlockSpec((TOP_K, tm), lambda i: (0, i)),
                  pl.BlockSpec((tm, ROW_W), lambda i: (jnp.minimum(i, na - 1), 0)),
                  pl.BlockSpec((tm, ROW_W), lambda i: (jnp.maximum(i - na, 0), 0))],
        out_specs=pl.BlockSpec(memory_space=pl.ANY),
        scratch_shapes=[pltpu.VMEM((tm, ROW_W), BF16), pltpu.VMEM((2, LOCAL_ROWS, ROW_W), BF16),
                        pltpu.VMEM((ROW_TILE, ROW_W), BF16),
                        pltpu.SemaphoreType.DMA((2,)), pltpu.SemaphoreType.DMA],
        compiler_params=pltpu.CompilerParams(dimension_semantics=("arbitrary",), vmem_limit_bytes=VMEM_LIMIT),
        name="dispatch",
    )(gdst, gdst, meta, lr, h_a, h_b)


def _expert_kernel(n_tiles_all, ts_ref, xs_ref, wg_ref, wu_ref, wd_ref, bg_ref, bu_ref, bd_ref, y_ref,
                   wstage, wgb, wub, wdb, xbuf, ybuf, zbuf, sem_w, sem_in, sem_out, zsem):
    e = pl.program_id(0)
    t0 = ts_ref[e]
    nt = ts_ref[e + 1] - t0
    wslot = e % 2

    def w_copies(ex, slot):
        return [pltpu.make_async_copy(w_ref.at[ex], wstage.at[slot, j], sem_w.at[slot])
                for j, w_ref in enumerate((wg_ref, wu_ref, wd_ref))]

    def in_copy(t, slot):
        rows = pl.ds(pl.multiple_of((t0 + t) * ROW_TILE, ROW_TILE), ROW_TILE)
        return pltpu.make_async_copy(xs_ref.at[rows], xbuf.at[slot], sem_in.at[slot])

    def out_copy(tile, slot):
        rows = pl.ds(pl.multiple_of(tile * ROW_TILE, ROW_TILE), ROW_TILE)
        return pltpu.make_async_copy(ybuf.at[slot], y_ref.at[rows], sem_out.at[slot])

    @pl.when((e == 0) & (nt > 0))
    def _():
        for c in w_copies(0, 0):
            c.start()

    for j in range(TILE_SLOTS):
        @pl.when(nt > j)
        def _():
            in_copy(j, j).start()

    @pl.when(e + 1 < N_EXPERTS)
    def _():
        @pl.when(ts_ref[e + 2] > ts_ref[e + 1])
        def _():
            for c in w_copies(e + 1, 1 - wslot):
                c.start(priority=1)

    @pl.when(nt > 0)
    def _():
        for c in w_copies(e, wslot):
            c.wait()
        wgb[...] = wstage[wslot, 0].astype(BF16)
        wub[...] = wstage[wslot, 1].astype(BF16)
        wdb[...] = wstage[wslot, 2].astype(BF16)
        e_f = e.astype(F32)

        def acquire(t):
            slot = t % TILE_SLOTS
            in_copy(t, slot).wait()

            @pl.when(t >= TILE_SLOTS)
            def _():
                out_copy(t0 + t - TILE_SLOTS, slot).wait()

        def compute(t):
            slot = t % TILE_SLOTS
            xw = xbuf[slot]
            x = xw[:, :D_MODEL]
            info = xw[:, D_MODEL:].astype(F32)
            w_row = jnp.zeros((ROW_TILE, 1), F32)
            for k in range(TOP_K):
                wk = info[:, TOP_K + k:TOP_K + k + 1] + info[:, 2 * TOP_K + k:2 * TOP_K + k + 1]
                w_row = w_row + jnp.where(info[:, k:k + 1] == e_f, wk, 0.0)
            g = jnp.minimum(_dot(x, wgb[...]) + bg_ref[0], SWIGLU_LIMIT)
            u = jnp.clip(_dot(x, wub[...]) + bu_ref[0], -SWIGLU_LIMIT, SWIGLU_LIMIT)
            act = g * jax.nn.sigmoid(SWIGLU_ALPHA * g) * (u + 1.0)
            ybuf[slot] = ((_dot(act.astype(BF16), wdb[...]) + bd_ref[0]) * w_row).astype(BF16)

        def release(t):
            slot = t % TILE_SLOTS
            out_copy(t0 + t, slot).start()

            @pl.when(t + TILE_SLOTS < nt)
            def _():
                in_copy(t + TILE_SLOTS, slot).start()

        def pair(p, carry):
            ta, tb = 2 * p, 2 * p + 1
            acquire(ta)
            acquire(tb)
            compute(ta)
            compute(tb)
            release(ta)
            release(tb)
            return carry

        lax.fori_loop(0, nt // 2, pair, 0)

        @pl.when(nt % 2 == 1)
        def _():
            acquire(nt - 1)
            compute(nt - 1)
            release(nt - 1)

        for j in range(1, TILE_SLOTS + 1):
            @pl.when(nt >= j)
            def _():
                out_copy(t0 + nt - j, (nt - j) % TILE_SLOTS).wait()

    def zero_copy(tile):
        rows = pl.ds(pl.multiple_of(tile * ROW_TILE, ROW_TILE), ROW_TILE)
        return pltpu.make_async_copy(zbuf, y_ref.at[rows], zsem)

    def for_unused_tiles(fn):
        lax.fori_loop(ts_ref[N_EXPERTS], n_tiles_all, lambda tile, c: (fn(zero_copy(tile)), c)[1], 0)

    @pl.when(e == 0)
    def _():
        zbuf[...] = jnp.zeros_like(zbuf)
        for_unused_tiles(lambda c: c.start(priority=1))

    @pl.when(e == pl.num_programs(0) - 1)
    def _():
        for_unused_tiles(lambda c: c.wait())


def _experts(tile_start, xs, w_gate, b_gate, w_up, b_up, w_down, b_down):
    n_rows = xs.shape[0]
    b_spec = pl.BlockSpec((1, 1, D_MODEL), lambda e, ts: (e, 0, 0))
    any_spec = pl.BlockSpec(memory_space=pl.ANY)
    return pl.pallas_call(
        functools.partial(_expert_kernel, n_rows // ROW_TILE),
        out_shape=jax.ShapeDtypeStruct((n_rows, D_MODEL), BF16),
        grid_spec=pltpu.PrefetchScalarGridSpec(
            num_scalar_prefetch=1,
            grid=(N_EXPERTS,),
            in_specs=[any_spec, any_spec, any_spec, any_spec, b_spec, b_spec, b_spec],
            out_specs=any_spec,
            scratch_shapes=[pltpu.VMEM((2, 3, D_MODEL, D_MODEL), F32)]
                           + [pltpu.VMEM((D_MODEL, D_MODEL), BF16)] * 3
                           + [pltpu.VMEM((TILE_SLOTS, ROW_TILE, ROW_W), BF16),
                              pltpu.VMEM((TILE_SLOTS, ROW_TILE, D_MODEL), BF16),
                              pltpu.VMEM((ROW_TILE, D_MODEL), BF16),
                              pltpu.SemaphoreType.DMA((2,)), pltpu.SemaphoreType.DMA((TILE_SLOTS,)),
                              pltpu.SemaphoreType.DMA((TILE_SLOTS,)), pltpu.SemaphoreType.DMA],
        ),
        compiler_params=pltpu.CompilerParams(dimension_semantics=("arbitrary",), vmem_limit_bytes=VMEM_LIMIT),
        name="experts",
    )(tile_start, xs, w_gate, w_up, w_down, b_gate[:, None, :], b_up[:, None, :], b_down[:, None, :])


def _combine_kernel(gdst_ref, gdst_next_ref, lr_ref, y_ref, x1_ref, gt_ref, gf_ref, out_ref, loc_ref, sem):
    i = pl.program_id(0)
    slot = i % 2
    x1 = x1_ref[...]
    sb, l, _ = x1.shape

    @pl.when(i == 0)
    def _():
        _start_all(_group_copies(loc_ref, slot, y_ref, gdst_ref, sem, False))

    _wait_groups(loc_ref, slot, sem)

    prefetch = _group_copies(loc_ref, 1 - slot, y_ref, gdst_next_ref, sem, False)
    lr = lr_ref[...]
    moe = jnp.zeros((sb * l, D_MODEL), F32)
    for c in range(N_CHUNKS):
        rows = loc_ref[slot, c * MASK_ROWS:(c + 1) * MASK_ROWS, :]
        moe = moe + lax.dot_general(_sort_matrix(lr, c), rows, (((0,), (0,)), ((), ())),
                                    preferred_element_type=F32)
        _start_all(prefetch[c * CHUNK_GROUPS:(c + 1) * CHUNK_GROUPS])
    x2 = x1 + gt_ref[...] * moe.reshape(sb, l, D_MODEL)
    out_ref[...] = x2 * lax.rsqrt(jnp.mean(x2 * x2, axis=-1, keepdims=True) + EPS) * gf_ref[...]

    @pl.when(i == pl.num_programs(0) - 1)
    def _():
        _wait_groups(loc_ref, 1 - slot, sem)


def _combine(gdst, lr, y, x1, gt, g_final, sb, l, seq_div, blk_off):
    n3 = x1.shape[0]
    nblk = n3 // sb
    tm = sb * l
    assert tm == TOK_TILE
    return pl.pallas_call(
        _combine_kernel,
        out_shape=jax.ShapeDtypeStruct(x1.shape, F32),
        grid=(nblk,),
        in_specs=[pl.BlockSpec((1, 1, GDST_LANES), lambda i: (i + blk_off, 0, 0), memory_space=pltpu.SMEM),
                  pl.BlockSpec((1, 1, GDST_LANES), lambda i: (jnp.minimum(i + 1, nblk - 1) + blk_off, 0, 0),
                               memory_space=pltpu.SMEM),
                  pl.BlockSpec((TOP_K, tm), lambda i: (0, i + blk_off)),
                  pl.BlockSpec(memory_space=pl.ANY),
                  pl.BlockSpec((sb, l, D_MODEL), lambda i: (i, 0, 0)),
                  _ada_spec(gt, sb, seq_div),
                  pl.BlockSpec((1, D_MODEL), lambda i: (0, 0))],
        out_specs=pl.BlockSpec((sb, l, D_MODEL), lambda i: (i, 0, 0)),
        scratch_shapes=[pltpu.VMEM((2, LOCAL_ROWS, D_MODEL), BF16), pltpu.SemaphoreType.DMA((2,))],
        compiler_params=pltpu.CompilerParams(dimension_semantics=("arbitrary",), vmem_limit_bytes=VMEM_LIMIT),
        name="combine",
    )(gdst, gdst, lr, y, x1, gt[0], g_final)


def kernel(x_prompt, x_sample, c_prompt, c_sample, state_ssm, state_conv, w_ada, b_ada, g_mix, w_in, g_v_a, w_spatial, b_spatial, g_out_a, conv_w, conv_b, dt_bias, a_log, d_skip, g_out_b, w_out, g_ffn, w_router, b_router, w_gate, b_gate, w_up, b_up, w_down, b_down, g_final):
    assert w_ada.shape[0] == 1, "single-layer step"
    p = dict(w_in=w_in[0], g_mix=g_mix[0], g_v_a=g_v_a[0], w_spatial=w_spatial[0], b_spatial=b_spatial[0],
             g_out_a=g_out_a[0], conv_w=conv_w[0], conv_b=conv_b[0], dt_bias=dt_bias[0], a_log=a_log[0],
             d_skip=d_skip[0], g_out_b=g_out_b[0])
    bp, lp, _ = x_prompt.shape
    bs, ls, _ = x_sample.shape
    tp, ts = bp * lp, bs * ls

    ada = _ada(jnp.concatenate([c_sample, c_prompt], axis=0), w_ada[0], b_ada[0][None, :])
    ada_s = [(ada, j, 0) for j in range(6)]
    ada_p = [(ada, j, bs) for j in range(6)]

    fw = _front_weights(p)
    mixed_p, conv_p, ssm_p = _prompt_mixer(x_prompt, ada_p[0], ada_p[1], fw, p)
    mixed_s, v_s, conv_s, ssm_s = _sample_mixer(x_sample, ada_s[0], ada_s[1], state_ssm[0], state_conv[0], fw, p)

    w_out_b = w_out[0].astype(BF16)
    g_ffn2 = g_ffn[0][None, :]
    wr_t = w_router[0].T
    wr_hi = wr_t.astype(BF16)
    wr_both = jnp.concatenate([wr_hi, (wr_t - wr_hi.astype(F32)).astype(BF16)], axis=0)
    br = b_router[0][:, None]
    tps = lp // TOK_TILE
    sbs = TOK_TILE // ls
    xp3 = x_prompt.reshape(tp // POST_TILE, POST_TILE, D_MODEL)
    x1_p, h2p_p, ids_p = _post(mixed_p, xp3, ada_p[2], ada_p[4], ada_p[3], w_out_b, g_ffn2, wr_both, br,
                               1, POST_TILE, lp // POST_TILE)
    x1_p = x1_p.reshape(bp * tps, TOK_TILE, D_MODEL)
    x1_s, h2p_s, ids_s = _post(mixed_s, x_sample, ada_s[2], ada_s[4], ada_s[3], w_out_b, g_ffn2, wr_both, br,
                               POST_TILE // ls, ls, 1)

    n_blocks = (tp + ts) // TOK_TILE
    max_groups = (tp + ts) * TOP_K // ROW_GROUP + n_blocks * N_EXPERTS + N_EXPERTS * (TILE_GROUPS - 1)
    n_tiles = -(-max_groups // TILE_GROUPS)
    n_tiles_all = n_tiles + 2 * LOCAL_GROUPS // TILE_GROUPS
    lr, gdst, meta = _plan(jnp.concatenate([ids_p, ids_s], axis=1), n_tiles * TILE_GROUPS)

    xs = _dispatch(gdst, meta, lr, h2p_p, h2p_s, n_tiles_all)
    y = _experts(meta[0], xs, w_gate[0], b_gate[0], w_up[0], b_up[0], w_down[0], b_down[0])

    gf = g_final[None, :]
    y_p = _combine(gdst, lr, y, x1_p, ada_p[5], gf, 1, TOK_TILE, tps, 0).reshape(bp, lp, D_MODEL)
    y_s = _combine(gdst, lr, y, x1_s, ada_s[5], gf, sbs, ls, 1, tp // TOK_TILE)

    return (y_p, y_s, ssm_p[None], conv_p[None], ssm_s[None], conv_s[None], v_s.reshape(1, bs, ls, A_WIDTH))
```

```python
import functools
import math

import numpy as np
import jax
import jax.numpy as jnp
from jax import lax
from jax.experimental import pallas as pl
from jax.experimental.pallas import tpu as pltpu

F32 = jnp.float32
BF16 = jnp.bfloat16
I32 = jnp.int32

D_MODEL = 1024
A_WIDTH = 512
A_HEADS = 4
A_HEAD_DIM = 128
CHUNK = 128
B_WIDTH = 512
SSD_HEAD_DIM = 64
SSD_HEADS = 8
SSD_GROUPS = 2
SSD_STATE = 128
GROUP_W = B_WIDTH // SSD_GROUPS
CONV_K = 4
CONV_DIM = 1024
CONV_PAD = 8
N_EXPERTS = 32
TOP_K = 4
SWIGLU_LIMIT = 7.0
SWIGLU_ALPHA = 1.702
EPS = 1e-6
DECAY_MASKED = -1e30
LANES = 128

TOK_TILE = 512
SAMPLE_SEQ_TILE = 16
SEQ_UNROLL = 8
ROW_TILE = 256
ROW_GROUP = 16
TILE_GROUPS = ROW_TILE // ROW_GROUP
LOCAL_GROUPS = TOK_TILE * TOP_K // ROW_GROUP + N_EXPERTS
LOCAL_ROWS = LOCAL_GROUPS * ROW_GROUP
GDST_LANES = -(-LOCAL_GROUPS // LANES) * LANES
ROW_W = D_MODEL + LANES
INFO_ROWS = 16
MASK_ROWS = 256
META_ROWS = 8
POST_TILE = 1024
PROMPT_TILE = 512
PROMPT_SPLITS = 2
BIAS_ROWS = 8
TILE_SLOTS = 4
VMEM_LIMIT = 56 * 1024 * 1024


def _dot(a, b):
    return jnp.dot(a, b, preferred_element_type=F32)


def _dot_nt(a, b):
    return lax.dot_general(a, b, (((1,), (1,)), ((), ())), preferred_element_type=F32)


def _split(x):
    hi = x.astype(BF16)
    lo = (x - hi.astype(F32)).astype(BF16)
    return hi, lo


def _dot_exact_l(t, x):
    hi, lo = _split(x)
    return _dot(t, hi) + _dot(t, lo)


def _dot_exact_r(x, t):
    hi, lo = _split(x)
    return _dot(hi, t) + _dot(lo, t)


def _silu(x):
    return x * jax.nn.sigmoid(x)


def _gelu(x):
    return 0.5 * x * (1.0 + lax.erf(x * (1.0 / math.sqrt(2.0))))


def _softplus(x):
    return jnp.maximum(x, 0.0) + jnp.log1p(jnp.exp(-jnp.abs(x)))


def _rms(x, g):
    return x * lax.rsqrt(jnp.mean(x * x, axis=-1, keepdims=True) + EPS) * g


def _ada_kernel(c_ref, w_ref, b_ref, o_ref):
    s_hi, s_lo = _split(_silu(c_ref[...]))
    w_hi, w_lo = _split(w_ref[...])
    res = _dot(s_hi, w_hi) + _dot(s_lo, w_hi) + _dot(s_hi, w_lo) + b_ref[...]
    o_ref[...] = res.reshape(o_ref.shape)


def _ada(c_all, w_ada, b_ada):
    m = c_all.shape[0]
    n = w_ada.shape[1]
    bn = 1024
    return pl.pallas_call(
        _ada_kernel,
        out_shape=jax.ShapeDtypeStruct((m, 1, n), F32),
        grid=(n // bn,),
        in_specs=[pl.BlockSpec((m, D_MODEL), lambda j: (0, 0)),
                  pl.BlockSpec((D_MODEL, bn), lambda j: (0, j)),
                  pl.BlockSpec((1, bn), lambda j: (0, j))],
        out_specs=pl.BlockSpec((m, 1, bn), lambda j: (0, 0, j)),
        compiler_params=pltpu.CompilerParams(dimension_semantics=("arbitrary",), vmem_limit_bytes=VMEM_LIMIT),
        name="ada",
    )(c_all, w_ada, b_ada)


def _mixer_front(x3, sh, sc, prev, refs, xp_ref):
    (g_mix, w_uvz, w_xbc, w_dt, wbd, bias_sp, g_v, g_oa, conv_w, conv_b, dt_bias, a_row) = refs
    sb, l, _ = x3.shape
    tm = sb * l
    xn = x3 * lax.rsqrt(jnp.mean(x3 * x3, axis=-1, keepdims=True) + EPS) * g_mix[...]
    h = (xn * (1.0 + sc) + sh).reshape(tm, D_MODEL)
    hb = h.astype(BF16)
    uvz = _dot(hb, w_uvz[...])
    xbc = _dot(hb, w_xbc[...])
    dt_raw = _dot(hb, w_dt[...])

    u = _gelu(uvz[:, :A_WIDTH])
    vg = _gelu(uvz[:, A_WIDTH:2 * A_WIDTH])
    z = uvz[:, 2 * A_WIDTH:]
    v_parts, s_parts = [], []
    for hd in range(A_HEADS):
        sl = slice(hd * A_HEAD_DIM, (hd + 1) * A_HEAD_DIM)
        vh = _rms(vg[:, sl], g_v[:, sl])
        v_parts.append(vh)
        vb = vh.astype(BF16)
        s_parts.append(jnp.concatenate([_dot(wbd[hd], vb[r0:r0 + CHUNK]) + bias_sp[:, sl]
                                        for r0 in range(0, tm, CHUNK)], axis=0))
    v = jnp.concatenate(v_parts, axis=1)
    s_a = jnp.concatenate(s_parts, axis=1)
    out_a = _rms(u * s_a, g_oa[...])

    xp_ref[:, 0:CONV_PAD, :] = prev
    xp_ref[:, CONV_PAD:, :] = xbc.reshape(sb, l, CONV_DIM)
    xp = xp_ref[...]
    acc = conv_b[...] + xp[:, CONV_PAD:, :] * conv_w[CONV_K - 1:CONV_K, :]
    for s in range(1, CONV_K):
        back = pltpu.roll(xp, s, axis=1)[:, CONV_PAD:, :]
        acc = acc + back * conv_w[CONV_K - 1 - s:CONV_K - s, :]
    xc = _silu(acc).reshape(tm, CONV_DIM)
    dt = _softplus(dt_raw + dt_bias[...])
    d_a = dt * a_row[...]
    return out_a, v, z, xc, dt, d_a


def _ssd_chunk(xs, bm, cm, dt, d_a, cref):
    tril, ones, expand, neg_mask = cref
    cs = _dot_exact_l(tril[...], d_a)
    cs_t = cs.T
    dt_t = dt.T
    cs_tot = _dot_exact_l(ones[...], d_a)
    vals = jnp.concatenate([dt * jnp.exp(cs_tot - cs), jnp.exp(cs)], axis=0)
    vals_e = _dot_exact_r(vals, expand[...])
    n = xs.shape[0]
    w_e, e_e = vals_e[:n], vals_e[n:]
    xdtd = xs * w_e
    neg = neg_mask[...]
    row_lt_half = lax.broadcasted_iota(I32, (2 * n, LANES), 0) < n
    lane_lt_half = lax.broadcasted_iota(I32, (2 * n, LANES), 1) < SSD_HEAD_DIM
    y_parts = []
    for g in range(SSD_GROUPS):
        cb = _dot_nt(cm[:, g * SSD_STATE:(g + 1) * SSD_STATE].astype(BF16),
                     bm[:, g * SSD_STATE:(g + 1) * SSD_STATE].astype(BF16))
        for hp in range(SSD_HEADS // SSD_GROUPS // 2):
            h0 = g * (SSD_HEADS // SSD_GROUPS) + 2 * hp
            ms = []
            for hh in (h0, h0 + 1):
                diff = cs[:, hh:hh + 1] - cs_t[hh:hh + 1, :]
                ms.append((cb * jnp.exp(diff + neg) * dt_t[hh:hh + 1, :]).astype(BF16))
            pair = xs[:, h0 * SSD_HEAD_DIM:(h0 + 2) * SSD_HEAD_DIM]
            rhs = jnp.where(row_lt_half == lane_lt_half, jnp.concatenate([pair, pair], axis=0), 0.0).astype(BF16)
            y_parts.append(_dot(jnp.concatenate(ms, axis=1), rhs))
    y_diag = jnp.concatenate(y_parts, axis=1)
    return y_diag, e_e, xdtd, cs_tot


def _mixer_back(y, xs, z, out_a, dskip_e, g_ob):
    y = y + xs * dskip_e
    gated = y * _silu(z)
    parts = [_rms(gated[:, g * GROUP_W:(g + 1) * GROUP_W], g_ob[:, g * GROUP_W:(g + 1) * GROUP_W])
             for g in range(SSD_GROUPS)]
    return jnp.concatenate([out_a] + parts, axis=1).astype(BF16)


N_FRONT = 12
N_SSD = 4


def _prompt_mixer_kernel(tiles_per_seq, x_ref, sh_ref, sc_ref, *rest):
    front = rest[:N_FRONT]
    cref = rest[N_FRONT:N_FRONT + N_SSD]
    dskip_e, g_ob = rest[N_FRONT + N_SSD:N_FRONT + N_SSD + 2]
    mixed_ref, conv_out_ref, ssm_out_ref = rest[N_FRONT + N_SSD + 2:N_FRONT + N_SSD + 5]
    xp_ref, carry_ref, st_ref = rest[N_FRONT + N_SSD + 5:]
    i = pl.program_id(0)
    first = (i % tiles_per_seq) == 0

    @pl.when(first)
    def _():
        carry_ref[...] = jnp.zeros_like(carry_ref)
        st_ref[...] = jnp.zeros_like(st_ref)

    l = x_ref.shape[1]
    hl = l // PROMPT_SPLITS
    parts = []
    prev = carry_ref[...]
    for hh in range(PROMPT_SPLITS):
        xp_h = xp_ref.at[hh]
        parts.append(_mixer_front(x_ref[:, hh * hl:(hh + 1) * hl, :], sh_ref[...], sc_ref[...], prev, front, xp_h))
        prev = xp_h[:, hl:hl + CONV_PAD, :]
    carry_ref[...] = prev
    out_a, _, z, xc, dt, d_a = [jnp.concatenate([p[j] for p in parts], axis=0) for j in range(6)]
    xs = xc[:, :B_WIDTH]
    y_rows = []
    for c in range(l // CHUNK):
        r = slice(c * CHUNK, (c + 1) * CHUNK)
        bm = xc[r, B_WIDTH:B_WIDTH + SSD_GROUPS * SSD_STATE]
        cm = xc[r, B_WIDTH + SSD_GROUPS * SSD_STATE:]
        y_diag, e_e, xdtd, _ = _ssd_chunk(xs[r], bm, cm, dt[r], d_a[r], cref)
        st = st_ref[...]
        y_off, upd = [], []
        for g in range(SSD_GROUPS):
            gs = slice(g * GROUP_W, (g + 1) * GROUP_W)
            ns = slice(g * SSD_STATE, (g + 1) * SSD_STATE)
            y_off.append(_dot(cm[:, ns].astype(BF16), st[:, gs].astype(BF16)))
            upd.append(_dot(bm[:, ns].T.astype(BF16), xdtd[:, gs].astype(BF16)))
        y_rows.append(y_diag + jnp.concatenate(y_off, axis=1) * e_e)
        st_ref[...] = st * e_e[CHUNK - 1:CHUNK, :] + jnp.concatenate(upd, axis=1)
    y = jnp.concatenate(y_rows, axis=0)
    mixed_ref[...] = _mixer_back(y, xs, z, out_a, dskip_e[...], g_ob[...])

    @pl.when((i % tiles_per_seq) == tiles_per_seq - 1)
    def _():
        conv_out_ref[...] = xp_ref[PROMPT_SPLITS - 1, :, hl + CONV_PAD - (CONV_K - 1):hl + CONV_PAD, :]
        ssm_out_ref[0] = st_ref[...].T


def _sample_mixer_kernel(x_ref, sh_ref, sc_ref, prev_ref, ssm0_ref, *rest):
    front = rest[:N_FRONT]
    cref = rest[N_FRONT:N_FRONT + N_SSD]
    dskip_e, g_ob, selseq = rest[N_FRONT + N_SSD:N_FRONT + N_SSD + 3]
    mixed_ref, v_ref, conv_out_ref, ssm_out_ref = rest[N_FRONT + N_SSD + 3:N_FRONT + N_SSD + 7]
    xp_ref, yoff_ref, cbf_ref, bbf_ref, t1_ref, dtab_ref = rest[N_FRONT + N_SSD + 7:]
    x3 = x_ref[...]
    sb, l, _ = x3.shape
    tm = sb * l
    out_a, v, z, xc, dt, d_a = _mixer_front(x3, sh_ref[...], sc_ref[...], prev_ref[...], front, xp_ref)
    v_ref[...] = v
    conv_out_ref[...] = xp_ref[:, l + CONV_PAD - (CONV_K - 1):l + CONV_PAD, :]
    xs = xc[:, :B_WIDTH]
    bm = xc[:, B_WIDTH:B_WIDTH + SSD_GROUPS * SSD_STATE]
    cm = xc[:, B_WIDTH + SSD_GROUPS * SSD_STATE:]
    y_diag, e_e, xdtd, _ = _ssd_chunk(xs, bm, cm, dt, d_a, cref)

    e_tot = jnp.exp(_dot_exact_l(selseq[...], d_a))
    for hh in range(SSD_HEADS):
        dtab_ref[hh] = jnp.broadcast_to(e_tot[:, hh:hh + 1], (sb, LANES))
    cbf_ref[...] = cm
    bbf_ref[...] = bm
    for g in range(SSD_GROUPS):
        t1_ref[g] = xdtd[:, g * GROUP_W:(g + 1) * GROUP_W].T.astype(BF16)
    seq_of_row = lax.broadcasted_iota(I32, (tm, SSD_STATE), 0) // l
    heads_per_group = SSD_HEADS // SSD_GROUPS

    def one_seq(j):
        r0 = pl.multiple_of(j * l, l)
        s0 = ssm0_ref[j]
        for g in range(SSD_GROUPS):
            ns = slice(g * SSD_STATE, (g + 1) * SSD_STATE)
            s0g = s0[g * heads_per_group:(g + 1) * heads_per_group].reshape(GROUP_W, SSD_STATE)
            cj = cbf_ref[pl.ds(r0, l), ns].astype(BF16)
            yoff_ref[pl.ds(r0, l), g * GROUP_W:(g + 1) * GROUP_W] = _dot_nt(cj, s0g.astype(BF16))
            bmask = jnp.where(seq_of_row == j, bbf_ref[:, ns], 0.0).astype(BF16)
            upd = _dot(t1_ref[g], bmask)
            for hq in range(heads_per_group):
                hh = g * heads_per_group + hq
                dec = dtab_ref[hh, pl.ds(j, 1), :]
                ssm_out_ref[j, hh] = s0[hh] * dec + upd[hq * SSD_HEAD_DIM:(hq + 1) * SSD_HEAD_DIM]

    def body(jj, carry):
        for u in range(SEQ_UNROLL):
            one_seq(jj * SEQ_UNROLL + u)
        return carry

    lax.fori_loop(0, sb // SEQ_UNROLL, body, 0)
    y = y_diag + yoff_ref[...] * e_e
    mixed_ref[...] = _mixer_back(y, xs, z, out_a, dskip_e[...], g_ob[...])


def _ada_spec(ref, sb, div):
    _, term, row0 = ref
    assert row0 % sb == 0
    return pl.BlockSpec((sb, 1, D_MODEL), lambda i: (row0 // sb + i // div, 0, term))


def _const_spec(a):
    nd = a.ndim
    return pl.BlockSpec(a.shape, lambda i, _nd=nd: (0,) * _nd)


def _spatial_consts(w_spatial, b_spatial, cl, tm):
    w = jnp.where(jnp.tril(jnp.ones((cl, cl), bool)), w_spatial[:, :cl, :cl], 0.0)
    eye = jnp.eye(tm // cl, dtype=F32)
    wbd = jnp.einsum("ab,hts->hatbs", eye, w).reshape(A_HEADS, tm, tm).astype(BF16)
    bias = jnp.tile(jnp.repeat(b_spatial[:, :cl].T, A_HEAD_DIM, axis=1), (tm // cl, 1))
    return wbd, bias


def _ssd_consts(cl):
    r = np.arange(CHUNK)
    same = (r[:, None] // cl) == (r[None, :] // cl)
    tril = same & (r[:, None] >= r[None, :])
    expand = np.zeros((LANES, B_WIDTH), np.float32)
    for hh in range(SSD_HEADS):
        expand[hh, hh * SSD_HEAD_DIM:(hh + 1) * SSD_HEAD_DIM] = 1.0
    return (jnp.asarray(tril, BF16), jnp.asarray(same, BF16), jnp.asarray(expand, BF16),
            jnp.asarray(np.where(tril, 0.0, DECAY_MASKED), F32))


def _front_weights(p):
    w_in = p["w_in"]
    c0, c1 = 3 * A_WIDTH, 3 * A_WIDTH + CONV_DIM
    w_dt = w_in[:, c1:]
    pad8 = lambda v: jnp.pad(v, (0, LANES - SSD_HEADS))
    a = -jnp.exp(p["a_log"])
    return dict(
        g_mix=p["g_mix"][None, :],
        w_uvz=w_in[:, :c0].astype(BF16),
        w_xbc=w_in[:, c0:c1].astype(BF16),
        w_dt=jnp.pad(w_dt, ((0, 0), (0, LANES - SSD_HEADS))).astype(BF16),
        g_v=p["g_v_a"][None, :], g_oa=p["g_out_a"][None, :],
        conv_w=p["conv_w"], conv_b=p["conv_b"][None, :],
        dt_bias=pad8(p["dt_bias"])[None, :],
        a_row=pad8(a)[None, :],
        dskip_e=jnp.repeat(p["d_skip"], SSD_HEAD_DIM)[None, :],
        g_ob=p["g_out_b"][None, :],
    )


def _front_list(fw, wbd, bias_sp):
    return [fw["g_mix"], fw["w_uvz"], fw["w_xbc"], fw["w_dt"], wbd, bias_sp, fw["g_v"], fw["g_oa"],
            fw["conv_w"], fw["conv_b"], fw["dt_bias"], fw["a_row"]]


def _prompt_mixer(x, sh, sc, fw, p):
    nseq, lseq, _ = x.shape
    tile = PROMPT_TILE
    tps = lseq // tile
    nt = nseq * tps
    x4 = x.reshape(nt, tile, D_MODEL)
    wbd, bias_sp = _spatial_consts(p["w_spatial"], p["b_spatial"], CHUNK, CHUNK)
    consts = _front_list(fw, wbd, bias_sp) + list(_ssd_consts(CHUNK)) + [fw["dskip_e"], fw["g_ob"]]
    mixed, conv_new, ssm_new = pl.pallas_call(
        functools.partial(_prompt_mixer_kernel, tps),
        out_shape=(jax.ShapeDtypeStruct((nt * tile, D_MODEL), BF16),
                   jax.ShapeDtypeStruct((nseq, CONV_K - 1, CONV_DIM), F32),
                   jax.ShapeDtypeStruct((nseq, B_WIDTH, SSD_STATE), F32)),
        grid=(nt,),
        in_specs=[pl.BlockSpec((1, tile, D_MODEL), lambda i: (i, 0, 0)), _ada_spec(sh, 1, tps), _ada_spec(sc, 1, tps)]
                 + [_const_spec(a) for a in consts],
        out_specs=(pl.BlockSpec((tile, D_MODEL), lambda i: (i, 0)),
                   pl.BlockSpec((1, CONV_K - 1, CONV_DIM), lambda i: (i // tps, 0, 0)),
                   pl.BlockSpec((1, B_WIDTH, SSD_STATE), lambda i: (i // tps, 0, 0))),
        scratch_shapes=[pltpu.VMEM((PROMPT_SPLITS, 1, tile // PROMPT_SPLITS + CONV_PAD, CONV_DIM), F32),
                        pltpu.VMEM((1, CONV_PAD, CONV_DIM), F32),
                        pltpu.VMEM((SSD_STATE, B_WIDTH), F32)],
        compiler_params=pltpu.CompilerParams(dimension_semantics=("arbitrary",), vmem_limit_bytes=VMEM_LIMIT),
        name="prompt_mixer",
    )(x4, sh[0], sc[0], *consts)
    return mixed, conv_new, ssm_new.reshape(nseq, SSD_HEADS, SSD_HEAD_DIM, SSD_STATE)


def _sample_mixer(x, sh, sc, state_ssm, state_conv, fw, p):
    nseq, l, _ = x.shape
    sb = SAMPLE_SEQ_TILE
    tm = sb * l
    assert tm == CHUNK
    wbd, bias_sp = _spatial_consts(p["w_spatial"], p["b_spatial"], l, tm)
    selseq = jnp.asarray((np.arange(tm)[None, :] // l) == np.arange(sb)[:, None], BF16)
    consts = _front_list(fw, wbd, bias_sp) + list(_ssd_consts(l)) + [fw["dskip_e"], fw["g_ob"], selseq]
    prev = jnp.pad(state_conv, ((0, 0), (CONV_PAD - (CONV_K - 1), 0), (0, 0)))
    ssm_spec = pl.BlockSpec((sb, SSD_HEADS, SSD_HEAD_DIM, SSD_STATE), lambda i: (i, 0, 0, 0))
    return pl.pallas_call(
        _sample_mixer_kernel,
        out_shape=(jax.ShapeDtypeStruct((nseq * l, D_MODEL), BF16),
                   jax.ShapeDtypeStruct((nseq * l, A_WIDTH), F32),
                   jax.ShapeDtypeStruct((nseq, CONV_K - 1, CONV_DIM), F32),
                   jax.ShapeDtypeStruct(state_ssm.shape, F32)),
        grid=(nseq // sb,),
        in_specs=[pl.BlockSpec((sb, l, D_MODEL), lambda i: (i, 0, 0)), _ada_spec(sh, sb, 1), _ada_spec(sc, sb, 1),
                  pl.BlockSpec((sb, CONV_PAD, CONV_DIM), lambda i: (i, 0, 0)), ssm_spec]
                 + [_const_spec(a) for a in consts],
        out_specs=(pl.BlockSpec((tm, D_MODEL), lambda i: (i, 0)),
                   pl.BlockSpec((tm, A_WIDTH), lambda i: (i, 0)),
                   pl.BlockSpec((sb, CONV_K - 1, CONV_DIM), lambda i: (i, 0, 0)),
                   ssm_spec),
        scratch_shapes=[pltpu.VMEM((sb, l + CONV_PAD, CONV_DIM), F32),
                        pltpu.VMEM((tm, B_WIDTH), F32),
                        pltpu.VMEM((tm, SSD_GROUPS * SSD_STATE), F32),
                        pltpu.VMEM((tm, SSD_GROUPS * SSD_STATE), F32),
                        pltpu.VMEM((SSD_GROUPS, GROUP_W, tm), BF16),
                        pltpu.VMEM((SSD_HEADS, sb, LANES), F32)],
        compiler_params=pltpu.CompilerParams(dimension_semantics=("arbitrary",), vmem_limit_bytes=VMEM_LIMIT),
        name="sample_mixer",
    )(x, sh[0], sc[0], prev, state_ssm, *consts)


def _post_kernel(mixed_ref, x_ref, gt_ref, sc_ref, sh_ref, w_out_ref, g_ffn_ref, wr_both_ref, br_ref,
                 x1_ref, h2p_ref, ids_ref):
    x3 = x_ref[...]
    sb, l, _ = x3.shape
    tm = sb * l
    mix = _dot(mixed_ref[...], w_out_ref[...]).reshape(sb, l, D_MODEL)
    x1 = x3 + gt_ref[...] * mix
    x1_ref[...] = x1
    xn = x1 * lax.rsqrt(jnp.mean(x1 * x1, axis=-1, keepdims=True) + EPS) * g_ffn_ref[...]
    h2 = (xn * (1.0 + sc_ref[...]) + sh_ref[...]).reshape(tm, D_MODEL)
    h_hi, h_lo = _split(h2)
    h2p_ref[:, :D_MODEL] = h_hi
    both = _dot_nt(wr_both_ref[...], h_hi)
    logits = (both[:N_EXPERTS] + both[N_EXPERTS:] + _dot_nt(wr_both_ref[:N_EXPERTS, :], h_lo)
              + br_ref[...])
    e_iota = lax.broadcasted_iota(I32, logits.shape, 0)
    vals, idxs = [], []
    for _ in range(TOP_K):
        m = jnp.max(logits, axis=0, keepdims=True)
        idx = jnp.min(jnp.where(logits == m, e_iota, N_EXPERTS), axis=0, keepdims=True)
        vals.append(m)
        idxs.append(idx)
        logits = jnp.where(e_iota == idx, -jnp.inf, logits)
    ex = [jnp.exp(v - vals[0]) for v in vals]
    tot = ex[0] + ex[1] + ex[2] + ex[3]
    ids = jnp.concatenate(idxs, axis=0)
    ids_ref[...] = ids
    wts = jnp.concatenate([e / tot for e in ex], axis=0)
    w_hi = wts.astype(BF16).astype(F32)
    info = jnp.concatenate([ids.astype(F32), w_hi, wts - w_hi, jnp.zeros((TOP_K, tm), F32)], axis=0).astype(BF16)
    r = lax.broadcasted_iota(I32, (INFO_ROWS, LANES), 0)
    c = lax.broadcasted_iota(I32, (INFO_ROWS, LANES), 1)
    place = jnp.where(r == c, 1.0, 0.0).astype(BF16)
    h2p_ref[:, D_MODEL:] = lax.dot_general(info, place, (((0,), (0,)), ((), ())),
                                           preferred_element_type=F32).astype(BF16)


def _post(mixed, x, gt, sc, sh, w_out_b, g_ffn, wr_both, br, sb, l, seq_div):
    n3, _, _ = x.shape
    nblk = n3 // sb
    tm = sb * l
    t = n3 * l
    consts = [w_out_b, g_ffn, wr_both, br]
    return pl.pallas_call(
        _post_kernel,
        out_shape=(jax.ShapeDtypeStruct(x.shape, F32),
                   jax.ShapeDtypeStruct((t, ROW_W), BF16),
                   jax.ShapeDtypeStruct((TOP_K, t), I32)),
        grid=(nblk,),
        in_specs=[pl.BlockSpec((tm, D_MODEL), lambda i: (i, 0)),
                  pl.BlockSpec((sb, l, D_MODEL), lambda i: (i, 0, 0)),
                  _ada_spec(gt, sb, seq_div), _ada_spec(sc, sb, seq_div), _ada_spec(sh, sb, seq_div)]
                 + [_const_spec(a) for a in consts],
        out_specs=(pl.BlockSpec((sb, l, D_MODEL), lambda i: (i, 0, 0)),
                   pl.BlockSpec((tm, ROW_W), lambda i: (i, 0)),
                   pl.BlockSpec((TOP_K, tm), lambda i: (0, i))),
        compiler_params=pltpu.CompilerParams(dimension_semantics=("arbitrary",), vmem_limit_bytes=VMEM_LIMIT),
        name="post",
    )(mixed, x, gt[0], sc[0], sh[0], *consts)


def _strict_upper(n):
    r = lax.broadcasted_iota(I32, (n, n), 0)
    c = lax.broadcasted_iota(I32, (n, n), 1)
    return jnp.where(r < c, 1.0, 0.0).astype(BF16)


def _expert_prefix(col):
    r = lax.broadcasted_iota(I32, (N_EXPERTS, N_EXPERTS), 0)
    c = lax.broadcasted_iota(I32, (N_EXPERTS, N_EXPERTS), 1)
    as_row = jnp.sum(jnp.where(r == c, col, 0.0), axis=0, keepdims=True)
    return jnp.sum(jnp.where(c < r, as_row, 0.0), axis=1, keepdims=True)


def _plan_kernel(dump_group, ids_ref, lr_ref, gdst_ref, tile_e_ref):
    tm = TOK_TILE
    nb = ids_ref.shape[1] // tm
    blk_lane = lax.broadcasted_iota(I32, (N_EXPERTS, LANES), 1)

    def block_masks(b):
        ids = ids_ref[:, pl.ds(pl.multiple_of(b * tm, tm), tm)]
        e_iota = lax.broadcasted_iota(I32, (N_EXPERTS, tm), 0)
        onehot = [ids[k:k + 1, :] == e_iota for k in range(TOP_K)]
        sel = (onehot[0] | onehot[1]) | (onehot[2] | onehot[3])
        return onehot, jnp.where(sel, 1.0, 0.0)

    def count(b, seg):
        _, m = block_masks(b)
        seg_b = jnp.ceil(jnp.sum(m, axis=1, keepdims=True) * (1.0 / ROW_GROUP))
        return jnp.where(blk_lane == b, seg_b, seg)

    seg = lax.fori_loop(0, nb, count, jnp.zeros((N_EXPERTS, LANES), F32))
    tot = jnp.sum(seg, axis=1, keepdims=True)
    padded = jnp.ceil(tot * (1.0 / TILE_GROUPS)) * TILE_GROUPS
    gstart = _expert_prefix(padded)
    gb = gstart + _dot(seg.astype(BF16), _strict_upper(LANES))
    r = lax.broadcasted_iota(I32, (N_EXPERTS, LANES), 0)
    as_row = lambda col: jnp.sum(jnp.where(r == blk_lane, col, 0.0), axis=0, keepdims=True)
    n_used = jnp.sum(padded, axis=0, keepdims=True)
    lane = lax.broadcasted_iota(I32, (1, LANES), 1)
    tiles = jnp.where(lane == N_EXPERTS, n_used, as_row(gstart)) * (1.0 / TILE_GROUPS)
    meta = jnp.concatenate([tiles, as_row(gstart + tot), as_row(padded - tot),
                            jnp.zeros((META_ROWS - 3, LANES), F32)], axis=0)
    tile_e_ref[...] = meta.astype(I32)
    upper = _strict_upper(tm)

    def place(b, carry):
        onehot, m = block_masks(b)
        seg_b = jnp.sum(jnp.where(blk_lane == b, seg, 0.0), axis=1, keepdims=True)
        gb_b = jnp.sum(jnp.where(blk_lane == b, gb, 0.0), axis=1, keepdims=True)
        loc_b = _expert_prefix(seg_b)
        before = _dot(m.astype(BF16), upper) + loc_b * ROW_GROUP
        lr_ref[:, pl.ds(pl.multiple_of(b * tm, tm), tm)] = jnp.concatenate(
            [jnp.sum(jnp.where(onehot[k], before, 0.0), axis=0, keepdims=True) for k in range(TOP_K)],
            axis=0).astype(I32)
        g = lax.broadcasted_iota(I32, (N_EXPERTS, GDST_LANES), 1).astype(F32)
        inside = (loc_b <= g) & (g < loc_b + seg_b)
        dst = jnp.sum(jnp.where(inside, gb_b + g - loc_b, 0.0), axis=0, keepdims=True)
        used = jnp.sum(jnp.where(inside, 1.0, 0.0), axis=0, keepdims=True) > 0.5
        dump = dump_group + lax.convert_element_type(b % 2, F32) * LOCAL_GROUPS + g[0:1, :]
        n_local = jnp.sum(seg_b, axis=0, keepdims=True)
        gdst_ref[b] = jnp.where(g[0:1, :] == GDST_LANES - 1, n_local, jnp.where(used, dst, dump)).astype(I32)
        return carry

    lax.fori_loop(0, nb, place, 0)


def _plan(ids, dump_group):
    t = ids.shape[1]
    nb = t // TOK_TILE
    assert nb <= LANES
    return pl.pallas_call(
        functools.partial(_plan_kernel, float(dump_group)),
        out_shape=(jax.ShapeDtypeStruct((TOP_K, t), I32),
                   jax.ShapeDtypeStruct((nb, 1, GDST_LANES), I32),
                   jax.ShapeDtypeStruct((META_ROWS, LANES), I32)),
        name="plan",
    )(ids)


def _sort_matrix(lr, c):
    r_iota = lax.broadcasted_iota(I32, (MASK_ROWS, lr.shape[1]), 0) + c * MASK_ROWS
    p = jnp.where(r_iota == lr[TOP_K - 1:TOP_K, :], 1.0, 0.0)
    for k in range(TOP_K - 1):
        p = jnp.where(r_iota == lr[k:k + 1, :], 1.0, p)
    return p.astype(BF16)


def _group_copies(loc_ref, slot, far_ref, gdst_ref, sem, to_far):
    copies = []
    for g in range(LOCAL_GROUPS):
        dst = pl.multiple_of(gdst_ref[0, 0, g] * ROW_GROUP, ROW_GROUP)
        near = loc_ref.at[slot, pl.ds(g * ROW_GROUP, ROW_GROUP)]
        far = far_ref.at[pl.ds(dst, ROW_GROUP)]
        copies.append(pltpu.make_async_copy(near, far, sem.at[slot]) if to_far
                      else pltpu.make_async_copy(far, near, sem.at[slot]))
    return copies


N_CHUNKS = LOCAL_ROWS // MASK_ROWS
CHUNK_GROUPS = LOCAL_GROUPS // N_CHUNKS
SURE_GROUPS = LOCAL_GROUPS - CHUNK_GROUPS


def _start_all(copies):
    for c in copies:
        c.start()


def _uses_last_chunk(gdst_ref):
    return gdst_ref[0, 0, GDST_LANES - 1] > SURE_GROUPS


def _wait_groups(loc_ref, slot, sem, with_last=None):
    def wait(n_groups):
        part = loc_ref.at[slot, pl.ds(0, n_groups * ROW_GROUP)]
        pltpu.make_async_copy(part, part, sem.at[slot]).wait()

    if with_last is None:
        wait(LOCAL_GROUPS)
        return
    pl.when(with_last)(functools.partial(wait, LOCAL_GROUPS))
    pl.when(jnp.logical_not(with_last))(functools.partial(wait, SURE_GROUPS))


def _dispatch_kernel(n_first, first_dump_tile, n_tiles_all, gdst_ref, gdst_prev_ref, meta_ref, lr_ref, ha_ref, hb_ref,
                     xs_ref, h_ref, loc_ref, zero_ref, sem, zsem):
    i = pl.program_id(0)
    last = pl.num_programs(0) - 1
    slot = i % 2
    with_last = _uses_last_chunk(gdst_ref)

    @pl.when(i < n_first)
    def _():
        h_ref[...] = ha_ref[...]

    @pl.when(i >= n_first)
    def _():
        h_ref[...] = hb_ref[...]

    lr = lr_ref[...]
    copies = _group_copies(loc_ref, slot, xs_ref, gdst_ref, sem, True)

    def sort_chunk(c):
        loc_ref[slot, c * MASK_ROWS:(c + 1) * MASK_ROWS, :] = _dot(_sort_matrix(lr, c), h_ref[...]).astype(BF16)
        for cp in copies[c * CHUNK_GROUPS:(c + 1) * CHUNK_GROUPS]:
            cp.start()

    for c in range(N_CHUNKS - 1):
        sort_chunk(c)
    pl.when(with_last)(functools.partial(sort_chunk, N_CHUNKS - 1))

    @pl.when(i > 0)
    def _():
        _wait_groups(loc_ref, 1 - slot, sem, _uses_last_chunk(gdst_prev_ref))

    def pad_copy(e, j):
        row = pl.multiple_of((meta_ref[1, e] + j) * ROW_GROUP, ROW_GROUP)
        return pltpu.make_async_copy(zero_ref.at[pl.ds(0, ROW_GROUP)], xs_ref.at[pl.ds(row, ROW_GROUP)], zsem)

    def tile_copy(t):
        row = pl.multiple_of(t * ROW_TILE, ROW_TILE)
        return pltpu.make_async_copy(zero_ref, xs_ref.at[pl.ds(row, ROW_TILE)], zsem)

    def pads(fn):
        def body(e, carry):
            for j in range(TILE_GROUPS - 1):
                @pl.when(j < meta_ref[2, e])
                def _():
                    fn(pad_copy(e, j))
            return carry
        lax.fori_loop(0, N_EXPERTS, body, 0)

    def tiles(fn, lo, hi):
        lax.fori_loop(lo, hi, lambda t, carry: (fn(tile_copy(t)), carry)[1], 0)

    n_used = meta_ref[0, N_EXPERTS]

    @pl.when(i == 0)
    def _():
        zero_ref[...] = jnp.zeros_like(zero_ref)
        pads(lambda c: c.start(priority=1))
        tiles(lambda c: c.start(priority=1), n_used, first_dump_tile)

    @pl.when(i == last)
    def _():
        _wait_groups(loc_ref, slot, sem, with_last)
        tiles(lambda c: c.start(), first_dump_tile, n_tiles_all)
        pads(lambda c: c.wait())
        tiles(lambda c: c.wait(), n_used, n_tiles_all)


def _dispatch(gdst, meta, lr, h_a, h_b, n_tiles_all):
    tm = TOK_TILE
    na, nb2 = h_a.shape[0] // tm, h_b.shape[0] // tm
    return pl.pallas_call(
        functools.partial(_dispatch_kernel, na, n_tiles_all - 2 * LOCAL_GROUPS // TILE_GROUPS, n_tiles_all),
        out_shape=jax.ShapeDtypeStruct((n_tiles_all * ROW_TILE, ROW_W), BF16),
        grid=(na + nb2,),
        in_specs=[pl.BlockSpec((1, 1, GDST_LANES), lambda i: (i, 0, 0), memory_space=pltpu.SMEM),
                  pl.BlockSpec((1, 1, GDST_LANES), lambda i: (jnp.maximum(i - 1, 0), 0, 0), memory_space=pltpu.SMEM),
                  pl.BlockSpec((META_ROWS, LANES), lambda i: (0, 0), memory_space=pltpu.SMEM),
                  pl.BlockSpec((TOP_K, tm), lambda i: (0, i)),
                  pl.BlockSpec((tm, ROW_W), lambda i: (jnp.minimum(i, na - 1), 0)),
                  pl.BlockSpec((tm, ROW_W), lambda i: (jnp.maximum(i - na, 0), 0))],
        out_specs=pl.BlockSpec(memory_space=pl.ANY),
        scratch_shapes=[pltpu.VMEM((tm, ROW_W), BF16), pltpu.VMEM((2, LOCAL_ROWS, ROW_W), BF16),
                        pltpu.VMEM((ROW_TILE, ROW_W), BF16),
                        pltpu.SemaphoreType.DMA((2,)), pltpu.SemaphoreType.DMA],
        compiler_params=pltpu.CompilerParams(dimension_semantics=("arbitrary",), vmem_limit_bytes=VMEM_LIMIT),
        name="dispatch",
    )(gdst, gdst, meta, lr, h_a, h_b)


def _expert_kernel(n_tiles_all, ts_ref, xs_ref, wg_ref, wu_ref, wd_ref, bg_ref, bu_ref, bd_ref, y_ref,
                   wstage, wgb, wub, wdb, xbuf, ybuf, zbuf, sem_w, sem_in, sem_out, zsem):
    e = pl.program_id(0)
    t0 = ts_ref[e]
    nt = ts_ref[e + 1] - t0
    wslot = e % 2

    def w_copies(ex, slot):
        return [pltpu.make_async_copy(w_ref.at[ex], wstage.at[slot, j], sem_w.at[slot])
                for j, w_ref in enumerate((wg_ref, wu_ref, wd_ref))]

    def in_copy(t, slot):
        rows = pl.ds(pl.multiple_of((t0 + t) * ROW_TILE, ROW_TILE), ROW_TILE)
        return pltpu.make_async_copy(xs_ref.at[rows], xbuf.at[slot], sem_in.at[slot])

    def out_copy(tile, slot):
        rows = pl.ds(pl.multiple_of(tile * ROW_TILE, ROW_TILE), ROW_TILE)
        return pltpu.make_async_copy(ybuf.at[slot], y_ref.at[rows], sem_out.at[slot])

    @pl.when((e == 0) & (nt > 0))
    def _():
        for c in w_copies(0, 0):
            c.start()

    for j in range(TILE_SLOTS):
        @pl.when(nt > j)
        def _():
            in_copy(j, j).start()

    @pl.when(e + 1 < N_EXPERTS)
    def _():
        @pl.when(ts_ref[e + 2] > ts_ref[e + 1])
        def _():
            for c in w_copies(e + 1, 1 - wslot):
                c.start(priority=1)

    @pl.when(nt > 0)
    def _():
        for c in w_copies(e, wslot):
            c.wait()
        wgb[...] = wstage[wslot, 0].astype(BF16)
        wub[...] = wstage[wslot, 1].astype(BF16)
        wdb[...] = wstage[wslot, 2].astype(BF16)
        e_f = e.astype(F32)

        def acquire(t):
            slot = t % TILE_SLOTS
            in_copy(t, slot).wait()

            @pl.when(t >= TILE_SLOTS)
            def _():
                out_copy(t0 + t - TILE_SLOTS, slot).wait()

        def compute(t):
            slot = t % TILE_SLOTS
            xw = xbuf[slot]
            x = xw[:, :D_MODEL]
            info = xw[:, D_MODEL:].astype(F32)
            w_row = jnp.zeros((ROW_TILE, 1), F32)
            for k in range(TOP_K):
                wk = info[:, TOP_K + k:TOP_K + k + 1] + info[:, 2 * TOP_K + k:2 * TOP_K + k + 1]
                w_row = w_row + jnp.where(info[:, k:k + 1] == e_f, wk, 0.0)
            b_row = pl.ds(e % BIAS_ROWS, 1)
            g = jnp.minimum(_dot(x, wgb[...]) + bg_ref[b_row, :], SWIGLU_LIMIT)
            u = jnp.clip(_dot(x, wub[...]) + bu_ref[b_row, :], -SWIGLU_LIMIT, SWIGLU_LIMIT)
            act = g * jax.nn.sigmoid(SWIGLU_ALPHA * g) * (u + 1.0)
            ybuf[slot] = ((_dot(act.astype(BF16), wdb[...]) + bd_ref[b_row, :]) * w_row).astype(BF16)

        def release(t):
            slot = t % TILE_SLOTS
            out_copy(t0 + t, slot).start()

            @pl.when(t + TILE_SLOTS < nt)
            def _():
                in_copy(t + TILE_SLOTS, slot).start()

        def pair(p, carry):
            ta, tb = 2 * p, 2 * p + 1
            acquire(ta)
            acquire(tb)
            compute(ta)
            compute(tb)
            release(ta)
            release(tb)
            return carry

        lax.fori_loop(0, nt // 2, pair, 0)

        @pl.when(nt % 2 == 1)
        def _():
            acquire(nt - 1)
            compute(nt - 1)
            release(nt - 1)

        for j in range(1, TILE_SLOTS + 1):
            @pl.when(nt >= j)
            def _():
                out_copy(t0 + nt - j, (nt - j) % TILE_SLOTS).wait()

    def zero_copy(tile):
        rows = pl.ds(pl.multiple_of(tile * ROW_TILE, ROW_TILE), ROW_TILE)
        return pltpu.make_async_copy(zbuf, y_ref.at[rows], zsem)

    def for_unused_tiles(fn):
        lax.fori_loop(ts_ref[N_EXPERTS], n_tiles_all, lambda tile, c: (fn(zero_copy(tile)), c)[1], 0)

    @pl.when(e == 0)
    def _():
        zbuf[...] = jnp.zeros_like(zbuf)
        for_unused_tiles(lambda c: c.start(priority=1))

    @pl.when(e == pl.num_programs(0) - 1)
    def _():
        for_unused_tiles(lambda c: c.wait())


def _experts(tile_start, xs, w_gate, b_gate, w_up, b_up, w_down, b_down):
    n_rows = xs.shape[0]
    b_spec = pl.BlockSpec((BIAS_ROWS, D_MODEL), lambda e, ts: (e // BIAS_ROWS, 0))
    any_spec = pl.BlockSpec(memory_space=pl.ANY)
    return pl.pallas_call(
        functools.partial(_expert_kernel, n_rows // ROW_TILE),
        out_shape=jax.ShapeDtypeStruct((n_rows, D_MODEL), BF16),
        grid_spec=pltpu.PrefetchScalarGridSpec(
            num_scalar_prefetch=1,
            grid=(N_EXPERTS,),
            in_specs=[any_spec, any_spec, any_spec, any_spec, b_spec, b_spec, b_spec],
            out_specs=any_spec,
            scratch_shapes=[pltpu.VMEM((2, 3, D_MODEL, D_MODEL), F32)]
                           + [pltpu.VMEM((D_MODEL, D_MODEL), BF16)] * 3
                           + [pltpu.VMEM((TILE_SLOTS, ROW_TILE, ROW_W), BF16),
                              pltpu.VMEM((TILE_SLOTS, ROW_TILE, D_MODEL), BF16),
                              pltpu.VMEM((ROW_TILE, D_MODEL), BF16),
                              pltpu.SemaphoreType.DMA((2,)), pltpu.SemaphoreType.DMA((TILE_SLOTS,)),
                              pltpu.SemaphoreType.DMA((TILE_SLOTS,)), pltpu.SemaphoreType.DMA],
        ),
        compiler_params=pltpu.CompilerParams(dimension_semantics=("arbitrary",), vmem_limit_bytes=VMEM_LIMIT),
        name="experts",
    )(tile_start, xs, w_gate, w_up, w_down, b_gate, b_up, b_down)


def _combine_kernel(gdst_ref, gdst_next_ref, lr_ref, y_ref, x1_ref, gt_ref, gf_ref, out_ref, loc_ref, sem):
    i = pl.program_id(0)
    slot = i % 2
    x1 = x1_ref[...]
    sb, l, _ = x1.shape

    @pl.when(i == 0)
    def _():
        _start_all(_group_copies(loc_ref, slot, y_ref, gdst_ref, sem, False))

    _wait_groups(loc_ref, slot, sem)

    prefetch = _group_copies(loc_ref, 1 - slot, y_ref, gdst_next_ref, sem, False)
    lr = lr_ref[...]
    moe = jnp.zeros((sb * l, D_MODEL), F32)
    for c in range(N_CHUNKS):
        rows = loc_ref[slot, c * MASK_ROWS:(c + 1) * MASK_ROWS, :]
        moe = moe + lax.dot_general(_sort_matrix(lr, c), rows, (((0,), (0,)), ((), ())),
                                    preferred_element_type=F32)
        _start_all(prefetch[c * CHUNK_GROUPS:(c + 1) * CHUNK_GROUPS])
    x2 = x1 + gt_ref[...] * moe.reshape(sb, l, D_MODEL)
    out_ref[...] = x2 * lax.rsqrt(jnp.mean(x2 * x2, axis=-1, keepdims=True) + EPS) * gf_ref[...]

    @pl.when(i == pl.num_programs(0) - 1)
    def _():
        _wait_groups(loc_ref, 1 - slot, sem)


def _combine(gdst, lr, y, x1, gt, g_final, sb, l, seq_div, blk_off):
    n3 = x1.shape[0]
    nblk = n3 // sb
    tm = sb * l
    assert tm == TOK_TILE
    return pl.pallas_call(
        _combine_kernel,
        out_shape=jax.ShapeDtypeStruct(x1.shape, F32),
        grid=(nblk,),
        in_specs=[pl.BlockSpec((1, 1, GDST_LANES), lambda i: (i + blk_off, 0, 0), memory_space=pltpu.SMEM),
                  pl.BlockSpec((1, 1, GDST_LANES), lambda i: (jnp.minimum(i + 1, nblk - 1) + blk_off, 0, 0),
                               memory_space=pltpu.SMEM),
                  pl.BlockSpec((TOP_K, tm), lambda i: (0, i + blk_off)),
                  pl.BlockSpec(memory_space=pl.ANY),
                  pl.BlockSpec((sb, l, D_MODEL), lambda i: (i, 0, 0)),
                  _ada_spec(gt, sb, seq_div),
                  pl.BlockSpec((1, D_MODEL), lambda i: (0, 0))],
        out_specs=pl.BlockSpec((sb, l, D_MODEL), lambda i: (i, 0, 0)),
        scratch_shapes=[pltpu.VMEM((2, LOCAL_ROWS, D_MODEL), BF16), pltpu.SemaphoreType.DMA((2,))],
        compiler_params=pltpu.CompilerParams(dimension_semantics=("arbitrary",), vmem_limit_bytes=VMEM_LIMIT),
        name="combine",
    )(gdst, gdst, lr, y, x1, gt[0], g_final)


def kernel(x_prompt, x_sample, c_prompt, c_sample, state_ssm, state_conv, w_ada, b_ada, g_mix, w_in, g_v_a, w_spatial, b_spatial, g_out_a, conv_w, conv_b, dt_bias, a_log, d_skip, g_out_b, w_out, g_ffn, w_router, b_router, w_gate, b_gate, w_up, b_up, w_down, b_down, g_final):
    assert w_ada.shape[0] == 1, "single-layer step"
    p = dict(w_in=w_in[0], g_mix=g_mix[0], g_v_a=g_v_a[0], w_spatial=w_spatial[0], b_spatial=b_spatial[0],
             g_out_a=g_out_a[0], conv_w=conv_w[0], conv_b=conv_b[0], dt_bias=dt_bias[0], a_log=a_log[0],
             d_skip=d_skip[0], g_out_b=g_out_b[0])
    bp, lp, _ = x_prompt.shape
    bs, ls, _ = x_sample.shape
    tp, ts = bp * lp, bs * ls

    ada = _ada(jnp.concatenate([c_sample, c_prompt], axis=0), w_ada[0], b_ada[0][None, :])
    ada_s = [(ada, j, 0) for j in range(6)]
    ada_p = [(ada, j, bs) for j in range(6)]

    fw = _front_weights(p)
    mixed_p, conv_p, ssm_p = _prompt_mixer(x_prompt, ada_p[0], ada_p[1], fw, p)
    mixed_s, v_s, conv_s, ssm_s = _sample_mixer(x_sample, ada_s[0], ada_s[1], state_ssm[0], state_conv[0], fw, p)

    w_out_b = w_out[0].astype(BF16)
    g_ffn2 = g_ffn[0][None, :]
    wr_t = w_router[0].T
    wr_hi = wr_t.astype(BF16)
    wr_both = jnp.concatenate([wr_hi, (wr_t - wr_hi.astype(F32)).astype(BF16)], axis=0)
    br = b_router[0][:, None]
    tps = lp // TOK_TILE
    sbs = TOK_TILE // ls
    xp3 = x_prompt.reshape(tp // POST_TILE, POST_TILE, D_MODEL)
    x1_p, h2p_p, ids_p = _post(mixed_p, xp3, ada_p[2], ada_p[4], ada_p[3], w_out_b, g_ffn2, wr_both, br,
                               1, POST_TILE, lp // POST_TILE)
    x1_p = x1_p.reshape(bp * tps, TOK_TILE, D_MODEL)
    x1_s, h2p_s, ids_s = _post(mixed_s, x_sample, ada_s[2], ada_s[4], ada_s[3], w_out_b, g_ffn2, wr_both, br,
                               POST_TILE // ls, ls, 1)

    n_blocks = (tp + ts) // TOK_TILE
    max_groups = (tp + ts) * TOP_K // ROW_GROUP + n_blocks * N_EXPERTS + N_EXPERTS * (TILE_GROUPS - 1)
    n_tiles = -(-max_groups // TILE_GROUPS)
    n_tiles_all = n_tiles + 2 * LOCAL_GROUPS // TILE_GROUPS
    lr, gdst, meta = _plan(jnp.concatenate([ids_p, ids_s], axis=1), n_tiles * TILE_GROUPS)

    xs = _dispatch(gdst, meta, lr, h2p_p, h2p_s, n_tiles_all)
    y = _experts(meta[0], xs, w_gate[0], b_gate[0], w_up[0], b_up[0], w_down[0], b_down[0])

    gf = g_final[None, :]
    y_p = _combine(gdst, lr, y, x1_p, ada_p[5], gf, 1, TOK_TILE, tps, 0).reshape(bp, lp, D_MODEL)
    y_s = _combine(gdst, lr, y, x1_s, ada_s[5], gf, sbs, ls, 1, tp // TOK_TILE)

    return (y_p, y_s, ssm_p[None], conv_p[None], ssm_s[None], conv_s[None], v_s.reshape(1, bs, ls, A_WIDTH))
```
